```python
import math
import jax
import jax.numpy as jnp
from jax import lax
import numpy as np

D_MODEL = 1024
BATCH = 2
SEQ = 8192
DEPTH = 1
DEC_BATCH = 128
DEC_SEQ = 8
PAST_LEN = 8192
PAGE_SIZE = 128

N_HEADS = 16
HEAD_DIM = 64
N_KV_HEADS = 4
GQA_GROUP = N_HEADS // N_KV_HEADS
WINDOW = 128
ATTN_BLOCK = WINDOW
Q_DIM = N_HEADS * HEAD_DIM
KV_DIM = N_KV_HEADS * HEAD_DIM
N_BUCKETS = 32
MAX_EXACT = N_BUCKETS // 2
MAX_DISTANCE = 128
D_CONV = D_MODEL
CONV_WIDTH = 31
N_GROUPS = 4
EXPERTS_PER_GROUP = 8
N_EXPERTS = N_GROUPS * EXPERTS_PER_GROUP
TOP_K = 2
D_EXPERT = 256
MOE_BLOCK = 128
EPS = 1e-6
IN_DIM = 2 * D_CONV + Q_DIM + 2 * KV_DIM + 2 * D_MODEL
SPLITS = [D_CONV, 2 * D_CONV, 2 * D_CONV + Q_DIM, 2 * D_CONV + Q_DIM + KV_DIM,
          2 * D_CONV + Q_DIM + 2 * KV_DIM, 2 * D_CONV + Q_DIM + 2 * KV_DIM + D_MODEL]

kernel_name = "hybrid_conformer_swa_hmoe_step"


def rmsnorm(x, g):
    xf = x.astype(jnp.float32)
    y = xf * lax.rsqrt(jnp.mean(xf * xf, axis=-1, keepdims=True) + EPS)
    return (y * g.astype(jnp.float32)).astype(x.dtype)


def layernorm(x, g, b):
    xf = x.astype(jnp.float32)
    xc = xf - jnp.mean(xf, axis=-1, keepdims=True)
    var = jnp.mean(xc * xc, axis=-1, keepdims=True)
    y = xc * lax.rsqrt(var + EPS) * g.astype(jnp.float32) + b.astype(jnp.float32)
    return y.astype(x.dtype)


def t5_bucket(dist):
    n = jnp.maximum(dist, 0)
    nf = jnp.maximum(n, 1).astype(jnp.float32)
    large = MAX_EXACT + (jnp.log(nf / MAX_EXACT) / math.log(MAX_DISTANCE / MAX_EXACT)
                         * (N_BUCKETS - MAX_EXACT)).astype(jnp.int32)
    return jnp.where(n < MAX_EXACT, n, jnp.minimum(large, N_BUCKETS - 1))


def project(xn, w_in, q_norm_g, k_norm_g):
    lead = xn.shape[:-1]
    h = xn @ w_in
    a, b, q, k, v, ga, gb = jnp.split(h, SPLITS, axis=-1)
    glu = a * jax.nn.sigmoid(b)
    q = rmsnorm(q.reshape(lead + (N_KV_HEADS, GQA_GROUP, HEAD_DIM)), q_norm_g)
    k = rmsnorm(k.reshape(lead + (N_KV_HEADS, HEAD_DIM)), k_norm_g)
    v = v.reshape(lead + (N_KV_HEADS, HEAD_DIM))
    return glu, q, k, v, ga, gb


def conv_branch(glu_hist, w_dw, b_dw, conv_ln_g, conv_ln_b, w_conv_out, b_conv_out):
    y = lax.conv_general_dilated(glu_hist, w_dw[:, None, :], window_strides=(1,), padding='VALID',
                                 dimension_numbers=('NWC', 'WIO', 'NWC'),
                                 feature_group_count=D_CONV) + b_dw
    y = jax.nn.silu(layernorm(y, conv_ln_g, conv_ln_b))
    return y @ w_conv_out + b_conv_out


def window_attention(q, k, v, q_pos, k_pos, rel_bias, attn_sinks):
    s = jnp.einsum('...qkgd,...skd->...qskg', q, k).astype(jnp.float32) * (HEAD_DIM ** -0.5)
    dist = q_pos[..., :, None] - k_pos[..., None, :]
    valid = (dist >= 0) & (dist < WINDOW) & (k_pos[..., None, :] >= 0)
    bias = rel_bias.reshape(N_BUCKETS, N_KV_HEADS, GQA_GROUP)[t5_bucket(dist)].astype(jnp.float32)
    s = jnp.where(valid[..., None, None], s + bias, -jnp.inf)
    sink = jnp.broadcast_to(attn_sinks.reshape(N_KV_HEADS, GQA_GROUP).astype(jnp.float32),
                            s.shape[:-3] + (1,) + s.shape[-2:])
    p = jax.nn.softmax(jnp.concatenate([s, sink], axis=-3), axis=-3)[..., :-1, :, :]
    return jnp.einsum('...qskg,...skd->...qkgd', p.astype(v.dtype), v)


def hier_moe(xn, w_grp, b_grp, w_router, b_router, w_gate, w_up, w_down):
    t = xn.shape[0]
    g_logits = (xn @ w_grp).astype(jnp.float32) + b_grp.astype(jnp.float32)
    grp = jnp.argmax(g_logits, axis=-1)
    p_grp = jnp.take_along_axis(jax.nn.softmax(g_logits, axis=-1), grp[:, None], axis=-1)
    e_logits = ((xn @ w_router).astype(jnp.float32) + b_router.astype(jnp.float32)).reshape(
        t, N_GROUPS, EXPERTS_PER_GROUP)
    e_logits = jnp.take_along_axis(e_logits, grp[:, None, None], axis=1)[:, 0]
    top_p, top_i = lax.top_k(jax.nn.softmax(e_logits, axis=-1), TOP_K)
    weights = p_grp * top_p / jnp.sum(top_p, axis=-1, keepdims=True)
    eid = grp[:, None].astype(jnp.int32) * EXPERTS_PER_GROUP + top_i.astype(jnp.int32)
    n_pairs = t * TOP_K
    e_flat = eid.reshape(n_pairs)
    order = jnp.argsort(e_flat).astype(jnp.int32)
    e_sorted = e_flat[order]
    counts = jnp.bincount(e_flat, length=N_EXPERTS).astype(jnp.int32)
    starts = jnp.cumsum(counts) - counts
    pcounts = (counts + MOE_BLOCK - 1) // MOE_BLOCK * MOE_BLOCK
    pends = jnp.cumsum(pcounts)
    pstarts = pends - pcounts
    dest = pstarts[e_sorted] + jnp.arange(n_pairs, dtype=jnp.int32) - starts[e_sorted]
    n_blocks = -(-(n_pairs + N_EXPERTS * (MOE_BLOCK - 1)) // MOE_BLOCK)
    rows = n_blocks * MOE_BLOCK
    tok_buf = jnp.zeros((rows,), jnp.int32).at[dest].set(order // TOP_K)
    w_buf = jnp.zeros((rows,), jnp.float32).at[dest].set(weights.reshape(n_pairs)[order])
    blk_start = jnp.arange(n_blocks, dtype=jnp.int32) * MOE_BLOCK
    blk_expert = jnp.minimum(jnp.sum(blk_start[:, None] >= pends[None, :], axis=1), N_EXPERTS - 1)
    xb = xn[tok_buf].reshape(n_blocks, MOE_BLOCK, D_MODEL)

    def expert_block(args):
        xblk, e = args
        hid = jax.nn.silu(xblk @ w_gate[e]) * (xblk @ w_up[e])
        return hid @ w_down[e]

    yb = lax.map(expert_block, (xb, blk_expert)).reshape(rows, D_MODEL)
    return jnp.zeros_like(xn).at[tok_buf].add(yb * w_buf[:, None].astype(yb.dtype))


def finish_layer(x, conv_out, attn_o, ga, gb, w_attn_out, w_out, norm_ffn_g,
                 w_grp, b_grp, w_router, b_router, w_gate, w_up, w_down):
    bsz, t = x.shape[0], x.shape[1]
    attn_out = attn_o.reshape(bsz, t, Q_DIM) @ w_attn_out
    merged = jax.nn.sigmoid(ga) * conv_out + jax.nn.sigmoid(gb) * attn_out
    h = x + merged @ w_out
    hn = rmsnorm(h, norm_ffn_g).reshape(bsz * t, D_MODEL)
    return h + hier_moe(hn, w_grp, b_grp, w_router, b_router, w_gate, w_up, w_down).reshape(bsz, t, D_MODEL)


def setup_inputs(seed: int = 0) -> dict:
    key = jax.random.key(seed)
    ks = jax.random.split(key, 28)

    def nrm(k, shape, scale):
        return jax.random.normal(k, shape, jnp.float32) * scale

    return {
        "x_prompt": nrm(ks[0], (BATCH, SEQ, D_MODEL), 1.0),
        "x_sample": nrm(ks[1], (DEC_BATCH, DEC_SEQ, D_MODEL), 1.0),
        "state_conv": nrm(ks[2], (DEC_BATCH, CONV_WIDTH - 1, D_CONV), 0.5),
        "cache_k": nrm(ks[3], (DEC_BATCH, WINDOW, N_KV_HEADS, HEAD_DIM), 1.0),
        "cache_v": nrm(ks[4], (DEC_BATCH, WINDOW, N_KV_HEADS, HEAD_DIM), 1.0),
        "norm_attn_g": 1.0 + nrm(ks[5], (D_MODEL,), 0.05),
        "w_in": nrm(ks[6], (D_MODEL, IN_DIM), D_MODEL ** -0.5),
        "q_norm_g": 1.0 + nrm(ks[7], (HEAD_DIM,), 0.05),
        "k_norm_g": 1.0 + nrm(ks[8], (HEAD_DIM,), 0.05),
        "rel_bias": nrm(ks[9], (N_BUCKETS, N_HEADS), 0.5),
        "attn_sinks": nrm(ks[10], (N_HEADS,), 0.5),
        "w_dw": nrm(ks[11], (CONV_WIDTH, D_CONV), CONV_WIDTH ** -0.5),
        "b_dw": nrm(ks[12], (D_CONV,), 0.02),
        "conv_ln_g": 1.0 + nrm(ks[13], (D_CONV,), 0.05),
        "conv_ln_b": nrm(ks[14], (D_CONV,), 0.02),
        "w_conv_out": nrm(ks[15], (D_CONV, D_MODEL), D_CONV ** -0.5),
        "b_conv_out": nrm(ks[16], (D_MODEL,), 0.02),
        "w_attn_out": nrm(ks[17], (Q_DIM, D_MODEL), Q_DIM ** -0.5),
        "w_out": nrm(ks[18], (D_MODEL, D_MODEL), D_MODEL ** -0.5),
        "norm_ffn_g": 1.0 + nrm(ks[19], (D_MODEL,), 0.05),
        "w_grp": nrm(ks[20], (D_MODEL, N_GROUPS), D_MODEL ** -0.5),
        "b_grp": nrm(ks[21], (N_GROUPS,), 0.01),
        "w_router": nrm(ks[22], (D_MODEL, N_EXPERTS), D_MODEL ** -0.5),
        "b_router": nrm(ks[23], (N_EXPERTS,), 0.01),
        "w_gate": nrm(ks[24], (N_EXPERTS, D_MODEL, D_EXPERT), D_MODEL ** -0.5),
        "w_up": nrm(ks[25], (N_EXPERTS, D_MODEL, D_EXPERT), D_MODEL ** -0.5),
        "w_down": nrm(ks[26], (N_EXPERTS, D_EXPERT, D_MODEL), D_EXPERT ** -0.5),
    }


def reference(x_prompt, x_sample, state_conv, cache_k, cache_v, norm_attn_g, w_in, q_norm_g, k_norm_g,
              rel_bias, attn_sinks, w_dw, b_dw, conv_ln_g, conv_ln_b, w_conv_out, b_conv_out,
              w_attn_out, w_out, norm_ffn_g, w_grp, b_grp, w_router, b_router, w_gate, w_up, w_down):
    y_prompt = x_prompt
    y_sample = x_sample
    for _layer in range(DEPTH):
        bsz, t = y_prompt.shape[0], y_prompt.shape[1]
        xn = rmsnorm(y_prompt, norm_attn_g)
        glu, q, k, v, ga, gb = project(xn, w_in, q_norm_g, k_norm_g)
        glu_hist = jnp.concatenate([jnp.zeros((bsz, CONV_WIDTH - 1, D_CONV), glu.dtype), glu], axis=1)
        conv_out = conv_branch(glu_hist, w_dw, b_dw, conv_ln_g, conv_ln_b, w_conv_out, b_conv_out)
        nblk = t // ATTN_BLOCK
        qb = q.reshape(bsz, nblk, ATTN_BLOCK, N_KV_HEADS, GQA_GROUP, HEAD_DIM)
        kb = k.reshape(bsz, nblk, ATTN_BLOCK, N_KV_HEADS, HEAD_DIM)
        vb = v.reshape(bsz, nblk, ATTN_BLOCK, N_KV_HEADS, HEAD_DIM)
        k_band = jnp.concatenate([jnp.concatenate([jnp.zeros_like(kb[:, :1]), kb[:, :-1]], axis=1), kb], axis=2)
        v_band = jnp.concatenate([jnp.concatenate([jnp.zeros_like(vb[:, :1]), vb[:, :-1]], axis=1), vb], axis=2)
        q_pos = jnp.arange(t, dtype=jnp.int32).reshape(nblk, ATTN_BLOCK)
        k_pos = jnp.concatenate([q_pos - ATTN_BLOCK, q_pos], axis=-1)
        o = window_attention(qb, k_band, v_band, q_pos, k_pos, rel_bias, attn_sinks)
        o = o.reshape(bsz, t, N_KV_HEADS, GQA_GROUP, HEAD_DIM)
        y_prompt = finish_layer(y_prompt, conv_out, o, ga, gb, w_attn_out, w_out, norm_ffn_g,
                                w_grp, b_grp, w_router, b_router, w_gate, w_up, w_down)
        state_conv_prompt = glu_hist[:, -(CONV_WIDTH - 1):]
        cache_k_prompt = k[:, -WINDOW:]
        cache_v_prompt = v[:, -WINDOW:]

        xn = rmsnorm(y_sample, norm_attn_g)
        glu, q, k, v, ga, gb = project(xn, w_in, q_norm_g, k_norm_g)
        glu_hist = jnp.concatenate([state_conv.astype(glu.dtype), glu], axis=1)
        conv_out = conv_branch(glu_hist, w_dw, b_dw, conv_ln_g, conv_ln_b, w_conv_out, b_conv_out)
        k_all = jnp.concatenate([cache_k.astype(k.dtype), k], axis=1)
        v_all = jnp.concatenate([cache_v.astype(v.dtype), v], axis=1)
        q_pos = PAST_LEN + jnp.arange(y_sample.shape[1], dtype=jnp.int32)
        k_pos = jnp.concatenate([PAST_LEN - WINDOW + jnp.arange(WINDOW, dtype=jnp.int32), q_pos])
        o = window_attention(q, k_all, v_all, q_pos, k_pos, rel_bias, attn_sinks)
        y_sample = finish_layer(y_sample, conv_out, o, ga, gb, w_attn_out, w_out, norm_ffn_g,
                                w_grp, b_grp, w_router, b_router, w_gate, w_up, w_down)
        state_conv_sample = glu_hist[:, -(CONV_WIDTH - 1):]
        cache_k_sample = k_all[:, -WINDOW:]
        cache_v_sample = v_all[:, -WINDOW:]
    return (y_prompt, y_sample, state_conv_prompt, cache_k_prompt, cache_v_prompt,
            state_conv_sample, cache_k_sample, cache_v_sample)
```

```python
import functools
import math

import numpy as np
import jax
import jax.numpy as jnp
from jax import lax
from jax.experimental import pallas as pl
from jax.experimental.pallas import tpu as pltpu

D_MODEL = 1024
N_HEADS = 16
HEAD_DIM = 64
N_KV_HEADS = 4
WINDOW = 128
Q_DIM = N_HEADS * HEAD_DIM
KV_DIM = N_KV_HEADS * HEAD_DIM
N_BUCKETS = 32
MAX_EXACT = N_BUCKETS // 2
MAX_DISTANCE = 128
D_CONV = D_MODEL
CONV_WIDTH = 31
N_GROUPS = 4
EXPERTS_PER_GROUP = 8
N_EXPERTS = N_GROUPS * EXPERTS_PER_GROUP
TOP_K = 2
D_EXPERT = 256
EPS = 1e-6

LANES = 128
SUBLANES = 8
N_PAIRS = N_HEADS // 2
MOE_ROWS = 128
MASK_VALUE = -1e30
VMEM_LIMIT = 48 * 1024 * 1024

_F32 = jnp.float32
_BF16 = jnp.bfloat16


def _dot(a, b):
    return jnp.dot(a, b, preferred_element_type=_F32)


def _split_bf16(x):
    hi = x.astype(_BF16)
    lo = (x - hi.astype(_F32)).astype(_BF16)
    return hi, lo


def _head_rms_scale(z, gsum, gexp):
    hi, lo = _split_bf16(z * z)
    ssum = _dot(hi, gsum) + _dot(lo, gsum)
    r = lax.rsqrt(ssum * (1.0 / HEAD_DIM) + EPS)
    rhi, rlo = _split_bf16(r)
    return _dot(rhi, gexp) + _dot(rlo, gexp)


def _inproj_kernel(x_ref, g_ref, w_ref, qg_ref, kg_ref, gsum_ref, gexp_ref,
                   glu_ref, q_ref, k_ref, v_ref, ga_ref, gb_ref):
    x = x_ref[...]
    xn = x * lax.rsqrt(jnp.mean(x * x, axis=-1, keepdims=True) + EPS) * g_ref[...]
    xb = xn.astype(_BF16)

    def seg(lo, width):
        return _dot(xb, w_ref[:, lo:lo + width])

    a = seg(0, D_CONV)
    b = seg(D_CONV, D_CONV)
    glu_ref[...] = a * jax.nn.sigmoid(b)
    off = 2 * D_CONV
    q = seg(off, Q_DIM)
    q_ref[...] = (q * _head_rms_scale(q, gsum_ref[...], gexp_ref[...]) * qg_ref[...]).astype(q_ref.dtype)
    off += Q_DIM
    k = seg(off, KV_DIM)
    k_ref[...] = k * _head_rms_scale(k, gsum_ref[:KV_DIM, :], gexp_ref[:, :KV_DIM]) * kg_ref[...]
    off += KV_DIM
    v_ref[...] = seg(off, KV_DIM)
    off += KV_DIM
    ga_ref[...] = jax.nn.sigmoid(seg(off, D_MODEL))
    off += D_MODEL
    gb_ref[...] = jax.nn.sigmoid(seg(off, D_MODEL))


def _inproj(x, g, w_in_b, qg, kg, gsum, gexp, q_dtype, tm):
    n = x.shape[0]
    in_dim = w_in_b.shape[1]
    row = lambda w: pl.BlockSpec((tm, w), lambda i: (i, 0))
    full = lambda a: pl.BlockSpec(a.shape, lambda i: (0,) * a.ndim)
    return pl.pallas_call(
        _inproj_kernel,
        grid=(n // tm,),
        in_specs=[row(D_MODEL), full(g), full(w_in_b), full(qg), full(kg), full(gsum), full(gexp)],
        out_specs=[row(D_CONV), row(Q_DIM), row(KV_DIM), row(KV_DIM), row(D_MODEL), row(D_MODEL)],
        out_shape=[jax.ShapeDtypeStruct((n, D_CONV), _F32),
                   jax.ShapeDtypeStruct((n, Q_DIM), q_dtype),
                   jax.ShapeDtypeStruct((n, KV_DIM), _F32),
                   jax.ShapeDtypeStruct((n, KV_DIM), _F32),
                   jax.ShapeDtypeStruct((n, D_MODEL), _F32),
                   jax.ShapeDtypeStruct((n, D_MODEL), _F32)],
        compiler_params=pltpu.CompilerParams(dimension_semantics=("arbitrary",),
                                             vmem_limit_bytes=VMEM_LIMIT),
        name="inproj",
    )(x, g, w_in_b, qg, kg, gsum, gexp)


def _ln_swish_project(y, lng_ref, lnb_ref, wo_ref, bo_ref):
    mu = jnp.mean(y, axis=-1, keepdims=True)
    yc = y - mu
    var = jnp.mean(yc * yc, axis=-1, keepdims=True)
    z = yc * lax.rsqrt(var + EPS) * lng_ref[...] + lnb_ref[...]
    z = z * jax.nn.sigmoid(z)
    return _dot(z.astype(_BF16), wo_ref[...]) + bo_ref[...]


HALO = 32
CONV_ROWS = 64


def _conv_prompt_kernel(glu_ref, wdw_ref, bdw_ref, lng_ref, lnb_ref, wo_ref, bo_ref,
                        out_ref, ext_ref, y_ref, *, tm):
    i = pl.program_id(1)

    @pl.when(i == 0)
    def _():
        ext_ref[0:HALO, :] = jnp.zeros((HALO, D_CONV), _F32)

    @pl.when(i > 0)
    def _():
        ext_ref[0:HALO, :] = ext_ref[tm:tm + HALO, :]

    ext_ref[HALO:HALO + tm, :] = glu_ref[...]

    base = HALO - (CONV_WIDTH - 1)

    def lane_chunk(c, carry):
        cs = pl.ds(pl.multiple_of(c * LANES, LANES), LANES)
        for rc in range(tm // CONV_ROWS):
            r0 = rc * CONV_ROWS
            acc = jnp.zeros((CONV_ROWS, LANES), _F32)
            for r in range(SUBLANES):
                taps = [j for j in range(CONV_WIDTH) if (base + j) % SUBLANES == r]
                a_lo = (base + taps[0]) // SUBLANES
                a_hi = (base + taps[-1]) // SUBLANES
                span = CONV_ROWS + (a_hi - a_lo) * SUBLANES
                slab = ext_ref[pl.ds(r0 + a_lo * SUBLANES + r, span), cs]
                for j in taps:
                    a = (base + j) // SUBLANES - a_lo
                    acc = acc + slab[a * SUBLANES:a * SUBLANES + CONV_ROWS, :] * wdw_ref[pl.ds(j, 1), cs]
            y_ref[pl.ds(r0, CONV_ROWS), cs] = acc + bdw_ref[:, cs]
        return carry

    lax.fori_loop(0, D_CONV // LANES, lane_chunk, 0)
    out_ref[...] = _ln_swish_project(y_ref[...], lng_ref, lnb_ref, wo_ref, bo_ref)


def _conv_prompt(glu, bsz, t, w_dw, b_dw, lng, lnb, wo_b, bo, tm):
    nt = t // tm
    full = lambda a: pl.BlockSpec(a.shape, lambda b, i: (0,) * a.ndim)
    row = pl.BlockSpec((tm, D_CONV), lambda b, i: (b * nt + i, 0))
    return pl.pallas_call(
        functools.partial(_conv_prompt_kernel, tm=tm),
        grid=(bsz, nt),
        in_specs=[row, full(w_dw), full(b_dw), full(lng), full(lnb), full(wo_b), full(bo)],
        out_specs=pl.BlockSpec((tm, D_MODEL), lambda b, i: (b * nt + i, 0)),
        out_shape=jax.ShapeDtypeStruct((bsz * t, D_MODEL), _F32),
        scratch_shapes=[pltpu.VMEM((tm + HALO, D_CONV), _F32), pltpu.VMEM((tm, D_CONV), _F32)],
        compiler_params=pltpu.CompilerParams(dimension_semantics=("arbitrary", "arbitrary"),
                                             vmem_limit_bytes=VMEM_LIMIT),
        name="conv_prompt",
    )(glu, w_dw, b_dw, lng, lnb, wo_b, bo)


def _conv_sample_kernel(hist_ref, wdw_ref, bdw_ref, lng_ref, lnb_ref, wo_ref, bo_ref, out_ref, *, steps):
    for t in range(steps):
        acc = hist_ref[t] * wdw_ref[0:1, :]
        for j in range(1, CONV_WIDTH):
            acc = acc + hist_ref[t + j] * wdw_ref[j:j + 1, :]
        out_ref[t] = _ln_swish_project(acc + bdw_ref[...], lng_ref, lnb_ref, wo_ref, bo_ref)


def _conv_sample(hist, w_dw, b_dw, lng, lnb, wo_b, bo, sb):
    rows, nseq, _ = hist.shape
    steps = rows - (CONV_WIDTH - 1)
    full = lambda a: pl.BlockSpec(a.shape, lambda i: (0,) * a.ndim)
    return pl.pallas_call(
        functools.partial(_conv_sample_kernel, steps=steps),
        grid=(nseq // sb,),
        in_specs=[pl.BlockSpec((rows, sb, D_CONV), lambda i: (0, i, 0)),
                  full(w_dw), full(b_dw), full(lng), full(lnb), full(wo_b), full(bo)],
        out_specs=pl.BlockSpec((steps, sb, D_MODEL), lambda i: (0, i, 0)),
        out_shape=jax.ShapeDtypeStruct((steps, nseq, D_MODEL), _F32),
        compiler_params=pltpu.CompilerParams(dimension_semantics=("arbitrary",),
                                             vmem_limit_bytes=VMEM_LIMIT),
        name="conv_sample",
    )(hist, w_dw, b_dw, lng, lnb, wo_b, bo)


def _bucket_maps():
    i = np.arange(WINDOW)[:, None]
    j = np.arange(WINDOW)[None, :]

    def bucket(dist):
        n = np.maximum(dist, 0)
        nf = np.maximum(n, 1).astype(np.float32)
        large = MAX_EXACT + (np.log(nf / np.float32(MAX_EXACT)) / np.float32(math.log(MAX_DISTANCE / MAX_EXACT))
                             * np.float32(N_BUCKETS - MAX_EXACT)).astype(np.int32)
        return np.where(n < MAX_EXACT, n, np.minimum(large, N_BUCKETS - 1)).astype(np.int32)

    d_prev = i - j + WINDOW
    d_own = i - j
    bp = np.where((d_prev >= 0) & (d_prev < WINDOW), bucket(d_prev), -1)
    bo = np.where((d_own >= 0) & (d_own < WINDOW), bucket(d_own), -1)
    return np.concatenate([bp, bo], axis=1).astype(np.int32)


def _bias_table_kernel(rb_ref, bm_ref, tp_ref, to_ref):
    p = pl.program_id(0)
    bm = bm_ref[...]
    for half in range(2):
        h = 2 * p + half
        t = jnp.full(bm.shape, MASK_VALUE, _F32)
        for b in range(N_BUCKETS):
            t = jnp.where(bm == b, rb_ref[b, h], t)
        tp_ref[0, :, half * WINDOW:(half + 1) * WINDOW] = t[:, :WINDOW]
        to_ref[0, :, half * WINDOW:(half + 1) * WINDOW] = t[:, WINDOW:]


def _bias_tables(rel_bias):
    bm = jnp.asarray(_bucket_maps())
    tbl = pl.BlockSpec((1, WINDOW, 2 * WINDOW), lambda p: (p, 0, 0))
    shape = jax.ShapeDtypeStruct((N_PAIRS, WINDOW, 2 * WINDOW), _F32)
    return pl.pallas_call(
        _bias_table_kernel,
        grid=(N_PAIRS,),
        in_specs=[pl.BlockSpec(memory_space=pltpu.SMEM), pl.BlockSpec(bm.shape, lambda p: (0, 0))],
        out_specs=[tbl, tbl],
        out_shape=[shape, shape],
        name="bias_tables",
    )(rel_bias, bm)


def _block_diag_pairs(slab):
    low = lax.broadcasted_iota(jnp.int32, slab.shape, 1) < HEAD_DIM
    swapped = pltpu.roll(slab, HEAD_DIM, axis=1)
    zero = jnp.zeros_like(slab)
    first = jnp.concatenate([jnp.where(low, slab, zero), jnp.where(low, zero, swapped)], axis=0)
    second = jnp.concatenate([jnp.where(low, swapped, zero), jnp.where(low, zero, slab)], axis=0)
    return first.astype(_BF16), second.astype(_BF16)


def _attend(q, k_prev, k_own, v_prev, v_own, tp_ref, to_ref, sink_ref, prev_shift, store):
    tq = q.shape[0]
    low = lax.broadcasted_iota(jnp.int32, (tq, LANES), 1) < HEAD_DIM
    contract_last = (((1,), (1,)), ((), ()))
    for slab in range(KV_DIM // LANES):
        cols = slice(slab * LANES, (slab + 1) * LANES)
        kp_pair = _block_diag_pairs(k_prev[:, cols])
        ko_pair = _block_diag_pairs(k_own[:, cols])
        vp_pair = _block_diag_pairs(v_prev[:, cols])
        vo_pair = _block_diag_pairs(v_own[:, cols])
        for sub in range(2):
            kvh = 2 * slab + sub
            for pp in range(2):
                pair = 2 * kvh + pp
                qp = q[:, pair * LANES:(pair + 1) * LANES]
                sp = lax.dot_general(qp, kp_pair[sub], contract_last, preferred_element_type=_F32)
                so = lax.dot_general(qp, ko_pair[sub], contract_last, preferred_element_type=_F32)
                sp = sp + tp_ref[pair, 0:tq, :] + prev_shift
                so = so + to_ref[pair, 0:tq, :]
                p_prev, p_own, inv = [], [], []
                for half in range(2):
                    hs = slice(half * WINDOW, (half + 1) * WINDOW)
                    sink = sink_ref[2 * pair + half]
                    sph, soh = sp[:, hs], so[:, hs]
                    m = jnp.maximum(jnp.maximum(jnp.max(sph, axis=-1, keepdims=True),
                                                jnp.max(soh, axis=-1, keepdims=True)), sink)
                    eph = jnp.exp(sph - m)
                    eoh = jnp.exp(soh - m)
                    den = (jnp.sum(eph, axis=-1, keepdims=True) + jnp.sum(eoh, axis=-1, keepdims=True)
                           + jnp.exp(sink - m))
                    p_prev.append(eph.astype(_BF16))
                    p_own.append(eoh.astype(_BF16))
                    inv.append(1.0 / den)
                o = (_dot(jnp.concatenate(p_prev, axis=1), vp_pair[sub])
                     + _dot(jnp.concatenate(p_own, axis=1), vo_pair[sub]))
                store(pair, o * jnp.where(low, inv[0], inv[1]))


def _attn_prompt_kernel(sink_ref, q_ref, kp_ref, ko_ref, vp_ref, vo_ref, tp_ref, to_ref, o_ref):
    prev_shift = jnp.where(pl.program_id(1) == 0, MASK_VALUE, 0.0).astype(_F32)

    def store(pair, o):
        o_ref[:, pair * LANES:(pair + 1) * LANES] = o.astype(o_ref.dtype)

    _attend(q_ref[...], kp_ref[...], ko_ref[...], vp_ref[...], vo_ref[...],
            tp_ref, to_ref, sink_ref, prev_shift, store)


def _attn_prompt(q, k, v, tbl_p, tbl_o, sinks, bsz, t):
    nb = t // WINDOW
    own = lambda w: pl.BlockSpec((WINDOW, w), lambda b, i: (b * nb + i, 0))
    prev = lambda w: pl.BlockSpec((WINDOW, w), lambda b, i: (b * nb + jnp.maximum(i - 1, 0), 0))
    tbl = pl.BlockSpec(tbl_p.shape, lambda b, i: (0, 0, 0))
    return pl.pallas_call(
        _attn_prompt_kernel,
        grid=(bsz, nb),
        in_specs=[pl.BlockSpec(memory_space=pltpu.SMEM), own(Q_DIM), prev(KV_DIM), own(KV_DIM),
                  prev(KV_DIM), own(KV_DIM), tbl, tbl],
        out_specs=own(Q_DIM),
        out_shape=jax.ShapeDtypeStruct((bsz * t, Q_DIM), _BF16),
        compiler_params=pltpu.CompilerParams(dimension_semantics=("arbitrary", "arbitrary"),
                                             vmem_limit_bytes=VMEM_LIMIT),
        name="attn_prompt",
    )(sinks, q, k, k, v, v, tbl_p, tbl_o)


def _attn_sample_kernel(sink_ref, q_ref, kn_ref, vn_ref, ck_ref, cv_ref, tp_ref, to_ref, o_ref, *, sb, steps):
    pad = jnp.zeros((WINDOW - steps, KV_DIM), _F32)

    def one_sequence(s, carry):
        def store(pair, o):
            o_ref[s, :, pair * LANES:(pair + 1) * LANES] = o

        k_own = jnp.concatenate([kn_ref[s], pad], axis=0)
        v_own = jnp.concatenate([vn_ref[s], pad], axis=0)
        _attend(q_ref[s].astype(_BF16), ck_ref[s], k_own, cv_ref[s], v_own,
                tp_ref, to_ref, sink_ref, jnp.float32(0.0), store)
        return carry

    lax.fori_loop(0, sb, one_sequence, 0)


def _attn_sample(q, k_new, v_new, cache_k, cache_v, tbl_p, tbl_o, sinks, sb):
    nseq, steps, _ = q.shape
    seq = lambda r, w: pl.BlockSpec((sb, r, w), lambda i: (i, 0, 0))
    tbl = pl.BlockSpec(tbl_p.shape, lambda i: (0, 0, 0))
    return pl.pallas_call(
        functools.partial(_attn_sample_kernel, sb=sb, steps=steps),
        grid=(nseq // sb,),
        in_specs=[pl.BlockSpec(memory_space=pltpu.SMEM), seq(steps, Q_DIM), seq(steps, KV_DIM), seq(steps, KV_DIM),
                  seq(WINDOW, KV_DIM), seq(WINDOW, KV_DIM), tbl, tbl],
        out_specs=seq(steps, Q_DIM),
        out_shape=jax.ShapeDtypeStruct((nseq, steps, Q_DIM), _F32),
        compiler_params=pltpu.CompilerParams(dimension_semantics=("arbitrary",),
                                             vmem_limit_bytes=VMEM_LIMIT),
        name="attn_sample",
    )(sinks, q, k_new, v_new, cache_k, cache_v, tbl_p, tbl_o)


def _lane_min_index(mask, lane):
    return jnp.min(jnp.where(mask, lane, LANES), axis=-1, keepdims=True)


def _finish_kernel(x_ref, conv_ref, o_ref, ga_ref, gb_ref, wa_ref, wo_ref, ng_ref, wr_hi_ref, wr_lo_ref, br_ref,
                   h_ref, hn_ref, route_ref):
    attn_out = _dot(o_ref[...].astype(_BF16), wa_ref[...])
    merged = ga_ref[...] * conv_ref[...] + gb_ref[...] * attn_out
    h = x_ref[...] + _dot(merged.astype(_BF16), wo_ref[...])
    h_ref[...] = h
    hn = h * lax.rsqrt(jnp.mean(h * h, axis=-1, keepdims=True) + EPS) * ng_ref[...]
    hn_ref[...] = hn

    hi, lo = _split_bf16(hn)
    logits = _dot(hi, wr_hi_ref[...]) + _dot(lo, wr_hi_ref[...]) + _dot(hi, wr_lo_ref[...]) + br_ref[...]
    lane = lax.broadcasted_iota(jnp.int32, logits.shape, 1)
    gmask = lane < N_GROUPS
    gl = jnp.where(gmask, logits, MASK_VALUE)
    gmax = jnp.max(gl, axis=-1, keepdims=True)
    grp = _lane_min_index(gmask & (gl == gmax), lane)
    p_grp = 1.0 / jnp.sum(jnp.where(gmask, jnp.exp(gl - gmax), 0.0), axis=-1, keepdims=True)
    e_lo = N_GROUPS + grp * EXPERTS_PER_GROUP
    emask = (lane >= e_lo) & (lane < e_lo + EXPERTS_PER_GROUP)
    el = jnp.where(emask, logits, MASK_VALUE)
    ex = jnp.where(emask, jnp.exp(el - jnp.max(el, axis=-1, keepdims=True)), 0.0)
    prob = jnp.where(emask, ex / jnp.sum(ex, axis=-1, keepdims=True), -1.0)
    p1 = jnp.max(prob, axis=-1, keepdims=True)
    i1 = _lane_min_index(prob == p1, lane)
    rest = jnp.where(lane == i1, -1.0, prob)
    p2 = jnp.max(rest, axis=-1, keepdims=True)
    i2 = _lane_min_index(rest == p2, lane)
    w1 = p_grp * p1 / (p1 + p2)
    w2 = p_grp * p2 / (p1 + p2)
    e1 = (i1 - N_GROUPS).astype(_F32)
    e2 = (i2 - N_GROUPS).astype(_F32)
    route_ref[...] = jnp.where(lane == 0, e1, jnp.where(lane == 1, e2, jnp.where(lane == 2, w1,
                               jnp.where(lane == 3, w2, 0.0))))


def _finish(x, conv_out, o, ga, gb, wa_b, wo_b, ng, wr_hi, wr_lo, br, tm):
    n = x.shape[0]
    row = lambda w: pl.BlockSpec((tm, w), lambda i: (i, 0))
    full = lambda a: pl.BlockSpec(a.shape, lambda i: (0,) * a.ndim)
    return pl.pallas_call(
        _finish_kernel,
        grid=(n // tm,),
        in_specs=[row(D_MODEL), row(D_MODEL), row(Q_DIM), row(D_MODEL), row(D_MODEL),
                  full(wa_b), full(wo_b), full(ng), full(wr_hi), full(wr_lo), full(br)],
        out_specs=[row(D_MODEL), row(D_MODEL), row(LANES)],
        out_shape=[jax.ShapeDtypeStruct((n, D_MODEL), _F32),
                   jax.ShapeDtypeStruct((n, D_MODEL), _F32),
                   jax.ShapeDtypeStruct((n, LANES), _F32)],
        compiler_params=pltpu.CompilerParams(dimension_semantics=("arbitrary",),
                                             vmem_limit_bytes=VMEM_LIMIT),
        name="finish",
    )(x, conv_out, o, ga, gb, wa_b, wo_b, ng, wr_hi, wr_lo, br)


def _gather_rows(idx_ref, src_hbm, dst_ref, sem, nrows):
    def copy(r):
        return pltpu.make_async_copy(src_hbm.at[pl.ds(idx_ref[r], 1), :], dst_ref.at[pl.ds(r, 1), :], sem)

    def start(r, carry):
        copy(r).start()
        return carry

    def wait(r, carry):
        copy(r).wait()
        return carry

    lax.fori_loop(0, nrows, start, 0)
    lax.fori_loop(0, nrows, wait, 0)


def _expert_kernel(blk_expert_ref, tok_ref, hn_hbm, wg_ref, wu_ref, wd_ref, yb_ref, xg_ref, sem):
    del blk_expert_ref
    _gather_rows(tok_ref.at[0, 0], hn_hbm, xg_ref, sem, MOE_ROWS)
    xb = xg_ref[...].astype(_BF16)
    g = _dot(xb, wg_ref[0])
    u = _dot(xb, wu_ref[0])
    hid = g * jax.nn.sigmoid(g) * u
    yb_ref[...] = _dot(hid.astype(_BF16), wd_ref[0])


def _experts(blk_expert, tok_blocks, hn, wg_b, wu_b, wd_b):
    n_blocks = tok_blocks.shape[0]
    wspec = lambda a: pl.BlockSpec((1,) + a.shape[1:], lambda b, be: (be[b], 0, 0))
    grid_spec = pltpu.PrefetchScalarGridSpec(
        num_scalar_prefetch=1,
        grid=(n_blocks,),
        in_specs=[pl.BlockSpec((1, 1, MOE_ROWS), lambda b, be: (b, 0, 0), memory_space=pltpu.SMEM),
                  pl.BlockSpec(memory_space=pl.ANY), wspec(wg_b), wspec(wu_b), wspec(wd_b)],
        out_specs=pl.BlockSpec((MOE_ROWS, D_MODEL), lambda b, be: (b, 0)),
        scratch_shapes=[pltpu.VMEM((MOE_ROWS, D_MODEL), _F32), pltpu.SemaphoreType.DMA(())],
    )
    return pl.pallas_call(
        _expert_kernel,
        grid_spec=grid_spec,
        out_shape=jax.ShapeDtypeStruct((n_blocks * MOE_ROWS, D_MODEL), _F32),
        compiler_params=pltpu.CompilerParams(dimension_semantics=("arbitrary",),
                                             vmem_limit_bytes=VMEM_LIMIT),
        name="experts",
    )(blk_expert, tok_blocks, hn, wg_b, wu_b, wd_b)


def _combine_kernel(d0_ref, d1_ref, h_ref, route_ref, yb_hbm, y_ref, g0_ref, g1_ref, sem0, sem1, *, tm):
    _gather_rows(d0_ref.at[0, 0], yb_hbm, g0_ref, sem0, tm)
    _gather_rows(d1_ref.at[0, 0], yb_hbm, g1_ref, sem1, tm)
    route = route_ref[...]
    y_ref[...] = h_ref[...] + route[:, 2:3] * g0_ref[...] + route[:, 3:4] * g1_ref[...]


def _combine(dest0, dest1, h, route, yb, tm):
    n = h.shape[0]
    idx = pl.BlockSpec((1, 1, tm), lambda i: (i, 0, 0), memory_space=pltpu.SMEM)
    row = lambda w: pl.BlockSpec((tm, w), lambda i: (i, 0))
    return pl.pallas_call(
        functools.partial(_combine_kernel, tm=tm),
        grid=(n // tm,),
        in_specs=[idx, idx, row(D_MODEL), row(LANES), pl.BlockSpec(memory_space=pl.ANY)],
        out_specs=row(D_MODEL),
        out_shape=jax.ShapeDtypeStruct((n, D_MODEL), _F32),
        scratch_shapes=[pltpu.VMEM((tm, D_MODEL), _F32), pltpu.VMEM((tm, D_MODEL), _F32),
                        pltpu.SemaphoreType.DMA(()), pltpu.SemaphoreType.DMA(())],
        compiler_params=pltpu.CompilerParams(dimension_semantics=("arbitrary",),
                                             vmem_limit_bytes=VMEM_LIMIT),
        name="combine",
    )(dest0.reshape(n // tm, 1, tm), dest1.reshape(n // tm, 1, tm), h, route, yb)


def _moe(h, hn, route, wg_b, wu_b, wd_b, tm):
    n = h.shape[0]
    n_pairs = n * TOP_K
    e_flat = route[:, 0:TOP_K].astype(jnp.int32).reshape(n_pairs)
    order = jnp.argsort(e_flat).astype(jnp.int32)
    e_sorted = e_flat[order]
    counts = jnp.zeros((N_EXPERTS,), jnp.int32).at[e_flat].add(1)
    starts = jnp.cumsum(counts) - counts
    pcounts = (counts + MOE_ROWS - 1) // MOE_ROWS * MOE_ROWS
    pends = jnp.cumsum(pcounts)
    pstarts = pends - pcounts
    dest_sorted = pstarts[e_sorted] + jnp.arange(n_pairs, dtype=jnp.int32) - starts[e_sorted]
    n_blocks = -(-(n_pairs + N_EXPERTS * (MOE_ROWS - 1)) // MOE_ROWS)
    rows = n_blocks * MOE_ROWS
    tok_buf = jnp.zeros((rows,), jnp.int32).at[dest_sorted].set(order // TOP_K)
    dest = jnp.zeros((n_pairs,), jnp.int32).at[order].set(dest_sorted).reshape(n, TOP_K)
    blk_start = jnp.arange(n_blocks, dtype=jnp.int32) * MOE_ROWS
    blk_expert = jnp.minimum(jnp.sum(blk_start[:, None] >= pends[None, :], axis=1), N_EXPERTS - 1).astype(jnp.int32)
    yb = _experts(blk_expert, tok_buf.reshape(n_blocks, 1, MOE_ROWS), hn, wg_b, wu_b, wd_b)
    return _combine(dest[:, 0], dest[:, 1], h, route, yb, tm)


def kernel(x_prompt, x_sample, state_conv, cache_k, cache_v, norm_attn_g, w_in, q_norm_g, k_norm_g, rel_bias, attn_sinks, w_dw, b_dw, conv_ln_g, conv_ln_b, w_conv_out, b_conv_out, w_attn_out, w_out, norm_ffn_g, w_grp, b_grp, w_router, b_router, w_gate, w_up, w_down):
    bsz, t, _ = x_prompt.shape
    nseq, steps, _ = x_sample.shape
    row = lambda a: a.reshape(1, -1).astype(_F32)

    w_in_b = w_in.astype(_BF16)
    wco_b = w_conv_out.astype(_BF16)
    wa_b = w_attn_out.astype(_BF16)
    wo_b = w_out.astype(_BF16)
    wg_b, wu_b, wd_b = w_gate.astype(_BF16), w_up.astype(_BF16), w_down.astype(_BF16)
    qg = row(jnp.tile(q_norm_g, N_HEADS)) * (HEAD_DIM ** -0.5)
    kg = row(jnp.tile(k_norm_g, N_KV_HEADS))
    group_of_lane = np.arange(Q_DIM) // HEAD_DIM
    gsum = jnp.asarray(group_of_lane[:, None] == np.arange(LANES)[None, :], _BF16)
    gexp = jnp.asarray(np.arange(LANES)[:, None] == group_of_lane[None, :], _BF16)
    w_rt = jnp.zeros((D_MODEL, LANES), _F32).at[:, :N_GROUPS].set(w_grp).at[:, N_GROUPS:N_GROUPS + N_EXPERTS].set(w_router)
    wr_hi = w_rt.astype(_BF16)
    wr_lo = (w_rt - wr_hi.astype(_F32)).astype(_BF16)
    b_rt = jnp.zeros((1, LANES), _F32).at[0, :N_GROUPS].set(b_grp).at[0, N_GROUPS:N_GROUPS + N_EXPERTS].set(b_router)
    tbl_p, tbl_o = _bias_tables(rel_bias)
    conv_params = (w_dw, row(b_dw), row(conv_ln_g), row(conv_ln_b), wco_b, row(b_conv_out))

    def finish_and_moe(x2d, conv_out, o, ga, gb, tm):
        h, hn, route = _finish(x2d, conv_out, o, ga, gb, wa_b, wo_b, row(norm_ffn_g), wr_hi, wr_lo, b_rt, tm)
        return _moe(h, hn, route, wg_b, wu_b, wd_b, MOE_ROWS)

    xp = x_prompt.reshape(bsz * t, D_MODEL)
    glu, q, k, v, ga, gb = _inproj(xp, row(norm_attn_g), w_in_b, qg, kg, gsum, gexp, _BF16, 256)
    conv_out = _conv_prompt(glu, bsz, t, *conv_params, 256)
    o = _attn_prompt(q, k, v, tbl_p, tbl_o, attn_sinks, bsz, t)
    y_prompt = finish_and_moe(xp, conv_out, o, ga, gb, 256).reshape(bsz, t, D_MODEL)
    glu3 = glu.reshape(bsz, t, D_CONV)
    state_conv_prompt = glu3[:, t - (CONV_WIDTH - 1):]
    cache_k_prompt = k.reshape(bsz, t, N_KV_HEADS, HEAD_DIM)[:, t - WINDOW:]
    cache_v_prompt = v.reshape(bsz, t, N_KV_HEADS, HEAD_DIM)[:, t - WINDOW:]

    xs = x_sample.reshape(nseq * steps, D_MODEL)
    glu, q, k, v, ga, gb = _inproj(xs, row(norm_attn_g), w_in_b, qg, kg, gsum, gexp, _F32, 256)
    glu3 = glu.reshape(nseq, steps, D_CONV)
    hist = jnp.concatenate([state_conv, glu3], axis=1)
    conv_out = _conv_sample(hist.transpose(1, 0, 2), *conv_params, 64)
    conv_out = conv_out.transpose(1, 0, 2).reshape(nseq * steps, D_MODEL)
    k3 = k.reshape(nseq, steps, KV_DIM)
    v3 = v.reshape(nseq, steps, KV_DIM)
    o = _attn_sample(q.reshape(nseq, steps, Q_DIM), k3, v3, cache_k.reshape(nseq, WINDOW, KV_DIM),
                     cache_v.reshape(nseq, WINDOW, KV_DIM), tbl_p, tbl_o, attn_sinks, 8)
    y_sample = finish_and_moe(xs, conv_out, o.reshape(nseq * steps, Q_DIM), ga, gb, 256).reshape(nseq, steps, D_MODEL)
    state_conv_sample = hist[:, steps:]
    cache_k_sample = jnp.concatenate([cache_k, k3.reshape(nseq, steps, N_KV_HEADS, HEAD_DIM)], axis=1)[:, steps:]
    cache_v_sample = jnp.concatenate([cache_v, v3.reshape(nseq, steps, N_KV_HEADS, HEAD_DIM)], axis=1)[:, steps:]

    return (y_prompt, y_sample, state_conv_prompt, cache_k_prompt, cache_v_prompt,
            state_conv_sample, cache_k_sample, cache_v_sample)
```

```python
import functools
import math

import numpy as np
import jax
import jax.numpy as jnp
from jax import lax
from jax.experimental import pallas as pl
from jax.experimental.pallas import tpu as pltpu
from jax.experimental.pallas import tpu_sc as plsc

D_MODEL = 1024
N_HEADS = 16
HEAD_DIM = 64
N_KV_HEADS = 4
WINDOW = 128
Q_DIM = N_HEADS * HEAD_DIM
KV_DIM = N_KV_HEADS * HEAD_DIM
N_BUCKETS = 32
MAX_EXACT = N_BUCKETS // 2
MAX_DISTANCE = 128
D_CONV = D_MODEL
CONV_WIDTH = 31
N_GROUPS = 4
EXPERTS_PER_GROUP = 8
N_EXPERTS = N_GROUPS * EXPERTS_PER_GROUP
TOP_K = 2
D_EXPERT = 256
EPS = 1e-6

LANES = 128
SUBLANES = 8
N_PAIRS = N_HEADS // 2
MOE_ROWS = 128
MASK_VALUE = -1e30
VMEM_LIMIT = 48 * 1024 * 1024

_F32 = jnp.float32
_BF16 = jnp.bfloat16


def _dot(a, b):
    return jnp.dot(a, b, preferred_element_type=_F32)


def _split_bf16(x):
    hi = x.astype(_BF16)
    lo = (x - hi.astype(_F32)).astype(_BF16)
    return hi, lo


def _head_rms_scale(z, gsum, gexp):
    hi, lo = _split_bf16(z * z)
    ssum = _dot(hi, gsum) + _dot(lo, gsum)
    r = lax.rsqrt(ssum * (1.0 / HEAD_DIM) + EPS)
    rhi, rlo = _split_bf16(r)
    return _dot(rhi, gexp) + _dot(rlo, gexp)


def _inproj_kernel(x_ref, g_ref, w_ref, qg_ref, kg_ref, gsum_ref, gexp_ref,
                   glu_ref, q_ref, k_ref, v_ref, ga_ref, gb_ref):
    x = x_ref[...]
    xn = x * lax.rsqrt(jnp.mean(x * x, axis=-1, keepdims=True) + EPS) * g_ref[...]
    xb = xn.astype(_BF16)

    def seg(lo, width):
        return _dot(xb, w_ref[:, lo:lo + width])

    a = seg(0, D_CONV)
    b = seg(D_CONV, D_CONV)
    glu_ref[...] = a * jax.nn.sigmoid(b)
    off = 2 * D_CONV
    q = seg(off, Q_DIM)
    q_ref[...] = (q * _head_rms_scale(q, gsum_ref[...], gexp_ref[...]) * qg_ref[...]).astype(q_ref.dtype)
    off += Q_DIM
    k = seg(off, KV_DIM)
    k_ref[...] = k * _head_rms_scale(k, gsum_ref[:KV_DIM, :], gexp_ref[:, :KV_DIM]) * kg_ref[...]
    off += KV_DIM
    v_ref[...] = seg(off, KV_DIM)
    off += KV_DIM
    ga_ref[...] = jax.nn.sigmoid(seg(off, D_MODEL))
    off += D_MODEL
    gb_ref[...] = jax.nn.sigmoid(seg(off, D_MODEL))


def _inproj(x, g, w_in_b, qg, kg, gsum, gexp, q_dtype, tm):
    n = x.shape[0]
    in_dim = w_in_b.shape[1]
    row = lambda w: pl.BlockSpec((tm, w), lambda i: (i, 0))
    full = lambda a: pl.BlockSpec(a.shape, lambda i: (0,) * a.ndim)
    return pl.pallas_call(
        _inproj_kernel,
        grid=(n // tm,),
        in_specs=[row(D_MODEL), full(g), full(w_in_b), full(qg), full(kg), full(gsum), full(gexp)],
        out_specs=[row(D_CONV), row(Q_DIM), row(KV_DIM), row(KV_DIM), row(D_MODEL), row(D_MODEL)],
        out_shape=[jax.ShapeDtypeStruct((n, D_CONV), _F32),
                   jax.ShapeDtypeStruct((n, Q_DIM), q_dtype),
                   jax.ShapeDtypeStruct((n, KV_DIM), _F32),
                   jax.ShapeDtypeStruct((n, KV_DIM), _F32),
                   jax.ShapeDtypeStruct((n, D_MODEL), _F32),
                   jax.ShapeDtypeStruct((n, D_MODEL), _F32)],
        compiler_params=pltpu.CompilerParams(dimension_semantics=("arbitrary",),
                                             vmem_limit_bytes=VMEM_LIMIT),
        name="inproj",
    )(x, g, w_in_b, qg, kg, gsum, gexp)


def _ln_swish_project(y, lng_ref, lnb_ref, wo_ref, bo_ref):
    mu = jnp.mean(y, axis=-1, keepdims=True)
    yc = y - mu
    var = jnp.mean(yc * yc, axis=-1, keepdims=True)
    z = yc * lax.rsqrt(var + EPS) * lng_ref[...] + lnb_ref[...]
    z = z * jax.nn.sigmoid(z)
    return _dot(z.astype(_BF16), wo_ref[...]) + bo_ref[...]


HALO = 32
CONV_ROWS = 64


def _conv_prompt_kernel(glu_ref, wdw_ref, bdw_ref, lng_ref, lnb_ref, wo_ref, bo_ref,
                        out_ref, ext_ref, y_ref, *, tm):
    i = pl.program_id(1)

    @pl.when(i == 0)
    def _():
        ext_ref[0:HALO, :] = jnp.zeros((HALO, D_CONV), _F32)

    @pl.when(i > 0)
    def _():
        ext_ref[0:HALO, :] = ext_ref[tm:tm + HALO, :]

    ext_ref[HALO:HALO + tm, :] = glu_ref[...]

    base = HALO - (CONV_WIDTH - 1)

    def lane_chunk(c, carry):
        cs = pl.ds(pl.multiple_of(c * LANES, LANES), LANES)
        for rc in range(tm // CONV_ROWS):
            r0 = rc * CONV_ROWS
            acc = jnp.zeros((CONV_ROWS, LANES), _F32)
            for r in range(SUBLANES):
                taps = [j for j in range(CONV_WIDTH) if (base + j) % SUBLANES == r]
                a_lo = (base + taps[0]) // SUBLANES
                a_hi = (base + taps[-1]) // SUBLANES
                span = CONV_ROWS + (a_hi - a_lo) * SUBLANES
                slab = ext_ref[pl.ds(r0 + a_lo * SUBLANES + r, span), cs]
                for j in taps:
                    a = (base + j) // SUBLANES - a_lo
                    acc = acc + slab[a * SUBLANES:a * SUBLANES + CONV_ROWS, :] * wdw_ref[pl.ds(j, 1), cs]
            y_ref[pl.ds(r0, CONV_ROWS), cs] = acc + bdw_ref[:, cs]
        return carry

    lax.fori_loop(0, D_CONV // LANES, lane_chunk, 0)
    out_ref[...] = _ln_swish_project(y_ref[...], lng_ref, lnb_ref, wo_ref, bo_ref)


def _conv_prompt(glu, bsz, t, w_dw, b_dw, lng, lnb, wo_b, bo, tm):
    nt = t // tm
    full = lambda a: pl.BlockSpec(a.shape, lambda b, i: (0,) * a.ndim)
    row = pl.BlockSpec((tm, D_CONV), lambda b, i: (b * nt + i, 0))
    return pl.pallas_call(
        functools.partial(_conv_prompt_kernel, tm=tm),
        grid=(bsz, nt),
        in_specs=[row, full(w_dw), full(b_dw), full(lng), full(lnb), full(wo_b), full(bo)],
        out_specs=pl.BlockSpec((tm, D_MODEL), lambda b, i: (b * nt + i, 0)),
        out_shape=jax.ShapeDtypeStruct((bsz * t, D_MODEL), _F32),
        scratch_shapes=[pltpu.VMEM((tm + HALO, D_CONV), _F32), pltpu.VMEM((tm, D_CONV), _F32)],
        compiler_params=pltpu.CompilerParams(dimension_semantics=("arbitrary", "arbitrary"),
                                             vmem_limit_bytes=VMEM_LIMIT),
        name="conv_prompt",
    )(glu, w_dw, b_dw, lng, lnb, wo_b, bo)


def _conv_sample_kernel(hist_ref, wdw_ref, bdw_ref, lng_ref, lnb_ref, wo_ref, bo_ref, out_ref, *, steps):
    for t in range(steps):
        acc = hist_ref[t] * wdw_ref[0:1, :]
        for j in range(1, CONV_WIDTH):
            acc = acc + hist_ref[t + j] * wdw_ref[j:j + 1, :]
        out_ref[t] = _ln_swish_project(acc + bdw_ref[...], lng_ref, lnb_ref, wo_ref, bo_ref)


def _conv_sample(hist, w_dw, b_dw, lng, lnb, wo_b, bo, sb):
    rows, nseq, _ = hist.shape
    steps = rows - (CONV_WIDTH - 1)
    full = lambda a: pl.BlockSpec(a.shape, lambda i: (0,) * a.ndim)
    return pl.pallas_call(
        functools.partial(_conv_sample_kernel, steps=steps),
        grid=(nseq // sb,),
        in_specs=[pl.BlockSpec((rows, sb, D_CONV), lambda i: (0, i, 0)),
                  full(w_dw), full(b_dw), full(lng), full(lnb), full(wo_b), full(bo)],
        out_specs=pl.BlockSpec((steps, sb, D_MODEL), lambda i: (0, i, 0)),
        out_shape=jax.ShapeDtypeStruct((steps, nseq, D_MODEL), _F32),
        compiler_params=pltpu.CompilerParams(dimension_semantics=("arbitrary",),
                                             vmem_limit_bytes=VMEM_LIMIT),
        name="conv_sample",
    )(hist, w_dw, b_dw, lng, lnb, wo_b, bo)


def _bucket_maps():
    i = np.arange(WINDOW)[:, None]
    j = np.arange(WINDOW)[None, :]

    def bucket(dist):
        n = np.maximum(dist, 0)
        nf = np.maximum(n, 1).astype(np.float32)
        large = MAX_EXACT + (np.log(nf / np.float32(MAX_EXACT)) / np.float32(math.log(MAX_DISTANCE / MAX_EXACT))
                             * np.float32(N_BUCKETS - MAX_EXACT)).astype(np.int32)
        return np.where(n < MAX_EXACT, n, np.minimum(large, N_BUCKETS - 1)).astype(np.int32)

    d_prev = i - j + WINDOW
    d_own = i - j
    bp = np.where((d_prev >= 0) & (d_prev < WINDOW), bucket(d_prev), -1)
    bo = np.where((d_own >= 0) & (d_own < WINDOW), bucket(d_own), -1)
    return np.concatenate([bp, bo], axis=1).astype(np.int32)


def _bias_table_kernel(rb_ref, bm_ref, tp_ref, to_ref):
    p = pl.program_id(0)
    bm = bm_ref[...]
    for half in range(2):
        h = 2 * p + half
        t = jnp.full(bm.shape, MASK_VALUE, _F32)
        for b in range(N_BUCKETS):
            t = jnp.where(bm == b, rb_ref[b, h], t)
        tp_ref[0, :, half * WINDOW:(half + 1) * WINDOW] = t[:, :WINDOW]
        to_ref[0, :, half * WINDOW:(half + 1) * WINDOW] = t[:, WINDOW:]


def _bias_tables(rel_bias):
    bm = jnp.asarray(_bucket_maps())
    tbl = pl.BlockSpec((1, WINDOW, 2 * WINDOW), lambda p: (p, 0, 0))
    shape = jax.ShapeDtypeStruct((N_PAIRS, WINDOW, 2 * WINDOW), _F32)
    return pl.pallas_call(
        _bias_table_kernel,
        grid=(N_PAIRS,),
        in_specs=[pl.BlockSpec(memory_space=pltpu.SMEM), pl.BlockSpec(bm.shape, lambda p: (0, 0))],
        out_specs=[tbl, tbl],
        out_shape=[shape, shape],
        name="bias_tables",
    )(rel_bias, bm)


def _block_diag_pairs(slab):
    low = lax.broadcasted_iota(jnp.int32, slab.shape, 1) < HEAD_DIM
    swapped = pltpu.roll(slab, HEAD_DIM, axis=1)
    zero = jnp.zeros_like(slab)
    first = jnp.concatenate([jnp.where(low, slab, zero), jnp.where(low, zero, swapped)], axis=0)
    second = jnp.concatenate([jnp.where(low, swapped, zero), jnp.where(low, zero, slab)], axis=0)
    return first.astype(_BF16), second.astype(_BF16)


def _attend(q, k_prev, k_own, v_prev, v_own, tp_ref, to_ref, sink_ref, prev_shift, store):
    tq = q.shape[0]
    low = lax.broadcasted_iota(jnp.int32, (tq, LANES), 1) < HEAD_DIM
    contract_last = (((1,), (1,)), ((), ()))
    for slab in range(KV_DIM // LANES):
        cols = slice(slab * LANES, (slab + 1) * LANES)
        kp_pair = _block_diag_pairs(k_prev[:, cols])
        ko_pair = _block_diag_pairs(k_own[:, cols])
        vp_pair = _block_diag_pairs(v_prev[:, cols])
        vo_pair = _block_diag_pairs(v_own[:, cols])
        for sub in range(2):
            kvh = 2 * slab + sub
            for pp in range(2):
                pair = 2 * kvh + pp
                qp = q[:, pair * LANES:(pair + 1) * LANES]
                sp = lax.dot_general(qp, kp_pair[sub], contract_last, preferred_element_type=_F32)
                so = lax.dot_general(qp, ko_pair[sub], contract_last, preferred_element_type=_F32)
                sp = sp + tp_ref[pair, 0:tq, :] + prev_shift
                so = so + to_ref[pair, 0:tq, :]
                p_prev, p_own, inv = [], [], []
                for half in range(2):
                    hs = slice(half * WINDOW, (half + 1) * WINDOW)
                    sink = sink_ref[2 * pair + half]
                    sph, soh = sp[:, hs], so[:, hs]
                    m = jnp.maximum(jnp.maximum(jnp.max(sph, axis=-1, keepdims=True),
                                                jnp.max(soh, axis=-1, keepdims=True)), sink)
                    eph = jnp.exp(sph - m)
                    eoh = jnp.exp(soh - m)
                    den = (jnp.sum(eph, axis=-1, keepdims=True) + jnp.sum(eoh, axis=-1, keepdims=True)
                           + jnp.exp(sink - m))
                    p_prev.append(eph.astype(_BF16))
                    p_own.append(eoh.astype(_BF16))
                    inv.append(1.0 / den)
                o = (_dot(jnp.concatenate(p_prev, axis=1), vp_pair[sub])
                     + _dot(jnp.concatenate(p_own, axis=1), vo_pair[sub]))
                store(pair, o * jnp.where(low, inv[0], inv[1]))


def _attn_prompt_kernel(sink_ref, q_ref, kp_ref, ko_ref, vp_ref, vo_ref, tp_ref, to_ref, o_ref):
    prev_shift = jnp.where(pl.program_id(1) == 0, MASK_VALUE, 0.0).astype(_F32)

    def store(pair, o):
        o_ref[:, pair * LANES:(pair + 1) * LANES] = o.astype(o_ref.dtype)

    _attend(q_ref[...], kp_ref[...], ko_ref[...], vp_ref[...], vo_ref[...],
            tp_ref, to_ref, sink_ref, prev_shift, store)


def _attn_prompt(q, k, v, tbl_p, tbl_o, sinks, bsz, t):
    nb = t // WINDOW
    own = lambda w: pl.BlockSpec((WINDOW, w), lambda b, i: (b * nb + i, 0))
    prev = lambda w: pl.BlockSpec((WINDOW, w), lambda b, i: (b * nb + jnp.maximum(i - 1, 0), 0))
    tbl = pl.BlockSpec(tbl_p.shape, lambda b, i: (0, 0, 0))
    return pl.pallas_call(
        _attn_prompt_kernel,
        grid=(bsz, nb),
        in_specs=[pl.BlockSpec(memory_space=pltpu.SMEM), own(Q_DIM), prev(KV_DIM), own(KV_DIM),
                  prev(KV_DIM), own(KV_DIM), tbl, tbl],
        out_specs=own(Q_DIM),
        out_shape=jax.ShapeDtypeStruct((bsz * t, Q_DIM), _BF16),
        compiler_params=pltpu.CompilerParams(dimension_semantics=("arbitrary", "arbitrary"),
                                             vmem_limit_bytes=VMEM_LIMIT),
        name="attn_prompt",
    )(sinks, q, k, k, v, v, tbl_p, tbl_o)


def _attn_sample_kernel(sink_ref, q_ref, kn_ref, vn_ref, ck_ref, cv_ref, tp_ref, to_ref, o_ref, *, sb, steps):
    pad = jnp.zeros((WINDOW - steps, KV_DIM), _F32)

    def one_sequence(s, carry):
        def store(pair, o):
            o_ref[s, :, pair * LANES:(pair + 1) * LANES] = o

        k_own = jnp.concatenate([kn_ref[s], pad], axis=0)
        v_own = jnp.concatenate([vn_ref[s], pad], axis=0)
        _attend(q_ref[s].astype(_BF16), ck_ref[s], k_own, cv_ref[s], v_own,
                tp_ref, to_ref, sink_ref, jnp.float32(0.0), store)
        return carry

    lax.fori_loop(0, sb, one_sequence, 0)


def _attn_sample(q, k_new, v_new, cache_k, cache_v, tbl_p, tbl_o, sinks, sb):
    nseq, steps, _ = q.shape
    seq = lambda r, w: pl.BlockSpec((sb, r, w), lambda i: (i, 0, 0))
    tbl = pl.BlockSpec(tbl_p.shape, lambda i: (0, 0, 0))
    return pl.pallas_call(
        functools.partial(_attn_sample_kernel, sb=sb, steps=steps),
        grid=(nseq // sb,),
        in_specs=[pl.BlockSpec(memory_space=pltpu.SMEM), seq(steps, Q_DIM), seq(steps, KV_DIM), seq(steps, KV_DIM),
                  seq(WINDOW, KV_DIM), seq(WINDOW, KV_DIM), tbl, tbl],
        out_specs=seq(steps, Q_DIM),
        out_shape=jax.ShapeDtypeStruct((nseq, steps, Q_DIM), _F32),
        compiler_params=pltpu.CompilerParams(dimension_semantics=("arbitrary",),
                                             vmem_limit_bytes=VMEM_LIMIT),
        name="attn_sample",
    )(sinks, q, k_new, v_new, cache_k, cache_v, tbl_p, tbl_o)


def _lane_min_index(mask, lane):
    return jnp.min(jnp.where(mask, lane, LANES), axis=-1, keepdims=True)


def _finish_kernel(x_ref, conv_ref, o_ref, ga_ref, gb_ref, wa_ref, wo_ref, ng_ref, wr_hi_ref, wr_lo_ref, br_ref,
                   h_ref, hn_ref, route_ref):
    attn_out = _dot(o_ref[...].astype(_BF16), wa_ref[...])
    merged = ga_ref[...] * conv_ref[...] + gb_ref[...] * attn_out
    h = x_ref[...] + _dot(merged.astype(_BF16), wo_ref[...])
    h_ref[...] = h
    hn = h * lax.rsqrt(jnp.mean(h * h, axis=-1, keepdims=True) + EPS) * ng_ref[...]
    hn_ref[...] = hn

    hi, lo = _split_bf16(hn)
    logits = _dot(hi, wr_hi_ref[...]) + _dot(lo, wr_hi_ref[...]) + _dot(hi, wr_lo_ref[...]) + br_ref[...]
    lane = lax.broadcasted_iota(jnp.int32, logits.shape, 1)
    gmask = lane < N_GROUPS
    gl = jnp.where(gmask, logits, MASK_VALUE)
    gmax = jnp.max(gl, axis=-1, keepdims=True)
    grp = _lane_min_index(gmask & (gl == gmax), lane)
    p_grp = 1.0 / jnp.sum(jnp.where(gmask, jnp.exp(gl - gmax), 0.0), axis=-1, keepdims=True)
    e_lo = N_GROUPS + grp * EXPERTS_PER_GROUP
    emask = (lane >= e_lo) & (lane < e_lo + EXPERTS_PER_GROUP)
    el = jnp.where(emask, logits, MASK_VALUE)
    ex = jnp.where(emask, jnp.exp(el - jnp.max(el, axis=-1, keepdims=True)), 0.0)
    prob = jnp.where(emask, ex / jnp.sum(ex, axis=-1, keepdims=True), -1.0)
    p1 = jnp.max(prob, axis=-1, keepdims=True)
    i1 = _lane_min_index(prob == p1, lane)
    rest = jnp.where(lane == i1, -1.0, prob)
    p2 = jnp.max(rest, axis=-1, keepdims=True)
    i2 = _lane_min_index(rest == p2, lane)
    w1 = p_grp * p1 / (p1 + p2)
    w2 = p_grp * p2 / (p1 + p2)
    e1 = (i1 - N_GROUPS).astype(_F32)
    e2 = (i2 - N_GROUPS).astype(_F32)
    route_ref[...] = jnp.where(lane == 0, e1, jnp.where(lane == 1, e2, jnp.where(lane == 2, w1,
                               jnp.where(lane == 3, w2, 0.0))))


def _finish(x, conv_out, o, ga, gb, wa_b, wo_b, ng, wr_hi, wr_lo, br, tm):
    n = x.shape[0]
    row = lambda w: pl.BlockSpec((tm, w), lambda i: (i, 0))
    full = lambda a: pl.BlockSpec(a.shape, lambda i: (0,) * a.ndim)
    return pl.pallas_call(
        _finish_kernel,
        grid=(n // tm,),
        in_specs=[row(D_MODEL), row(D_MODEL), row(Q_DIM), row(D_MODEL), row(D_MODEL),
                  full(wa_b), full(wo_b), full(ng), full(wr_hi), full(wr_lo), full(br)],
        out_specs=[row(D_MODEL), row(D_MODEL), row(LANES)],
        out_shape=[jax.ShapeDtypeStruct((n, D_MODEL), _F32),
                   jax.ShapeDtypeStruct((n, D_MODEL), _F32),
                   jax.ShapeDtypeStruct((n, LANES), _F32)],
        compiler_params=pltpu.CompilerParams(dimension_semantics=("arbitrary",),
                                             vmem_limit_bytes=VMEM_LIMIT),
        name="finish",
    )(x, conv_out, o, ga, gb, wa_b, wo_b, ng, wr_hi, wr_lo, br)


SC_CORES = 2
SC_SUBCORES = 16
SC_WORKERS = SC_CORES * SC_SUBCORES
SC_CHUNK = 32


def _sc_gather_rows(table, idx):
    b = idx.shape[0]
    d = table.shape[1]
    per_worker = b // SC_WORKERS
    n_chunks = per_worker // SC_CHUNK
    assert per_worker * SC_WORKERS == b and n_chunks * SC_CHUNK == per_worker, (b,)
    mesh = plsc.VectorSubcoreMesh(core_axis_name="c", subcore_axis_name="s",
                                  num_cores=SC_CORES, num_subcores=SC_SUBCORES)

    @functools.partial(
        pl.kernel, mesh=mesh,
        out_type=jax.ShapeDtypeStruct((b, d), table.dtype),
        scratch_types=[pltpu.VMEM((SC_CHUNK,), jnp.int32), pltpu.VMEM((SC_CHUNK, d), table.dtype),
                       pltpu.SemaphoreType.DMA],
        name="sc_gather_rows",
    )
    def gather(table_hbm, idx_hbm, out_hbm, idx_v, rows_v, sem):
        worker = lax.axis_index("s") * SC_CORES + lax.axis_index("c")

        @pl.loop(0, n_chunks)
        def _(c):
            base = pl.multiple_of(worker * per_worker + c * SC_CHUNK, SC_CHUNK)
            pltpu.sync_copy(idx_hbm.at[pl.ds(base, SC_CHUNK)], idx_v)
            pltpu.async_copy(table_hbm.at[idx_v], rows_v, sem).wait()
            pltpu.sync_copy(rows_v, out_hbm.at[pl.ds(base, SC_CHUNK)])

    return gather(table, idx)


def _expert_kernel(blk_expert_ref, x_ref, wg_ref, wu_ref, wd_ref, yb_ref):
    del blk_expert_ref
    xb = x_ref[...].astype(_BF16)
    g = _dot(xb, wg_ref[0])
    u = _dot(xb, wu_ref[0])
    hid = g * jax.nn.sigmoid(g) * u
    yb_ref[...] = _dot(hid.astype(_BF16), wd_ref[0])


def _experts(blk_expert, xs, wg_b, wu_b, wd_b):
    n_blocks = blk_expert.shape[0]
    wspec = lambda a: pl.BlockSpec((1,) + a.shape[1:], lambda b, be: (be[b], 0, 0))
    rows = pl.BlockSpec((MOE_ROWS, D_MODEL), lambda b, be: (b, 0))
    grid_spec = pltpu.PrefetchScalarGridSpec(
        num_scalar_prefetch=1,
        grid=(n_blocks,),
        in_specs=[rows, wspec(wg_b), wspec(wu_b), wspec(wd_b)],
        out_specs=rows,
    )
    return pl.pallas_call(
        _expert_kernel,
        grid_spec=grid_spec,
        out_shape=jax.ShapeDtypeStruct((n_blocks * MOE_ROWS, D_MODEL), _F32),
        compiler_params=pltpu.CompilerParams(dimension_semantics=("arbitrary",),
                                             vmem_limit_bytes=VMEM_LIMIT),
        name="experts",
    )(blk_expert, xs, wg_b, wu_b, wd_b)


def _combine_kernel(h_ref, route_ref, g_ref, y_ref):
    route = route_ref[...]
    y_ref[...] = h_ref[...] + route[:, 2:3] * g_ref[:, :D_MODEL] + route[:, 3:4] * g_ref[:, D_MODEL:]


def _combine(h, route, g, tm):
    n = h.shape[0]
    row = lambda w: pl.BlockSpec((tm, w), lambda i: (i, 0))
    return pl.pallas_call(
        _combine_kernel,
        grid=(n // tm,),
        in_specs=[row(D_MODEL), row(LANES), row(TOP_K * D_MODEL)],
        out_specs=row(D_MODEL),
        out_shape=jax.ShapeDtypeStruct((n, D_MODEL), _F32),
        compiler_params=pltpu.CompilerParams(dimension_semantics=("arbitrary",),
                                             vmem_limit_bytes=VMEM_LIMIT),
        name="combine",
    )(h, route, g)


def _moe(h, hn, route, wg_b, wu_b, wd_b, tm):
    n = h.shape[0]
    n_pairs = n * TOP_K
    e_flat = route[:, 0:TOP_K].astype(jnp.int32).reshape(n_pairs)
    order = jnp.argsort(e_flat).astype(jnp.int32)
    e_sorted = e_flat[order]
    counts = jnp.zeros((N_EXPERTS,), jnp.int32).at[e_flat].add(1)
    starts = jnp.cumsum(counts) - counts
    pcounts = (counts + MOE_ROWS - 1) // MOE_ROWS * MOE_ROWS
    pends = jnp.cumsum(pcounts)
    pstarts = pends - pcounts
    dest_sorted = pstarts[e_sorted] + jnp.arange(n_pairs, dtype=jnp.int32) - starts[e_sorted]
    n_blocks = -(-(n_pairs + N_EXPERTS * (MOE_ROWS - 1)) // MOE_ROWS)
    rows = n_blocks * MOE_ROWS
    tok_buf = jnp.zeros((rows,), jnp.int32).at[dest_sorted].set(order // TOP_K)
    dest = jnp.zeros((n_pairs,), jnp.int32).at[order].set(dest_sorted)
    blk_start = jnp.arange(n_blocks, dtype=jnp.int32) * MOE_ROWS
    blk_expert = jnp.minimum(jnp.sum(blk_start[:, None] >= pends[None, :], axis=1), N_EXPERTS - 1).astype(jnp.int32)
    xs = _sc_gather_rows(hn, tok_buf)
    yb = _experts(blk_expert, xs, wg_b, wu_b, wd_b)
    g = _sc_gather_rows(yb, dest).reshape(n, TOP_K * D_MODEL)
    return _combine(h, route, g, tm)


def kernel(x_prompt, x_sample, state_conv, cache_k, cache_v, norm_attn_g, w_in, q_norm_g, k_norm_g, rel_bias, attn_sinks, w_dw, b_dw, conv_ln_g, conv_ln_b, w_conv_out, b_conv_out, w_attn_out, w_out, norm_ffn_g, w_grp, b_grp, w_router, b_router, w_gate, w_up, w_down):
    bsz, t, _ = x_prompt.shape
    nseq, steps, _ = x_sample.shape
    row = lambda a: a.reshape(1, -1).astype(_F32)

    w_in_b = w_in.astype(_BF16)
    wco_b = w_conv_out.astype(_BF16)
    wa_b = w_attn_out.astype(_BF16)
    wo_b = w_out.astype(_BF16)
    wg_b, wu_b, wd_b = w_gate.astype(_BF16), w_up.astype(_BF16), w_down.astype(_BF16)
    qg = row(jnp.tile(q_norm_g, N_HEADS)) * (HEAD_DIM ** -0.5)
    kg = row(jnp.tile(k_norm_g, N_KV_HEADS))
    group_of_lane = np.arange(Q_DIM) // HEAD_DIM
    gsum = jnp.asarray(group_of_lane[:, None] == np.arange(LANES)[None, :], _BF16)
    gexp = jnp.asarray(np.arange(LANES)[:, None] == group_of_lane[None, :], _BF16)
    w_rt = jnp.zeros((D_MODEL, LANES), _F32).at[:, :N_GROUPS].set(w_grp).at[:, N_GROUPS:N_GROUPS + N_EXPERTS].set(w_router)
    wr_hi = w_rt.astype(_BF16)
    wr_lo = (w_rt - wr_hi.astype(_F32)).astype(_BF16)
    b_rt = jnp.zeros((1, LANES), _F32).at[0, :N_GROUPS].set(b_grp).at[0, N_GROUPS:N_GROUPS + N_EXPERTS].set(b_router)
    tbl_p, tbl_o = _bias_tables(rel_bias)
    conv_params = (w_dw, row(b_dw), row(conv_ln_g), row(conv_ln_b), wco_b, row(b_conv_out))

    def finish_and_moe(x2d, conv_out, o, ga, gb, tm):
        h, hn, route = _finish(x2d, conv_out, o, ga, gb, wa_b, wo_b, row(norm_ffn_g), wr_hi, wr_lo, b_rt, tm)
        return _moe(h, hn, route, wg_b, wu_b, wd_b, tm)

    xp = x_prompt.reshape(bsz * t, D_MODEL)
    glu, q, k, v, ga, gb = _inproj(xp, row(norm_attn_g), w_in_b, qg, kg, gsum, gexp, _BF16, 256)
    conv_out = _conv_prompt(glu, bsz, t, *conv_params, 256)
    o = _attn_prompt(q, k, v, tbl_p, tbl_o, attn_sinks, bsz, t)
    y_prompt = finish_and_moe(xp, conv_out, o, ga, gb, 256).reshape(bsz, t, D_MODEL)
    glu3 = glu.reshape(bsz, t, D_CONV)
    state_conv_prompt = glu3[:, t - (CONV_WIDTH - 1):]
    cache_k_prompt = k.reshape(bsz, t, N_KV_HEADS, HEAD_DIM)[:, t - WINDOW:]
    cache_v_prompt = v.reshape(bsz, t, N_KV_HEADS, HEAD_DIM)[:, t - WINDOW:]

    xs = x_sample.reshape(nseq * steps, D_MODEL)
    glu, q, k, v, ga, gb = _inproj(xs, row(norm_attn_g), w_in_b, qg, kg, gsum, gexp, _F32, 256)
    glu3 = glu.reshape(nseq, steps, D_CONV)
    hist = jnp.concatenate([state_conv, glu3], axis=1)
    conv_out = _conv_sample(hist.transpose(1, 0, 2), *conv_params, 64)
    conv_out = conv_out.transpose(1, 0, 2).reshape(nseq * steps, D_MODEL)
    k3 = k.reshape(nseq, steps, KV_DIM)
    v3 = v.reshape(nseq, steps, KV_DIM)
    o = _attn_sample(q.reshape(nseq, steps, Q_DIM), k3, v3, cache_k.reshape(nseq, WINDOW, KV_DIM),
                     cache_v.reshape(nseq, WINDOW, KV_DIM), tbl_p, tbl_o, attn_sinks, 8)
    y_sample = finish_and_moe(xs, conv_out, o.reshape(nseq * steps, Q_DIM), ga, gb, 256).reshape(nseq, steps, D_MODEL)
    state_conv_sample = hist[:, steps:]
    cache_k_sample = jnp.concatenate([cache_k, k3.reshape(nseq, steps, N_KV_HEADS, HEAD_DIM)], axis=1)[:, steps:]
    cache_v_sample = jnp.concatenate([cache_v, v3.reshape(nseq, steps, N_KV_HEADS, HEAD_DIM)], axis=1)[:, steps:]

    return (y_prompt, y_sample, state_conv_prompt, cache_k_prompt, cache_v_prompt,
            state_conv_sample, cache_k_sample, cache_v_sample)
```

```python
import functools
import math

import numpy as np
import jax
import jax.numpy as jnp
from jax import lax
from jax.experimental import pallas as pl
from jax.experimental.pallas import tpu as pltpu
from jax.experimental.pallas import tpu_sc as plsc

D_MODEL = 1024
N_HEADS = 16
HEAD_DIM = 64
N_KV_HEADS = 4
WINDOW = 128
Q_DIM = N_HEADS * HEAD_DIM
KV_DIM = N_KV_HEADS * HEAD_DIM
N_BUCKETS = 32
MAX_EXACT = N_BUCKETS // 2
MAX_DISTANCE = 128
D_CONV = D_MODEL
CONV_WIDTH = 31
N_GROUPS = 4
EXPERTS_PER_GROUP = 8
N_EXPERTS = N_GROUPS * EXPERTS_PER_GROUP
TOP_K = 2
D_EXPERT = 256
EPS = 1e-6

LANES = 128
SUBLANES = 8
N_PAIRS = N_HEADS // 2
MOE_ROWS = 128
MASK_VALUE = -1e30
VMEM_LIMIT = 48 * 1024 * 1024

_F32 = jnp.float32
_BF16 = jnp.bfloat16


def _dot(a, b):
    return jnp.dot(a, b, preferred_element_type=_F32)


def _split_bf16(x):
    hi = x.astype(_BF16)
    lo = (x - hi.astype(_F32)).astype(_BF16)
    return hi, lo


def _head_rms_scale(z, gsum, gexp):
    hi, lo = _split_bf16(z * z)
    ssum = _dot(hi, gsum) + _dot(lo, gsum)
    r = lax.rsqrt(ssum * (1.0 / HEAD_DIM) + EPS)
    rhi, rlo = _split_bf16(r)
    return _dot(rhi, gexp) + _dot(rlo, gexp)


def _inproj_kernel(x_ref, g_ref, w_ref, qg_ref, kg_ref, gsum_ref, gexp_ref,
                   glu_ref, q_ref, k_ref, v_ref, ga_ref, gb_ref):
    x = x_ref[...]
    xn = x * lax.rsqrt(jnp.mean(x * x, axis=-1, keepdims=True) + EPS) * g_ref[...]
    xb = xn.astype(_BF16)

    def seg(lo, width):
        return _dot(xb, w_ref[:, lo:lo + width])

    a = seg(0, D_CONV)
    b = seg(D_CONV, D_CONV)
    glu_ref[...] = a * jax.nn.sigmoid(b)
    off = 2 * D_CONV
    q = seg(off, Q_DIM)
    q_ref[...] = (q * _head_rms_scale(q, gsum_ref[...], gexp_ref[...]) * qg_ref[...]).astype(q_ref.dtype)
    off += Q_DIM
    k = seg(off, KV_DIM)
    k_ref[...] = k * _head_rms_scale(k, gsum_ref[:KV_DIM, :], gexp_ref[:, :KV_DIM]) * kg_ref[...]
    off += KV_DIM
    v_ref[...] = seg(off, KV_DIM)
    off += KV_DIM
    ga_ref[...] = jax.nn.sigmoid(seg(off, D_MODEL))
    off += D_MODEL
    gb_ref[...] = jax.nn.sigmoid(seg(off, D_MODEL))


def _inproj(x, g, w_in_b, qg, kg, gsum, gexp, q_dtype, tm):
    n = x.shape[0]
    in_dim = w_in_b.shape[1]
    row = lambda w: pl.BlockSpec((tm, w), lambda i: (i, 0))
    full = lambda a: pl.BlockSpec(a.shape, lambda i: (0,) * a.ndim)
    return pl.pallas_call(
        _inproj_kernel,
        grid=(n // tm,),
        in_specs=[row(D_MODEL), full(g), full(w_in_b), full(qg), full(kg), full(gsum), full(gexp)],
        out_specs=[row(D_CONV), row(Q_DIM), row(KV_DIM), row(KV_DIM), row(D_MODEL), row(D_MODEL)],
        out_shape=[jax.ShapeDtypeStruct((n, D_CONV), _F32),
                   jax.ShapeDtypeStruct((n, Q_DIM), q_dtype),
                   jax.ShapeDtypeStruct((n, KV_DIM), _F32),
                   jax.ShapeDtypeStruct((n, KV_DIM), _F32),
                   jax.ShapeDtypeStruct((n, D_MODEL), _F32),
                   jax.ShapeDtypeStruct((n, D_MODEL), _F32)],
        compiler_params=pltpu.CompilerParams(dimension_semantics=("arbitrary",),
                                             vmem_limit_bytes=VMEM_LIMIT),
        name="inproj",
    )(x, g, w_in_b, qg, kg, gsum, gexp)


def _ln_swish_project(y, lng_ref, lnb_ref, wo_ref, bo_ref):
    mu = jnp.mean(y, axis=-1, keepdims=True)
    yc = y - mu
    var = jnp.mean(yc * yc, axis=-1, keepdims=True)
    z = yc * lax.rsqrt(var + EPS) * lng_ref[...] + lnb_ref[...]
    z = z * jax.nn.sigmoid(z)
    return _dot(z.astype(_BF16), wo_ref[...]) + bo_ref[...]


HALO = 32
CONV_ROWS = 64


def _conv_prompt_kernel(glu_ref, wdw_ref, bdw_ref, lng_ref, lnb_ref, wo_ref, bo_ref,
                        out_ref, ext_ref, y_ref, *, tm):
    i = pl.program_id(1)

    @pl.when(i == 0)
    def _():
        ext_ref[0:HALO, :] = jnp.zeros((HALO, D_CONV), _F32)

    @pl.when(i > 0)
    def _():
        ext_ref[0:HALO, :] = ext_ref[tm:tm + HALO, :]

    ext_ref[HALO:HALO + tm, :] = glu_ref[...]

    base = HALO - (CONV_WIDTH - 1)

    def lane_chunk(c, carry):
        cs = pl.ds(pl.multiple_of(c * LANES, LANES), LANES)
        for rc in range(tm // CONV_ROWS):
            r0 = rc * CONV_ROWS
            acc = jnp.zeros((CONV_ROWS, LANES), _F32)
            for r in range(SUBLANES):
                taps = [j for j in range(CONV_WIDTH) if (base + j) % SUBLANES == r]
                a_lo = (base + taps[0]) // SUBLANES
                a_hi = (base + taps[-1]) // SUBLANES
                span = CONV_ROWS + (a_hi - a_lo) * SUBLANES
                slab = ext_ref[pl.ds(r0 + a_lo * SUBLANES + r, span), cs]
                for j in taps:
                    a = (base + j) // SUBLANES - a_lo
                    acc = acc + slab[a * SUBLANES:a * SUBLANES + CONV_ROWS, :] * wdw_ref[pl.ds(j, 1), cs]
            y_ref[pl.ds(r0, CONV_ROWS), cs] = acc + bdw_ref[:, cs]
        return carry

    lax.fori_loop(0, D_CONV // LANES, lane_chunk, 0)
    out_ref[...] = _ln_swish_project(y_ref[...], lng_ref, lnb_ref, wo_ref, bo_ref)


def _conv_prompt(glu, bsz, t, w_dw, b_dw, lng, lnb, wo_b, bo, tm):
    nt = t // tm
    full = lambda a: pl.BlockSpec(a.shape, lambda b, i: (0,) * a.ndim)
    row = pl.BlockSpec((tm, D_CONV), lambda b, i: (b * nt + i, 0))
    return pl.pallas_call(
        functools.partial(_conv_prompt_kernel, tm=tm),
        grid=(bsz, nt),
        in_specs=[row, full(w_dw), full(b_dw), full(lng), full(lnb), full(wo_b), full(bo)],
        out_specs=pl.BlockSpec((tm, D_MODEL), lambda b, i: (b * nt + i, 0)),
        out_shape=jax.ShapeDtypeStruct((bsz * t, D_MODEL), _F32),
        scratch_shapes=[pltpu.VMEM((tm + HALO, D_CONV), _F32), pltpu.VMEM((tm, D_CONV), _F32)],
        compiler_params=pltpu.CompilerParams(dimension_semantics=("arbitrary", "arbitrary"),
                                             vmem_limit_bytes=VMEM_LIMIT),
        name="conv_prompt",
    )(glu, w_dw, b_dw, lng, lnb, wo_b, bo)


def _conv_sample_kernel(hist_ref, wdw_ref, bdw_ref, lng_ref, lnb_ref, wo_ref, bo_ref, out_ref, *, steps):
    for t in range(steps):
        acc = hist_ref[t] * wdw_ref[0:1, :]
        for j in range(1, CONV_WIDTH):
            acc = acc + hist_ref[t + j] * wdw_ref[j:j + 1, :]
        out_ref[t] = _ln_swish_project(acc + bdw_ref[...], lng_ref, lnb_ref, wo_ref, bo_ref)


def _conv_sample(hist, w_dw, b_dw, lng, lnb, wo_b, bo, sb):
    rows, nseq, _ = hist.shape
    steps = rows - (CONV_WIDTH - 1)
    full = lambda a: pl.BlockSpec(a.shape, lambda i: (0,) * a.ndim)
    return pl.pallas_call(
        functools.partial(_conv_sample_kernel, steps=steps),
        grid=(nseq // sb,),
        in_specs=[pl.BlockSpec((rows, sb, D_CONV), lambda i: (0, i, 0)),
                  full(w_dw), full(b_dw), full(lng), full(lnb), full(wo_b), full(bo)],
        out_specs=pl.BlockSpec((steps, sb, D_MODEL), lambda i: (0, i, 0)),
        out_shape=jax.ShapeDtypeStruct((steps, nseq, D_MODEL), _F32),
        compiler_params=pltpu.CompilerParams(dimension_semantics=("arbitrary",),
                                             vmem_limit_bytes=VMEM_LIMIT),
        name="conv_sample",
    )(hist, w_dw, b_dw, lng, lnb, wo_b, bo)


def _bucket_map():
    i = np.arange(WINDOW)[:, None]
    j = np.arange(WINDOW)[None, :]
    n = (i - j) % WINDOW
    nf = np.maximum(n, 1).astype(np.float32)
    large = MAX_EXACT + (np.log(nf / np.float32(MAX_EXACT)) / np.float32(math.log(MAX_DISTANCE / MAX_EXACT))
                         * np.float32(N_BUCKETS - MAX_EXACT)).astype(np.int32)
    return np.where(n < MAX_EXACT, n, np.minimum(large, N_BUCKETS - 1)).astype(np.int32)


def _bias_table_kernel(rb_ref, bm_ref, tbl_ref):
    p = pl.program_id(0)
    bm = bm_ref[...]
    for half in range(2):
        h = 2 * p + half
        t = jnp.zeros(bm.shape, _F32)
        for b in range(N_BUCKETS):
            t = jnp.where(bm == b, rb_ref[b, h], t)
        tbl_ref[0, :, half * WINDOW:(half + 1) * WINDOW] = t


def _bias_tables(rel_bias):
    bm = jnp.asarray(_bucket_map())
    return pl.pallas_call(
        _bias_table_kernel,
        grid=(N_PAIRS,),
        in_specs=[pl.BlockSpec(memory_space=pltpu.SMEM), pl.BlockSpec(bm.shape, lambda p: (0, 0))],
        out_specs=pl.BlockSpec((1, WINDOW, 2 * WINDOW), lambda p: (p, 0, 0)),
        out_shape=jax.ShapeDtypeStruct((N_PAIRS, WINDOW, 2 * WINDOW), _F32),
        name="bias_tables",
    )(rel_bias, bm)


def _block_diag_pairs(slab):
    low = lax.broadcasted_iota(jnp.int32, slab.shape, 1) < HEAD_DIM
    swapped = pltpu.roll(slab, HEAD_DIM, axis=1)
    zero = jnp.zeros_like(slab)
    first = jnp.concatenate([jnp.where(low, slab, zero), jnp.where(low, zero, swapped)], axis=0)
    second = jnp.concatenate([jnp.where(low, swapped, zero), jnp.where(low, zero, slab)], axis=0)
    return first.astype(_BF16), second.astype(_BF16)


def _attend(q, k_prev, k_own, v_prev, v_own, tbl_ref, sink_ref, prev_shift, store):
    tq = q.shape[0]
    rows = 2 * tq
    row = lax.broadcasted_iota(jnp.int32, (rows, 2 * WINDOW), 0)
    col = lax.broadcasted_iota(jnp.int32, (rows, 2 * WINDOW), 1)
    from_prev = (col & (WINDOW - 1)) > jnp.where(row >= tq, row - tq, row)
    top = lax.broadcasted_iota(jnp.int32, (rows, 1), 0) < tq
    low = lax.broadcasted_iota(jnp.int32, (rows, LANES), 1) < HEAD_DIM
    r2 = lax.broadcasted_iota(jnp.int32, (2 * WINDOW, LANES), 0)
    c2 = lax.broadcasted_iota(jnp.int32, (2 * WINDOW, LANES), 1)
    head_ones = jnp.where((r2 < WINDOW) == (c2 < HEAD_DIM), 1.0, 0.0).astype(_BF16)
    contract_last = (((1,), (1,)), ((), ()))
    for slab in range(KV_DIM // LANES):
        cols = slice(slab * LANES, (slab + 1) * LANES)
        kp_pair = _block_diag_pairs(k_prev[:, cols])
        ko_pair = _block_diag_pairs(k_own[:, cols])
        vp_pair = _block_diag_pairs(v_prev[:, cols])
        vo_pair = _block_diag_pairs(v_own[:, cols])
        for sub in range(2):
            pair_a = 2 * (2 * slab + sub)
            pair_b = pair_a + 1
            qq = jnp.concatenate([q[:, pair_a * LANES:(pair_a + 1) * LANES],
                                  q[:, pair_b * LANES:(pair_b + 1) * LANES]], axis=0).astype(_BF16)
            sp = lax.dot_general(qq, kp_pair[sub], contract_last, preferred_element_type=_F32)
            so = lax.dot_general(qq, ko_pair[sub], contract_last, preferred_element_type=_F32)
            bias = jnp.concatenate([tbl_ref[pair_a, 0:tq, :], tbl_ref[pair_b, 0:tq, :]], axis=0)
            s = jnp.where(from_prev, sp + prev_shift, so) + bias
            sink_even = jnp.where(top, sink_ref[2 * pair_a], sink_ref[2 * pair_b])
            sink_odd = jnp.where(top, sink_ref[2 * pair_a + 1], sink_ref[2 * pair_b + 1])
            m_even = jnp.maximum(jnp.max(s[:, :WINDOW], axis=-1, keepdims=True), sink_even)
            m_odd = jnp.maximum(jnp.max(s[:, WINDOW:], axis=-1, keepdims=True), sink_odd)
            p = jnp.exp(s - jnp.where(col < WINDOW, m_even, m_odd)).astype(_BF16)
            zero = jnp.zeros_like(p)
            o = _dot(jnp.where(from_prev, p, zero), vp_pair[sub]) + _dot(jnp.where(from_prev, zero, p), vo_pair[sub])
            den = _dot(p, head_ones) + jnp.where(low, jnp.exp(sink_even - m_even), jnp.exp(sink_odd - m_odd))
            o = o / den
            store(pair_a, o[:tq])
            store(pair_b, o[tq:])


def _attn_prompt_kernel(sink_ref, q_ref, kp_ref, ko_ref, vp_ref, vo_ref, tbl_ref, o_ref):
    prev_shift = jnp.where(pl.program_id(1) == 0, MASK_VALUE, 0.0).astype(_F32)

    def store(pair, o):
        o_ref[:, pair * LANES:(pair + 1) * LANES] = o.astype(o_ref.dtype)

    _attend(q_ref[...], kp_ref[...], ko_ref[...], vp_ref[...], vo_ref[...], tbl_ref, sink_ref, prev_shift, store)


def _attn_prompt(q, k, v, tbl, sinks, bsz, t):
    nb = t // WINDOW
    own = lambda w: pl.BlockSpec((WINDOW, w), lambda b, i: (b * nb + i, 0))
    prev = lambda w: pl.BlockSpec((WINDOW, w), lambda b, i: (b * nb + jnp.maximum(i - 1, 0), 0))
    return pl.pallas_call(
        _attn_prompt_kernel,
        grid=(bsz, nb),
        in_specs=[pl.BlockSpec(memory_space=pltpu.SMEM), own(Q_DIM), prev(KV_DIM), own(KV_DIM),
                  prev(KV_DIM), own(KV_DIM), pl.BlockSpec(tbl.shape, lambda b, i: (0, 0, 0))],
        out_specs=own(Q_DIM),
        out_shape=jax.ShapeDtypeStruct((bsz * t, Q_DIM), _BF16),
        compiler_params=pltpu.CompilerParams(dimension_semantics=("arbitrary", "arbitrary"),
                                             vmem_limit_bytes=VMEM_LIMIT),
        name="attn_prompt",
    )(sinks, q, k, k, v, v, tbl)


def _attn_sample_kernel(sink_ref, q_ref, kn_ref, vn_ref, ck_ref, cv_ref, tbl_ref, o_ref, *, sb, steps):
    pad = jnp.zeros((WINDOW - steps, KV_DIM), _F32)

    def one_sequence(s, carry):
        def store(pair, o):
            o_ref[s, :, pair * LANES:(pair + 1) * LANES] = o

        k_own = jnp.concatenate([kn_ref[s], pad], axis=0)
        v_own = jnp.concatenate([vn_ref[s], pad], axis=0)
        _attend(q_ref[s], ck_ref[s], k_own, cv_ref[s], v_own, tbl_ref, sink_ref, jnp.float32(0.0), store)
        return carry

    lax.fori_loop(0, sb, one_sequence, 0)


def _attn_sample(q, k_new, v_new, cache_k, cache_v, tbl, sinks, sb):
    nseq, steps, _ = q.shape
    seq = lambda r, w: pl.BlockSpec((sb, r, w), lambda i: (i, 0, 0))
    return pl.pallas_call(
        functools.partial(_attn_sample_kernel, sb=sb, steps=steps),
        grid=(nseq // sb,),
        in_specs=[pl.BlockSpec(memory_space=pltpu.SMEM), seq(steps, Q_DIM), seq(steps, KV_DIM), seq(steps, KV_DIM),
                  seq(WINDOW, KV_DIM), seq(WINDOW, KV_DIM), pl.BlockSpec(tbl.shape, lambda i: (0, 0, 0))],
        out_specs=seq(steps, Q_DIM),
        out_shape=jax.ShapeDtypeStruct((nseq, steps, Q_DIM), _F32),
        compiler_params=pltpu.CompilerParams(dimension_semantics=("arbitrary",),
                                             vmem_limit_bytes=VMEM_LIMIT),
        name="attn_sample",
    )(sinks, q, k_new, v_new, cache_k, cache_v, tbl)


def _lane_min_index(mask, lane):
    return jnp.min(jnp.where(mask, lane, LANES), axis=-1, keepdims=True)


def _finish_kernel(x_ref, conv_ref, o_ref, ga_ref, gb_ref, wa_ref, wo_ref, ng_ref, wr_hi_ref, wr_lo_ref, br_ref,
                   h_ref, hn_ref, route_ref):
    attn_out = _dot(o_ref[...].astype(_BF16), wa_ref[...])
    merged = ga_ref[...] * conv_ref[...] + gb_ref[...] * attn_out
    h = x_ref[...] + _dot(merged.astype(_BF16), wo_ref[...])
    h_ref[...] = h
    hn = h * lax.rsqrt(jnp.mean(h * h, axis=-1, keepdims=True) + EPS) * ng_ref[...]
    hn_ref[...] = hn

    hi, lo = _split_bf16(hn)
    logits = _dot(hi, wr_hi_ref[...]) + _dot(lo, wr_hi_ref[...]) + _dot(hi, wr_lo_ref[...]) + br_ref[...]
    lane = lax.broadcasted_iota(jnp.int32, logits.shape, 1)
    gmask = lane < N_GROUPS
    gl = jnp.where(gmask, logits, MASK_VALUE)
    gmax = jnp.max(gl, axis=-1, keepdims=True)
    grp = _lane_min_index(gmask & (gl == gmax), lane)
    p_grp = 1.0 / jnp.sum(jnp.where(gmask, jnp.exp(gl - gmax), 0.0), axis=-1, keepdims=True)
    e_lo = N_GROUPS + grp * EXPERTS_PER_GROUP
    emask = (lane >= e_lo) & (lane < e_lo + EXPERTS_PER_GROUP)
    el = jnp.where(emask, logits, MASK_VALUE)
    ex = jnp.where(emask, jnp.exp(el - jnp.max(el, axis=-1, keepdims=True)), 0.0)
    prob = jnp.where(emask, ex / jnp.sum(ex, axis=-1, keepdims=True), -1.0)
    p1 = jnp.max(prob, axis=-1, keepdims=True)
    i1 = _lane_min_index(prob == p1, lane)
    rest = jnp.where(lane == i1, -1.0, prob)
    p2 = jnp.max(rest, axis=-1, keepdims=True)
    i2 = _lane_min_index(rest == p2, lane)
    w1 = p_grp * p1 / (p1 + p2)
    w2 = p_grp * p2 / (p1 + p2)
    e1 = (i1 - N_GROUPS).astype(_F32)
    e2 = (i2 - N_GROUPS).astype(_F32)
    route_ref[...] = jnp.where(lane == 0, e1, jnp.where(lane == 1, e2, jnp.where(lane == 2, w1,
                               jnp.where(lane == 3, w2, 0.0))))


def _finish(x, conv_out, o, ga, gb, wa_b, wo_b, ng, wr_hi, wr_lo, br, tm):
    n = x.shape[0]
    row = lambda w: pl.BlockSpec((tm, w), lambda i: (i, 0))
    full = lambda a: pl.BlockSpec(a.shape, lambda i: (0,) * a.ndim)
    return pl.pallas_call(
        _finish_kernel,
        grid=(n // tm,),
        in_specs=[row(D_MODEL), row(D_MODEL), row(Q_DIM), row(D_MODEL), row(D_MODEL),
                  full(wa_b), full(wo_b), full(ng), full(wr_hi), full(wr_lo), full(br)],
        out_specs=[row(D_MODEL), row(D_MODEL), row(LANES)],
        out_shape=[jax.ShapeDtypeStruct((n, D_MODEL), _F32),
                   jax.ShapeDtypeStruct((n, D_MODEL), _F32),
                   jax.ShapeDtypeStruct((n, LANES), _F32)],
        compiler_params=pltpu.CompilerParams(dimension_semantics=("arbitrary",),
                                             vmem_limit_bytes=VMEM_LIMIT),
        name="finish",
    )(x, conv_out, o, ga, gb, wa_b, wo_b, ng, wr_hi, wr_lo, br)


SC_CORES = 2
SC_SUBCORES = 16
SC_WORKERS = SC_CORES * SC_SUBCORES
SC_CHUNK = 32


def _sc_gather_rows(table, idx):
    b = idx.shape[0]
    d = table.shape[1]
    per_worker = b // SC_WORKERS
    n_chunks = per_worker // SC_CHUNK
    assert per_worker * SC_WORKERS == b and n_chunks * SC_CHUNK == per_worker, (b,)
    mesh = plsc.VectorSubcoreMesh(core_axis_name="c", subcore_axis_name="s",
                                  num_cores=SC_CORES, num_subcores=SC_SUBCORES)

    @functools.partial(
        pl.kernel, mesh=mesh,
        out_type=jax.ShapeDtypeStruct((b, d), table.dtype),
        scratch_types=[pltpu.VMEM((SC_CHUNK,), jnp.int32), pltpu.VMEM((SC_CHUNK, d), table.dtype),
                       pltpu.SemaphoreType.DMA],
        name="sc_gather_rows",
    )
    def gather(table_hbm, idx_hbm, out_hbm, idx_v, rows_v, sem):
        worker = lax.axis_index("s") * SC_CORES + lax.axis_index("c")

        @pl.loop(0, n_chunks)
        def _(c):
            base = pl.multiple_of(worker * per_worker + c * SC_CHUNK, SC_CHUNK)
            pltpu.sync_copy(idx_hbm.at[pl.ds(base, SC_CHUNK)], idx_v)
            pltpu.async_copy(table_hbm.at[idx_v], rows_v, sem).wait()
            pltpu.sync_copy(rows_v, out_hbm.at[pl.ds(base, SC_CHUNK)])

    return gather(table, idx)


def _expert_kernel(blk_expert_ref, x_ref, wg_ref, wu_ref, wd_ref, yb_ref):
    del blk_expert_ref
    xb = x_ref[...].astype(_BF16)
    g = _dot(xb, wg_ref[0])
    u = _dot(xb, wu_ref[0])
    hid = g * jax.nn.sigmoid(g) * u
    yb_ref[...] = _dot(hid.astype(_BF16), wd_ref[0])


def _experts(blk_expert, xs, wg_b, wu_b, wd_b):
    n_blocks = blk_expert.shape[0]
    wspec = lambda a: pl.BlockSpec((1,) + a.shape[1:], lambda b, be: (be[b], 0, 0))
    rows = pl.BlockSpec((MOE_ROWS, D_MODEL), lambda b, be: (b, 0))
    grid_spec = pltpu.PrefetchScalarGridSpec(
        num_scalar_prefetch=1,
        grid=(n_blocks,),
        in_specs=[rows, wspec(wg_b), wspec(wu_b), wspec(wd_b)],
        out_specs=rows,
    )
    return pl.pallas_call(
        _expert_kernel,
        grid_spec=grid_spec,
        out_shape=jax.ShapeDtypeStruct((n_blocks * MOE_ROWS, D_MODEL), _F32),
        compiler_params=pltpu.CompilerParams(dimension_semantics=("arbitrary",),
                                             vmem_limit_bytes=VMEM_LIMIT),
        name="experts",
    )(blk_expert, xs, wg_b, wu_b, wd_b)


def _combine_kernel(h_ref, route_ref, g0_ref, g1_ref, y_ref):
    route = route_ref[...]
    y_ref[...] = h_ref[...] + route[:, 2:3] * g0_ref[...] + route[:, 3:4] * g1_ref[...]


def _combine(h, route, g, tm):
    n = h.shape[0]
    nt = n // tm
    row = lambda w: pl.BlockSpec((tm, w), lambda i: (i, 0))
    return pl.pallas_call(
        _combine_kernel,
        grid=(nt,),
        in_specs=[row(D_MODEL), row(LANES), row(D_MODEL), pl.BlockSpec((tm, D_MODEL), lambda i: (nt + i, 0))],
        out_specs=row(D_MODEL),
        out_shape=jax.ShapeDtypeStruct((n, D_MODEL), _F32),
        compiler_params=pltpu.CompilerParams(dimension_semantics=("arbitrary",),
                                             vmem_limit_bytes=VMEM_LIMIT),
        name="combine",
    )(h, route, g, g)


def _moe(h, hn, route, wg_b, wu_b, wd_b, tm):
    n = h.shape[0]
    n_pairs = n * TOP_K
    e_flat = route[:, 0:TOP_K].astype(jnp.int32).reshape(n_pairs)
    order = jnp.argsort(e_flat).astype(jnp.int32)
    e_sorted = e_flat[order]
    counts = jnp.zeros((N_EXPERTS,), jnp.int32).at[e_flat].add(1)
    starts = jnp.cumsum(counts) - counts
    pcounts = (counts + MOE_ROWS - 1) // MOE_ROWS * MOE_ROWS
    pends = jnp.cumsum(pcounts)
    pstarts = pends - pcounts
    dest_sorted = pstarts[e_sorted] + jnp.arange(n_pairs, dtype=jnp.int32) - starts[e_sorted]
    n_blocks = -(-(n_pairs + N_EXPERTS * (MOE_ROWS - 1)) // MOE_ROWS)
    rows = n_blocks * MOE_ROWS
    tok_buf = jnp.zeros((rows,), jnp.int32).at[dest_sorted].set(order // TOP_K)
    dest = jnp.zeros((n_pairs,), jnp.int32).at[order].set(dest_sorted)
    blk_start = jnp.arange(n_blocks, dtype=jnp.int32) * MOE_ROWS
    blk_expert = jnp.minimum(jnp.sum(blk_start[:, None] >= pends[None, :], axis=1), N_EXPERTS - 1).astype(jnp.int32)
    xs = _sc_gather_rows(hn, tok_buf)
    yb = _experts(blk_expert, xs, wg_b, wu_b, wd_b)
    g = _sc_gather_rows(yb, dest.reshape(n, TOP_K).T.reshape(n_pairs))
    return _combine(h, route, g, tm)


def kernel(x_prompt, x_sample, state_conv, cache_k, cache_v, norm_attn_g, w_in, q_norm_g, k_norm_g, rel_bias, attn_sinks, w_dw, b_dw, conv_ln_g, conv_ln_b, w_conv_out, b_conv_out, w_attn_out, w_out, norm_ffn_g, w_grp, b_grp, w_router, b_router, w_gate, w_up, w_down):
    bsz, t, _ = x_prompt.shape
    nseq, steps, _ = x_sample.shape
    row = lambda a: a.reshape(1, -1).astype(_F32)

    w_in_b = w_in.astype(_BF16)
    wco_b = w_conv_out.astype(_BF16)
    wa_b = w_attn_out.astype(_BF16)
    wo_b = w_out.astype(_BF16)
    wg_b, wu_b, wd_b = w_gate.astype(_BF16), w_up.astype(_BF16), w_down.astype(_BF16)
    qg = row(jnp.tile(q_norm_g, N_HEADS)) * (HEAD_DIM ** -0.5)
    kg = row(jnp.tile(k_norm_g, N_KV_HEADS))
    group_of_lane = np.arange(Q_DIM) // HEAD_DIM
    gsum = jnp.asarray(group_of_lane[:, None] == np.arange(LANES)[None, :], _BF16)
    gexp = jnp.asarray(np.arange(LANES)[:, None] == group_of_lane[None, :], _BF16)
    w_rt = jnp.zeros((D_MODEL, LANES), _F32).at[:, :N_GROUPS].set(w_grp).at[:, N_GROUPS:N_GROUPS + N_EXPERTS].set(w_router)
    wr_hi = w_rt.astype(_BF16)
    wr_lo = (w_rt - wr_hi.astype(_F32)).astype(_BF16)
    b_rt = jnp.zeros((1, LANES), _F32).at[0, :N_GROUPS].set(b_grp).at[0, N_GROUPS:N_GROUPS + N_EXPERTS].set(b_router)
    tbl = _bias_tables(rel_bias)
    conv_params = (w_dw, row(b_dw), row(conv_ln_g), row(conv_ln_b), wco_b, row(b_conv_out))

    def finish_and_moe(x2d, conv_out, o, ga, gb, tm):
        h, hn, route = _finish(x2d, conv_out, o, ga, gb, wa_b, wo_b, row(norm_ffn_g), wr_hi, wr_lo, b_rt, tm)
        return _moe(h, hn, route, wg_b, wu_b, wd_b, tm)

    xp = x_prompt.reshape(bsz * t, D_MODEL)
    glu, q, k, v, ga, gb = _inproj(xp, row(norm_attn_g), w_in_b, qg, kg, gsum, gexp, _BF16, 256)
    conv_out = _conv_prompt(glu, bsz, t, *conv_params, 256)
    o = _attn_prompt(q, k, v, tbl, attn_sinks, bsz, t)
    y_prompt = finish_and_moe(xp, conv_out, o, ga, gb, 256).reshape(bsz, t, D_MODEL)
    glu3 = glu.reshape(bsz, t, D_CONV)
    state_conv_prompt = glu3[:, t - (CONV_WIDTH - 1):]
    tail = lambda a: a.reshape(bsz, t, KV_DIM)[:, t - WINDOW:].reshape(bsz, WINDOW, N_KV_HEADS, HEAD_DIM)
    cache_k_prompt, cache_v_prompt = tail(k), tail(v)

    xs = x_sample.reshape(nseq * steps, D_MODEL)
    glu, q, k, v, ga, gb = _inproj(xs, row(norm_attn_g), w_in_b, qg, kg, gsum, gexp, _F32, 256)
    glu3 = glu.reshape(nseq, steps, D_CONV)
    hist = jnp.concatenate([state_conv, glu3], axis=1)
    conv_out = _conv_sample(hist.transpose(1, 0, 2), *conv_params, 64)
    conv_out = conv_out.transpose(1, 0, 2).reshape(nseq * steps, D_MODEL)
    k3 = k.reshape(nseq, steps, KV_DIM)
    v3 = v.reshape(nseq, steps, KV_DIM)
    o = _attn_sample(q.reshape(nseq, steps, Q_DIM), k3, v3, cache_k.reshape(nseq, WINDOW, KV_DIM),
                     cache_v.reshape(nseq, WINDOW, KV_DIM), tbl, attn_sinks, 8)
    y_sample = finish_and_moe(xs, conv_out, o.reshape(nseq * steps, Q_DIM), ga, gb, 256).reshape(nseq, steps, D_MODEL)
    state_conv_sample = hist[:, steps:]
    cache_k_sample = jnp.concatenate([cache_k, k3.reshape(nseq, steps, N_KV_HEADS, HEAD_DIM)], axis=1)[:, steps:]
    cache_v_sample = jnp.concatenate([cache_v, v3.reshape(nseq, steps, N_KV_HEADS, HEAD_DIM)], axis=1)[:, steps:]

    return (y_prompt, y_sample, state_conv_prompt, cache_k_prompt, cache_v_prompt,
            state_conv_sample, cache_k_sample, cache_v_sample)
```

```python
import functools
import math

import numpy as np
import jax
import jax.numpy as jnp
from jax import lax
from jax.experimental import pallas as pl
from jax.experimental.pallas import tpu as pltpu
from jax.experimental.pallas import tpu_sc as plsc

D_MODEL = 1024
N_HEADS = 16
HEAD_DIM = 64
N_KV_HEADS = 4
WINDOW = 128
Q_DIM = N_HEADS * HEAD_DIM
KV_DIM = N_KV_HEADS * HEAD_DIM
N_BUCKETS = 32
MAX_EXACT = N_BUCKETS // 2
MAX_DISTANCE = 128
D_CONV = D_MODEL
CONV_WIDTH = 31
N_GROUPS = 4
EXPERTS_PER_GROUP = 8
N_EXPERTS = N_GROUPS * EXPERTS_PER_GROUP
TOP_K = 2
D_EXPERT = 256
EPS = 1e-6

LANES = 128
SUBLANES = 8
N_PAIRS = N_HEADS // 2
MOE_ROWS = 256
MASK_VALUE = -1e30
VMEM_LIMIT = 48 * 1024 * 1024

_F32 = jnp.float32
_BF16 = jnp.bfloat16


def _dot(a, b):
    return jnp.dot(a, b, preferred_element_type=_F32)


def _split_bf16(x):
    hi = x.astype(_BF16)
    lo = (x - hi.astype(_F32)).astype(_BF16)
    return hi, lo


def _head_rms_scale(z, gsum, gexp):
    hi, lo = _split_bf16(z * z)
    ssum = _dot(hi, gsum) + _dot(lo, gsum)
    r = lax.rsqrt(ssum * (1.0 / HEAD_DIM) + EPS)
    rhi, rlo = _split_bf16(r)
    return _dot(rhi, gexp) + _dot(rlo, gexp)


def _inproj_kernel(x_ref, g_ref, w_ref, qg_ref, kg_ref, gsum_ref, gexp_ref,
                   glu_ref, q_ref, k_ref, v_ref, ga_ref, gb_ref):
    x = x_ref[...]
    xn = x * lax.rsqrt(jnp.mean(x * x, axis=-1, keepdims=True) + EPS) * g_ref[...]
    xb = xn.astype(_BF16)

    def seg(lo, width):
        return _dot(xb, w_ref[:, lo:lo + width])

    a = seg(0, D_CONV)
    b = seg(D_CONV, D_CONV)
    glu_ref[...] = a * jax.nn.sigmoid(b)
    off = 2 * D_CONV
    q = seg(off, Q_DIM)
    q_ref[...] = (q * _head_rms_scale(q, gsum_ref[...], gexp_ref[...]) * qg_ref[...]).astype(q_ref.dtype)
    off += Q_DIM
    k = seg(off, KV_DIM)
    k_ref[...] = k * _head_rms_scale(k, gsum_ref[:KV_DIM, :], gexp_ref[:, :KV_DIM]) * kg_ref[...]
    off += KV_DIM
    v_ref[...] = seg(off, KV_DIM)
    off += KV_DIM
    ga_ref[...] = jax.nn.sigmoid(seg(off, D_MODEL))
    off += D_MODEL
    gb_ref[...] = jax.nn.sigmoid(seg(off, D_MODEL))


def _inproj(x, g, w_in_b, qg, kg, gsum, gexp, q_dtype, tm):
    n = x.shape[0]
    in_dim = w_in_b.shape[1]
    row = lambda w: pl.BlockSpec((tm, w), lambda i: (i, 0))
    full = lambda a: pl.BlockSpec(a.shape, lambda i: (0,) * a.ndim)
    return pl.pallas_call(
        _inproj_kernel,
        grid=(n // tm,),
        in_specs=[row(D_MODEL), full(g), full(w_in_b), full(qg), full(kg), full(gsum), full(gexp)],
        out_specs=[row(D_CONV), row(Q_DIM), row(KV_DIM), row(KV_DIM), row(D_MODEL), row(D_MODEL)],
        out_shape=[jax.ShapeDtypeStruct((n, D_CONV), _F32),
                   jax.ShapeDtypeStruct((n, Q_DIM), q_dtype),
                   jax.ShapeDtypeStruct((n, KV_DIM), _F32),
                   jax.ShapeDtypeStruct((n, KV_DIM), _F32),
                   jax.ShapeDtypeStruct((n, D_MODEL), _F32),
                   jax.ShapeDtypeStruct((n, D_MODEL), _F32)],
        compiler_params=pltpu.CompilerParams(dimension_semantics=("arbitrary",),
                                             vmem_limit_bytes=VMEM_LIMIT),
        name="inproj",
    )(x, g, w_in_b, qg, kg, gsum, gexp)


def _ln_swish_project(y, lng_ref, lnb_ref, wo_ref, bo_ref):
    mu = jnp.mean(y, axis=-1, keepdims=True)
    yc = y - mu
    var = jnp.mean(yc * yc, axis=-1, keepdims=True)
    z = yc * lax.rsqrt(var + EPS) * lng_ref[...] + lnb_ref[...]
    z = z * jax.nn.sigmoid(z)
    return _dot(z.astype(_BF16), wo_ref[...]) + bo_ref[...]


HALO = 32
CONV_ROWS = 64


def _conv_prompt_kernel(glu_ref, wdw_ref, bdw_ref, lng_ref, lnb_ref, wo_ref, bo_ref,
                        out_ref, ext_ref, y_ref, *, tm):
    i = pl.program_id(1)

    @pl.when(i == 0)
    def _():
        ext_ref[0:HALO, :] = jnp.zeros((HALO, D_CONV), _F32)

    @pl.when(i > 0)
    def _():
        ext_ref[0:HALO, :] = ext_ref[tm:tm + HALO, :]

    ext_ref[HALO:HALO + tm, :] = glu_ref[...]

    base = HALO - (CONV_WIDTH - 1)

    def lane_chunk(c, carry):
        cs = pl.ds(pl.multiple_of(c * LANES, LANES), LANES)
        for rc in range(tm // CONV_ROWS):
            r0 = rc * CONV_ROWS
            acc = jnp.zeros((CONV_ROWS, LANES), _F32)
            for r in range(SUBLANES):
                taps = [j for j in range(CONV_WIDTH) if (base + j) % SUBLANES == r]
                a_lo = (base + taps[0]) // SUBLANES
                a_hi = (base + taps[-1]) // SUBLANES
                span = CONV_ROWS + (a_hi - a_lo) * SUBLANES
                slab = ext_ref[pl.ds(r0 + a_lo * SUBLANES + r, span), cs]
                for j in taps:
                    a = (base + j) // SUBLANES - a_lo
                    acc = acc + slab[a * SUBLANES:a * SUBLANES + CONV_ROWS, :] * wdw_ref[pl.ds(j, 1), cs]
            y_ref[pl.ds(r0, CONV_ROWS), cs] = acc + bdw_ref[:, cs]
        return carry

    lax.fori_loop(0, D_CONV // LANES, lane_chunk, 0)
    out_ref[...] = _ln_swish_project(y_ref[...], lng_ref, lnb_ref, wo_ref, bo_ref)


def _conv_prompt(glu, bsz, t, w_dw, b_dw, lng, lnb, wo_b, bo, tm):
    nt = t // tm
    full = lambda a: pl.BlockSpec(a.shape, lambda b, i: (0,) * a.ndim)
    row = pl.BlockSpec((tm, D_CONV), lambda b, i: (b * nt + i, 0))
    return pl.pallas_call(
        functools.partial(_conv_prompt_kernel, tm=tm),
        grid=(bsz, nt),
        in_specs=[row, full(w_dw), full(b_dw), full(lng), full(lnb), full(wo_b), full(bo)],
        out_specs=pl.BlockSpec((tm, D_MODEL), lambda b, i: (b * nt + i, 0)),
        out_shape=jax.ShapeDtypeStruct((bsz * t, D_MODEL), _F32),
        scratch_shapes=[pltpu.VMEM((tm + HALO, D_CONV), _F32), pltpu.VMEM((tm, D_CONV), _F32)],
        compiler_params=pltpu.CompilerParams(dimension_semantics=("arbitrary", "arbitrary"),
                                             vmem_limit_bytes=VMEM_LIMIT),
        name="conv_prompt",
    )(glu, w_dw, b_dw, lng, lnb, wo_b, bo)


def _conv_sample_kernel(hist_ref, wdw_ref, bdw_ref, lng_ref, lnb_ref, wo_ref, bo_ref, out_ref, *, steps):
    for t in range(steps):
        acc = hist_ref[t] * wdw_ref[0:1, :]
        for j in range(1, CONV_WIDTH):
            acc = acc + hist_ref[t + j] * wdw_ref[j:j + 1, :]
        out_ref[t] = _ln_swish_project(acc + bdw_ref[...], lng_ref, lnb_ref, wo_ref, bo_ref)


def _conv_sample(hist, w_dw, b_dw, lng, lnb, wo_b, bo, sb):
    rows, nseq, _ = hist.shape
    steps = rows - (CONV_WIDTH - 1)
    full = lambda a: pl.BlockSpec(a.shape, lambda i: (0,) * a.ndim)
    return pl.pallas_call(
        functools.partial(_conv_sample_kernel, steps=steps),
        grid=(nseq // sb,),
        in_specs=[pl.BlockSpec((rows, sb, D_CONV), lambda i: (0, i, 0)),
                  full(w_dw), full(b_dw), full(lng), full(lnb), full(wo_b), full(bo)],
        out_specs=pl.BlockSpec((steps, sb, D_MODEL), lambda i: (0, i, 0)),
        out_shape=jax.ShapeDtypeStruct((steps, nseq, D_MODEL), _F32),
        compiler_params=pltpu.CompilerParams(dimension_semantics=("arbitrary",),
                                             vmem_limit_bytes=VMEM_LIMIT),
        name="conv_sample",
    )(hist, w_dw, b_dw, lng, lnb, wo_b, bo)


def _bucket_map():
    i = np.arange(WINDOW)[:, None]
    j = np.arange(WINDOW)[None, :]
    n = (i - j) % WINDOW
    nf = np.maximum(n, 1).astype(np.float32)
    large = MAX_EXACT + (np.log(nf / np.float32(MAX_EXACT)) / np.float32(math.log(MAX_DISTANCE / MAX_EXACT))
                         * np.float32(N_BUCKETS - MAX_EXACT)).astype(np.int32)
    return np.where(n < MAX_EXACT, n, np.minimum(large, N_BUCKETS - 1)).astype(np.int32)


def _bias_table_kernel(rb_ref, bm_ref, tbl_ref):
    p = pl.program_id(0)
    bm = bm_ref[...]
    for half in range(2):
        h = 2 * p + half
        t = jnp.zeros(bm.shape, _F32)
        for b in range(N_BUCKETS):
            t = jnp.where(bm == b, rb_ref[b, h], t)
        tbl_ref[0, :, half * WINDOW:(half + 1) * WINDOW] = t


def _bias_tables(rel_bias):
    bm = jnp.asarray(_bucket_map())
    return pl.pallas_call(
        _bias_table_kernel,
        grid=(N_PAIRS,),
        in_specs=[pl.BlockSpec(memory_space=pltpu.SMEM), pl.BlockSpec(bm.shape, lambda p: (0, 0))],
        out_specs=pl.BlockSpec((1, WINDOW, 2 * WINDOW), lambda p: (p, 0, 0)),
        out_shape=jax.ShapeDtypeStruct((N_PAIRS, WINDOW, 2 * WINDOW), _F32),
        name="bias_tables",
    )(rel_bias, bm)


def _block_diag_pairs(slab):
    low = lax.broadcasted_iota(jnp.int32, slab.shape, 1) < HEAD_DIM
    swapped = pltpu.roll(slab, HEAD_DIM, axis=1)
    zero = jnp.zeros_like(slab)
    first = jnp.concatenate([jnp.where(low, slab, zero), jnp.where(low, zero, swapped)], axis=0)
    second = jnp.concatenate([jnp.where(low, swapped, zero), jnp.where(low, zero, slab)], axis=0)
    return first.astype(_BF16), second.astype(_BF16)


def _attend(q, k_prev, k_own, v_prev, v_own, tbl_ref, sink_ref, prev_shift, store):
    tq = q.shape[0]
    rows = 2 * tq
    row = lax.broadcasted_iota(jnp.int32, (rows, 2 * WINDOW), 0)
    col = lax.broadcasted_iota(jnp.int32, (rows, 2 * WINDOW), 1)
    from_prev = (col & (WINDOW - 1)) > jnp.where(row >= tq, row - tq, row)
    top = lax.broadcasted_iota(jnp.int32, (rows, 1), 0) < tq
    low = lax.broadcasted_iota(jnp.int32, (rows, LANES), 1) < HEAD_DIM
    r2 = lax.broadcasted_iota(jnp.int32, (2 * WINDOW, LANES), 0)
    c2 = lax.broadcasted_iota(jnp.int32, (2 * WINDOW, LANES), 1)
    head_ones = jnp.where((r2 < WINDOW) == (c2 < HEAD_DIM), 1.0, 0.0).astype(_BF16)
    contract_last = (((1,), (1,)), ((), ()))
    for slab in range(KV_DIM // LANES):
        cols = slice(slab * LANES, (slab + 1) * LANES)
        kp_pair = _block_diag_pairs(k_prev[:, cols])
        ko_pair = _block_diag_pairs(k_own[:, cols])
        vp_pair = _block_diag_pairs(v_prev[:, cols])
        vo_pair = _block_diag_pairs(v_own[:, cols])
        for sub in range(2):
            pair_a = 2 * (2 * slab + sub)
            pair_b = pair_a + 1
            qq = jnp.concatenate([q[:, pair_a * LANES:(pair_a + 1) * LANES],
                                  q[:, pair_b * LANES:(pair_b + 1) * LANES]], axis=0).astype(_BF16)
            sp = lax.dot_general(qq, kp_pair[sub], contract_last, preferred_element_type=_F32)
            so = lax.dot_general(qq, ko_pair[sub], contract_last, preferred_element_type=_F32)
            bias = jnp.concatenate([tbl_ref[pair_a, 0:tq, :], tbl_ref[pair_b, 0:tq, :]], axis=0)
            s = jnp.where(from_prev, sp + prev_shift, so) + bias
            sink_even = jnp.where(top, sink_ref[2 * pair_a], sink_ref[2 * pair_b])
            sink_odd = jnp.where(top, sink_ref[2 * pair_a + 1], sink_ref[2 * pair_b + 1])
            m_even = jnp.maximum(jnp.max(s[:, :WINDOW], axis=-1, keepdims=True), sink_even)
            m_odd = jnp.maximum(jnp.max(s[:, WINDOW:], axis=-1, keepdims=True), sink_odd)
            p = jnp.exp(s - jnp.where(col < WINDOW, m_even, m_odd)).astype(_BF16)
            zero = jnp.zeros_like(p)
            o = _dot(jnp.where(from_prev, p, zero), vp_pair[sub]) + _dot(jnp.where(from_prev, zero, p), vo_pair[sub])
            den = _dot(p, head_ones) + jnp.where(low, jnp.exp(sink_even - m_even), jnp.exp(sink_odd - m_odd))
            o = o / den
            store(pair_a, o[:tq])
            store(pair_b, o[tq:])


def _attn_prompt_kernel(sink_ref, q_ref, kp_ref, ko_ref, vp_ref, vo_ref, tbl_ref, o_ref):
    prev_shift = jnp.where(pl.program_id(1) == 0, MASK_VALUE, 0.0).astype(_F32)

    def store(pair, o):
        o_ref[:, pair * LANES:(pair + 1) * LANES] = o.astype(o_ref.dtype)

    _attend(q_ref[...], kp_ref[...], ko_ref[...], vp_ref[...], vo_ref[...], tbl_ref, sink_ref, prev_shift, store)


def _attn_prompt(q, k, v, tbl, sinks, bsz, t):
    nb = t // WINDOW
    own = lambda w: pl.BlockSpec((WINDOW, w), lambda b, i: (b * nb + i, 0))
    prev = lambda w: pl.BlockSpec((WINDOW, w), lambda b, i: (b * nb + jnp.maximum(i - 1, 0), 0))
    return pl.pallas_call(
        _attn_prompt_kernel,
        grid=(bsz, nb),
        in_specs=[pl.BlockSpec(memory_space=pltpu.SMEM), own(Q_DIM), prev(KV_DIM), own(KV_DIM),
                  prev(KV_DIM), own(KV_DIM), pl.BlockSpec(tbl.shape, lambda b, i: (0, 0, 0))],
        out_specs=own(Q_DIM),
        out_shape=jax.ShapeDtypeStruct((bsz * t, Q_DIM), _BF16),
        compiler_params=pltpu.CompilerParams(dimension_semantics=("arbitrary", "arbitrary"),
                                             vmem_limit_bytes=VMEM_LIMIT),
        name="attn_prompt",
    )(sinks, q, k, k, v, v, tbl)


def _attn_sample_kernel(sink_ref, q_ref, kn_ref, vn_ref, ck_ref, cv_ref, tbl_ref, o_ref, *, sb, steps):
    pad = jnp.zeros((WINDOW - steps, KV_DIM), _F32)

    def one_sequence(s, carry):
        def store(pair, o):
            o_ref[s, :, pair * LANES:(pair + 1) * LANES] = o

        k_own = jnp.concatenate([kn_ref[s], pad], axis=0)
        v_own = jnp.concatenate([vn_ref[s], pad], axis=0)
        _attend(q_ref[s], ck_ref[s], k_own, cv_ref[s], v_own, tbl_ref, sink_ref, jnp.float32(0.0), store)
        return carry

    lax.fori_loop(0, sb, one_sequence, 0)


def _attn_sample(q, k_new, v_new, cache_k, cache_v, tbl, sinks, sb):
    nseq, steps, _ = q.shape
    seq = lambda r, w: pl.BlockSpec((sb, r, w), lambda i: (i, 0, 0))
    return pl.pallas_call(
        functools.partial(_attn_sample_kernel, sb=sb, steps=steps),
        grid=(nseq // sb,),
        in_specs=[pl.BlockSpec(memory_space=pltpu.SMEM), seq(steps, Q_DIM), seq(steps, KV_DIM), seq(steps, KV_DIM),
                  seq(WINDOW, KV_DIM), seq(WINDOW, KV_DIM), pl.BlockSpec(tbl.shape, lambda i: (0, 0, 0))],
        out_specs=seq(steps, Q_DIM),
        out_shape=jax.ShapeDtypeStruct((nseq, steps, Q_DIM), _F32),
        compiler_params=pltpu.CompilerParams(dimension_semantics=("arbitrary",),
                                             vmem_limit_bytes=VMEM_LIMIT),
        name="attn_sample",
    )(sinks, q, k_new, v_new, cache_k, cache_v, tbl)


def _lane_min_index(mask, lane):
    return jnp.min(jnp.where(mask, lane, LANES), axis=-1, keepdims=True)


def _finish_kernel(x_ref, conv_ref, o_ref, ga_ref, gb_ref, wa_ref, wo_ref, ng_ref, wr_hi_ref, wr_lo_ref, br_ref,
                   tri_ref, h_ref, hn_ref, route_ref, count_ref, running_ref):
    @pl.when(pl.program_id(0) == 0)
    def _():
        running_ref[...] = jnp.zeros_like(running_ref)

    attn_out = _dot(o_ref[...].astype(_BF16), wa_ref[...])
    merged = ga_ref[...] * conv_ref[...] + gb_ref[...] * attn_out
    h = x_ref[...] + _dot(merged.astype(_BF16), wo_ref[...])
    h_ref[...] = h
    hn = h * lax.rsqrt(jnp.mean(h * h, axis=-1, keepdims=True) + EPS) * ng_ref[...]
    hn_ref[...] = hn

    hi, lo = _split_bf16(hn)
    logits = _dot(hi, wr_hi_ref[...]) + _dot(lo, wr_hi_ref[...]) + _dot(hi, wr_lo_ref[...]) + br_ref[...]
    lane = lax.broadcasted_iota(jnp.int32, logits.shape, 1)
    gmask = lane < N_GROUPS
    gl = jnp.where(gmask, logits, MASK_VALUE)
    gmax = jnp.max(gl, axis=-1, keepdims=True)
    grp = _lane_min_index(gmask & (gl == gmax), lane)
    p_grp = 1.0 / jnp.sum(jnp.where(gmask, jnp.exp(gl - gmax), 0.0), axis=-1, keepdims=True)
    e_lo = N_GROUPS + grp * EXPERTS_PER_GROUP
    emask = (lane >= e_lo) & (lane < e_lo + EXPERTS_PER_GROUP)
    el = jnp.where(emask, logits, MASK_VALUE)
    ex = jnp.where(emask, jnp.exp(el - jnp.max(el, axis=-1, keepdims=True)), 0.0)
    prob = jnp.where(emask, ex / jnp.sum(ex, axis=-1, keepdims=True), -1.0)
    p1 = jnp.max(prob, axis=-1, keepdims=True)
    i1 = _lane_min_index(prob == p1, lane)
    rest = jnp.where(lane == i1, -1.0, prob)
    p2 = jnp.max(rest, axis=-1, keepdims=True)
    i2 = _lane_min_index(rest == p2, lane)
    w1 = p_grp * p1 / (p1 + p2)
    w2 = p_grp * p2 / (p1 + p2)
    e1 = i1 - N_GROUPS
    e2 = i2 - N_GROUPS

    hot1 = lane == e1
    hot2 = lane == e2
    hot = jnp.where(hot1 | hot2, 1.0, 0.0)
    before = _dot(tri_ref[...], hot.astype(_BF16)) + running_ref[...]
    rank1 = jnp.sum(jnp.where(hot1, before, 0.0), axis=-1, keepdims=True)
    rank2 = jnp.sum(jnp.where(hot2, before, 0.0), axis=-1, keepdims=True)
    running_ref[...] += jnp.sum(hot, axis=0, keepdims=True)
    count_ref[...] = jnp.broadcast_to(running_ref[...], count_ref.shape)

    fields = (e1.astype(_F32), e2.astype(_F32), w1, w2, rank1, rank2)
    route = jnp.zeros(logits.shape, _F32)
    for pos, val in enumerate(fields):
        route = jnp.where(lane == pos, val, route)
    route_ref[...] = route


ROUTE_E, ROUTE_W, ROUTE_RANK = 0, 2, 4


def _finish(x, conv_out, o, ga, gb, wa_b, wo_b, ng, wr_hi, wr_lo, br, tm):
    n = x.shape[0]
    tri = jnp.asarray(np.tril(np.ones((tm, tm), np.float32), -1), _BF16)
    row = lambda w: pl.BlockSpec((tm, w), lambda i: (i, 0))
    full = lambda a: pl.BlockSpec(a.shape, lambda i: (0,) * a.ndim)
    return pl.pallas_call(
        _finish_kernel,
        grid=(n // tm,),
        in_specs=[row(D_MODEL), row(D_MODEL), row(Q_DIM), row(D_MODEL), row(D_MODEL),
                  full(wa_b), full(wo_b), full(ng), full(wr_hi), full(wr_lo), full(br), full(tri)],
        out_specs=[row(D_MODEL), row(D_MODEL), row(LANES), pl.BlockSpec((SUBLANES, LANES), lambda i: (0, 0))],
        out_shape=[jax.ShapeDtypeStruct((n, D_MODEL), _F32),
                   jax.ShapeDtypeStruct((n, D_MODEL), _F32),
                   jax.ShapeDtypeStruct((n, LANES), _F32),
                   jax.ShapeDtypeStruct((SUBLANES, LANES), _F32)],
        scratch_shapes=[pltpu.VMEM((1, LANES), _F32)],
        compiler_params=pltpu.CompilerParams(dimension_semantics=("arbitrary",),
                                             vmem_limit_bytes=VMEM_LIMIT),
        name="finish",
    )(x, conv_out, o, ga, gb, wa_b, wo_b, ng, wr_hi, wr_lo, br, tri)


def _dest_kernel(route_ref, starts_ref, dest_ref):
    route = route_ref[...]
    lane = lax.broadcasted_iota(jnp.int32, route.shape, 1)
    out = jnp.zeros(route.shape, jnp.int32)
    for j in range(TOP_K):
        e = route[:, ROUTE_E + j:ROUTE_E + j + 1].astype(jnp.int32)
        start = jnp.sum(jnp.where(lane == e, starts_ref[...], 0.0), axis=-1, keepdims=True)
        d = (start + route[:, ROUTE_RANK + j:ROUTE_RANK + j + 1]).astype(jnp.int32)
        out = jnp.where(lane == j, d, out)
    dest_ref[...] = out


def _dest(route, starts_row, tm):
    n = route.shape[0]
    row = pl.BlockSpec((tm, LANES), lambda i: (i, 0))
    return pl.pallas_call(
        _dest_kernel,
        grid=(n // tm,),
        in_specs=[row, pl.BlockSpec((1, LANES), lambda i: (0, 0))],
        out_specs=row,
        out_shape=jax.ShapeDtypeStruct((n, LANES), jnp.int32),
        name="dest",
    )(route, starts_row)


SC_CORES = 2
SC_SUBCORES = 16
SC_WORKERS = SC_CORES * SC_SUBCORES
SC_CHUNK = 32


def _sc_gather_rows(table, idx):
    b = idx.shape[0]
    d = table.shape[1]
    per_worker = b // SC_WORKERS
    n_chunks = per_worker // SC_CHUNK
    assert per_worker * SC_WORKERS == b and n_chunks * SC_CHUNK == per_worker, (b,)
    mesh = plsc.VectorSubcoreMesh(core_axis_name="c", subcore_axis_name="s",
                                  num_cores=SC_CORES, num_subcores=SC_SUBCORES)

    @functools.partial(
        pl.kernel, mesh=mesh,
        out_type=jax.ShapeDtypeStruct((b, d), table.dtype),
        scratch_types=[pltpu.VMEM((SC_CHUNK,), jnp.int32), pltpu.VMEM((SC_CHUNK, d), table.dtype),
                       pltpu.SemaphoreType.DMA],
        name="sc_gather_rows",
    )
    def gather(table_hbm, idx_hbm, out_hbm, idx_v, rows_v, sem):
        worker = lax.axis_index("s") * SC_CORES + lax.axis_index("c")

        @pl.loop(0, n_chunks)
        def _(c):
            base = pl.multiple_of(worker * per_worker + c * SC_CHUNK, SC_CHUNK)
            pltpu.sync_copy(idx_hbm.at[pl.ds(base, SC_CHUNK)], idx_v)
            pltpu.async_copy(table_hbm.at[idx_v], rows_v, sem).wait()
            pltpu.sync_copy(rows_v, out_hbm.at[pl.ds(base, SC_CHUNK)])

    return gather(table, idx)


def _sc_scatter_rows(src, idx):
    n, d = src.shape
    b = idx.shape[0]
    per_worker = b // SC_WORKERS
    n_chunks = per_worker // SC_CHUNK
    assert per_worker * SC_WORKERS == b and n_chunks * SC_CHUNK == per_worker and n % per_worker == 0, (b, n)
    mesh = plsc.VectorSubcoreMesh(core_axis_name="c", subcore_axis_name="s",
                                  num_cores=SC_CORES, num_subcores=SC_SUBCORES)

    @functools.partial(
        pl.kernel, mesh=mesh,
        out_type=jax.ShapeDtypeStruct((b, d), src.dtype),
        scratch_types=[pltpu.VMEM((SC_CHUNK,), jnp.int32), pltpu.VMEM((SC_CHUNK, d), src.dtype)],
        name="sc_scatter_rows",
    )
    def scatter(src_hbm, idx_hbm, out_hbm, idx_v, rows_v):
        worker = lax.axis_index("s") * SC_CORES + lax.axis_index("c")

        @pl.loop(0, n_chunks)
        def _(c):
            base = pl.multiple_of(worker * per_worker + c * SC_CHUNK, SC_CHUNK)
            src_base = pl.multiple_of(lax.rem(base, n), SC_CHUNK)
            pltpu.sync_copy(idx_hbm.at[pl.ds(base, SC_CHUNK)], idx_v)
            pltpu.sync_copy(src_hbm.at[pl.ds(src_base, SC_CHUNK)], rows_v)
            pltpu.sync_copy(rows_v, out_hbm.at[idx_v])

    return scatter(src, idx)


def _expert_kernel(blk_ref, exp_ref, lo_ref, hi_ref, x_ref, wg_ref, wu_ref, wd_ref, yb_ref):
    del blk_ref, exp_ref
    k = pl.program_id(0)
    lo, hi = lo_ref[k], hi_ref[k]

    @pl.when(hi > lo)
    def _():
        xb = x_ref[...].astype(_BF16)
        g = _dot(xb, wg_ref[0])
        u = _dot(xb, wu_ref[0])
        hid = g * jax.nn.sigmoid(g) * u
        y = _dot(hid.astype(_BF16), wd_ref[0])
        r = lax.broadcasted_iota(jnp.int32, y.shape, 0)
        pltpu.store(yb_ref, y, mask=(r >= lo) & (r < hi))


def _experts(items, xs, wg_b, wu_b, wd_b):
    n_items = items[0].shape[0]
    wspec = lambda a: pl.BlockSpec((1,) + a.shape[1:], lambda k, blk, exp, lo, hi: (exp[k], 0, 0))
    rows = pl.BlockSpec((MOE_ROWS, D_MODEL), lambda k, blk, exp, lo, hi: (blk[k], 0))
    grid_spec = pltpu.PrefetchScalarGridSpec(
        num_scalar_prefetch=4,
        grid=(n_items,),
        in_specs=[rows, wspec(wg_b), wspec(wu_b), wspec(wd_b)],
        out_specs=rows,
    )
    return pl.pallas_call(
        _expert_kernel,
        grid_spec=grid_spec,
        out_shape=jax.ShapeDtypeStruct(xs.shape, _F32),
        compiler_params=pltpu.CompilerParams(dimension_semantics=("arbitrary",),
                                             vmem_limit_bytes=VMEM_LIMIT),
        name="experts",
    )(*items, xs, wg_b, wu_b, wd_b)


def _combine_kernel(h_ref, route_ref, g0_ref, g1_ref, y_ref):
    route = route_ref[...]
    y_ref[...] = h_ref[...] + route[:, 2:3] * g0_ref[...] + route[:, 3:4] * g1_ref[...]


def _combine(h, route, g, tm):
    n = h.shape[0]
    nt = n // tm
    row = lambda w: pl.BlockSpec((tm, w), lambda i: (i, 0))
    return pl.pallas_call(
        _combine_kernel,
        grid=(nt,),
        in_specs=[row(D_MODEL), row(LANES), row(D_MODEL), pl.BlockSpec((tm, D_MODEL), lambda i: (nt + i, 0))],
        out_specs=row(D_MODEL),
        out_shape=jax.ShapeDtypeStruct((n, D_MODEL), _F32),
        compiler_params=pltpu.CompilerParams(dimension_semantics=("arbitrary",),
                                             vmem_limit_bytes=VMEM_LIMIT),
        name="combine",
    )(h, route, g, g)


def _work_items(counts, n_pairs):
    n_blocks = n_pairs // MOE_ROWS
    starts = jnp.cumsum(counts) - counts
    cuts = jnp.sort(jnp.concatenate([jnp.arange(n_blocks, dtype=jnp.int32) * MOE_ROWS, starts]))
    ends = jnp.concatenate([cuts[1:], jnp.full((1,), n_pairs, jnp.int32)])
    blk = jnp.minimum(cuts // MOE_ROWS, n_blocks - 1)
    expert = jnp.clip(jnp.sum(starts[None, :] <= cuts[:, None], axis=1) - 1, 0, N_EXPERTS - 1).astype(jnp.int32)
    return starts, (blk, expert, cuts - blk * MOE_ROWS, ends - blk * MOE_ROWS)


def _moe(h, hn, route, counts_rows, wg_b, wu_b, wd_b, tm):
    n = h.shape[0]
    n_pairs = n * TOP_K
    counts = counts_rows[0, :N_EXPERTS].astype(jnp.int32)
    starts, items = _work_items(counts, n_pairs)
    starts_row = jnp.zeros((1, LANES), _F32).at[0, :N_EXPERTS].set(starts.astype(_F32))
    dest = _dest(route, starts_row, tm)[:, :TOP_K].T.reshape(n_pairs)
    xs = _sc_scatter_rows(hn, dest)
    yb = _experts(items, xs, wg_b, wu_b, wd_b)
    g = _sc_gather_rows(yb, dest)
    return _combine(h, route, g, tm)


def kernel(x_prompt, x_sample, state_conv, cache_k, cache_v, norm_attn_g, w_in, q_norm_g, k_norm_g, rel_bias, attn_sinks, w_dw, b_dw, conv_ln_g, conv_ln_b, w_conv_out, b_conv_out, w_attn_out, w_out, norm_ffn_g, w_grp, b_grp, w_router, b_router, w_gate, w_up, w_down):
    bsz, t, _ = x_prompt.shape
    nseq, steps, _ = x_sample.shape
    row = lambda a: a.reshape(1, -1).astype(_F32)

    w_in_b = w_in.astype(_BF16)
    wco_b = w_conv_out.astype(_BF16)
    wa_b = w_attn_out.astype(_BF16)
    wo_b = w_out.astype(_BF16)
    wg_b, wu_b, wd_b = w_gate.astype(_BF16), w_up.astype(_BF16), w_down.astype(_BF16)
    qg = row(jnp.tile(q_norm_g, N_HEADS)) * (HEAD_DIM ** -0.5)
    kg = row(jnp.tile(k_norm_g, N_KV_HEADS))
    group_of_lane = np.arange(Q_DIM) // HEAD_DIM
    gsum = jnp.asarray(group_of_lane[:, None] == np.arange(LANES)[None, :], _BF16)
    gexp = jnp.asarray(np.arange(LANES)[:, None] == group_of_lane[None, :], _BF16)
    w_rt = jnp.zeros((D_MODEL, LANES), _F32).at[:, :N_GROUPS].set(w_grp).at[:, N_GROUPS:N_GROUPS + N_EXPERTS].set(w_router)
    wr_hi = w_rt.astype(_BF16)
    wr_lo = (w_rt - wr_hi.astype(_F32)).astype(_BF16)
    b_rt = jnp.zeros((1, LANES), _F32).at[0, :N_GROUPS].set(b_grp).at[0, N_GROUPS:N_GROUPS + N_EXPERTS].set(b_router)
    tbl = _bias_tables(rel_bias)
    conv_params = (w_dw, row(b_dw), row(conv_ln_g), row(conv_ln_b), wco_b, row(b_conv_out))

    def finish_and_moe(x2d, conv_out, o, ga, gb, tm):
        h, hn, route, counts = _finish(x2d, conv_out, o, ga, gb, wa_b, wo_b, row(norm_ffn_g), wr_hi, wr_lo, b_rt, tm)
        return _moe(h, hn, route, counts, wg_b, wu_b, wd_b, tm)

    xp = x_prompt.reshape(bsz * t, D_MODEL)
    glu, q, k, v, ga, gb = _inproj(xp, row(norm_attn_g), w_in_b, qg, kg, gsum, gexp, _BF16, 256)
    conv_out = _conv_prompt(glu, bsz, t, *conv_params, 256)
    o = _attn_prompt(q, k, v, tbl, attn_sinks, bsz, t)
    y_prompt = finish_and_moe(xp, conv_out, o, ga, gb, 256).reshape(bsz, t, D_MODEL)
    glu3 = glu.reshape(bsz, t, D_CONV)
    state_conv_prompt = glu3[:, t - (CONV_WIDTH - 1):]
    tail = lambda a: a.reshape(bsz, t, KV_DIM)[:, t - WINDOW:].reshape(bsz, WINDOW, N_KV_HEADS, HEAD_DIM)
    cache_k_prompt, cache_v_prompt = tail(k), tail(v)

    xs = x_sample.reshape(nseq * steps, D_MODEL)
    glu, q, k, v, ga, gb = _inproj(xs, row(norm_attn_g), w_in_b, qg, kg, gsum, gexp, _F32, 256)
    glu3 = glu.reshape(nseq, steps, D_CONV)
    hist = jnp.concatenate([state_conv, glu3], axis=1)
    conv_out = _conv_sample(hist.transpose(1, 0, 2), *conv_params, 64)
    conv_out = conv_out.transpose(1, 0, 2).reshape(nseq * steps, D_MODEL)
    k3 = k.reshape(nseq, steps, KV_DIM)
    v3 = v.reshape(nseq, steps, KV_DIM)
    o = _attn_sample(q.reshape(nseq, steps, Q_DIM), k3, v3, cache_k.reshape(nseq, WINDOW, KV_DIM),
                     cache_v.reshape(nseq, WINDOW, KV_DIM), tbl, attn_sinks, 8)
    y_sample = finish_and_moe(xs, conv_out, o.reshape(nseq * steps, Q_DIM), ga, gb, 256).reshape(nseq, steps, D_MODEL)
    state_conv_sample = hist[:, steps:]
    cache_k_sample = jnp.concatenate([cache_k, k3.reshape(nseq, steps, N_KV_HEADS, HEAD_DIM)], axis=1)[:, steps:]
    cache_v_sample = jnp.concatenate([cache_v, v3.reshape(nseq, steps, N_KV_HEADS, HEAD_DIM)], axis=1)[:, steps:]

    return (y_prompt, y_sample, state_conv_prompt, cache_k_prompt, cache_v_prompt,
            state_conv_sample, cache_k_sample, cache_v_sample)
```

```python
import functools
import math

import numpy as np
import jax
import jax.numpy as jnp
from jax import lax
from jax.experimental import pallas as pl
from jax.experimental.pallas import tpu as pltpu
from jax.experimental.pallas import tpu_sc as plsc

D_MODEL = 1024
N_HEADS = 16
HEAD_DIM = 64
N_KV_HEADS = 4
WINDOW = 128
Q_DIM = N_HEADS * HEAD_DIM
KV_DIM = N_KV_HEADS * HEAD_DIM
N_BUCKETS = 32
MAX_EXACT = N_BUCKETS // 2
MAX_DISTANCE = 128
D_CONV = D_MODEL
CONV_WIDTH = 31
N_GROUPS = 4
EXPERTS_PER_GROUP = 8
N_EXPERTS = N_GROUPS * EXPERTS_PER_GROUP
TOP_K = 2
D_EXPERT = 256
EPS = 1e-6

LANES = 128
SUBLANES = 8
N_PAIRS = N_HEADS // 2
MOE_ROWS = 256
MASK_VALUE = -1e30
VMEM_LIMIT = 48 * 1024 * 1024

_F32 = jnp.float32
_BF16 = jnp.bfloat16


def _dot(a, b):
    return jnp.dot(a, b, preferred_element_type=_F32)


def _split_bf16(x):
    hi = x.astype(_BF16)
    lo = (x - hi.astype(_F32)).astype(_BF16)
    return hi, lo


def _head_rms_scale(z, gsum, gexp):
    hi, lo = _split_bf16(z * z)
    ssum = _dot(hi, gsum) + _dot(lo, gsum)
    r = lax.rsqrt(ssum * (1.0 / HEAD_DIM) + EPS)
    rhi, rlo = _split_bf16(r)
    return _dot(rhi, gexp) + _dot(rlo, gexp)


def _inproj_kernel(x_ref, g_ref, w_ref, qg_ref, kg_ref, gsum_ref, gexp_ref,
                   glu_ref, q_ref, k_ref, v_ref, ga_ref, gb_ref):
    x = x_ref[...]
    xn = x * lax.rsqrt(jnp.mean(x * x, axis=-1, keepdims=True) + EPS) * g_ref[...]
    xb = xn.astype(_BF16)

    def seg(lo, width):
        return _dot(xb, w_ref[:, lo:lo + width])

    a = seg(0, D_CONV)
    b = seg(D_CONV, D_CONV)
    glu_ref[...] = a * jax.nn.sigmoid(b)
    off = 2 * D_CONV
    q = seg(off, Q_DIM)
    q_ref[...] = (q * _head_rms_scale(q, gsum_ref[...], gexp_ref[...]) * qg_ref[...]).astype(q_ref.dtype)
    off += Q_DIM
    k = seg(off, KV_DIM)
    k_ref[...] = k * _head_rms_scale(k, gsum_ref[:KV_DIM, :], gexp_ref[:, :KV_DIM]) * kg_ref[...]
    off += KV_DIM
    v_ref[...] = seg(off, KV_DIM)
    off += KV_DIM
    ga_ref[...] = jax.nn.sigmoid(seg(off, D_MODEL))
    off += D_MODEL
    gb_ref[...] = jax.nn.sigmoid(seg(off, D_MODEL))


def _inproj(x, g, w_in_b, qg, kg, gsum, gexp, q_dtype, tm):
    n = x.shape[0]
    in_dim = w_in_b.shape[1]
    row = lambda w: pl.BlockSpec((tm, w), lambda i: (i, 0))
    full = lambda a: pl.BlockSpec(a.shape, lambda i: (0,) * a.ndim)
    return pl.pallas_call(
        _inproj_kernel,
        grid=(n // tm,),
        in_specs=[row(D_MODEL), full(g), full(w_in_b), full(qg), full(kg), full(gsum), full(gexp)],
        out_specs=[row(D_CONV), row(Q_DIM), row(KV_DIM), row(KV_DIM), row(D_MODEL), row(D_MODEL)],
        out_shape=[jax.ShapeDtypeStruct((n, D_CONV), _F32),
                   jax.ShapeDtypeStruct((n, Q_DIM), q_dtype),
                   jax.ShapeDtypeStruct((n, KV_DIM), _F32),
                   jax.ShapeDtypeStruct((n, KV_DIM), _F32),
                   jax.ShapeDtypeStruct((n, D_MODEL), _F32),
                   jax.ShapeDtypeStruct((n, D_MODEL), _F32)],
        compiler_params=pltpu.CompilerParams(dimension_semantics=("arbitrary",),
                                             vmem_limit_bytes=VMEM_LIMIT),
        name="inproj",
    )(x, g, w_in_b, qg, kg, gsum, gexp)


def _ln_swish_project(y, lng_ref, lnb_ref, wo_ref, bo_ref):
    mu = jnp.mean(y, axis=-1, keepdims=True)
    yc = y - mu
    var = jnp.mean(yc * yc, axis=-1, keepdims=True)
    z = yc * lax.rsqrt(var + EPS) * lng_ref[...] + lnb_ref[...]
    z = z * jax.nn.sigmoid(z)
    return _dot(z.astype(_BF16), wo_ref[...]) + bo_ref[...]


HALO = 32
CONV_STEPS = 16
CH_TILES = D_CONV // LANES


def _conv_prompt_kernel(glu_ref, w8_ref, b8_ref, lng_ref, lnb_ref, wo_ref, bo_ref,
                        out_ref, hist_ref, y_ref, *, tm):
    i = pl.program_id(1)

    @pl.when(i == 0)
    def _():
        hist_ref[0:HALO * CH_TILES, :] = jnp.zeros((HALO * CH_TILES, LANES), _F32)

    @pl.when(i > 0)
    def _():
        hist_ref[0:HALO * CH_TILES, :] = hist_ref[tm * CH_TILES:(tm + HALO) * CH_TILES, :]

    for c in range(CH_TILES):
        hist_ref[pl.ds(HALO * CH_TILES + c, tm, stride=CH_TILES), :] = glu_ref[:, c * LANES:(c + 1) * LANES]

    first = HALO - (CONV_WIDTH - 1)
    span = CONV_STEPS + CONV_WIDTH - 1

    def chunk(ci, carry):
        t0 = ci * CONV_STEPS
        x = hist_ref[pl.ds(pl.multiple_of((t0 + first) * CH_TILES, CH_TILES), span * CH_TILES), :]
        x = x.reshape(span, CH_TILES, LANES)
        acc = jnp.broadcast_to(b8_ref[...][None], (CONV_STEPS, CH_TILES, LANES))
        for j in range(CONV_WIDTH):
            acc = acc + x[j:j + CONV_STEPS] * w8_ref[j][None]
        y_ref[pl.ds(pl.multiple_of(t0 * CH_TILES, CH_TILES), CONV_STEPS * CH_TILES), :] = (
            acc.reshape(CONV_STEPS * CH_TILES, LANES))
        return carry

    lax.fori_loop(0, tm // CONV_STEPS, chunk, 0)
    y = jnp.concatenate([y_ref[pl.ds(c, tm, stride=CH_TILES), :] for c in range(CH_TILES)], axis=1)
    out_ref[...] = _ln_swish_project(y, lng_ref, lnb_ref, wo_ref, bo_ref)


def _conv_prompt(glu, bsz, t, w_dw, b_dw, lng, lnb, wo_b, bo, tm):
    assert CH_TILES == SUBLANES
    nt = t // tm
    w8 = w_dw.reshape(CONV_WIDTH, CH_TILES, LANES)
    b8 = b_dw.reshape(CH_TILES, LANES)
    full = lambda a: pl.BlockSpec(a.shape, lambda b, i: (0,) * a.ndim)
    row = pl.BlockSpec((tm, D_CONV), lambda b, i: (b * nt + i, 0))
    return pl.pallas_call(
        functools.partial(_conv_prompt_kernel, tm=tm),
        grid=(bsz, nt),
        in_specs=[row, full(w8), full(b8), full(lng), full(lnb), full(wo_b), full(bo)],
        out_specs=pl.BlockSpec((tm, D_MODEL), lambda b, i: (b * nt + i, 0)),
        out_shape=jax.ShapeDtypeStruct((bsz * t, D_MODEL), _F32),
        scratch_shapes=[pltpu.VMEM(((tm + HALO) * CH_TILES, LANES), _F32), pltpu.VMEM((tm * CH_TILES, LANES), _F32)],
        compiler_params=pltpu.CompilerParams(dimension_semantics=("arbitrary", "arbitrary"),
                                             vmem_limit_bytes=VMEM_LIMIT),
        name="conv_prompt",
    )(glu, w8, b8, lng, lnb, wo_b, bo)


def _conv_sample_kernel(hist_ref, wdw_ref, bdw_ref, lng_ref, lnb_ref, wo_ref, bo_ref, out_ref, *, steps):
    for t in range(steps):
        acc = hist_ref[t] * wdw_ref[0:1, :]
        for j in range(1, CONV_WIDTH):
            acc = acc + hist_ref[t + j] * wdw_ref[j:j + 1, :]
        out_ref[t] = _ln_swish_project(acc + bdw_ref[...], lng_ref, lnb_ref, wo_ref, bo_ref)


def _conv_sample(hist, w_dw, b_dw, lng, lnb, wo_b, bo, sb):
    rows, nseq, _ = hist.shape
    steps = rows - (CONV_WIDTH - 1)
    full = lambda a: pl.BlockSpec(a.shape, lambda i: (0,) * a.ndim)
    return pl.pallas_call(
        functools.partial(_conv_sample_kernel, steps=steps),
        grid=(nseq // sb,),
        in_specs=[pl.BlockSpec((rows, sb, D_CONV), lambda i: (0, i, 0)),
                  full(w_dw), full(b_dw), full(lng), full(lnb), full(wo_b), full(bo)],
        out_specs=pl.BlockSpec((steps, sb, D_MODEL), lambda i: (0, i, 0)),
        out_shape=jax.ShapeDtypeStruct((steps, nseq, D_MODEL), _F32),
        compiler_params=pltpu.CompilerParams(dimension_semantics=("arbitrary",),
                                             vmem_limit_bytes=VMEM_LIMIT),
        name="conv_sample",
    )(hist, w_dw, b_dw, lng, lnb, wo_b, bo)


def _bucket_map():
    i = np.arange(WINDOW)[:, None]
    j = np.arange(WINDOW)[None, :]
    n = (i - j) % WINDOW
    nf = np.maximum(n, 1).astype(np.float32)
    large = MAX_EXACT + (np.log(nf / np.float32(MAX_EXACT)) / np.float32(math.log(MAX_DISTANCE / MAX_EXACT))
                         * np.float32(N_BUCKETS - MAX_EXACT)).astype(np.int32)
    return np.where(n < MAX_EXACT, n, np.minimum(large, N_BUCKETS - 1)).astype(np.int32)


def _bias_table_kernel(rb_ref, bm_ref, tbl_ref):
    p = pl.program_id(0)
    bm = bm_ref[...]
    for half in range(2):
        h = 2 * p + half
        t = jnp.zeros(bm.shape, _F32)
        for b in range(N_BUCKETS):
            t = jnp.where(bm == b, rb_ref[b, h], t)
        tbl_ref[0, :, half * WINDOW:(half + 1) * WINDOW] = t


def _bias_tables(rel_bias):
    bm = jnp.asarray(_bucket_map())
    return pl.pallas_call(
        _bias_table_kernel,
        grid=(N_PAIRS,),
        in_specs=[pl.BlockSpec(memory_space=pltpu.SMEM), pl.BlockSpec(bm.shape, lambda p: (0, 0))],
        out_specs=pl.BlockSpec((1, WINDOW, 2 * WINDOW), lambda p: (p, 0, 0)),
        out_shape=jax.ShapeDtypeStruct((N_PAIRS, WINDOW, 2 * WINDOW), _F32),
        name="bias_tables",
    )(rel_bias, bm)


def _block_diag_pairs(slab):
    low = lax.broadcasted_iota(jnp.int32, slab.shape, 1) < HEAD_DIM
    swapped = pltpu.roll(slab, HEAD_DIM, axis=1)
    zero = jnp.zeros_like(slab)
    first = jnp.concatenate([jnp.where(low, slab, zero), jnp.where(low, zero, swapped)], axis=0)
    second = jnp.concatenate([jnp.where(low, swapped, zero), jnp.where(low, zero, slab)], axis=0)
    return first.astype(_BF16), second.astype(_BF16)


def _attend(q, k_prev, k_own, v_prev, v_own, tbl_ref, sink_ref, prev_shift, store):
    tq = q.shape[0]
    rows = 2 * tq
    row = lax.broadcasted_iota(jnp.int32, (rows, 2 * WINDOW), 0)
    col = lax.broadcasted_iota(jnp.int32, (rows, 2 * WINDOW), 1)
    from_prev = (col & (WINDOW - 1)) > jnp.where(row >= tq, row - tq, row)
    top = lax.broadcasted_iota(jnp.int32, (rows, 1), 0) < tq
    low = lax.broadcasted_iota(jnp.int32, (rows, LANES), 1) < HEAD_DIM
    r2 = lax.broadcasted_iota(jnp.int32, (2 * WINDOW, LANES), 0)
    c2 = lax.broadcasted_iota(jnp.int32, (2 * WINDOW, LANES), 1)
    head_ones = jnp.where((r2 < WINDOW) == (c2 < HEAD_DIM), 1.0, 0.0).astype(_BF16)
    contract_last = (((1,), (1,)), ((), ()))
    for slab in range(KV_DIM // LANES):
        cols = slice(slab * LANES, (slab + 1) * LANES)
        kp_pair = _block_diag_pairs(k_prev[:, cols])
        ko_pair = _block_diag_pairs(k_own[:, cols])
        vp_pair = _block_diag_pairs(v_prev[:, cols])
        vo_pair = _block_diag_pairs(v_own[:, cols])
        for sub in range(2):
            pair_a = 2 * (2 * slab + sub)
            pair_b = pair_a + 1
            qq = jnp.concatenate([q[:, pair_a * LANES:(pair_a + 1) * LANES],
                                  q[:, pair_b * LANES:(pair_b + 1) * LANES]], axis=0).astype(_BF16)
            sp = lax.dot_general(qq, kp_pair[sub], contract_last, preferred_element_type=_F32)
            so = lax.dot_general(qq, ko_pair[sub], contract_last, preferred_element_type=_F32)
            bias = jnp.concatenate([tbl_ref[pair_a, 0:tq, :], tbl_ref[pair_b, 0:tq, :]], axis=0)
            s = jnp.where(from_prev, sp + prev_shift, so) + bias
            sink_even = jnp.where(top, sink_ref[2 * pair_a], sink_ref[2 * pair_b])
            sink_odd = jnp.where(top, sink_ref[2 * pair_a + 1], sink_ref[2 * pair_b + 1])
            m_even = jnp.maximum(jnp.max(s[:, :WINDOW], axis=-1, keepdims=True), sink_even)
            m_odd = jnp.maximum(jnp.max(s[:, WINDOW:], axis=-1, keepdims=True), sink_odd)
            p = jnp.exp(s - jnp.where(col < WINDOW, m_even, m_odd)).astype(_BF16)
            zero = jnp.zeros_like(p)
            o = _dot(jnp.where(from_prev, p, zero), vp_pair[sub]) + _dot(jnp.where(from_prev, zero, p), vo_pair[sub])
            den = _dot(p, head_ones) + jnp.where(low, jnp.exp(sink_even - m_even), jnp.exp(sink_odd - m_odd))
            o = o / den
            store(pair_a, o[:tq])
            store(pair_b, o[tq:])


def _attn_prompt_kernel(sink_ref, q_ref, kp_ref, ko_ref, vp_ref, vo_ref, tbl_ref, o_ref):
    prev_shift = jnp.where(pl.program_id(1) == 0, MASK_VALUE, 0.0).astype(_F32)

    def store(pair, o):
        o_ref[:, pair * LANES:(pair + 1) * LANES] = o.astype(o_ref.dtype)

    _attend(q_ref[...], kp_ref[...], ko_ref[...], vp_ref[...], vo_ref[...], tbl_ref, sink_ref, prev_shift, store)


def _attn_prompt(q, k, v, tbl, sinks, bsz, t):
    nb = t // WINDOW
    own = lambda w: pl.BlockSpec((WINDOW, w), lambda b, i: (b * nb + i, 0))
    prev = lambda w: pl.BlockSpec((WINDOW, w), lambda b, i: (b * nb + jnp.maximum(i - 1, 0), 0))
    return pl.pallas_call(
        _attn_prompt_kernel,
        grid=(bsz, nb),
        in_specs=[pl.BlockSpec(memory_space=pltpu.SMEM), own(Q_DIM), prev(KV_DIM), own(KV_DIM),
                  prev(KV_DIM), own(KV_DIM), pl.BlockSpec(tbl.shape, lambda b, i: (0, 0, 0))],
        out_specs=own(Q_DIM),
        out_shape=jax.ShapeDtypeStruct((bsz * t, Q_DIM), _BF16),
        compiler_params=pltpu.CompilerParams(dimension_semantics=("arbitrary", "arbitrary"),
                                             vmem_limit_bytes=VMEM_LIMIT),
        name="attn_prompt",
    )(sinks, q, k, k, v, v, tbl)


def _attn_sample_kernel(sink_ref, q_ref, kn_ref, vn_ref, ck_ref, cv_ref, tbl_ref, o_ref, *, sb, steps):
    pad = jnp.zeros((WINDOW - steps, KV_DIM), _F32)

    def one_sequence(s, carry):
        def store(pair, o):
            o_ref[s, :, pair * LANES:(pair + 1) * LANES] = o

        k_own = jnp.concatenate([kn_ref[s], pad], axis=0)
        v_own = jnp.concatenate([vn_ref[s], pad], axis=0)
        _attend(q_ref[s], ck_ref[s], k_own, cv_ref[s], v_own, tbl_ref, sink_ref, jnp.float32(0.0), store)
        return carry

    lax.fori_loop(0, sb, one_sequence, 0)


def _attn_sample(q, k_new, v_new, cache_k, cache_v, tbl, sinks, sb):
    nseq, steps, _ = q.shape
    seq = lambda r, w: pl.BlockSpec((sb, r, w), lambda i: (i, 0, 0))
    return pl.pallas_call(
        functools.partial(_attn_sample_kernel, sb=sb, steps=steps),
        grid=(nseq // sb,),
        in_specs=[pl.BlockSpec(memory_space=pltpu.SMEM), seq(steps, Q_DIM), seq(steps, KV_DIM), seq(steps, KV_DIM),
                  seq(WINDOW, KV_DIM), seq(WINDOW, KV_DIM), pl.BlockSpec(tbl.shape, lambda i: (0, 0, 0))],
        out_specs=seq(steps, Q_DIM),
        out_shape=jax.ShapeDtypeStruct((nseq, steps, Q_DIM), _F32),
        compiler_params=pltpu.CompilerParams(dimension_semantics=("arbitrary",),
                                             vmem_limit_bytes=VMEM_LIMIT),
        name="attn_sample",
    )(sinks, q, k_new, v_new, cache_k, cache_v, tbl)


def _lane_min_index(mask, lane):
    return jnp.min(jnp.where(mask, lane, LANES), axis=-1, keepdims=True)


def _finish_kernel(x_ref, conv_ref, o_ref, ga_ref, gb_ref, wa_ref, wo_ref, ng_ref, wr_hi_ref, wr_lo_ref, br_ref,
                   tri_ref, h_ref, hn_ref, route_ref, count_ref, running_ref):
    @pl.when(pl.program_id(0) == 0)
    def _():
        running_ref[...] = jnp.zeros_like(running_ref)

    attn_out = _dot(o_ref[...].astype(_BF16), wa_ref[...])
    merged = ga_ref[...] * conv_ref[...] + gb_ref[...] * attn_out
    h = x_ref[...] + _dot(merged.astype(_BF16), wo_ref[...])
    h_ref[...] = h
    hn = h * lax.rsqrt(jnp.mean(h * h, axis=-1, keepdims=True) + EPS) * ng_ref[...]
    hn_ref[...] = hn

    hi, lo = _split_bf16(hn)
    logits = _dot(hi, wr_hi_ref[...]) + _dot(lo, wr_hi_ref[...]) + _dot(hi, wr_lo_ref[...]) + br_ref[...]
    lane = lax.broadcasted_iota(jnp.int32, logits.shape, 1)
    gmask = lane < N_GROUPS
    gl = jnp.where(gmask, logits, MASK_VALUE)
    gmax = jnp.max(gl, axis=-1, keepdims=True)
    grp = _lane_min_index(gmask & (gl == gmax), lane)
    p_grp = 1.0 / jnp.sum(jnp.where(gmask, jnp.exp(gl - gmax), 0.0), axis=-1, keepdims=True)
    e_lo = N_GROUPS + grp * EXPERTS_PER_GROUP
    emask = (lane >= e_lo) & (lane < e_lo + EXPERTS_PER_GROUP)
    el = jnp.where(emask, logits, MASK_VALUE)
    ex = jnp.where(emask, jnp.exp(el - jnp.max(el, axis=-1, keepdims=True)), 0.0)
    prob = jnp.where(emask, ex / jnp.sum(ex, axis=-1, keepdims=True), -1.0)
    p1 = jnp.max(prob, axis=-1, keepdims=True)
    i1 = _lane_min_index(prob == p1, lane)
    rest = jnp.where(lane == i1, -1.0, prob)
    p2 = jnp.max(rest, axis=-1, keepdims=True)
    i2 = _lane_min_index(rest == p2, lane)
    w1 = p_grp * p1 / (p1 + p2)
    w2 = p_grp * p2 / (p1 + p2)
    e1 = i1 - N_GROUPS
    e2 = i2 - N_GROUPS

    hot1 = lane == e1
    hot2 = lane == e2
    hot = jnp.where(hot1 | hot2, 1.0, 0.0)
    before = _dot(tri_ref[...], hot.astype(_BF16)) + running_ref[...]
    rank1 = jnp.sum(jnp.where(hot1, before, 0.0), axis=-1, keepdims=True)
    rank2 = jnp.sum(jnp.where(hot2, before, 0.0), axis=-1, keepdims=True)
    running_ref[...] += jnp.sum(hot, axis=0, keepdims=True)
    count_ref[...] = jnp.broadcast_to(running_ref[...], count_ref.shape)

    fields = (e1.astype(_F32), e2.astype(_F32), w1, w2, rank1, rank2)
    route = jnp.zeros(logits.shape, _F32)
    for pos, val in enumerate(fields):
        route = jnp.where(lane == pos, val, route)
    route_ref[...] = route


ROUTE_E, ROUTE_W, ROUTE_RANK = 0, 2, 4


def _finish(x, conv_out, o, ga, gb, wa_b, wo_b, ng, wr_hi, wr_lo, br, tm):
    n = x.shape[0]
    tri = jnp.asarray(np.tril(np.ones((tm, tm), np.float32), -1), _BF16)
    row = lambda w: pl.BlockSpec((tm, w), lambda i: (i, 0))
    full = lambda a: pl.BlockSpec(a.shape, lambda i: (0,) * a.ndim)
    return pl.pallas_call(
        _finish_kernel,
        grid=(n // tm,),
        in_specs=[row(D_MODEL), row(D_MODEL), row(Q_DIM), row(D_MODEL), row(D_MODEL),
                  full(wa_b), full(wo_b), full(ng), full(wr_hi), full(wr_lo), full(br), full(tri)],
        out_specs=[row(D_MODEL), row(D_MODEL), row(LANES), pl.BlockSpec((SUBLANES, LANES), lambda i: (0, 0))],
        out_shape=[jax.ShapeDtypeStruct((n, D_MODEL), _F32),
                   jax.ShapeDtypeStruct((n, D_MODEL), _F32),
                   jax.ShapeDtypeStruct((n, LANES), _F32),
                   jax.ShapeDtypeStruct((SUBLANES, LANES), _F32)],
        scratch_shapes=[pltpu.VMEM((1, LANES), _F32)],
        compiler_params=pltpu.CompilerParams(dimension_semantics=("arbitrary",),
                                             vmem_limit_bytes=VMEM_LIMIT),
        name="finish",
    )(x, conv_out, o, ga, gb, wa_b, wo_b, ng, wr_hi, wr_lo, br, tri)


def _dest_kernel(route_ref, starts_ref, dest_ref):
    route = route_ref[...]
    lane = lax.broadcasted_iota(jnp.int32, route.shape, 1)
    out = jnp.zeros(route.shape, jnp.int32)
    for j in range(TOP_K):
        e = route[:, ROUTE_E + j:ROUTE_E + j + 1].astype(jnp.int32)
        start = jnp.sum(jnp.where(lane == e, starts_ref[...], 0.0), axis=-1, keepdims=True)
        d = (start + route[:, ROUTE_RANK + j:ROUTE_RANK + j + 1]).astype(jnp.int32)
        out = jnp.where(lane == j, d, out)
    dest_ref[...] = out


def _dest(route, starts_row, tm):
    n = route.shape[0]
    row = pl.BlockSpec((tm, LANES), lambda i: (i, 0))
    return pl.pallas_call(
        _dest_kernel,
        grid=(n // tm,),
        in_specs=[row, pl.BlockSpec((1, LANES), lambda i: (0, 0))],
        out_specs=row,
        out_shape=jax.ShapeDtypeStruct((n, LANES), jnp.int32),
        name="dest",
    )(route, starts_row)


SC_CORES = 2
SC_SUBCORES = 16
SC_WORKERS = SC_CORES * SC_SUBCORES
SC_CHUNK = 32


def _sc_gather_rows(table, idx):
    b = idx.shape[0]
    d = table.shape[1]
    per_worker = b // SC_WORKERS
    n_chunks = per_worker // SC_CHUNK
    assert per_worker * SC_WORKERS == b and n_chunks * SC_CHUNK == per_worker, (b,)
    mesh = plsc.VectorSubcoreMesh(core_axis_name="c", subcore_axis_name="s",
                                  num_cores=SC_CORES, num_subcores=SC_SUBCORES)

    @functools.partial(
        pl.kernel, mesh=mesh,
        out_type=jax.ShapeDtypeStruct((b, d), table.dtype),
        scratch_types=[pltpu.VMEM((SC_CHUNK,), jnp.int32), pltpu.VMEM((SC_CHUNK, d), table.dtype),
                       pltpu.SemaphoreType.DMA],
        name="sc_gather_rows",
    )
    def gather(table_hbm, idx_hbm, out_hbm, idx_v, rows_v, sem):
        worker = lax.axis_index("s") * SC_CORES + lax.axis_index("c")

        @pl.loop(0, n_chunks)
        def _(c):
            base = pl.multiple_of(worker * per_worker + c * SC_CHUNK, SC_CHUNK)
            pltpu.sync_copy(idx_hbm.at[pl.ds(base, SC_CHUNK)], idx_v)
            pltpu.async_copy(table_hbm.at[idx_v], rows_v, sem).wait()
            pltpu.sync_copy(rows_v, out_hbm.at[pl.ds(base, SC_CHUNK)])

    return gather(table, idx)


def _sc_scatter_rows(src, idx):
    n, d = src.shape
    b = idx.shape[0]
    per_worker = b // SC_WORKERS
    n_chunks = per_worker // SC_CHUNK
    assert per_worker * SC_WORKERS == b and n_chunks * SC_CHUNK == per_worker and n % per_worker == 0, (b, n)
    mesh = plsc.VectorSubcoreMesh(core_axis_name="c", subcore_axis_name="s",
                                  num_cores=SC_CORES, num_subcores=SC_SUBCORES)

    @functools.partial(
        pl.kernel, mesh=mesh,
        out_type=jax.ShapeDtypeStruct((b, d), src.dtype),
        scratch_types=[pltpu.VMEM((SC_CHUNK,), jnp.int32), pltpu.VMEM((SC_CHUNK, d), src.dtype)],
        name="sc_scatter_rows",
    )
    def scatter(src_hbm, idx_hbm, out_hbm, idx_v, rows_v):
        worker = lax.axis_index("s") * SC_CORES + lax.axis_index("c")

        @pl.loop(0, n_chunks)
        def _(c):
            base = pl.multiple_of(worker * per_worker + c * SC_CHUNK, SC_CHUNK)
            src_base = pl.multiple_of(lax.rem(base, n), SC_CHUNK)
            pltpu.sync_copy(idx_hbm.at[pl.ds(base, SC_CHUNK)], idx_v)
            pltpu.sync_copy(src_hbm.at[pl.ds(src_base, SC_CHUNK)], rows_v)
            pltpu.sync_copy(rows_v, out_hbm.at[idx_v])

    return scatter(src, idx)


def _expert_kernel(blk_ref, exp_ref, lo_ref, hi_ref, x_ref, wg_ref, wu_ref, wd_ref, yb_ref):
    del blk_ref, exp_ref
    k = pl.program_id(0)
    lo, hi = lo_ref[k], hi_ref[k]

    @pl.when(hi > lo)
    def _():
        xb = x_ref[...].astype(_BF16)
        g = _dot(xb, wg_ref[0])
        u = _dot(xb, wu_ref[0])
        hid = g * jax.nn.sigmoid(g) * u
        y = _dot(hid.astype(_BF16), wd_ref[0])
        r = lax.broadcasted_iota(jnp.int32, y.shape, 0)
        pltpu.store(yb_ref, y, mask=(r >= lo) & (r < hi))


def _experts(items, xs, wg_b, wu_b, wd_b):
    n_items = items[0].shape[0]
    wspec = lambda a: pl.BlockSpec((1,) + a.shape[1:], lambda k, blk, exp, lo, hi: (exp[k], 0, 0))
    rows = pl.BlockSpec((MOE_ROWS, D_MODEL), lambda k, blk, exp, lo, hi: (blk[k], 0))
    grid_spec = pltpu.PrefetchScalarGridSpec(
        num_scalar_prefetch=4,
        grid=(n_items,),
        in_specs=[rows, wspec(wg_b), wspec(wu_b), wspec(wd_b)],
        out_specs=rows,
    )
    return pl.pallas_call(
        _expert_kernel,
        grid_spec=grid_spec,
        out_shape=jax.ShapeDtypeStruct(xs.shape, _F32),
        compiler_params=pltpu.CompilerParams(dimension_semantics=("arbitrary",),
                                             vmem_limit_bytes=VMEM_LIMIT),
        name="experts",
    )(*items, xs, wg_b, wu_b, wd_b)


def _combine_kernel(h_ref, route_ref, g0_ref, g1_ref, y_ref):
    route = route_ref[...]
    y_ref[...] = h_ref[...] + route[:, 2:3] * g0_ref[...] + route[:, 3:4] * g1_ref[...]


def _combine(h, route, g, tm):
    n = h.shape[0]
    nt = n // tm
    row = lambda w: pl.BlockSpec((tm, w), lambda i: (i, 0))
    return pl.pallas_call(
        _combine_kernel,
        grid=(nt,),
        in_specs=[row(D_MODEL), row(LANES), row(D_MODEL), pl.BlockSpec((tm, D_MODEL), lambda i: (nt + i, 0))],
        out_specs=row(D_MODEL),
        out_shape=jax.ShapeDtypeStruct((n, D_MODEL), _F32),
        compiler_params=pltpu.CompilerParams(dimension_semantics=("arbitrary",),
                                             vmem_limit_bytes=VMEM_LIMIT),
        name="combine",
    )(h, route, g, g)


def _work_items(counts, n_pairs):
    n_blocks = n_pairs // MOE_ROWS
    starts = jnp.cumsum(counts) - counts
    cuts = jnp.sort(jnp.concatenate([jnp.arange(n_blocks, dtype=jnp.int32) * MOE_ROWS, starts]))
    ends = jnp.concatenate([cuts[1:], jnp.full((1,), n_pairs, jnp.int32)])
    blk = jnp.minimum(cuts // MOE_ROWS, n_blocks - 1)
    expert = jnp.clip(jnp.sum(starts[None, :] <= cuts[:, None], axis=1) - 1, 0, N_EXPERTS - 1).astype(jnp.int32)
    return starts, (blk, expert, cuts - blk * MOE_ROWS, ends - blk * MOE_ROWS)


def _moe(h, hn, route, counts_rows, wg_b, wu_b, wd_b, tm):
    n = h.shape[0]
    n_pairs = n * TOP_K
    counts = counts_rows[0, :N_EXPERTS].astype(jnp.int32)
    starts, items = _work_items(counts, n_pairs)
    starts_row = jnp.zeros((1, LANES), _F32).at[0, :N_EXPERTS].set(starts.astype(_F32))
    dest = _dest(route, starts_row, tm)[:, :TOP_K].T.reshape(n_pairs)
    xs = _sc_scatter_rows(hn, dest)
    yb = _experts(items, xs, wg_b, wu_b, wd_b)
    g = _sc_gather_rows(yb, dest)
    return _combine(h, route, g, tm)


def kernel(x_prompt, x_sample, state_conv, cache_k, cache_v, norm_attn_g, w_in, q_norm_g, k_norm_g, rel_bias, attn_sinks, w_dw, b_dw, conv_ln_g, conv_ln_b, w_conv_out, b_conv_out, w_attn_out, w_out, norm_ffn_g, w_grp, b_grp, w_router, b_router, w_gate, w_up, w_down):
    bsz, t, _ = x_prompt.shape
    nseq, steps, _ = x_sample.shape
    row = lambda a: a.reshape(1, -1).astype(_F32)

    w_in_b = w_in.astype(_BF16)
    wco_b = w_conv_out.astype(_BF16)
    wa_b = w_attn_out.astype(_BF16)
    wo_b = w_out.astype(_BF16)
    wg_b, wu_b, wd_b = w_gate.astype(_BF16), w_up.astype(_BF16), w_down.astype(_BF16)
    qg = row(jnp.tile(q_norm_g, N_HEADS)) * (HEAD_DIM ** -0.5)
    kg = row(jnp.tile(k_norm_g, N_KV_HEADS))
    group_of_lane = np.arange(Q_DIM) // HEAD_DIM
    gsum = jnp.asarray(group_of_lane[:, None] == np.arange(LANES)[None, :], _BF16)
    gexp = jnp.asarray(np.arange(LANES)[:, None] == group_of_lane[None, :], _BF16)
    w_rt = jnp.zeros((D_MODEL, LANES), _F32).at[:, :N_GROUPS].set(w_grp).at[:, N_GROUPS:N_GROUPS + N_EXPERTS].set(w_router)
    wr_hi = w_rt.astype(_BF16)
    wr_lo = (w_rt - wr_hi.astype(_F32)).astype(_BF16)
    b_rt = jnp.zeros((1, LANES), _F32).at[0, :N_GROUPS].set(b_grp).at[0, N_GROUPS:N_GROUPS + N_EXPERTS].set(b_router)
    tbl = _bias_tables(rel_bias)
    conv_params = (w_dw, row(b_dw), row(conv_ln_g), row(conv_ln_b), wco_b, row(b_conv_out))

    def finish_and_moe(x2d, conv_out, o, ga, gb, tm):
        h, hn, route, counts = _finish(x2d, conv_out, o, ga, gb, wa_b, wo_b, row(norm_ffn_g), wr_hi, wr_lo, b_rt, tm)
        return _moe(h, hn, route, counts, wg_b, wu_b, wd_b, tm)

    xp = x_prompt.reshape(bsz * t, D_MODEL)
    glu, q, k, v, ga, gb = _inproj(xp, row(norm_attn_g), w_in_b, qg, kg, gsum, gexp, _BF16, 256)
    conv_out = _conv_prompt(glu, bsz, t, *conv_params, 256)
    o = _attn_prompt(q, k, v, tbl, attn_sinks, bsz, t)
    y_prompt = finish_and_moe(xp, conv_out, o, ga, gb, 256).reshape(bsz, t, D_MODEL)
    glu3 = glu.reshape(bsz, t, D_CONV)
    state_conv_prompt = glu3[:, t - (CONV_WIDTH - 1):]
    tail = lambda a: a.reshape(bsz, t, KV_DIM)[:, t - WINDOW:].reshape(bsz, WINDOW, N_KV_HEADS, HEAD_DIM)
    cache_k_prompt, cache_v_prompt = tail(k), tail(v)

    xs = x_sample.reshape(nseq * steps, D_MODEL)
    glu, q, k, v, ga, gb = _inproj(xs, row(norm_attn_g), w_in_b, qg, kg, gsum, gexp, _F32, 256)
    glu3 = glu.reshape(nseq, steps, D_CONV)
    hist = jnp.concatenate([state_conv, glu3], axis=1)
    conv_out = _conv_sample(hist.transpose(1, 0, 2), *conv_params, 64)
    conv_out = conv_out.transpose(1, 0, 2).reshape(nseq * steps, D_MODEL)
    k3 = k.reshape(nseq, steps, KV_DIM)
    v3 = v.reshape(nseq, steps, KV_DIM)
    o = _attn_sample(q.reshape(nseq, steps, Q_DIM), k3, v3, cache_k.reshape(nseq, WINDOW, KV_DIM),
                     cache_v.reshape(nseq, WINDOW, KV_DIM), tbl, attn_sinks, 8)
    y_sample = finish_and_moe(xs, conv_out, o.reshape(nseq * steps, Q_DIM), ga, gb, 256).reshape(nseq, steps, D_MODEL)
    state_conv_sample = hist[:, steps:]
    cache_k_sample = jnp.concatenate([cache_k, k3.reshape(nseq, steps, N_KV_HEADS, HEAD_DIM)], axis=1)[:, steps:]
    cache_v_sample = jnp.concatenate([cache_v, v3.reshape(nseq, steps, N_KV_HEADS, HEAD_DIM)], axis=1)[:, steps:]

    return (y_prompt, y_sample, state_conv_prompt, cache_k_prompt, cache_v_prompt,
            state_conv_sample, cache_k_sample, cache_v_sample)
```

```python
import functools
import math

import numpy as np
import jax
import jax.numpy as jnp
from jax import lax
from jax.experimental import pallas as pl
from jax.experimental.pallas import tpu as pltpu
from jax.experimental.pallas import tpu_sc as plsc

D_MODEL = 1024
N_HEADS = 16
HEAD_DIM = 64
N_KV_HEADS = 4
WINDOW = 128
Q_DIM = N_HEADS * HEAD_DIM
KV_DIM = N_KV_HEADS * HEAD_DIM
N_BUCKETS = 32
MAX_EXACT = N_BUCKETS // 2
MAX_DISTANCE = 128
D_CONV = D_MODEL
CONV_WIDTH = 31
N_GROUPS = 4
EXPERTS_PER_GROUP = 8
N_EXPERTS = N_GROUPS * EXPERTS_PER_GROUP
TOP_K = 2
D_EXPERT = 256
EPS = 1e-6

LANES = 128
SUBLANES = 8
N_PAIRS = N_HEADS // 2
MOE_ROWS = 256
MASK_VALUE = -1e30
VMEM_LIMIT = 48 * 1024 * 1024

_F32 = jnp.float32
_BF16 = jnp.bfloat16


def _dot(a, b):
    return jnp.dot(a, b, preferred_element_type=_F32)


def _split_bf16(x):
    hi = x.astype(_BF16)
    lo = (x - hi.astype(_F32)).astype(_BF16)
    return hi, lo


def _head_rms_scale(z, gsum, gexp):
    hi, lo = _split_bf16(z * z)
    ssum = _dot(hi, gsum) + _dot(lo, gsum)
    r = lax.rsqrt(ssum * (1.0 / HEAD_DIM) + EPS)
    rhi, rlo = _split_bf16(r)
    return _dot(rhi, gexp) + _dot(rlo, gexp)


def _inproj_kernel(x_ref, g_ref, w_ref, qg_ref, kg_ref, gsum_ref, gexp_ref,
                   glu_ref, q_ref, k_ref, v_ref, ga_ref, gb_ref):
    x = x_ref[...]
    xn = x * lax.rsqrt(jnp.mean(x * x, axis=-1, keepdims=True) + EPS) * g_ref[...]
    xb = xn.astype(_BF16)

    def seg(lo, width):
        return _dot(xb, w_ref[:, lo:lo + width])

    a = seg(0, D_CONV)
    b = seg(D_CONV, D_CONV)
    glu_ref[...] = a * jax.nn.sigmoid(b)
    off = 2 * D_CONV
    q = seg(off, Q_DIM)
    q_ref[...] = (q * _head_rms_scale(q, gsum_ref[...], gexp_ref[...]) * qg_ref[...]).astype(q_ref.dtype)
    off += Q_DIM
    k = seg(off, KV_DIM)
    k_ref[...] = k * _head_rms_scale(k, gsum_ref[:KV_DIM, :], gexp_ref[:, :KV_DIM]) * kg_ref[...]
    off += KV_DIM
    v_ref[...] = seg(off, KV_DIM)
    off += KV_DIM
    ga_ref[...] = jax.nn.sigmoid(seg(off, D_MODEL)).astype(ga_ref.dtype)
    off += D_MODEL
    gb_ref[...] = jax.nn.sigmoid(seg(off, D_MODEL)).astype(gb_ref.dtype)


def _inproj(x, g, w_in_b, qg, kg, gsum, gexp, q_dtype, tm):
    n = x.shape[0]
    in_dim = w_in_b.shape[1]
    row = lambda w: pl.BlockSpec((tm, w), lambda i: (i, 0))
    full = lambda a: pl.BlockSpec(a.shape, lambda i: (0,) * a.ndim)
    return pl.pallas_call(
        _inproj_kernel,
        grid=(n // tm,),
        in_specs=[row(D_MODEL), full(g), full(w_in_b), full(qg), full(kg), full(gsum), full(gexp)],
        out_specs=[row(D_CONV), row(Q_DIM), row(KV_DIM), row(KV_DIM), row(D_MODEL), row(D_MODEL)],
        out_shape=[jax.ShapeDtypeStruct((n, D_CONV), _F32),
                   jax.ShapeDtypeStruct((n, Q_DIM), q_dtype),
                   jax.ShapeDtypeStruct((n, KV_DIM), _F32),
                   jax.ShapeDtypeStruct((n, KV_DIM), _F32),
                   jax.ShapeDtypeStruct((n, D_MODEL), _BF16),
                   jax.ShapeDtypeStruct((n, D_MODEL), _BF16)],
        compiler_params=pltpu.CompilerParams(dimension_semantics=("arbitrary",),
                                             vmem_limit_bytes=VMEM_LIMIT),
        name="inproj",
    )(x, g, w_in_b, qg, kg, gsum, gexp)


def _ln_swish_project(y, lng_ref, lnb_ref, wo_ref, bo_ref):
    mu = jnp.mean(y, axis=-1, keepdims=True)
    yc = y - mu
    var = jnp.mean(yc * yc, axis=-1, keepdims=True)
    z = yc * lax.rsqrt(var + EPS) * lng_ref[...] + lnb_ref[...]
    z = z * jax.nn.sigmoid(z)
    return (_dot(z.astype(_BF16), wo_ref[...]) + bo_ref[...]).astype(_BF16)


HALO = 32
CONV_STEPS = 16
CH_TILES = D_CONV // LANES


def _conv_prompt_kernel(glu_ref, w8_ref, b8_ref, lng_ref, lnb_ref, wo_ref, bo_ref,
                        out_ref, hist_ref, y_ref, *, tm):
    i = pl.program_id(1)

    @pl.when(i == 0)
    def _():
        hist_ref[0:HALO * CH_TILES, :] = jnp.zeros((HALO * CH_TILES, LANES), _F32)

    @pl.when(i > 0)
    def _():
        hist_ref[0:HALO * CH_TILES, :] = hist_ref[tm * CH_TILES:(tm + HALO) * CH_TILES, :]

    for c in range(CH_TILES):
        hist_ref[pl.ds(HALO * CH_TILES + c, tm, stride=CH_TILES), :] = glu_ref[:, c * LANES:(c + 1) * LANES]

    first = HALO - (CONV_WIDTH - 1)
    span = CONV_STEPS + CONV_WIDTH - 1

    def chunk(ci, carry):
        t0 = ci * CONV_STEPS
        x = hist_ref[pl.ds(pl.multiple_of((t0 + first) * CH_TILES, CH_TILES), span * CH_TILES), :]
        x = x.reshape(span, CH_TILES, LANES)
        acc = jnp.broadcast_to(b8_ref[...][None], (CONV_STEPS, CH_TILES, LANES))
        for j in range(CONV_WIDTH):
            acc = acc + x[j:j + CONV_STEPS] * w8_ref[j][None]
        y_ref[pl.ds(pl.multiple_of(t0 * CH_TILES, CH_TILES), CONV_STEPS * CH_TILES), :] = (
            acc.reshape(CONV_STEPS * CH_TILES, LANES))
        return carry

    lax.fori_loop(0, tm // CONV_STEPS, chunk, 0)
    y = jnp.concatenate([y_ref[pl.ds(c, tm, stride=CH_TILES), :] for c in range(CH_TILES)], axis=1)
    out_ref[...] = _ln_swish_project(y, lng_ref, lnb_ref, wo_ref, bo_ref)


def _conv_prompt(glu, bsz, t, w_dw, b_dw, lng, lnb, wo_b, bo, tm):
    assert CH_TILES == SUBLANES
    nt = t // tm
    w8 = w_dw.reshape(CONV_WIDTH, CH_TILES, LANES)
    b8 = b_dw.reshape(CH_TILES, LANES)
    full = lambda a: pl.BlockSpec(a.shape, lambda b, i: (0,) * a.ndim)
    row = pl.BlockSpec((tm, D_CONV), lambda b, i: (b * nt + i, 0))
    return pl.pallas_call(
        functools.partial(_conv_prompt_kernel, tm=tm),
        grid=(bsz, nt),
        in_specs=[row, full(w8), full(b8), full(lng), full(lnb), full(wo_b), full(bo)],
        out_specs=pl.BlockSpec((tm, D_MODEL), lambda b, i: (b * nt + i, 0)),
        out_shape=jax.ShapeDtypeStruct((bsz * t, D_MODEL), _BF16),
        scratch_shapes=[pltpu.VMEM(((tm + HALO) * CH_TILES, LANES), _F32), pltpu.VMEM((tm * CH_TILES, LANES), _F32)],
        compiler_params=pltpu.CompilerParams(dimension_semantics=("arbitrary", "arbitrary"),
                                             vmem_limit_bytes=VMEM_LIMIT),
        name="conv_prompt",
    )(glu, w8, b8, lng, lnb, wo_b, bo)


def _conv_sample_kernel(hist_ref, wdw_ref, bdw_ref, lng_ref, lnb_ref, wo_ref, bo_ref, out_ref, *, steps):
    for t in range(steps):
        acc = hist_ref[t] * wdw_ref[0:1, :]
        for j in range(1, CONV_WIDTH):
            acc = acc + hist_ref[t + j] * wdw_ref[j:j + 1, :]
        out_ref[t] = _ln_swish_project(acc + bdw_ref[...], lng_ref, lnb_ref, wo_ref, bo_ref)


def _conv_sample(hist, w_dw, b_dw, lng, lnb, wo_b, bo, sb):
    rows, nseq, _ = hist.shape
    steps = rows - (CONV_WIDTH - 1)
    full = lambda a: pl.BlockSpec(a.shape, lambda i: (0,) * a.ndim)
    return pl.pallas_call(
        functools.partial(_conv_sample_kernel, steps=steps),
        grid=(nseq // sb,),
        in_specs=[pl.BlockSpec((rows, sb, D_CONV), lambda i: (0, i, 0)),
                  full(w_dw), full(b_dw), full(lng), full(lnb), full(wo_b), full(bo)],
        out_specs=pl.BlockSpec((steps, sb, D_MODEL), lambda i: (0, i, 0)),
        out_shape=jax.ShapeDtypeStruct((steps, nseq, D_MODEL), _BF16),
        compiler_params=pltpu.CompilerParams(dimension_semantics=("arbitrary",),
                                             vmem_limit_bytes=VMEM_LIMIT),
        name="conv_sample",
    )(hist, w_dw, b_dw, lng, lnb, wo_b, bo)


def _bucket_map():
    i = np.arange(WINDOW)[:, None]
    j = np.arange(WINDOW)[None, :]
    n = (i - j) % WINDOW
    nf = np.maximum(n, 1).astype(np.float32)
    large = MAX_EXACT + (np.log(nf / np.float32(MAX_EXACT)) / np.float32(math.log(MAX_DISTANCE / MAX_EXACT))
                         * np.float32(N_BUCKETS - MAX_EXACT)).astype(np.int32)
    return np.where(n < MAX_EXACT, n, np.minimum(large, N_BUCKETS - 1)).astype(np.int32)


def _bias_table_kernel(rb_ref, bm_ref, tbl_ref):
    p = pl.program_id(0)
    bm = bm_ref[...]
    for half in range(2):
        h = 2 * p + half
        t = jnp.zeros(bm.shape, _F32)
        for b in range(N_BUCKETS):
            t = jnp.where(bm == b, rb_ref[b, h], t)
        tbl_ref[0, :, half * WINDOW:(half + 1) * WINDOW] = t


def _bias_tables(rel_bias):
    bm = jnp.asarray(_bucket_map())
    return pl.pallas_call(
        _bias_table_kernel,
        grid=(N_PAIRS,),
        in_specs=[pl.BlockSpec(memory_space=pltpu.SMEM), pl.BlockSpec(bm.shape, lambda p: (0, 0))],
        out_specs=pl.BlockSpec((1, WINDOW, 2 * WINDOW), lambda p: (p, 0, 0)),
        out_shape=jax.ShapeDtypeStruct((N_PAIRS, WINDOW, 2 * WINDOW), _F32),
        name="bias_tables",
    )(rel_bias, bm)


def _block_diag_pairs(slab):
    low = lax.broadcasted_iota(jnp.int32, slab.shape, 1) < HEAD_DIM
    swapped = pltpu.roll(slab, HEAD_DIM, axis=1)
    zero = jnp.zeros_like(slab)
    first = jnp.concatenate([jnp.where(low, slab, zero), jnp.where(low, zero, swapped)], axis=0)
    second = jnp.concatenate([jnp.where(low, swapped, zero), jnp.where(low, zero, slab)], axis=0)
    return first.astype(_BF16), second.astype(_BF16)


def _attend(q, k_prev, k_own, v_prev, v_own, tbl_ref, sink_ref, prev_shift, store):
    tq = q.shape[0]
    rows = 2 * tq
    row = lax.broadcasted_iota(jnp.int32, (rows, 2 * WINDOW), 0)
    col = lax.broadcasted_iota(jnp.int32, (rows, 2 * WINDOW), 1)
    from_prev = (col & (WINDOW - 1)) > jnp.where(row >= tq, row - tq, row)
    top = lax.broadcasted_iota(jnp.int32, (rows, 1), 0) < tq
    low = lax.broadcasted_iota(jnp.int32, (rows, LANES), 1) < HEAD_DIM
    r2 = lax.broadcasted_iota(jnp.int32, (2 * WINDOW, LANES), 0)
    c2 = lax.broadcasted_iota(jnp.int32, (2 * WINDOW, LANES), 1)
    head_ones = jnp.where((r2 < WINDOW) == (c2 < HEAD_DIM), 1.0, 0.0).astype(_BF16)
    contract_last = (((1,), (1,)), ((), ()))
    for slab in range(KV_DIM // LANES):
        cols = slice(slab * LANES, (slab + 1) * LANES)
        kp_pair = _block_diag_pairs(k_prev[:, cols])
        ko_pair = _block_diag_pairs(k_own[:, cols])
        vp_pair = _block_diag_pairs(v_prev[:, cols])
        vo_pair = _block_diag_pairs(v_own[:, cols])
        for sub in range(2):
            pair_a = 2 * (2 * slab + sub)
            pair_b = pair_a + 1
            qq = jnp.concatenate([q[:, pair_a * LANES:(pair_a + 1) * LANES],
                                  q[:, pair_b * LANES:(pair_b + 1) * LANES]], axis=0).astype(_BF16)
            sp = lax.dot_general(qq, kp_pair[sub], contract_last, preferred_element_type=_F32)
            so = lax.dot_general(qq, ko_pair[sub], contract_last, preferred_element_type=_F32)
            bias = jnp.concatenate([tbl_ref[pair_a, 0:tq, :], tbl_ref[pair_b, 0:tq, :]], axis=0)
            s = jnp.where(from_prev, sp + prev_shift, so) + bias
            sink_even = jnp.where(top, sink_ref[2 * pair_a], sink_ref[2 * pair_b])
            sink_odd = jnp.where(top, sink_ref[2 * pair_a + 1], sink_ref[2 * pair_b + 1])
            m_even = jnp.maximum(jnp.max(s[:, :WINDOW], axis=-1, keepdims=True), sink_even)
            m_odd = jnp.maximum(jnp.max(s[:, WINDOW:], axis=-1, keepdims=True), sink_odd)
            p = jnp.exp(s - jnp.where(col < WINDOW, m_even, m_odd)).astype(_BF16)
            zero = jnp.zeros_like(p)
            o = _dot(jnp.where(from_prev, p, zero), vp_pair[sub]) + _dot(jnp.where(from_prev, zero, p), vo_pair[sub])
            den = _dot(p, head_ones) + jnp.where(low, jnp.exp(sink_even - m_even), jnp.exp(sink_odd - m_odd))
            o = o / den
            store(pair_a, o[:tq])
            store(pair_b, o[tq:])


def _attn_prompt_kernel(sink_ref, q_ref, kp_ref, ko_ref, vp_ref, vo_ref, tbl_ref, o_ref):
    prev_shift = jnp.where(pl.program_id(1) == 0, MASK_VALUE, 0.0).astype(_F32)

    def store(pair, o):
        o_ref[:, pair * LANES:(pair + 1) * LANES] = o.astype(o_ref.dtype)

    _attend(q_ref[...], kp_ref[...], ko_ref[...], vp_ref[...], vo_ref[...], tbl_ref, sink_ref, prev_shift, store)


def _attn_prompt(q, k, v, tbl, sinks, bsz, t):
    nb = t // WINDOW
    own = lambda w: pl.BlockSpec((WINDOW, w), lambda b, i: (b * nb + i, 0))
    prev = lambda w: pl.BlockSpec((WINDOW, w), lambda b, i: (b * nb + jnp.maximum(i - 1, 0), 0))
    return pl.pallas_call(
        _attn_prompt_kernel,
        grid=(bsz, nb),
        in_specs=[pl.BlockSpec(memory_space=pltpu.SMEM), own(Q_DIM), prev(KV_DIM), own(KV_DIM),
                  prev(KV_DIM), own(KV_DIM), pl.BlockSpec(tbl.shape, lambda b, i: (0, 0, 0))],
        out_specs=own(Q_DIM),
        out_shape=jax.ShapeDtypeStruct((bsz * t, Q_DIM), _BF16),
        compiler_params=pltpu.CompilerParams(dimension_semantics=("arbitrary", "arbitrary"),
                                             vmem_limit_bytes=VMEM_LIMIT),
        name="attn_prompt",
    )(sinks, q, k, k, v, v, tbl)


def _attn_sample_kernel(sink_ref, q_ref, kn_ref, vn_ref, ck_ref, cv_ref, tbl_ref, o_ref, *, sb, steps):
    pad = jnp.zeros((WINDOW - steps, KV_DIM), _F32)

    def one_sequence(s, carry):
        def store(pair, o):
            o_ref[s, :, pair * LANES:(pair + 1) * LANES] = o

        k_own = jnp.concatenate([kn_ref[s], pad], axis=0)
        v_own = jnp.concatenate([vn_ref[s], pad], axis=0)
        _attend(q_ref[s], ck_ref[s], k_own, cv_ref[s], v_own, tbl_ref, sink_ref, jnp.float32(0.0), store)
        return carry

    lax.fori_loop(0, sb, one_sequence, 0)


def _attn_sample(q, k_new, v_new, cache_k, cache_v, tbl, sinks, sb):
    nseq, steps, _ = q.shape
    seq = lambda r, w: pl.BlockSpec((sb, r, w), lambda i: (i, 0, 0))
    return pl.pallas_call(
        functools.partial(_attn_sample_kernel, sb=sb, steps=steps),
        grid=(nseq // sb,),
        in_specs=[pl.BlockSpec(memory_space=pltpu.SMEM), seq(steps, Q_DIM), seq(steps, KV_DIM), seq(steps, KV_DIM),
                  seq(WINDOW, KV_DIM), seq(WINDOW, KV_DIM), pl.BlockSpec(tbl.shape, lambda i: (0, 0, 0))],
        out_specs=seq(steps, Q_DIM),
        out_shape=jax.ShapeDtypeStruct((nseq, steps, Q_DIM), _F32),
        compiler_params=pltpu.CompilerParams(dimension_semantics=("arbitrary",),
                                             vmem_limit_bytes=VMEM_LIMIT),
        name="attn_sample",
    )(sinks, q, k_new, v_new, cache_k, cache_v, tbl)


def _lane_min_index(mask, lane):
    return jnp.min(jnp.where(mask, lane, LANES), axis=-1, keepdims=True)


def _finish_kernel(x_ref, conv_ref, o_ref, ga_ref, gb_ref, wa_ref, wo_ref, ng_ref, wr_hi_ref, wr_lo_ref, br_ref,
                   tri_ref, h_ref, hn_ref, route_ref, count_ref, running_ref):
    @pl.when(pl.program_id(0) == 0)
    def _():
        running_ref[...] = jnp.zeros_like(running_ref)

    attn_out = _dot(o_ref[...].astype(_BF16), wa_ref[...])
    merged = ga_ref[...].astype(_F32) * conv_ref[...].astype(_F32) + gb_ref[...].astype(_F32) * attn_out
    h = x_ref[...] + _dot(merged.astype(_BF16), wo_ref[...])
    h_ref[...] = h
    hn = h * lax.rsqrt(jnp.mean(h * h, axis=-1, keepdims=True) + EPS) * ng_ref[...]
    hn_ref[...] = hn

    hi, lo = _split_bf16(hn)
    logits = _dot(hi, wr_hi_ref[...]) + _dot(lo, wr_hi_ref[...]) + _dot(hi, wr_lo_ref[...]) + br_ref[...]
    lane = lax.broadcasted_iota(jnp.int32, logits.shape, 1)
    gmask = lane < N_GROUPS
    gl = jnp.where(gmask, logits, MASK_VALUE)
    gmax = jnp.max(gl, axis=-1, keepdims=True)
    grp = _lane_min_index(gmask & (gl == gmax), lane)
    p_grp = 1.0 / jnp.sum(jnp.where(gmask, jnp.exp(gl - gmax), 0.0), axis=-1, keepdims=True)
    e_lo = N_GROUPS + grp * EXPERTS_PER_GROUP
    emask = (lane >= e_lo) & (lane < e_lo + EXPERTS_PER_GROUP)
    el = jnp.where(emask, logits, MASK_VALUE)
    ex = jnp.where(emask, jnp.exp(el - jnp.max(el, axis=-1, keepdims=True)), 0.0)
    prob = jnp.where(emask, ex / jnp.sum(ex, axis=-1, keepdims=True), -1.0)
    p1 = jnp.max(prob, axis=-1, keepdims=True)
    i1 = _lane_min_index(prob == p1, lane)
    rest = jnp.where(lane == i1, -1.0, prob)
    p2 = jnp.max(rest, axis=-1, keepdims=True)
    i2 = _lane_min_index(rest == p2, lane)
    w1 = p_grp * p1 / (p1 + p2)
    w2 = p_grp * p2 / (p1 + p2)
    e1 = i1 - N_GROUPS
    e2 = i2 - N_GROUPS

    hot1 = lane == e1
    hot2 = lane == e2
    hot = jnp.where(hot1 | hot2, 1.0, 0.0)
    before = _dot(tri_ref[...], hot.astype(_BF16)) + running_ref[...]
    rank1 = jnp.sum(jnp.where(hot1, before, 0.0), axis=-1, keepdims=True)
    rank2 = jnp.sum(jnp.where(hot2, before, 0.0), axis=-1, keepdims=True)
    running_ref[...] += jnp.sum(hot, axis=0, keepdims=True)
    count_ref[...] = jnp.broadcast_to(running_ref[...], count_ref.shape)

    fields = (e1.astype(_F32), e2.astype(_F32), w1, w2, rank1, rank2)
    route = jnp.zeros(logits.shape, _F32)
    for pos, val in enumerate(fields):
        route = jnp.where(lane == pos, val, route)
    route_ref[...] = route


ROUTE_E, ROUTE_W, ROUTE_RANK = 0, 2, 4
DEST_ROWS = 1024


def _finish(x, conv_out, o, ga, gb, wa_b, wo_b, ng, wr_hi, wr_lo, br, tm):
    n = x.shape[0]
    tri = jnp.asarray(np.tril(np.ones((tm, tm), np.float32), -1), _BF16)
    row = lambda w: pl.BlockSpec((tm, w), lambda i: (i, 0))
    full = lambda a: pl.BlockSpec(a.shape, lambda i: (0,) * a.ndim)
    return pl.pallas_call(
        _finish_kernel,
        grid=(n // tm,),
        in_specs=[row(D_MODEL), row(D_MODEL), row(Q_DIM), row(D_MODEL), row(D_MODEL),
                  full(wa_b), full(wo_b), full(ng), full(wr_hi), full(wr_lo), full(br), full(tri)],
        out_specs=[row(D_MODEL), row(D_MODEL), row(LANES), pl.BlockSpec((SUBLANES, LANES), lambda i: (0, 0))],
        out_shape=[jax.ShapeDtypeStruct((n, D_MODEL), _F32),
                   jax.ShapeDtypeStruct((n, D_MODEL), _F32),
                   jax.ShapeDtypeStruct((n, LANES), _F32),
                   jax.ShapeDtypeStruct((SUBLANES, LANES), _F32)],
        scratch_shapes=[pltpu.VMEM((1, LANES), _F32)],
        compiler_params=pltpu.CompilerParams(dimension_semantics=("arbitrary",),
                                             vmem_limit_bytes=VMEM_LIMIT),
        name="finish",
    )(x, conv_out, o, ga, gb, wa_b, wo_b, ng, wr_hi, wr_lo, br, tri)


def _dest_kernel(route_ref, starts_ref, dest_ref):
    route = route_ref[...]
    lane = lax.broadcasted_iota(jnp.int32, route.shape, 1)
    out = jnp.zeros(route.shape, jnp.int32)
    for j in range(TOP_K):
        e = route[:, ROUTE_E + j:ROUTE_E + j + 1].astype(jnp.int32)
        start = jnp.sum(jnp.where(lane == e, starts_ref[...], 0.0), axis=-1, keepdims=True)
        d = (start + route[:, ROUTE_RANK + j:ROUTE_RANK + j + 1]).astype(jnp.int32)
        out = jnp.where(lane == j, d, out)
    dest_ref[...] = out


def _dest(route, starts_row, tm):
    n = route.shape[0]
    row = pl.BlockSpec((tm, LANES), lambda i: (i, 0))
    return pl.pallas_call(
        _dest_kernel,
        grid=(n // tm,),
        in_specs=[row, pl.BlockSpec((1, LANES), lambda i: (0, 0))],
        out_specs=row,
        out_shape=jax.ShapeDtypeStruct((n, LANES), jnp.int32),
        name="dest",
    )(route, starts_row)


SC_CORES = 2
SC_SUBCORES = 16
SC_WORKERS = SC_CORES * SC_SUBCORES
SC_CHUNK = 32


def _sc_gather_rows(table, idx):
    b = idx.shape[0]
    d = table.shape[1]
    per_worker = b // SC_WORKERS
    n_chunks = per_worker // SC_CHUNK
    assert per_worker * SC_WORKERS == b and n_chunks * SC_CHUNK == per_worker, (b,)
    mesh = plsc.VectorSubcoreMesh(core_axis_name="c", subcore_axis_name="s",
                                  num_cores=SC_CORES, num_subcores=SC_SUBCORES)

    @functools.partial(
        pl.kernel, mesh=mesh,
        out_type=jax.ShapeDtypeStruct((b, d), table.dtype),
        scratch_types=[pltpu.VMEM((SC_CHUNK,), jnp.int32), pltpu.VMEM((SC_CHUNK, d), table.dtype),
                       pltpu.SemaphoreType.DMA],
        name="sc_gather_rows",
    )
    def gather(table_hbm, idx_hbm, out_hbm, idx_v, rows_v, sem):
        worker = lax.axis_index("s") * SC_CORES + lax.axis_index("c")

        @pl.loop(0, n_chunks)
        def _(c):
            base = pl.multiple_of(worker * per_worker + c * SC_CHUNK, SC_CHUNK)
            pltpu.sync_copy(idx_hbm.at[pl.ds(base, SC_CHUNK)], idx_v)
            pltpu.async_copy(table_hbm.at[idx_v], rows_v, sem).wait()
            pltpu.sync_copy(rows_v, out_hbm.at[pl.ds(base, SC_CHUNK)])

    return gather(table, idx)


def _sc_scatter_rows(src, idx):
    n, d = src.shape
    b = idx.shape[0]
    per_worker = b // SC_WORKERS
    n_chunks = per_worker // SC_CHUNK
    assert per_worker * SC_WORKERS == b and n_chunks * SC_CHUNK == per_worker and n % per_worker == 0, (b, n)
    mesh = plsc.VectorSubcoreMesh(core_axis_name="c", subcore_axis_name="s",
                                  num_cores=SC_CORES, num_subcores=SC_SUBCORES)

    @functools.partial(
        pl.kernel, mesh=mesh,
        out_type=jax.ShapeDtypeStruct((b, d), src.dtype),
        scratch_types=[pltpu.VMEM((SC_CHUNK,), jnp.int32), pltpu.VMEM((SC_CHUNK, d), src.dtype)],
        name="sc_scatter_rows",
    )
    def scatter(src_hbm, idx_hbm, out_hbm, idx_v, rows_v):
        worker = lax.axis_index("s") * SC_CORES + lax.axis_index("c")

        @pl.loop(0, n_chunks)
        def _(c):
            base = pl.multiple_of(worker * per_worker + c * SC_CHUNK, SC_CHUNK)
            src_base = pl.multiple_of(lax.rem(base, n), SC_CHUNK)
            pltpu.sync_copy(idx_hbm.at[pl.ds(base, SC_CHUNK)], idx_v)
            pltpu.sync_copy(src_hbm.at[pl.ds(src_base, SC_CHUNK)], rows_v)
            pltpu.sync_copy(rows_v, out_hbm.at[idx_v])

    return scatter(src, idx)


def _expert_kernel(blk_ref, exp_ref, lo_ref, hi_ref, x_ref, wg_ref, wu_ref, wd_ref, yb_ref,
                   wg_b, wu_b, wd_b, held_ref):
    del blk_ref
    k = pl.program_id(0)
    lo, hi, e = lo_ref[k], hi_ref[k], exp_ref[k]

    @pl.when(k == 0)
    def _():
        held_ref[0] = -1

    @pl.when(hi > lo)
    def _():
        @pl.when(held_ref[0] != e)
        def _():
            wg_b[...] = wg_ref[0].astype(_BF16)
            wu_b[...] = wu_ref[0].astype(_BF16)
            wd_b[...] = wd_ref[0].astype(_BF16)
            held_ref[0] = e

        xb = x_ref[...].astype(_BF16)
        g = _dot(xb, wg_b[...])
        u = _dot(xb, wu_b[...])
        hid = g * jax.nn.sigmoid(g) * u
        y = _dot(hid.astype(_BF16), wd_b[...])
        r = lax.broadcasted_iota(jnp.int32, y.shape, 0)
        pltpu.store(yb_ref, y, mask=(r >= lo) & (r < hi))


def _experts(items, xs, w_gate, w_up, w_down):
    n_items = items[0].shape[0]
    wspec = lambda a: pl.BlockSpec((1,) + a.shape[1:], lambda k, blk, exp, lo, hi: (exp[k], 0, 0))
    rows = pl.BlockSpec((MOE_ROWS, D_MODEL), lambda k, blk, exp, lo, hi: (blk[k], 0))
    grid_spec = pltpu.PrefetchScalarGridSpec(
        num_scalar_prefetch=4,
        grid=(n_items,),
        in_specs=[rows, wspec(w_gate), wspec(w_up), wspec(w_down)],
        out_specs=rows,
        scratch_shapes=[pltpu.VMEM(w_gate.shape[1:], _BF16), pltpu.VMEM(w_up.shape[1:], _BF16),
                        pltpu.VMEM(w_down.shape[1:], _BF16), pltpu.SMEM((1,), jnp.int32)],
    )
    return pl.pallas_call(
        _expert_kernel,
        grid_spec=grid_spec,
        out_shape=jax.ShapeDtypeStruct(xs.shape, _F32),
        compiler_params=pltpu.CompilerParams(dimension_semantics=("arbitrary",),
                                             vmem_limit_bytes=VMEM_LIMIT),
        name="experts",
    )(*items, xs, w_gate, w_up, w_down)


def _combine_kernel(h_ref, route_ref, g0_ref, g1_ref, y_ref):
    route = route_ref[...]
    y_ref[...] = h_ref[...] + route[:, 2:3] * g0_ref[...] + route[:, 3:4] * g1_ref[...]


def _combine(h, route, g, tm):
    n = h.shape[0]
    nt = n // tm
    row = lambda w: pl.BlockSpec((tm, w), lambda i: (i, 0))
    return pl.pallas_call(
        _combine_kernel,
        grid=(nt,),
        in_specs=[row(D_MODEL), row(LANES), row(D_MODEL), pl.BlockSpec((tm, D_MODEL), lambda i: (nt + i, 0))],
        out_specs=row(D_MODEL),
        out_shape=jax.ShapeDtypeStruct((n, D_MODEL), _F32),
        compiler_params=pltpu.CompilerParams(dimension_semantics=("arbitrary",),
                                             vmem_limit_bytes=VMEM_LIMIT),
        name="combine",
    )(h, route, g, g)


def _work_items(counts, n_pairs):
    n_blocks = n_pairs // MOE_ROWS
    starts = jnp.cumsum(counts) - counts
    cuts = jnp.sort(jnp.concatenate([jnp.arange(n_blocks, dtype=jnp.int32) * MOE_ROWS, starts]))
    ends = jnp.concatenate([cuts[1:], jnp.full((1,), n_pairs, jnp.int32)])
    blk = jnp.minimum(cuts // MOE_ROWS, n_blocks - 1)
    expert = jnp.clip(jnp.sum(starts[None, :] <= cuts[:, None], axis=1) - 1, 0, N_EXPERTS - 1).astype(jnp.int32)
    return starts, (blk, expert, cuts - blk * MOE_ROWS, ends - blk * MOE_ROWS)


def _moe(h, hn, route, counts_rows, wg_b, wu_b, wd_b, tm):
    n = h.shape[0]
    n_pairs = n * TOP_K
    counts = counts_rows[0, :N_EXPERTS].astype(jnp.int32)
    starts, items = _work_items(counts, n_pairs)
    starts_row = jnp.zeros((1, LANES), _F32).at[0, :N_EXPERTS].set(starts.astype(_F32))
    dest = _dest(route, starts_row, min(n, DEST_ROWS))[:, :TOP_K].T.reshape(n_pairs)
    xs = _sc_scatter_rows(hn, dest)
    yb = _experts(items, xs, wg_b, wu_b, wd_b)
    g = _sc_gather_rows(yb, dest)
    return _combine(h, route, g, tm)


def kernel(x_prompt, x_sample, state_conv, cache_k, cache_v, norm_attn_g, w_in, q_norm_g, k_norm_g, rel_bias, attn_sinks, w_dw, b_dw, conv_ln_g, conv_ln_b, w_conv_out, b_conv_out, w_attn_out, w_out, norm_ffn_g, w_grp, b_grp, w_router, b_router, w_gate, w_up, w_down):
    bsz, t, _ = x_prompt.shape
    nseq, steps, _ = x_sample.shape
    row = lambda a: a.reshape(1, -1).astype(_F32)

    w_in_b = w_in.astype(_BF16)
    wco_b = w_conv_out.astype(_BF16)
    wa_b = w_attn_out.astype(_BF16)
    wo_b = w_out.astype(_BF16)
    qg = row(jnp.tile(q_norm_g, N_HEADS)) * (HEAD_DIM ** -0.5)
    kg = row(jnp.tile(k_norm_g, N_KV_HEADS))
    group_of_lane = np.arange(Q_DIM) // HEAD_DIM
    gsum = jnp.asarray(group_of_lane[:, None] == np.arange(LANES)[None, :], _BF16)
    gexp = jnp.asarray(np.arange(LANES)[:, None] == group_of_lane[None, :], _BF16)
    w_rt = jnp.zeros((D_MODEL, LANES), _F32).at[:, :N_GROUPS].set(w_grp).at[:, N_GROUPS:N_GROUPS + N_EXPERTS].set(w_router)
    wr_hi = w_rt.astype(_BF16)
    wr_lo = (w_rt - wr_hi.astype(_F32)).astype(_BF16)
    b_rt = jnp.zeros((1, LANES), _F32).at[0, :N_GROUPS].set(b_grp).at[0, N_GROUPS:N_GROUPS + N_EXPERTS].set(b_router)
    tbl = _bias_tables(rel_bias)
    conv_params = (w_dw, row(b_dw), row(conv_ln_g), row(conv_ln_b), wco_b, row(b_conv_out))

    def finish_and_moe(x2d, conv_out, o, ga, gb, tm):
        h, hn, route, counts = _finish(x2d, conv_out, o, ga, gb, wa_b, wo_b, row(norm_ffn_g), wr_hi, wr_lo, b_rt, tm)
        return _moe(h, hn, route, counts, w_gate, w_up, w_down, tm)

    xp = x_prompt.reshape(bsz * t, D_MODEL)
    glu, q, k, v, ga, gb = _inproj(xp, row(norm_attn_g), w_in_b, qg, kg, gsum, gexp, _BF16, 256)
    conv_out = _conv_prompt(glu, bsz, t, *conv_params, 256)
    o = _attn_prompt(q, k, v, tbl, attn_sinks, bsz, t)
    y_prompt = finish_and_moe(xp, conv_out, o, ga, gb, 256).reshape(bsz, t, D_MODEL)
    glu3 = glu.reshape(bsz, t, D_CONV)
    state_conv_prompt = glu3[:, t - (CONV_WIDTH - 1):]
    tail = lambda a: a.reshape(bsz, t, KV_DIM)[:, t - WINDOW:].reshape(bsz, WINDOW, N_KV_HEADS, HEAD_DIM)
    cache_k_prompt, cache_v_prompt = tail(k), tail(v)

    xs = x_sample.reshape(nseq * steps, D_MODEL)
    glu, q, k, v, ga, gb = _inproj(xs, row(norm_attn_g), w_in_b, qg, kg, gsum, gexp, _F32, 256)
    glu3 = glu.reshape(nseq, steps, D_CONV)
    hist = jnp.concatenate([state_conv, glu3], axis=1)
    conv_out = _conv_sample(hist.transpose(1, 0, 2), *conv_params, 64)
    conv_out = conv_out.transpose(1, 0, 2).reshape(nseq * steps, D_MODEL)
    k3 = k.reshape(nseq, steps, KV_DIM)
    v3 = v.reshape(nseq, steps, KV_DIM)
    o = _attn_sample(q.reshape(nseq, steps, Q_DIM), k3, v3, cache_k.reshape(nseq, WINDOW, KV_DIM),
                     cache_v.reshape(nseq, WINDOW, KV_DIM), tbl, attn_sinks, 8)
    y_sample = finish_and_moe(xs, conv_out, o.reshape(nseq * steps, Q_DIM), ga, gb, 256).reshape(nseq, steps, D_MODEL)
    state_conv_sample = hist[:, steps:]
    cache_k_sample = jnp.concatenate([cache_k, k3.reshape(nseq, steps, N_KV_HEADS, HEAD_DIM)], axis=1)[:, steps:]
    cache_v_sample = jnp.concatenate([cache_v, v3.reshape(nseq, steps, N_KV_HEADS, HEAD_DIM)], axis=1)[:, steps:]

    return (y_prompt, y_sample, state_conv_prompt, cache_k_prompt, cache_v_prompt,
            state_conv_sample, cache_k_sample, cache_v_sample)
```

```python
import functools
import math

import numpy as np
import jax
import jax.numpy as jnp
from jax import lax
from jax.experimental import pallas as pl
from jax.experimental.pallas import tpu as pltpu
from jax.experimental.pallas import tpu_sc as plsc

D_MODEL = 1024
N_HEADS = 16
HEAD_DIM = 64
N_KV_HEADS = 4
WINDOW = 128
Q_DIM = N_HEADS * HEAD_DIM
KV_DIM = N_KV_HEADS * HEAD_DIM
N_BUCKETS = 32
MAX_EXACT = N_BUCKETS // 2
MAX_DISTANCE = 128
D_CONV = D_MODEL
CONV_WIDTH = 31
N_GROUPS = 4
EXPERTS_PER_GROUP = 8
N_EXPERTS = N_GROUPS * EXPERTS_PER_GROUP
TOP_K = 2
D_EXPERT = 256
EPS = 1e-6

LANES = 128
SUBLANES = 8
N_PAIRS = N_HEADS // 2
MOE_ROWS = 256
MASK_VALUE = -1e30
VMEM_LIMIT = 48 * 1024 * 1024

_F32 = jnp.float32
_BF16 = jnp.bfloat16


def _dot(a, b):
    return jnp.dot(a, b, preferred_element_type=_F32)


def _split_bf16(x):
    hi = x.astype(_BF16)
    lo = (x - hi.astype(_F32)).astype(_BF16)
    return hi, lo


def _head_rms_scale(z, gsum, gexp):
    hi, lo = _split_bf16(z * z)
    ssum = _dot(hi, gsum) + _dot(lo, gsum)
    r = lax.rsqrt(ssum * (1.0 / HEAD_DIM) + EPS)
    rhi, rlo = _split_bf16(r)
    return _dot(rhi, gexp) + _dot(rlo, gexp)


def _inproj_kernel(x_ref, g_ref, w_ref, qg_ref, kg_ref, gsum_ref, gexp_ref,
                   glu_ref, q_ref, k_ref, v_ref, ga_ref, gb_ref):
    x = x_ref[...]
    xn = x * lax.rsqrt(jnp.mean(x * x, axis=-1, keepdims=True) + EPS) * g_ref[...]
    xb = xn.astype(_BF16)

    def seg(lo, width):
        return _dot(xb, w_ref[:, lo:lo + width])

    a = seg(0, D_CONV)
    b = seg(D_CONV, D_CONV)
    glu_ref[...] = a * jax.nn.sigmoid(b)
    off = 2 * D_CONV
    q = seg(off, Q_DIM)
    q_ref[...] = (q * _head_rms_scale(q, gsum_ref[...], gexp_ref[...]) * qg_ref[...]).astype(q_ref.dtype)
    off += Q_DIM
    k = seg(off, KV_DIM)
    k_ref[...] = k * _head_rms_scale(k, gsum_ref[:KV_DIM, :], gexp_ref[:, :KV_DIM]) * kg_ref[...]
    off += KV_DIM
    v_ref[...] = seg(off, KV_DIM)
    off += KV_DIM
    ga_ref[...] = jax.nn.sigmoid(seg(off, D_MODEL)).astype(ga_ref.dtype)
    off += D_MODEL
    gb_ref[...] = jax.nn.sigmoid(seg(off, D_MODEL)).astype(gb_ref.dtype)


def _inproj(x, g, w_in_b, qg, kg, gsum, gexp, q_dtype, tm):
    n = x.shape[0]
    in_dim = w_in_b.shape[1]
    row = lambda w: pl.BlockSpec((tm, w), lambda i: (i, 0))
    full = lambda a: pl.BlockSpec(a.shape, lambda i: (0,) * a.ndim)
    return pl.pallas_call(
        _inproj_kernel,
        grid=(n // tm,),
        in_specs=[row(D_MODEL), full(g), full(w_in_b), full(qg), full(kg), full(gsum), full(gexp)],
        out_specs=[row(D_CONV), row(Q_DIM), row(KV_DIM), row(KV_DIM), row(D_MODEL), row(D_MODEL)],
        out_shape=[jax.ShapeDtypeStruct((n, D_CONV), _F32),
                   jax.ShapeDtypeStruct((n, Q_DIM), q_dtype),
                   jax.ShapeDtypeStruct((n, KV_DIM), _F32),
                   jax.ShapeDtypeStruct((n, KV_DIM), _F32),
                   jax.ShapeDtypeStruct((n, D_MODEL), _BF16),
                   jax.ShapeDtypeStruct((n, D_MODEL), _BF16)],
        compiler_params=pltpu.CompilerParams(dimension_semantics=("arbitrary",),
                                             vmem_limit_bytes=VMEM_LIMIT),
        name="inproj",
    )(x, g, w_in_b, qg, kg, gsum, gexp)


def _ln_swish_project(y, lng_ref, lnb_ref, wo_ref, bo_ref):
    mu = jnp.mean(y, axis=-1, keepdims=True)
    yc = y - mu
    var = jnp.mean(yc * yc, axis=-1, keepdims=True)
    z = yc * lax.rsqrt(var + EPS) * lng_ref[...] + lnb_ref[...]
    z = z * jax.nn.sigmoid(z)
    return (_dot(z.astype(_BF16), wo_ref[...]) + bo_ref[...]).astype(_BF16)


HALO = 32
CONV_STEPS = 16
CH_TILES = D_CONV // LANES


def _conv_prompt_kernel(glu_ref, w8_ref, b8_ref, lng_ref, lnb_ref, wo_ref, bo_ref,
                        out_ref, hist_ref, y_ref, *, tm):
    i = pl.program_id(1)

    @pl.when(i == 0)
    def _():
        hist_ref[0:HALO * CH_TILES, :] = jnp.zeros((HALO * CH_TILES, LANES), _F32)

    @pl.when(i > 0)
    def _():
        hist_ref[0:HALO * CH_TILES, :] = hist_ref[tm * CH_TILES:(tm + HALO) * CH_TILES, :]

    for c in range(CH_TILES):
        hist_ref[pl.ds(HALO * CH_TILES + c, tm, stride=CH_TILES), :] = glu_ref[:, c * LANES:(c + 1) * LANES]

    first = HALO - (CONV_WIDTH - 1)
    span = CONV_STEPS + CONV_WIDTH - 1

    def chunk(ci, carry):
        t0 = ci * CONV_STEPS
        x = hist_ref[pl.ds(pl.multiple_of((t0 + first) * CH_TILES, CH_TILES), span * CH_TILES), :]
        x = x.reshape(span, CH_TILES, LANES)
        acc = jnp.broadcast_to(b8_ref[...][None], (CONV_STEPS, CH_TILES, LANES))
        for j in range(CONV_WIDTH):
            acc = acc + x[j:j + CONV_STEPS] * w8_ref[j][None]
        y_ref[pl.ds(pl.multiple_of(t0 * CH_TILES, CH_TILES), CONV_STEPS * CH_TILES), :] = (
            acc.reshape(CONV_STEPS * CH_TILES, LANES))
        return carry

    lax.fori_loop(0, tm // CONV_STEPS, chunk, 0)
    y = jnp.concatenate([y_ref[pl.ds(c, tm, stride=CH_TILES), :] for c in range(CH_TILES)], axis=1)
    out_ref[...] = _ln_swish_project(y, lng_ref, lnb_ref, wo_ref, bo_ref)


def _conv_prompt(glu, bsz, t, w_dw, b_dw, lng, lnb, wo_b, bo, tm):
    assert CH_TILES == SUBLANES
    nt = t // tm
    w8 = w_dw.reshape(CONV_WIDTH, CH_TILES, LANES)
    b8 = b_dw.reshape(CH_TILES, LANES)
    full = lambda a: pl.BlockSpec(a.shape, lambda b, i: (0,) * a.ndim)
    row = pl.BlockSpec((tm, D_CONV), lambda b, i: (b * nt + i, 0))
    return pl.pallas_call(
        functools.partial(_conv_prompt_kernel, tm=tm),
        grid=(bsz, nt),
        in_specs=[row, full(w8), full(b8), full(lng), full(lnb), full(wo_b), full(bo)],
        out_specs=pl.BlockSpec((tm, D_MODEL), lambda b, i: (b * nt + i, 0)),
        out_shape=jax.ShapeDtypeStruct((bsz * t, D_MODEL), _BF16),
        scratch_shapes=[pltpu.VMEM(((tm + HALO) * CH_TILES, LANES), _F32), pltpu.VMEM((tm * CH_TILES, LANES), _F32)],
        compiler_params=pltpu.CompilerParams(dimension_semantics=("arbitrary", "arbitrary"),
                                             vmem_limit_bytes=VMEM_LIMIT),
        name="conv_prompt",
    )(glu, w8, b8, lng, lnb, wo_b, bo)


def _conv_sample_kernel(hist_ref, wdw_ref, bdw_ref, lng_ref, lnb_ref, wo_ref, bo_ref, out_ref, *, steps):
    for t in range(steps):
        acc = hist_ref[t] * wdw_ref[0:1, :]
        for j in range(1, CONV_WIDTH):
            acc = acc + hist_ref[t + j] * wdw_ref[j:j + 1, :]
        out_ref[t] = _ln_swish_project(acc + bdw_ref[...], lng_ref, lnb_ref, wo_ref, bo_ref)


def _conv_sample(hist, w_dw, b_dw, lng, lnb, wo_b, bo, sb):
    rows, nseq, _ = hist.shape
    steps = rows - (CONV_WIDTH - 1)
    full = lambda a: pl.BlockSpec(a.shape, lambda i: (0,) * a.ndim)
    return pl.pallas_call(
        functools.partial(_conv_sample_kernel, steps=steps),
        grid=(nseq // sb,),
        in_specs=[pl.BlockSpec((rows, sb, D_CONV), lambda i: (0, i, 0)),
                  full(w_dw), full(b_dw), full(lng), full(lnb), full(wo_b), full(bo)],
        out_specs=pl.BlockSpec((steps, sb, D_MODEL), lambda i: (0, i, 0)),
        out_shape=jax.ShapeDtypeStruct((steps, nseq, D_MODEL), _BF16),
        compiler_params=pltpu.CompilerParams(dimension_semantics=("arbitrary",),
                                             vmem_limit_bytes=VMEM_LIMIT),
        name="conv_sample",
    )(hist, w_dw, b_dw, lng, lnb, wo_b, bo)


def _bucket_map():
    i = np.arange(WINDOW)[:, None]
    j = np.arange(WINDOW)[None, :]
    n = (i - j) % WINDOW
    nf = np.maximum(n, 1).astype(np.float32)
    large = MAX_EXACT + (np.log(nf / np.float32(MAX_EXACT)) / np.float32(math.log(MAX_DISTANCE / MAX_EXACT))
                         * np.float32(N_BUCKETS - MAX_EXACT)).astype(np.int32)
    return np.where(n < MAX_EXACT, n, np.minimum(large, N_BUCKETS - 1)).astype(np.int32)


def _bias_table_kernel(rb_ref, bm_ref, tbl_ref):
    p = pl.program_id(0)
    bm = bm_ref[...]
    for half in range(2):
        h = 2 * p + half
        t = jnp.zeros(bm.shape, _F32)
        for b in range(N_BUCKETS):
            t = jnp.where(bm == b, rb_ref[b, h], t)
        tbl_ref[0, :, half * WINDOW:(half + 1) * WINDOW] = t


def _bias_tables(rel_bias):
    bm = jnp.asarray(_bucket_map())
    return pl.pallas_call(
        _bias_table_kernel,
        grid=(N_PAIRS,),
        in_specs=[pl.BlockSpec(memory_space=pltpu.SMEM), pl.BlockSpec(bm.shape, lambda p: (0, 0))],
        out_specs=pl.BlockSpec((1, WINDOW, 2 * WINDOW), lambda p: (p, 0, 0)),
        out_shape=jax.ShapeDtypeStruct((N_PAIRS, WINDOW, 2 * WINDOW), _F32),
        name="bias_tables",
    )(rel_bias, bm)


def _block_diag_pairs(slab):
    low = lax.broadcasted_iota(jnp.int32, slab.shape, 1) < HEAD_DIM
    swapped = pltpu.roll(slab, HEAD_DIM, axis=1)
    zero = jnp.zeros_like(slab)
    first = jnp.concatenate([jnp.where(low, slab, zero), jnp.where(low, zero, swapped)], axis=0)
    second = jnp.concatenate([jnp.where(low, swapped, zero), jnp.where(low, zero, slab)], axis=0)
    return first.astype(_BF16), second.astype(_BF16)


def _kv_operands(k_blk, v_blk):
    ops = []
    for slab in range(KV_DIM // LANES):
        cols = slice(slab * LANES, (slab + 1) * LANES)
        ops.extend(zip(_block_diag_pairs(k_blk[:, cols]), _block_diag_pairs(v_blk[:, cols])))
    return ops


def _attend(q, prev_ops, own_ops, tbl_ref, sink_ref, prev_shift, store):
    tq = q.shape[0]
    rows = 2 * tq
    row = lax.broadcasted_iota(jnp.int32, (rows, 2 * WINDOW), 0)
    col = lax.broadcasted_iota(jnp.int32, (rows, 2 * WINDOW), 1)
    from_prev = (col & (WINDOW - 1)) > jnp.where(row >= tq, row - tq, row)
    top = lax.broadcasted_iota(jnp.int32, (rows, 1), 0) < tq
    low = lax.broadcasted_iota(jnp.int32, (rows, LANES), 1) < HEAD_DIM
    r2 = lax.broadcasted_iota(jnp.int32, (2 * WINDOW, LANES), 0)
    c2 = lax.broadcasted_iota(jnp.int32, (2 * WINDOW, LANES), 1)
    head_ones = jnp.where((r2 < WINDOW) == (c2 < HEAD_DIM), 1.0, 0.0).astype(_BF16)
    contract_last = (((1,), (1,)), ((), ()))
    for kvh in range(N_KV_HEADS):
        (k_prev, v_prev), (k_own, v_own) = prev_ops[kvh], own_ops[kvh]
        pair_a = 2 * kvh
        pair_b = pair_a + 1
        qq = jnp.concatenate([q[:, pair_a * LANES:(pair_a + 1) * LANES],
                              q[:, pair_b * LANES:(pair_b + 1) * LANES]], axis=0).astype(_BF16)
        sp = lax.dot_general(qq, k_prev, contract_last, preferred_element_type=_F32)
        so = lax.dot_general(qq, k_own, contract_last, preferred_element_type=_F32)
        bias = jnp.concatenate([tbl_ref[pair_a, 0:tq, :], tbl_ref[pair_b, 0:tq, :]], axis=0)
        s = jnp.where(from_prev, sp + prev_shift, so) + bias
        sink_even = jnp.where(top, sink_ref[2 * pair_a], sink_ref[2 * pair_b])
        sink_odd = jnp.where(top, sink_ref[2 * pair_a + 1], sink_ref[2 * pair_b + 1])
        m_even = jnp.maximum(jnp.max(s[:, :WINDOW], axis=-1, keepdims=True), sink_even)
        m_odd = jnp.maximum(jnp.max(s[:, WINDOW:], axis=-1, keepdims=True), sink_odd)
        p = jnp.exp(s - jnp.where(col < WINDOW, m_even, m_odd)).astype(_BF16)
        zero = jnp.zeros_like(p)
        o = _dot(jnp.where(from_prev, p, zero), v_prev) + _dot(jnp.where(from_prev, zero, p), v_own)
        den = _dot(p, head_ones) + jnp.where(low, jnp.exp(sink_even - m_even), jnp.exp(sink_odd - m_odd))
        o = o / den
        store(pair_a, o[:tq])
        store(pair_b, o[tq:])


PROMPT_QBLOCKS = 2


def _attn_prompt_kernel(sink_ref, q_ref, kp_ref, ko_ref, vp_ref, vo_ref, tbl_ref, o_ref):
    prev_shift = jnp.where(pl.program_id(1) == 0, MASK_VALUE, 0.0).astype(_F32)
    ops = [_kv_operands(kp_ref[...], vp_ref[...])]
    for b in range(PROMPT_QBLOCKS):
        rows = slice(b * WINDOW, (b + 1) * WINDOW)
        ops.append(_kv_operands(ko_ref[rows, :], vo_ref[rows, :]))

        def store(pair, o, rows=rows):
            o_ref[rows, pair * LANES:(pair + 1) * LANES] = o.astype(o_ref.dtype)

        _attend(q_ref[rows, :], ops[b], ops[b + 1], tbl_ref, sink_ref,
                prev_shift if b == 0 else jnp.float32(0.0), store)


def _attn_prompt(q, k, v, tbl, sinks, bsz, t):
    tq = PROMPT_QBLOCKS * WINDOW
    nb = t // tq
    own = lambda w: pl.BlockSpec((tq, w), lambda b, i: (b * nb + i, 0))
    prev = lambda w: pl.BlockSpec((WINDOW, w),
                                  lambda b, i: (PROMPT_QBLOCKS * (b * nb + i) - jnp.minimum(i, 1), 0))
    return pl.pallas_call(
        _attn_prompt_kernel,
        grid=(bsz, nb),
        in_specs=[pl.BlockSpec(memory_space=pltpu.SMEM), own(Q_DIM), prev(KV_DIM), own(KV_DIM),
                  prev(KV_DIM), own(KV_DIM), pl.BlockSpec(tbl.shape, lambda b, i: (0, 0, 0))],
        out_specs=own(Q_DIM),
        out_shape=jax.ShapeDtypeStruct((bsz * t, Q_DIM), _BF16),
        compiler_params=pltpu.CompilerParams(dimension_semantics=("arbitrary", "arbitrary"),
                                             vmem_limit_bytes=VMEM_LIMIT),
        name="attn_prompt",
    )(sinks, q, k, k, v, v, tbl)


SAMPLE_UNROLL = 4


def _attn_sample_kernel(sink_ref, q_ref, kn_ref, vn_ref, ck_ref, cv_ref, tbl_ref, o_ref, *, sb, steps):
    pad = jnp.zeros((WINDOW - steps, KV_DIM), _F32)

    def one_sequence(s, carry):
        def store(pair, o):
            o_ref[s, :, pair * LANES:(pair + 1) * LANES] = o

        own = _kv_operands(jnp.concatenate([kn_ref[s], pad], axis=0), jnp.concatenate([vn_ref[s], pad], axis=0))
        _attend(q_ref[s], _kv_operands(ck_ref[s], cv_ref[s]), own, tbl_ref, sink_ref, jnp.float32(0.0), store)
        return carry

    lax.fori_loop(0, sb, one_sequence, 0, unroll=SAMPLE_UNROLL)


def _attn_sample(q, k_new, v_new, cache_k, cache_v, tbl, sinks, sb):
    nseq, steps, _ = q.shape
    seq = lambda r, w: pl.BlockSpec((sb, r, w), lambda i: (i, 0, 0))
    return pl.pallas_call(
        functools.partial(_attn_sample_kernel, sb=sb, steps=steps),
        grid=(nseq // sb,),
        in_specs=[pl.BlockSpec(memory_space=pltpu.SMEM), seq(steps, Q_DIM), seq(steps, KV_DIM), seq(steps, KV_DIM),
                  seq(WINDOW, KV_DIM), seq(WINDOW, KV_DIM), pl.BlockSpec(tbl.shape, lambda i: (0, 0, 0))],
        out_specs=seq(steps, Q_DIM),
        out_shape=jax.ShapeDtypeStruct((nseq, steps, Q_DIM), _F32),
        compiler_params=pltpu.CompilerParams(dimension_semantics=("arbitrary",),
                                             vmem_limit_bytes=VMEM_LIMIT),
        name="attn_sample",
    )(sinks, q, k_new, v_new, cache_k, cache_v, tbl)


def _lane_min_index(mask, lane):
    return jnp.min(jnp.where(mask, lane, LANES), axis=-1, keepdims=True)


def _finish_kernel(x_ref, conv_ref, o_ref, ga_ref, gb_ref, wa_ref, wo_ref, ng_ref, wr_hi_ref, wr_lo_ref, br_ref,
                   tri_ref, h_ref, hn_ref, route_ref, count_ref, running_ref):
    @pl.when(pl.program_id(0) == 0)
    def _():
        running_ref[...] = jnp.zeros_like(running_ref)

    attn_out = _dot(o_ref[...].astype(_BF16), wa_ref[...])
    merged = ga_ref[...].astype(_F32) * conv_ref[...].astype(_F32) + gb_ref[...].astype(_F32) * attn_out
    h = x_ref[...] + _dot(merged.astype(_BF16), wo_ref[...])
    h_ref[...] = h
    hn = h * lax.rsqrt(jnp.mean(h * h, axis=-1, keepdims=True) + EPS) * ng_ref[...]
    hn_ref[...] = hn

    hi, lo = _split_bf16(hn)
    logits = _dot(hi, wr_hi_ref[...]) + _dot(lo, wr_hi_ref[...]) + _dot(hi, wr_lo_ref[...]) + br_ref[...]
    lane = lax.broadcasted_iota(jnp.int32, logits.shape, 1)
    gmask = lane < N_GROUPS
    gl = jnp.where(gmask, logits, MASK_VALUE)
    gmax = jnp.max(gl, axis=-1, keepdims=True)
    grp = _lane_min_index(gmask & (gl == gmax), lane)
    p_grp = 1.0 / jnp.sum(jnp.where(gmask, jnp.exp(gl - gmax), 0.0), axis=-1, keepdims=True)
    e_lo = N_GROUPS + grp * EXPERTS_PER_GROUP
    emask = (lane >= e_lo) & (lane < e_lo + EXPERTS_PER_GROUP)
    el = jnp.where(emask, logits, MASK_VALUE)
    ex = jnp.where(emask, jnp.exp(el - jnp.max(el, axis=-1, keepdims=True)), 0.0)
    prob = jnp.where(emask, ex / jnp.sum(ex, axis=-1, keepdims=True), -1.0)
    p1 = jnp.max(prob, axis=-1, keepdims=True)
    i1 = _lane_min_index(prob == p1, lane)
    rest = jnp.where(lane == i1, -1.0, prob)
    p2 = jnp.max(rest, axis=-1, keepdims=True)
    i2 = _lane_min_index(rest == p2, lane)
    w1 = p_grp * p1 / (p1 + p2)
    w2 = p_grp * p2 / (p1 + p2)
    e1 = i1 - N_GROUPS
    e2 = i2 - N_GROUPS

    hot1 = lane == e1
    hot2 = lane == e2
    hot = jnp.where(hot1 | hot2, 1.0, 0.0)
    before = _dot(tri_ref[...], hot.astype(_BF16)) + running_ref[...]
    rank1 = jnp.sum(jnp.where(hot1, before, 0.0), axis=-1, keepdims=True)
    rank2 = jnp.sum(jnp.where(hot2, before, 0.0), axis=-1, keepdims=True)
    running_ref[...] += jnp.sum(hot, axis=0, keepdims=True)
    count_ref[...] = jnp.broadcast_to(running_ref[...], count_ref.shape)

    fields = (e1.astype(_F32), e2.astype(_F32), w1, w2, rank1, rank2)
    route = jnp.zeros(logits.shape, _F32)
    for pos, val in enumerate(fields):
        route = jnp.where(lane == pos, val, route)
    route_ref[...] = route


ROUTE_E, ROUTE_W, ROUTE_RANK = 0, 2, 4
DEST_ROWS = 1024


def _finish(x, conv_out, o, ga, gb, wa_b, wo_b, ng, wr_hi, wr_lo, br, tm):
    n = x.shape[0]
    tri = jnp.asarray(np.tril(np.ones((tm, tm), np.float32), -1), _BF16)
    row = lambda w: pl.BlockSpec((tm, w), lambda i: (i, 0))
    full = lambda a: pl.BlockSpec(a.shape, lambda i: (0,) * a.ndim)
    return pl.pallas_call(
        _finish_kernel,
        grid=(n // tm,),
        in_specs=[row(D_MODEL), row(D_MODEL), row(Q_DIM), row(D_MODEL), row(D_MODEL),
                  full(wa_b), full(wo_b), full(ng), full(wr_hi), full(wr_lo), full(br), full(tri)],
        out_specs=[row(D_MODEL), row(D_MODEL), row(LANES), pl.BlockSpec((SUBLANES, LANES), lambda i: (0, 0))],
        out_shape=[jax.ShapeDtypeStruct((n, D_MODEL), _F32),
                   jax.ShapeDtypeStruct((n, D_MODEL), _F32),
                   jax.ShapeDtypeStruct((n, LANES), _F32),
                   jax.ShapeDtypeStruct((SUBLANES, LANES), _F32)],
        scratch_shapes=[pltpu.VMEM((1, LANES), _F32)],
        compiler_params=pltpu.CompilerParams(dimension_semantics=("arbitrary",),
                                             vmem_limit_bytes=VMEM_LIMIT),
        name="finish",
    )(x, conv_out, o, ga, gb, wa_b, wo_b, ng, wr_hi, wr_lo, br, tri)


def _dest_kernel(route_ref, starts_ref, dest_ref):
    route = route_ref[...]
    lane = lax.broadcasted_iota(jnp.int32, route.shape, 1)
    out = jnp.zeros(route.shape, jnp.int32)
    for j in range(TOP_K):
        e = route[:, ROUTE_E + j:ROUTE_E + j + 1].astype(jnp.int32)
        start = jnp.sum(jnp.where(lane == e, starts_ref[...], 0.0), axis=-1, keepdims=True)
        d = (start + route[:, ROUTE_RANK + j:ROUTE_RANK + j + 1]).astype(jnp.int32)
        out = jnp.where(lane == j, d, out)
    dest_ref[...] = out


def _dest(route, starts_row, tm):
    n = route.shape[0]
    row = pl.BlockSpec((tm, LANES), lambda i: (i, 0))
    return pl.pallas_call(
        _dest_kernel,
        grid=(n // tm,),
        in_specs=[row, pl.BlockSpec((1, LANES), lambda i: (0, 0))],
        out_specs=row,
        out_shape=jax.ShapeDtypeStruct((n, LANES), jnp.int32),
        name="dest",
    )(route, starts_row)


SC_CORES = 2
SC_SUBCORES = 16
SC_WORKERS = SC_CORES * SC_SUBCORES
SC_CHUNK = 32


def _sc_gather_rows(table, idx):
    b = idx.shape[0]
    d = table.shape[1]
    per_worker = b // SC_WORKERS
    n_chunks = per_worker // SC_CHUNK
    assert per_worker * SC_WORKERS == b and n_chunks * SC_CHUNK == per_worker, (b,)
    mesh = plsc.VectorSubcoreMesh(core_axis_name="c", subcore_axis_name="s",
                                  num_cores=SC_CORES, num_subcores=SC_SUBCORES)

    @functools.partial(
        pl.kernel, mesh=mesh,
        out_type=jax.ShapeDtypeStruct((b, d), table.dtype),
        scratch_types=[pltpu.VMEM((SC_CHUNK,), jnp.int32), pltpu.VMEM((SC_CHUNK, d), table.dtype),
                       pltpu.SemaphoreType.DMA],
        name="sc_gather_rows",
    )
    def gather(table_hbm, idx_hbm, out_hbm, idx_v, rows_v, sem):
        worker = lax.axis_index("s") * SC_CORES + lax.axis_index("c")

        @pl.loop(0, n_chunks)
        def _(c):
            base = pl.multiple_of(worker * per_worker + c * SC_CHUNK, SC_CHUNK)
            pltpu.sync_copy(idx_hbm.at[pl.ds(base, SC_CHUNK)], idx_v)
            pltpu.async_copy(table_hbm.at[idx_v], rows_v, sem).wait()
            pltpu.sync_copy(rows_v, out_hbm.at[pl.ds(base, SC_CHUNK)])

    return gather(table, idx)


def _sc_scatter_rows(src, idx):
    n, d = src.shape
    b = idx.shape[0]
    per_worker = b // SC_WORKERS
    n_chunks = per_worker // SC_CHUNK
    assert per_worker * SC_WORKERS == b and n_chunks * SC_CHUNK == per_worker and n % per_worker == 0, (b, n)
    mesh = plsc.VectorSubcoreMesh(core_axis_name="c", subcore_axis_name="s",
                                  num_cores=SC_CORES, num_subcores=SC_SUBCORES)

    @functools.partial(
        pl.kernel, mesh=mesh,
        out_type=jax.ShapeDtypeStruct((b, d), src.dtype),
        scratch_types=[pltpu.VMEM((SC_CHUNK,), jnp.int32), pltpu.VMEM((SC_CHUNK, d), src.dtype)],
        name="sc_scatter_rows",
    )
    def scatter(src_hbm, idx_hbm, out_hbm, idx_v, rows_v):
        worker = lax.axis_index("s") * SC_CORES + lax.axis_index("c")

        @pl.loop(0, n_chunks)
        def _(c):
            base = pl.multiple_of(worker * per_worker + c * SC_CHUNK, SC_CHUNK)
            src_base = pl.multiple_of(lax.rem(base, n), SC_CHUNK)
            pltpu.sync_copy(idx_hbm.at[pl.ds(base, SC_CHUNK)], idx_v)
            pltpu.sync_copy(src_hbm.at[pl.ds(src_base, SC_CHUNK)], rows_v)
            pltpu.sync_copy(rows_v, out_hbm.at[idx_v])

    return scatter(src, idx)


def _expert_kernel(blk_ref, exp_ref, lo_ref, hi_ref, x_ref, wg_ref, wu_ref, wd_ref, yb_ref,
                   wg_b, wu_b, wd_b, held_ref):
    del blk_ref
    k = pl.program_id(0)
    lo, hi, e = lo_ref[k], hi_ref[k], exp_ref[k]

    @pl.when(k == 0)
    def _():
        held_ref[0] = -1

    @pl.when(hi > lo)
    def _():
        @pl.when(held_ref[0] != e)
        def _():
            wg_b[...] = wg_ref[0].astype(_BF16)
            wu_b[...] = wu_ref[0].astype(_BF16)
            wd_b[...] = wd_ref[0].astype(_BF16)
            held_ref[0] = e

        xb = x_ref[...].astype(_BF16)
        g = _dot(xb, wg_b[...])
        u = _dot(xb, wu_b[...])
        hid = g * jax.nn.sigmoid(g) * u
        y = _dot(hid.astype(_BF16), wd_b[...])
        r = lax.broadcasted_iota(jnp.int32, y.shape, 0)
        pltpu.store(yb_ref, y, mask=(r >= lo) & (r < hi))


def _experts(items, xs, w_gate, w_up, w_down):
    n_items = items[0].shape[0]
    wspec = lambda a: pl.BlockSpec((1,) + a.shape[1:], lambda k, blk, exp, lo, hi: (exp[k], 0, 0))
    rows = pl.BlockSpec((MOE_ROWS, D_MODEL), lambda k, blk, exp, lo, hi: (blk[k], 0))
    grid_spec = pltpu.PrefetchScalarGridSpec(
        num_scalar_prefetch=4,
        grid=(n_items,),
        in_specs=[rows, wspec(w_gate), wspec(w_up), wspec(w_down)],
        out_specs=rows,
        scratch_shapes=[pltpu.VMEM(w_gate.shape[1:], _BF16), pltpu.VMEM(w_up.shape[1:], _BF16),
                        pltpu.VMEM(w_down.shape[1:], _BF16), pltpu.SMEM((1,), jnp.int32)],
    )
    return pl.pallas_call(
        _expert_kernel,
        grid_spec=grid_spec,
        out_shape=jax.ShapeDtypeStruct(xs.shape, _F32),
        compiler_params=pltpu.CompilerParams(dimension_semantics=("arbitrary",),
                                             vmem_limit_bytes=VMEM_LIMIT),
        name="experts",
    )(*items, xs, w_gate, w_up, w_down)


def _combine_kernel(h_ref, route_ref, g0_ref, g1_ref, y_ref):
    route = route_ref[...]
    y_ref[...] = h_ref[...] + route[:, 2:3] * g0_ref[...] + route[:, 3:4] * g1_ref[...]


def _combine(h, route, g, tm):
    n = h.shape[0]
    nt = n // tm
    row = lambda w: pl.BlockSpec((tm, w), lambda i: (i, 0))
    return pl.pallas_call(
        _combine_kernel,
        grid=(nt,),
        in_specs=[row(D_MODEL), row(LANES), row(D_MODEL), pl.BlockSpec((tm, D_MODEL), lambda i: (nt + i, 0))],
        out_specs=row(D_MODEL),
        out_shape=jax.ShapeDtypeStruct((n, D_MODEL), _F32),
        compiler_params=pltpu.CompilerParams(dimension_semantics=("arbitrary",),
                                             vmem_limit_bytes=VMEM_LIMIT),
        name="combine",
    )(h, route, g, g)


def _work_items(counts, n_pairs):
    n_blocks = n_pairs // MOE_ROWS
    starts = jnp.cumsum(counts) - counts
    cuts = jnp.sort(jnp.concatenate([jnp.arange(n_blocks, dtype=jnp.int32) * MOE_ROWS, starts]))
    ends = jnp.concatenate([cuts[1:], jnp.full((1,), n_pairs, jnp.int32)])
    blk = jnp.minimum(cuts // MOE_ROWS, n_blocks - 1)
    expert = jnp.clip(jnp.sum(starts[None, :] <= cuts[:, None], axis=1) - 1, 0, N_EXPERTS - 1).astype(jnp.int32)
    return starts, (blk, expert, cuts - blk * MOE_ROWS, ends - blk * MOE_ROWS)


def _moe(h, hn, route, counts_rows, wg_b, wu_b, wd_b, tm):
    n = h.shape[0]
    n_pairs = n * TOP_K
    counts = counts_rows[0, :N_EXPERTS].astype(jnp.int32)
    starts, items = _work_items(counts, n_pairs)
    starts_row = jnp.zeros((1, LANES), _F32).at[0, :N_EXPERTS].set(starts.astype(_F32))
    dest = _dest(route, starts_row, min(n, DEST_ROWS))[:, :TOP_K].T.reshape(n_pairs)
    xs = _sc_scatter_rows(hn, dest)
    yb = _experts(items, xs, wg_b, wu_b, wd_b)
    g = _sc_gather_rows(yb, dest)
    return _combine(h, route, g, tm)


def kernel(x_prompt, x_sample, state_conv, cache_k, cache_v, norm_attn_g, w_in, q_norm_g, k_norm_g, rel_bias, attn_sinks, w_dw, b_dw, conv_ln_g, conv_ln_b, w_conv_out, b_conv_out, w_attn_out, w_out, norm_ffn_g, w_grp, b_grp, w_router, b_router, w_gate, w_up, w_down):
    bsz, t, _ = x_prompt.shape
    nseq, steps, _ = x_sample.shape
    row = lambda a: a.reshape(1, -1).astype(_F32)

    w_in_b = w_in.astype(_BF16)
    wco_b = w_conv_out.astype(_BF16)
    wa_b = w_attn_out.astype(_BF16)
    wo_b = w_out.astype(_BF16)
    qg = row(jnp.tile(q_norm_g, N_HEADS)) * (HEAD_DIM ** -0.5)
    kg = row(jnp.tile(k_norm_g, N_KV_HEADS))
    group_of_lane = np.arange(Q_DIM) // HEAD_DIM
    gsum = jnp.asarray(group_of_lane[:, None] == np.arange(LANES)[None, :], _BF16)
    gexp = jnp.asarray(np.arange(LANES)[:, None] == group_of_lane[None, :], _BF16)
    w_rt = jnp.zeros((D_MODEL, LANES), _F32).at[:, :N_GROUPS].set(w_grp).at[:, N_GROUPS:N_GROUPS + N_EXPERTS].set(w_router)
    wr_hi = w_rt.astype(_BF16)
    wr_lo = (w_rt - wr_hi.astype(_F32)).astype(_BF16)
    b_rt = jnp.zeros((1, LANES), _F32).at[0, :N_GROUPS].set(b_grp).at[0, N_GROUPS:N_GROUPS + N_EXPERTS].set(b_router)
    tbl = _bias_tables(rel_bias)
    conv_params = (w_dw, row(b_dw), row(conv_ln_g), row(conv_ln_b), wco_b, row(b_conv_out))

    def finish_and_moe(x2d, conv_out, o, ga, gb, tm):
        h, hn, route, counts = _finish(x2d, conv_out, o, ga, gb, wa_b, wo_b, row(norm_ffn_g), wr_hi, wr_lo, b_rt, tm)
        return _moe(h, hn, route, counts, w_gate, w_up, w_down, tm)

    xp = x_prompt.reshape(bsz * t, D_MODEL)
    glu, q, k, v, ga, gb = _inproj(xp, row(norm_attn_g), w_in_b, qg, kg, gsum, gexp, _BF16, 256)
    conv_out = _conv_prompt(glu, bsz, t, *conv_params, 256)
    o = _attn_prompt(q, k, v, tbl, attn_sinks, bsz, t)
    y_prompt = finish_and_moe(xp, conv_out, o, ga, gb, 256).reshape(bsz, t, D_MODEL)
    glu3 = glu.reshape(bsz, t, D_CONV)
    state_conv_prompt = glu3[:, t - (CONV_WIDTH - 1):]
    tail = lambda a: a.reshape(bsz, t, KV_DIM)[:, t - WINDOW:].reshape(bsz, WINDOW, N_KV_HEADS, HEAD_DIM)
    cache_k_prompt, cache_v_prompt = tail(k), tail(v)

    xs = x_sample.reshape(nseq * steps, D_MODEL)
    glu, q, k, v, ga, gb = _inproj(xs, row(norm_attn_g), w_in_b, qg, kg, gsum, gexp, _F32, 256)
    glu3 = glu.reshape(nseq, steps, D_CONV)
    hist = jnp.concatenate([state_conv, glu3], axis=1)
    conv_out = _conv_sample(hist.transpose(1, 0, 2), *conv_params, 64)
    conv_out = conv_out.transpose(1, 0, 2).reshape(nseq * steps, D_MODEL)
    k3 = k.reshape(nseq, steps, KV_DIM)
    v3 = v.reshape(nseq, steps, KV_DIM)
    o = _attn_sample(q.reshape(nseq, steps, Q_DIM), k3, v3, cache_k.reshape(nseq, WINDOW, KV_DIM),
                     cache_v.reshape(nseq, WINDOW, KV_DIM), tbl, attn_sinks, 8)
    y_sample = finish_and_moe(xs, conv_out, o.reshape(nseq * steps, Q_DIM), ga, gb, 256).reshape(nseq, steps, D_MODEL)
    state_conv_sample = hist[:, steps:]
    cache_k_sample = jnp.concatenate([cache_k, k3.reshape(nseq, steps, N_KV_HEADS, HEAD_DIM)], axis=1)[:, steps:]
    cache_v_sample = jnp.concatenate([cache_v, v3.reshape(nseq, steps, N_KV_HEADS, HEAD_DIM)], axis=1)[:, steps:]

    return (y_prompt, y_sample, state_conv_prompt, cache_k_prompt, cache_v_prompt,
            state_conv_sample, cache_k_sample, cache_v_sample)
```

```python
import functools
import math

import numpy as np
import jax
import jax.numpy as jnp
from jax import lax
from jax.experimental import pallas as pl
from jax.experimental.pallas import tpu as pltpu
from jax.experimental.pallas import tpu_sc as plsc

D_MODEL = 1024
N_HEADS = 16
HEAD_DIM = 64
N_KV_HEADS = 4
WINDOW = 128
Q_DIM = N_HEADS * HEAD_DIM
KV_DIM = N_KV_HEADS * HEAD_DIM
N_BUCKETS = 32
MAX_EXACT = N_BUCKETS // 2
MAX_DISTANCE = 128
D_CONV = D_MODEL
CONV_WIDTH = 31
N_GROUPS = 4
EXPERTS_PER_GROUP = 8
N_EXPERTS = N_GROUPS * EXPERTS_PER_GROUP
TOP_K = 2
D_EXPERT = 256
EPS = 1e-6

LANES = 128
SUBLANES = 8
N_PAIRS = N_HEADS // 2
MOE_ROWS = 256
MASK_VALUE = -1e30
VMEM_LIMIT = 48 * 1024 * 1024

_F32 = jnp.float32
_BF16 = jnp.bfloat16


def _dot(a, b):
    return jnp.dot(a, b, preferred_element_type=_F32)


def _split_bf16(x):
    hi = x.astype(_BF16)
    lo = (x - hi.astype(_F32)).astype(_BF16)
    return hi, lo


HALF = D_MODEL // 2


def _pack_bf16_pairs(x):
    lo = pltpu.bitcast(x[:, :HALF].astype(_BF16).astype(_F32), jnp.uint32)
    hi = pltpu.bitcast(x[:, HALF:].astype(_BF16).astype(_F32), jnp.uint32)
    return hi | (lo >> 16)


def _unpack_bf16_pairs(w):
    lo = pltpu.bitcast(w << 16, _F32)
    hi = pltpu.bitcast(w & jnp.uint32(0xFFFF0000), _F32)
    return jnp.concatenate([lo, hi], axis=1)


def _head_rms_scale(z, gsum, gexp):
    hi, lo = _split_bf16(z * z)
    ssum = _dot(hi, gsum) + _dot(lo, gsum)
    r = lax.rsqrt(ssum * (1.0 / HEAD_DIM) + EPS)
    rhi, rlo = _split_bf16(r)
    return _dot(rhi, gexp) + _dot(rlo, gexp)


def _inproj_kernel(x_ref, g_ref, w_ref, qg_ref, kg_ref, gsum_ref, gexp_ref,
                   glu_ref, q_ref, k_ref, v_ref, ga_ref, gb_ref):
    x = x_ref[...]
    xn = x * lax.rsqrt(jnp.mean(x * x, axis=-1, keepdims=True) + EPS) * g_ref[...]
    xb = xn.astype(_BF16)

    def seg(lo, width):
        return _dot(xb, w_ref[:, lo:lo + width])

    a = seg(0, D_CONV)
    b = seg(D_CONV, D_CONV)
    glu_ref[...] = a * jax.nn.sigmoid(b)
    off = 2 * D_CONV
    q = seg(off, Q_DIM)
    q_ref[...] = (q * _head_rms_scale(q, gsum_ref[...], gexp_ref[...]) * qg_ref[...]).astype(q_ref.dtype)
    off += Q_DIM
    k = seg(off, KV_DIM)
    k_ref[...] = k * _head_rms_scale(k, gsum_ref[:KV_DIM, :], gexp_ref[:, :KV_DIM]) * kg_ref[...]
    off += KV_DIM
    v_ref[...] = seg(off, KV_DIM)
    off += KV_DIM
    ga_ref[...] = jax.nn.sigmoid(seg(off, D_MODEL)).astype(ga_ref.dtype)
    off += D_MODEL
    gb_ref[...] = jax.nn.sigmoid(seg(off, D_MODEL)).astype(gb_ref.dtype)


def _inproj(x, g, w_in_b, qg, kg, gsum, gexp, q_dtype, tm):
    n = x.shape[0]
    in_dim = w_in_b.shape[1]
    row = lambda w: pl.BlockSpec((tm, w), lambda i: (i, 0))
    full = lambda a: pl.BlockSpec(a.shape, lambda i: (0,) * a.ndim)
    return pl.pallas_call(
        _inproj_kernel,
        grid=(n // tm,),
        in_specs=[row(D_MODEL), full(g), full(w_in_b), full(qg), full(kg), full(gsum), full(gexp)],
        out_specs=[row(D_CONV), row(Q_DIM), row(KV_DIM), row(KV_DIM), row(D_MODEL), row(D_MODEL)],
        out_shape=[jax.ShapeDtypeStruct((n, D_CONV), _F32),
                   jax.ShapeDtypeStruct((n, Q_DIM), q_dtype),
                   jax.ShapeDtypeStruct((n, KV_DIM), _F32),
                   jax.ShapeDtypeStruct((n, KV_DIM), _F32),
                   jax.ShapeDtypeStruct((n, D_MODEL), _BF16),
                   jax.ShapeDtypeStruct((n, D_MODEL), _BF16)],
        compiler_params=pltpu.CompilerParams(dimension_semantics=("arbitrary",),
                                             vmem_limit_bytes=VMEM_LIMIT),
        name="inproj",
    )(x, g, w_in_b, qg, kg, gsum, gexp)


def _ln_swish_project(y, lng_ref, lnb_ref, wo_ref, bo_ref):
    mu = jnp.mean(y, axis=-1, keepdims=True)
    yc = y - mu
    var = jnp.mean(yc * yc, axis=-1, keepdims=True)
    z = yc * lax.rsqrt(var + EPS) * lng_ref[...] + lnb_ref[...]
    z = z * jax.nn.sigmoid(z)
    return (_dot(z.astype(_BF16), wo_ref[...]) + bo_ref[...]).astype(_BF16)


HALO = 32
CONV_STEPS = 16
CH_TILES = D_CONV // LANES


def _conv_prompt_kernel(glu_ref, w8_ref, b8_ref, lng_ref, lnb_ref, wo_ref, bo_ref,
                        out_ref, hist_ref, y_ref, *, tm):
    i = pl.program_id(1)

    @pl.when(i == 0)
    def _():
        hist_ref[0:HALO * CH_TILES, :] = jnp.zeros((HALO * CH_TILES, LANES), _F32)

    @pl.when(i > 0)
    def _():
        hist_ref[0:HALO * CH_TILES, :] = hist_ref[tm * CH_TILES:(tm + HALO) * CH_TILES, :]

    for c in range(CH_TILES):
        hist_ref[pl.ds(HALO * CH_TILES + c, tm, stride=CH_TILES), :] = glu_ref[:, c * LANES:(c + 1) * LANES]

    first = HALO - (CONV_WIDTH - 1)
    span = CONV_STEPS + CONV_WIDTH - 1

    def chunk(ci, carry):
        t0 = ci * CONV_STEPS
        x = hist_ref[pl.ds(pl.multiple_of((t0 + first) * CH_TILES, CH_TILES), span * CH_TILES), :]
        x = x.reshape(span, CH_TILES, LANES)
        acc = jnp.broadcast_to(b8_ref[...][None], (CONV_STEPS, CH_TILES, LANES))
        for j in range(CONV_WIDTH):
            acc = acc + x[j:j + CONV_STEPS] * w8_ref[j][None]
        y_ref[pl.ds(pl.multiple_of(t0 * CH_TILES, CH_TILES), CONV_STEPS * CH_TILES), :] = (
            acc.reshape(CONV_STEPS * CH_TILES, LANES))
        return carry

    lax.fori_loop(0, tm // CONV_STEPS, chunk, 0)
    y = jnp.concatenate([y_ref[pl.ds(c, tm, stride=CH_TILES), :] for c in range(CH_TILES)], axis=1)
    out_ref[...] = _ln_swish_project(y, lng_ref, lnb_ref, wo_ref, bo_ref)


def _conv_prompt(glu, bsz, t, w_dw, b_dw, lng, lnb, wo_b, bo, tm):
    assert CH_TILES == SUBLANES
    nt = t // tm
    w8 = w_dw.reshape(CONV_WIDTH, CH_TILES, LANES)
    b8 = b_dw.reshape(CH_TILES, LANES)
    full = lambda a: pl.BlockSpec(a.shape, lambda b, i: (0,) * a.ndim)
    row = pl.BlockSpec((tm, D_CONV), lambda b, i: (b * nt + i, 0))
    return pl.pallas_call(
        functools.partial(_conv_prompt_kernel, tm=tm),
        grid=(bsz, nt),
        in_specs=[row, full(w8), full(b8), full(lng), full(lnb), full(wo_b), full(bo)],
        out_specs=pl.BlockSpec((tm, D_MODEL), lambda b, i: (b * nt + i, 0)),
        out_shape=jax.ShapeDtypeStruct((bsz * t, D_MODEL), _BF16),
        scratch_shapes=[pltpu.VMEM(((tm + HALO) * CH_TILES, LANES), _F32), pltpu.VMEM((tm * CH_TILES, LANES), _F32)],
        compiler_params=pltpu.CompilerParams(dimension_semantics=("arbitrary", "arbitrary"),
                                             vmem_limit_bytes=VMEM_LIMIT),
        name="conv_prompt",
    )(glu, w8, b8, lng, lnb, wo_b, bo)


def _conv_sample_kernel(hist_ref, wdw_ref, bdw_ref, lng_ref, lnb_ref, wo_ref, bo_ref, out_ref, *, steps):
    for t in range(steps):
        acc = hist_ref[t] * wdw_ref[0:1, :]
        for j in range(1, CONV_WIDTH):
            acc = acc + hist_ref[t + j] * wdw_ref[j:j + 1, :]
        out_ref[t] = _ln_swish_project(acc + bdw_ref[...], lng_ref, lnb_ref, wo_ref, bo_ref)


def _conv_sample(hist, w_dw, b_dw, lng, lnb, wo_b, bo, sb):
    rows, nseq, _ = hist.shape
    steps = rows - (CONV_WIDTH - 1)
    full = lambda a: pl.BlockSpec(a.shape, lambda i: (0,) * a.ndim)
    return pl.pallas_call(
        functools.partial(_conv_sample_kernel, steps=steps),
        grid=(nseq // sb,),
        in_specs=[pl.BlockSpec((rows, sb, D_CONV), lambda i: (0, i, 0)),
                  full(w_dw), full(b_dw), full(lng), full(lnb), full(wo_b), full(bo)],
        out_specs=pl.BlockSpec((steps, sb, D_MODEL), lambda i: (0, i, 0)),
        out_shape=jax.ShapeDtypeStruct((steps, nseq, D_MODEL), _BF16),
        compiler_params=pltpu.CompilerParams(dimension_semantics=("arbitrary",),
                                             vmem_limit_bytes=VMEM_LIMIT),
        name="conv_sample",
    )(hist, w_dw, b_dw, lng, lnb, wo_b, bo)


def _bucket_map():
    i = np.arange(WINDOW)[:, None]
    j = np.arange(WINDOW)[None, :]
    n = (i - j) % WINDOW
    nf = np.maximum(n, 1).astype(np.float32)
    large = MAX_EXACT + (np.log(nf / np.float32(MAX_EXACT)) / np.float32(math.log(MAX_DISTANCE / MAX_EXACT))
                         * np.float32(N_BUCKETS - MAX_EXACT)).astype(np.int32)
    return np.where(n < MAX_EXACT, n, np.minimum(large, N_BUCKETS - 1)).astype(np.int32)


def _bias_table_kernel(rb_ref, bm_ref, tbl_ref):
    p = pl.program_id(0)
    bm = bm_ref[...]
    for half in range(2):
        h = 2 * p + half
        t = jnp.zeros(bm.shape, _F32)
        for b in range(N_BUCKETS):
            t = jnp.where(bm == b, rb_ref[b, h], t)
        tbl_ref[0, :, half * WINDOW:(half + 1) * WINDOW] = t


def _bias_tables(rel_bias):
    bm = jnp.asarray(_bucket_map())
    return pl.pallas_call(
        _bias_table_kernel,
        grid=(N_PAIRS,),
        in_specs=[pl.BlockSpec(memory_space=pltpu.SMEM), pl.BlockSpec(bm.shape, lambda p: (0, 0))],
        out_specs=pl.BlockSpec((1, WINDOW, 2 * WINDOW), lambda p: (p, 0, 0)),
        out_shape=jax.ShapeDtypeStruct((N_PAIRS, WINDOW, 2 * WINDOW), _F32),
        name="bias_tables",
    )(rel_bias, bm)


def _block_diag_pairs(slab):
    low = lax.broadcasted_iota(jnp.int32, slab.shape, 1) < HEAD_DIM
    swapped = pltpu.roll(slab, HEAD_DIM, axis=1)
    zero = jnp.zeros_like(slab)
    first = jnp.concatenate([jnp.where(low, slab, zero), jnp.where(low, zero, swapped)], axis=0)
    second = jnp.concatenate([jnp.where(low, swapped, zero), jnp.where(low, zero, slab)], axis=0)
    return first.astype(_BF16), second.astype(_BF16)


def _kv_operands(k_blk, v_blk):
    ops = []
    for slab in range(KV_DIM // LANES):
        cols = slice(slab * LANES, (slab + 1) * LANES)
        ops.extend(zip(_block_diag_pairs(k_blk[:, cols]), _block_diag_pairs(v_blk[:, cols])))
    return ops


def _attend(q, prev_ops, own_ops, tbl_ref, sink_ref, prev_shift, store):
    tq = q.shape[0]
    rows = 2 * tq
    row = lax.broadcasted_iota(jnp.int32, (rows, 2 * WINDOW), 0)
    col = lax.broadcasted_iota(jnp.int32, (rows, 2 * WINDOW), 1)
    from_prev = (col & (WINDOW - 1)) > jnp.where(row >= tq, row - tq, row)
    top = lax.broadcasted_iota(jnp.int32, (rows, 1), 0) < tq
    low = lax.broadcasted_iota(jnp.int32, (rows, LANES), 1) < HEAD_DIM
    r2 = lax.broadcasted_iota(jnp.int32, (2 * WINDOW, LANES), 0)
    c2 = lax.broadcasted_iota(jnp.int32, (2 * WINDOW, LANES), 1)
    head_ones = jnp.where((r2 < WINDOW) == (c2 < HEAD_DIM), 1.0, 0.0).astype(_BF16)
    contract_last = (((1,), (1,)), ((), ()))
    for kvh in range(N_KV_HEADS):
        (k_prev, v_prev), (k_own, v_own) = prev_ops[kvh], own_ops[kvh]
        pair_a = 2 * kvh
        pair_b = pair_a + 1
        qq = jnp.concatenate([q[:, pair_a * LANES:(pair_a + 1) * LANES],
                              q[:, pair_b * LANES:(pair_b + 1) * LANES]], axis=0).astype(_BF16)
        sp = lax.dot_general(qq, k_prev, contract_last, preferred_element_type=_F32)
        so = lax.dot_general(qq, k_own, contract_last, preferred_element_type=_F32)
        bias = jnp.concatenate([tbl_ref[pair_a, 0:tq, :], tbl_ref[pair_b, 0:tq, :]], axis=0)
        s = jnp.where(from_prev, sp + prev_shift, so) + bias
        sink_even = jnp.where(top, sink_ref[2 * pair_a], sink_ref[2 * pair_b])
        sink_odd = jnp.where(top, sink_ref[2 * pair_a + 1], sink_ref[2 * pair_b + 1])
        m_even = jnp.maximum(jnp.max(s[:, :WINDOW], axis=-1, keepdims=True), sink_even)
        m_odd = jnp.maximum(jnp.max(s[:, WINDOW:], axis=-1, keepdims=True), sink_odd)
        p = jnp.exp(s - jnp.where(col < WINDOW, m_even, m_odd)).astype(_BF16)
        zero = jnp.zeros_like(p)
        o = _dot(jnp.where(from_prev, p, zero), v_prev) + _dot(jnp.where(from_prev, zero, p), v_own)
        den = _dot(p, head_ones) + jnp.where(low, jnp.exp(sink_even - m_even), jnp.exp(sink_odd - m_odd))
        o = o / den
        store(pair_a, o[:tq])
        store(pair_b, o[tq:])


PROMPT_QBLOCKS = 2


def _attn_prompt_kernel(sink_ref, q_ref, kp_ref, ko_ref, vp_ref, vo_ref, tbl_ref, o_ref):
    prev_shift = jnp.where(pl.program_id(1) == 0, MASK_VALUE, 0.0).astype(_F32)
    ops = [_kv_operands(kp_ref[...], vp_ref[...])]
    for b in range(PROMPT_QBLOCKS):
        rows = slice(b * WINDOW, (b + 1) * WINDOW)
        ops.append(_kv_operands(ko_ref[rows, :], vo_ref[rows, :]))

        def store(pair, o, rows=rows):
            o_ref[rows, pair * LANES:(pair + 1) * LANES] = o.astype(o_ref.dtype)

        _attend(q_ref[rows, :], ops[b], ops[b + 1], tbl_ref, sink_ref,
                prev_shift if b == 0 else jnp.float32(0.0), store)


def _attn_prompt(q, k, v, tbl, sinks, bsz, t):
    tq = PROMPT_QBLOCKS * WINDOW
    nb = t // tq
    own = lambda w: pl.BlockSpec((tq, w), lambda b, i: (b * nb + i, 0))
    prev = lambda w: pl.BlockSpec((WINDOW, w),
                                  lambda b, i: (PROMPT_QBLOCKS * (b * nb + i) - jnp.minimum(i, 1), 0))
    return pl.pallas_call(
        _attn_prompt_kernel,
        grid=(bsz, nb),
        in_specs=[pl.BlockSpec(memory_space=pltpu.SMEM), own(Q_DIM), prev(KV_DIM), own(KV_DIM),
                  prev(KV_DIM), own(KV_DIM), pl.BlockSpec(tbl.shape, lambda b, i: (0, 0, 0))],
        out_specs=own(Q_DIM),
        out_shape=jax.ShapeDtypeStruct((bsz * t, Q_DIM), _BF16),
        compiler_params=pltpu.CompilerParams(dimension_semantics=("arbitrary", "arbitrary"),
                                             vmem_limit_bytes=VMEM_LIMIT),
        name="attn_prompt",
    )(sinks, q, k, k, v, v, tbl)


SAMPLE_UNROLL = 4


def _attn_sample_kernel(sink_ref, q_ref, kn_ref, vn_ref, ck_ref, cv_ref, tbl_ref, o_ref, *, sb, steps):
    pad = jnp.zeros((WINDOW - steps, KV_DIM), _F32)

    def one_sequence(s, carry):
        def store(pair, o):
            o_ref[s, :, pair * LANES:(pair + 1) * LANES] = o

        own = _kv_operands(jnp.concatenate([kn_ref[s], pad], axis=0), jnp.concatenate([vn_ref[s], pad], axis=0))
        _attend(q_ref[s], _kv_operands(ck_ref[s], cv_ref[s]), own, tbl_ref, sink_ref, jnp.float32(0.0), store)
        return carry

    lax.fori_loop(0, sb, one_sequence, 0, unroll=SAMPLE_UNROLL)


def _attn_sample(q, k_new, v_new, cache_k, cache_v, tbl, sinks, sb):
    nseq, steps, _ = q.shape
    seq = lambda r, w: pl.BlockSpec((sb, r, w), lambda i: (i, 0, 0))
    return pl.pallas_call(
        functools.partial(_attn_sample_kernel, sb=sb, steps=steps),
        grid=(nseq // sb,),
        in_specs=[pl.BlockSpec(memory_space=pltpu.SMEM), seq(steps, Q_DIM), seq(steps, KV_DIM), seq(steps, KV_DIM),
                  seq(WINDOW, KV_DIM), seq(WINDOW, KV_DIM), pl.BlockSpec(tbl.shape, lambda i: (0, 0, 0))],
        out_specs=seq(steps, Q_DIM),
        out_shape=jax.ShapeDtypeStruct((nseq, steps, Q_DIM), _F32),
        compiler_params=pltpu.CompilerParams(dimension_semantics=("arbitrary",),
                                             vmem_limit_bytes=VMEM_LIMIT),
        name="attn_sample",
    )(sinks, q, k_new, v_new, cache_k, cache_v, tbl)


def _lane_min_index(mask, lane):
    return jnp.min(jnp.where(mask, lane, LANES), axis=-1, keepdims=True)


def _finish_kernel(x_ref, conv_ref, o_ref, ga_ref, gb_ref, wa_ref, wo_ref, ng_ref, wr_hi_ref, wr_lo_ref, br_ref,
                   tri_ref, h_ref, hn_ref, route_ref, count_ref, running_ref):
    @pl.when(pl.program_id(0) == 0)
    def _():
        running_ref[...] = jnp.zeros_like(running_ref)

    attn_out = _dot(o_ref[...].astype(_BF16), wa_ref[...])
    merged = ga_ref[...].astype(_F32) * conv_ref[...].astype(_F32) + gb_ref[...].astype(_F32) * attn_out
    h = x_ref[...] + _dot(merged.astype(_BF16), wo_ref[...])
    h_ref[...] = h
    hn = h * lax.rsqrt(jnp.mean(h * h, axis=-1, keepdims=True) + EPS) * ng_ref[...]
    hn_ref[...] = _pack_bf16_pairs(hn)

    hi, lo = _split_bf16(hn)
    logits = _dot(hi, wr_hi_ref[...]) + _dot(lo, wr_hi_ref[...]) + _dot(hi, wr_lo_ref[...]) + br_ref[...]
    lane = lax.broadcasted_iota(jnp.int32, logits.shape, 1)
    gmask = lane < N_GROUPS
    gl = jnp.where(gmask, logits, MASK_VALUE)
    gmax = jnp.max(gl, axis=-1, keepdims=True)
    grp = _lane_min_index(gmask & (gl == gmax), lane)
    p_grp = 1.0 / jnp.sum(jnp.where(gmask, jnp.exp(gl - gmax), 0.0), axis=-1, keepdims=True)
    e_lo = N_GROUPS + grp * EXPERTS_PER_GROUP
    emask = (lane >= e_lo) & (lane < e_lo + EXPERTS_PER_GROUP)
    el = jnp.where(emask, logits, MASK_VALUE)
    ex = jnp.where(emask, jnp.exp(el - jnp.max(el, axis=-1, keepdims=True)), 0.0)
    prob = jnp.where(emask, ex / jnp.sum(ex, axis=-1, keepdims=True), -1.0)
    p1 = jnp.max(prob, axis=-1, keepdims=True)
    i1 = _lane_min_index(prob == p1, lane)
    rest = jnp.where(lane == i1, -1.0, prob)
    p2 = jnp.max(rest, axis=-1, keepdims=True)
    i2 = _lane_min_index(rest == p2, lane)
    w1 = p_grp * p1 / (p1 + p2)
    w2 = p_grp * p2 / (p1 + p2)
    e1 = i1 - N_GROUPS
    e2 = i2 - N_GROUPS

    hot1 = lane == e1
    hot2 = lane == e2
    hot = jnp.where(hot1 | hot2, 1.0, 0.0)
    before = _dot(tri_ref[...], hot.astype(_BF16)) + running_ref[...]
    rank1 = jnp.sum(jnp.where(hot1, before, 0.0), axis=-1, keepdims=True)
    rank2 = jnp.sum(jnp.where(hot2, before, 0.0), axis=-1, keepdims=True)
    running_ref[...] += jnp.sum(hot, axis=0, keepdims=True)
    count_ref[...] = jnp.broadcast_to(running_ref[...], count_ref.shape)

    fields = (e1.astype(_F32), e2.astype(_F32), w1, w2, rank1, rank2)
    route = jnp.zeros(logits.shape, _F32)
    for pos, val in enumerate(fields):
        route = jnp.where(lane == pos, val, route)
    route_ref[...] = route


ROUTE_E, ROUTE_W, ROUTE_RANK = 0, 2, 4
DEST_ROWS = 1024


def _finish(x, conv_out, o, ga, gb, wa_b, wo_b, ng, wr_hi, wr_lo, br, tm):
    n = x.shape[0]
    tri = jnp.asarray(np.tril(np.ones((tm, tm), np.float32), -1), _BF16)
    row = lambda w: pl.BlockSpec((tm, w), lambda i: (i, 0))
    full = lambda a: pl.BlockSpec(a.shape, lambda i: (0,) * a.ndim)
    return pl.pallas_call(
        _finish_kernel,
        grid=(n // tm,),
        in_specs=[row(D_MODEL), row(D_MODEL), row(Q_DIM), row(D_MODEL), row(D_MODEL),
                  full(wa_b), full(wo_b), full(ng), full(wr_hi), full(wr_lo), full(br), full(tri)],
        out_specs=[row(D_MODEL), row(HALF), row(LANES), pl.BlockSpec((SUBLANES, LANES), lambda i: (0, 0))],
        out_shape=[jax.ShapeDtypeStruct((n, D_MODEL), _F32),
                   jax.ShapeDtypeStruct((n, HALF), jnp.uint32),
                   jax.ShapeDtypeStruct((n, LANES), _F32),
                   jax.ShapeDtypeStruct((SUBLANES, LANES), _F32)],
        scratch_shapes=[pltpu.VMEM((1, LANES), _F32)],
        compiler_params=pltpu.CompilerParams(dimension_semantics=("arbitrary",),
                                             vmem_limit_bytes=VMEM_LIMIT),
        name="finish",
    )(x, conv_out, o, ga, gb, wa_b, wo_b, ng, wr_hi, wr_lo, br, tri)


def _dest_kernel(route_ref, starts_ref, dest_ref):
    route = route_ref[...]
    lane = lax.broadcasted_iota(jnp.int32, route.shape, 1)
    out = jnp.zeros(route.shape, jnp.int32)
    for j in range(TOP_K):
        e = route[:, ROUTE_E + j:ROUTE_E + j + 1].astype(jnp.int32)
        start = jnp.sum(jnp.where(lane == e, starts_ref[...], 0.0), axis=-1, keepdims=True)
        d = (start + route[:, ROUTE_RANK + j:ROUTE_RANK + j + 1]).astype(jnp.int32)
        out = jnp.where(lane == j, d, out)
    dest_ref[...] = out


def _dest(route, starts_row, tm):
    n = route.shape[0]
    row = pl.BlockSpec((tm, LANES), lambda i: (i, 0))
    return pl.pallas_call(
        _dest_kernel,
        grid=(n // tm,),
        in_specs=[row, pl.BlockSpec((1, LANES), lambda i: (0, 0))],
        out_specs=row,
        out_shape=jax.ShapeDtypeStruct((n, LANES), jnp.int32),
        name="dest",
    )(route, starts_row)


SC_CORES = 2
SC_SUBCORES = 16
SC_WORKERS = SC_CORES * SC_SUBCORES
SC_CHUNK_BYTES = 128 * 1024


def _sc_chunk(per_worker, row_bytes):
    chunk = min(per_worker, SC_CHUNK_BYTES // row_bytes)
    assert per_worker % chunk == 0 and chunk % SUBLANES == 0, (per_worker, row_bytes)
    return chunk


def _sc_gather_rows(table, idx):
    b = idx.shape[0]
    d = table.shape[1]
    per_worker = b // SC_WORKERS
    assert per_worker * SC_WORKERS == b, (b,)
    chunk = _sc_chunk(per_worker, d * 4)
    n_chunks = per_worker // chunk
    mesh = plsc.VectorSubcoreMesh(core_axis_name="c", subcore_axis_name="s",
                                  num_cores=SC_CORES, num_subcores=SC_SUBCORES)

    @functools.partial(
        pl.kernel, mesh=mesh,
        out_type=jax.ShapeDtypeStruct((b, d), table.dtype),
        scratch_types=[pltpu.VMEM((chunk,), jnp.int32), pltpu.VMEM((chunk, d), table.dtype),
                       pltpu.SemaphoreType.DMA],
        name="sc_gather_rows",
    )
    def gather(table_hbm, idx_hbm, out_hbm, idx_v, rows_v, sem):
        worker = lax.axis_index("s") * SC_CORES + lax.axis_index("c")

        @pl.loop(0, n_chunks)
        def _(c):
            base = pl.multiple_of(worker * per_worker + c * chunk, chunk)
            pltpu.sync_copy(idx_hbm.at[pl.ds(base, chunk)], idx_v)
            pltpu.async_copy(table_hbm.at[idx_v], rows_v, sem).wait()
            pltpu.sync_copy(rows_v, out_hbm.at[pl.ds(base, chunk)])

    return gather(table, idx)


def _sc_scatter_rows(src, idx):
    n, d = src.shape
    b = idx.shape[0]
    per_worker = b // SC_WORKERS
    assert per_worker * SC_WORKERS == b and n % per_worker == 0, (b, n)
    chunk = _sc_chunk(per_worker, d * 4)
    n_chunks = per_worker // chunk
    mesh = plsc.VectorSubcoreMesh(core_axis_name="c", subcore_axis_name="s",
                                  num_cores=SC_CORES, num_subcores=SC_SUBCORES)

    @functools.partial(
        pl.kernel, mesh=mesh,
        out_type=jax.ShapeDtypeStruct((b, d), src.dtype),
        scratch_types=[pltpu.VMEM((chunk,), jnp.int32), pltpu.VMEM((chunk, d), src.dtype)],
        name="sc_scatter_rows",
    )
    def scatter(src_hbm, idx_hbm, out_hbm, idx_v, rows_v):
        worker = lax.axis_index("s") * SC_CORES + lax.axis_index("c")

        @pl.loop(0, n_chunks)
        def _(c):
            base = pl.multiple_of(worker * per_worker + c * chunk, chunk)
            src_base = pl.multiple_of(lax.rem(base, n), chunk)
            pltpu.sync_copy(idx_hbm.at[pl.ds(base, chunk)], idx_v)
            pltpu.sync_copy(src_hbm.at[pl.ds(src_base, chunk)], rows_v)
            pltpu.sync_copy(rows_v, out_hbm.at[idx_v])

    return scatter(src, idx)


def _expert_kernel(blk_ref, exp_ref, lo_ref, hi_ref, x_ref, wg_ref, wu_ref, wd_ref, yb_ref,
                   wg_b, wu_b, wd_b, held_ref):
    del blk_ref
    k = pl.program_id(0)
    lo, hi, e = lo_ref[k], hi_ref[k], exp_ref[k]

    @pl.when(k == 0)
    def _():
        held_ref[0] = -1

    @pl.when(hi > lo)
    def _():
        @pl.when(held_ref[0] != e)
        def _():
            wg_b[...] = wg_ref[0].astype(_BF16)
            wu_b[...] = wu_ref[0].astype(_BF16)
            wd_b[...] = wd_ref[0].astype(_BF16)
            held_ref[0] = e

        xb = _unpack_bf16_pairs(x_ref[...]).astype(_BF16)
        g = _dot(xb, wg_b[...])
        u = _dot(xb, wu_b[...])
        hid = g * jax.nn.sigmoid(g) * u
        y = _dot(hid.astype(_BF16), wd_b[...])
        r = lax.broadcasted_iota(jnp.int32, yb_ref.shape, 0)
        pltpu.store(yb_ref, _pack_bf16_pairs(y), mask=(r >= lo) & (r < hi))


def _experts(items, xs, w_gate, w_up, w_down):
    n_items = items[0].shape[0]
    wspec = lambda a: pl.BlockSpec((1,) + a.shape[1:], lambda k, blk, exp, lo, hi: (exp[k], 0, 0))
    rows = pl.BlockSpec((MOE_ROWS, HALF), lambda k, blk, exp, lo, hi: (blk[k], 0))
    grid_spec = pltpu.PrefetchScalarGridSpec(
        num_scalar_prefetch=4,
        grid=(n_items,),
        in_specs=[rows, wspec(w_gate), wspec(w_up), wspec(w_down)],
        out_specs=rows,
        scratch_shapes=[pltpu.VMEM(w_gate.shape[1:], _BF16), pltpu.VMEM(w_up.shape[1:], _BF16),
                        pltpu.VMEM(w_down.shape[1:], _BF16), pltpu.SMEM((1,), jnp.int32)],
    )
    return pl.pallas_call(
        _expert_kernel,
        grid_spec=grid_spec,
        out_shape=jax.ShapeDtypeStruct(xs.shape, xs.dtype),
        compiler_params=pltpu.CompilerParams(dimension_semantics=("arbitrary",),
                                             vmem_limit_bytes=VMEM_LIMIT),
        name="experts",
    )(*items, xs, w_gate, w_up, w_down)


def _combine_kernel(h_ref, route_ref, g0_ref, g1_ref, y_ref):
    route = route_ref[...]
    y_ref[...] = (h_ref[...] + route[:, ROUTE_W:ROUTE_W + 1] * _unpack_bf16_pairs(g0_ref[...])
                  + route[:, ROUTE_W + 1:ROUTE_W + 2] * _unpack_bf16_pairs(g1_ref[...]))


def _combine(h, route, g, tm):
    n = h.shape[0]
    nt = n // tm
    row = lambda w: pl.BlockSpec((tm, w), lambda i: (i, 0))
    return pl.pallas_call(
        _combine_kernel,
        grid=(nt,),
        in_specs=[row(D_MODEL), row(LANES), row(HALF), pl.BlockSpec((tm, HALF), lambda i: (nt + i, 0))],
        out_specs=row(D_MODEL),
        out_shape=jax.ShapeDtypeStruct((n, D_MODEL), _F32),
        compiler_params=pltpu.CompilerParams(dimension_semantics=("arbitrary",),
                                             vmem_limit_bytes=VMEM_LIMIT),
        name="combine",
    )(h, route, g, g)


def _work_items(counts, n_pairs):
    n_blocks = n_pairs // MOE_ROWS
    starts = jnp.cumsum(counts) - counts
    cuts = jnp.sort(jnp.concatenate([jnp.arange(n_blocks, dtype=jnp.int32) * MOE_ROWS, starts]))
    ends = jnp.concatenate([cuts[1:], jnp.full((1,), n_pairs, jnp.int32)])
    blk = jnp.minimum(cuts // MOE_ROWS, n_blocks - 1)
    expert = jnp.clip(jnp.sum(starts[None, :] <= cuts[:, None], axis=1) - 1, 0, N_EXPERTS - 1).astype(jnp.int32)
    return starts, (blk, expert, cuts - blk * MOE_ROWS, ends - blk * MOE_ROWS)


def _moe(h, hn, route, counts_rows, wg_b, wu_b, wd_b, tm):
    n = h.shape[0]
    n_pairs = n * TOP_K
    counts = counts_rows[0, :N_EXPERTS].astype(jnp.int32)
    starts, items = _work_items(counts, n_pairs)
    starts_row = jnp.zeros((1, LANES), _F32).at[0, :N_EXPERTS].set(starts.astype(_F32))
    dest = _dest(route, starts_row, min(n, DEST_ROWS))[:, :TOP_K].T.reshape(n_pairs)
    xs = _sc_scatter_rows(hn, dest)
    yb = _experts(items, xs, wg_b, wu_b, wd_b)
    g = _sc_gather_rows(yb, dest)
    return _combine(h, route, g, tm)


def kernel(x_prompt, x_sample, state_conv, cache_k, cache_v, norm_attn_g, w_in, q_norm_g, k_norm_g, rel_bias, attn_sinks, w_dw, b_dw, conv_ln_g, conv_ln_b, w_conv_out, b_conv_out, w_attn_out, w_out, norm_ffn_g, w_grp, b_grp, w_router, b_router, w_gate, w_up, w_down):
    bsz, t, _ = x_prompt.shape
    nseq, steps, _ = x_sample.shape
    row = lambda a: a.reshape(1, -1).astype(_F32)

    w_in_b = w_in.astype(_BF16)
    wco_b = w_conv_out.astype(_BF16)
    wa_b = w_attn_out.astype(_BF16)
    wo_b = w_out.astype(_BF16)
    qg = row(jnp.tile(q_norm_g, N_HEADS)) * (HEAD_DIM ** -0.5)
    kg = row(jnp.tile(k_norm_g, N_KV_HEADS))
    group_of_lane = np.arange(Q_DIM) // HEAD_DIM
    gsum = jnp.asarray(group_of_lane[:, None] == np.arange(LANES)[None, :], _BF16)
    gexp = jnp.asarray(np.arange(LANES)[:, None] == group_of_lane[None, :], _BF16)
    w_rt = jnp.zeros((D_MODEL, LANES), _F32).at[:, :N_GROUPS].set(w_grp).at[:, N_GROUPS:N_GROUPS + N_EXPERTS].set(w_router)
    wr_hi = w_rt.astype(_BF16)
    wr_lo = (w_rt - wr_hi.astype(_F32)).astype(_BF16)
    b_rt = jnp.zeros((1, LANES), _F32).at[0, :N_GROUPS].set(b_grp).at[0, N_GROUPS:N_GROUPS + N_EXPERTS].set(b_router)
    tbl = _bias_tables(rel_bias)
    conv_params = (w_dw, row(b_dw), row(conv_ln_g), row(conv_ln_b), wco_b, row(b_conv_out))

    def finish_and_moe(x2d, conv_out, o, ga, gb, tm):
        h, hn, route, counts = _finish(x2d, conv_out, o, ga, gb, wa_b, wo_b, row(norm_ffn_g), wr_hi, wr_lo, b_rt, tm)
        return _moe(h, hn, route, counts, w_gate, w_up, w_down, tm)

    xp = x_prompt.reshape(bsz * t, D_MODEL)
    glu, q, k, v, ga, gb = _inproj(xp, row(norm_attn_g), w_in_b, qg, kg, gsum, gexp, _BF16, 256)
    conv_out = _conv_prompt(glu, bsz, t, *conv_params, 256)
    o = _attn_prompt(q, k, v, tbl, attn_sinks, bsz, t)
    y_prompt = finish_and_moe(xp, conv_out, o, ga, gb, 256).reshape(bsz, t, D_MODEL)
    glu3 = glu.reshape(bsz, t, D_CONV)
    state_conv_prompt = glu3[:, t - (CONV_WIDTH - 1):]
    tail = lambda a: a.reshape(bsz, t, KV_DIM)[:, t - WINDOW:].reshape(bsz, WINDOW, N_KV_HEADS, HEAD_DIM)
    cache_k_prompt, cache_v_prompt = tail(k), tail(v)

    xs = x_sample.reshape(nseq * steps, D_MODEL)
    glu, q, k, v, ga, gb = _inproj(xs, row(norm_attn_g), w_in_b, qg, kg, gsum, gexp, _F32, 256)
    glu3 = glu.reshape(nseq, steps, D_CONV)
    hist = jnp.concatenate([state_conv, glu3], axis=1)
    conv_out = _conv_sample(hist.transpose(1, 0, 2), *conv_params, 64)
    conv_out = conv_out.transpose(1, 0, 2).reshape(nseq * steps, D_MODEL)
    k3 = k.reshape(nseq, steps, KV_DIM)
    v3 = v.reshape(nseq, steps, KV_DIM)
    o = _attn_sample(q.reshape(nseq, steps, Q_DIM), k3, v3, cache_k.reshape(nseq, WINDOW, KV_DIM),
                     cache_v.reshape(nseq, WINDOW, KV_DIM), tbl, attn_sinks, 8)
    y_sample = finish_and_moe(xs, conv_out, o.reshape(nseq * steps, Q_DIM), ga, gb, 256).reshape(nseq, steps, D_MODEL)
    state_conv_sample = hist[:, steps:]
    cache_k_sample = jnp.concatenate([cache_k, k3.reshape(nseq, steps, N_KV_HEADS, HEAD_DIM)], axis=1)[:, steps:]
    cache_v_sample = jnp.concatenate([cache_v, v3.reshape(nseq, steps, N_KV_HEADS, HEAD_DIM)], axis=1)[:, steps:]

    return (y_prompt, y_sample, state_conv_prompt, cache_k_prompt, cache_v_prompt,
            state_conv_sample, cache_k_sample, cache_v_sample)
```

```python
import functools
import math

import numpy as np
import jax
import jax.numpy as jnp
from jax import lax
from jax.experimental import pallas as pl
from jax.experimental.pallas import tpu as pltpu
from jax.experimental.pallas import tpu_sc as plsc

D_MODEL = 1024
N_HEADS = 16
HEAD_DIM = 64
N_KV_HEADS = 4
WINDOW = 128
Q_DIM = N_HEADS * HEAD_DIM
KV_DIM = N_KV_HEADS * HEAD_DIM
N_BUCKETS = 32
MAX_EXACT = N_BUCKETS // 2
MAX_DISTANCE = 128
D_CONV = D_MODEL
CONV_WIDTH = 31
N_GROUPS = 4
EXPERTS_PER_GROUP = 8
N_EXPERTS = N_GROUPS * EXPERTS_PER_GROUP
TOP_K = 2
D_EXPERT = 256
EPS = 1e-6

LANES = 128
SUBLANES = 8
N_PAIRS = N_HEADS // 2
MOE_ROWS = 256
MASK_VALUE = -1e30
VMEM_LIMIT = 56 * 1024 * 1024
ROW_TILE = 512

_F32 = jnp.float32
_BF16 = jnp.bfloat16


def _resident(a):
    return pl.BlockSpec(a.shape, lambda *_: (0,) * a.ndim, pipeline_mode=pl.Buffered(1))


def _dot(a, b):
    return jnp.dot(a, b, preferred_element_type=_F32)


def _split_bf16(x):
    hi = x.astype(_BF16)
    lo = (x - hi.astype(_F32)).astype(_BF16)
    return hi, lo


HALF = D_MODEL // 2


def _pack_bf16_pairs(x):
    lo = pltpu.bitcast(x[:, :HALF].astype(_BF16).astype(_F32), jnp.uint32)
    hi = pltpu.bitcast(x[:, HALF:].astype(_BF16).astype(_F32), jnp.uint32)
    return hi | (lo >> 16)


def _unpack_bf16_pairs(w):
    lo = pltpu.bitcast(w << 16, _F32)
    hi = pltpu.bitcast(w & jnp.uint32(0xFFFF0000), _F32)
    return jnp.concatenate([lo, hi], axis=1)


def _head_rms_scale(z, gsum, gexp):
    hi, lo = _split_bf16(z * z)
    ssum = _dot(hi, gsum) + _dot(lo, gsum)
    r = lax.rsqrt(ssum * (1.0 / HEAD_DIM) + EPS)
    rhi, rlo = _split_bf16(r)
    return _dot(rhi, gexp) + _dot(rlo, gexp)


def _inproj_kernel(x_ref, g_ref, w_ref, qg_ref, kg_ref, gsum_ref, gexp_ref,
                   glu_ref, q_ref, k_ref, v_ref, ga_ref, gb_ref):
    x = x_ref[...]
    xn = x * lax.rsqrt(jnp.mean(x * x, axis=-1, keepdims=True) + EPS) * g_ref[...]
    xb = xn.astype(_BF16)

    def seg(lo, width):
        return _dot(xb, w_ref[:, lo:lo + width])

    a = seg(0, D_CONV)
    b = seg(D_CONV, D_CONV)
    glu_ref[...] = a * jax.nn.sigmoid(b)
    off = 2 * D_CONV
    q = seg(off, Q_DIM)
    q_ref[...] = (q * _head_rms_scale(q, gsum_ref[...], gexp_ref[...]) * qg_ref[...]).astype(q_ref.dtype)
    off += Q_DIM
    k = seg(off, KV_DIM)
    k_ref[...] = k * _head_rms_scale(k, gsum_ref[:KV_DIM, :], gexp_ref[:, :KV_DIM]) * kg_ref[...]
    off += KV_DIM
    v_ref[...] = seg(off, KV_DIM)
    off += KV_DIM
    ga_ref[...] = jax.nn.sigmoid(seg(off, D_MODEL)).astype(ga_ref.dtype)
    off += D_MODEL
    gb_ref[...] = jax.nn.sigmoid(seg(off, D_MODEL)).astype(gb_ref.dtype)


def _inproj(x, g, w_in_b, qg, kg, gsum, gexp, q_dtype, tm):
    n = x.shape[0]
    in_dim = w_in_b.shape[1]
    row = lambda w: pl.BlockSpec((tm, w), lambda i: (i, 0))
    full = _resident
    return pl.pallas_call(
        _inproj_kernel,
        grid=(n // tm,),
        in_specs=[row(D_MODEL), full(g), full(w_in_b), full(qg), full(kg), full(gsum), full(gexp)],
        out_specs=[row(D_CONV), row(Q_DIM), row(KV_DIM), row(KV_DIM), row(D_MODEL), row(D_MODEL)],
        out_shape=[jax.ShapeDtypeStruct((n, D_CONV), _F32),
                   jax.ShapeDtypeStruct((n, Q_DIM), q_dtype),
                   jax.ShapeDtypeStruct((n, KV_DIM), _F32),
                   jax.ShapeDtypeStruct((n, KV_DIM), _F32),
                   jax.ShapeDtypeStruct((n, D_MODEL), _BF16),
                   jax.ShapeDtypeStruct((n, D_MODEL), _BF16)],
        compiler_params=pltpu.CompilerParams(dimension_semantics=("arbitrary",),
                                             vmem_limit_bytes=VMEM_LIMIT),
        name="inproj",
    )(x, g, w_in_b, qg, kg, gsum, gexp)


def _ln_swish_project(y, lng_ref, lnb_ref, wo_ref, bo_ref):
    mu = jnp.mean(y, axis=-1, keepdims=True)
    yc = y - mu
    var = jnp.mean(yc * yc, axis=-1, keepdims=True)
    z = yc * lax.rsqrt(var + EPS) * lng_ref[...] + lnb_ref[...]
    z = z * jax.nn.sigmoid(z)
    return (_dot(z.astype(_BF16), wo_ref[...]) + bo_ref[...]).astype(_BF16)


HALO = 32
CONV_STEPS = 16
CH_TILES = D_CONV // LANES


def _conv_prompt_kernel(glu_ref, w8_ref, b8_ref, lng_ref, lnb_ref, wo_ref, bo_ref,
                        out_ref, hist_ref, y_ref, *, tm):
    i = pl.program_id(1)

    @pl.when(i == 0)
    def _():
        hist_ref[0:HALO * CH_TILES, :] = jnp.zeros((HALO * CH_TILES, LANES), _F32)

    @pl.when(i > 0)
    def _():
        hist_ref[0:HALO * CH_TILES, :] = hist_ref[tm * CH_TILES:(tm + HALO) * CH_TILES, :]

    for c in range(CH_TILES):
        hist_ref[pl.ds(HALO * CH_TILES + c, tm, stride=CH_TILES), :] = glu_ref[:, c * LANES:(c + 1) * LANES]

    first = HALO - (CONV_WIDTH - 1)
    span = CONV_STEPS + CONV_WIDTH - 1

    def chunk(ci, carry):
        t0 = ci * CONV_STEPS
        x = hist_ref[pl.ds(pl.multiple_of((t0 + first) * CH_TILES, CH_TILES), span * CH_TILES), :]
        x = x.reshape(span, CH_TILES, LANES)
        acc = jnp.broadcast_to(b8_ref[...][None], (CONV_STEPS, CH_TILES, LANES))
        for j in range(CONV_WIDTH):
            acc = acc + x[j:j + CONV_STEPS] * w8_ref[j][None]
        y_ref[pl.ds(pl.multiple_of(t0 * CH_TILES, CH_TILES), CONV_STEPS * CH_TILES), :] = (
            acc.reshape(CONV_STEPS * CH_TILES, LANES))
        return carry

    lax.fori_loop(0, tm // CONV_STEPS, chunk, 0)
    y = jnp.concatenate([y_ref[pl.ds(c, tm, stride=CH_TILES), :] for c in range(CH_TILES)], axis=1)
    out_ref[...] = _ln_swish_project(y, lng_ref, lnb_ref, wo_ref, bo_ref)


def _conv_prompt(glu, bsz, t, w_dw, b_dw, lng, lnb, wo_b, bo, tm):
    assert CH_TILES == SUBLANES
    nt = t // tm
    w8 = w_dw.reshape(CONV_WIDTH, CH_TILES, LANES)
    b8 = b_dw.reshape(CH_TILES, LANES)
    full = _resident
    row = pl.BlockSpec((tm, D_CONV), lambda b, i: (b * nt + i, 0))
    return pl.pallas_call(
        functools.partial(_conv_prompt_kernel, tm=tm),
        grid=(bsz, nt),
        in_specs=[row, full(w8), full(b8), full(lng), full(lnb), full(wo_b), full(bo)],
        out_specs=pl.BlockSpec((tm, D_MODEL), lambda b, i: (b * nt + i, 0)),
        out_shape=jax.ShapeDtypeStruct((bsz * t, D_MODEL), _BF16),
        scratch_shapes=[pltpu.VMEM(((tm + HALO) * CH_TILES, LANES), _F32), pltpu.VMEM((tm * CH_TILES, LANES), _F32)],
        compiler_params=pltpu.CompilerParams(dimension_semantics=("arbitrary", "arbitrary"),
                                             vmem_limit_bytes=VMEM_LIMIT),
        name="conv_prompt",
    )(glu, w8, b8, lng, lnb, wo_b, bo)


def _conv_sample_kernel(hist_ref, wdw_ref, bdw_ref, lng_ref, lnb_ref, wo_ref, bo_ref, out_ref, *, steps):
    for t in range(steps):
        acc = hist_ref[t] * wdw_ref[0:1, :]
        for j in range(1, CONV_WIDTH):
            acc = acc + hist_ref[t + j] * wdw_ref[j:j + 1, :]
        out_ref[t] = _ln_swish_project(acc + bdw_ref[...], lng_ref, lnb_ref, wo_ref, bo_ref)


def _conv_sample(hist, w_dw, b_dw, lng, lnb, wo_b, bo, sb):
    rows, nseq, _ = hist.shape
    steps = rows - (CONV_WIDTH - 1)
    full = _resident
    return pl.pallas_call(
        functools.partial(_conv_sample_kernel, steps=steps),
        grid=(nseq // sb,),
        in_specs=[pl.BlockSpec((rows, sb, D_CONV), lambda i: (0, i, 0)),
                  full(w_dw), full(b_dw), full(lng), full(lnb), full(wo_b), full(bo)],
        out_specs=pl.BlockSpec((steps, sb, D_MODEL), lambda i: (0, i, 0)),
        out_shape=jax.ShapeDtypeStruct((steps, nseq, D_MODEL), _BF16),
        compiler_params=pltpu.CompilerParams(dimension_semantics=("arbitrary",),
                                             vmem_limit_bytes=VMEM_LIMIT),
        name="conv_sample",
    )(hist, w_dw, b_dw, lng, lnb, wo_b, bo)


def _bucket_map():
    i = np.arange(WINDOW)[:, None]
    j = np.arange(WINDOW)[None, :]
    n = (i - j) % WINDOW
    nf = np.maximum(n, 1).astype(np.float32)
    large = MAX_EXACT + (np.log(nf / np.float32(MAX_EXACT)) / np.float32(math.log(MAX_DISTANCE / MAX_EXACT))
                         * np.float32(N_BUCKETS - MAX_EXACT)).astype(np.int32)
    return np.where(n < MAX_EXACT, n, np.minimum(large, N_BUCKETS - 1)).astype(np.int32)


def _bias_table_kernel(rb_ref, bm_ref, tbl_ref):
    p = pl.program_id(0)
    bm = bm_ref[...]
    for half in range(2):
        h = 2 * p + half
        t = jnp.zeros(bm.shape, _F32)
        for b in range(N_BUCKETS):
            t = jnp.where(bm == b, rb_ref[b, h], t)
        tbl_ref[0, :, half * WINDOW:(half + 1) * WINDOW] = t


def _bias_tables(rel_bias):
    bm = jnp.asarray(_bucket_map())
    return pl.pallas_call(
        _bias_table_kernel,
        grid=(N_PAIRS,),
        in_specs=[pl.BlockSpec(memory_space=pltpu.SMEM), pl.BlockSpec(bm.shape, lambda p: (0, 0))],
        out_specs=pl.BlockSpec((1, WINDOW, 2 * WINDOW), lambda p: (p, 0, 0)),
        out_shape=jax.ShapeDtypeStruct((N_PAIRS, WINDOW, 2 * WINDOW), _F32),
        name="bias_tables",
    )(rel_bias, bm)


def _block_diag_pairs(slab):
    low = lax.broadcasted_iota(jnp.int32, slab.shape, 1) < HEAD_DIM
    swapped = pltpu.roll(slab, HEAD_DIM, axis=1)
    zero = jnp.zeros_like(slab)
    first = jnp.concatenate([jnp.where(low, slab, zero), jnp.where(low, zero, swapped)], axis=0)
    second = jnp.concatenate([jnp.where(low, swapped, zero), jnp.where(low, zero, slab)], axis=0)
    return first.astype(_BF16), second.astype(_BF16)


def _kv_operands(k_blk, v_blk):
    ops = []
    for slab in range(KV_DIM // LANES):
        cols = slice(slab * LANES, (slab + 1) * LANES)
        ops.extend(zip(_block_diag_pairs(k_blk[:, cols]), _block_diag_pairs(v_blk[:, cols])))
    return ops


def _attend(q, prev_ops, own_ops, tbl_ref, sink_ref, prev_shift, store):
    tq = q.shape[0]
    rows = 2 * tq
    row = lax.broadcasted_iota(jnp.int32, (rows, 2 * WINDOW), 0)
    col = lax.broadcasted_iota(jnp.int32, (rows, 2 * WINDOW), 1)
    from_prev = (col & (WINDOW - 1)) > jnp.where(row >= tq, row - tq, row)
    top = lax.broadcasted_iota(jnp.int32, (rows, 1), 0) < tq
    low = lax.broadcasted_iota(jnp.int32, (rows, LANES), 1) < HEAD_DIM
    r2 = lax.broadcasted_iota(jnp.int32, (2 * WINDOW, LANES), 0)
    c2 = lax.broadcasted_iota(jnp.int32, (2 * WINDOW, LANES), 1)
    head_ones = jnp.where((r2 < WINDOW) == (c2 < HEAD_DIM), 1.0, 0.0).astype(_BF16)
    contract_last = (((1,), (1,)), ((), ()))
    for kvh in range(N_KV_HEADS):
        (k_prev, v_prev), (k_own, v_own) = prev_ops[kvh], own_ops[kvh]
        pair_a = 2 * kvh
        pair_b = pair_a + 1
        qq = jnp.concatenate([q[:, pair_a * LANES:(pair_a + 1) * LANES],
                              q[:, pair_b * LANES:(pair_b + 1) * LANES]], axis=0).astype(_BF16)
        sp = lax.dot_general(qq, k_prev, contract_last, preferred_element_type=_F32)
        so = lax.dot_general(qq, k_own, contract_last, preferred_element_type=_F32)
        bias = jnp.concatenate([tbl_ref[pair_a, 0:tq, :], tbl_ref[pair_b, 0:tq, :]], axis=0)
        s = jnp.where(from_prev, sp + prev_shift, so) + bias
        sink_even = jnp.where(top, sink_ref[2 * pair_a], sink_ref[2 * pair_b])
        sink_odd = jnp.where(top, sink_ref[2 * pair_a + 1], sink_ref[2 * pair_b + 1])
        m_even = jnp.maximum(jnp.max(s[:, :WINDOW], axis=-1, keepdims=True), sink_even)
        m_odd = jnp.maximum(jnp.max(s[:, WINDOW:], axis=-1, keepdims=True), sink_odd)
        p = jnp.exp(s - jnp.where(col < WINDOW, m_even, m_odd)).astype(_BF16)
        zero = jnp.zeros_like(p)
        o = _dot(jnp.where(from_prev, p, zero), v_prev) + _dot(jnp.where(from_prev, zero, p), v_own)
        den = _dot(p, head_ones) + jnp.where(low, jnp.exp(sink_even - m_even), jnp.exp(sink_odd - m_odd))
        o = o / den
        store(pair_a, o[:tq])
        store(pair_b, o[tq:])


PROMPT_QBLOCKS = 4


def _attn_prompt_kernel(sink_ref, q_ref, kp_ref, ko_ref, vp_ref, vo_ref, tbl_ref, o_ref):
    prev_shift = jnp.where(pl.program_id(1) == 0, MASK_VALUE, 0.0).astype(_F32)
    ops = [_kv_operands(kp_ref[...], vp_ref[...])]
    for b in range(PROMPT_QBLOCKS):
        rows = slice(b * WINDOW, (b + 1) * WINDOW)
        ops.append(_kv_operands(ko_ref[rows, :], vo_ref[rows, :]))

        def store(pair, o, rows=rows):
            o_ref[rows, pair * LANES:(pair + 1) * LANES] = o.astype(o_ref.dtype)

        _attend(q_ref[rows, :], ops[b], ops[b + 1], tbl_ref, sink_ref,
                prev_shift if b == 0 else jnp.float32(0.0), store)


def _attn_prompt(q, k, v, tbl, sinks, bsz, t):
    tq = PROMPT_QBLOCKS * WINDOW
    nb = t // tq
    own = lambda w: pl.BlockSpec((tq, w), lambda b, i: (b * nb + i, 0))
    prev = lambda w: pl.BlockSpec((WINDOW, w),
                                  lambda b, i: (PROMPT_QBLOCKS * (b * nb + i) - jnp.minimum(i, 1), 0))
    return pl.pallas_call(
        _attn_prompt_kernel,
        grid=(bsz, nb),
        in_specs=[pl.BlockSpec(memory_space=pltpu.SMEM), own(Q_DIM), prev(KV_DIM), own(KV_DIM),
                  prev(KV_DIM), own(KV_DIM), pl.BlockSpec(tbl.shape, lambda b, i: (0, 0, 0))],
        out_specs=own(Q_DIM),
        out_shape=jax.ShapeDtypeStruct((bsz * t, Q_DIM), _BF16),
        compiler_params=pltpu.CompilerParams(dimension_semantics=("arbitrary", "arbitrary"),
                                             vmem_limit_bytes=VMEM_LIMIT),
        name="attn_prompt",
    )(sinks, q, k, k, v, v, tbl)


SAMPLE_UNROLL = 4


def _attn_sample_kernel(sink_ref, q_ref, kn_ref, vn_ref, ck_ref, cv_ref, tbl_ref, o_ref, *, sb, steps):
    pad = jnp.zeros((WINDOW - steps, KV_DIM), _F32)

    def one_sequence(s, carry):
        def store(pair, o):
            o_ref[s, :, pair * LANES:(pair + 1) * LANES] = o

        own = _kv_operands(jnp.concatenate([kn_ref[s], pad], axis=0), jnp.concatenate([vn_ref[s], pad], axis=0))
        _attend(q_ref[s], _kv_operands(ck_ref[s], cv_ref[s]), own, tbl_ref, sink_ref, jnp.float32(0.0), store)
        return carry

    lax.fori_loop(0, sb, one_sequence, 0, unroll=SAMPLE_UNROLL)


def _attn_sample(q, k_new, v_new, cache_k, cache_v, tbl, sinks, sb):
    nseq, steps, _ = q.shape
    seq = lambda r, w: pl.BlockSpec((sb, r, w), lambda i: (i, 0, 0))
    return pl.pallas_call(
        functools.partial(_attn_sample_kernel, sb=sb, steps=steps),
        grid=(nseq // sb,),
        in_specs=[pl.BlockSpec(memory_space=pltpu.SMEM), seq(steps, Q_DIM), seq(steps, KV_DIM), seq(steps, KV_DIM),
                  seq(WINDOW, KV_DIM), seq(WINDOW, KV_DIM), pl.BlockSpec(tbl.shape, lambda i: (0, 0, 0))],
        out_specs=seq(steps, Q_DIM),
        out_shape=jax.ShapeDtypeStruct((nseq, steps, Q_DIM), _F32),
        compiler_params=pltpu.CompilerParams(dimension_semantics=("arbitrary",),
                                             vmem_limit_bytes=VMEM_LIMIT),
        name="attn_sample",
    )(sinks, q, k_new, v_new, cache_k, cache_v, tbl)


def _lane_min_index(mask, lane):
    return jnp.min(jnp.where(mask, lane, LANES), axis=-1, keepdims=True)


def _finish_kernel(x_ref, conv_ref, o_ref, ga_ref, gb_ref, wa_ref, wo_ref, ng_ref, wr_hi_ref, wr_lo_ref, br_ref,
                   tri_ref, h_ref, hn_ref, route_ref, count_ref, running_ref):
    @pl.when(pl.program_id(0) == 0)
    def _():
        running_ref[...] = jnp.zeros_like(running_ref)

    attn_out = _dot(o_ref[...].astype(_BF16), wa_ref[...])
    merged = ga_ref[...].astype(_F32) * conv_ref[...].astype(_F32) + gb_ref[...].astype(_F32) * attn_out
    h = x_ref[...] + _dot(merged.astype(_BF16), wo_ref[...])
    h_ref[...] = h
    hn = h * lax.rsqrt(jnp.mean(h * h, axis=-1, keepdims=True) + EPS) * ng_ref[...]
    hn_ref[...] = _pack_bf16_pairs(hn)

    hi, lo = _split_bf16(hn)
    logits = _dot(hi, wr_hi_ref[...]) + _dot(lo, wr_hi_ref[...]) + _dot(hi, wr_lo_ref[...]) + br_ref[...]
    lane = lax.broadcasted_iota(jnp.int32, logits.shape, 1)
    gmask = lane < N_GROUPS
    gl = jnp.where(gmask, logits, MASK_VALUE)
    gmax = jnp.max(gl, axis=-1, keepdims=True)
    grp = _lane_min_index(gmask & (gl == gmax), lane)
    p_grp = 1.0 / jnp.sum(jnp.where(gmask, jnp.exp(gl - gmax), 0.0), axis=-1, keepdims=True)
    e_lo = N_GROUPS + grp * EXPERTS_PER_GROUP
    emask = (lane >= e_lo) & (lane < e_lo + EXPERTS_PER_GROUP)
    el = jnp.where(emask, logits, MASK_VALUE)
    ex = jnp.where(emask, jnp.exp(el - jnp.max(el, axis=-1, keepdims=True)), 0.0)
    prob = jnp.where(emask, ex / jnp.sum(ex, axis=-1, keepdims=True), -1.0)
    p1 = jnp.max(prob, axis=-1, keepdims=True)
    i1 = _lane_min_index(prob == p1, lane)
    rest = jnp.where(lane == i1, -1.0, prob)
    p2 = jnp.max(rest, axis=-1, keepdims=True)
    i2 = _lane_min_index(rest == p2, lane)
    w1 = p_grp * p1 / (p1 + p2)
    w2 = p_grp * p2 / (p1 + p2)
    e1 = i1 - N_GROUPS
    e2 = i2 - N_GROUPS

    hot1 = lane == e1
    hot2 = lane == e2
    hot = jnp.where(hot1 | hot2, 1.0, 0.0)
    before = _dot(tri_ref[...], hot.astype(_BF16)) + running_ref[...]
    rank1 = jnp.sum(jnp.where(hot1, before, 0.0), axis=-1, keepdims=True)
    rank2 = jnp.sum(jnp.where(hot2, before, 0.0), axis=-1, keepdims=True)
    running_ref[...] += jnp.sum(hot, axis=0, keepdims=True)
    count_ref[...] = jnp.broadcast_to(running_ref[...], count_ref.shape)

    fields = (e1.astype(_F32), e2.astype(_F32), w1, w2, rank1, rank2)
    route = jnp.zeros(logits.shape, _F32)
    for pos, val in enumerate(fields):
        route = jnp.where(lane == pos, val, route)
    route_ref[...] = route


ROUTE_E, ROUTE_W, ROUTE_RANK = 0, 2, 4
DEST_ROWS = 1024


def _finish(x, conv_out, o, ga, gb, wa_b, wo_b, ng, wr_hi, wr_lo, br, tm):
    n = x.shape[0]
    tri = jnp.asarray(np.tril(np.ones((tm, tm), np.float32), -1), _BF16)
    row = lambda w: pl.BlockSpec((tm, w), lambda i: (i, 0))
    full = _resident
    return pl.pallas_call(
        _finish_kernel,
        grid=(n // tm,),
        in_specs=[row(D_MODEL), row(D_MODEL), row(Q_DIM), row(D_MODEL), row(D_MODEL),
                  full(wa_b), full(wo_b), full(ng), full(wr_hi), full(wr_lo), full(br), full(tri)],
        out_specs=[row(D_MODEL), row(HALF), row(LANES), pl.BlockSpec((SUBLANES, LANES), lambda i: (0, 0))],
        out_shape=[jax.ShapeDtypeStruct((n, D_MODEL), _F32),
                   jax.ShapeDtypeStruct((n, HALF), jnp.uint32),
                   jax.ShapeDtypeStruct((n, LANES), _F32),
                   jax.ShapeDtypeStruct((SUBLANES, LANES), _F32)],
        scratch_shapes=[pltpu.VMEM((1, LANES), _F32)],
        compiler_params=pltpu.CompilerParams(dimension_semantics=("arbitrary",),
                                             vmem_limit_bytes=VMEM_LIMIT),
        name="finish",
    )(x, conv_out, o, ga, gb, wa_b, wo_b, ng, wr_hi, wr_lo, br, tri)


def _dest_kernel(route_ref, starts_ref, dest_ref):
    route = route_ref[...]
    lane = lax.broadcasted_iota(jnp.int32, route.shape, 1)
    out = jnp.zeros(route.shape, jnp.int32)
    for j in range(TOP_K):
        e = route[:, ROUTE_E + j:ROUTE_E + j + 1].astype(jnp.int32)
        start = jnp.sum(jnp.where(lane == e, starts_ref[...], 0.0), axis=-1, keepdims=True)
        d = (start + route[:, ROUTE_RANK + j:ROUTE_RANK + j + 1]).astype(jnp.int32)
        out = jnp.where(lane == j, d, out)
    dest_ref[...] = out


def _dest(route, starts_row, tm):
    n = route.shape[0]
    row = pl.BlockSpec((tm, LANES), lambda i: (i, 0))
    return pl.pallas_call(
        _dest_kernel,
        grid=(n // tm,),
        in_specs=[row, pl.BlockSpec((1, LANES), lambda i: (0, 0))],
        out_specs=row,
        out_shape=jax.ShapeDtypeStruct((n, LANES), jnp.int32),
        name="dest",
    )(route, starts_row)


SC_CORES = 2
SC_SUBCORES = 16
SC_WORKERS = SC_CORES * SC_SUBCORES
SC_CHUNK_BYTES = 128 * 1024


def _sc_chunk(per_worker, row_bytes):
    chunk = min(per_worker, SC_CHUNK_BYTES // row_bytes)
    assert per_worker % chunk == 0 and chunk % SUBLANES == 0, (per_worker, row_bytes)
    return chunk


def _sc_gather_rows(table, idx):
    b = idx.shape[0]
    d = table.shape[1]
    per_worker = b // SC_WORKERS
    assert per_worker * SC_WORKERS == b, (b,)
    chunk = _sc_chunk(per_worker, d * 4)
    n_chunks = per_worker // chunk
    mesh = plsc.VectorSubcoreMesh(core_axis_name="c", subcore_axis_name="s",
                                  num_cores=SC_CORES, num_subcores=SC_SUBCORES)

    @functools.partial(
        pl.kernel, mesh=mesh,
        out_type=jax.ShapeDtypeStruct((b, d), table.dtype),
        scratch_types=[pltpu.VMEM((chunk,), jnp.int32), pltpu.VMEM((chunk, d), table.dtype),
                       pltpu.SemaphoreType.DMA],
        name="sc_gather_rows",
    )
    def gather(table_hbm, idx_hbm, out_hbm, idx_v, rows_v, sem):
        worker = lax.axis_index("s") * SC_CORES + lax.axis_index("c")

        @pl.loop(0, n_chunks)
        def _(c):
            base = pl.multiple_of(worker * per_worker + c * chunk, chunk)
            pltpu.sync_copy(idx_hbm.at[pl.ds(base, chunk)], idx_v)
            pltpu.async_copy(table_hbm.at[idx_v], rows_v, sem).wait()
            pltpu.sync_copy(rows_v, out_hbm.at[pl.ds(base, chunk)])

    return gather(table, idx)


def _sc_scatter_rows(src, idx):
    n, d = src.shape
    b = idx.shape[0]
    per_worker = b // SC_WORKERS
    assert per_worker * SC_WORKERS == b and n % per_worker == 0, (b, n)
    chunk = _sc_chunk(per_worker, d * 4)
    n_chunks = per_worker // chunk
    mesh = plsc.VectorSubcoreMesh(core_axis_name="c", subcore_axis_name="s",
                                  num_cores=SC_CORES, num_subcores=SC_SUBCORES)

    @functools.partial(
        pl.kernel, mesh=mesh,
        out_type=jax.ShapeDtypeStruct((b, d), src.dtype),
        scratch_types=[pltpu.VMEM((chunk,), jnp.int32), pltpu.VMEM((chunk, d), src.dtype)],
        name="sc_scatter_rows",
    )
    def scatter(src_hbm, idx_hbm, out_hbm, idx_v, rows_v):
        worker = lax.axis_index("s") * SC_CORES + lax.axis_index("c")

        @pl.loop(0, n_chunks)
        def _(c):
            base = pl.multiple_of(worker * per_worker + c * chunk, chunk)
            src_base = pl.multiple_of(lax.rem(base, n), chunk)
            pltpu.sync_copy(idx_hbm.at[pl.ds(base, chunk)], idx_v)
            pltpu.sync_copy(src_hbm.at[pl.ds(src_base, chunk)], rows_v)
            pltpu.sync_copy(rows_v, out_hbm.at[idx_v])

    return scatter(src, idx)


def _expert_kernel(blk_ref, exp_ref, lo_ref, hi_ref, x_ref, wg_ref, wu_ref, wd_ref, yb_ref,
                   wg_b, wu_b, wd_b, held_ref):
    del blk_ref
    k = pl.program_id(0)
    lo, hi, e = lo_ref[k], hi_ref[k], exp_ref[k]

    @pl.when(k == 0)
    def _():
        held_ref[0] = -1

    @pl.when(hi > lo)
    def _():
        @pl.when(held_ref[0] != e)
        def _():
            wg_b[...] = wg_ref[0].astype(_BF16)
            wu_b[...] = wu_ref[0].astype(_BF16)
            wd_b[...] = wd_ref[0].astype(_BF16)
            held_ref[0] = e

        xb = _unpack_bf16_pairs(x_ref[...]).astype(_BF16)
        g = _dot(xb, wg_b[...])
        u = _dot(xb, wu_b[...])
        hid = g * jax.nn.sigmoid(g) * u
        y = _dot(hid.astype(_BF16), wd_b[...])
        r = lax.broadcasted_iota(jnp.int32, yb_ref.shape, 0)
        pltpu.store(yb_ref, _pack_bf16_pairs(y), mask=(r >= lo) & (r < hi))


def _experts(items, xs, w_gate, w_up, w_down):
    n_items = items[0].shape[0]
    wspec = lambda a: pl.BlockSpec((1,) + a.shape[1:], lambda k, blk, exp, lo, hi: (exp[k], 0, 0))
    rows = pl.BlockSpec((MOE_ROWS, HALF), lambda k, blk, exp, lo, hi: (blk[k], 0))
    grid_spec = pltpu.PrefetchScalarGridSpec(
        num_scalar_prefetch=4,
        grid=(n_items,),
        in_specs=[rows, wspec(w_gate), wspec(w_up), wspec(w_down)],
        out_specs=rows,
        scratch_shapes=[pltpu.VMEM(w_gate.shape[1:], _BF16), pltpu.VMEM(w_up.shape[1:], _BF16),
                        pltpu.VMEM(w_down.shape[1:], _BF16), pltpu.SMEM((1,), jnp.int32)],
    )
    return pl.pallas_call(
        _expert_kernel,
        grid_spec=grid_spec,
        out_shape=jax.ShapeDtypeStruct(xs.shape, xs.dtype),
        compiler_params=pltpu.CompilerParams(dimension_semantics=("arbitrary",),
                                             vmem_limit_bytes=VMEM_LIMIT),
        name="experts",
    )(*items, xs, w_gate, w_up, w_down)


def _combine_kernel(h_ref, route_ref, g0_ref, g1_ref, y_ref):
    route = route_ref[...]
    y_ref[...] = (h_ref[...] + route[:, ROUTE_W:ROUTE_W + 1] * _unpack_bf16_pairs(g0_ref[...])
                  + route[:, ROUTE_W + 1:ROUTE_W + 2] * _unpack_bf16_pairs(g1_ref[...]))


def _combine(h, route, g, tm):
    n = h.shape[0]
    nt = n // tm
    row = lambda w: pl.BlockSpec((tm, w), lambda i: (i, 0))
    return pl.pallas_call(
        _combine_kernel,
        grid=(nt,),
        in_specs=[row(D_MODEL), row(LANES), row(HALF), pl.BlockSpec((tm, HALF), lambda i: (nt + i, 0))],
        out_specs=row(D_MODEL),
        out_shape=jax.ShapeDtypeStruct((n, D_MODEL), _F32),
        compiler_params=pltpu.CompilerParams(dimension_semantics=("arbitrary",),
                                             vmem_limit_bytes=VMEM_LIMIT),
        name="combine",
    )(h, route, g, g)


def _work_items(counts, n_pairs):
    n_blocks = n_pairs // MOE_ROWS
    starts = jnp.cumsum(counts) - counts
    cuts = jnp.sort(jnp.concatenate([jnp.arange(n_blocks, dtype=jnp.int32) * MOE_ROWS, starts]))
    ends = jnp.concatenate([cuts[1:], jnp.full((1,), n_pairs, jnp.int32)])
    blk = jnp.minimum(cuts // MOE_ROWS, n_blocks - 1)
    expert = jnp.clip(jnp.sum(starts[None, :] <= cuts[:, None], axis=1) - 1, 0, N_EXPERTS - 1).astype(jnp.int32)
    return starts, (blk, expert, cuts - blk * MOE_ROWS, ends - blk * MOE_ROWS)


def _moe(h, hn, route, counts_rows, wg_b, wu_b, wd_b, tm):
    n = h.shape[0]
    n_pairs = n * TOP_K
    counts = counts_rows[0, :N_EXPERTS].astype(jnp.int32)
    starts, items = _work_items(counts, n_pairs)
    starts_row = jnp.zeros((1, LANES), _F32).at[0, :N_EXPERTS].set(starts.astype(_F32))
    dest = _dest(route, starts_row, min(n, DEST_ROWS))[:, :TOP_K].T.reshape(n_pairs)
    xs = _sc_scatter_rows(hn, dest)
    yb = _experts(items, xs, wg_b, wu_b, wd_b)
    g = _sc_gather_rows(yb, dest)
    return _combine(h, route, g, tm)


def kernel(x_prompt, x_sample, state_conv, cache_k, cache_v, norm_attn_g, w_in, q_norm_g, k_norm_g, rel_bias, attn_sinks, w_dw, b_dw, conv_ln_g, conv_ln_b, w_conv_out, b_conv_out, w_attn_out, w_out, norm_ffn_g, w_grp, b_grp, w_router, b_router, w_gate, w_up, w_down):
    bsz, t, _ = x_prompt.shape
    nseq, steps, _ = x_sample.shape
    row = lambda a: a.reshape(1, -1).astype(_F32)

    w_in_b = w_in.astype(_BF16)
    wco_b = w_conv_out.astype(_BF16)
    wa_b = w_attn_out.astype(_BF16)
    wo_b = w_out.astype(_BF16)
    qg = row(jnp.tile(q_norm_g, N_HEADS)) * (HEAD_DIM ** -0.5)
    kg = row(jnp.tile(k_norm_g, N_KV_HEADS))
    group_of_lane = np.arange(Q_DIM) // HEAD_DIM
    gsum = jnp.asarray(group_of_lane[:, None] == np.arange(LANES)[None, :], _BF16)
    gexp = jnp.asarray(np.arange(LANES)[:, None] == group_of_lane[None, :], _BF16)
    w_rt = jnp.zeros((D_MODEL, LANES), _F32).at[:, :N_GROUPS].set(w_grp).at[:, N_GROUPS:N_GROUPS + N_EXPERTS].set(w_router)
    wr_hi = w_rt.astype(_BF16)
    wr_lo = (w_rt - wr_hi.astype(_F32)).astype(_BF16)
    b_rt = jnp.zeros((1, LANES), _F32).at[0, :N_GROUPS].set(b_grp).at[0, N_GROUPS:N_GROUPS + N_EXPERTS].set(b_router)
    tbl = _bias_tables(rel_bias)
    conv_params = (w_dw, row(b_dw), row(conv_ln_g), row(conv_ln_b), wco_b, row(b_conv_out))

    def finish_and_moe(x2d, conv_out, o, ga, gb, tm):
        h, hn, route, counts = _finish(x2d, conv_out, o, ga, gb, wa_b, wo_b, row(norm_ffn_g), wr_hi, wr_lo, b_rt, tm)
        return _moe(h, hn, route, counts, w_gate, w_up, w_down, tm)

    xp = x_prompt.reshape(bsz * t, D_MODEL)
    glu, q, k, v, ga, gb = _inproj(xp, row(norm_attn_g), w_in_b, qg, kg, gsum, gexp, _BF16, ROW_TILE)
    conv_out = _conv_prompt(glu, bsz, t, *conv_params, ROW_TILE)
    o = _attn_prompt(q, k, v, tbl, attn_sinks, bsz, t)
    y_prompt = finish_and_moe(xp, conv_out, o, ga, gb, ROW_TILE).reshape(bsz, t, D_MODEL)
    glu3 = glu.reshape(bsz, t, D_CONV)
    state_conv_prompt = glu3[:, t - (CONV_WIDTH - 1):]
    tail = lambda a: a.reshape(bsz, t, KV_DIM)[:, t - WINDOW:].reshape(bsz, WINDOW, N_KV_HEADS, HEAD_DIM)
    cache_k_prompt, cache_v_prompt = tail(k), tail(v)

    xs = x_sample.reshape(nseq * steps, D_MODEL)
    glu, q, k, v, ga, gb = _inproj(xs, row(norm_attn_g), w_in_b, qg, kg, gsum, gexp, _F32, ROW_TILE)
    glu3 = glu.reshape(nseq, steps, D_CONV)
    hist = jnp.concatenate([state_conv, glu3], axis=1)
    conv_out = _conv_sample(hist.transpose(1, 0, 2), *conv_params, 64)
    conv_out = conv_out.transpose(1, 0, 2).reshape(nseq * steps, D_MODEL)
    k3 = k.reshape(nseq, steps, KV_DIM)
    v3 = v.reshape(nseq, steps, KV_DIM)
    o = _attn_sample(q.reshape(nseq, steps, Q_DIM), k3, v3, cache_k.reshape(nseq, WINDOW, KV_DIM),
                     cache_v.reshape(nseq, WINDOW, KV_DIM), tbl, attn_sinks, 8)
    y_sample = finish_and_moe(xs, conv_out, o.reshape(nseq * steps, Q_DIM), ga, gb, ROW_TILE).reshape(nseq, steps, D_MODEL)
    state_conv_sample = hist[:, steps:]
    cache_k_sample = jnp.concatenate([cache_k, k3.reshape(nseq, steps, N_KV_HEADS, HEAD_DIM)], axis=1)[:, steps:]
    cache_v_sample = jnp.concatenate([cache_v, v3.reshape(nseq, steps, N_KV_HEADS, HEAD_DIM)], axis=1)[:, steps:]

    return (y_prompt, y_sample, state_conv_prompt, cache_k_prompt, cache_v_prompt,
            state_conv_sample, cache_k_sample, cache_v_sample)
```

```python
import functools
import math

import numpy as np
import jax
import jax.numpy as jnp
from jax import lax
from jax.experimental import pallas as pl
from jax.experimental.pallas import tpu as pltpu
from jax.experimental.pallas import tpu_sc as plsc

D_MODEL = 1024
N_HEADS = 16
HEAD_DIM = 64
N_KV_HEADS = 4
WINDOW = 128
Q_DIM = N_HEADS * HEAD_DIM
KV_DIM = N_KV_HEADS * HEAD_DIM
N_BUCKETS = 32
MAX_EXACT = N_BUCKETS // 2
MAX_DISTANCE = 128
D_CONV = D_MODEL
CONV_WIDTH = 31
N_GROUPS = 4
EXPERTS_PER_GROUP = 8
N_EXPERTS = N_GROUPS * EXPERTS_PER_GROUP
TOP_K = 2
D_EXPERT = 256
EPS = 1e-6

LANES = 128
SUBLANES = 8
N_PAIRS = N_HEADS // 2
MOE_ROWS = 256
MASK_VALUE = -1e30
VMEM_LIMIT = 56 * 1024 * 1024
ROW_TILE = 512

_F32 = jnp.float32
_BF16 = jnp.bfloat16


def _resident(a):
    return pl.BlockSpec(a.shape, lambda *_: (0,) * a.ndim, pipeline_mode=pl.Buffered(1))


def _dot(a, b):
    return jnp.dot(a, b, preferred_element_type=_F32)


def _split_bf16(x):
    hi = x.astype(_BF16)
    lo = (x - hi.astype(_F32)).astype(_BF16)
    return hi, lo


HALF = D_MODEL // 2


def _pack_bf16_pairs(x):
    lo = pltpu.bitcast(x[:, :HALF].astype(_BF16).astype(_F32), jnp.uint32)
    hi = pltpu.bitcast(x[:, HALF:].astype(_BF16).astype(_F32), jnp.uint32)
    return hi | (lo >> 16)


def _unpack_bf16_pairs(w):
    lo = pltpu.bitcast(w << 16, _F32)
    hi = pltpu.bitcast(w & jnp.uint32(0xFFFF0000), _F32)
    return jnp.concatenate([lo, hi], axis=1)


def _head_rms_scale(z, gsum, gexp):
    hi, lo = _split_bf16(z * z)
    ssum = _dot(hi, gsum) + _dot(lo, gsum)
    r = lax.rsqrt(ssum * (1.0 / HEAD_DIM) + EPS)
    rhi, rlo = _split_bf16(r)
    return _dot(rhi, gexp) + _dot(rlo, gexp)


def _inproj_kernel(x_ref, g_ref, w_ref, qg_ref, kg_ref, gsum_ref, gexp_ref,
                   glu_ref, q_ref, k_ref, v_ref, ga_ref, gb_ref):
    x = x_ref[...]
    xn = x * lax.rsqrt(jnp.mean(x * x, axis=-1, keepdims=True) + EPS) * g_ref[...]
    xb = xn.astype(_BF16)

    def seg(lo, width):
        return _dot(xb, w_ref[:, lo:lo + width])

    a = seg(0, D_CONV)
    b = seg(D_CONV, D_CONV)
    glu_ref[...] = a * jax.nn.sigmoid(b)
    off = 2 * D_CONV
    q = seg(off, Q_DIM)
    q_ref[...] = (q * _head_rms_scale(q, gsum_ref[...], gexp_ref[...]) * qg_ref[...]).astype(q_ref.dtype)
    off += Q_DIM
    k = seg(off, KV_DIM)
    k_ref[...] = k * _head_rms_scale(k, gsum_ref[:KV_DIM, :], gexp_ref[:, :KV_DIM]) * kg_ref[...]
    off += KV_DIM
    v_ref[...] = seg(off, KV_DIM)
    off += KV_DIM
    ga_ref[...] = jax.nn.sigmoid(seg(off, D_MODEL)).astype(ga_ref.dtype)
    off += D_MODEL
    gb_ref[...] = jax.nn.sigmoid(seg(off, D_MODEL)).astype(gb_ref.dtype)


def _inproj(x, g, w_in_b, qg, kg, gsum, gexp, q_dtype, tm):
    n = x.shape[0]
    in_dim = w_in_b.shape[1]
    row = lambda w: pl.BlockSpec((tm, w), lambda i: (i, 0))
    full = _resident
    return pl.pallas_call(
        _inproj_kernel,
        grid=(n // tm,),
        in_specs=[row(D_MODEL), full(g), full(w_in_b), full(qg), full(kg), full(gsum), full(gexp)],
        out_specs=[row(D_CONV), row(Q_DIM), row(KV_DIM), row(KV_DIM), row(D_MODEL), row(D_MODEL)],
        out_shape=[jax.ShapeDtypeStruct((n, D_CONV), _F32),
                   jax.ShapeDtypeStruct((n, Q_DIM), q_dtype),
                   jax.ShapeDtypeStruct((n, KV_DIM), _F32),
                   jax.ShapeDtypeStruct((n, KV_DIM), _F32),
                   jax.ShapeDtypeStruct((n, D_MODEL), _BF16),
                   jax.ShapeDtypeStruct((n, D_MODEL), _BF16)],
        compiler_params=pltpu.CompilerParams(dimension_semantics=("arbitrary",),
                                             vmem_limit_bytes=VMEM_LIMIT),
        name="inproj",
    )(x, g, w_in_b, qg, kg, gsum, gexp)


def _ln_swish_project(y, lng_ref, lnb_ref, wo_ref, bo_ref):
    mu = jnp.mean(y, axis=-1, keepdims=True)
    yc = y - mu
    var = jnp.mean(yc * yc, axis=-1, keepdims=True)
    z = yc * lax.rsqrt(var + EPS) * lng_ref[...] + lnb_ref[...]
    z = z * jax.nn.sigmoid(z)
    return (_dot(z.astype(_BF16), wo_ref[...]) + bo_ref[...]).astype(_BF16)


HALO = 32
CONV_STEPS = 16
CH_TILES = D_CONV // LANES


def _conv_prompt_kernel(glu_ref, w8_ref, b8_ref, lng_ref, lnb_ref, wo_ref, bo_ref,
                        out_ref, hist_ref, y_ref, *, tm):
    i = pl.program_id(1)

    @pl.when(i == 0)
    def _():
        hist_ref[0:HALO * CH_TILES, :] = jnp.zeros((HALO * CH_TILES, LANES), _F32)

    @pl.when(i > 0)
    def _():
        hist_ref[0:HALO * CH_TILES, :] = hist_ref[tm * CH_TILES:(tm + HALO) * CH_TILES, :]

    for c in range(CH_TILES):
        hist_ref[pl.ds(HALO * CH_TILES + c, tm, stride=CH_TILES), :] = glu_ref[:, c * LANES:(c + 1) * LANES]

    first = HALO - (CONV_WIDTH - 1)
    span = CONV_STEPS + CONV_WIDTH - 1

    def chunk(ci, carry):
        t0 = ci * CONV_STEPS
        x = hist_ref[pl.ds(pl.multiple_of((t0 + first) * CH_TILES, CH_TILES), span * CH_TILES), :]
        x = x.reshape(span, CH_TILES, LANES)
        acc = jnp.broadcast_to(b8_ref[...][None], (CONV_STEPS, CH_TILES, LANES))
        for j in range(CONV_WIDTH):
            acc = acc + x[j:j + CONV_STEPS] * w8_ref[j][None]
        y_ref[pl.ds(pl.multiple_of(t0 * CH_TILES, CH_TILES), CONV_STEPS * CH_TILES), :] = (
            acc.reshape(CONV_STEPS * CH_TILES, LANES))
        return carry

    lax.fori_loop(0, tm // CONV_STEPS, chunk, 0)
    y = jnp.concatenate([y_ref[pl.ds(c, tm, stride=CH_TILES), :] for c in range(CH_TILES)], axis=1)
    out_ref[...] = _ln_swish_project(y, lng_ref, lnb_ref, wo_ref, bo_ref)


def _conv_prompt(glu, bsz, t, w_dw, b_dw, lng, lnb, wo_b, bo, tm):
    assert CH_TILES == SUBLANES
    nt = t // tm
    w8 = w_dw.reshape(CONV_WIDTH, CH_TILES, LANES)
    b8 = b_dw.reshape(CH_TILES, LANES)
    full = _resident
    row = pl.BlockSpec((tm, D_CONV), lambda b, i: (b * nt + i, 0))
    return pl.pallas_call(
        functools.partial(_conv_prompt_kernel, tm=tm),
        grid=(bsz, nt),
        in_specs=[row, full(w8), full(b8), full(lng), full(lnb), full(wo_b), full(bo)],
        out_specs=pl.BlockSpec((tm, D_MODEL), lambda b, i: (b * nt + i, 0)),
        out_shape=jax.ShapeDtypeStruct((bsz * t, D_MODEL), _BF16),
        scratch_shapes=[pltpu.VMEM(((tm + HALO) * CH_TILES, LANES), _F32), pltpu.VMEM((tm * CH_TILES, LANES), _F32)],
        compiler_params=pltpu.CompilerParams(dimension_semantics=("arbitrary", "arbitrary"),
                                             vmem_limit_bytes=VMEM_LIMIT),
        name="conv_prompt",
    )(glu, w8, b8, lng, lnb, wo_b, bo)


def _conv_sample_kernel(hist_ref, wdw_ref, bdw_ref, lng_ref, lnb_ref, wo_ref, bo_ref, out_ref, *, steps):
    for t in range(steps):
        acc = hist_ref[t] * wdw_ref[0:1, :]
        for j in range(1, CONV_WIDTH):
            acc = acc + hist_ref[t + j] * wdw_ref[j:j + 1, :]
        out_ref[t] = _ln_swish_project(acc + bdw_ref[...], lng_ref, lnb_ref, wo_ref, bo_ref)


def _conv_sample(hist, w_dw, b_dw, lng, lnb, wo_b, bo, sb):
    rows, nseq, _ = hist.shape
    steps = rows - (CONV_WIDTH - 1)
    full = _resident
    return pl.pallas_call(
        functools.partial(_conv_sample_kernel, steps=steps),
        grid=(nseq // sb,),
        in_specs=[pl.BlockSpec((rows, sb, D_CONV), lambda i: (0, i, 0)),
                  full(w_dw), full(b_dw), full(lng), full(lnb), full(wo_b), full(bo)],
        out_specs=pl.BlockSpec((steps, sb, D_MODEL), lambda i: (0, i, 0)),
        out_shape=jax.ShapeDtypeStruct((steps, nseq, D_MODEL), _BF16),
        compiler_params=pltpu.CompilerParams(dimension_semantics=("arbitrary",),
                                             vmem_limit_bytes=VMEM_LIMIT),
        name="conv_sample",
    )(hist, w_dw, b_dw, lng, lnb, wo_b, bo)


def _bucket_map():
    i = np.arange(WINDOW)[:, None]
    j = np.arange(WINDOW)[None, :]
    n = (i - j) % WINDOW
    nf = np.maximum(n, 1).astype(np.float32)
    large = MAX_EXACT + (np.log(nf / np.float32(MAX_EXACT)) / np.float32(math.log(MAX_DISTANCE / MAX_EXACT))
                         * np.float32(N_BUCKETS - MAX_EXACT)).astype(np.int32)
    return np.where(n < MAX_EXACT, n, np.minimum(large, N_BUCKETS - 1)).astype(np.int32)


def _bias_table_kernel(rb_ref, bm_ref, tbl_ref):
    p = pl.program_id(0)
    bm = bm_ref[...]
    for half in range(2):
        h = 2 * p + half
        t = jnp.zeros(bm.shape, _F32)
        for b in range(N_BUCKETS):
            t = jnp.where(bm == b, rb_ref[b, h], t)
        tbl_ref[0, :, half * WINDOW:(half + 1) * WINDOW] = t


def _bias_tables(rel_bias):
    bm = jnp.asarray(_bucket_map())
    return pl.pallas_call(
        _bias_table_kernel,
        grid=(N_PAIRS,),
        in_specs=[pl.BlockSpec(memory_space=pltpu.SMEM), pl.BlockSpec(bm.shape, lambda p: (0, 0))],
        out_specs=pl.BlockSpec((1, WINDOW, 2 * WINDOW), lambda p: (p, 0, 0)),
        out_shape=jax.ShapeDtypeStruct((N_PAIRS, WINDOW, 2 * WINDOW), _F32),
        name="bias_tables",
    )(rel_bias, bm)


def _block_diag_pairs(slab):
    low = lax.broadcasted_iota(jnp.int32, slab.shape, 1) < HEAD_DIM
    swapped = pltpu.roll(slab, HEAD_DIM, axis=1)
    zero = jnp.zeros_like(slab)
    first = jnp.concatenate([jnp.where(low, slab, zero), jnp.where(low, zero, swapped)], axis=0)
    second = jnp.concatenate([jnp.where(low, swapped, zero), jnp.where(low, zero, slab)], axis=0)
    return first.astype(_BF16), second.astype(_BF16)


def _kv_operands(k_blk, v_blk):
    ops = []
    for slab in range(KV_DIM // LANES):
        cols = slice(slab * LANES, (slab + 1) * LANES)
        ops.extend(zip(_block_diag_pairs(k_blk[:, cols]), _block_diag_pairs(v_blk[:, cols])))
    return ops


def _attend(q, prev_ops, own_ops, tbl_ref, sink_ref, prev_shift, store, transposed=False):
    tq = q.shape[0]
    rows = 2 * tq
    row = lax.broadcasted_iota(jnp.int32, (rows, 2 * WINDOW), 0)
    col = lax.broadcasted_iota(jnp.int32, (rows, 2 * WINDOW), 1)
    from_prev = (col & (WINDOW - 1)) > jnp.where(row >= tq, row - tq, row)
    top = lax.broadcasted_iota(jnp.int32, (rows, 1), 0) < tq
    low = lax.broadcasted_iota(jnp.int32, (rows, LANES), 1) < HEAD_DIM
    r2 = lax.broadcasted_iota(jnp.int32, (2 * WINDOW, LANES), 0)
    c2 = lax.broadcasted_iota(jnp.int32, (2 * WINDOW, LANES), 1)
    head_ones = jnp.where((r2 < WINDOW) == (c2 < HEAD_DIM), 1.0, 0.0).astype(_BF16)
    contract_last = (((1,), (1,)), ((), ()))

    def logits(a, k_op):
        return _dot(a, k_op) if transposed else lax.dot_general(a, k_op, contract_last, preferred_element_type=_F32)

    def weighted_values(pr, v_op):
        return lax.dot_general(pr, v_op, contract_last, preferred_element_type=_F32) if transposed else _dot(pr, v_op)

    for kvh in range(N_KV_HEADS):
        (k_prev, v_prev), (k_own, v_own) = prev_ops[kvh], own_ops[kvh]
        pair_a = 2 * kvh
        pair_b = pair_a + 1
        qq = jnp.concatenate([q[:, pair_a * LANES:(pair_a + 1) * LANES],
                              q[:, pair_b * LANES:(pair_b + 1) * LANES]], axis=0).astype(_BF16)
        sp = logits(qq, k_prev)
        so = logits(qq, k_own)
        bias = jnp.concatenate([tbl_ref[pair_a, 0:tq, :], tbl_ref[pair_b, 0:tq, :]], axis=0)
        s = jnp.where(from_prev, sp + prev_shift, so) + bias
        sink_even = jnp.where(top, sink_ref[2 * pair_a], sink_ref[2 * pair_b])
        sink_odd = jnp.where(top, sink_ref[2 * pair_a + 1], sink_ref[2 * pair_b + 1])
        m_even = jnp.maximum(jnp.max(s[:, :WINDOW], axis=-1, keepdims=True), sink_even)
        m_odd = jnp.maximum(jnp.max(s[:, WINDOW:], axis=-1, keepdims=True), sink_odd)
        p = jnp.exp(s - jnp.where(col < WINDOW, m_even, m_odd)).astype(_BF16)
        zero = jnp.zeros_like(p)
        o = (weighted_values(jnp.where(from_prev, p, zero), v_prev)
             + weighted_values(jnp.where(from_prev, zero, p), v_own))
        if tq < WINDOW:
            pf = p.astype(_F32)
            sums = jnp.where(low, jnp.sum(pf[:, :WINDOW], axis=-1, keepdims=True),
                             jnp.sum(pf[:, WINDOW:], axis=-1, keepdims=True))
        else:
            sums = _dot(p, head_ones)
        den = sums + jnp.where(low, jnp.exp(sink_even - m_even), jnp.exp(sink_odd - m_odd))
        o = o / den
        store(pair_a, o[:tq])
        store(pair_b, o[tq:])


PROMPT_QBLOCKS = 4


def _attn_prompt_kernel(sink_ref, q_ref, kp_ref, ko_ref, vp_ref, vo_ref, tbl_ref, o_ref):
    prev_shift = jnp.where(pl.program_id(1) == 0, MASK_VALUE, 0.0).astype(_F32)
    ops = [_kv_operands(kp_ref[...], vp_ref[...])]
    for b in range(PROMPT_QBLOCKS):
        rows = slice(b * WINDOW, (b + 1) * WINDOW)
        ops.append(_kv_operands(ko_ref[rows, :], vo_ref[rows, :]))

        def store(pair, o, rows=rows):
            o_ref[rows, pair * LANES:(pair + 1) * LANES] = o.astype(o_ref.dtype)

        _attend(q_ref[rows, :], ops[b], ops[b + 1], tbl_ref, sink_ref,
                prev_shift if b == 0 else jnp.float32(0.0), store)


def _attn_prompt(q, k, v, tbl, sinks, bsz, t):
    tq = PROMPT_QBLOCKS * WINDOW
    nb = t // tq
    own = lambda w: pl.BlockSpec((tq, w), lambda b, i: (b * nb + i, 0))
    prev = lambda w: pl.BlockSpec((WINDOW, w),
                                  lambda b, i: (PROMPT_QBLOCKS * (b * nb + i) - jnp.minimum(i, 1), 0))
    return pl.pallas_call(
        _attn_prompt_kernel,
        grid=(bsz, nb),
        in_specs=[pl.BlockSpec(memory_space=pltpu.SMEM), own(Q_DIM), prev(KV_DIM), own(KV_DIM),
                  prev(KV_DIM), own(KV_DIM), pl.BlockSpec(tbl.shape, lambda b, i: (0, 0, 0))],
        out_specs=own(Q_DIM),
        out_shape=jax.ShapeDtypeStruct((bsz * t, Q_DIM), _BF16),
        compiler_params=pltpu.CompilerParams(dimension_semantics=("arbitrary", "arbitrary"),
                                             vmem_limit_bytes=VMEM_LIMIT),
        name="attn_prompt",
    )(sinks, q, k, k, v, v, tbl)


SAMPLE_UNROLL = 2


def _block_diag_t(x):
    xb = x.astype(_BF16)
    z = jnp.zeros_like(xb)
    return jnp.concatenate([jnp.concatenate([xb, z], axis=1), jnp.concatenate([z, xb], axis=1)], axis=0)


def _attn_sample_kernel(sink_ref, q_ref, kn_ref, vn_ref, ck_ref, cv_ref, tbl_ref, o_ref, cko_ref, cvo_ref,
                        *, sb, steps):
    pad = jnp.zeros((WINDOW - steps, LANES), _F32)
    lane = lax.broadcasted_iota(jnp.int32, (HEAD_DIM, WINDOW), 1)

    def one_sequence(s, carry):
        def store(pair, o):
            o_ref[s, :, pair * LANES:(pair + 1) * LANES] = o

        prev_ops, own_ops = [], []
        for slab in range(KV_DIM // LANES):
            cols = slice(slab * LANES, (slab + 1) * LANES)
            new_k = jnp.concatenate([kn_ref[s][:, cols], pad], axis=0).T
            new_v = jnp.concatenate([vn_ref[s][:, cols], pad], axis=0).T
            for sub in range(2):
                kvh = 2 * slab + sub
                part = slice(sub * HEAD_DIM, (sub + 1) * HEAD_DIM)
                kt, vt = ck_ref[s, kvh], cv_ref[s, kvh]
                cko_ref[s, kvh] = pltpu.roll(jnp.where(lane < steps, new_k[part], kt), WINDOW - steps, axis=1)
                cvo_ref[s, kvh] = pltpu.roll(jnp.where(lane < steps, new_v[part], vt), WINDOW - steps, axis=1)
                prev_ops.append((_block_diag_t(kt), _block_diag_t(vt)))
                own_ops.append((_block_diag_t(new_k[part]), _block_diag_t(new_v[part])))
        _attend(q_ref[s], prev_ops, own_ops, tbl_ref, sink_ref, jnp.float32(0.0), store, transposed=True)
        return carry

    lax.fori_loop(0, sb, one_sequence, 0, unroll=SAMPLE_UNROLL)


def _attn_sample(q, k_new, v_new, cache_kt, cache_vt, tbl, sinks, sb):
    nseq, steps, _ = q.shape
    seq = lambda r, w: pl.BlockSpec((sb, r, w), lambda i: (i, 0, 0))
    cache = pl.BlockSpec((sb, N_KV_HEADS, HEAD_DIM, WINDOW), lambda i: (i, 0, 0, 0))
    return pl.pallas_call(
        functools.partial(_attn_sample_kernel, sb=sb, steps=steps),
        grid=(nseq // sb,),
        in_specs=[pl.BlockSpec(memory_space=pltpu.SMEM), seq(steps, Q_DIM), seq(steps, KV_DIM), seq(steps, KV_DIM),
                  cache, cache, pl.BlockSpec(tbl.shape, lambda i: (0, 0, 0))],
        out_specs=[seq(steps, Q_DIM), cache, cache],
        out_shape=[jax.ShapeDtypeStruct((nseq, steps, Q_DIM), _F32),
                   jax.ShapeDtypeStruct(cache_kt.shape, _F32), jax.ShapeDtypeStruct(cache_vt.shape, _F32)],
        compiler_params=pltpu.CompilerParams(dimension_semantics=("arbitrary",),
                                             vmem_limit_bytes=VMEM_LIMIT),
        name="attn_sample",
    )(sinks, q, k_new, v_new, cache_kt, cache_vt, tbl)


def _lane_min_index(mask, lane):
    return jnp.min(jnp.where(mask, lane, LANES), axis=-1, keepdims=True)


def _finish_kernel(x_ref, conv_ref, o_ref, ga_ref, gb_ref, wa_ref, wo_ref, ng_ref, wr_hi_ref, wr_lo_ref, br_ref,
                   tri_ref, h_ref, hn_ref, route_ref, count_ref, running_ref):
    @pl.when(pl.program_id(0) == 0)
    def _():
        running_ref[...] = jnp.zeros_like(running_ref)

    attn_out = _dot(o_ref[...].astype(_BF16), wa_ref[...])
    merged = ga_ref[...].astype(_F32) * conv_ref[...].astype(_F32) + gb_ref[...].astype(_F32) * attn_out
    h = x_ref[...] + _dot(merged.astype(_BF16), wo_ref[...])
    h_ref[...] = h
    hn = h * lax.rsqrt(jnp.mean(h * h, axis=-1, keepdims=True) + EPS) * ng_ref[...]
    hn_ref[...] = _pack_bf16_pairs(hn)

    hi, lo = _split_bf16(hn)
    logits = _dot(hi, wr_hi_ref[...]) + _dot(lo, wr_hi_ref[...]) + _dot(hi, wr_lo_ref[...]) + br_ref[...]
    lane = lax.broadcasted_iota(jnp.int32, logits.shape, 1)
    gmask = lane < N_GROUPS
    gl = jnp.where(gmask, logits, MASK_VALUE)
    gmax = jnp.max(gl, axis=-1, keepdims=True)
    grp = _lane_min_index(gmask & (gl == gmax), lane)
    p_grp = 1.0 / jnp.sum(jnp.where(gmask, jnp.exp(gl - gmax), 0.0), axis=-1, keepdims=True)
    e_lo = N_GROUPS + grp * EXPERTS_PER_GROUP
    emask = (lane >= e_lo) & (lane < e_lo + EXPERTS_PER_GROUP)
    el = jnp.where(emask, logits, MASK_VALUE)
    ex = jnp.where(emask, jnp.exp(el - jnp.max(el, axis=-1, keepdims=True)), 0.0)
    prob = jnp.where(emask, ex / jnp.sum(ex, axis=-1, keepdims=True), -1.0)
    p1 = jnp.max(prob, axis=-1, keepdims=True)
    i1 = _lane_min_index(prob == p1, lane)
    rest = jnp.where(lane == i1, -1.0, prob)
    p2 = jnp.max(rest, axis=-1, keepdims=True)
    i2 = _lane_min_index(rest == p2, lane)
    w1 = p_grp * p1 / (p1 + p2)
    w2 = p_grp * p2 / (p1 + p2)
    e1 = i1 - N_GROUPS
    e2 = i2 - N_GROUPS

    hot1 = lane == e1
    hot2 = lane == e2
    hot = jnp.where(hot1 | hot2, 1.0, 0.0)
    before = _dot(tri_ref[...], hot.astype(_BF16)) + running_ref[...]
    rank1 = jnp.sum(jnp.where(hot1, before, 0.0), axis=-1, keepdims=True)
    rank2 = jnp.sum(jnp.where(hot2, before, 0.0), axis=-1, keepdims=True)
    running_ref[...] += jnp.sum(hot, axis=0, keepdims=True)
    count_ref[...] = jnp.broadcast_to(running_ref[...], count_ref.shape)

    fields = (e1.astype(_F32), e2.astype(_F32), w1, w2, rank1, rank2)
    route = jnp.zeros(logits.shape, _F32)
    for pos, val in enumerate(fields):
        route = jnp.where(lane == pos, val, route)
    route_ref[...] = route


ROUTE_E, ROUTE_W, ROUTE_RANK = 0, 2, 4
DEST_ROWS = 1024


def _finish(x, conv_out, o, ga, gb, wa_b, wo_b, ng, wr_hi, wr_lo, br, tm):
    n = x.shape[0]
    tri = jnp.asarray(np.tril(np.ones((tm, tm), np.float32), -1), _BF16)
    row = lambda w: pl.BlockSpec((tm, w), lambda i: (i, 0))
    full = _resident
    return pl.pallas_call(
        _finish_kernel,
        grid=(n // tm,),
        in_specs=[row(D_MODEL), row(D_MODEL), row(Q_DIM), row(D_MODEL), row(D_MODEL),
                  full(wa_b), full(wo_b), full(ng), full(wr_hi), full(wr_lo), full(br), full(tri)],
        out_specs=[row(D_MODEL), row(HALF), row(LANES), pl.BlockSpec((SUBLANES, LANES), lambda i: (0, 0))],
        out_shape=[jax.ShapeDtypeStruct((n, D_MODEL), _F32),
                   jax.ShapeDtypeStruct((n, HALF), jnp.uint32),
                   jax.ShapeDtypeStruct((n, LANES), _F32),
                   jax.ShapeDtypeStruct((SUBLANES, LANES), _F32)],
        scratch_shapes=[pltpu.VMEM((1, LANES), _F32)],
        compiler_params=pltpu.CompilerParams(dimension_semantics=("arbitrary",),
                                             vmem_limit_bytes=VMEM_LIMIT),
        name="finish",
    )(x, conv_out, o, ga, gb, wa_b, wo_b, ng, wr_hi, wr_lo, br, tri)


def _dest_kernel(route_ref, starts_ref, dest_ref):
    route = route_ref[...]
    lane = lax.broadcasted_iota(jnp.int32, route.shape, 1)
    out = jnp.zeros(route.shape, jnp.int32)
    for j in range(TOP_K):
        e = route[:, ROUTE_E + j:ROUTE_E + j + 1].astype(jnp.int32)
        start = jnp.sum(jnp.where(lane == e, starts_ref[...], 0.0), axis=-1, keepdims=True)
        d = (start + route[:, ROUTE_RANK + j:ROUTE_RANK + j + 1]).astype(jnp.int32)
        out = jnp.where(lane == j, d, out)
    dest_ref[...] = out


def _dest(route, starts_row, tm):
    n = route.shape[0]
    row = pl.BlockSpec((tm, LANES), lambda i: (i, 0))
    return pl.pallas_call(
        _dest_kernel,
        grid=(n // tm,),
        in_specs=[row, pl.BlockSpec((1, LANES), lambda i: (0, 0))],
        out_specs=row,
        out_shape=jax.ShapeDtypeStruct((n, LANES), jnp.int32),
        name="dest",
    )(route, starts_row)


SC_CORES = 2
SC_SUBCORES = 16
SC_WORKERS = SC_CORES * SC_SUBCORES
SC_CHUNK_BYTES = 128 * 1024


def _sc_chunk(per_worker, row_bytes):
    chunk = min(per_worker, SC_CHUNK_BYTES // row_bytes)
    assert per_worker % chunk == 0 and chunk % SUBLANES == 0, (per_worker, row_bytes)
    return chunk


def _sc_gather_rows(table, idx):
    b = idx.shape[0]
    d = table.shape[1]
    per_worker = b // SC_WORKERS
    assert per_worker * SC_WORKERS == b, (b,)
    chunk = _sc_chunk(per_worker, d * 4)
    n_chunks = per_worker // chunk
    mesh = plsc.VectorSubcoreMesh(core_axis_name="c", subcore_axis_name="s",
                                  num_cores=SC_CORES, num_subcores=SC_SUBCORES)

    @functools.partial(
        pl.kernel, mesh=mesh,
        out_type=jax.ShapeDtypeStruct((b, d), table.dtype),
        scratch_types=[pltpu.VMEM((chunk,), jnp.int32), pltpu.VMEM((chunk, d), table.dtype),
                       pltpu.SemaphoreType.DMA],
        name="sc_gather_rows",
    )
    def gather(table_hbm, idx_hbm, out_hbm, idx_v, rows_v, sem):
        worker = lax.axis_index("s") * SC_CORES + lax.axis_index("c")

        @pl.loop(0, n_chunks)
        def _(c):
            base = pl.multiple_of(worker * per_worker + c * chunk, chunk)
            pltpu.sync_copy(idx_hbm.at[pl.ds(base, chunk)], idx_v)
            pltpu.async_copy(table_hbm.at[idx_v], rows_v, sem).wait()
            pltpu.sync_copy(rows_v, out_hbm.at[pl.ds(base, chunk)])

    return gather(table, idx)


def _sc_scatter_rows(src, idx):
    n, d = src.shape
    b = idx.shape[0]
    per_worker = b // SC_WORKERS
    assert per_worker * SC_WORKERS == b and n % per_worker == 0, (b, n)
    chunk = _sc_chunk(per_worker, d * 4)
    n_chunks = per_worker // chunk
    mesh = plsc.VectorSubcoreMesh(core_axis_name="c", subcore_axis_name="s",
                                  num_cores=SC_CORES, num_subcores=SC_SUBCORES)

    @functools.partial(
        pl.kernel, mesh=mesh,
        out_type=jax.ShapeDtypeStruct((b, d), src.dtype),
        scratch_types=[pltpu.VMEM((chunk,), jnp.int32), pltpu.VMEM((chunk, d), src.dtype)],
        name="sc_scatter_rows",
    )
    def scatter(src_hbm, idx_hbm, out_hbm, idx_v, rows_v):
        worker = lax.axis_index("s") * SC_CORES + lax.axis_index("c")

        @pl.loop(0, n_chunks)
        def _(c):
            base = pl.multiple_of(worker * per_worker + c * chunk, chunk)
            src_base = pl.multiple_of(lax.rem(base, n), chunk)
            pltpu.sync_copy(idx_hbm.at[pl.ds(base, chunk)], idx_v)
            pltpu.sync_copy(src_hbm.at[pl.ds(src_base, chunk)], rows_v)
            pltpu.sync_copy(rows_v, out_hbm.at[idx_v])

    return scatter(src, idx)


def _expert_kernel(blk_ref, exp_ref, lo_ref, hi_ref, x_ref, wg_ref, wu_ref, wd_ref, yb_ref,
                   wg_b, wu_b, wd_b, held_ref):
    del blk_ref
    k = pl.program_id(0)
    lo, hi, e = lo_ref[k], hi_ref[k], exp_ref[k]

    @pl.when(k == 0)
    def _():
        held_ref[0] = -1

    @pl.when(hi > lo)
    def _():
        @pl.when(held_ref[0] != e)
        def _():
            wg_b[...] = wg_ref[0].astype(_BF16)
            wu_b[...] = wu_ref[0].astype(_BF16)
            wd_b[...] = wd_ref[0].astype(_BF16)
            held_ref[0] = e

        xb = _unpack_bf16_pairs(x_ref[...]).astype(_BF16)
        g = _dot(xb, wg_b[...])
        u = _dot(xb, wu_b[...])
        hid = g * jax.nn.sigmoid(g) * u
        y = _dot(hid.astype(_BF16), wd_b[...])
        r = lax.broadcasted_iota(jnp.int32, yb_ref.shape, 0)
        pltpu.store(yb_ref, _pack_bf16_pairs(y), mask=(r >= lo) & (r < hi))


def _experts(items, xs, w_gate, w_up, w_down):
    n_items = items[0].shape[0]
    wspec = lambda a: pl.BlockSpec((1,) + a.shape[1:], lambda k, blk, exp, lo, hi: (exp[k], 0, 0))
    rows = pl.BlockSpec((MOE_ROWS, HALF), lambda k, blk, exp, lo, hi: (blk[k], 0))
    grid_spec = pltpu.PrefetchScalarGridSpec(
        num_scalar_prefetch=4,
        grid=(n_items,),
        in_specs=[rows, wspec(w_gate), wspec(w_up), wspec(w_down)],
        out_specs=rows,
        scratch_shapes=[pltpu.VMEM(w_gate.shape[1:], _BF16), pltpu.VMEM(w_up.shape[1:], _BF16),
                        pltpu.VMEM(w_down.shape[1:], _BF16), pltpu.SMEM((1,), jnp.int32)],
    )
    return pl.pallas_call(
        _expert_kernel,
        grid_spec=grid_spec,
        out_shape=jax.ShapeDtypeStruct(xs.shape, xs.dtype),
        compiler_params=pltpu.CompilerParams(dimension_semantics=("arbitrary",),
                                             vmem_limit_bytes=VMEM_LIMIT),
        name="experts",
    )(*items, xs, w_gate, w_up, w_down)


def _combine_kernel(h_ref, route_ref, g0_ref, g1_ref, y_ref):
    route = route_ref[...]
    y_ref[...] = (h_ref[...] + route[:, ROUTE_W:ROUTE_W + 1] * _unpack_bf16_pairs(g0_ref[...])
                  + route[:, ROUTE_W + 1:ROUTE_W + 2] * _unpack_bf16_pairs(g1_ref[...]))


def _combine(h, route, g, tm):
    n = h.shape[0]
    nt = n // tm
    row = lambda w: pl.BlockSpec((tm, w), lambda i: (i, 0))
    return pl.pallas_call(
        _combine_kernel,
        grid=(nt,),
        in_specs=[row(D_MODEL), row(LANES), row(HALF), pl.BlockSpec((tm, HALF), lambda i: (nt + i, 0))],
        out_specs=row(D_MODEL),
        out_shape=jax.ShapeDtypeStruct((n, D_MODEL), _F32),
        compiler_params=pltpu.CompilerParams(dimension_semantics=("arbitrary",),
                                             vmem_limit_bytes=VMEM_LIMIT),
        name="combine",
    )(h, route, g, g)


def _work_items(counts, n_pairs):
    n_blocks = n_pairs // MOE_ROWS
    starts = jnp.cumsum(counts) - counts
    cuts = jnp.sort(jnp.concatenate([jnp.arange(n_blocks, dtype=jnp.int32) * MOE_ROWS, starts]))
    ends = jnp.concatenate([cuts[1:], jnp.full((1,), n_pairs, jnp.int32)])
    blk = jnp.minimum(cuts // MOE_ROWS, n_blocks - 1)
    expert = jnp.clip(jnp.sum(starts[None, :] <= cuts[:, None], axis=1) - 1, 0, N_EXPERTS - 1).astype(jnp.int32)
    return starts, (blk, expert, cuts - blk * MOE_ROWS, ends - blk * MOE_ROWS)


def _moe(h, hn, route, counts_rows, wg_b, wu_b, wd_b, tm):
    n = h.shape[0]
    n_pairs = n * TOP_K
    counts = counts_rows[0, :N_EXPERTS].astype(jnp.int32)
    starts, items = _work_items(counts, n_pairs)
    starts_row = jnp.zeros((1, LANES), _F32).at[0, :N_EXPERTS].set(starts.astype(_F32))
    dest = _dest(route, starts_row, min(n, DEST_ROWS))[:, :TOP_K].T.reshape(n_pairs)
    xs = _sc_scatter_rows(hn, dest)
    yb = _experts(items, xs, wg_b, wu_b, wd_b)
    g = _sc_gather_rows(yb, dest)
    return _combine(h, route, g, tm)


def kernel(x_prompt, x_sample, state_conv, cache_k, cache_v, norm_attn_g, w_in, q_norm_g, k_norm_g, rel_bias, attn_sinks, w_dw, b_dw, conv_ln_g, conv_ln_b, w_conv_out, b_conv_out, w_attn_out, w_out, norm_ffn_g, w_grp, b_grp, w_router, b_router, w_gate, w_up, w_down):
    bsz, t, _ = x_prompt.shape
    nseq, steps, _ = x_sample.shape
    row = lambda a: a.reshape(1, -1).astype(_F32)

    w_in_b = w_in.astype(_BF16)
    wco_b = w_conv_out.astype(_BF16)
    wa_b = w_attn_out.astype(_BF16)
    wo_b = w_out.astype(_BF16)
    qg = row(jnp.tile(q_norm_g, N_HEADS)) * (HEAD_DIM ** -0.5)
    kg = row(jnp.tile(k_norm_g, N_KV_HEADS))
    group_of_lane = np.arange(Q_DIM) // HEAD_DIM
    gsum = jnp.asarray(group_of_lane[:, None] == np.arange(LANES)[None, :], _BF16)
    gexp = jnp.asarray(np.arange(LANES)[:, None] == group_of_lane[None, :], _BF16)
    w_rt = jnp.zeros((D_MODEL, LANES), _F32).at[:, :N_GROUPS].set(w_grp).at[:, N_GROUPS:N_GROUPS + N_EXPERTS].set(w_router)
    wr_hi = w_rt.astype(_BF16)
    wr_lo = (w_rt - wr_hi.astype(_F32)).astype(_BF16)
    b_rt = jnp.zeros((1, LANES), _F32).at[0, :N_GROUPS].set(b_grp).at[0, N_GROUPS:N_GROUPS + N_EXPERTS].set(b_router)
    tbl = _bias_tables(rel_bias)
    conv_params = (w_dw, row(b_dw), row(conv_ln_g), row(conv_ln_b), wco_b, row(b_conv_out))

    def finish_and_moe(x2d, conv_out, o, ga, gb, tm):
        h, hn, route, counts = _finish(x2d, conv_out, o, ga, gb, wa_b, wo_b, row(norm_ffn_g), wr_hi, wr_lo, b_rt, tm)
        return _moe(h, hn, route, counts, w_gate, w_up, w_down, tm)

    xp = x_prompt.reshape(bsz * t, D_MODEL)
    glu, q, k, v, ga, gb = _inproj(xp, row(norm_attn_g), w_in_b, qg, kg, gsum, gexp, _BF16, ROW_TILE)
    conv_out = _conv_prompt(glu, bsz, t, *conv_params, ROW_TILE)
    o = _attn_prompt(q, k, v, tbl, attn_sinks, bsz, t)
    y_prompt = finish_and_moe(xp, conv_out, o, ga, gb, ROW_TILE).reshape(bsz, t, D_MODEL)
    glu3 = glu.reshape(bsz, t, D_CONV)
    state_conv_prompt = glu3[:, t - (CONV_WIDTH - 1):]
    tail = lambda a: a.reshape(bsz, t, KV_DIM)[:, t - WINDOW:].reshape(bsz, WINDOW, N_KV_HEADS, HEAD_DIM)
    cache_k_prompt, cache_v_prompt = tail(k), tail(v)

    xs = x_sample.reshape(nseq * steps, D_MODEL)
    glu, q, k, v, ga, gb = _inproj(xs, row(norm_attn_g), w_in_b, qg, kg, gsum, gexp, _F32, ROW_TILE)
    glu3 = glu.reshape(nseq, steps, D_CONV)
    hist = jnp.concatenate([state_conv, glu3], axis=1)
    conv_out = _conv_sample(hist.transpose(1, 0, 2), *conv_params, 64)
    conv_out = conv_out.transpose(1, 0, 2).reshape(nseq * steps, D_MODEL)
    k3 = k.reshape(nseq, steps, KV_DIM)
    v3 = v.reshape(nseq, steps, KV_DIM)
    o, ck_t, cv_t = _attn_sample(q.reshape(nseq, steps, Q_DIM), k3, v3, cache_k.transpose(0, 2, 3, 1),
                                 cache_v.transpose(0, 2, 3, 1), tbl, attn_sinks, 8)
    y_sample = finish_and_moe(xs, conv_out, o.reshape(nseq * steps, Q_DIM), ga, gb, ROW_TILE).reshape(nseq, steps, D_MODEL)
    state_conv_sample = hist[:, steps:]
    cache_k_sample = ck_t.transpose(0, 3, 1, 2)
    cache_v_sample = cv_t.transpose(0, 3, 1, 2)

    return (y_prompt, y_sample, state_conv_prompt, cache_k_prompt, cache_v_prompt,
            state_conv_sample, cache_k_sample, cache_v_sample)
```

```python
import functools
import math

import numpy as np
import jax
import jax.numpy as jnp
from jax import lax
from jax.experimental import pallas as pl
from jax.experimental.pallas import tpu as pltpu
from jax.experimental.pallas import tpu_sc as plsc

D_MODEL = 1024
N_HEADS = 16
HEAD_DIM = 64
N_KV_HEADS = 4
WINDOW = 128
Q_DIM = N_HEADS * HEAD_DIM
KV_DIM = N_KV_HEADS * HEAD_DIM
N_BUCKETS = 32
MAX_EXACT = N_BUCKETS // 2
MAX_DISTANCE = 128
D_CONV = D_MODEL
CONV_WIDTH = 31
N_GROUPS = 4
EXPERTS_PER_GROUP = 8
N_EXPERTS = N_GROUPS * EXPERTS_PER_GROUP
TOP_K = 2
D_EXPERT = 256
EPS = 1e-6

LANES = 128
SUBLANES = 8
N_PAIRS = N_HEADS // 2
MOE_ROWS = 256
MASK_VALUE = -1e30
VMEM_LIMIT = 56 * 1024 * 1024
ROW_TILE = 512

_F32 = jnp.float32
_BF16 = jnp.bfloat16


def _resident(a):
    return pl.BlockSpec(a.shape, lambda *_: (0,) * a.ndim, pipeline_mode=pl.Buffered(1))


def _dot(a, b):
    return jnp.dot(a, b, preferred_element_type=_F32)


def _split_bf16(x):
    hi = x.astype(_BF16)
    lo = (x - hi.astype(_F32)).astype(_BF16)
    return hi, lo


HALF = D_MODEL // 2


def _pack_bf16_pairs(x):
    lo = pltpu.bitcast(x[:, :HALF].astype(_BF16).astype(_F32), jnp.uint32)
    hi = pltpu.bitcast(x[:, HALF:].astype(_BF16).astype(_F32), jnp.uint32)
    return hi | (lo >> 16)


def _unpack_bf16_pairs(w):
    lo = pltpu.bitcast(w << 16, _F32)
    hi = pltpu.bitcast(w & jnp.uint32(0xFFFF0000), _F32)
    return jnp.concatenate([lo, hi], axis=1)


def _head_rms_scale(z):
    low = lax.broadcasted_iota(jnp.int32, (z.shape[0], LANES), 1) < HEAD_DIM
    slabs = []
    for c in range(z.shape[1] // LANES):
        sq = z[:, c * LANES:(c + 1) * LANES]
        sq = sq * sq
        first = jnp.sum(jnp.where(low, sq, 0.0), axis=-1, keepdims=True)
        second = jnp.sum(jnp.where(low, 0.0, sq), axis=-1, keepdims=True)
        slabs.append(lax.rsqrt(jnp.where(low, first, second) * (1.0 / HEAD_DIM) + EPS))
    return jnp.concatenate(slabs, axis=1)


def _inproj_kernel(x_ref, g_ref, w_ref, qg_ref, kg_ref,
                   glu_ref, q_ref, k_ref, v_ref, ga_ref, gb_ref):
    x = x_ref[...]
    xn = x * lax.rsqrt(jnp.mean(x * x, axis=-1, keepdims=True) + EPS) * g_ref[...]
    xb = xn.astype(_BF16)

    def seg(lo, width):
        return _dot(xb, w_ref[:, lo:lo + width])

    a = seg(0, D_CONV)
    b = seg(D_CONV, D_CONV)
    glu_ref[...] = a * jax.nn.sigmoid(b)
    off = 2 * D_CONV
    q = seg(off, Q_DIM)
    q_ref[...] = (q * _head_rms_scale(q) * qg_ref[...]).astype(q_ref.dtype)
    off += Q_DIM
    k = seg(off, KV_DIM)
    k_ref[...] = k * _head_rms_scale(k) * kg_ref[...]
    off += KV_DIM
    v_ref[...] = seg(off, KV_DIM)
    off += KV_DIM
    ga_ref[...] = jax.nn.sigmoid(seg(off, D_MODEL)).astype(ga_ref.dtype)
    off += D_MODEL
    gb_ref[...] = jax.nn.sigmoid(seg(off, D_MODEL)).astype(gb_ref.dtype)


def _inproj(x, g, w_in_b, qg, kg, q_dtype, tm):
    n = x.shape[0]
    in_dim = w_in_b.shape[1]
    row = lambda w: pl.BlockSpec((tm, w), lambda i: (i, 0))
    full = _resident
    return pl.pallas_call(
        _inproj_kernel,
        grid=(n // tm,),
        in_specs=[row(D_MODEL), full(g), full(w_in_b), full(qg), full(kg)],
        out_specs=[row(D_CONV), row(Q_DIM), row(KV_DIM), row(KV_DIM), row(D_MODEL), row(D_MODEL)],
        out_shape=[jax.ShapeDtypeStruct((n, D_CONV), _F32),
                   jax.ShapeDtypeStruct((n, Q_DIM), q_dtype),
                   jax.ShapeDtypeStruct((n, KV_DIM), _F32),
                   jax.ShapeDtypeStruct((n, KV_DIM), _F32),
                   jax.ShapeDtypeStruct((n, D_MODEL), _BF16),
                   jax.ShapeDtypeStruct((n, D_MODEL), _BF16)],
        compiler_params=pltpu.CompilerParams(dimension_semantics=("arbitrary",),
                                             vmem_limit_bytes=VMEM_LIMIT),
        name="inproj",
    )(x, g, w_in_b, qg, kg)


def _ln_swish_project(y, lng_ref, lnb_ref, wo_ref, bo_ref):
    mu = jnp.mean(y, axis=-1, keepdims=True)
    yc = y - mu
    var = jnp.mean(yc * yc, axis=-1, keepdims=True)
    z = yc * lax.rsqrt(var + EPS) * lng_ref[...] + lnb_ref[...]
    z = z * jax.nn.sigmoid(z)
    return (_dot(z.astype(_BF16), wo_ref[...]) + bo_ref[...]).astype(_BF16)


HALO = 32
CONV_STEPS = 16
CH_TILES = D_CONV // LANES


def _conv_prompt_kernel(glu_ref, w8_ref, b8_ref, lng_ref, lnb_ref, wo_ref, bo_ref,
                        out_ref, hist_ref, y_ref, *, tm):
    i = pl.program_id(1)

    @pl.when(i == 0)
    def _():
        hist_ref[0:HALO * CH_TILES, :] = jnp.zeros((HALO * CH_TILES, LANES), _F32)

    @pl.when(i > 0)
    def _():
        hist_ref[0:HALO * CH_TILES, :] = hist_ref[tm * CH_TILES:(tm + HALO) * CH_TILES, :]

    for c in range(CH_TILES):
        hist_ref[pl.ds(HALO * CH_TILES + c, tm, stride=CH_TILES), :] = glu_ref[:, c * LANES:(c + 1) * LANES]

    first = HALO - (CONV_WIDTH - 1)
    span = CONV_STEPS + CONV_WIDTH - 1

    def chunk(ci, carry):
        t0 = ci * CONV_STEPS
        x = hist_ref[pl.ds(pl.multiple_of((t0 + first) * CH_TILES, CH_TILES), span * CH_TILES), :]
        x = x.reshape(span, CH_TILES, LANES)
        acc = jnp.broadcast_to(b8_ref[...][None], (CONV_STEPS, CH_TILES, LANES))
        for j in range(CONV_WIDTH):
            acc = acc + x[j:j + CONV_STEPS] * w8_ref[j][None]
        y_ref[pl.ds(pl.multiple_of(t0 * CH_TILES, CH_TILES), CONV_STEPS * CH_TILES), :] = (
            acc.reshape(CONV_STEPS * CH_TILES, LANES))
        return carry

    lax.fori_loop(0, tm // CONV_STEPS, chunk, 0)
    y = jnp.concatenate([y_ref[pl.ds(c, tm, stride=CH_TILES), :] for c in range(CH_TILES)], axis=1)
    out_ref[...] = _ln_swish_project(y, lng_ref, lnb_ref, wo_ref, bo_ref)


def _conv_prompt(glu, bsz, t, w_dw, b_dw, lng, lnb, wo_b, bo, tm):
    assert CH_TILES == SUBLANES
    nt = t // tm
    w8 = w_dw.reshape(CONV_WIDTH, CH_TILES, LANES)
    b8 = b_dw.reshape(CH_TILES, LANES)
    full = _resident
    row = pl.BlockSpec((tm, D_CONV), lambda b, i: (b * nt + i, 0))
    return pl.pallas_call(
        functools.partial(_conv_prompt_kernel, tm=tm),
        grid=(bsz, nt),
        in_specs=[row, full(w8), full(b8), full(lng), full(lnb), full(wo_b), full(bo)],
        out_specs=pl.BlockSpec((tm, D_MODEL), lambda b, i: (b * nt + i, 0)),
        out_shape=jax.ShapeDtypeStruct((bsz * t, D_MODEL), _BF16),
        scratch_shapes=[pltpu.VMEM(((tm + HALO) * CH_TILES, LANES), _F32), pltpu.VMEM((tm * CH_TILES, LANES), _F32)],
        compiler_params=pltpu.CompilerParams(dimension_semantics=("arbitrary", "arbitrary"),
                                             vmem_limit_bytes=VMEM_LIMIT),
        name="conv_prompt",
    )(glu, w8, b8, lng, lnb, wo_b, bo)


def _conv_sample_kernel(hist_ref, wdw_ref, bdw_ref, lng_ref, lnb_ref, wo_ref, bo_ref, out_ref, *, steps):
    for t in range(steps):
        acc = hist_ref[t] * wdw_ref[0:1, :]
        for j in range(1, CONV_WIDTH):
            acc = acc + hist_ref[t + j] * wdw_ref[j:j + 1, :]
        out_ref[t] = _ln_swish_project(acc + bdw_ref[...], lng_ref, lnb_ref, wo_ref, bo_ref)


def _conv_sample(hist, w_dw, b_dw, lng, lnb, wo_b, bo, sb):
    rows, nseq, _ = hist.shape
    steps = rows - (CONV_WIDTH - 1)
    full = _resident
    return pl.pallas_call(
        functools.partial(_conv_sample_kernel, steps=steps),
        grid=(nseq // sb,),
        in_specs=[pl.BlockSpec((rows, sb, D_CONV), lambda i: (0, i, 0)),
                  full(w_dw), full(b_dw), full(lng), full(lnb), full(wo_b), full(bo)],
        out_specs=pl.BlockSpec((steps, sb, D_MODEL), lambda i: (0, i, 0)),
        out_shape=jax.ShapeDtypeStruct((steps, nseq, D_MODEL), _BF16),
        compiler_params=pltpu.CompilerParams(dimension_semantics=("arbitrary",),
                                             vmem_limit_bytes=VMEM_LIMIT),
        name="conv_sample",
    )(hist, w_dw, b_dw, lng, lnb, wo_b, bo)


def _bucket_map():
    i = np.arange(WINDOW)[:, None]
    j = np.arange(WINDOW)[None, :]
    n = (i - j) % WINDOW
    nf = np.maximum(n, 1).astype(np.float32)
    large = MAX_EXACT + (np.log(nf / np.float32(MAX_EXACT)) / np.float32(math.log(MAX_DISTANCE / MAX_EXACT))
                         * np.float32(N_BUCKETS - MAX_EXACT)).astype(np.int32)
    return np.where(n < MAX_EXACT, n, np.minimum(large, N_BUCKETS - 1)).astype(np.int32)


def _bias_table_kernel(rb_ref, bm_ref, tbl_ref):
    p = pl.program_id(0)
    bm = bm_ref[...]
    for half in range(2):
        h = 2 * p + half
        t = jnp.zeros(bm.shape, _F32)
        for b in range(N_BUCKETS):
            t = jnp.where(bm == b, rb_ref[b, h], t)
        tbl_ref[0, :, half * WINDOW:(half + 1) * WINDOW] = t


def _bias_tables(rel_bias):
    bm = jnp.asarray(_bucket_map())
    return pl.pallas_call(
        _bias_table_kernel,
        grid=(N_PAIRS,),
        in_specs=[pl.BlockSpec(memory_space=pltpu.SMEM), pl.BlockSpec(bm.shape, lambda p: (0, 0))],
        out_specs=pl.BlockSpec((1, WINDOW, 2 * WINDOW), lambda p: (p, 0, 0)),
        out_shape=jax.ShapeDtypeStruct((N_PAIRS, WINDOW, 2 * WINDOW), _F32),
        name="bias_tables",
    )(rel_bias, bm)


def _block_diag_pairs(slab):
    low = lax.broadcasted_iota(jnp.int32, slab.shape, 1) < HEAD_DIM
    swapped = pltpu.roll(slab, HEAD_DIM, axis=1)
    zero = jnp.zeros_like(slab)
    first = jnp.concatenate([jnp.where(low, slab, zero), jnp.where(low, zero, swapped)], axis=0)
    second = jnp.concatenate([jnp.where(low, swapped, zero), jnp.where(low, zero, slab)], axis=0)
    return first.astype(_BF16), second.astype(_BF16)


def _kv_operands(k_blk, v_blk):
    ops = []
    for slab in range(KV_DIM // LANES):
        cols = slice(slab * LANES, (slab + 1) * LANES)
        ops.extend(zip(_block_diag_pairs(k_blk[:, cols]), _block_diag_pairs(v_blk[:, cols])))
    return ops


def _attend(q, prev_ops, own_ops, tbl_ref, sink_ref, prev_shift, store, transposed=False):
    tq = q.shape[0]
    rows = 2 * tq
    row = lax.broadcasted_iota(jnp.int32, (rows, 2 * WINDOW), 0)
    col = lax.broadcasted_iota(jnp.int32, (rows, 2 * WINDOW), 1)
    from_prev = (col & (WINDOW - 1)) > jnp.where(row >= tq, row - tq, row)
    top = lax.broadcasted_iota(jnp.int32, (rows, 1), 0) < tq
    low = lax.broadcasted_iota(jnp.int32, (rows, LANES), 1) < HEAD_DIM
    contract_last = (((1,), (1,)), ((), ()))

    def logits(a, k_op):
        return _dot(a, k_op) if transposed else lax.dot_general(a, k_op, contract_last, preferred_element_type=_F32)

    def weighted_values(pr, v_op):
        return lax.dot_general(pr, v_op, contract_last, preferred_element_type=_F32) if transposed else _dot(pr, v_op)

    for kvh in range(N_KV_HEADS):
        (k_prev, v_prev), (k_own, v_own) = prev_ops[kvh], own_ops[kvh]
        pair_a = 2 * kvh
        pair_b = pair_a + 1
        qq = jnp.concatenate([q[:, pair_a * LANES:(pair_a + 1) * LANES],
                              q[:, pair_b * LANES:(pair_b + 1) * LANES]], axis=0).astype(_BF16)
        sp = logits(qq, k_prev)
        so = logits(qq, k_own)
        bias = jnp.concatenate([tbl_ref[pair_a, 0:tq, :], tbl_ref[pair_b, 0:tq, :]], axis=0)
        s = jnp.where(from_prev, sp + prev_shift, so) + bias
        sink_even = jnp.where(top, sink_ref[2 * pair_a], sink_ref[2 * pair_b])
        sink_odd = jnp.where(top, sink_ref[2 * pair_a + 1], sink_ref[2 * pair_b + 1])
        m_even = jnp.maximum(jnp.max(s[:, :WINDOW], axis=-1, keepdims=True), sink_even)
        m_odd = jnp.maximum(jnp.max(s[:, WINDOW:], axis=-1, keepdims=True), sink_odd)
        p = jnp.exp(s - jnp.where(col < WINDOW, m_even, m_odd)).astype(_BF16)
        zero = jnp.zeros_like(p)
        o = (weighted_values(jnp.where(from_prev, p, zero), v_prev)
             + weighted_values(jnp.where(from_prev, zero, p), v_own))
        pf = p.astype(_F32)
        sums = jnp.where(low, jnp.sum(pf[:, :WINDOW], axis=-1, keepdims=True),
                         jnp.sum(pf[:, WINDOW:], axis=-1, keepdims=True))
        den = sums + jnp.where(low, jnp.exp(sink_even - m_even), jnp.exp(sink_odd - m_odd))
        o = o / den
        store(pair_a, o[:tq])
        store(pair_b, o[tq:])


PROMPT_QBLOCKS = 4


def _attn_prompt_kernel(sink_ref, q_ref, kp_ref, ko_ref, vp_ref, vo_ref, tbl_ref, o_ref):
    prev_shift = jnp.where(pl.program_id(1) == 0, MASK_VALUE, 0.0).astype(_F32)
    ops = [_kv_operands(kp_ref[...], vp_ref[...])]
    for b in range(PROMPT_QBLOCKS):
        rows = slice(b * WINDOW, (b + 1) * WINDOW)
        ops.append(_kv_operands(ko_ref[rows, :], vo_ref[rows, :]))

        def store(pair, o, rows=rows):
            o_ref[rows, pair * LANES:(pair + 1) * LANES] = o.astype(o_ref.dtype)

        _attend(q_ref[rows, :], ops[b], ops[b + 1], tbl_ref, sink_ref,
                prev_shift if b == 0 else jnp.float32(0.0), store)


def _attn_prompt(q, k, v, tbl, sinks, bsz, t):
    tq = PROMPT_QBLOCKS * WINDOW
    nb = t // tq
    own = lambda w: pl.BlockSpec((tq, w), lambda b, i: (b * nb + i, 0))
    prev = lambda w: pl.BlockSpec((WINDOW, w),
                                  lambda b, i: (PROMPT_QBLOCKS * (b * nb + i) - jnp.minimum(i, 1), 0))
    return pl.pallas_call(
        _attn_prompt_kernel,
        grid=(bsz, nb),
        in_specs=[pl.BlockSpec(memory_space=pltpu.SMEM), own(Q_DIM), prev(KV_DIM), own(KV_DIM),
                  prev(KV_DIM), own(KV_DIM), pl.BlockSpec(tbl.shape, lambda b, i: (0, 0, 0))],
        out_specs=own(Q_DIM),
        out_shape=jax.ShapeDtypeStruct((bsz * t, Q_DIM), _BF16),
        compiler_params=pltpu.CompilerParams(dimension_semantics=("arbitrary", "arbitrary"),
                                             vmem_limit_bytes=VMEM_LIMIT),
        name="attn_prompt",
    )(sinks, q, k, k, v, v, tbl)


SAMPLE_UNROLL = 2


def _block_diag_t(x):
    xb = x.astype(_BF16)
    z = jnp.zeros_like(xb)
    return jnp.concatenate([jnp.concatenate([xb, z], axis=1), jnp.concatenate([z, xb], axis=1)], axis=0)


def _attn_sample_kernel(sink_ref, q_ref, kn_ref, vn_ref, ck_ref, cv_ref, tbl_ref, o_ref, cko_ref, cvo_ref,
                        *, sb, steps):
    pad = jnp.zeros((WINDOW - steps, LANES), _F32)
    lane = lax.broadcasted_iota(jnp.int32, (HEAD_DIM, WINDOW), 1)

    def one_sequence(s, carry):
        def store(pair, o):
            o_ref[s, :, pair * LANES:(pair + 1) * LANES] = o

        prev_ops, own_ops = [], []
        for slab in range(KV_DIM // LANES):
            cols = slice(slab * LANES, (slab + 1) * LANES)
            new_k = jnp.concatenate([kn_ref[s][:, cols], pad], axis=0).T
            new_v = jnp.concatenate([vn_ref[s][:, cols], pad], axis=0).T
            for sub in range(2):
                kvh = 2 * slab + sub
                part = slice(sub * HEAD_DIM, (sub + 1) * HEAD_DIM)
                kt, vt = ck_ref[s, kvh], cv_ref[s, kvh]
                cko_ref[s, kvh] = pltpu.roll(jnp.where(lane < steps, new_k[part], kt), WINDOW - steps, axis=1)
                cvo_ref[s, kvh] = pltpu.roll(jnp.where(lane < steps, new_v[part], vt), WINDOW - steps, axis=1)
                prev_ops.append((_block_diag_t(kt), _block_diag_t(vt)))
                own_ops.append((_block_diag_t(new_k[part]), _block_diag_t(new_v[part])))
        _attend(q_ref[s], prev_ops, own_ops, tbl_ref, sink_ref, jnp.float32(0.0), store, transposed=True)
        return carry

    lax.fori_loop(0, sb, one_sequence, 0, unroll=SAMPLE_UNROLL)


def _attn_sample(q, k_new, v_new, cache_kt, cache_vt, tbl, sinks, sb):
    nseq, steps, _ = q.shape
    seq = lambda r, w: pl.BlockSpec((sb, r, w), lambda i: (i, 0, 0))
    cache = pl.BlockSpec((sb, N_KV_HEADS, HEAD_DIM, WINDOW), lambda i: (i, 0, 0, 0))
    return pl.pallas_call(
        functools.partial(_attn_sample_kernel, sb=sb, steps=steps),
        grid=(nseq // sb,),
        in_specs=[pl.BlockSpec(memory_space=pltpu.SMEM), seq(steps, Q_DIM), seq(steps, KV_DIM), seq(steps, KV_DIM),
                  cache, cache, pl.BlockSpec(tbl.shape, lambda i: (0, 0, 0))],
        out_specs=[seq(steps, Q_DIM), cache, cache],
        out_shape=[jax.ShapeDtypeStruct((nseq, steps, Q_DIM), _F32),
                   jax.ShapeDtypeStruct(cache_kt.shape, _F32), jax.ShapeDtypeStruct(cache_vt.shape, _F32)],
        compiler_params=pltpu.CompilerParams(dimension_semantics=("arbitrary",),
                                             vmem_limit_bytes=VMEM_LIMIT),
        name="attn_sample",
    )(sinks, q, k_new, v_new, cache_kt, cache_vt, tbl)


def _lane_min_index(mask, lane):
    return jnp.min(jnp.where(mask, lane, LANES), axis=-1, keepdims=True)


def _finish_kernel(x_ref, conv_ref, o_ref, ga_ref, gb_ref, wa_ref, wo_ref, ng_ref, wr_hi_ref, wr_lo_ref, br_ref,
                   tri_ref, h_ref, hn_ref, route_ref, count_ref, running_ref):
    @pl.when(pl.program_id(0) == 0)
    def _():
        running_ref[...] = jnp.zeros_like(running_ref)

    attn_out = _dot(o_ref[...].astype(_BF16), wa_ref[...])
    merged = ga_ref[...].astype(_F32) * conv_ref[...].astype(_F32) + gb_ref[...].astype(_F32) * attn_out
    h = x_ref[...] + _dot(merged.astype(_BF16), wo_ref[...])
    h_ref[...] = h
    hn = h * lax.rsqrt(jnp.mean(h * h, axis=-1, keepdims=True) + EPS) * ng_ref[...]
    hn_ref[...] = _pack_bf16_pairs(hn)

    hi, lo = _split_bf16(hn)
    logits = _dot(hi, wr_hi_ref[...]) + _dot(lo, wr_hi_ref[...]) + _dot(hi, wr_lo_ref[...]) + br_ref[...]
    lane = lax.broadcasted_iota(jnp.int32, logits.shape, 1)
    gmask = lane < N_GROUPS
    gl = jnp.where(gmask, logits, MASK_VALUE)
    gmax = jnp.max(gl, axis=-1, keepdims=True)
    grp = _lane_min_index(gmask & (gl == gmax), lane)
    p_grp = 1.0 / jnp.sum(jnp.where(gmask, jnp.exp(gl - gmax), 0.0), axis=-1, keepdims=True)
    e_lo = N_GROUPS + grp * EXPERTS_PER_GROUP
    emask = (lane >= e_lo) & (lane < e_lo + EXPERTS_PER_GROUP)
    el = jnp.where(emask, logits, MASK_VALUE)
    ex = jnp.where(emask, jnp.exp(el - jnp.max(el, axis=-1, keepdims=True)), 0.0)
    prob = jnp.where(emask, ex / jnp.sum(ex, axis=-1, keepdims=True), -1.0)
    p1 = jnp.max(prob, axis=-1, keepdims=True)
    i1 = _lane_min_index(prob == p1, lane)
    rest = jnp.where(lane == i1, -1.0, prob)
    p2 = jnp.max(rest, axis=-1, keepdims=True)
    i2 = _lane_min_index(rest == p2, lane)
    w1 = p_grp * p1 / (p1 + p2)
    w2 = p_grp * p2 / (p1 + p2)
    e1 = i1 - N_GROUPS
    e2 = i2 - N_GROUPS

    hot1 = lane == e1
    hot2 = lane == e2
    hot = jnp.where(hot1 | hot2, 1.0, 0.0)
    before = _dot(tri_ref[...], hot.astype(_BF16)) + running_ref[...]
    rank1 = jnp.sum(jnp.where(hot1, before, 0.0), axis=-1, keepdims=True)
    rank2 = jnp.sum(jnp.where(hot2, before, 0.0), axis=-1, keepdims=True)
    running_ref[...] += jnp.sum(hot, axis=0, keepdims=True)
    count_ref[...] = jnp.broadcast_to(running_ref[...], count_ref.shape)

    fields = (e1.astype(_F32), e2.astype(_F32), w1, w2, rank1, rank2)
    route = jnp.zeros(logits.shape, _F32)
    for pos, val in enumerate(fields):
        route = jnp.where(lane == pos, val, route)
    route_ref[...] = route


ROUTE_E, ROUTE_W, ROUTE_RANK = 0, 2, 4
DEST_ROWS = 1024


def _finish(x, conv_out, o, ga, gb, wa_b, wo_b, ng, wr_hi, wr_lo, br, tm):
    n = x.shape[0]
    tri = jnp.asarray(np.tril(np.ones((tm, tm), np.float32), -1), _BF16)
    row = lambda w: pl.BlockSpec((tm, w), lambda i: (i, 0))
    full = _resident
    return pl.pallas_call(
        _finish_kernel,
        grid=(n // tm,),
        in_specs=[row(D_MODEL), row(D_MODEL), row(Q_DIM), row(D_MODEL), row(D_MODEL),
                  full(wa_b), full(wo_b), full(ng), full(wr_hi), full(wr_lo), full(br), full(tri)],
        out_specs=[row(D_MODEL), row(HALF), row(LANES), pl.BlockSpec((SUBLANES, LANES), lambda i: (0, 0))],
        out_shape=[jax.ShapeDtypeStruct((n, D_MODEL), _F32),
                   jax.ShapeDtypeStruct((n, HALF), jnp.uint32),
                   jax.ShapeDtypeStruct((n, LANES), _F32),
                   jax.ShapeDtypeStruct((SUBLANES, LANES), _F32)],
        scratch_shapes=[pltpu.VMEM((1, LANES), _F32)],
        compiler_params=pltpu.CompilerParams(dimension_semantics=("arbitrary",),
                                             vmem_limit_bytes=VMEM_LIMIT),
        name="finish",
    )(x, conv_out, o, ga, gb, wa_b, wo_b, ng, wr_hi, wr_lo, br, tri)


def _dest_kernel(route_ref, starts_ref, dest_ref):
    route = route_ref[...]
    lane = lax.broadcasted_iota(jnp.int32, route.shape, 1)
    out = jnp.zeros(route.shape, jnp.int32)
    for j in range(TOP_K):
        e = route[:, ROUTE_E + j:ROUTE_E + j + 1].astype(jnp.int32)
        start = jnp.sum(jnp.where(lane == e, starts_ref[...], 0.0), axis=-1, keepdims=True)
        d = (start + route[:, ROUTE_RANK + j:ROUTE_RANK + j + 1]).astype(jnp.int32)
        out = jnp.where(lane == j, d, out)
    dest_ref[...] = out


def _dest(route, starts_row, tm):
    n = route.shape[0]
    row = pl.BlockSpec((tm, LANES), lambda i: (i, 0))
    return pl.pallas_call(
        _dest_kernel,
        grid=(n // tm,),
        in_specs=[row, pl.BlockSpec((1, LANES), lambda i: (0, 0))],
        out_specs=row,
        out_shape=jax.ShapeDtypeStruct((n, LANES), jnp.int32),
        name="dest",
    )(route, starts_row)


SC_CORES = 2
SC_SUBCORES = 16
SC_WORKERS = SC_CORES * SC_SUBCORES
SC_CHUNK_BYTES = 128 * 1024


def _sc_chunk(per_worker, row_bytes):
    chunk = min(per_worker, SC_CHUNK_BYTES // row_bytes)
    assert per_worker % chunk == 0 and chunk % SUBLANES == 0, (per_worker, row_bytes)
    return chunk


def _sc_gather_rows(table, idx):
    b = idx.shape[0]
    d = table.shape[1]
    per_worker = b // SC_WORKERS
    assert per_worker * SC_WORKERS == b, (b,)
    chunk = _sc_chunk(per_worker, d * 4)
    n_chunks = per_worker // chunk
    mesh = plsc.VectorSubcoreMesh(core_axis_name="c", subcore_axis_name="s",
                                  num_cores=SC_CORES, num_subcores=SC_SUBCORES)

    @functools.partial(
        pl.kernel, mesh=mesh,
        out_type=jax.ShapeDtypeStruct((b, d), table.dtype),
        scratch_types=[pltpu.VMEM((chunk,), jnp.int32), pltpu.VMEM((chunk, d), table.dtype),
                       pltpu.SemaphoreType.DMA],
        name="sc_gather_rows",
    )
    def gather(table_hbm, idx_hbm, out_hbm, idx_v, rows_v, sem):
        worker = lax.axis_index("s") * SC_CORES + lax.axis_index("c")

        @pl.loop(0, n_chunks)
        def _(c):
            base = pl.multiple_of(worker * per_worker + c * chunk, chunk)
            pltpu.sync_copy(idx_hbm.at[pl.ds(base, chunk)], idx_v)
            pltpu.async_copy(table_hbm.at[idx_v], rows_v, sem).wait()
            pltpu.sync_copy(rows_v, out_hbm.at[pl.ds(base, chunk)])

    return gather(table, idx)


def _sc_scatter_rows(src, idx):
    n, d = src.shape
    b = idx.shape[0]
    per_worker = b // SC_WORKERS
    assert per_worker * SC_WORKERS == b and n % per_worker == 0, (b, n)
    chunk = _sc_chunk(per_worker, d * 4)
    n_chunks = per_worker // chunk
    mesh = plsc.VectorSubcoreMesh(core_axis_name="c", subcore_axis_name="s",
                                  num_cores=SC_CORES, num_subcores=SC_SUBCORES)

    @functools.partial(
        pl.kernel, mesh=mesh,
        out_type=jax.ShapeDtypeStruct((b, d), src.dtype),
        scratch_types=[pltpu.VMEM((chunk,), jnp.int32), pltpu.VMEM((chunk, d), src.dtype)],
        name="sc_scatter_rows",
    )
    def scatter(src_hbm, idx_hbm, out_hbm, idx_v, rows_v):
        worker = lax.axis_index("s") * SC_CORES + lax.axis_index("c")

        @pl.loop(0, n_chunks)
        def _(c):
            base = pl.multiple_of(worker * per_worker + c * chunk, chunk)
            src_base = pl.multiple_of(lax.rem(base, n), chunk)
            pltpu.sync_copy(idx_hbm.at[pl.ds(base, chunk)], idx_v)
            pltpu.sync_copy(src_hbm.at[pl.ds(src_base, chunk)], rows_v)
            pltpu.sync_copy(rows_v, out_hbm.at[idx_v])

    return scatter(src, idx)


def _expert_kernel(blk_ref, exp_ref, lo_ref, hi_ref, x_ref, wg_ref, wu_ref, wd_ref, yb_ref,
                   wg_b, wu_b, wd_b, held_ref):
    del blk_ref
    k = pl.program_id(0)
    lo, hi, e = lo_ref[k], hi_ref[k], exp_ref[k]

    @pl.when(k == 0)
    def _():
        held_ref[0] = -1

    @pl.when(hi > lo)
    def _():
        @pl.when(held_ref[0] != e)
        def _():
            wg_b[...] = wg_ref[0].astype(_BF16)
            wu_b[...] = wu_ref[0].astype(_BF16)
            wd_b[...] = wd_ref[0].astype(_BF16)
            held_ref[0] = e

        xb = _unpack_bf16_pairs(x_ref[...]).astype(_BF16)
        g = _dot(xb, wg_b[...])
        u = _dot(xb, wu_b[...])
        hid = g * jax.nn.sigmoid(g) * u
        y = _dot(hid.astype(_BF16), wd_b[...])
        r = lax.broadcasted_iota(jnp.int32, yb_ref.shape, 0)
        pltpu.store(yb_ref, _pack_bf16_pairs(y), mask=(r >= lo) & (r < hi))


def _experts(items, xs, w_gate, w_up, w_down):
    n_items = items[0].shape[0]
    wspec = lambda a: pl.BlockSpec((1,) + a.shape[1:], lambda k, blk, exp, lo, hi: (exp[k], 0, 0))
    rows = pl.BlockSpec((MOE_ROWS, HALF), lambda k, blk, exp, lo, hi: (blk[k], 0))
    grid_spec = pltpu.PrefetchScalarGridSpec(
        num_scalar_prefetch=4,
        grid=(n_items,),
        in_specs=[rows, wspec(w_gate), wspec(w_up), wspec(w_down)],
        out_specs=rows,
        scratch_shapes=[pltpu.VMEM(w_gate.shape[1:], _BF16), pltpu.VMEM(w_up.shape[1:], _BF16),
                        pltpu.VMEM(w_down.shape[1:], _BF16), pltpu.SMEM((1,), jnp.int32)],
    )
    return pl.pallas_call(
        _expert_kernel,
        grid_spec=grid_spec,
        out_shape=jax.ShapeDtypeStruct(xs.shape, xs.dtype),
        compiler_params=pltpu.CompilerParams(dimension_semantics=("arbitrary",),
                                             vmem_limit_bytes=VMEM_LIMIT),
        name="experts",
    )(*items, xs, w_gate, w_up, w_down)


def _combine_kernel(h_ref, route_ref, g0_ref, g1_ref, y_ref):
    route = route_ref[...]
    y_ref[...] = (h_ref[...] + route[:, ROUTE_W:ROUTE_W + 1] * _unpack_bf16_pairs(g0_ref[...])
                  + route[:, ROUTE_W + 1:ROUTE_W + 2] * _unpack_bf16_pairs(g1_ref[...]))


def _combine(h, route, g, tm):
    n = h.shape[0]
    nt = n // tm
    row = lambda w: pl.BlockSpec((tm, w), lambda i: (i, 0))
    return pl.pallas_call(
        _combine_kernel,
        grid=(nt,),
        in_specs=[row(D_MODEL), row(LANES), row(HALF), pl.BlockSpec((tm, HALF), lambda i: (nt + i, 0))],
        out_specs=row(D_MODEL),
        out_shape=jax.ShapeDtypeStruct((n, D_MODEL), _F32),
        compiler_params=pltpu.CompilerParams(dimension_semantics=("arbitrary",),
                                             vmem_limit_bytes=VMEM_LIMIT),
        name="combine",
    )(h, route, g, g)


def _work_items(counts, n_pairs):
    n_blocks = n_pairs // MOE_ROWS
    starts = jnp.cumsum(counts) - counts
    cuts = jnp.sort(jnp.concatenate([jnp.arange(n_blocks, dtype=jnp.int32) * MOE_ROWS, starts]))
    ends = jnp.concatenate([cuts[1:], jnp.full((1,), n_pairs, jnp.int32)])
    blk = jnp.minimum(cuts // MOE_ROWS, n_blocks - 1)
    expert = jnp.clip(jnp.sum(starts[None, :] <= cuts[:, None], axis=1) - 1, 0, N_EXPERTS - 1).astype(jnp.int32)
    return starts, (blk, expert, cuts - blk * MOE_ROWS, ends - blk * MOE_ROWS)


def _moe(h, hn, route, counts_rows, wg_b, wu_b, wd_b, tm):
    n = h.shape[0]
    n_pairs = n * TOP_K
    counts = counts_rows[0, :N_EXPERTS].astype(jnp.int32)
    starts, items = _work_items(counts, n_pairs)
    starts_row = jnp.zeros((1, LANES), _F32).at[0, :N_EXPERTS].set(starts.astype(_F32))
    dest = _dest(route, starts_row, min(n, DEST_ROWS))[:, :TOP_K].T.reshape(n_pairs)
    xs = _sc_scatter_rows(hn, dest)
    yb = _experts(items, xs, wg_b, wu_b, wd_b)
    g = _sc_gather_rows(yb, dest)
    return _combine(h, route, g, tm)


def kernel(x_prompt, x_sample, state_conv, cache_k, cache_v, norm_attn_g, w_in, q_norm_g, k_norm_g, rel_bias, attn_sinks, w_dw, b_dw, conv_ln_g, conv_ln_b, w_conv_out, b_conv_out, w_attn_out, w_out, norm_ffn_g, w_grp, b_grp, w_router, b_router, w_gate, w_up, w_down):
    bsz, t, _ = x_prompt.shape
    nseq, steps, _ = x_sample.shape
    row = lambda a: a.reshape(1, -1).astype(_F32)

    w_in_b = w_in.astype(_BF16)
    wco_b = w_conv_out.astype(_BF16)
    wa_b = w_attn_out.astype(_BF16)
    wo_b = w_out.astype(_BF16)
    qg = row(jnp.tile(q_norm_g, N_HEADS)) * (HEAD_DIM ** -0.5)
    kg = row(jnp.tile(k_norm_g, N_KV_HEADS))
    w_rt = jnp.zeros((D_MODEL, LANES), _F32).at[:, :N_GROUPS].set(w_grp).at[:, N_GROUPS:N_GROUPS + N_EXPERTS].set(w_router)
    wr_hi = w_rt.astype(_BF16)
    wr_lo = (w_rt - wr_hi.astype(_F32)).astype(_BF16)
    b_rt = jnp.zeros((1, LANES), _F32).at[0, :N_GROUPS].set(b_grp).at[0, N_GROUPS:N_GROUPS + N_EXPERTS].set(b_router)
    tbl = _bias_tables(rel_bias)
    conv_params = (w_dw, row(b_dw), row(conv_ln_g), row(conv_ln_b), wco_b, row(b_conv_out))

    def finish_and_moe(x2d, conv_out, o, ga, gb, tm):
        h, hn, route, counts = _finish(x2d, conv_out, o, ga, gb, wa_b, wo_b, row(norm_ffn_g), wr_hi, wr_lo, b_rt, tm)
        return _moe(h, hn, route, counts, w_gate, w_up, w_down, tm)

    xp = x_prompt.reshape(bsz * t, D_MODEL)
    glu, q, k, v, ga, gb = _inproj(xp, row(norm_attn_g), w_in_b, qg, kg, _BF16, ROW_TILE)
    conv_out = _conv_prompt(glu, bsz, t, *conv_params, ROW_TILE)
    o = _attn_prompt(q, k, v, tbl, attn_sinks, bsz, t)
    y_prompt = finish_and_moe(xp, conv_out, o, ga, gb, ROW_TILE).reshape(bsz, t, D_MODEL)
    glu3 = glu.reshape(bsz, t, D_CONV)
    state_conv_prompt = glu3[:, t - (CONV_WIDTH - 1):]
    tail = lambda a: a.reshape(bsz, t, KV_DIM)[:, t - WINDOW:].reshape(bsz, WINDOW, N_KV_HEADS, HEAD_DIM)
    cache_k_prompt, cache_v_prompt = tail(k), tail(v)

    xs = x_sample.reshape(nseq * steps, D_MODEL)
    glu, q, k, v, ga, gb = _inproj(xs, row(norm_attn_g), w_in_b, qg, kg, _F32, ROW_TILE)
    glu3 = glu.reshape(nseq, steps, D_CONV)
    hist = jnp.concatenate([state_conv, glu3], axis=1)
    conv_out = _conv_sample(hist.transpose(1, 0, 2), *conv_params, 64)
    conv_out = conv_out.transpose(1, 0, 2).reshape(nseq * steps, D_MODEL)
    k3 = k.reshape(nseq, steps, KV_DIM)
    v3 = v.reshape(nseq, steps, KV_DIM)
    o, ck_t, cv_t = _attn_sample(q.reshape(nseq, steps, Q_DIM), k3, v3, cache_k.transpose(0, 2, 3, 1),
                                 cache_v.transpose(0, 2, 3, 1), tbl, attn_sinks, 8)
    y_sample = finish_and_moe(xs, conv_out, o.reshape(nseq * steps, Q_DIM), ga, gb, ROW_TILE).reshape(nseq, steps, D_MODEL)
    state_conv_sample = hist[:, steps:]
    cache_k_sample = ck_t.transpose(0, 3, 1, 2)
    cache_v_sample = cv_t.transpose(0, 3, 1, 2)

    return (y_prompt, y_sample, state_conv_prompt, cache_k_prompt, cache_v_prompt,
            state_conv_sample, cache_k_sample, cache_v_sample)
```

```python
import functools
import math

import numpy as np
import jax
import jax.numpy as jnp
from jax import lax
from jax.experimental import pallas as pl
from jax.experimental.pallas import tpu as pltpu
from jax.experimental.pallas import tpu_sc as plsc

D_MODEL = 1024
N_HEADS = 16
HEAD_DIM = 64
N_KV_HEADS = 4
WINDOW = 128
Q_DIM = N_HEADS * HEAD_DIM
KV_DIM = N_KV_HEADS * HEAD_DIM
N_BUCKETS = 32
MAX_EXACT = N_BUCKETS // 2
MAX_DISTANCE = 128
D_CONV = D_MODEL
CONV_WIDTH = 31
N_GROUPS = 4
EXPERTS_PER_GROUP = 8
N_EXPERTS = N_GROUPS * EXPERTS_PER_GROUP
TOP_K = 2
D_EXPERT = 256
EPS = 1e-6

LANES = 128
SUBLANES = 8
N_PAIRS = N_HEADS // 2
MOE_ROWS = 512
MASK_VALUE = -1e30
VMEM_LIMIT = 56 * 1024 * 1024
ROW_TILE = 512

_F32 = jnp.float32
_BF16 = jnp.bfloat16


def _resident(a):
    return pl.BlockSpec(a.shape, lambda *_: (0,) * a.ndim, pipeline_mode=pl.Buffered(1))


def _dot(a, b):
    return jnp.dot(a, b, preferred_element_type=_F32)


def _split_bf16(x):
    hi = x.astype(_BF16)
    lo = (x - hi.astype(_F32)).astype(_BF16)
    return hi, lo


HALF = D_MODEL // 2


def _pack_bf16_pairs(x):
    lo = pltpu.bitcast(x[:, :HALF].astype(_BF16).astype(_F32), jnp.uint32)
    hi = pltpu.bitcast(x[:, HALF:].astype(_BF16).astype(_F32), jnp.uint32)
    return hi | (lo >> 16)


def _unpack_bf16_pairs(w):
    lo = pltpu.bitcast(w << 16, _F32)
    hi = pltpu.bitcast(w & jnp.uint32(0xFFFF0000), _F32)
    return jnp.concatenate([lo, hi], axis=1)


def _head_rms_scale(z):
    low = lax.broadcasted_iota(jnp.int32, (z.shape[0], LANES), 1) < HEAD_DIM
    slabs = []
    for c in range(z.shape[1] // LANES):
        sq = z[:, c * LANES:(c + 1) * LANES]
        sq = sq * sq
        first = jnp.sum(jnp.where(low, sq, 0.0), axis=-1, keepdims=True)
        second = jnp.sum(jnp.where(low, 0.0, sq), axis=-1, keepdims=True)
        slabs.append(lax.rsqrt(jnp.where(low, first, second) * (1.0 / HEAD_DIM) + EPS))
    return jnp.concatenate(slabs, axis=1)


def _inproj_kernel(x_ref, g_ref, w_ref, qg_ref, kg_ref,
                   glu_ref, q_ref, k_ref, v_ref, ga_ref, gb_ref):
    x = x_ref[...]
    xn = x * lax.rsqrt(jnp.mean(x * x, axis=-1, keepdims=True) + EPS) * g_ref[...]
    xb = xn.astype(_BF16)

    def seg(lo, width):
        return _dot(xb, w_ref[:, lo:lo + width])

    a = seg(0, D_CONV)
    b = seg(D_CONV, D_CONV)
    glu_ref[...] = a * jax.nn.sigmoid(b)
    off = 2 * D_CONV
    q = seg(off, Q_DIM)
    q_ref[...] = (q * _head_rms_scale(q) * qg_ref[...]).astype(q_ref.dtype)
    off += Q_DIM
    k = seg(off, KV_DIM)
    k_ref[...] = k * _head_rms_scale(k) * kg_ref[...]
    off += KV_DIM
    v_ref[...] = seg(off, KV_DIM)
    off += KV_DIM
    ga_ref[...] = jax.nn.sigmoid(seg(off, D_MODEL)).astype(ga_ref.dtype)
    off += D_MODEL
    gb_ref[...] = jax.nn.sigmoid(seg(off, D_MODEL)).astype(gb_ref.dtype)


def _inproj(x, g, w_in_b, qg, kg, q_dtype, tm):
    n = x.shape[0]
    in_dim = w_in_b.shape[1]
    row = lambda w: pl.BlockSpec((tm, w), lambda i: (i, 0))
    full = _resident
    return pl.pallas_call(
        _inproj_kernel,
        grid=(n // tm,),
        in_specs=[row(D_MODEL), full(g), full(w_in_b), full(qg), full(kg)],
        out_specs=[row(D_CONV), row(Q_DIM), row(KV_DIM), row(KV_DIM), row(D_MODEL), row(D_MODEL)],
        out_shape=[jax.ShapeDtypeStruct((n, D_CONV), _F32),
                   jax.ShapeDtypeStruct((n, Q_DIM), q_dtype),
                   jax.ShapeDtypeStruct((n, KV_DIM), _F32),
                   jax.ShapeDtypeStruct((n, KV_DIM), _F32),
                   jax.ShapeDtypeStruct((n, D_MODEL), _BF16),
                   jax.ShapeDtypeStruct((n, D_MODEL), _BF16)],
        compiler_params=pltpu.CompilerParams(dimension_semantics=("arbitrary",),
                                             vmem_limit_bytes=VMEM_LIMIT),
        name="inproj",
    )(x, g, w_in_b, qg, kg)


def _ln_swish_project(y, lng_ref, lnb_ref, wo_ref, bo_ref):
    mu = jnp.mean(y, axis=-1, keepdims=True)
    yc = y - mu
    var = jnp.mean(yc * yc, axis=-1, keepdims=True)
    z = yc * lax.rsqrt(var + EPS) * lng_ref[...] + lnb_ref[...]
    z = z * jax.nn.sigmoid(z)
    return (_dot(z.astype(_BF16), wo_ref[...]) + bo_ref[...]).astype(_BF16)


HALO = 32
CONV_STEPS = 16
CH_TILES = D_CONV // LANES


def _conv_prompt_kernel(glu_ref, w8_ref, b8_ref, lng_ref, lnb_ref, wo_ref, bo_ref,
                        out_ref, hist_ref, y_ref, *, tm):
    i = pl.program_id(1)

    @pl.when(i == 0)
    def _():
        hist_ref[0:HALO * CH_TILES, :] = jnp.zeros((HALO * CH_TILES, LANES), _F32)

    @pl.when(i > 0)
    def _():
        hist_ref[0:HALO * CH_TILES, :] = hist_ref[tm * CH_TILES:(tm + HALO) * CH_TILES, :]

    for c in range(CH_TILES):
        hist_ref[pl.ds(HALO * CH_TILES + c, tm, stride=CH_TILES), :] = glu_ref[:, c * LANES:(c + 1) * LANES]

    first = HALO - (CONV_WIDTH - 1)
    span = CONV_STEPS + CONV_WIDTH - 1

    def chunk(ci, carry):
        t0 = ci * CONV_STEPS
        x = hist_ref[pl.ds(pl.multiple_of((t0 + first) * CH_TILES, CH_TILES), span * CH_TILES), :]
        x = x.reshape(span, CH_TILES, LANES)
        acc = jnp.broadcast_to(b8_ref[...][None], (CONV_STEPS, CH_TILES, LANES))
        for j in range(CONV_WIDTH):
            acc = acc + x[j:j + CONV_STEPS] * w8_ref[j][None]
        y_ref[pl.ds(pl.multiple_of(t0 * CH_TILES, CH_TILES), CONV_STEPS * CH_TILES), :] = (
            acc.reshape(CONV_STEPS * CH_TILES, LANES))
        return carry

    lax.fori_loop(0, tm // CONV_STEPS, chunk, 0)
    y = jnp.concatenate([y_ref[pl.ds(c, tm, stride=CH_TILES), :] for c in range(CH_TILES)], axis=1)
    out_ref[...] = _ln_swish_project(y, lng_ref, lnb_ref, wo_ref, bo_ref)


def _conv_prompt(glu, bsz, t, w_dw, b_dw, lng, lnb, wo_b, bo, tm):
    assert CH_TILES == SUBLANES
    nt = t // tm
    w8 = w_dw.reshape(CONV_WIDTH, CH_TILES, LANES)
    b8 = b_dw.reshape(CH_TILES, LANES)
    full = _resident
    row = pl.BlockSpec((tm, D_CONV), lambda b, i: (b * nt + i, 0))
    return pl.pallas_call(
        functools.partial(_conv_prompt_kernel, tm=tm),
        grid=(bsz, nt),
        in_specs=[row, full(w8), full(b8), full(lng), full(lnb), full(wo_b), full(bo)],
        out_specs=pl.BlockSpec((tm, D_MODEL), lambda b, i: (b * nt + i, 0)),
        out_shape=jax.ShapeDtypeStruct((bsz * t, D_MODEL), _BF16),
        scratch_shapes=[pltpu.VMEM(((tm + HALO) * CH_TILES, LANES), _F32), pltpu.VMEM((tm * CH_TILES, LANES), _F32)],
        compiler_params=pltpu.CompilerParams(dimension_semantics=("arbitrary", "arbitrary"),
                                             vmem_limit_bytes=VMEM_LIMIT),
        name="conv_prompt",
    )(glu, w8, b8, lng, lnb, wo_b, bo)


def _conv_sample_kernel(hist_ref, wdw_ref, bdw_ref, lng_ref, lnb_ref, wo_ref, bo_ref, out_ref, *, steps):
    for t in range(steps):
        acc = hist_ref[t] * wdw_ref[0:1, :]
        for j in range(1, CONV_WIDTH):
            acc = acc + hist_ref[t + j] * wdw_ref[j:j + 1, :]
        out_ref[t] = _ln_swish_project(acc + bdw_ref[...], lng_ref, lnb_ref, wo_ref, bo_ref)


def _conv_sample(hist, w_dw, b_dw, lng, lnb, wo_b, bo, sb):
    rows, nseq, _ = hist.shape
    steps = rows - (CONV_WIDTH - 1)
    full = _resident
    return pl.pallas_call(
        functools.partial(_conv_sample_kernel, steps=steps),
        grid=(nseq // sb,),
        in_specs=[pl.BlockSpec((rows, sb, D_CONV), lambda i: (0, i, 0)),
                  full(w_dw), full(b_dw), full(lng), full(lnb), full(wo_b), full(bo)],
        out_specs=pl.BlockSpec((steps, sb, D_MODEL), lambda i: (0, i, 0)),
        out_shape=jax.ShapeDtypeStruct((steps, nseq, D_MODEL), _BF16),
        compiler_params=pltpu.CompilerParams(dimension_semantics=("arbitrary",),
                                             vmem_limit_bytes=VMEM_LIMIT),
        name="conv_sample",
    )(hist, w_dw, b_dw, lng, lnb, wo_b, bo)


def _bucket_map():
    i = np.arange(WINDOW)[:, None]
    j = np.arange(WINDOW)[None, :]
    n = (i - j) % WINDOW
    nf = np.maximum(n, 1).astype(np.float32)
    large = MAX_EXACT + (np.log(nf / np.float32(MAX_EXACT)) / np.float32(math.log(MAX_DISTANCE / MAX_EXACT))
                         * np.float32(N_BUCKETS - MAX_EXACT)).astype(np.int32)
    return np.where(n < MAX_EXACT, n, np.minimum(large, N_BUCKETS - 1)).astype(np.int32)


def _bias_table_kernel(rb_ref, bm_ref, tbl_ref):
    p = pl.program_id(0)
    bm = bm_ref[...]
    for half in range(2):
        h = 2 * p + half
        t = jnp.zeros(bm.shape, _F32)
        for b in range(N_BUCKETS):
            t = jnp.where(bm == b, rb_ref[b, h], t)
        tbl_ref[0, :, half * WINDOW:(half + 1) * WINDOW] = t


def _bias_tables(rel_bias):
    bm = jnp.asarray(_bucket_map())
    return pl.pallas_call(
        _bias_table_kernel,
        grid=(N_PAIRS,),
        in_specs=[pl.BlockSpec(memory_space=pltpu.SMEM), pl.BlockSpec(bm.shape, lambda p: (0, 0))],
        out_specs=pl.BlockSpec((1, WINDOW, 2 * WINDOW), lambda p: (p, 0, 0)),
        out_shape=jax.ShapeDtypeStruct((N_PAIRS, WINDOW, 2 * WINDOW), _F32),
        name="bias_tables",
    )(rel_bias, bm)


def _block_diag_pairs(slab):
    low = lax.broadcasted_iota(jnp.int32, slab.shape, 1) < HEAD_DIM
    swapped = pltpu.roll(slab, HEAD_DIM, axis=1)
    zero = jnp.zeros_like(slab)
    first = jnp.concatenate([jnp.where(low, slab, zero), jnp.where(low, zero, swapped)], axis=0)
    second = jnp.concatenate([jnp.where(low, swapped, zero), jnp.where(low, zero, slab)], axis=0)
    return first.astype(_BF16), second.astype(_BF16)


def _kv_operands(k_blk, v_blk):
    ops = []
    for slab in range(KV_DIM // LANES):
        cols = slice(slab * LANES, (slab + 1) * LANES)
        ops.extend(zip(_block_diag_pairs(k_blk[:, cols]), _block_diag_pairs(v_blk[:, cols])))
    return ops


def _attend(q, prev_ops, own_ops, tbl_ref, sink_ref, prev_shift, store, transposed=False):
    tq = q.shape[0]
    rows = 2 * tq
    row = lax.broadcasted_iota(jnp.int32, (rows, 2 * WINDOW), 0)
    col = lax.broadcasted_iota(jnp.int32, (rows, 2 * WINDOW), 1)
    from_prev = (col & (WINDOW - 1)) > jnp.where(row >= tq, row - tq, row)
    top = lax.broadcasted_iota(jnp.int32, (rows, 1), 0) < tq
    low = lax.broadcasted_iota(jnp.int32, (rows, LANES), 1) < HEAD_DIM
    contract_last = (((1,), (1,)), ((), ()))

    def logits(a, k_op):
        return _dot(a, k_op) if transposed else lax.dot_general(a, k_op, contract_last, preferred_element_type=_F32)

    def weighted_values(pr, v_op):
        return lax.dot_general(pr, v_op, contract_last, preferred_element_type=_F32) if transposed else _dot(pr, v_op)

    for kvh in range(N_KV_HEADS):
        (k_prev, v_prev), (k_own, v_own) = prev_ops[kvh], own_ops[kvh]
        pair_a = 2 * kvh
        pair_b = pair_a + 1
        qq = jnp.concatenate([q[:, pair_a * LANES:(pair_a + 1) * LANES],
                              q[:, pair_b * LANES:(pair_b + 1) * LANES]], axis=0).astype(_BF16)
        sp = logits(qq, k_prev)
        so = logits(qq, k_own)
        bias = jnp.concatenate([tbl_ref[pair_a, 0:tq, :], tbl_ref[pair_b, 0:tq, :]], axis=0)
        s = jnp.where(from_prev, sp + prev_shift, so) + bias
        sink_even = jnp.where(top, sink_ref[2 * pair_a], sink_ref[2 * pair_b])
        sink_odd = jnp.where(top, sink_ref[2 * pair_a + 1], sink_ref[2 * pair_b + 1])
        m_even = jnp.maximum(jnp.max(s[:, :WINDOW], axis=-1, keepdims=True), sink_even)
        m_odd = jnp.maximum(jnp.max(s[:, WINDOW:], axis=-1, keepdims=True), sink_odd)
        p = jnp.exp(s - jnp.where(col < WINDOW, m_even, m_odd)).astype(_BF16)
        zero = jnp.zeros_like(p)
        o = (weighted_values(jnp.where(from_prev, p, zero), v_prev)
             + weighted_values(jnp.where(from_prev, zero, p), v_own))
        pf = p.astype(_F32)
        sums = jnp.where(low, jnp.sum(pf[:, :WINDOW], axis=-1, keepdims=True),
                         jnp.sum(pf[:, WINDOW:], axis=-1, keepdims=True))
        den = sums + jnp.where(low, jnp.exp(sink_even - m_even), jnp.exp(sink_odd - m_odd))
        o = o / den
        store(pair_a, o[:tq])
        store(pair_b, o[tq:])


PROMPT_QBLOCKS = 4


def _attn_prompt_kernel(sink_ref, q_ref, kp_ref, ko_ref, vp_ref, vo_ref, tbl_ref, o_ref):
    prev_shift = jnp.where(pl.program_id(1) == 0, MASK_VALUE, 0.0).astype(_F32)
    ops = [_kv_operands(kp_ref[...], vp_ref[...])]
    for b in range(PROMPT_QBLOCKS):
        rows = slice(b * WINDOW, (b + 1) * WINDOW)
        ops.append(_kv_operands(ko_ref[rows, :], vo_ref[rows, :]))

        def store(pair, o, rows=rows):
            o_ref[rows, pair * LANES:(pair + 1) * LANES] = o.astype(o_ref.dtype)

        _attend(q_ref[rows, :], ops[b], ops[b + 1], tbl_ref, sink_ref,
                prev_shift if b == 0 else jnp.float32(0.0), store)


def _attn_prompt(q, k, v, tbl, sinks, bsz, t):
    tq = PROMPT_QBLOCKS * WINDOW
    nb = t // tq
    own = lambda w: pl.BlockSpec((tq, w), lambda b, i: (b * nb + i, 0))
    prev = lambda w: pl.BlockSpec((WINDOW, w),
                                  lambda b, i: (PROMPT_QBLOCKS * (b * nb + i) - jnp.minimum(i, 1), 0))
    return pl.pallas_call(
        _attn_prompt_kernel,
        grid=(bsz, nb),
        in_specs=[pl.BlockSpec(memory_space=pltpu.SMEM), own(Q_DIM), prev(KV_DIM), own(KV_DIM),
                  prev(KV_DIM), own(KV_DIM), pl.BlockSpec(tbl.shape, lambda b, i: (0, 0, 0))],
        out_specs=own(Q_DIM),
        out_shape=jax.ShapeDtypeStruct((bsz * t, Q_DIM), _BF16),
        compiler_params=pltpu.CompilerParams(dimension_semantics=("arbitrary", "arbitrary"),
                                             vmem_limit_bytes=VMEM_LIMIT),
        name="attn_prompt",
    )(sinks, q, k, k, v, v, tbl)


SAMPLE_UNROLL = 2


def _block_diag_t(x):
    xb = x.astype(_BF16)
    z = jnp.zeros_like(xb)
    return jnp.concatenate([jnp.concatenate([xb, z], axis=1), jnp.concatenate([z, xb], axis=1)], axis=0)


def _attn_sample_kernel(sink_ref, q_ref, kn_ref, vn_ref, ck_ref, cv_ref, tbl_ref, o_ref, cko_ref, cvo_ref,
                        *, sb, steps):
    pad = jnp.zeros((WINDOW - steps, LANES), _F32)
    lane = lax.broadcasted_iota(jnp.int32, (HEAD_DIM, WINDOW), 1)

    def one_sequence(s, carry):
        def store(pair, o):
            o_ref[s, :, pair * LANES:(pair + 1) * LANES] = o

        prev_ops, own_ops = [], []
        for slab in range(KV_DIM // LANES):
            cols = slice(slab * LANES, (slab + 1) * LANES)
            new_k = jnp.concatenate([kn_ref[s][:, cols], pad], axis=0).T
            new_v = jnp.concatenate([vn_ref[s][:, cols], pad], axis=0).T
            for sub in range(2):
                kvh = 2 * slab + sub
                part = slice(sub * HEAD_DIM, (sub + 1) * HEAD_DIM)
                kt, vt = ck_ref[s, kvh], cv_ref[s, kvh]
                cko_ref[s, kvh] = pltpu.roll(jnp.where(lane < steps, new_k[part], kt), WINDOW - steps, axis=1)
                cvo_ref[s, kvh] = pltpu.roll(jnp.where(lane < steps, new_v[part], vt), WINDOW - steps, axis=1)
                prev_ops.append((_block_diag_t(kt), _block_diag_t(vt)))
                own_ops.append((_block_diag_t(new_k[part]), _block_diag_t(new_v[part])))
        _attend(q_ref[s], prev_ops, own_ops, tbl_ref, sink_ref, jnp.float32(0.0), store, transposed=True)
        return carry

    lax.fori_loop(0, sb, one_sequence, 0, unroll=SAMPLE_UNROLL)


def _attn_sample(q, k_new, v_new, cache_kt, cache_vt, tbl, sinks, sb):
    nseq, steps, _ = q.shape
    seq = lambda r, w: pl.BlockSpec((sb, r, w), lambda i: (i, 0, 0))
    cache = pl.BlockSpec((sb, N_KV_HEADS, HEAD_DIM, WINDOW), lambda i: (i, 0, 0, 0))
    return pl.pallas_call(
        functools.partial(_attn_sample_kernel, sb=sb, steps=steps),
        grid=(nseq // sb,),
        in_specs=[pl.BlockSpec(memory_space=pltpu.SMEM), seq(steps, Q_DIM), seq(steps, KV_DIM), seq(steps, KV_DIM),
                  cache, cache, pl.BlockSpec(tbl.shape, lambda i: (0, 0, 0))],
        out_specs=[seq(steps, Q_DIM), cache, cache],
        out_shape=[jax.ShapeDtypeStruct((nseq, steps, Q_DIM), _F32),
                   jax.ShapeDtypeStruct(cache_kt.shape, _F32), jax.ShapeDtypeStruct(cache_vt.shape, _F32)],
        compiler_params=pltpu.CompilerParams(dimension_semantics=("arbitrary",),
                                             vmem_limit_bytes=VMEM_LIMIT),
        name="attn_sample",
    )(sinks, q, k_new, v_new, cache_kt, cache_vt, tbl)


def _lane_min_index(mask, lane):
    return jnp.min(jnp.where(mask, lane, LANES), axis=-1, keepdims=True)


def _finish_kernel(x_ref, conv_ref, o_ref, ga_ref, gb_ref, wa_ref, wo_ref, ng_ref, wr_hi_ref, wr_lo_ref, br_ref,
                   tri_ref, h_ref, hn_ref, route_ref, count_ref, running_ref):
    @pl.when(pl.program_id(0) == 0)
    def _():
        running_ref[...] = jnp.zeros_like(running_ref)

    attn_out = _dot(o_ref[...].astype(_BF16), wa_ref[...])
    merged = ga_ref[...].astype(_F32) * conv_ref[...].astype(_F32) + gb_ref[...].astype(_F32) * attn_out
    h = x_ref[...] + _dot(merged.astype(_BF16), wo_ref[...])
    h_ref[...] = h
    hn = h * lax.rsqrt(jnp.mean(h * h, axis=-1, keepdims=True) + EPS) * ng_ref[...]
    hn_ref[...] = _pack_bf16_pairs(hn)

    hi, lo = _split_bf16(hn)
    logits = _dot(hi, wr_hi_ref[...]) + _dot(lo, wr_hi_ref[...]) + _dot(hi, wr_lo_ref[...]) + br_ref[...]
    lane = lax.broadcasted_iota(jnp.int32, logits.shape, 1)
    gmask = lane < N_GROUPS
    gl = jnp.where(gmask, logits, MASK_VALUE)
    gmax = jnp.max(gl, axis=-1, keepdims=True)
    grp = _lane_min_index(gmask & (gl == gmax), lane)
    p_grp = 1.0 / jnp.sum(jnp.where(gmask, jnp.exp(gl - gmax), 0.0), axis=-1, keepdims=True)
    e_lo = N_GROUPS + grp * EXPERTS_PER_GROUP
    emask = (lane >= e_lo) & (lane < e_lo + EXPERTS_PER_GROUP)
    el = jnp.where(emask, logits, MASK_VALUE)
    ex = jnp.where(emask, jnp.exp(el - jnp.max(el, axis=-1, keepdims=True)), 0.0)
    prob = jnp.where(emask, ex / jnp.sum(ex, axis=-1, keepdims=True), -1.0)
    p1 = jnp.max(prob, axis=-1, keepdims=True)
    i1 = _lane_min_index(prob == p1, lane)
    rest = jnp.where(lane == i1, -1.0, prob)
    p2 = jnp.max(rest, axis=-1, keepdims=True)
    i2 = _lane_min_index(rest == p2, lane)
    w1 = p_grp * p1 / (p1 + p2)
    w2 = p_grp * p2 / (p1 + p2)
    e1 = i1 - N_GROUPS
    e2 = i2 - N_GROUPS

    hot1 = lane == e1
    hot2 = lane == e2
    hot = jnp.where(hot1 | hot2, 1.0, 0.0)
    before = _dot(tri_ref[...], hot.astype(_BF16)) + running_ref[...]
    rank1 = jnp.sum(jnp.where(hot1, before, 0.0), axis=-1, keepdims=True)
    rank2 = jnp.sum(jnp.where(hot2, before, 0.0), axis=-1, keepdims=True)
    running_ref[...] += jnp.sum(hot, axis=0, keepdims=True)
    count_ref[...] = jnp.broadcast_to(running_ref[...], count_ref.shape)

    fields = (e1.astype(_F32), e2.astype(_F32), w1, w2, rank1, rank2)
    route = jnp.zeros(logits.shape, _F32)
    for pos, val in enumerate(fields):
        route = jnp.where(lane == pos, val, route)
    route_ref[...] = route


ROUTE_E, ROUTE_W, ROUTE_RANK = 0, 2, 4
DEST_ROWS = 1024


def _finish(x, conv_out, o, ga, gb, wa_b, wo_b, ng, wr_hi, wr_lo, br, tm):
    n = x.shape[0]
    tri = jnp.asarray(np.tril(np.ones((tm, tm), np.float32), -1), _BF16)
    row = lambda w: pl.BlockSpec((tm, w), lambda i: (i, 0))
    full = _resident
    return pl.pallas_call(
        _finish_kernel,
        grid=(n // tm,),
        in_specs=[row(D_MODEL), row(D_MODEL), row(Q_DIM), row(D_MODEL), row(D_MODEL),
                  full(wa_b), full(wo_b), full(ng), full(wr_hi), full(wr_lo), full(br), full(tri)],
        out_specs=[row(D_MODEL), row(HALF), row(LANES), pl.BlockSpec((SUBLANES, LANES), lambda i: (0, 0))],
        out_shape=[jax.ShapeDtypeStruct((n, D_MODEL), _F32),
                   jax.ShapeDtypeStruct((n, HALF), jnp.uint32),
                   jax.ShapeDtypeStruct((n, LANES), _F32),
                   jax.ShapeDtypeStruct((SUBLANES, LANES), _F32)],
        scratch_shapes=[pltpu.VMEM((1, LANES), _F32)],
        compiler_params=pltpu.CompilerParams(dimension_semantics=("arbitrary",),
                                             vmem_limit_bytes=VMEM_LIMIT),
        name="finish",
    )(x, conv_out, o, ga, gb, wa_b, wo_b, ng, wr_hi, wr_lo, br, tri)


def _dest_kernel(route_ref, starts_ref, dest_ref):
    route = route_ref[...]
    lane = lax.broadcasted_iota(jnp.int32, route.shape, 1)
    out = jnp.zeros(route.shape, jnp.int32)
    for j in range(TOP_K):
        e = route[:, ROUTE_E + j:ROUTE_E + j + 1].astype(jnp.int32)
        start = jnp.sum(jnp.where(lane == e, starts_ref[...], 0.0), axis=-1, keepdims=True)
        d = (start + route[:, ROUTE_RANK + j:ROUTE_RANK + j + 1]).astype(jnp.int32)
        out = jnp.where(lane == j, d, out)
    dest_ref[...] = out


def _dest(route, starts_row, tm):
    n = route.shape[0]
    row = pl.BlockSpec((tm, LANES), lambda i: (i, 0))
    return pl.pallas_call(
        _dest_kernel,
        grid=(n // tm,),
        in_specs=[row, pl.BlockSpec((1, LANES), lambda i: (0, 0))],
        out_specs=row,
        out_shape=jax.ShapeDtypeStruct((n, LANES), jnp.int32),
        name="dest",
    )(route, starts_row)


SC_CORES = 2
SC_SUBCORES = 16
SC_WORKERS = SC_CORES * SC_SUBCORES
SC_CHUNK_BYTES = 128 * 1024


def _sc_chunk(per_worker, row_bytes):
    chunk = min(per_worker, SC_CHUNK_BYTES // row_bytes)
    assert per_worker % chunk == 0 and chunk % SUBLANES == 0, (per_worker, row_bytes)
    return chunk


def _sc_gather_rows(table, idx):
    b = idx.shape[0]
    d = table.shape[1]
    per_worker = b // SC_WORKERS
    assert per_worker * SC_WORKERS == b, (b,)
    chunk = _sc_chunk(per_worker, d * 4)
    n_chunks = per_worker // chunk
    mesh = plsc.VectorSubcoreMesh(core_axis_name="c", subcore_axis_name="s",
                                  num_cores=SC_CORES, num_subcores=SC_SUBCORES)

    @functools.partial(
        pl.kernel, mesh=mesh,
        out_type=jax.ShapeDtypeStruct((b, d), table.dtype),
        scratch_types=[pltpu.VMEM((chunk,), jnp.int32), pltpu.VMEM((chunk, d), table.dtype),
                       pltpu.SemaphoreType.DMA],
        name="sc_gather_rows",
    )
    def gather(table_hbm, idx_hbm, out_hbm, idx_v, rows_v, sem):
        worker = lax.axis_index("s") * SC_CORES + lax.axis_index("c")

        @pl.loop(0, n_chunks)
        def _(c):
            base = pl.multiple_of(worker * per_worker + c * chunk, chunk)
            pltpu.sync_copy(idx_hbm.at[pl.ds(base, chunk)], idx_v)
            pltpu.async_copy(table_hbm.at[idx_v], rows_v, sem).wait()
            pltpu.sync_copy(rows_v, out_hbm.at[pl.ds(base, chunk)])

    return gather(table, idx)


def _sc_scatter_rows(src, idx):
    n, d = src.shape
    b = idx.shape[0]
    per_worker = b // SC_WORKERS
    assert per_worker * SC_WORKERS == b and n % per_worker == 0, (b, n)
    chunk = _sc_chunk(per_worker, d * 4)
    n_chunks = per_worker // chunk
    mesh = plsc.VectorSubcoreMesh(core_axis_name="c", subcore_axis_name="s",
                                  num_cores=SC_CORES, num_subcores=SC_SUBCORES)

    @functools.partial(
        pl.kernel, mesh=mesh,
        out_type=jax.ShapeDtypeStruct((b, d), src.dtype),
        scratch_types=[pltpu.VMEM((chunk,), jnp.int32), pltpu.VMEM((chunk, d), src.dtype)],
        name="sc_scatter_rows",
    )
    def scatter(src_hbm, idx_hbm, out_hbm, idx_v, rows_v):
        worker = lax.axis_index("s") * SC_CORES + lax.axis_index("c")

        @pl.loop(0, n_chunks)
        def _(c):
            base = pl.multiple_of(worker * per_worker + c * chunk, chunk)
            src_base = pl.multiple_of(lax.rem(base, n), chunk)
            pltpu.sync_copy(idx_hbm.at[pl.ds(base, chunk)], idx_v)
            pltpu.sync_copy(src_hbm.at[pl.ds(src_base, chunk)], rows_v)
            pltpu.sync_copy(rows_v, out_hbm.at[idx_v])

    return scatter(src, idx)


def _expert_kernel(blk_ref, exp_ref, lo_ref, hi_ref, x_ref, wg_ref, wu_ref, wd_ref, yb_ref,
                   wg_b, wu_b, wd_b, held_ref):
    del blk_ref
    k = pl.program_id(0)
    lo, hi, e = lo_ref[k], hi_ref[k], exp_ref[k]

    @pl.when(k == 0)
    def _():
        held_ref[0] = -1

    @pl.when(hi > lo)
    def _():
        @pl.when(held_ref[0] != e)
        def _():
            wg_b[...] = wg_ref[0].astype(_BF16)
            wu_b[...] = wu_ref[0].astype(_BF16)
            wd_b[...] = wd_ref[0].astype(_BF16)
            held_ref[0] = e

        xb = _unpack_bf16_pairs(x_ref[...]).astype(_BF16)
        g = _dot(xb, wg_b[...])
        u = _dot(xb, wu_b[...])
        hid = g * jax.nn.sigmoid(g) * u
        y = _dot(hid.astype(_BF16), wd_b[...])
        r = lax.broadcasted_iota(jnp.int32, yb_ref.shape, 0)
        pltpu.store(yb_ref, _pack_bf16_pairs(y), mask=(r >= lo) & (r < hi))


def _experts(items, xs, w_gate, w_up, w_down):
    n_items = items[0].shape[0]
    wspec = lambda a: pl.BlockSpec((1,) + a.shape[1:], lambda k, blk, exp, lo, hi: (exp[k], 0, 0))
    rows = pl.BlockSpec((MOE_ROWS, HALF), lambda k, blk, exp, lo, hi: (blk[k], 0))
    grid_spec = pltpu.PrefetchScalarGridSpec(
        num_scalar_prefetch=4,
        grid=(n_items,),
        in_specs=[rows, wspec(w_gate), wspec(w_up), wspec(w_down)],
        out_specs=rows,
        scratch_shapes=[pltpu.VMEM(w_gate.shape[1:], _BF16), pltpu.VMEM(w_up.shape[1:], _BF16),
                        pltpu.VMEM(w_down.shape[1:], _BF16), pltpu.SMEM((1,), jnp.int32)],
    )
    return pl.pallas_call(
        _expert_kernel,
        grid_spec=grid_spec,
        out_shape=jax.ShapeDtypeStruct(xs.shape, xs.dtype),
        compiler_params=pltpu.CompilerParams(dimension_semantics=("arbitrary",),
                                             vmem_limit_bytes=VMEM_LIMIT),
        name="experts",
    )(*items, xs, w_gate, w_up, w_down)


def _combine_kernel(h_ref, route_ref, g0_ref, g1_ref, y_ref):
    route = route_ref[...]
    y_ref[...] = (h_ref[...] + route[:, ROUTE_W:ROUTE_W + 1] * _unpack_bf16_pairs(g0_ref[...])
                  + route[:, ROUTE_W + 1:ROUTE_W + 2] * _unpack_bf16_pairs(g1_ref[...]))


def _combine(h, route, g, tm):
    n = h.shape[0]
    nt = n // tm
    row = lambda w: pl.BlockSpec((tm, w), lambda i: (i, 0))
    return pl.pallas_call(
        _combine_kernel,
        grid=(nt,),
        in_specs=[row(D_MODEL), row(LANES), row(HALF), pl.BlockSpec((tm, HALF), lambda i: (nt + i, 0))],
        out_specs=row(D_MODEL),
        out_shape=jax.ShapeDtypeStruct((n, D_MODEL), _F32),
        compiler_params=pltpu.CompilerParams(dimension_semantics=("arbitrary",),
                                             vmem_limit_bytes=VMEM_LIMIT),
        name="combine",
    )(h, route, g, g)


def _work_items(counts, n_pairs):
    n_blocks = n_pairs // MOE_ROWS
    starts = jnp.cumsum(counts) - counts
    cuts = jnp.sort(jnp.concatenate([jnp.arange(n_blocks, dtype=jnp.int32) * MOE_ROWS, starts]))
    ends = jnp.concatenate([cuts[1:], jnp.full((1,), n_pairs, jnp.int32)])
    blk = jnp.minimum(cuts // MOE_ROWS, n_blocks - 1)
    expert = jnp.clip(jnp.sum(starts[None, :] <= cuts[:, None], axis=1) - 1, 0, N_EXPERTS - 1).astype(jnp.int32)
    return starts, (blk, expert, cuts - blk * MOE_ROWS, ends - blk * MOE_ROWS)


def _moe(h, hn, route, counts_rows, wg_b, wu_b, wd_b, tm):
    n = h.shape[0]
    n_pairs = n * TOP_K
    counts = counts_rows[0, :N_EXPERTS].astype(jnp.int32)
    starts, items = _work_items(counts, n_pairs)
    starts_row = jnp.zeros((1, LANES), _F32).at[0, :N_EXPERTS].set(starts.astype(_F32))
    dest = _dest(route, starts_row, min(n, DEST_ROWS))[:, :TOP_K].T.reshape(n_pairs)
    xs = _sc_scatter_rows(hn, dest)
    yb = _experts(items, xs, wg_b, wu_b, wd_b)
    g = _sc_gather_rows(yb, dest)
    return _combine(h, route, g, tm)


def kernel(x_prompt, x_sample, state_conv, cache_k, cache_v, norm_attn_g, w_in, q_norm_g, k_norm_g, rel_bias, attn_sinks, w_dw, b_dw, conv_ln_g, conv_ln_b, w_conv_out, b_conv_out, w_attn_out, w_out, norm_ffn_g, w_grp, b_grp, w_router, b_router, w_gate, w_up, w_down):
    bsz, t, _ = x_prompt.shape
    nseq, steps, _ = x_sample.shape
    row = lambda a: a.reshape(1, -1).astype(_F32)

    w_in_b = w_in.astype(_BF16)
    wco_b = w_conv_out.astype(_BF16)
    wa_b = w_attn_out.astype(_BF16)
    wo_b = w_out.astype(_BF16)
    qg = row(jnp.tile(q_norm_g, N_HEADS)) * (HEAD_DIM ** -0.5)
    kg = row(jnp.tile(k_norm_g, N_KV_HEADS))
    w_rt = jnp.zeros((D_MODEL, LANES), _F32).at[:, :N_GROUPS].set(w_grp).at[:, N_GROUPS:N_GROUPS + N_EXPERTS].set(w_router)
    wr_hi = w_rt.astype(_BF16)
    wr_lo = (w_rt - wr_hi.astype(_F32)).astype(_BF16)
    b_rt = jnp.zeros((1, LANES), _F32).at[0, :N_GROUPS].set(b_grp).at[0, N_GROUPS:N_GROUPS + N_EXPERTS].set(b_router)
    tbl = _bias_tables(rel_bias)
    conv_params = (w_dw, row(b_dw), row(conv_ln_g), row(conv_ln_b), wco_b, row(b_conv_out))

    def finish_and_moe(x2d, conv_out, o, ga, gb, tm):
        h, hn, route, counts = _finish(x2d, conv_out, o, ga, gb, wa_b, wo_b, row(norm_ffn_g), wr_hi, wr_lo, b_rt, tm)
        return _moe(h, hn, route, counts, w_gate, w_up, w_down, tm)

    xp = x_prompt.reshape(bsz * t, D_MODEL)
    glu, q, k, v, ga, gb = _inproj(xp, row(norm_attn_g), w_in_b, qg, kg, _BF16, ROW_TILE)
    conv_out = _conv_prompt(glu, bsz, t, *conv_params, ROW_TILE)
    o = _attn_prompt(q, k, v, tbl, attn_sinks, bsz, t)
    y_prompt = finish_and_moe(xp, conv_out, o, ga, gb, ROW_TILE).reshape(bsz, t, D_MODEL)
    glu3 = glu.reshape(bsz, t, D_CONV)
    state_conv_prompt = glu3[:, t - (CONV_WIDTH - 1):]
    tail = lambda a: a.reshape(bsz, t, KV_DIM)[:, t - WINDOW:].reshape(bsz, WINDOW, N_KV_HEADS, HEAD_DIM)
    cache_k_prompt, cache_v_prompt = tail(k), tail(v)

    xs = x_sample.reshape(nseq * steps, D_MODEL)
    glu, q, k, v, ga, gb = _inproj(xs, row(norm_attn_g), w_in_b, qg, kg, _F32, ROW_TILE)
    glu3 = glu.reshape(nseq, steps, D_CONV)
    hist = jnp.concatenate([state_conv, glu3], axis=1)
    conv_out = _conv_sample(hist.transpose(1, 0, 2), *conv_params, 64)
    conv_out = conv_out.transpose(1, 0, 2).reshape(nseq * steps, D_MODEL)
    k3 = k.reshape(nseq, steps, KV_DIM)
    v3 = v.reshape(nseq, steps, KV_DIM)
    o, ck_t, cv_t = _attn_sample(q.reshape(nseq, steps, Q_DIM), k3, v3, cache_k.transpose(0, 2, 3, 1),
                                 cache_v.transpose(0, 2, 3, 1), tbl, attn_sinks, 8)
    y_sample = finish_and_moe(xs, conv_out, o.reshape(nseq * steps, Q_DIM), ga, gb, ROW_TILE).reshape(nseq, steps, D_MODEL)
    state_conv_sample = hist[:, steps:]
    cache_k_sample = ck_t.transpose(0, 3, 1, 2)
    cache_v_sample = cv_t.transpose(0, 3, 1, 2)

    return (y_prompt, y_sample, state_conv_prompt, cache_k_prompt, cache_v_prompt,
            state_conv_sample, cache_k_sample, cache_v_sample)
```

```python
import functools
import math

import numpy as np
import jax
import jax.numpy as jnp
from jax import lax
from jax.experimental import pallas as pl
from jax.experimental.pallas import tpu as pltpu
from jax.experimental.pallas import tpu_sc as plsc

D_MODEL = 1024
N_HEADS = 16
HEAD_DIM = 64
N_KV_HEADS = 4
WINDOW = 128
Q_DIM = N_HEADS * HEAD_DIM
KV_DIM = N_KV_HEADS * HEAD_DIM
N_BUCKETS = 32
MAX_EXACT = N_BUCKETS // 2
MAX_DISTANCE = 128
D_CONV = D_MODEL
CONV_WIDTH = 31
N_GROUPS = 4
EXPERTS_PER_GROUP = 8
N_EXPERTS = N_GROUPS * EXPERTS_PER_GROUP
TOP_K = 2
D_EXPERT = 256
EPS = 1e-6

LANES = 128
SUBLANES = 8
N_PAIRS = N_HEADS // 2
MOE_ROWS = 512
MASK_VALUE = -1e30
VMEM_LIMIT = 56 * 1024 * 1024
ROW_TILE = 512

_F32 = jnp.float32
_BF16 = jnp.bfloat16


def _resident(a):
    return pl.BlockSpec(a.shape, lambda *_: (0,) * a.ndim, pipeline_mode=pl.Buffered(1))


def _dot(a, b):
    return jnp.dot(a, b, preferred_element_type=_F32)


def _split_bf16(x):
    hi = x.astype(_BF16)
    lo = (x - hi.astype(_F32)).astype(_BF16)
    return hi, lo


HALF = D_MODEL // 2


def _pack_bf16_pairs(x):
    lo = pltpu.bitcast(x[:, :HALF].astype(_BF16).astype(_F32), jnp.uint32)
    hi = pltpu.bitcast(x[:, HALF:].astype(_BF16).astype(_F32), jnp.uint32)
    return hi | (lo >> 16)


def _unpack_bf16_pairs(w):
    lo = pltpu.bitcast(w << 16, _F32)
    hi = pltpu.bitcast(w & jnp.uint32(0xFFFF0000), _F32)
    return jnp.concatenate([lo, hi], axis=1)


def _head_rms_scale(z):
    low = lax.broadcasted_iota(jnp.int32, (z.shape[0], LANES), 1) < HEAD_DIM
    slabs = []
    for c in range(z.shape[1] // LANES):
        sq = z[:, c * LANES:(c + 1) * LANES]
        sq = sq * sq
        first = jnp.sum(jnp.where(low, sq, 0.0), axis=-1, keepdims=True)
        second = jnp.sum(jnp.where(low, 0.0, sq), axis=-1, keepdims=True)
        slabs.append(lax.rsqrt(jnp.where(low, first, second) * (1.0 / HEAD_DIM) + EPS))
    return jnp.concatenate(slabs, axis=1)


def _inproj_kernel(x_ref, g_ref, w_ref, qg_ref, kg_ref,
                   glu_ref, q_ref, k_ref, v_ref, ga_ref, gb_ref):
    x = x_ref[...]
    xn = x * lax.rsqrt(jnp.mean(x * x, axis=-1, keepdims=True) + EPS) * g_ref[...]
    xb = xn.astype(_BF16)

    def seg(lo, width):
        return _dot(xb, w_ref[:, lo:lo + width])

    a = seg(0, D_CONV)
    b = seg(D_CONV, D_CONV)
    glu_ref[...] = a * jax.nn.sigmoid(b)
    off = 2 * D_CONV
    q = seg(off, Q_DIM)
    q_ref[...] = (q * _head_rms_scale(q) * qg_ref[...]).astype(q_ref.dtype)
    off += Q_DIM
    k = seg(off, KV_DIM)
    k_ref[...] = k * _head_rms_scale(k) * kg_ref[...]
    off += KV_DIM
    v_ref[...] = seg(off, KV_DIM)
    off += KV_DIM
    ga_ref[...] = jax.nn.sigmoid(seg(off, D_MODEL)).astype(ga_ref.dtype)
    off += D_MODEL
    gb_ref[...] = jax.nn.sigmoid(seg(off, D_MODEL)).astype(gb_ref.dtype)


def _inproj(x, g, w_in_b, qg, kg, q_dtype, tm):
    n = x.shape[0]
    in_dim = w_in_b.shape[1]
    row = lambda w: pl.BlockSpec((tm, w), lambda i: (i, 0))
    full = _resident
    return pl.pallas_call(
        _inproj_kernel,
        grid=(n // tm,),
        in_specs=[row(D_MODEL), full(g), full(w_in_b), full(qg), full(kg)],
        out_specs=[row(D_CONV), row(Q_DIM), row(KV_DIM), row(KV_DIM), row(D_MODEL), row(D_MODEL)],
        out_shape=[jax.ShapeDtypeStruct((n, D_CONV), _F32),
                   jax.ShapeDtypeStruct((n, Q_DIM), q_dtype),
                   jax.ShapeDtypeStruct((n, KV_DIM), _F32),
                   jax.ShapeDtypeStruct((n, KV_DIM), _F32),
                   jax.ShapeDtypeStruct((n, D_MODEL), _BF16),
                   jax.ShapeDtypeStruct((n, D_MODEL), _BF16)],
        compiler_params=pltpu.CompilerParams(dimension_semantics=("arbitrary",),
                                             vmem_limit_bytes=VMEM_LIMIT),
        name="inproj",
    )(x, g, w_in_b, qg, kg)


def _ln_swish_project(y, lng_ref, lnb_ref, wo_ref, bo_ref):
    mu = jnp.mean(y, axis=-1, keepdims=True)
    yc = y - mu
    var = jnp.mean(yc * yc, axis=-1, keepdims=True)
    z = yc * lax.rsqrt(var + EPS) * lng_ref[...] + lnb_ref[...]
    z = z * jax.nn.sigmoid(z)
    return (_dot(z.astype(_BF16), wo_ref[...]) + bo_ref[...]).astype(_BF16)


HALO = 32
CONV_STEPS = 16
CH_TILES = D_CONV // LANES


def _conv_prompt_kernel(glu_ref, w8_ref, b8_ref, lng_ref, lnb_ref, wo_ref, bo_ref,
                        out_ref, hist_ref, y_ref, *, tm):
    i = pl.program_id(1)

    @pl.when(i == 0)
    def _():
        hist_ref[0:HALO * CH_TILES, :] = jnp.zeros((HALO * CH_TILES, LANES), _F32)

    @pl.when(i > 0)
    def _():
        hist_ref[0:HALO * CH_TILES, :] = hist_ref[tm * CH_TILES:(tm + HALO) * CH_TILES, :]

    for c in range(CH_TILES):
        hist_ref[pl.ds(HALO * CH_TILES + c, tm, stride=CH_TILES), :] = glu_ref[:, c * LANES:(c + 1) * LANES]

    first = HALO - (CONV_WIDTH - 1)
    span = CONV_STEPS + CONV_WIDTH - 1

    def chunk(ci, carry):
        t0 = ci * CONV_STEPS
        x = hist_ref[pl.ds(pl.multiple_of((t0 + first) * CH_TILES, CH_TILES), span * CH_TILES), :]
        x = x.reshape(span, CH_TILES, LANES)
        acc = jnp.broadcast_to(b8_ref[...][None], (CONV_STEPS, CH_TILES, LANES))
        for j in range(CONV_WIDTH):
            acc = acc + x[j:j + CONV_STEPS] * w8_ref[j][None]
        y_ref[pl.ds(pl.multiple_of(t0 * CH_TILES, CH_TILES), CONV_STEPS * CH_TILES), :] = (
            acc.reshape(CONV_STEPS * CH_TILES, LANES))
        return carry

    lax.fori_loop(0, tm // CONV_STEPS, chunk, 0)
    y = jnp.concatenate([y_ref[pl.ds(c, tm, stride=CH_TILES), :] for c in range(CH_TILES)], axis=1)
    out_ref[...] = _ln_swish_project(y, lng_ref, lnb_ref, wo_ref, bo_ref)


def _conv_prompt(glu, bsz, t, w_dw, b_dw, lng, lnb, wo_b, bo, tm):
    assert CH_TILES == SUBLANES
    nt = t // tm
    w8 = w_dw.reshape(CONV_WIDTH, CH_TILES, LANES)
    b8 = b_dw.reshape(CH_TILES, LANES)
    full = _resident
    row = pl.BlockSpec((tm, D_CONV), lambda b, i: (b * nt + i, 0))
    return pl.pallas_call(
        functools.partial(_conv_prompt_kernel, tm=tm),
        grid=(bsz, nt),
        in_specs=[row, full(w8), full(b8), full(lng), full(lnb), full(wo_b), full(bo)],
        out_specs=pl.BlockSpec((tm, D_MODEL), lambda b, i: (b * nt + i, 0)),
        out_shape=jax.ShapeDtypeStruct((bsz * t, D_MODEL), _BF16),
        scratch_shapes=[pltpu.VMEM(((tm + HALO) * CH_TILES, LANES), _F32), pltpu.VMEM((tm * CH_TILES, LANES), _F32)],
        compiler_params=pltpu.CompilerParams(dimension_semantics=("arbitrary", "arbitrary"),
                                             vmem_limit_bytes=VMEM_LIMIT),
        name="conv_prompt",
    )(glu, w8, b8, lng, lnb, wo_b, bo)


def _conv_sample_kernel(state_ref, glu_ref, wdw_ref, bdw_ref, lng_ref, lnb_ref, wo_ref, bo_ref,
                        out_ref, state_out_ref):
    keep, steps = state_ref.shape[0], glu_ref.shape[0]

    def hist(u):
        return state_ref[u] if u < keep else glu_ref[u - keep]

    for t in range(steps):
        acc = hist(t) * wdw_ref[0:1, :]
        for j in range(1, CONV_WIDTH):
            acc = acc + hist(t + j) * wdw_ref[j:j + 1, :]
        out_ref[t] = _ln_swish_project(acc + bdw_ref[...], lng_ref, lnb_ref, wo_ref, bo_ref)
    state_out_ref[0:keep - steps] = state_ref[steps:keep]
    state_out_ref[keep - steps:keep] = glu_ref[...]


def _conv_sample(state_t, glu_t, w_dw, b_dw, lng, lnb, wo_b, bo, sb):
    keep, nseq, _ = state_t.shape
    steps = glu_t.shape[0]
    full = _resident
    blk = lambda r: pl.BlockSpec((r, sb, D_CONV), lambda i: (0, i, 0))
    return pl.pallas_call(
        _conv_sample_kernel,
        grid=(nseq // sb,),
        in_specs=[blk(keep), blk(steps), full(w_dw), full(b_dw), full(lng), full(lnb), full(wo_b), full(bo)],
        out_specs=[blk(steps), blk(keep)],
        out_shape=[jax.ShapeDtypeStruct((steps, nseq, D_MODEL), _BF16),
                   jax.ShapeDtypeStruct(state_t.shape, _F32)],
        compiler_params=pltpu.CompilerParams(dimension_semantics=("arbitrary",),
                                             vmem_limit_bytes=VMEM_LIMIT),
        name="conv_sample",
    )(state_t, glu_t, w_dw, b_dw, lng, lnb, wo_b, bo)


def _bucket_map():
    i = np.arange(WINDOW)[:, None]
    j = np.arange(WINDOW)[None, :]
    n = (i - j) % WINDOW
    nf = np.maximum(n, 1).astype(np.float32)
    large = MAX_EXACT + (np.log(nf / np.float32(MAX_EXACT)) / np.float32(math.log(MAX_DISTANCE / MAX_EXACT))
                         * np.float32(N_BUCKETS - MAX_EXACT)).astype(np.int32)
    return np.where(n < MAX_EXACT, n, np.minimum(large, N_BUCKETS - 1)).astype(np.int32)


def _bias_table_kernel(rb_ref, bm_ref, tbl_ref):
    p = pl.program_id(0)
    bm = bm_ref[...]
    for half in range(2):
        h = 2 * p + half
        t = jnp.zeros(bm.shape, _F32)
        for b in range(N_BUCKETS):
            t = jnp.where(bm == b, rb_ref[b, h], t)
        tbl_ref[0, :, half * WINDOW:(half + 1) * WINDOW] = t


def _bias_tables(rel_bias):
    bm = jnp.asarray(_bucket_map())
    return pl.pallas_call(
        _bias_table_kernel,
        grid=(N_PAIRS,),
        in_specs=[pl.BlockSpec(memory_space=pltpu.SMEM), pl.BlockSpec(bm.shape, lambda p: (0, 0))],
        out_specs=pl.BlockSpec((1, WINDOW, 2 * WINDOW), lambda p: (p, 0, 0)),
        out_shape=jax.ShapeDtypeStruct((N_PAIRS, WINDOW, 2 * WINDOW), _F32),
        name="bias_tables",
    )(rel_bias, bm)


def _block_diag_pairs(slab):
    low = lax.broadcasted_iota(jnp.int32, slab.shape, 1) < HEAD_DIM
    swapped = pltpu.roll(slab, HEAD_DIM, axis=1)
    zero = jnp.zeros_like(slab)
    first = jnp.concatenate([jnp.where(low, slab, zero), jnp.where(low, zero, swapped)], axis=0)
    second = jnp.concatenate([jnp.where(low, swapped, zero), jnp.where(low, zero, slab)], axis=0)
    return first.astype(_BF16), second.astype(_BF16)


def _kv_operands(k_blk, v_blk):
    ops = []
    for slab in range(KV_DIM // LANES):
        cols = slice(slab * LANES, (slab + 1) * LANES)
        ops.extend(zip(_block_diag_pairs(k_blk[:, cols]), _block_diag_pairs(v_blk[:, cols])))
    return ops


def _attend(q, prev_ops, own_ops, tbl_ref, sink_ref, prev_shift, store, transposed=False):
    tq = q.shape[0]
    rows = 2 * tq
    row = lax.broadcasted_iota(jnp.int32, (rows, 2 * WINDOW), 0)
    col = lax.broadcasted_iota(jnp.int32, (rows, 2 * WINDOW), 1)
    from_prev = (col & (WINDOW - 1)) > jnp.where(row >= tq, row - tq, row)
    top = lax.broadcasted_iota(jnp.int32, (rows, 1), 0) < tq
    low = lax.broadcasted_iota(jnp.int32, (rows, LANES), 1) < HEAD_DIM
    contract_last = (((1,), (1,)), ((), ()))

    def logits(a, k_op):
        return _dot(a, k_op) if transposed else lax.dot_general(a, k_op, contract_last, preferred_element_type=_F32)

    def weighted_values(pr, v_op):
        return lax.dot_general(pr, v_op, contract_last, preferred_element_type=_F32) if transposed else _dot(pr, v_op)

    for kvh in range(N_KV_HEADS):
        (k_prev, v_prev), (k_own, v_own) = prev_ops[kvh], own_ops[kvh]
        pair_a = 2 * kvh
        pair_b = pair_a + 1
        qq = jnp.concatenate([q[:, pair_a * LANES:(pair_a + 1) * LANES],
                              q[:, pair_b * LANES:(pair_b + 1) * LANES]], axis=0).astype(_BF16)
        sp = logits(qq, k_prev)
        so = logits(qq, k_own)
        bias = jnp.concatenate([tbl_ref[pair_a, 0:tq, :], tbl_ref[pair_b, 0:tq, :]], axis=0)
        s = jnp.where(from_prev, sp + prev_shift, so) + bias
        sink_even = jnp.where(top, sink_ref[2 * pair_a], sink_ref[2 * pair_b])
        sink_odd = jnp.where(top, sink_ref[2 * pair_a + 1], sink_ref[2 * pair_b + 1])
        m_even = jnp.maximum(jnp.max(s[:, :WINDOW], axis=-1, keepdims=True), sink_even)
        m_odd = jnp.maximum(jnp.max(s[:, WINDOW:], axis=-1, keepdims=True), sink_odd)
        p = jnp.exp(s - jnp.where(col < WINDOW, m_even, m_odd)).astype(_BF16)
        zero = jnp.zeros_like(p)
        o = (weighted_values(jnp.where(from_prev, p, zero), v_prev)
             + weighted_values(jnp.where(from_prev, zero, p), v_own))
        pf = p.astype(_F32)
        sums = jnp.where(low, jnp.sum(pf[:, :WINDOW], axis=-1, keepdims=True),
                         jnp.sum(pf[:, WINDOW:], axis=-1, keepdims=True))
        den = sums + jnp.where(low, jnp.exp(sink_even - m_even), jnp.exp(sink_odd - m_odd))
        o = o / den
        store(pair_a, o[:tq])
        store(pair_b, o[tq:])


PROMPT_QBLOCKS = 4


def _attn_prompt_kernel(sink_ref, q_ref, kp_ref, ko_ref, vp_ref, vo_ref, tbl_ref, o_ref):
    prev_shift = jnp.where(pl.program_id(1) == 0, MASK_VALUE, 0.0).astype(_F32)
    ops = [_kv_operands(kp_ref[...], vp_ref[...])]
    for b in range(PROMPT_QBLOCKS):
        rows = slice(b * WINDOW, (b + 1) * WINDOW)
        ops.append(_kv_operands(ko_ref[rows, :], vo_ref[rows, :]))

        def store(pair, o, rows=rows):
            o_ref[rows, pair * LANES:(pair + 1) * LANES] = o.astype(o_ref.dtype)

        _attend(q_ref[rows, :], ops[b], ops[b + 1], tbl_ref, sink_ref,
                prev_shift if b == 0 else jnp.float32(0.0), store)


def _attn_prompt(q, k, v, tbl, sinks, bsz, t):
    tq = PROMPT_QBLOCKS * WINDOW
    nb = t // tq
    own = lambda w: pl.BlockSpec((tq, w), lambda b, i: (b * nb + i, 0))
    prev = lambda w: pl.BlockSpec((WINDOW, w),
                                  lambda b, i: (PROMPT_QBLOCKS * (b * nb + i) - jnp.minimum(i, 1), 0))
    return pl.pallas_call(
        _attn_prompt_kernel,
        grid=(bsz, nb),
        in_specs=[pl.BlockSpec(memory_space=pltpu.SMEM), own(Q_DIM), prev(KV_DIM), own(KV_DIM),
                  prev(KV_DIM), own(KV_DIM), pl.BlockSpec(tbl.shape, lambda b, i: (0, 0, 0))],
        out_specs=own(Q_DIM),
        out_shape=jax.ShapeDtypeStruct((bsz * t, Q_DIM), _BF16),
        compiler_params=pltpu.CompilerParams(dimension_semantics=("arbitrary", "arbitrary"),
                                             vmem_limit_bytes=VMEM_LIMIT),
        name="attn_prompt",
    )(sinks, q, k, k, v, v, tbl)


SAMPLE_UNROLL = 2


def _block_diag_t(x):
    xb = x.astype(_BF16)
    z = jnp.zeros_like(xb)
    return jnp.concatenate([jnp.concatenate([xb, z], axis=1), jnp.concatenate([z, xb], axis=1)], axis=0)


def _attn_sample_kernel(sink_ref, q_ref, kn_ref, vn_ref, ck_ref, cv_ref, tbl_ref, o_ref, cko_ref, cvo_ref,
                        *, sb, steps):
    pad = jnp.zeros((WINDOW - steps, LANES), _F32)
    lane = lax.broadcasted_iota(jnp.int32, (HEAD_DIM, WINDOW), 1)

    def one_sequence(s, carry):
        def store(pair, o):
            o_ref[s, :, pair * LANES:(pair + 1) * LANES] = o

        prev_ops, own_ops = [], []
        for slab in range(KV_DIM // LANES):
            cols = slice(slab * LANES, (slab + 1) * LANES)
            new_k = jnp.concatenate([kn_ref[s][:, cols], pad], axis=0).T
            new_v = jnp.concatenate([vn_ref[s][:, cols], pad], axis=0).T
            for sub in range(2):
                kvh = 2 * slab + sub
                part = slice(sub * HEAD_DIM, (sub + 1) * HEAD_DIM)
                kt, vt = ck_ref[s, kvh], cv_ref[s, kvh]
                cko_ref[s, kvh] = pltpu.roll(jnp.where(lane < steps, new_k[part], kt), WINDOW - steps, axis=1)
                cvo_ref[s, kvh] = pltpu.roll(jnp.where(lane < steps, new_v[part], vt), WINDOW - steps, axis=1)
                prev_ops.append((_block_diag_t(kt), _block_diag_t(vt)))
                own_ops.append((_block_diag_t(new_k[part]), _block_diag_t(new_v[part])))
        _attend(q_ref[s], prev_ops, own_ops, tbl_ref, sink_ref, jnp.float32(0.0), store, transposed=True)
        return carry

    lax.fori_loop(0, sb, one_sequence, 0, unroll=SAMPLE_UNROLL)


def _attn_sample(q, k_new, v_new, cache_kt, cache_vt, tbl, sinks, sb):
    nseq, steps, _ = q.shape
    seq = lambda r, w: pl.BlockSpec((sb, r, w), lambda i: (i, 0, 0))
    cache = pl.BlockSpec((sb, N_KV_HEADS, HEAD_DIM, WINDOW), lambda i: (i, 0, 0, 0))
    return pl.pallas_call(
        functools.partial(_attn_sample_kernel, sb=sb, steps=steps),
        grid=(nseq // sb,),
        in_specs=[pl.BlockSpec(memory_space=pltpu.SMEM), seq(steps, Q_DIM), seq(steps, KV_DIM), seq(steps, KV_DIM),
                  cache, cache, pl.BlockSpec(tbl.shape, lambda i: (0, 0, 0))],
        out_specs=[seq(steps, Q_DIM), cache, cache],
        out_shape=[jax.ShapeDtypeStruct((nseq, steps, Q_DIM), _F32),
                   jax.ShapeDtypeStruct(cache_kt.shape, _F32), jax.ShapeDtypeStruct(cache_vt.shape, _F32)],
        compiler_params=pltpu.CompilerParams(dimension_semantics=("arbitrary",),
                                             vmem_limit_bytes=VMEM_LIMIT),
        name="attn_sample",
    )(sinks, q, k_new, v_new, cache_kt, cache_vt, tbl)


def _lane_min_index(mask, lane):
    return jnp.min(jnp.where(mask, lane, LANES), axis=-1, keepdims=True)


def _finish_kernel(x_ref, conv_ref, o_ref, ga_ref, gb_ref, wa_ref, wo_ref, ng_ref, wr_hi_ref, wr_lo_ref, br_ref,
                   tri_ref, h_ref, hn_ref, route_ref, count_ref, running_ref):
    @pl.when(pl.program_id(0) == 0)
    def _():
        running_ref[...] = jnp.zeros_like(running_ref)

    attn_out = _dot(o_ref[...].astype(_BF16), wa_ref[...])
    merged = ga_ref[...].astype(_F32) * conv_ref[...].astype(_F32) + gb_ref[...].astype(_F32) * attn_out
    h = x_ref[...] + _dot(merged.astype(_BF16), wo_ref[...])
    h_ref[...] = h
    hn = h * lax.rsqrt(jnp.mean(h * h, axis=-1, keepdims=True) + EPS) * ng_ref[...]
    hn_ref[...] = _pack_bf16_pairs(hn)

    hi, lo = _split_bf16(hn)
    logits = _dot(hi, wr_hi_ref[...]) + _dot(lo, wr_hi_ref[...]) + _dot(hi, wr_lo_ref[...]) + br_ref[...]
    lane = lax.broadcasted_iota(jnp.int32, logits.shape, 1)
    gmask = lane < N_GROUPS
    gl = jnp.where(gmask, logits, MASK_VALUE)
    gmax = jnp.max(gl, axis=-1, keepdims=True)
    grp = _lane_min_index(gmask & (gl == gmax), lane)
    p_grp = 1.0 / jnp.sum(jnp.where(gmask, jnp.exp(gl - gmax), 0.0), axis=-1, keepdims=True)
    e_lo = N_GROUPS + grp * EXPERTS_PER_GROUP
    emask = (lane >= e_lo) & (lane < e_lo + EXPERTS_PER_GROUP)
    el = jnp.where(emask, logits, MASK_VALUE)
    ex = jnp.where(emask, jnp.exp(el - jnp.max(el, axis=-1, keepdims=True)), 0.0)
    prob = jnp.where(emask, ex / jnp.sum(ex, axis=-1, keepdims=True), -1.0)
    p1 = jnp.max(prob, axis=-1, keepdims=True)
    i1 = _lane_min_index(prob == p1, lane)
    rest = jnp.where(lane == i1, -1.0, prob)
    p2 = jnp.max(rest, axis=-1, keepdims=True)
    i2 = _lane_min_index(rest == p2, lane)
    w1 = p_grp * p1 / (p1 + p2)
    w2 = p_grp * p2 / (p1 + p2)
    e1 = i1 - N_GROUPS
    e2 = i2 - N_GROUPS

    hot1 = lane == e1
    hot2 = lane == e2
    hot = jnp.where(hot1 | hot2, 1.0, 0.0)
    before = _dot(tri_ref[...], hot.astype(_BF16)) + running_ref[...]
    rank1 = jnp.sum(jnp.where(hot1, before, 0.0), axis=-1, keepdims=True)
    rank2 = jnp.sum(jnp.where(hot2, before, 0.0), axis=-1, keepdims=True)
    running_ref[...] += jnp.sum(hot, axis=0, keepdims=True)
    count_ref[...] = jnp.broadcast_to(running_ref[...], count_ref.shape)

    fields = (e1.astype(_F32), e2.astype(_F32), w1, w2, rank1, rank2)
    route = jnp.zeros(logits.shape, _F32)
    for pos, val in enumerate(fields):
        route = jnp.where(lane == pos, val, route)
    route_ref[...] = route


ROUTE_E, ROUTE_W, ROUTE_RANK = 0, 2, 4
DEST_ROWS = 1024


def _finish(x, conv_out, o, ga, gb, wa_b, wo_b, ng, wr_hi, wr_lo, br, tm):
    n = x.shape[0]
    tri = jnp.asarray(np.tril(np.ones((tm, tm), np.float32), -1), _BF16)
    row = lambda w: pl.BlockSpec((tm, w), lambda i: (i, 0))
    full = _resident
    return pl.pallas_call(
        _finish_kernel,
        grid=(n // tm,),
        in_specs=[row(D_MODEL), row(D_MODEL), row(Q_DIM), row(D_MODEL), row(D_MODEL),
                  full(wa_b), full(wo_b), full(ng), full(wr_hi), full(wr_lo), full(br), full(tri)],
        out_specs=[row(D_MODEL), row(HALF), row(LANES), pl.BlockSpec((SUBLANES, LANES), lambda i: (0, 0))],
        out_shape=[jax.ShapeDtypeStruct((n, D_MODEL), _F32),
                   jax.ShapeDtypeStruct((n, HALF), jnp.uint32),
                   jax.ShapeDtypeStruct((n, LANES), _F32),
                   jax.ShapeDtypeStruct((SUBLANES, LANES), _F32)],
        scratch_shapes=[pltpu.VMEM((1, LANES), _F32)],
        compiler_params=pltpu.CompilerParams(dimension_semantics=("arbitrary",),
                                             vmem_limit_bytes=VMEM_LIMIT),
        name="finish",
    )(x, conv_out, o, ga, gb, wa_b, wo_b, ng, wr_hi, wr_lo, br, tri)


def _dest_kernel(route_ref, starts_ref, dest_ref):
    route = route_ref[...]
    lane = lax.broadcasted_iota(jnp.int32, route.shape, 1)
    out = jnp.zeros(route.shape, jnp.int32)
    for j in range(TOP_K):
        e = route[:, ROUTE_E + j:ROUTE_E + j + 1].astype(jnp.int32)
        start = jnp.sum(jnp.where(lane == e, starts_ref[...], 0.0), axis=-1, keepdims=True)
        d = (start + route[:, ROUTE_RANK + j:ROUTE_RANK + j + 1]).astype(jnp.int32)
        out = jnp.where(lane == j, d, out)
    dest_ref[...] = out


def _dest(route, starts_row, tm):
    n = route.shape[0]
    row = pl.BlockSpec((tm, LANES), lambda i: (i, 0))
    return pl.pallas_call(
        _dest_kernel,
        grid=(n // tm,),
        in_specs=[row, pl.BlockSpec((1, LANES), lambda i: (0, 0))],
        out_specs=row,
        out_shape=jax.ShapeDtypeStruct((n, LANES), jnp.int32),
        name="dest",
    )(route, starts_row)


SC_CORES = 2
SC_SUBCORES = 16
SC_WORKERS = SC_CORES * SC_SUBCORES
SC_CHUNK_BYTES = 128 * 1024


def _sc_chunk(per_worker, row_bytes):
    chunk = min(per_worker, SC_CHUNK_BYTES // row_bytes)
    assert per_worker % chunk == 0 and chunk % SUBLANES == 0, (per_worker, row_bytes)
    return chunk


def _sc_gather_rows(table, idx):
    b = idx.shape[0]
    d = table.shape[1]
    per_worker = b // SC_WORKERS
    assert per_worker * SC_WORKERS == b, (b,)
    chunk = _sc_chunk(per_worker, d * 4)
    n_chunks = per_worker // chunk
    mesh = plsc.VectorSubcoreMesh(core_axis_name="c", subcore_axis_name="s",
                                  num_cores=SC_CORES, num_subcores=SC_SUBCORES)

    @functools.partial(
        pl.kernel, mesh=mesh,
        out_type=jax.ShapeDtypeStruct((b, d), table.dtype),
        scratch_types=[pltpu.VMEM((chunk,), jnp.int32), pltpu.VMEM((chunk, d), table.dtype),
                       pltpu.SemaphoreType.DMA],
        name="sc_gather_rows",
    )
    def gather(table_hbm, idx_hbm, out_hbm, idx_v, rows_v, sem):
        worker = lax.axis_index("s") * SC_CORES + lax.axis_index("c")

        @pl.loop(0, n_chunks)
        def _(c):
            base = pl.multiple_of(worker * per_worker + c * chunk, chunk)
            pltpu.sync_copy(idx_hbm.at[pl.ds(base, chunk)], idx_v)
            pltpu.async_copy(table_hbm.at[idx_v], rows_v, sem).wait()
            pltpu.sync_copy(rows_v, out_hbm.at[pl.ds(base, chunk)])

    return gather(table, idx)


def _sc_scatter_rows(src, idx):
    n, d = src.shape
    b = idx.shape[0]
    per_worker = b // SC_WORKERS
    assert per_worker * SC_WORKERS == b and n % per_worker == 0, (b, n)
    chunk = _sc_chunk(per_worker, d * 4)
    n_chunks = per_worker // chunk
    mesh = plsc.VectorSubcoreMesh(core_axis_name="c", subcore_axis_name="s",
                                  num_cores=SC_CORES, num_subcores=SC_SUBCORES)

    @functools.partial(
        pl.kernel, mesh=mesh,
        out_type=jax.ShapeDtypeStruct((b, d), src.dtype),
        scratch_types=[pltpu.VMEM((chunk,), jnp.int32), pltpu.VMEM((chunk, d), src.dtype)],
        name="sc_scatter_rows",
    )
    def scatter(src_hbm, idx_hbm, out_hbm, idx_v, rows_v):
        worker = lax.axis_index("s") * SC_CORES + lax.axis_index("c")

        @pl.loop(0, n_chunks)
        def _(c):
            base = pl.multiple_of(worker * per_worker + c * chunk, chunk)
            src_base = pl.multiple_of(lax.rem(base, n), chunk)
            pltpu.sync_copy(idx_hbm.at[pl.ds(base, chunk)], idx_v)
            pltpu.sync_copy(src_hbm.at[pl.ds(src_base, chunk)], rows_v)
            pltpu.sync_copy(rows_v, out_hbm.at[idx_v])

    return scatter(src, idx)


def _expert_kernel(blk_ref, exp_ref, lo_ref, hi_ref, x_ref, wg_ref, wu_ref, wd_ref, yb_ref,
                   wg_b, wu_b, wd_b, held_ref):
    del blk_ref
    k = pl.program_id(0)
    lo, hi, e = lo_ref[k], hi_ref[k], exp_ref[k]

    @pl.when(k == 0)
    def _():
        held_ref[0] = -1

    @pl.when(hi > lo)
    def _():
        @pl.when(held_ref[0] != e)
        def _():
            wg_b[...] = wg_ref[0].astype(_BF16)
            wu_b[...] = wu_ref[0].astype(_BF16)
            wd_b[...] = wd_ref[0].astype(_BF16)
            held_ref[0] = e

        xb = _unpack_bf16_pairs(x_ref[...]).astype(_BF16)
        g = _dot(xb, wg_b[...])
        u = _dot(xb, wu_b[...])
        hid = g * jax.nn.sigmoid(g) * u
        y = _dot(hid.astype(_BF16), wd_b[...])
        r = lax.broadcasted_iota(jnp.int32, yb_ref.shape, 0)
        pltpu.store(yb_ref, _pack_bf16_pairs(y), mask=(r >= lo) & (r < hi))


def _experts(items, xs, w_gate, w_up, w_down):
    n_items = items[0].shape[0]
    wspec = lambda a: pl.BlockSpec((1,) + a.shape[1:], lambda k, blk, exp, lo, hi: (exp[k], 0, 0))
    rows = pl.BlockSpec((MOE_ROWS, HALF), lambda k, blk, exp, lo, hi: (blk[k], 0))
    grid_spec = pltpu.PrefetchScalarGridSpec(
        num_scalar_prefetch=4,
        grid=(n_items,),
        in_specs=[rows, wspec(w_gate), wspec(w_up), wspec(w_down)],
        out_specs=rows,
        scratch_shapes=[pltpu.VMEM(w_gate.shape[1:], _BF16), pltpu.VMEM(w_up.shape[1:], _BF16),
                        pltpu.VMEM(w_down.shape[1:], _BF16), pltpu.SMEM((1,), jnp.int32)],
    )
    return pl.pallas_call(
        _expert_kernel,
        grid_spec=grid_spec,
        out_shape=jax.ShapeDtypeStruct(xs.shape, xs.dtype),
        compiler_params=pltpu.CompilerParams(dimension_semantics=("arbitrary",),
                                             vmem_limit_bytes=VMEM_LIMIT),
        name="experts",
    )(*items, xs, w_gate, w_up, w_down)


def _combine_kernel(h_ref, route_ref, g0_ref, g1_ref, y_ref):
    route = route_ref[...]
    y_ref[...] = (h_ref[...] + route[:, ROUTE_W:ROUTE_W + 1] * _unpack_bf16_pairs(g0_ref[...])
                  + route[:, ROUTE_W + 1:ROUTE_W + 2] * _unpack_bf16_pairs(g1_ref[...]))


def _combine(h, route, g, tm):
    n = h.shape[0]
    nt = n // tm
    row = lambda w: pl.BlockSpec((tm, w), lambda i: (i, 0))
    return pl.pallas_call(
        _combine_kernel,
        grid=(nt,),
        in_specs=[row(D_MODEL), row(LANES), row(HALF), pl.BlockSpec((tm, HALF), lambda i: (nt + i, 0))],
        out_specs=row(D_MODEL),
        out_shape=jax.ShapeDtypeStruct((n, D_MODEL), _F32),
        compiler_params=pltpu.CompilerParams(dimension_semantics=("arbitrary",),
                                             vmem_limit_bytes=VMEM_LIMIT),
        name="combine",
    )(h, route, g, g)


def _work_items(counts, n_pairs):
    n_blocks = n_pairs // MOE_ROWS
    starts = jnp.cumsum(counts) - counts
    cuts = jnp.sort(jnp.concatenate([jnp.arange(n_blocks, dtype=jnp.int32) * MOE_ROWS, starts]))
    ends = jnp.concatenate([cuts[1:], jnp.full((1,), n_pairs, jnp.int32)])
    blk = jnp.minimum(cuts // MOE_ROWS, n_blocks - 1)
    expert = jnp.clip(jnp.sum(starts[None, :] <= cuts[:, None], axis=1) - 1, 0, N_EXPERTS - 1).astype(jnp.int32)
    return starts, (blk, expert, cuts - blk * MOE_ROWS, ends - blk * MOE_ROWS)


def _moe(h, hn, route, counts_rows, wg_b, wu_b, wd_b, tm):
    n = h.shape[0]
    n_pairs = n * TOP_K
    counts = counts_rows[0, :N_EXPERTS].astype(jnp.int32)
    starts, items = _work_items(counts, n_pairs)
    starts_row = jnp.zeros((1, LANES), _F32).at[0, :N_EXPERTS].set(starts.astype(_F32))
    dest = _dest(route, starts_row, min(n, DEST_ROWS))[:, :TOP_K].T.reshape(n_pairs)
    xs = _sc_scatter_rows(hn, dest)
    yb = _experts(items, xs, wg_b, wu_b, wd_b)
    g = _sc_gather_rows(yb, dest)
    return _combine(h, route, g, tm)


def kernel(x_prompt, x_sample, state_conv, cache_k, cache_v, norm_attn_g, w_in, q_norm_g, k_norm_g, rel_bias, attn_sinks, w_dw, b_dw, conv_ln_g, conv_ln_b, w_conv_out, b_conv_out, w_attn_out, w_out, norm_ffn_g, w_grp, b_grp, w_router, b_router, w_gate, w_up, w_down):
    bsz, t, _ = x_prompt.shape
    nseq, steps, _ = x_sample.shape
    row = lambda a: a.reshape(1, -1).astype(_F32)

    w_in_b = w_in.astype(_BF16)
    wco_b = w_conv_out.astype(_BF16)
    wa_b = w_attn_out.astype(_BF16)
    wo_b = w_out.astype(_BF16)
    qg = row(jnp.tile(q_norm_g, N_HEADS)) * (HEAD_DIM ** -0.5)
    kg = row(jnp.tile(k_norm_g, N_KV_HEADS))
    w_rt = jnp.zeros((D_MODEL, LANES), _F32).at[:, :N_GROUPS].set(w_grp).at[:, N_GROUPS:N_GROUPS + N_EXPERTS].set(w_router)
    wr_hi = w_rt.astype(_BF16)
    wr_lo = (w_rt - wr_hi.astype(_F32)).astype(_BF16)
    b_rt = jnp.zeros((1, LANES), _F32).at[0, :N_GROUPS].set(b_grp).at[0, N_GROUPS:N_GROUPS + N_EXPERTS].set(b_router)
    tbl = _bias_tables(rel_bias)
    conv_params = (w_dw, row(b_dw), row(conv_ln_g), row(conv_ln_b), wco_b, row(b_conv_out))

    def finish_and_moe(x2d, conv_out, o, ga, gb, tm):
        h, hn, route, counts = _finish(x2d, conv_out, o, ga, gb, wa_b, wo_b, row(norm_ffn_g), wr_hi, wr_lo, b_rt, tm)
        return _moe(h, hn, route, counts, w_gate, w_up, w_down, tm)

    xp = x_prompt.reshape(bsz * t, D_MODEL)
    glu, q, k, v, ga, gb = _inproj(xp, row(norm_attn_g), w_in_b, qg, kg, _BF16, ROW_TILE)
    conv_out = _conv_prompt(glu, bsz, t, *conv_params, ROW_TILE)
    o = _attn_prompt(q, k, v, tbl, attn_sinks, bsz, t)
    y_prompt = finish_and_moe(xp, conv_out, o, ga, gb, ROW_TILE).reshape(bsz, t, D_MODEL)
    glu3 = glu.reshape(bsz, t, D_CONV)
    state_conv_prompt = glu3[:, t - (CONV_WIDTH - 1):]
    tail = lambda a: a.reshape(bsz, t, KV_DIM)[:, t - WINDOW:].reshape(bsz, WINDOW, N_KV_HEADS, HEAD_DIM)
    cache_k_prompt, cache_v_prompt = tail(k), tail(v)

    xs = x_sample.reshape(nseq * steps, D_MODEL)
    glu, q, k, v, ga, gb = _inproj(xs, row(norm_attn_g), w_in_b, qg, kg, _F32, ROW_TILE)
    glu_t = glu.reshape(nseq, steps, D_CONV).transpose(1, 0, 2)
    conv_out, state_t = _conv_sample(state_conv.transpose(1, 0, 2), glu_t, *conv_params, 64)
    conv_out = conv_out.transpose(1, 0, 2).reshape(nseq * steps, D_MODEL)
    k3 = k.reshape(nseq, steps, KV_DIM)
    v3 = v.reshape(nseq, steps, KV_DIM)
    o, ck_t, cv_t = _attn_sample(q.reshape(nseq, steps, Q_DIM), k3, v3, cache_k.transpose(0, 2, 3, 1),
                                 cache_v.transpose(0, 2, 3, 1), tbl, attn_sinks, 8)
    y_sample = finish_and_moe(xs, conv_out, o.reshape(nseq * steps, Q_DIM), ga, gb, ROW_TILE).reshape(nseq, steps, D_MODEL)
    state_conv_sample = state_t.transpose(1, 0, 2)
    cache_k_sample = ck_t.transpose(0, 3, 1, 2)
    cache_v_sample = cv_t.transpose(0, 3, 1, 2)

    return (y_prompt, y_sample, state_conv_prompt, cache_k_prompt, cache_v_prompt,
            state_conv_sample, cache_k_sample, cache_v_sample)
```

```python
import functools
import math

import numpy as np
import jax
import jax.numpy as jnp
from jax import lax
from jax.experimental import pallas as pl
from jax.experimental.pallas import tpu as pltpu
from jax.experimental.pallas import tpu_sc as plsc

D_MODEL = 1024
N_HEADS = 16
HEAD_DIM = 64
N_KV_HEADS = 4
WINDOW = 128
Q_DIM = N_HEADS * HEAD_DIM
KV_DIM = N_KV_HEADS * HEAD_DIM
N_BUCKETS = 32
MAX_EXACT = N_BUCKETS // 2
MAX_DISTANCE = 128
D_CONV = D_MODEL
CONV_WIDTH = 31
N_GROUPS = 4
EXPERTS_PER_GROUP = 8
N_EXPERTS = N_GROUPS * EXPERTS_PER_GROUP
TOP_K = 2
D_EXPERT = 256
EPS = 1e-6

LANES = 128
SUBLANES = 8
N_PAIRS = N_HEADS // 2
MOE_ROWS = 512
MASK_VALUE = -1e30
VMEM_LIMIT = 56 * 1024 * 1024
ROW_TILE = 512

_F32 = jnp.float32
_BF16 = jnp.bfloat16


def _resident(a):
    return pl.BlockSpec(a.shape, lambda *_: (0,) * a.ndim, pipeline_mode=pl.Buffered(1))


def _dot(a, b):
    return jnp.dot(a, b, preferred_element_type=_F32)


def _split_bf16(x):
    hi = x.astype(_BF16)
    lo = (x - hi.astype(_F32)).astype(_BF16)
    return hi, lo


HALF = D_MODEL // 2


def _pack_bf16_pairs(x):
    lo = pltpu.bitcast(x[:, :HALF].astype(_BF16).astype(_F32), jnp.uint32)
    hi = pltpu.bitcast(x[:, HALF:].astype(_BF16).astype(_F32), jnp.uint32)
    return hi | (lo >> 16)


def _unpack_bf16_pairs(w):
    lo = pltpu.bitcast(w << 16, _F32)
    hi = pltpu.bitcast(w & jnp.uint32(0xFFFF0000), _F32)
    return jnp.concatenate([lo, hi], axis=1)


def _head_rms_scale(z):
    low = lax.broadcasted_iota(jnp.int32, (z.shape[0], LANES), 1) < HEAD_DIM
    slabs = []
    for c in range(z.shape[1] // LANES):
        sq = z[:, c * LANES:(c + 1) * LANES]
        sq = sq * sq
        first = jnp.sum(jnp.where(low, sq, 0.0), axis=-1, keepdims=True)
        second = jnp.sum(jnp.where(low, 0.0, sq), axis=-1, keepdims=True)
        slabs.append(lax.rsqrt(jnp.where(low, first, second) * (1.0 / HEAD_DIM) + EPS))
    return jnp.concatenate(slabs, axis=1)


def _inproj_kernel(x_ref, g_ref, w_ref, qg_ref, kg_ref,
                   glu_ref, q_ref, k_ref, v_ref, ga_ref, gb_ref):
    x = x_ref[...]
    xn = x * lax.rsqrt(jnp.mean(x * x, axis=-1, keepdims=True) + EPS) * g_ref[...]
    xb = xn.astype(_BF16)

    def seg(lo, width):
        return _dot(xb, w_ref[:, lo:lo + width])

    a = seg(0, D_CONV)
    b = seg(D_CONV, D_CONV)
    glu_ref[...] = a * jax.nn.sigmoid(b)
    off = 2 * D_CONV
    q = seg(off, Q_DIM)
    q_ref[...] = (q * _head_rms_scale(q) * qg_ref[...]).astype(q_ref.dtype)
    off += Q_DIM
    k = seg(off, KV_DIM)
    k_ref[...] = k * _head_rms_scale(k) * kg_ref[...]
    off += KV_DIM
    v_ref[...] = seg(off, KV_DIM)
    off += KV_DIM
    ga_ref[...] = jax.nn.sigmoid(seg(off, D_MODEL)).astype(ga_ref.dtype)
    off += D_MODEL
    gb_ref[...] = jax.nn.sigmoid(seg(off, D_MODEL)).astype(gb_ref.dtype)


def _inproj(x, g, w_in_b, qg, kg, q_dtype, tm):
    n = x.shape[0]
    in_dim = w_in_b.shape[1]
    row = lambda w: pl.BlockSpec((tm, w), lambda i: (i, 0))
    full = _resident
    return pl.pallas_call(
        _inproj_kernel,
        grid=(n // tm,),
        in_specs=[row(D_MODEL), full(g), full(w_in_b), full(qg), full(kg)],
        out_specs=[row(D_CONV), row(Q_DIM), row(KV_DIM), row(KV_DIM), row(D_MODEL), row(D_MODEL)],
        out_shape=[jax.ShapeDtypeStruct((n, D_CONV), _F32),
                   jax.ShapeDtypeStruct((n, Q_DIM), q_dtype),
                   jax.ShapeDtypeStruct((n, KV_DIM), _F32),
                   jax.ShapeDtypeStruct((n, KV_DIM), _F32),
                   jax.ShapeDtypeStruct((n, D_MODEL), _BF16),
                   jax.ShapeDtypeStruct((n, D_MODEL), _BF16)],
        compiler_params=pltpu.CompilerParams(dimension_semantics=("arbitrary",),
                                             vmem_limit_bytes=VMEM_LIMIT),
        name="inproj",
    )(x, g, w_in_b, qg, kg)


def _ln_swish_project(y, lng_ref, lnb_ref, wo_ref, bo_ref):
    mu = jnp.mean(y, axis=-1, keepdims=True)
    yc = y - mu
    var = jnp.mean(yc * yc, axis=-1, keepdims=True)
    z = yc * lax.rsqrt(var + EPS) * lng_ref[...] + lnb_ref[...]
    z = z * jax.nn.sigmoid(z)
    return (_dot(z.astype(_BF16), wo_ref[...]) + bo_ref[...]).astype(_BF16)


HALO = 32
CONV_STEPS = 16
CH_TILES = D_CONV // LANES


def _conv_prompt_kernel(glu_ref, w8_ref, b8_ref, lng_ref, lnb_ref, wo_ref, bo_ref,
                        out_ref, hist_ref, y_ref, *, tm):
    i = pl.program_id(1)

    @pl.when(i == 0)
    def _():
        hist_ref[0:HALO * CH_TILES, :] = jnp.zeros((HALO * CH_TILES, LANES), _F32)

    @pl.when(i > 0)
    def _():
        hist_ref[0:HALO * CH_TILES, :] = hist_ref[tm * CH_TILES:(tm + HALO) * CH_TILES, :]

    for c in range(CH_TILES):
        hist_ref[pl.ds(HALO * CH_TILES + c, tm, stride=CH_TILES), :] = glu_ref[:, c * LANES:(c + 1) * LANES]

    first = HALO - (CONV_WIDTH - 1)
    span = CONV_STEPS + CONV_WIDTH - 1

    def chunk(ci, carry):
        t0 = ci * CONV_STEPS
        x = hist_ref[pl.ds(pl.multiple_of((t0 + first) * CH_TILES, CH_TILES), span * CH_TILES), :]
        x = x.reshape(span, CH_TILES, LANES)
        acc = jnp.broadcast_to(b8_ref[...][None], (CONV_STEPS, CH_TILES, LANES))
        for j in range(CONV_WIDTH):
            acc = acc + x[j:j + CONV_STEPS] * w8_ref[j][None]
        y_ref[pl.ds(pl.multiple_of(t0 * CH_TILES, CH_TILES), CONV_STEPS * CH_TILES), :] = (
            acc.reshape(CONV_STEPS * CH_TILES, LANES))
        return carry

    lax.fori_loop(0, tm // CONV_STEPS, chunk, 0)
    y = jnp.concatenate([y_ref[pl.ds(c, tm, stride=CH_TILES), :] for c in range(CH_TILES)], axis=1)
    out_ref[...] = _ln_swish_project(y, lng_ref, lnb_ref, wo_ref, bo_ref)


def _conv_prompt(glu, bsz, t, w_dw, b_dw, lng, lnb, wo_b, bo, tm):
    assert CH_TILES == SUBLANES
    nt = t // tm
    w8 = w_dw.reshape(CONV_WIDTH, CH_TILES, LANES)
    b8 = b_dw.reshape(CH_TILES, LANES)
    full = _resident
    row = pl.BlockSpec((tm, D_CONV), lambda b, i: (b * nt + i, 0))
    return pl.pallas_call(
        functools.partial(_conv_prompt_kernel, tm=tm),
        grid=(bsz, nt),
        in_specs=[row, full(w8), full(b8), full(lng), full(lnb), full(wo_b), full(bo)],
        out_specs=pl.BlockSpec((tm, D_MODEL), lambda b, i: (b * nt + i, 0)),
        out_shape=jax.ShapeDtypeStruct((bsz * t, D_MODEL), _BF16),
        scratch_shapes=[pltpu.VMEM(((tm + HALO) * CH_TILES, LANES), _F32), pltpu.VMEM((tm * CH_TILES, LANES), _F32)],
        compiler_params=pltpu.CompilerParams(dimension_semantics=("arbitrary", "arbitrary"),
                                             vmem_limit_bytes=VMEM_LIMIT),
        name="conv_prompt",
    )(glu, w8, b8, lng, lnb, wo_b, bo)


def _conv_sample_kernel(state_ref, glu_ref, wdw_ref, bdw_ref, lng_ref, lnb_ref, wo_ref, bo_ref,
                        out_ref, state_out_ref):
    keep, steps = state_ref.shape[0], glu_ref.shape[0]

    def hist(u):
        return state_ref[u] if u < keep else glu_ref[u - keep]

    for t in range(steps):
        acc = hist(t) * wdw_ref[0:1, :]
        for j in range(1, CONV_WIDTH):
            acc = acc + hist(t + j) * wdw_ref[j:j + 1, :]
        out_ref[t] = _ln_swish_project(acc + bdw_ref[...], lng_ref, lnb_ref, wo_ref, bo_ref)
    state_out_ref[0:keep - steps] = state_ref[steps:keep]
    state_out_ref[keep - steps:keep] = glu_ref[...]


def _conv_sample(state_t, glu_t, w_dw, b_dw, lng, lnb, wo_b, bo, sb):
    keep, nseq, _ = state_t.shape
    steps = glu_t.shape[0]
    full = _resident
    blk = lambda r: pl.BlockSpec((r, sb, D_CONV), lambda i: (0, i, 0))
    return pl.pallas_call(
        _conv_sample_kernel,
        grid=(nseq // sb,),
        in_specs=[blk(keep), blk(steps), full(w_dw), full(b_dw), full(lng), full(lnb), full(wo_b), full(bo)],
        out_specs=[blk(steps), blk(keep)],
        out_shape=[jax.ShapeDtypeStruct((steps, nseq, D_MODEL), _BF16),
                   jax.ShapeDtypeStruct(state_t.shape, _F32)],
        compiler_params=pltpu.CompilerParams(dimension_semantics=("arbitrary",),
                                             vmem_limit_bytes=VMEM_LIMIT),
        name="conv_sample",
    )(state_t, glu_t, w_dw, b_dw, lng, lnb, wo_b, bo)


def _bucket_map():
    i = np.arange(WINDOW)[:, None]
    j = np.arange(WINDOW)[None, :]
    n = (i - j) % WINDOW
    nf = np.maximum(n, 1).astype(np.float32)
    large = MAX_EXACT + (np.log(nf / np.float32(MAX_EXACT)) / np.float32(math.log(MAX_DISTANCE / MAX_EXACT))
                         * np.float32(N_BUCKETS - MAX_EXACT)).astype(np.int32)
    return np.where(n < MAX_EXACT, n, np.minimum(large, N_BUCKETS - 1)).astype(np.int32)


def _bias_table_kernel(rb_ref, bm_ref, tbl_ref):
    p = pl.program_id(0)
    bm = bm_ref[...]
    for half in range(2):
        h = 2 * p + half
        t = jnp.zeros(bm.shape, _F32)
        for b in range(N_BUCKETS):
            t = jnp.where(bm == b, rb_ref[b, h], t)
        tbl_ref[0, :, half * WINDOW:(half + 1) * WINDOW] = t


def _bias_tables(rel_bias):
    bm = jnp.asarray(_bucket_map())
    return pl.pallas_call(
        _bias_table_kernel,
        grid=(N_PAIRS,),
        in_specs=[pl.BlockSpec(memory_space=pltpu.SMEM), pl.BlockSpec(bm.shape, lambda p: (0, 0))],
        out_specs=pl.BlockSpec((1, WINDOW, 2 * WINDOW), lambda p: (p, 0, 0)),
        out_shape=jax.ShapeDtypeStruct((N_PAIRS, WINDOW, 2 * WINDOW), _F32),
        name="bias_tables",
    )(rel_bias, bm)


def _block_diag_pairs(slab):
    low = lax.broadcasted_iota(jnp.int32, slab.shape, 1) < HEAD_DIM
    swapped = pltpu.roll(slab, HEAD_DIM, axis=1)
    zero = jnp.zeros_like(slab)
    first = jnp.concatenate([jnp.where(low, slab, zero), jnp.where(low, zero, swapped)], axis=0)
    second = jnp.concatenate([jnp.where(low, swapped, zero), jnp.where(low, zero, slab)], axis=0)
    return first.astype(_BF16), second.astype(_BF16)


def _kv_operands(k_blk, v_blk):
    ops = []
    for slab in range(KV_DIM // LANES):
        cols = slice(slab * LANES, (slab + 1) * LANES)
        ops.extend(zip(_block_diag_pairs(k_blk[:, cols]), _block_diag_pairs(v_blk[:, cols])))
    return ops


def _attend(q, prev_ops, own_ops, tbl_ref, sink_ref, prev_shift, store, transposed=False):
    tq = q.shape[0]
    rows = 2 * tq
    row = lax.broadcasted_iota(jnp.int32, (rows, 2 * WINDOW), 0)
    col = lax.broadcasted_iota(jnp.int32, (rows, 2 * WINDOW), 1)
    from_prev = (col & (WINDOW - 1)) > jnp.where(row >= tq, row - tq, row)
    top = lax.broadcasted_iota(jnp.int32, (rows, 1), 0) < tq
    low = lax.broadcasted_iota(jnp.int32, (rows, LANES), 1) < HEAD_DIM
    contract_last = (((1,), (1,)), ((), ()))

    def logits(a, k_op):
        return _dot(a, k_op) if transposed else lax.dot_general(a, k_op, contract_last, preferred_element_type=_F32)

    def weighted_values(pr, v_op):
        return lax.dot_general(pr, v_op, contract_last, preferred_element_type=_F32) if transposed else _dot(pr, v_op)

    for kvh in range(N_KV_HEADS):
        (k_prev, v_prev), (k_own, v_own) = prev_ops[kvh], own_ops[kvh]
        pair_a = 2 * kvh
        pair_b = pair_a + 1
        qq = jnp.concatenate([q[:, pair_a * LANES:(pair_a + 1) * LANES],
                              q[:, pair_b * LANES:(pair_b + 1) * LANES]], axis=0).astype(_BF16)
        sp = logits(qq, k_prev)
        so = logits(qq, k_own)
        bias = jnp.concatenate([tbl_ref[pair_a, 0:tq, :], tbl_ref[pair_b, 0:tq, :]], axis=0)
        s = jnp.where(from_prev, sp + prev_shift, so) + bias
        sink_even = jnp.where(top, sink_ref[2 * pair_a], sink_ref[2 * pair_b])
        sink_odd = jnp.where(top, sink_ref[2 * pair_a + 1], sink_ref[2 * pair_b + 1])
        m_even = jnp.maximum(jnp.max(s[:, :WINDOW], axis=-1, keepdims=True), sink_even)
        m_odd = jnp.maximum(jnp.max(s[:, WINDOW:], axis=-1, keepdims=True), sink_odd)
        p = jnp.exp(s - jnp.where(col < WINDOW, m_even, m_odd)).astype(_BF16)
        zero = jnp.zeros_like(p)
        o = (weighted_values(jnp.where(from_prev, p, zero), v_prev)
             + weighted_values(jnp.where(from_prev, zero, p), v_own))
        pf = p.astype(_F32)
        sums = jnp.where(low, jnp.sum(pf[:, :WINDOW], axis=-1, keepdims=True),
                         jnp.sum(pf[:, WINDOW:], axis=-1, keepdims=True))
        den = sums + jnp.where(low, jnp.exp(sink_even - m_even), jnp.exp(sink_odd - m_odd))
        o = o / den
        store(pair_a, o[:tq])
        store(pair_b, o[tq:])


PROMPT_QBLOCKS = 4


def _attn_prompt_kernel(sink_ref, q_ref, kp_ref, ko_ref, vp_ref, vo_ref, tbl_ref, o_ref):
    prev_shift = jnp.where(pl.program_id(1) == 0, MASK_VALUE, 0.0).astype(_F32)
    ops = [_kv_operands(kp_ref[...], vp_ref[...])]
    for b in range(PROMPT_QBLOCKS):
        rows = slice(b * WINDOW, (b + 1) * WINDOW)
        ops.append(_kv_operands(ko_ref[rows, :], vo_ref[rows, :]))

        def store(pair, o, rows=rows):
            o_ref[rows, pair * LANES:(pair + 1) * LANES] = o.astype(o_ref.dtype)

        _attend(q_ref[rows, :], ops[b], ops[b + 1], tbl_ref, sink_ref,
                prev_shift if b == 0 else jnp.float32(0.0), store)


def _attn_prompt(q, k, v, tbl, sinks, bsz, t):
    tq = PROMPT_QBLOCKS * WINDOW
    nb = t // tq
    own = lambda w: pl.BlockSpec((tq, w), lambda b, i: (b * nb + i, 0))
    prev = lambda w: pl.BlockSpec((WINDOW, w),
                                  lambda b, i: (PROMPT_QBLOCKS * (b * nb + i) - jnp.minimum(i, 1), 0))
    return pl.pallas_call(
        _attn_prompt_kernel,
        grid=(bsz, nb),
        in_specs=[pl.BlockSpec(memory_space=pltpu.SMEM), own(Q_DIM), prev(KV_DIM), own(KV_DIM),
                  prev(KV_DIM), own(KV_DIM), pl.BlockSpec(tbl.shape, lambda b, i: (0, 0, 0))],
        out_specs=own(Q_DIM),
        out_shape=jax.ShapeDtypeStruct((bsz * t, Q_DIM), _BF16),
        compiler_params=pltpu.CompilerParams(dimension_semantics=("arbitrary", "arbitrary"),
                                             vmem_limit_bytes=VMEM_LIMIT),
        name="attn_prompt",
    )(sinks, q, k, k, v, v, tbl)


SAMPLE_UNROLL = 2


def _block_diag_t(x):
    xb = x.astype(_BF16)
    z = jnp.zeros_like(xb)
    return jnp.concatenate([jnp.concatenate([xb, z], axis=1), jnp.concatenate([z, xb], axis=1)], axis=0)


def _attn_sample_kernel(sink_ref, q_ref, kn_ref, vn_ref, ck_ref, cv_ref, tbl_ref, o_ref, cko_ref, cvo_ref,
                        *, sb, steps):
    pad = jnp.zeros((WINDOW - steps, LANES), _F32)
    lane = lax.broadcasted_iota(jnp.int32, (HEAD_DIM, WINDOW), 1)

    def one_sequence(s, carry):
        def store(pair, o):
            o_ref[s, :, pair * LANES:(pair + 1) * LANES] = o

        prev_ops, own_ops = [], []
        for slab in range(KV_DIM // LANES):
            cols = slice(slab * LANES, (slab + 1) * LANES)
            new_k = jnp.concatenate([kn_ref[s][:, cols], pad], axis=0).T
            new_v = jnp.concatenate([vn_ref[s][:, cols], pad], axis=0).T
            for sub in range(2):
                kvh = 2 * slab + sub
                part = slice(sub * HEAD_DIM, (sub + 1) * HEAD_DIM)
                kt, vt = ck_ref[s, kvh], cv_ref[s, kvh]
                cko_ref[s, kvh] = pltpu.roll(jnp.where(lane < steps, new_k[part], kt), WINDOW - steps, axis=1)
                cvo_ref[s, kvh] = pltpu.roll(jnp.where(lane < steps, new_v[part], vt), WINDOW - steps, axis=1)
                prev_ops.append((_block_diag_t(kt), _block_diag_t(vt)))
                own_ops.append((_block_diag_t(new_k[part]), _block_diag_t(new_v[part])))
        _attend(q_ref[s], prev_ops, own_ops, tbl_ref, sink_ref, jnp.float32(0.0), store, transposed=True)
        return carry

    lax.fori_loop(0, sb, one_sequence, 0, unroll=SAMPLE_UNROLL)


def _attn_sample(q, k_new, v_new, cache_kt, cache_vt, tbl, sinks, sb):
    nseq, steps, _ = q.shape
    seq = lambda r, w: pl.BlockSpec((sb, r, w), lambda i: (i, 0, 0))
    cache = pl.BlockSpec((sb, N_KV_HEADS, HEAD_DIM, WINDOW), lambda i: (i, 0, 0, 0))
    return pl.pallas_call(
        functools.partial(_attn_sample_kernel, sb=sb, steps=steps),
        grid=(nseq // sb,),
        in_specs=[pl.BlockSpec(memory_space=pltpu.SMEM), seq(steps, Q_DIM), seq(steps, KV_DIM), seq(steps, KV_DIM),
                  cache, cache, pl.BlockSpec(tbl.shape, lambda i: (0, 0, 0))],
        out_specs=[seq(steps, Q_DIM), cache, cache],
        out_shape=[jax.ShapeDtypeStruct((nseq, steps, Q_DIM), _F32),
                   jax.ShapeDtypeStruct(cache_kt.shape, _F32), jax.ShapeDtypeStruct(cache_vt.shape, _F32)],
        compiler_params=pltpu.CompilerParams(dimension_semantics=("arbitrary",),
                                             vmem_limit_bytes=VMEM_LIMIT),
        name="attn_sample",
    )(sinks, q, k_new, v_new, cache_kt, cache_vt, tbl)


def _lane_min_index(mask, lane):
    return jnp.min(jnp.where(mask, lane, LANES), axis=-1, keepdims=True)


def _finish_kernel(x_ref, conv_ref, o_ref, ga_ref, gb_ref, wa_ref, wo_ref, ng_ref, wr_hi_ref, wr_lo_ref, br_ref,
                   tri_ref, h_ref, hn_ref, route_ref, count_ref, running_ref):
    @pl.when(pl.program_id(0) == 0)
    def _():
        running_ref[...] = jnp.zeros_like(running_ref)

    attn_out = _dot(o_ref[...].astype(_BF16), wa_ref[...])
    merged = ga_ref[...].astype(_F32) * conv_ref[...].astype(_F32) + gb_ref[...].astype(_F32) * attn_out
    h = x_ref[...] + _dot(merged.astype(_BF16), wo_ref[...])
    h_ref[...] = h
    hn = h * lax.rsqrt(jnp.mean(h * h, axis=-1, keepdims=True) + EPS) * ng_ref[...]
    hn_ref[...] = _pack_bf16_pairs(hn)

    hi, lo = _split_bf16(hn)
    logits = _dot(hi, wr_hi_ref[...]) + _dot(lo, wr_hi_ref[...]) + _dot(hi, wr_lo_ref[...]) + br_ref[...]
    lane = lax.broadcasted_iota(jnp.int32, logits.shape, 1)
    gmask = lane < N_GROUPS
    gl = jnp.where(gmask, logits, MASK_VALUE)
    gmax = jnp.max(gl, axis=-1, keepdims=True)
    grp = _lane_min_index(gmask & (gl == gmax), lane)
    p_grp = 1.0 / jnp.sum(jnp.where(gmask, jnp.exp(gl - gmax), 0.0), axis=-1, keepdims=True)
    e_lo = N_GROUPS + grp * EXPERTS_PER_GROUP
    emask = (lane >= e_lo) & (lane < e_lo + EXPERTS_PER_GROUP)
    el = jnp.where(emask, logits, MASK_VALUE)
    ex = jnp.where(emask, jnp.exp(el - jnp.max(el, axis=-1, keepdims=True)), 0.0)
    prob = jnp.where(emask, ex / jnp.sum(ex, axis=-1, keepdims=True), -1.0)
    p1 = jnp.max(prob, axis=-1, keepdims=True)
    i1 = _lane_min_index(prob == p1, lane)
    rest = jnp.where(lane == i1, -1.0, prob)
    p2 = jnp.max(rest, axis=-1, keepdims=True)
    i2 = _lane_min_index(rest == p2, lane)
    w1 = p_grp * p1 / (p1 + p2)
    w2 = p_grp * p2 / (p1 + p2)
    e1 = i1 - N_GROUPS
    e2 = i2 - N_GROUPS

    hot1 = lane == e1
    hot2 = lane == e2
    hot = jnp.where(hot1 | hot2, 1.0, 0.0)
    before = _dot(tri_ref[...], hot.astype(_BF16)) + running_ref[...]
    rank1 = jnp.sum(jnp.where(hot1, before, 0.0), axis=-1, keepdims=True)
    rank2 = jnp.sum(jnp.where(hot2, before, 0.0), axis=-1, keepdims=True)
    running_ref[...] += jnp.sum(hot, axis=0, keepdims=True)
    count_ref[...] = jnp.broadcast_to(running_ref[...], count_ref.shape)

    fields = (e1.astype(_F32), e2.astype(_F32), w1, w2, rank1, rank2)
    route = jnp.zeros(logits.shape, _F32)
    for pos, val in enumerate(fields):
        route = jnp.where(lane == pos, val, route)
    route_ref[...] = route


ROUTE_E, ROUTE_W, ROUTE_RANK = 0, 2, 4
DEST_ROWS = 1024


def _finish(x, conv_out, o, ga, gb, wa_b, wo_b, ng, wr_hi, wr_lo, br, tm):
    n = x.shape[0]
    tri = jnp.asarray(np.tril(np.ones((tm, tm), np.float32), -1), _BF16)
    row = lambda w: pl.BlockSpec((tm, w), lambda i: (i, 0))
    full = _resident
    return pl.pallas_call(
        _finish_kernel,
        grid=(n // tm,),
        in_specs=[row(D_MODEL), row(D_MODEL), row(Q_DIM), row(D_MODEL), row(D_MODEL),
                  full(wa_b), full(wo_b), full(ng), full(wr_hi), full(wr_lo), full(br), full(tri)],
        out_specs=[row(D_MODEL), row(HALF), row(LANES), pl.BlockSpec((SUBLANES, LANES), lambda i: (0, 0))],
        out_shape=[jax.ShapeDtypeStruct((n, D_MODEL), _F32),
                   jax.ShapeDtypeStruct((n, HALF), jnp.uint32),
                   jax.ShapeDtypeStruct((n, LANES), _F32),
                   jax.ShapeDtypeStruct((SUBLANES, LANES), _F32)],
        scratch_shapes=[pltpu.VMEM((1, LANES), _F32)],
        compiler_params=pltpu.CompilerParams(dimension_semantics=("arbitrary",),
                                             vmem_limit_bytes=VMEM_LIMIT),
        name="finish",
    )(x, conv_out, o, ga, gb, wa_b, wo_b, ng, wr_hi, wr_lo, br, tri)


def _dest_kernel(route_ref, starts_ref, dest_ref):
    route = route_ref[...]
    lane = lax.broadcasted_iota(jnp.int32, route.shape, 1)
    out = jnp.zeros(route.shape, jnp.int32)
    for j in range(TOP_K):
        e = route[:, ROUTE_E + j:ROUTE_E + j + 1].astype(jnp.int32)
        start = jnp.sum(jnp.where(lane == e, starts_ref[...], 0.0), axis=-1, keepdims=True)
        d = (start + route[:, ROUTE_RANK + j:ROUTE_RANK + j + 1]).astype(jnp.int32)
        out = jnp.where(lane == j, d, out)
    dest_ref[...] = out


def _dest(route, starts_row, tm):
    n = route.shape[0]
    row = pl.BlockSpec((tm, LANES), lambda i: (i, 0))
    return pl.pallas_call(
        _dest_kernel,
        grid=(n // tm,),
        in_specs=[row, pl.BlockSpec((1, LANES), lambda i: (0, 0))],
        out_specs=row,
        out_shape=jax.ShapeDtypeStruct((n, LANES), jnp.int32),
        name="dest",
    )(route, starts_row)


SC_CORES = 2
SC_SUBCORES = 16
SC_WORKERS = SC_CORES * SC_SUBCORES
SC_IN_FLIGHT = 4
SC_CHUNK_BYTES = 64 * 1024


def _sc_move_rows(src, idx, gather):
    n, d = src.shape
    b = idx.shape[0]
    per_worker = b // SC_WORKERS
    assert per_worker * SC_WORKERS == b and (gather or n % per_worker == 0), (b, n)
    chunk = min(per_worker // SC_IN_FLIGHT, SC_CHUNK_BYTES // (d * 4))
    n_iters = per_worker // (chunk * SC_IN_FLIGHT)
    assert n_iters * chunk * SC_IN_FLIGHT == per_worker and chunk % SUBLANES == 0, (per_worker, chunk)
    mesh = plsc.VectorSubcoreMesh(core_axis_name="c", subcore_axis_name="s",
                                  num_cores=SC_CORES, num_subcores=SC_SUBCORES)
    scratch = ([pltpu.VMEM((chunk,), jnp.int32)] * SC_IN_FLIGHT + [pltpu.VMEM((chunk, d), src.dtype)] * SC_IN_FLIGHT
               + [pltpu.SemaphoreType.DMA] * SC_IN_FLIGHT)

    @functools.partial(pl.kernel, mesh=mesh, out_type=jax.ShapeDtypeStruct((b, d), src.dtype),
                       scratch_types=scratch, name="sc_gather_rows" if gather else "sc_scatter_rows")
    def move(src_hbm, idx_hbm, out_hbm, *bufs):
        idx_v = bufs[:SC_IN_FLIGHT]
        rows_v = bufs[SC_IN_FLIGHT:2 * SC_IN_FLIGHT]
        sems = bufs[2 * SC_IN_FLIGHT:]
        worker = lax.axis_index("s") * SC_CORES + lax.axis_index("c")

        @pl.loop(0, n_iters)
        def _(it):
            bases = [pl.multiple_of(worker * per_worker + (it * SC_IN_FLIGHT + j) * chunk, chunk)
                     for j in range(SC_IN_FLIGHT)]
            loads = [pltpu.async_copy(idx_hbm.at[pl.ds(bases[j], chunk)], idx_v[j], sems[j])
                     for j in range(SC_IN_FLIGHT)]
            reads = []
            for j in range(SC_IN_FLIGHT):
                loads[j].wait()
                if gather:
                    rows = src_hbm.at[idx_v[j]]
                else:
                    rows = src_hbm.at[pl.ds(pl.multiple_of(lax.rem(bases[j], n), chunk), chunk)]
                reads.append(pltpu.async_copy(rows, rows_v[j], sems[j]))
            writes = []
            for j in range(SC_IN_FLIGHT):
                reads[j].wait()
                dst = out_hbm.at[pl.ds(bases[j], chunk)] if gather else out_hbm.at[idx_v[j]]
                writes.append(pltpu.async_copy(rows_v[j], dst, sems[j]))
            for w in writes:
                w.wait()

    return move(src, idx)


def _sc_gather_rows(table, idx):
    return _sc_move_rows(table, idx, gather=True)


def _sc_scatter_rows(src, idx):
    return _sc_move_rows(src, idx, gather=False)


def _expert_kernel(blk_ref, exp_ref, lo_ref, hi_ref, x_ref, wg_ref, wu_ref, wd_ref, yb_ref,
                   wg_b, wu_b, wd_b, held_ref):
    del blk_ref
    k = pl.program_id(0)
    lo, hi, e = lo_ref[k], hi_ref[k], exp_ref[k]

    @pl.when(k == 0)
    def _():
        held_ref[0] = -1

    @pl.when(hi > lo)
    def _():
        @pl.when(held_ref[0] != e)
        def _():
            wg_b[...] = wg_ref[0].astype(_BF16)
            wu_b[...] = wu_ref[0].astype(_BF16)
            wd_b[...] = wd_ref[0].astype(_BF16)
            held_ref[0] = e

        xb = _unpack_bf16_pairs(x_ref[...]).astype(_BF16)
        g = _dot(xb, wg_b[...])
        u = _dot(xb, wu_b[...])
        hid = g * jax.nn.sigmoid(g) * u
        y = _dot(hid.astype(_BF16), wd_b[...])
        r = lax.broadcasted_iota(jnp.int32, yb_ref.shape, 0)
        pltpu.store(yb_ref, _pack_bf16_pairs(y), mask=(r >= lo) & (r < hi))


def _experts(items, xs, w_gate, w_up, w_down):
    n_items = items[0].shape[0]
    wspec = lambda a: pl.BlockSpec((1,) + a.shape[1:], lambda k, blk, exp, lo, hi: (exp[k], 0, 0))
    rows = pl.BlockSpec((MOE_ROWS, HALF), lambda k, blk, exp, lo, hi: (blk[k], 0))
    grid_spec = pltpu.PrefetchScalarGridSpec(
        num_scalar_prefetch=4,
        grid=(n_items,),
        in_specs=[rows, wspec(w_gate), wspec(w_up), wspec(w_down)],
        out_specs=rows,
        scratch_shapes=[pltpu.VMEM(w_gate.shape[1:], _BF16), pltpu.VMEM(w_up.shape[1:], _BF16),
                        pltpu.VMEM(w_down.shape[1:], _BF16), pltpu.SMEM((1,), jnp.int32)],
    )
    return pl.pallas_call(
        _expert_kernel,
        grid_spec=grid_spec,
        out_shape=jax.ShapeDtypeStruct(xs.shape, xs.dtype),
        compiler_params=pltpu.CompilerParams(dimension_semantics=("arbitrary",),
                                             vmem_limit_bytes=VMEM_LIMIT),
        name="experts",
    )(*items, xs, w_gate, w_up, w_down)


def _combine_kernel(h_ref, route_ref, g0_ref, g1_ref, y_ref):
    route = route_ref[...]
    y_ref[...] = (h_ref[...] + route[:, ROUTE_W:ROUTE_W + 1] * _unpack_bf16_pairs(g0_ref[...])
                  + route[:, ROUTE_W + 1:ROUTE_W + 2] * _unpack_bf16_pairs(g1_ref[...]))


def _combine(h, route, g, tm):
    n = h.shape[0]
    nt = n // tm
    row = lambda w: pl.BlockSpec((tm, w), lambda i: (i, 0))
    return pl.pallas_call(
        _combine_kernel,
        grid=(nt,),
        in_specs=[row(D_MODEL), row(LANES), row(HALF), pl.BlockSpec((tm, HALF), lambda i: (nt + i, 0))],
        out_specs=row(D_MODEL),
        out_shape=jax.ShapeDtypeStruct((n, D_MODEL), _F32),
        compiler_params=pltpu.CompilerParams(dimension_semantics=("arbitrary",),
                                             vmem_limit_bytes=VMEM_LIMIT),
        name="combine",
    )(h, route, g, g)


def _work_items(counts, n_pairs):
    n_blocks = n_pairs // MOE_ROWS
    starts = jnp.cumsum(counts) - counts
    cuts = jnp.sort(jnp.concatenate([jnp.arange(n_blocks, dtype=jnp.int32) * MOE_ROWS, starts]))
    ends = jnp.concatenate([cuts[1:], jnp.full((1,), n_pairs, jnp.int32)])
    blk = jnp.minimum(cuts // MOE_ROWS, n_blocks - 1)
    expert = jnp.clip(jnp.sum(starts[None, :] <= cuts[:, None], axis=1) - 1, 0, N_EXPERTS - 1).astype(jnp.int32)
    return starts, (blk, expert, cuts - blk * MOE_ROWS, ends - blk * MOE_ROWS)


def _moe(h, hn, route, counts_rows, wg_b, wu_b, wd_b, tm):
    n = h.shape[0]
    n_pairs = n * TOP_K
    counts = counts_rows[0, :N_EXPERTS].astype(jnp.int32)
    starts, items = _work_items(counts, n_pairs)
    starts_row = jnp.zeros((1, LANES), _F32).at[0, :N_EXPERTS].set(starts.astype(_F32))
    dest = _dest(route, starts_row, min(n, DEST_ROWS))[:, :TOP_K].T.reshape(n_pairs)
    xs = _sc_scatter_rows(hn, dest)
    yb = _experts(items, xs, wg_b, wu_b, wd_b)
    g = _sc_gather_rows(yb, dest)
    return _combine(h, route, g, tm)


def kernel(x_prompt, x_sample, state_conv, cache_k, cache_v, norm_attn_g, w_in, q_norm_g, k_norm_g, rel_bias, attn_sinks, w_dw, b_dw, conv_ln_g, conv_ln_b, w_conv_out, b_conv_out, w_attn_out, w_out, norm_ffn_g, w_grp, b_grp, w_router, b_router, w_gate, w_up, w_down):
    bsz, t, _ = x_prompt.shape
    nseq, steps, _ = x_sample.shape
    row = lambda a: a.reshape(1, -1).astype(_F32)

    w_in_b = w_in.astype(_BF16)
    wco_b = w_conv_out.astype(_BF16)
    wa_b = w_attn_out.astype(_BF16)
    wo_b = w_out.astype(_BF16)
    qg = row(jnp.tile(q_norm_g, N_HEADS)) * (HEAD_DIM ** -0.5)
    kg = row(jnp.tile(k_norm_g, N_KV_HEADS))
    w_rt = jnp.zeros((D_MODEL, LANES), _F32).at[:, :N_GROUPS].set(w_grp).at[:, N_GROUPS:N_GROUPS + N_EXPERTS].set(w_router)
    wr_hi = w_rt.astype(_BF16)
    wr_lo = (w_rt - wr_hi.astype(_F32)).astype(_BF16)
    b_rt = jnp.zeros((1, LANES), _F32).at[0, :N_GROUPS].set(b_grp).at[0, N_GROUPS:N_GROUPS + N_EXPERTS].set(b_router)
    tbl = _bias_tables(rel_bias)
    conv_params = (w_dw, row(b_dw), row(conv_ln_g), row(conv_ln_b), wco_b, row(b_conv_out))

    def finish_and_moe(x2d, conv_out, o, ga, gb, tm):
        h, hn, route, counts = _finish(x2d, conv_out, o, ga, gb, wa_b, wo_b, row(norm_ffn_g), wr_hi, wr_lo, b_rt, tm)
        return _moe(h, hn, route, counts, w_gate, w_up, w_down, tm)

    xp = x_prompt.reshape(bsz * t, D_MODEL)
    glu, q, k, v, ga, gb = _inproj(xp, row(norm_attn_g), w_in_b, qg, kg, _BF16, ROW_TILE)
    conv_out = _conv_prompt(glu, bsz, t, *conv_params, ROW_TILE)
    o = _attn_prompt(q, k, v, tbl, attn_sinks, bsz, t)
    y_prompt = finish_and_moe(xp, conv_out, o, ga, gb, ROW_TILE).reshape(bsz, t, D_MODEL)
    glu3 = glu.reshape(bsz, t, D_CONV)
    state_conv_prompt = glu3[:, t - (CONV_WIDTH - 1):]
    tail = lambda a: a.reshape(bsz, t, KV_DIM)[:, t - WINDOW:].reshape(bsz, WINDOW, N_KV_HEADS, HEAD_DIM)
    cache_k_prompt, cache_v_prompt = tail(k), tail(v)

    xs = x_sample.reshape(nseq * steps, D_MODEL)
    glu, q, k, v, ga, gb = _inproj(xs, row(norm_attn_g), w_in_b, qg, kg, _F32, ROW_TILE)
    glu_t = glu.reshape(nseq, steps, D_CONV).transpose(1, 0, 2)
    conv_out, state_t = _conv_sample(state_conv.transpose(1, 0, 2), glu_t, *conv_params, 64)
    conv_out = conv_out.transpose(1, 0, 2).reshape(nseq * steps, D_MODEL)
    k3 = k.reshape(nseq, steps, KV_DIM)
    v3 = v.reshape(nseq, steps, KV_DIM)
    o, ck_t, cv_t = _attn_sample(q.reshape(nseq, steps, Q_DIM), k3, v3, cache_k.transpose(0, 2, 3, 1),
                                 cache_v.transpose(0, 2, 3, 1), tbl, attn_sinks, 8)
    y_sample = finish_and_moe(xs, conv_out, o.reshape(nseq * steps, Q_DIM), ga, gb, ROW_TILE).reshape(nseq, steps, D_MODEL)
    state_conv_sample = state_t.transpose(1, 0, 2)
    cache_k_sample = ck_t.transpose(0, 3, 1, 2)
    cache_v_sample = cv_t.transpose(0, 3, 1, 2)

    return (y_prompt, y_sample, state_conv_prompt, cache_k_prompt, cache_v_prompt,
            state_conv_sample, cache_k_sample, cache_v_sample)
```

```python
import functools
import math

import numpy as np
import jax
import jax.numpy as jnp
from jax import lax
from jax.experimental import pallas as pl
from jax.experimental.pallas import tpu as pltpu
from jax.experimental.pallas import tpu_sc as plsc

D_MODEL = 1024
N_HEADS = 16
HEAD_DIM = 64
N_KV_HEADS = 4
WINDOW = 128
Q_DIM = N_HEADS * HEAD_DIM
KV_DIM = N_KV_HEADS * HEAD_DIM
N_BUCKETS = 32
MAX_EXACT = N_BUCKETS // 2
MAX_DISTANCE = 128
D_CONV = D_MODEL
CONV_WIDTH = 31
N_GROUPS = 4
EXPERTS_PER_GROUP = 8
N_EXPERTS = N_GROUPS * EXPERTS_PER_GROUP
TOP_K = 2
D_EXPERT = 256
EPS = 1e-6

LANES = 128
SUBLANES = 8
N_PAIRS = N_HEADS // 2
MOE_ROWS = 512
MASK_VALUE = -1e30
VMEM_LIMIT = 56 * 1024 * 1024
ROW_TILE = 512

_F32 = jnp.float32
_BF16 = jnp.bfloat16


def _resident(a):
    return pl.BlockSpec(a.shape, lambda *_: (0,) * a.ndim, pipeline_mode=pl.Buffered(1))


def _dot(a, b):
    return jnp.dot(a, b, preferred_element_type=_F32)


def _split_bf16(x):
    hi = x.astype(_BF16)
    lo = (x - hi.astype(_F32)).astype(_BF16)
    return hi, lo


HALF = D_MODEL // 2


def _pack_bf16_pairs(x):
    lo = pltpu.bitcast(x[:, :HALF].astype(_BF16).astype(_F32), jnp.uint32)
    hi = pltpu.bitcast(x[:, HALF:].astype(_BF16).astype(_F32), jnp.uint32)
    return hi | (lo >> 16)


def _unpack_bf16_pairs(w):
    lo = pltpu.bitcast(w << 16, _F32)
    hi = pltpu.bitcast(w & jnp.uint32(0xFFFF0000), _F32)
    return jnp.concatenate([lo, hi], axis=1)


def _head_rms_scale(z):
    low = lax.broadcasted_iota(jnp.int32, (z.shape[0], LANES), 1) < HEAD_DIM
    slabs = []
    for c in range(z.shape[1] // LANES):
        sq = z[:, c * LANES:(c + 1) * LANES]
        sq = sq * sq
        first = jnp.sum(jnp.where(low, sq, 0.0), axis=-1, keepdims=True)
        second = jnp.sum(jnp.where(low, 0.0, sq), axis=-1, keepdims=True)
        slabs.append(lax.rsqrt(jnp.where(low, first, second) * (1.0 / HEAD_DIM) + EPS))
    return jnp.concatenate(slabs, axis=1)


def _inproj_kernel(x_ref, g_ref, w_ref, qg_ref, kg_ref,
                   glu_ref, q_ref, k_ref, v_ref, ga_ref, gb_ref):
    x = x_ref[...]
    xn = x * lax.rsqrt(jnp.mean(x * x, axis=-1, keepdims=True) + EPS) * g_ref[...]
    xb = xn.astype(_BF16)

    def seg(lo, width):
        return _dot(xb, w_ref[:, lo:lo + width])

    a = seg(0, D_CONV)
    b = seg(D_CONV, D_CONV)
    glu_ref[...] = a * jax.nn.sigmoid(b)
    off = 2 * D_CONV
    q = seg(off, Q_DIM)
    q_ref[...] = (q * _head_rms_scale(q) * qg_ref[...]).astype(q_ref.dtype)
    off += Q_DIM
    k = seg(off, KV_DIM)
    k_ref[...] = k * _head_rms_scale(k) * kg_ref[...]
    off += KV_DIM
    v_ref[...] = seg(off, KV_DIM)
    off += KV_DIM
    ga_ref[...] = jax.nn.sigmoid(seg(off, D_MODEL)).astype(ga_ref.dtype)
    off += D_MODEL
    gb_ref[...] = jax.nn.sigmoid(seg(off, D_MODEL)).astype(gb_ref.dtype)


def _inproj(x, g, w_in_b, qg, kg, q_dtype, tm):
    n = x.shape[0]
    in_dim = w_in_b.shape[1]
    row = lambda w: pl.BlockSpec((tm, w), lambda i: (i, 0))
    full = _resident
    return pl.pallas_call(
        _inproj_kernel,
        grid=(n // tm,),
        in_specs=[row(D_MODEL), full(g), full(w_in_b), full(qg), full(kg)],
        out_specs=[row(D_CONV), row(Q_DIM), row(KV_DIM), row(KV_DIM), row(D_MODEL), row(D_MODEL)],
        out_shape=[jax.ShapeDtypeStruct((n, D_CONV), _F32),
                   jax.ShapeDtypeStruct((n, Q_DIM), q_dtype),
                   jax.ShapeDtypeStruct((n, KV_DIM), _F32),
                   jax.ShapeDtypeStruct((n, KV_DIM), _F32),
                   jax.ShapeDtypeStruct((n, D_MODEL), _BF16),
                   jax.ShapeDtypeStruct((n, D_MODEL), _BF16)],
        compiler_params=pltpu.CompilerParams(dimension_semantics=("arbitrary",),
                                             vmem_limit_bytes=VMEM_LIMIT),
        name="inproj",
    )(x, g, w_in_b, qg, kg)


def _ln_swish_project(y, lng_ref, lnb_ref, wo_ref, bo_ref):
    mu = jnp.mean(y, axis=-1, keepdims=True)
    yc = y - mu
    var = jnp.mean(yc * yc, axis=-1, keepdims=True)
    z = yc * lax.rsqrt(var + EPS) * lng_ref[...] + lnb_ref[...]
    z = z * jax.nn.sigmoid(z)
    return (_dot(z.astype(_BF16), wo_ref[...]) + bo_ref[...]).astype(_BF16)


HALO = 32
CONV_STEPS = 16
CH_TILES = D_CONV // LANES


def _conv_prompt_kernel(glu_ref, w8_ref, b8_ref, lng_ref, lnb_ref, wo_ref, bo_ref,
                        out_ref, hist_ref, y_ref, *, tm):
    i = pl.program_id(1)

    @pl.when(i == 0)
    def _():
        hist_ref[0:HALO * CH_TILES, :] = jnp.zeros((HALO * CH_TILES, LANES), _F32)

    @pl.when(i > 0)
    def _():
        hist_ref[0:HALO * CH_TILES, :] = hist_ref[tm * CH_TILES:(tm + HALO) * CH_TILES, :]

    for c in range(CH_TILES):
        hist_ref[pl.ds(HALO * CH_TILES + c, tm, stride=CH_TILES), :] = glu_ref[:, c * LANES:(c + 1) * LANES]

    first = HALO - (CONV_WIDTH - 1)
    span = CONV_STEPS + CONV_WIDTH - 1

    def chunk(ci, carry):
        t0 = ci * CONV_STEPS
        x = hist_ref[pl.ds(pl.multiple_of((t0 + first) * CH_TILES, CH_TILES), span * CH_TILES), :]
        x = x.reshape(span, CH_TILES, LANES)
        acc = jnp.broadcast_to(b8_ref[...][None], (CONV_STEPS, CH_TILES, LANES))
        for j in range(CONV_WIDTH):
            acc = acc + x[j:j + CONV_STEPS] * w8_ref[j][None]
        y_ref[pl.ds(pl.multiple_of(t0 * CH_TILES, CH_TILES), CONV_STEPS * CH_TILES), :] = (
            acc.reshape(CONV_STEPS * CH_TILES, LANES))
        return carry

    lax.fori_loop(0, tm // CONV_STEPS, chunk, 0)
    y = jnp.concatenate([y_ref[pl.ds(c, tm, stride=CH_TILES), :] for c in range(CH_TILES)], axis=1)
    out_ref[...] = _ln_swish_project(y, lng_ref, lnb_ref, wo_ref, bo_ref)


def _conv_prompt(glu, bsz, t, w_dw, b_dw, lng, lnb, wo_b, bo, tm):
    assert CH_TILES == SUBLANES
    nt = t // tm
    w8 = w_dw.reshape(CONV_WIDTH, CH_TILES, LANES)
    b8 = b_dw.reshape(CH_TILES, LANES)
    full = _resident
    row = pl.BlockSpec((tm, D_CONV), lambda b, i: (b * nt + i, 0))
    return pl.pallas_call(
        functools.partial(_conv_prompt_kernel, tm=tm),
        grid=(bsz, nt),
        in_specs=[row, full(w8), full(b8), full(lng), full(lnb), full(wo_b), full(bo)],
        out_specs=pl.BlockSpec((tm, D_MODEL), lambda b, i: (b * nt + i, 0)),
        out_shape=jax.ShapeDtypeStruct((bsz * t, D_MODEL), _BF16),
        scratch_shapes=[pltpu.VMEM(((tm + HALO) * CH_TILES, LANES), _F32), pltpu.VMEM((tm * CH_TILES, LANES), _F32)],
        compiler_params=pltpu.CompilerParams(dimension_semantics=("arbitrary", "arbitrary"),
                                             vmem_limit_bytes=VMEM_LIMIT),
        name="conv_prompt",
    )(glu, w8, b8, lng, lnb, wo_b, bo)


def _conv_sample_kernel(state_ref, glu_ref, wdw_ref, bdw_ref, lng_ref, lnb_ref, wo_ref, bo_ref,
                        out_ref, state_out_ref):
    keep, steps = state_ref.shape[0], glu_ref.shape[0]

    def hist(u):
        return state_ref[u] if u < keep else glu_ref[u - keep]

    for t in range(steps):
        acc = hist(t) * wdw_ref[0:1, :]
        for j in range(1, CONV_WIDTH):
            acc = acc + hist(t + j) * wdw_ref[j:j + 1, :]
        out_ref[t] = _ln_swish_project(acc + bdw_ref[...], lng_ref, lnb_ref, wo_ref, bo_ref)
    state_out_ref[0:keep - steps] = state_ref[steps:keep]
    state_out_ref[keep - steps:keep] = glu_ref[...]


def _conv_sample(state_t, glu_t, w_dw, b_dw, lng, lnb, wo_b, bo, sb):
    keep, nseq, _ = state_t.shape
    steps = glu_t.shape[0]
    full = _resident
    blk = lambda r: pl.BlockSpec((r, sb, D_CONV), lambda i: (0, i, 0))
    return pl.pallas_call(
        _conv_sample_kernel,
        grid=(nseq // sb,),
        in_specs=[blk(keep), blk(steps), full(w_dw), full(b_dw), full(lng), full(lnb), full(wo_b), full(bo)],
        out_specs=[blk(steps), blk(keep)],
        out_shape=[jax.ShapeDtypeStruct((steps, nseq, D_MODEL), _BF16),
                   jax.ShapeDtypeStruct(state_t.shape, _F32)],
        compiler_params=pltpu.CompilerParams(dimension_semantics=("arbitrary",),
                                             vmem_limit_bytes=VMEM_LIMIT),
        name="conv_sample",
    )(state_t, glu_t, w_dw, b_dw, lng, lnb, wo_b, bo)


def _bucket_map():
    i = np.arange(WINDOW)[:, None]
    j = np.arange(WINDOW)[None, :]
    n = (i - j) % WINDOW
    nf = np.maximum(n, 1).astype(np.float32)
    large = MAX_EXACT + (np.log(nf / np.float32(MAX_EXACT)) / np.float32(math.log(MAX_DISTANCE / MAX_EXACT))
                         * np.float32(N_BUCKETS - MAX_EXACT)).astype(np.int32)
    return np.where(n < MAX_EXACT, n, np.minimum(large, N_BUCKETS - 1)).astype(np.int32)


def _bias_table_kernel(rb_ref, bm_ref, tbl_ref):
    p = pl.program_id(0)
    bm = bm_ref[...]
    for half in range(2):
        h = 2 * p + half
        t = jnp.zeros(bm.shape, _F32)
        for b in range(N_BUCKETS):
            t = jnp.where(bm == b, rb_ref[b, h], t)
        tbl_ref[0, :, half * WINDOW:(half + 1) * WINDOW] = t


def _bias_tables(rel_bias):
    bm = jnp.asarray(_bucket_map())
    return pl.pallas_call(
        _bias_table_kernel,
        grid=(N_PAIRS,),
        in_specs=[pl.BlockSpec(memory_space=pltpu.SMEM), pl.BlockSpec(bm.shape, lambda p: (0, 0))],
        out_specs=pl.BlockSpec((1, WINDOW, 2 * WINDOW), lambda p: (p, 0, 0)),
        out_shape=jax.ShapeDtypeStruct((N_PAIRS, WINDOW, 2 * WINDOW), _F32),
        name="bias_tables",
    )(rel_bias, bm)


def _block_diag_pairs(slab):
    low = lax.broadcasted_iota(jnp.int32, slab.shape, 1) < HEAD_DIM
    swapped = pltpu.roll(slab, HEAD_DIM, axis=1)
    zero = jnp.zeros_like(slab)
    first = jnp.concatenate([jnp.where(low, slab, zero), jnp.where(low, zero, swapped)], axis=0)
    second = jnp.concatenate([jnp.where(low, swapped, zero), jnp.where(low, zero, slab)], axis=0)
    return first.astype(_BF16), second.astype(_BF16)


def _kv_operands(k_blk, v_blk):
    ops = []
    for slab in range(KV_DIM // LANES):
        cols = slice(slab * LANES, (slab + 1) * LANES)
        ops.extend(zip(_block_diag_pairs(k_blk[:, cols]), _block_diag_pairs(v_blk[:, cols])))
    return ops


def _attend(q, prev_ops, own_ops, tbl_ref, sink_ref, prev_shift, store, transposed=False):
    tq = q.shape[0]
    rows = 2 * tq
    row = lax.broadcasted_iota(jnp.int32, (rows, 2 * WINDOW), 0)
    col = lax.broadcasted_iota(jnp.int32, (rows, 2 * WINDOW), 1)
    from_prev = (col & (WINDOW - 1)) > jnp.where(row >= tq, row - tq, row)
    top = lax.broadcasted_iota(jnp.int32, (rows, 1), 0) < tq
    low = lax.broadcasted_iota(jnp.int32, (rows, LANES), 1) < HEAD_DIM
    contract_last = (((1,), (1,)), ((), ()))

    def logits(a, k_op):
        return _dot(a, k_op) if transposed else lax.dot_general(a, k_op, contract_last, preferred_element_type=_F32)

    def weighted_values(pr, v_op):
        return lax.dot_general(pr, v_op, contract_last, preferred_element_type=_F32) if transposed else _dot(pr, v_op)

    for kvh in range(N_KV_HEADS):
        (k_prev, v_prev), (k_own, v_own) = prev_ops[kvh], own_ops[kvh]
        pair_a = 2 * kvh
        pair_b = pair_a + 1
        qq = jnp.concatenate([q[:, pair_a * LANES:(pair_a + 1) * LANES],
                              q[:, pair_b * LANES:(pair_b + 1) * LANES]], axis=0).astype(_BF16)
        sp = logits(qq, k_prev)
        so = logits(qq, k_own)
        bias = jnp.concatenate([tbl_ref[pair_a, 0:tq, :], tbl_ref[pair_b, 0:tq, :]], axis=0)
        s = jnp.where(from_prev, sp + prev_shift, so) + bias
        sink_even = jnp.where(top, sink_ref[2 * pair_a], sink_ref[2 * pair_b])
        sink_odd = jnp.where(top, sink_ref[2 * pair_a + 1], sink_ref[2 * pair_b + 1])
        m_even = jnp.maximum(jnp.max(s[:, :WINDOW], axis=-1, keepdims=True), sink_even)
        m_odd = jnp.maximum(jnp.max(s[:, WINDOW:], axis=-1, keepdims=True), sink_odd)
        p = jnp.exp(s - jnp.where(col < WINDOW, m_even, m_odd)).astype(_BF16)
        zero = jnp.zeros_like(p)
        o = (weighted_values(jnp.where(from_prev, p, zero), v_prev)
             + weighted_values(jnp.where(from_prev, zero, p), v_own))
        pf = p.astype(_F32)
        sums = jnp.where(low, jnp.sum(pf[:, :WINDOW], axis=-1, keepdims=True),
                         jnp.sum(pf[:, WINDOW:], axis=-1, keepdims=True))
        den = sums + jnp.where(low, jnp.exp(sink_even - m_even), jnp.exp(sink_odd - m_odd))
        o = o / den
        store(pair_a, o[:tq])
        store(pair_b, o[tq:])


PROMPT_QBLOCKS = 4


def _attn_prompt_kernel(sink_ref, q_ref, kp_ref, ko_ref, vp_ref, vo_ref, tbl_ref, o_ref):
    prev_shift = jnp.where(pl.program_id(1) == 0, MASK_VALUE, 0.0).astype(_F32)
    ops = [_kv_operands(kp_ref[...], vp_ref[...])]
    for b in range(PROMPT_QBLOCKS):
        rows = slice(b * WINDOW, (b + 1) * WINDOW)
        ops.append(_kv_operands(ko_ref[rows, :], vo_ref[rows, :]))

        def store(pair, o, rows=rows):
            o_ref[rows, pair * LANES:(pair + 1) * LANES] = o.astype(o_ref.dtype)

        _attend(q_ref[rows, :], ops[b], ops[b + 1], tbl_ref, sink_ref,
                prev_shift if b == 0 else jnp.float32(0.0), store)


def _attn_prompt(q, k, v, tbl, sinks, bsz, t):
    tq = PROMPT_QBLOCKS * WINDOW
    nb = t // tq
    own = lambda w: pl.BlockSpec((tq, w), lambda b, i: (b * nb + i, 0))
    prev = lambda w: pl.BlockSpec((WINDOW, w),
                                  lambda b, i: (PROMPT_QBLOCKS * (b * nb + i) - jnp.minimum(i, 1), 0))
    return pl.pallas_call(
        _attn_prompt_kernel,
        grid=(bsz, nb),
        in_specs=[pl.BlockSpec(memory_space=pltpu.SMEM), own(Q_DIM), prev(KV_DIM), own(KV_DIM),
                  prev(KV_DIM), own(KV_DIM), pl.BlockSpec(tbl.shape, lambda b, i: (0, 0, 0))],
        out_specs=own(Q_DIM),
        out_shape=jax.ShapeDtypeStruct((bsz * t, Q_DIM), _BF16),
        compiler_params=pltpu.CompilerParams(dimension_semantics=("arbitrary", "arbitrary"),
                                             vmem_limit_bytes=VMEM_LIMIT),
        name="attn_prompt",
    )(sinks, q, k, k, v, v, tbl)


SAMPLE_UNROLL = 2


def _block_diag_t(x):
    xb = x.astype(_BF16)
    z = jnp.zeros_like(xb)
    return jnp.concatenate([jnp.concatenate([xb, z], axis=1), jnp.concatenate([z, xb], axis=1)], axis=0)


def _attn_sample_kernel(sink_ref, q_ref, kn_ref, vn_ref, ck_ref, cv_ref, tbl_ref, o_ref, cko_ref, cvo_ref,
                        *, sb, steps):
    pad = jnp.zeros((WINDOW - steps, LANES), _F32)
    lane = lax.broadcasted_iota(jnp.int32, (HEAD_DIM, WINDOW), 1)

    def one_sequence(s, carry):
        def store(pair, o):
            o_ref[s, :, pair * LANES:(pair + 1) * LANES] = o

        prev_ops, own_ops = [], []
        for slab in range(KV_DIM // LANES):
            cols = slice(slab * LANES, (slab + 1) * LANES)
            new_k = jnp.concatenate([kn_ref[s][:, cols], pad], axis=0).T
            new_v = jnp.concatenate([vn_ref[s][:, cols], pad], axis=0).T
            for sub in range(2):
                kvh = 2 * slab + sub
                part = slice(sub * HEAD_DIM, (sub + 1) * HEAD_DIM)
                kt, vt = ck_ref[s, kvh], cv_ref[s, kvh]
                cko_ref[s, kvh] = pltpu.roll(jnp.where(lane < steps, new_k[part], kt), WINDOW - steps, axis=1)
                cvo_ref[s, kvh] = pltpu.roll(jnp.where(lane < steps, new_v[part], vt), WINDOW - steps, axis=1)
                prev_ops.append((_block_diag_t(kt), _block_diag_t(vt)))
                own_ops.append((_block_diag_t(new_k[part]), _block_diag_t(new_v[part])))
        _attend(q_ref[s], prev_ops, own_ops, tbl_ref, sink_ref, jnp.float32(0.0), store, transposed=True)
        return carry

    lax.fori_loop(0, sb, one_sequence, 0, unroll=SAMPLE_UNROLL)


def _attn_sample(q, k_new, v_new, cache_kt, cache_vt, tbl, sinks, sb):
    nseq, steps, _ = q.shape
    seq = lambda r, w: pl.BlockSpec((sb, r, w), lambda i: (i, 0, 0))
    cache = pl.BlockSpec((sb, N_KV_HEADS, HEAD_DIM, WINDOW), lambda i: (i, 0, 0, 0))
    return pl.pallas_call(
        functools.partial(_attn_sample_kernel, sb=sb, steps=steps),
        grid=(nseq // sb,),
        in_specs=[pl.BlockSpec(memory_space=pltpu.SMEM), seq(steps, Q_DIM), seq(steps, KV_DIM), seq(steps, KV_DIM),
                  cache, cache, pl.BlockSpec(tbl.shape, lambda i: (0, 0, 0))],
        out_specs=[seq(steps, Q_DIM), cache, cache],
        out_shape=[jax.ShapeDtypeStruct((nseq, steps, Q_DIM), _F32),
                   jax.ShapeDtypeStruct(cache_kt.shape, _F32), jax.ShapeDtypeStruct(cache_vt.shape, _F32)],
        compiler_params=pltpu.CompilerParams(dimension_semantics=("arbitrary",),
                                             vmem_limit_bytes=VMEM_LIMIT),
        name="attn_sample",
    )(sinks, q, k_new, v_new, cache_kt, cache_vt, tbl)


def _lane_min_index(mask, lane):
    return jnp.min(jnp.where(mask, lane, LANES), axis=-1, keepdims=True)


def _finish_kernel(x_ref, conv_ref, o_ref, ga_ref, gb_ref, wa_ref, wo_ref, ng_ref, wr_hi_ref, wr_lo_ref, br_ref,
                   tri_ref, h_ref, hn_ref, route_ref, count_ref, running_ref):
    @pl.when(pl.program_id(0) == 0)
    def _():
        running_ref[...] = jnp.zeros_like(running_ref)

    attn_out = _dot(o_ref[...].astype(_BF16), wa_ref[...])
    merged = ga_ref[...].astype(_F32) * conv_ref[...].astype(_F32) + gb_ref[...].astype(_F32) * attn_out
    h = x_ref[...] + _dot(merged.astype(_BF16), wo_ref[...])
    h_ref[...] = h
    hn = h * lax.rsqrt(jnp.mean(h * h, axis=-1, keepdims=True) + EPS) * ng_ref[...]
    hn_ref[...] = _pack_bf16_pairs(hn)

    hi, lo = _split_bf16(hn)
    logits = _dot(hi, wr_hi_ref[...]) + _dot(lo, wr_hi_ref[...]) + _dot(hi, wr_lo_ref[...]) + br_ref[...]
    lane = lax.broadcasted_iota(jnp.int32, logits.shape, 1)
    gmask = lane < N_GROUPS
    gl = jnp.where(gmask, logits, MASK_VALUE)
    gmax = jnp.max(gl, axis=-1, keepdims=True)
    grp = _lane_min_index(gmask & (gl == gmax), lane)
    p_grp = 1.0 / jnp.sum(jnp.where(gmask, jnp.exp(gl - gmax), 0.0), axis=-1, keepdims=True)
    e_lo = N_GROUPS + grp * EXPERTS_PER_GROUP
    emask = (lane >= e_lo) & (lane < e_lo + EXPERTS_PER_GROUP)
    el = jnp.where(emask, logits, MASK_VALUE)
    ex = jnp.where(emask, jnp.exp(el - jnp.max(el, axis=-1, keepdims=True)), 0.0)
    prob = jnp.where(emask, ex / jnp.sum(ex, axis=-1, keepdims=True), -1.0)
    p1 = jnp.max(prob, axis=-1, keepdims=True)
    i1 = _lane_min_index(prob == p1, lane)
    rest = jnp.where(lane == i1, -1.0, prob)
    p2 = jnp.max(rest, axis=-1, keepdims=True)
    i2 = _lane_min_index(rest == p2, lane)
    w1 = p_grp * p1 / (p1 + p2)
    w2 = p_grp * p2 / (p1 + p2)
    e1 = i1 - N_GROUPS
    e2 = i2 - N_GROUPS

    hot1 = lane == e1
    hot2 = lane == e2
    hot = jnp.where(hot1 | hot2, 1.0, 0.0)
    before = _dot(tri_ref[...], hot.astype(_BF16)) + running_ref[...]
    rank1 = jnp.sum(jnp.where(hot1, before, 0.0), axis=-1, keepdims=True)
    rank2 = jnp.sum(jnp.where(hot2, before, 0.0), axis=-1, keepdims=True)
    running_ref[...] += jnp.sum(hot, axis=0, keepdims=True)
    count_ref[...] = jnp.broadcast_to(running_ref[...], count_ref.shape)

    fields = (e1.astype(_F32), e2.astype(_F32), w1, w2, rank1, rank2)
    route = jnp.zeros(logits.shape, _F32)
    for pos, val in enumerate(fields):
        route = jnp.where(lane == pos, val, route)
    route_ref[...] = route


ROUTE_E, ROUTE_W, ROUTE_RANK = 0, 2, 4
DEST_ROWS = 1024


def _finish(x, conv_out, o, ga, gb, wa_b, wo_b, ng, wr_hi, wr_lo, br, tm):
    n = x.shape[0]
    tri = jnp.asarray(np.tril(np.ones((tm, tm), np.float32), -1), _BF16)
    row = lambda w: pl.BlockSpec((tm, w), lambda i: (i, 0))
    full = _resident
    return pl.pallas_call(
        _finish_kernel,
        grid=(n // tm,),
        in_specs=[row(D_MODEL), row(D_MODEL), row(Q_DIM), row(D_MODEL), row(D_MODEL),
                  full(wa_b), full(wo_b), full(ng), full(wr_hi), full(wr_lo), full(br), full(tri)],
        out_specs=[row(D_MODEL), row(HALF), row(LANES), pl.BlockSpec((SUBLANES, LANES), lambda i: (0, 0))],
        out_shape=[jax.ShapeDtypeStruct((n, D_MODEL), _F32),
                   jax.ShapeDtypeStruct((n, HALF), jnp.uint32),
                   jax.ShapeDtypeStruct((n, LANES), _F32),
                   jax.ShapeDtypeStruct((SUBLANES, LANES), _F32)],
        scratch_shapes=[pltpu.VMEM((1, LANES), _F32)],
        compiler_params=pltpu.CompilerParams(dimension_semantics=("arbitrary",),
                                             vmem_limit_bytes=VMEM_LIMIT),
        name="finish",
    )(x, conv_out, o, ga, gb, wa_b, wo_b, ng, wr_hi, wr_lo, br, tri)


def _dest_kernel(route_ref, starts_ref, dest_ref):
    route = route_ref[...]
    lane = lax.broadcasted_iota(jnp.int32, route.shape, 1)
    out = jnp.zeros(route.shape, jnp.int32)
    for j in range(TOP_K):
        e = route[:, ROUTE_E + j:ROUTE_E + j + 1].astype(jnp.int32)
        start = jnp.sum(jnp.where(lane == e, starts_ref[...], 0.0), axis=-1, keepdims=True)
        d = (start + route[:, ROUTE_RANK + j:ROUTE_RANK + j + 1]).astype(jnp.int32)
        out = jnp.where(lane == j, d, out)
    dest_ref[...] = out


def _dest(route, starts_row, tm):
    n = route.shape[0]
    row = pl.BlockSpec((tm, LANES), lambda i: (i, 0))
    return pl.pallas_call(
        _dest_kernel,
        grid=(n // tm,),
        in_specs=[row, pl.BlockSpec((1, LANES), lambda i: (0, 0))],
        out_specs=row,
        out_shape=jax.ShapeDtypeStruct((n, LANES), jnp.int32),
        name="dest",
    )(route, starts_row)


SC_CORES = 2
SC_SUBCORES = 16
SC_WORKERS = SC_CORES * SC_SUBCORES
SC_IN_FLIGHT = 4
SC_CHUNK_BYTES = 64 * 1024


def _sc_move_rows(src, idx, gather):
    n, d = src.shape
    b = idx.shape[0]
    per_worker = b // SC_WORKERS
    assert per_worker * SC_WORKERS == b and (gather or n % per_worker == 0), (b, n)
    chunk = min(per_worker // SC_IN_FLIGHT, SC_CHUNK_BYTES // (d * 4))
    n_iters = per_worker // (chunk * SC_IN_FLIGHT)
    assert n_iters * chunk * SC_IN_FLIGHT == per_worker and chunk % SUBLANES == 0, (per_worker, chunk)
    mesh = plsc.VectorSubcoreMesh(core_axis_name="c", subcore_axis_name="s",
                                  num_cores=SC_CORES, num_subcores=SC_SUBCORES)
    scratch = ([pltpu.VMEM((chunk,), jnp.int32)] * SC_IN_FLIGHT + [pltpu.VMEM((chunk, d), src.dtype)] * SC_IN_FLIGHT
               + [pltpu.SemaphoreType.DMA] * SC_IN_FLIGHT)

    @functools.partial(pl.kernel, mesh=mesh, out_type=jax.ShapeDtypeStruct((b, d), src.dtype),
                       scratch_types=scratch, name="sc_gather_rows" if gather else "sc_scatter_rows")
    def move(src_hbm, idx_hbm, out_hbm, *bufs):
        idx_v = bufs[:SC_IN_FLIGHT]
        rows_v = bufs[SC_IN_FLIGHT:2 * SC_IN_FLIGHT]
        sems = bufs[2 * SC_IN_FLIGHT:]
        worker = lax.axis_index("s") * SC_CORES + lax.axis_index("c")

        @pl.loop(0, n_iters)
        def _(it):
            bases = [pl.multiple_of(worker * per_worker + (it * SC_IN_FLIGHT + j) * chunk, chunk)
                     for j in range(SC_IN_FLIGHT)]
            loads = [pltpu.async_copy(idx_hbm.at[pl.ds(bases[j], chunk)], idx_v[j], sems[j])
                     for j in range(SC_IN_FLIGHT)]
            reads = []
            for j in range(SC_IN_FLIGHT):
                loads[j].wait()
                if gather:
                    rows = src_hbm.at[idx_v[j]]
                else:
                    rows = src_hbm.at[pl.ds(pl.multiple_of(lax.rem(bases[j], n), chunk), chunk)]
                reads.append(pltpu.async_copy(rows, rows_v[j], sems[j]))
            writes = []
            for j in range(SC_IN_FLIGHT):
                reads[j].wait()
                dst = out_hbm.at[pl.ds(bases[j], chunk)] if gather else out_hbm.at[idx_v[j]]
                writes.append(pltpu.async_copy(rows_v[j], dst, sems[j]))
            for w in writes:
                w.wait()

    return move(src, idx)


def _sc_gather_rows(table, idx):
    return _sc_move_rows(table, idx, gather=True)


def _sc_scatter_rows(src, idx):
    return _sc_move_rows(src, idx, gather=False)


def _expert_kernel(blk_ref, exp_ref, lo_ref, hi_ref, x_ref, wg_ref, wu_ref, wd_ref, yb_ref,
                   wg_b, wu_b, wd_b, held_ref):
    del blk_ref
    k = pl.program_id(0)
    lo, hi, e = lo_ref[k], hi_ref[k], exp_ref[k]

    @pl.when(k == 0)
    def _():
        held_ref[0] = -1

    @pl.when(hi > lo)
    def _():
        @pl.when(held_ref[0] != e)
        def _():
            wg_b[...] = wg_ref[0].astype(_BF16)
            wu_b[...] = wu_ref[0].astype(_BF16)
            wd_b[...] = wd_ref[0].astype(_BF16)
            held_ref[0] = e

        xb = _unpack_bf16_pairs(x_ref[...]).astype(_BF16)
        g = _dot(xb, wg_b[...])
        u = _dot(xb, wu_b[...])
        hid = g * jax.nn.sigmoid(g) * u
        y = _dot(hid.astype(_BF16), wd_b[...])
        r = lax.broadcasted_iota(jnp.int32, yb_ref.shape, 0)
        pltpu.store(yb_ref, _pack_bf16_pairs(y), mask=(r >= lo) & (r < hi))


def _experts(items, xs, w_gate, w_up, w_down):
    n_items = items[0].shape[0]
    wspec = lambda a: pl.BlockSpec((1,) + a.shape[1:], lambda k, blk, exp, lo, hi: (exp[k], 0, 0))
    rows = pl.BlockSpec((MOE_ROWS, HALF), lambda k, blk, exp, lo, hi: (blk[k], 0))
    grid_spec = pltpu.PrefetchScalarGridSpec(
        num_scalar_prefetch=4,
        grid=(n_items,),
        in_specs=[rows, wspec(w_gate), wspec(w_up), wspec(w_down)],
        out_specs=rows,
        scratch_shapes=[pltpu.VMEM(w_gate.shape[1:], _BF16), pltpu.VMEM(w_up.shape[1:], _BF16),
                        pltpu.VMEM(w_down.shape[1:], _BF16), pltpu.SMEM((1,), jnp.int32)],
    )
    return pl.pallas_call(
        _expert_kernel,
        grid_spec=grid_spec,
        out_shape=jax.ShapeDtypeStruct(xs.shape, xs.dtype),
        compiler_params=pltpu.CompilerParams(dimension_semantics=("arbitrary",),
                                             vmem_limit_bytes=VMEM_LIMIT),
        name="experts",
    )(*items, xs, w_gate, w_up, w_down)


def _combine_kernel(h_ref, route_ref, g0_ref, g1_ref, y_ref):
    route = route_ref[...]
    y_ref[...] = (h_ref[...] + route[:, ROUTE_W:ROUTE_W + 1] * _unpack_bf16_pairs(g0_ref[...])
                  + route[:, ROUTE_W + 1:ROUTE_W + 2] * _unpack_bf16_pairs(g1_ref[...]))


def _combine(h, route, g, tm):
    n = h.shape[0]
    nt = n // tm
    row = lambda w: pl.BlockSpec((tm, w), lambda i: (i, 0))
    return pl.pallas_call(
        _combine_kernel,
        grid=(nt,),
        in_specs=[row(D_MODEL), row(LANES), row(HALF), pl.BlockSpec((tm, HALF), lambda i: (nt + i, 0))],
        out_specs=row(D_MODEL),
        out_shape=jax.ShapeDtypeStruct((n, D_MODEL), _F32),
        compiler_params=pltpu.CompilerParams(dimension_semantics=("arbitrary",),
                                             vmem_limit_bytes=VMEM_LIMIT),
        name="combine",
    )(h, route, g, g)


def _work_items(counts, n_pairs):
    n_blocks = n_pairs // MOE_ROWS
    starts = jnp.cumsum(counts) - counts
    cuts = jnp.sort(jnp.concatenate([jnp.arange(n_blocks, dtype=jnp.int32) * MOE_ROWS, starts]))
    ends = jnp.concatenate([cuts[1:], jnp.full((1,), n_pairs, jnp.int32)])
    blk = jnp.minimum(cuts // MOE_ROWS, n_blocks - 1)
    expert = jnp.clip(jnp.sum(starts[None, :] <= cuts[:, None], axis=1) - 1, 0, N_EXPERTS - 1).astype(jnp.int32)
    return starts, (blk, expert, cuts - blk * MOE_ROWS, ends - blk * MOE_ROWS)


def _dispatch(hn, route, counts_rows):
    n = hn.shape[0]
    n_pairs = n * TOP_K
    counts = counts_rows[0, :N_EXPERTS].astype(jnp.int32)
    starts, items = _work_items(counts, n_pairs)
    starts_row = jnp.zeros((1, LANES), _F32).at[0, :N_EXPERTS].set(starts.astype(_F32))
    dest = _dest(route, starts_row, min(n, DEST_ROWS))[:, :TOP_K].T.reshape(n_pairs)
    return _sc_scatter_rows(hn, dest), dest, items


def _after(value, *earlier):
    return lax.optimization_barrier((value,) + earlier)[0]


def kernel(x_prompt, x_sample, state_conv, cache_k, cache_v, norm_attn_g, w_in, q_norm_g, k_norm_g, rel_bias, attn_sinks, w_dw, b_dw, conv_ln_g, conv_ln_b, w_conv_out, b_conv_out, w_attn_out, w_out, norm_ffn_g, w_grp, b_grp, w_router, b_router, w_gate, w_up, w_down):
    bsz, t, _ = x_prompt.shape
    nseq, steps, _ = x_sample.shape
    row = lambda a: a.reshape(1, -1).astype(_F32)

    w_in_b = w_in.astype(_BF16)
    wco_b = w_conv_out.astype(_BF16)
    wa_b = w_attn_out.astype(_BF16)
    wo_b = w_out.astype(_BF16)
    qg = row(jnp.tile(q_norm_g, N_HEADS)) * (HEAD_DIM ** -0.5)
    kg = row(jnp.tile(k_norm_g, N_KV_HEADS))
    w_rt = jnp.zeros((D_MODEL, LANES), _F32).at[:, :N_GROUPS].set(w_grp).at[:, N_GROUPS:N_GROUPS + N_EXPERTS].set(w_router)
    wr_hi = w_rt.astype(_BF16)
    wr_lo = (w_rt - wr_hi.astype(_F32)).astype(_BF16)
    b_rt = jnp.zeros((1, LANES), _F32).at[0, :N_GROUPS].set(b_grp).at[0, N_GROUPS:N_GROUPS + N_EXPERTS].set(b_router)
    tbl = _bias_tables(rel_bias)
    conv_params = (w_dw, row(b_dw), row(conv_ln_g), row(conv_ln_b), wco_b, row(b_conv_out))

    def finish(x2d, conv_out, o, ga, gb):
        return _finish(x2d, conv_out, o, ga, gb, wa_b, wo_b, row(norm_ffn_g), wr_hi, wr_lo, b_rt, ROW_TILE)

    xp = x_prompt.reshape(bsz * t, D_MODEL)
    glu, q, k, v, ga, gb = _inproj(xp, row(norm_attn_g), w_in_b, qg, kg, _BF16, ROW_TILE)
    conv_out = _conv_prompt(glu, bsz, t, *conv_params, ROW_TILE)
    o = _attn_prompt(q, k, v, tbl, attn_sinks, bsz, t)
    h_p, hn_p, route_p, counts_p = finish(xp, conv_out, o, ga, gb)
    rows_p, dest_p, items_p = _dispatch(hn_p, route_p, counts_p)
    glu3 = glu.reshape(bsz, t, D_CONV)
    state_conv_prompt = glu3[:, t - (CONV_WIDTH - 1):]
    tail = lambda a: a.reshape(bsz, t, KV_DIM)[:, t - WINDOW:].reshape(bsz, WINDOW, N_KV_HEADS, HEAD_DIM)
    cache_k_prompt, cache_v_prompt = tail(k), tail(v)

    xs = _after(x_sample, counts_p).reshape(nseq * steps, D_MODEL)
    glu, q, k, v, ga, gb = _inproj(xs, row(norm_attn_g), w_in_b, qg, kg, _F32, ROW_TILE)
    glu_t = glu.reshape(nseq, steps, D_CONV).transpose(1, 0, 2)
    conv_out, state_t = _conv_sample(state_conv.transpose(1, 0, 2), glu_t, *conv_params, 64)
    conv_out = conv_out.transpose(1, 0, 2).reshape(nseq * steps, D_MODEL)
    k3 = k.reshape(nseq, steps, KV_DIM)
    v3 = v.reshape(nseq, steps, KV_DIM)
    o, ck_t, cv_t = _attn_sample(q.reshape(nseq, steps, Q_DIM), k3, v3, cache_k.transpose(0, 2, 3, 1),
                                 cache_v.transpose(0, 2, 3, 1), tbl, attn_sinks, 8)
    h_s, hn_s, route_s, counts_s = finish(xs, conv_out, o.reshape(nseq * steps, Q_DIM), ga, gb)
    rows_s, dest_s, items_s = _dispatch(hn_s, route_s, counts_s)
    state_conv_sample = state_t.transpose(1, 0, 2)
    cache_k_sample = ck_t.transpose(0, 3, 1, 2)
    cache_v_sample = cv_t.transpose(0, 3, 1, 2)

    yb_p = _experts(items_p, rows_p, w_gate, w_up, w_down)
    g_p = _sc_gather_rows(yb_p, dest_p)
    yb_s = _experts(items_s, _after(rows_s, yb_p), w_gate, w_up, w_down)
    g_s = _sc_gather_rows(yb_s, dest_s)
    y_prompt = _combine(h_p, route_p, g_p, ROW_TILE).reshape(bsz, t, D_MODEL)
    y_sample = _combine(h_s, route_s, g_s, ROW_TILE).reshape(nseq, steps, D_MODEL)

    return (y_prompt, y_sample, state_conv_prompt, cache_k_prompt, cache_v_prompt,
            state_conv_sample, cache_k_sample, cache_v_sample)
```

```python
import functools
import math

import numpy as np
import jax
import jax.numpy as jnp
from jax import lax
from jax.experimental import pallas as pl
from jax.experimental.pallas import tpu as pltpu
from jax.experimental.pallas import tpu_sc as plsc

D_MODEL = 1024
N_HEADS = 16
HEAD_DIM = 64
N_KV_HEADS = 4
WINDOW = 128
Q_DIM = N_HEADS * HEAD_DIM
KV_DIM = N_KV_HEADS * HEAD_DIM
N_BUCKETS = 32
MAX_EXACT = N_BUCKETS // 2
MAX_DISTANCE = 128
D_CONV = D_MODEL
CONV_WIDTH = 31
N_GROUPS = 4
EXPERTS_PER_GROUP = 8
N_EXPERTS = N_GROUPS * EXPERTS_PER_GROUP
TOP_K = 2
D_EXPERT = 256
EPS = 1e-6

LANES = 128
SUBLANES = 8
N_PAIRS = N_HEADS // 2
MOE_ROWS = 512
MASK_VALUE = -1e30
VMEM_LIMIT = 56 * 1024 * 1024
ROW_TILE = 512

_F32 = jnp.float32
_BF16 = jnp.bfloat16


def _resident(a):
    return pl.BlockSpec(a.shape, lambda *_: (0,) * a.ndim, pipeline_mode=pl.Buffered(1))


def _dot(a, b):
    return jnp.dot(a, b, preferred_element_type=_F32)


def _split_bf16(x):
    hi = x.astype(_BF16)
    lo = (x - hi.astype(_F32)).astype(_BF16)
    return hi, lo


HALF = D_MODEL // 2


def _pack_bf16_pairs(x):
    lo = pltpu.bitcast(x[:, :HALF].astype(_BF16).astype(_F32), jnp.uint32)
    hi = pltpu.bitcast(x[:, HALF:].astype(_BF16).astype(_F32), jnp.uint32)
    return hi | (lo >> 16)


def _unpack_bf16_pairs(w):
    lo = pltpu.bitcast(w << 16, _F32)
    hi = pltpu.bitcast(w & jnp.uint32(0xFFFF0000), _F32)
    return jnp.concatenate([lo, hi], axis=1)


def _head_rms_scale(z):
    low = lax.broadcasted_iota(jnp.int32, (z.shape[0], LANES), 1) < HEAD_DIM
    slabs = []
    for c in range(z.shape[1] // LANES):
        sq = z[:, c * LANES:(c + 1) * LANES]
        sq = sq * sq
        first = jnp.sum(jnp.where(low, sq, 0.0), axis=-1, keepdims=True)
        second = jnp.sum(jnp.where(low, 0.0, sq), axis=-1, keepdims=True)
        slabs.append(lax.rsqrt(jnp.where(low, first, second) * (1.0 / HEAD_DIM) + EPS))
    return jnp.concatenate(slabs, axis=1)


def _inproj_kernel(x_ref, g_ref, w_ref, qg_ref, kg_ref,
                   glu_ref, q_ref, k_ref, v_ref, ga_ref, gb_ref):
    x = x_ref[...]
    xn = x * lax.rsqrt(jnp.mean(x * x, axis=-1, keepdims=True) + EPS) * g_ref[...]
    xb = xn.astype(_BF16)

    def seg(lo, width):
        return _dot(xb, w_ref[:, lo:lo + width])

    a = seg(0, D_CONV)
    b = seg(D_CONV, D_CONV)
    glu_ref[...] = a * jax.nn.sigmoid(b)
    off = 2 * D_CONV
    q = seg(off, Q_DIM)
    q_ref[...] = (q * _head_rms_scale(q) * qg_ref[...]).astype(q_ref.dtype)
    off += Q_DIM
    k = seg(off, KV_DIM)
    k_ref[...] = k * _head_rms_scale(k) * kg_ref[...]
    off += KV_DIM
    v_ref[...] = seg(off, KV_DIM)
    off += KV_DIM
    ga_ref[...] = jax.nn.sigmoid(seg(off, D_MODEL)).astype(ga_ref.dtype)
    off += D_MODEL
    gb_ref[...] = jax.nn.sigmoid(seg(off, D_MODEL)).astype(gb_ref.dtype)


def _inproj(x, g, w_in_b, qg, kg, q_dtype, tm):
    n = x.shape[0]
    in_dim = w_in_b.shape[1]
    row = lambda w: pl.BlockSpec((tm, w), lambda i: (i, 0))
    full = _resident
    return pl.pallas_call(
        _inproj_kernel,
        grid=(n // tm,),
        in_specs=[row(D_MODEL), full(g), full(w_in_b), full(qg), full(kg)],
        out_specs=[row(D_CONV), row(Q_DIM), row(KV_DIM), row(KV_DIM), row(D_MODEL), row(D_MODEL)],
        out_shape=[jax.ShapeDtypeStruct((n, D_CONV), _F32),
                   jax.ShapeDtypeStruct((n, Q_DIM), q_dtype),
                   jax.ShapeDtypeStruct((n, KV_DIM), _F32),
                   jax.ShapeDtypeStruct((n, KV_DIM), _F32),
                   jax.ShapeDtypeStruct((n, D_MODEL), _BF16),
                   jax.ShapeDtypeStruct((n, D_MODEL), _BF16)],
        compiler_params=pltpu.CompilerParams(dimension_semantics=("arbitrary",),
                                             vmem_limit_bytes=VMEM_LIMIT),
        name="inproj",
    )(x, g, w_in_b, qg, kg)


def _ln_swish_project(y, lng_ref, lnb_ref, wo_ref, bo_ref):
    mu = jnp.mean(y, axis=-1, keepdims=True)
    yc = y - mu
    var = jnp.mean(yc * yc, axis=-1, keepdims=True)
    z = yc * lax.rsqrt(var + EPS) * lng_ref[...] + lnb_ref[...]
    z = z * jax.nn.sigmoid(z)
    return (_dot(z.astype(_BF16), wo_ref[...]) + bo_ref[...]).astype(_BF16)


HALO = 32
CONV_STEPS = 16
CH_TILES = D_CONV // LANES


def _conv_prompt_kernel(glu_ref, w8_ref, b8_ref, lng_ref, lnb_ref, wo_ref, bo_ref,
                        out_ref, hist_ref, y_ref, *, tm):
    i = pl.program_id(1)

    @pl.when(i == 0)
    def _():
        hist_ref[0:HALO * CH_TILES, :] = jnp.zeros((HALO * CH_TILES, LANES), _F32)

    @pl.when(i > 0)
    def _():
        hist_ref[0:HALO * CH_TILES, :] = hist_ref[tm * CH_TILES:(tm + HALO) * CH_TILES, :]

    for c in range(CH_TILES):
        hist_ref[pl.ds(HALO * CH_TILES + c, tm, stride=CH_TILES), :] = glu_ref[:, c * LANES:(c + 1) * LANES]

    first = HALO - (CONV_WIDTH - 1)
    span = CONV_STEPS + CONV_WIDTH - 1

    def chunk(ci, carry):
        t0 = ci * CONV_STEPS
        x = hist_ref[pl.ds(pl.multiple_of((t0 + first) * CH_TILES, CH_TILES), span * CH_TILES), :]
        x = x.reshape(span, CH_TILES, LANES)
        acc = jnp.broadcast_to(b8_ref[...][None], (CONV_STEPS, CH_TILES, LANES))
        for j in range(CONV_WIDTH):
            acc = acc + x[j:j + CONV_STEPS] * w8_ref[j][None]
        y_ref[pl.ds(pl.multiple_of(t0 * CH_TILES, CH_TILES), CONV_STEPS * CH_TILES), :] = (
            acc.reshape(CONV_STEPS * CH_TILES, LANES))
        return carry

    lax.fori_loop(0, tm // CONV_STEPS, chunk, 0)
    y = jnp.concatenate([y_ref[pl.ds(c, tm, stride=CH_TILES), :] for c in range(CH_TILES)], axis=1)
    out_ref[...] = _ln_swish_project(y, lng_ref, lnb_ref, wo_ref, bo_ref)


def _conv_prompt(glu, bsz, t, w_dw, b_dw, lng, lnb, wo_b, bo, tm):
    assert CH_TILES == SUBLANES
    nt = t // tm
    w8 = w_dw.reshape(CONV_WIDTH, CH_TILES, LANES)
    b8 = b_dw.reshape(CH_TILES, LANES)
    full = _resident
    row = pl.BlockSpec((tm, D_CONV), lambda b, i: (b * nt + i, 0))
    return pl.pallas_call(
        functools.partial(_conv_prompt_kernel, tm=tm),
        grid=(bsz, nt),
        in_specs=[row, full(w8), full(b8), full(lng), full(lnb), full(wo_b), full(bo)],
        out_specs=pl.BlockSpec((tm, D_MODEL), lambda b, i: (b * nt + i, 0)),
        out_shape=jax.ShapeDtypeStruct((bsz * t, D_MODEL), _BF16),
        scratch_shapes=[pltpu.VMEM(((tm + HALO) * CH_TILES, LANES), _F32), pltpu.VMEM((tm * CH_TILES, LANES), _F32)],
        compiler_params=pltpu.CompilerParams(dimension_semantics=("arbitrary", "arbitrary"),
                                             vmem_limit_bytes=VMEM_LIMIT),
        name="conv_prompt",
    )(glu, w8, b8, lng, lnb, wo_b, bo)


def _conv_sample_kernel(state_ref, glu_ref, wdw_ref, bdw_ref, lng_ref, lnb_ref, wo_ref, bo_ref,
                        out_ref, state_out_ref):
    keep, steps = state_ref.shape[0], glu_ref.shape[0]

    def hist(u):
        return state_ref[u] if u < keep else glu_ref[u - keep]

    for t in range(steps):
        acc = hist(t) * wdw_ref[0:1, :]
        for j in range(1, CONV_WIDTH):
            acc = acc + hist(t + j) * wdw_ref[j:j + 1, :]
        out_ref[t] = _ln_swish_project(acc + bdw_ref[...], lng_ref, lnb_ref, wo_ref, bo_ref)
    state_out_ref[0:keep - steps] = state_ref[steps:keep]
    state_out_ref[keep - steps:keep] = glu_ref[...]


def _conv_sample(state_t, glu_t, w_dw, b_dw, lng, lnb, wo_b, bo, sb):
    keep, nseq, _ = state_t.shape
    steps = glu_t.shape[0]
    full = _resident
    blk = lambda r: pl.BlockSpec((r, sb, D_CONV), lambda i: (0, i, 0))
    return pl.pallas_call(
        _conv_sample_kernel,
        grid=(nseq // sb,),
        in_specs=[blk(keep), blk(steps), full(w_dw), full(b_dw), full(lng), full(lnb), full(wo_b), full(bo)],
        out_specs=[blk(steps), blk(keep)],
        out_shape=[jax.ShapeDtypeStruct((steps, nseq, D_MODEL), _BF16),
                   jax.ShapeDtypeStruct(state_t.shape, _F32)],
        compiler_params=pltpu.CompilerParams(dimension_semantics=("arbitrary",),
                                             vmem_limit_bytes=VMEM_LIMIT),
        name="conv_sample",
    )(state_t, glu_t, w_dw, b_dw, lng, lnb, wo_b, bo)


def _bucket_map():
    i = np.arange(WINDOW)[:, None]
    j = np.arange(WINDOW)[None, :]
    n = (i - j) % WINDOW
    nf = np.maximum(n, 1).astype(np.float32)
    large = MAX_EXACT + (np.log(nf / np.float32(MAX_EXACT)) / np.float32(math.log(MAX_DISTANCE / MAX_EXACT))
                         * np.float32(N_BUCKETS - MAX_EXACT)).astype(np.int32)
    return np.where(n < MAX_EXACT, n, np.minimum(large, N_BUCKETS - 1)).astype(np.int32)


def _bias_table_kernel(rb_ref, bm_ref, tbl_ref):
    p = pl.program_id(0)
    bm = bm_ref[...]
    for half in range(2):
        h = 2 * p + half
        t = jnp.zeros(bm.shape, _F32)
        for b in range(N_BUCKETS):
            t = jnp.where(bm == b, rb_ref[b, h], t)
        tbl_ref[0, :, half * WINDOW:(half + 1) * WINDOW] = t


def _bias_tables(rel_bias):
    bm = jnp.asarray(_bucket_map())
    return pl.pallas_call(
        _bias_table_kernel,
        grid=(N_PAIRS,),
        in_specs=[pl.BlockSpec(memory_space=pltpu.SMEM), pl.BlockSpec(bm.shape, lambda p: (0, 0))],
        out_specs=pl.BlockSpec((1, WINDOW, 2 * WINDOW), lambda p: (p, 0, 0)),
        out_shape=jax.ShapeDtypeStruct((N_PAIRS, WINDOW, 2 * WINDOW), _F32),
        name="bias_tables",
    )(rel_bias, bm)


def _block_diag_pairs(slab):
    low = lax.broadcasted_iota(jnp.int32, slab.shape, 1) < HEAD_DIM
    swapped = pltpu.roll(slab, HEAD_DIM, axis=1)
    zero = jnp.zeros_like(slab)
    first = jnp.concatenate([jnp.where(low, slab, zero), jnp.where(low, zero, swapped)], axis=0)
    second = jnp.concatenate([jnp.where(low, swapped, zero), jnp.where(low, zero, slab)], axis=0)
    return first.astype(_BF16), second.astype(_BF16)


def _kv_operands(k_blk, v_blk):
    ops = []
    for slab in range(KV_DIM // LANES):
        cols = slice(slab * LANES, (slab + 1) * LANES)
        ops.extend(zip(_block_diag_pairs(k_blk[:, cols]), _block_diag_pairs(v_blk[:, cols])))
    return ops


def _attend(q, prev_ops, own_ops, tbl_ref, sink_ref, prev_shift, store, transposed=False):
    tq = q.shape[0]
    rows = 2 * tq
    row = lax.broadcasted_iota(jnp.int32, (rows, 2 * WINDOW), 0)
    col = lax.broadcasted_iota(jnp.int32, (rows, 2 * WINDOW), 1)
    from_prev = (col & (WINDOW - 1)) > jnp.where(row >= tq, row - tq, row)
    top = lax.broadcasted_iota(jnp.int32, (rows, 1), 0) < tq
    low = lax.broadcasted_iota(jnp.int32, (rows, LANES), 1) < HEAD_DIM
    contract_last = (((1,), (1,)), ((), ()))

    def logits(a, k_op):
        return _dot(a, k_op) if transposed else lax.dot_general(a, k_op, contract_last, preferred_element_type=_F32)

    def weighted_values(pr, v_op):
        return lax.dot_general(pr, v_op, contract_last, preferred_element_type=_F32) if transposed else _dot(pr, v_op)

    for kvh in range(N_KV_HEADS):
        (k_prev, v_prev), (k_own, v_own) = prev_ops[kvh], own_ops[kvh]
        pair_a = 2 * kvh
        pair_b = pair_a + 1
        qq = jnp.concatenate([q[:, pair_a * LANES:(pair_a + 1) * LANES],
                              q[:, pair_b * LANES:(pair_b + 1) * LANES]], axis=0).astype(_BF16)
        sp = logits(qq, k_prev)
        so = logits(qq, k_own)
        bias = jnp.concatenate([tbl_ref[pair_a, 0:tq, :], tbl_ref[pair_b, 0:tq, :]], axis=0)
        s = jnp.where(from_prev, sp + prev_shift, so) + bias
        sink_even = jnp.where(top, sink_ref[2 * pair_a], sink_ref[2 * pair_b])
        sink_odd = jnp.where(top, sink_ref[2 * pair_a + 1], sink_ref[2 * pair_b + 1])
        m_even = jnp.maximum(jnp.max(s[:, :WINDOW], axis=-1, keepdims=True), sink_even)
        m_odd = jnp.maximum(jnp.max(s[:, WINDOW:], axis=-1, keepdims=True), sink_odd)
        p = jnp.exp(s - jnp.where(col < WINDOW, m_even, m_odd)).astype(_BF16)
        zero = jnp.zeros_like(p)
        o = (weighted_values(jnp.where(from_prev, p, zero), v_prev)
             + weighted_values(jnp.where(from_prev, zero, p), v_own))
        pf = p.astype(_F32)
        sums = jnp.where(low, jnp.sum(pf[:, :WINDOW], axis=-1, keepdims=True),
                         jnp.sum(pf[:, WINDOW:], axis=-1, keepdims=True))
        den = sums + jnp.where(low, jnp.exp(sink_even - m_even), jnp.exp(sink_odd - m_odd))
        o = o / den
        store(pair_a, o[:tq])
        store(pair_b, o[tq:])


PROMPT_QBLOCKS = 4


def _attn_prompt_kernel(sink_ref, q_ref, kp_ref, ko_ref, vp_ref, vo_ref, tbl_ref, o_ref):
    prev_shift = jnp.where(pl.program_id(1) == 0, MASK_VALUE, 0.0).astype(_F32)
    ops = [_kv_operands(kp_ref[...], vp_ref[...])]
    for b in range(PROMPT_QBLOCKS):
        rows = slice(b * WINDOW, (b + 1) * WINDOW)
        ops.append(_kv_operands(ko_ref[rows, :], vo_ref[rows, :]))

        def store(pair, o, rows=rows):
            o_ref[rows, pair * LANES:(pair + 1) * LANES] = o.astype(o_ref.dtype)

        _attend(q_ref[rows, :], ops[b], ops[b + 1], tbl_ref, sink_ref,
                prev_shift if b == 0 else jnp.float32(0.0), store)


def _attn_prompt(q, k, v, tbl, sinks, bsz, t):
    tq = PROMPT_QBLOCKS * WINDOW
    nb = t // tq
    own = lambda w: pl.BlockSpec((tq, w), lambda b, i: (b * nb + i, 0))
    prev = lambda w: pl.BlockSpec((WINDOW, w),
                                  lambda b, i: (PROMPT_QBLOCKS * (b * nb + i) - jnp.minimum(i, 1), 0))
    return pl.pallas_call(
        _attn_prompt_kernel,
        grid=(bsz, nb),
        in_specs=[pl.BlockSpec(memory_space=pltpu.SMEM), own(Q_DIM), prev(KV_DIM), own(KV_DIM),
                  prev(KV_DIM), own(KV_DIM), pl.BlockSpec(tbl.shape, lambda b, i: (0, 0, 0))],
        out_specs=own(Q_DIM),
        out_shape=jax.ShapeDtypeStruct((bsz * t, Q_DIM), _BF16),
        compiler_params=pltpu.CompilerParams(dimension_semantics=("arbitrary", "arbitrary"),
                                             vmem_limit_bytes=VMEM_LIMIT),
        name="attn_prompt",
    )(sinks, q, k, k, v, v, tbl)


SAMPLE_UNROLL = 2


def _block_diag_t(x):
    xb = x.astype(_BF16)
    z = jnp.zeros_like(xb)
    return jnp.concatenate([jnp.concatenate([xb, z], axis=1), jnp.concatenate([z, xb], axis=1)], axis=0)


def _attn_sample_kernel(sink_ref, q_ref, kn_ref, vn_ref, ck_ref, cv_ref, tbl_ref, o_ref, cko_ref, cvo_ref,
                        *, sb, steps):
    pad = jnp.zeros((WINDOW - steps, LANES), _F32)
    lane = lax.broadcasted_iota(jnp.int32, (HEAD_DIM, WINDOW), 1)

    def one_sequence(s, carry):
        def store(pair, o):
            o_ref[s, :, pair * LANES:(pair + 1) * LANES] = o

        prev_ops, own_ops = [], []
        for slab in range(KV_DIM // LANES):
            cols = slice(slab * LANES, (slab + 1) * LANES)
            new_k = jnp.concatenate([kn_ref[s][:, cols], pad], axis=0).T
            new_v = jnp.concatenate([vn_ref[s][:, cols], pad], axis=0).T
            for sub in range(2):
                kvh = 2 * slab + sub
                part = slice(sub * HEAD_DIM, (sub + 1) * HEAD_DIM)
                kt, vt = ck_ref[s, kvh], cv_ref[s, kvh]
                cko_ref[s, kvh] = pltpu.roll(jnp.where(lane < steps, new_k[part], kt), WINDOW - steps, axis=1)
                cvo_ref[s, kvh] = pltpu.roll(jnp.where(lane < steps, new_v[part], vt), WINDOW - steps, axis=1)
                prev_ops.append((_block_diag_t(kt), _block_diag_t(vt)))
                own_ops.append((_block_diag_t(new_k[part]), _block_diag_t(new_v[part])))
        _attend(q_ref[s], prev_ops, own_ops, tbl_ref, sink_ref, jnp.float32(0.0), store, transposed=True)
        return carry

    lax.fori_loop(0, sb, one_sequence, 0, unroll=SAMPLE_UNROLL)


def _attn_sample(q, k_new, v_new, cache_kt, cache_vt, tbl, sinks, sb):
    nseq, steps, _ = q.shape
    seq = lambda r, w: pl.BlockSpec((sb, r, w), lambda i: (i, 0, 0))
    cache = pl.BlockSpec((sb, N_KV_HEADS, HEAD_DIM, WINDOW), lambda i: (i, 0, 0, 0))
    return pl.pallas_call(
        functools.partial(_attn_sample_kernel, sb=sb, steps=steps),
        grid=(nseq // sb,),
        in_specs=[pl.BlockSpec(memory_space=pltpu.SMEM), seq(steps, Q_DIM), seq(steps, KV_DIM), seq(steps, KV_DIM),
                  cache, cache, pl.BlockSpec(tbl.shape, lambda i: (0, 0, 0))],
        out_specs=[seq(steps, Q_DIM), cache, cache],
        out_shape=[jax.ShapeDtypeStruct((nseq, steps, Q_DIM), _F32),
                   jax.ShapeDtypeStruct(cache_kt.shape, _F32), jax.ShapeDtypeStruct(cache_vt.shape, _F32)],
        compiler_params=pltpu.CompilerParams(dimension_semantics=("arbitrary",),
                                             vmem_limit_bytes=VMEM_LIMIT),
        name="attn_sample",
    )(sinks, q, k_new, v_new, cache_kt, cache_vt, tbl)


def _lane_min_index(mask, lane):
    return jnp.min(jnp.where(mask, lane, LANES), axis=-1, keepdims=True)


def _finish_kernel(x_ref, conv_ref, o_ref, ga_ref, gb_ref, wa_ref, wo_ref, ng_ref, wr_hi_ref, wr_lo_ref, br_ref,
                   tri_ref, h_ref, hn_ref, route_ref, count_ref, running_ref):
    @pl.when(pl.program_id(0) == 0)
    def _():
        running_ref[...] = jnp.zeros_like(running_ref)

    attn_out = _dot(o_ref[...].astype(_BF16), wa_ref[...])
    merged = ga_ref[...].astype(_F32) * conv_ref[...].astype(_F32) + gb_ref[...].astype(_F32) * attn_out
    h = x_ref[...] + _dot(merged.astype(_BF16), wo_ref[...])
    h_ref[...] = h
    hn = h * lax.rsqrt(jnp.mean(h * h, axis=-1, keepdims=True) + EPS) * ng_ref[...]
    hn_ref[...] = _pack_bf16_pairs(hn)

    hi, lo = _split_bf16(hn)
    logits = _dot(hi, wr_hi_ref[...]) + _dot(lo, wr_hi_ref[...]) + _dot(hi, wr_lo_ref[...]) + br_ref[...]
    lane = lax.broadcasted_iota(jnp.int32, logits.shape, 1)
    gmask = lane < N_GROUPS
    gl = jnp.where(gmask, logits, MASK_VALUE)
    gmax = jnp.max(gl, axis=-1, keepdims=True)
    grp = _lane_min_index(gmask & (gl == gmax), lane)
    p_grp = 1.0 / jnp.sum(jnp.where(gmask, jnp.exp(gl - gmax), 0.0), axis=-1, keepdims=True)
    e_lo = N_GROUPS + grp * EXPERTS_PER_GROUP
    emask = (lane >= e_lo) & (lane < e_lo + EXPERTS_PER_GROUP)
    el = jnp.where(emask, logits, MASK_VALUE)
    ex = jnp.where(emask, jnp.exp(el - jnp.max(el, axis=-1, keepdims=True)), 0.0)
    prob = jnp.where(emask, ex / jnp.sum(ex, axis=-1, keepdims=True), -1.0)
    p1 = jnp.max(prob, axis=-1, keepdims=True)
    i1 = _lane_min_index(prob == p1, lane)
    rest = jnp.where(lane == i1, -1.0, prob)
    p2 = jnp.max(rest, axis=-1, keepdims=True)
    i2 = _lane_min_index(rest == p2, lane)
    w1 = p_grp * p1 / (p1 + p2)
    w2 = p_grp * p2 / (p1 + p2)
    e1 = i1 - N_GROUPS
    e2 = i2 - N_GROUPS

    hot1 = lane == e1
    hot2 = lane == e2
    hot = jnp.where(hot1 | hot2, 1.0, 0.0)
    before = _dot(tri_ref[...], hot.astype(_BF16)) + running_ref[...]
    rank1 = jnp.sum(jnp.where(hot1, before, 0.0), axis=-1, keepdims=True)
    rank2 = jnp.sum(jnp.where(hot2, before, 0.0), axis=-1, keepdims=True)
    running_ref[...] += jnp.sum(hot, axis=0, keepdims=True)
    count_ref[...] = jnp.broadcast_to(running_ref[...], count_ref.shape)

    fields = (e1.astype(_F32), e2.astype(_F32), w1, w2, rank1, rank2)
    route = jnp.zeros(logits.shape, _F32)
    for pos, val in enumerate(fields):
        route = jnp.where(lane == pos, val, route)
    route_ref[...] = route


ROUTE_E, ROUTE_W, ROUTE_RANK = 0, 2, 4
DEST_ROWS = 1024


def _finish(x, conv_out, o, ga, gb, wa_b, wo_b, ng, wr_hi, wr_lo, br, tm):
    n = x.shape[0]
    tri = jnp.asarray(np.tril(np.ones((tm, tm), np.float32), -1), _BF16)
    row = lambda w: pl.BlockSpec((tm, w), lambda i: (i, 0))
    full = _resident
    return pl.pallas_call(
        _finish_kernel,
        grid=(n // tm,),
        in_specs=[row(D_MODEL), row(D_MODEL), row(Q_DIM), row(D_MODEL), row(D_MODEL),
                  full(wa_b), full(wo_b), full(ng), full(wr_hi), full(wr_lo), full(br), full(tri)],
        out_specs=[row(D_MODEL), row(HALF), row(LANES), pl.BlockSpec((SUBLANES, LANES), lambda i: (0, 0))],
        out_shape=[jax.ShapeDtypeStruct((n, D_MODEL), _F32),
                   jax.ShapeDtypeStruct((n, HALF), jnp.uint32),
                   jax.ShapeDtypeStruct((n, LANES), _F32),
                   jax.ShapeDtypeStruct((SUBLANES, LANES), _F32)],
        scratch_shapes=[pltpu.VMEM((1, LANES), _F32)],
        compiler_params=pltpu.CompilerParams(dimension_semantics=("arbitrary",),
                                             vmem_limit_bytes=VMEM_LIMIT),
        name="finish",
    )(x, conv_out, o, ga, gb, wa_b, wo_b, ng, wr_hi, wr_lo, br, tri)


def _dest_kernel(route_ref, starts_ref, dest_ref):
    route = route_ref[...]
    lane = lax.broadcasted_iota(jnp.int32, route.shape, 1)
    out = jnp.zeros(route.shape, jnp.int32)
    for j in range(TOP_K):
        e = route[:, ROUTE_E + j:ROUTE_E + j + 1].astype(jnp.int32)
        start = jnp.sum(jnp.where(lane == e, starts_ref[...], 0.0), axis=-1, keepdims=True)
        d = (start + route[:, ROUTE_RANK + j:ROUTE_RANK + j + 1]).astype(jnp.int32)
        out = jnp.where(lane == j, d, out)
    dest_ref[...] = out


def _dest(route, starts_row, tm):
    n = route.shape[0]
    row = pl.BlockSpec((tm, LANES), lambda i: (i, 0))
    return pl.pallas_call(
        _dest_kernel,
        grid=(n // tm,),
        in_specs=[row, pl.BlockSpec((1, LANES), lambda i: (0, 0))],
        out_specs=row,
        out_shape=jax.ShapeDtypeStruct((n, LANES), jnp.int32),
        name="dest",
    )(route, starts_row)


SC_CORES = 2
SC_SUBCORES = 16
SC_WORKERS = SC_CORES * SC_SUBCORES
SC_IN_FLIGHT = 4
SC_CHUNK_BYTES = 64 * 1024


def _sc_move_rows(src, idx, gather):
    n, d = src.shape
    b = idx.shape[0]
    per_worker = b // SC_WORKERS
    assert per_worker * SC_WORKERS == b and (gather or n % per_worker == 0), (b, n)
    chunk = min(per_worker // SC_IN_FLIGHT, SC_CHUNK_BYTES // (d * 4))
    n_iters = per_worker // (chunk * SC_IN_FLIGHT)
    assert n_iters * chunk * SC_IN_FLIGHT == per_worker and chunk % SUBLANES == 0, (per_worker, chunk)
    mesh = plsc.VectorSubcoreMesh(core_axis_name="c", subcore_axis_name="s",
                                  num_cores=SC_CORES, num_subcores=SC_SUBCORES)
    scratch = ([pltpu.VMEM((chunk,), jnp.int32)] * SC_IN_FLIGHT + [pltpu.VMEM((chunk, d), src.dtype)] * SC_IN_FLIGHT
               + [pltpu.SemaphoreType.DMA] * SC_IN_FLIGHT)

    @functools.partial(pl.kernel, mesh=mesh, out_type=jax.ShapeDtypeStruct((b, d), src.dtype),
                       scratch_types=scratch, name="sc_gather_rows" if gather else "sc_scatter_rows",
                       cost_estimate=pl.CostEstimate(flops=0, transcendentals=0, bytes_accessed=2 * b * d * 4 + b * 4))
    def move(src_hbm, idx_hbm, out_hbm, *bufs):
        idx_v = bufs[:SC_IN_FLIGHT]
        rows_v = bufs[SC_IN_FLIGHT:2 * SC_IN_FLIGHT]
        sems = bufs[2 * SC_IN_FLIGHT:]
        worker = lax.axis_index("s") * SC_CORES + lax.axis_index("c")

        @pl.loop(0, n_iters)
        def _(it):
            bases = [pl.multiple_of(worker * per_worker + (it * SC_IN_FLIGHT + j) * chunk, chunk)
                     for j in range(SC_IN_FLIGHT)]
            loads = [pltpu.async_copy(idx_hbm.at[pl.ds(bases[j], chunk)], idx_v[j], sems[j])
                     for j in range(SC_IN_FLIGHT)]
            reads = []
            for j in range(SC_IN_FLIGHT):
                loads[j].wait()
                if gather:
                    rows = src_hbm.at[idx_v[j]]
                else:
                    rows = src_hbm.at[pl.ds(pl.multiple_of(lax.rem(bases[j], n), chunk), chunk)]
                reads.append(pltpu.async_copy(rows, rows_v[j], sems[j]))
            writes = []
            for j in range(SC_IN_FLIGHT):
                reads[j].wait()
                dst = out_hbm.at[pl.ds(bases[j], chunk)] if gather else out_hbm.at[idx_v[j]]
                writes.append(pltpu.async_copy(rows_v[j], dst, sems[j]))
            for w in writes:
                w.wait()

    return move(src, idx)


def _sc_gather_rows(table, idx):
    return _sc_move_rows(table, idx, gather=True)


def _sc_scatter_rows(src, idx):
    return _sc_move_rows(src, idx, gather=False)


def _expert_kernel(blk_ref, exp_ref, lo_ref, hi_ref, x_ref, wg_ref, wu_ref, wd_ref, yb_ref,
                   wg_b, wu_b, wd_b, held_ref):
    del blk_ref
    k = pl.program_id(0)
    lo, hi, e = lo_ref[k], hi_ref[k], exp_ref[k]

    @pl.when(k == 0)
    def _():
        held_ref[0] = -1

    @pl.when(hi > lo)
    def _():
        @pl.when(held_ref[0] != e)
        def _():
            wg_b[...] = wg_ref[0].astype(_BF16)
            wu_b[...] = wu_ref[0].astype(_BF16)
            wd_b[...] = wd_ref[0].astype(_BF16)
            held_ref[0] = e

        xb = _unpack_bf16_pairs(x_ref[...]).astype(_BF16)
        g = _dot(xb, wg_b[...])
        u = _dot(xb, wu_b[...])
        hid = g * jax.nn.sigmoid(g) * u
        y = _dot(hid.astype(_BF16), wd_b[...])
        r = lax.broadcasted_iota(jnp.int32, yb_ref.shape, 0)
        pltpu.store(yb_ref, _pack_bf16_pairs(y), mask=(r >= lo) & (r < hi))


def _experts(items, xs, w_gate, w_up, w_down):
    n_items = items[0].shape[0]
    wspec = lambda a: pl.BlockSpec((1,) + a.shape[1:], lambda k, blk, exp, lo, hi: (exp[k], 0, 0))
    rows = pl.BlockSpec((MOE_ROWS, HALF), lambda k, blk, exp, lo, hi: (blk[k], 0))
    grid_spec = pltpu.PrefetchScalarGridSpec(
        num_scalar_prefetch=4,
        grid=(n_items,),
        in_specs=[rows, wspec(w_gate), wspec(w_up), wspec(w_down)],
        out_specs=rows,
        scratch_shapes=[pltpu.VMEM(w_gate.shape[1:], _BF16), pltpu.VMEM(w_up.shape[1:], _BF16),
                        pltpu.VMEM(w_down.shape[1:], _BF16), pltpu.SMEM((1,), jnp.int32)],
    )
    return pl.pallas_call(
        _expert_kernel,
        grid_spec=grid_spec,
        out_shape=jax.ShapeDtypeStruct(xs.shape, xs.dtype),
        compiler_params=pltpu.CompilerParams(dimension_semantics=("arbitrary",),
                                             vmem_limit_bytes=VMEM_LIMIT),
        cost_estimate=pl.CostEstimate(
            flops=n_items * MOE_ROWS * 6 * D_MODEL * D_EXPERT, transcendentals=n_items * MOE_ROWS * D_EXPERT,
            bytes_accessed=2 * xs.size * 4 + (w_gate.size + w_up.size + w_down.size) * 4),
        name="experts",
    )(*items, xs, w_gate, w_up, w_down)


def _combine_kernel(h_ref, route_ref, g0_ref, g1_ref, y_ref):
    route = route_ref[...]
    y_ref[...] = (h_ref[...] + route[:, ROUTE_W:ROUTE_W + 1] * _unpack_bf16_pairs(g0_ref[...])
                  + route[:, ROUTE_W + 1:ROUTE_W + 2] * _unpack_bf16_pairs(g1_ref[...]))


def _combine(h, route, g, tm):
    n = h.shape[0]
    nt = n // tm
    row = lambda w: pl.BlockSpec((tm, w), lambda i: (i, 0))
    return pl.pallas_call(
        _combine_kernel,
        grid=(nt,),
        in_specs=[row(D_MODEL), row(LANES), row(HALF), pl.BlockSpec((tm, HALF), lambda i: (nt + i, 0))],
        out_specs=row(D_MODEL),
        out_shape=jax.ShapeDtypeStruct((n, D_MODEL), _F32),
        compiler_params=pltpu.CompilerParams(dimension_semantics=("arbitrary",),
                                             vmem_limit_bytes=VMEM_LIMIT),
        name="combine",
    )(h, route, g, g)


def _work_items(counts, n_pairs):
    n_blocks = n_pairs // MOE_ROWS
    starts = jnp.cumsum(counts) - counts
    cuts = jnp.sort(jnp.concatenate([jnp.arange(n_blocks, dtype=jnp.int32) * MOE_ROWS, starts]))
    ends = jnp.concatenate([cuts[1:], jnp.full((1,), n_pairs, jnp.int32)])
    blk = jnp.minimum(cuts // MOE_ROWS, n_blocks - 1)
    expert = jnp.clip(jnp.sum(starts[None, :] <= cuts[:, None], axis=1) - 1, 0, N_EXPERTS - 1).astype(jnp.int32)
    return starts, (blk, expert, cuts - blk * MOE_ROWS, ends - blk * MOE_ROWS)


def _dispatch(hn, route, counts_rows):
    n = hn.shape[0]
    n_pairs = n * TOP_K
    counts = counts_rows[0, :N_EXPERTS].astype(jnp.int32)
    starts, items = _work_items(counts, n_pairs)
    starts_row = jnp.zeros((1, LANES), _F32).at[0, :N_EXPERTS].set(starts.astype(_F32))
    dest = _dest(route, starts_row, min(n, DEST_ROWS))[:, :TOP_K].T.reshape(n_pairs)
    return _sc_scatter_rows(hn, dest), dest, items


def _after(value, *earlier):
    return lax.optimization_barrier((value,) + earlier)[0]


def kernel(x_prompt, x_sample, state_conv, cache_k, cache_v, norm_attn_g, w_in, q_norm_g, k_norm_g, rel_bias, attn_sinks, w_dw, b_dw, conv_ln_g, conv_ln_b, w_conv_out, b_conv_out, w_attn_out, w_out, norm_ffn_g, w_grp, b_grp, w_router, b_router, w_gate, w_up, w_down):
    bsz, t, _ = x_prompt.shape
    nseq, steps, _ = x_sample.shape
    row = lambda a: a.reshape(1, -1).astype(_F32)

    w_in_b = w_in.astype(_BF16)
    wco_b = w_conv_out.astype(_BF16)
    wa_b = w_attn_out.astype(_BF16)
    wo_b = w_out.astype(_BF16)
    qg = row(jnp.tile(q_norm_g, N_HEADS)) * (HEAD_DIM ** -0.5)
    kg = row(jnp.tile(k_norm_g, N_KV_HEADS))
    w_rt = jnp.zeros((D_MODEL, LANES), _F32).at[:, :N_GROUPS].set(w_grp).at[:, N_GROUPS:N_GROUPS + N_EXPERTS].set(w_router)
    wr_hi = w_rt.astype(_BF16)
    wr_lo = (w_rt - wr_hi.astype(_F32)).astype(_BF16)
    b_rt = jnp.zeros((1, LANES), _F32).at[0, :N_GROUPS].set(b_grp).at[0, N_GROUPS:N_GROUPS + N_EXPERTS].set(b_router)
    tbl = _bias_tables(rel_bias)
    conv_params = (w_dw, row(b_dw), row(conv_ln_g), row(conv_ln_b), wco_b, row(b_conv_out))

    def finish(x2d, conv_out, o, ga, gb):
        return _finish(x2d, conv_out, o, ga, gb, wa_b, wo_b, row(norm_ffn_g), wr_hi, wr_lo, b_rt, ROW_TILE)

    xp = x_prompt.reshape(bsz * t, D_MODEL)
    glu, q, k, v, ga, gb = _inproj(xp, row(norm_attn_g), w_in_b, qg, kg, _BF16, ROW_TILE)
    conv_out = _conv_prompt(glu, bsz, t, *conv_params, ROW_TILE)
    o = _attn_prompt(q, k, v, tbl, attn_sinks, bsz, t)
    h_p, hn_p, route_p, counts_p = finish(xp, conv_out, o, ga, gb)
    rows_p, dest_p, items_p = _dispatch(hn_p, route_p, counts_p)
    glu3 = glu.reshape(bsz, t, D_CONV)
    state_conv_prompt = glu3[:, t - (CONV_WIDTH - 1):]
    tail = lambda a: a.reshape(bsz, t, KV_DIM)[:, t - WINDOW:].reshape(bsz, WINDOW, N_KV_HEADS, HEAD_DIM)
    cache_k_prompt, cache_v_prompt = tail(k), tail(v)

    xs = _after(x_sample, dest_p).reshape(nseq * steps, D_MODEL)
    glu, q, k, v, ga, gb = _inproj(xs, row(norm_attn_g), w_in_b, qg, kg, _F32, ROW_TILE)
    glu_t = glu.reshape(nseq, steps, D_CONV).transpose(1, 0, 2)
    conv_out, state_t = _conv_sample(state_conv.transpose(1, 0, 2), glu_t, *conv_params, 64)
    conv_out = conv_out.transpose(1, 0, 2).reshape(nseq * steps, D_MODEL)
    k3 = k.reshape(nseq, steps, KV_DIM)
    v3 = v.reshape(nseq, steps, KV_DIM)
    o, ck_t, cv_t = _attn_sample(q.reshape(nseq, steps, Q_DIM), k3, v3, cache_k.transpose(0, 2, 3, 1),
                                 cache_v.transpose(0, 2, 3, 1), tbl, attn_sinks, 8)
    h_s, hn_s, route_s, counts_s = finish(xs, conv_out, o.reshape(nseq * steps, Q_DIM), ga, gb)
    rows_s, dest_s, items_s = _dispatch(hn_s, route_s, counts_s)
    state_conv_sample = state_t.transpose(1, 0, 2)
    cache_k_sample = ck_t.transpose(0, 3, 1, 2)
    cache_v_sample = cv_t.transpose(0, 3, 1, 2)

    yb_p = _experts(items_p, rows_p, w_gate, w_up, w_down)
    g_p = _sc_gather_rows(yb_p, dest_p)
    yb_s = _experts(items_s, _after(rows_s, yb_p), w_gate, w_up, w_down)
    g_s = _sc_gather_rows(yb_s, dest_s)
    y_prompt = _combine(h_p, route_p, g_p, ROW_TILE).reshape(bsz, t, D_MODEL)
    y_sample = _combine(h_s, route_s, g_s, ROW_TILE).reshape(nseq, steps, D_MODEL)

    return (y_prompt, y_sample, state_conv_prompt, cache_k_prompt, cache_v_prompt,
            state_conv_sample, cache_k_sample, cache_v_sample)
```

```python
import functools
import math

import numpy as np
import jax
import jax.numpy as jnp
from jax import lax
from jax.experimental import pallas as pl
from jax.experimental.pallas import tpu as pltpu
from jax.experimental.pallas import tpu_sc as plsc

D_MODEL = 1024
N_HEADS = 16
HEAD_DIM = 64
N_KV_HEADS = 4
WINDOW = 128
Q_DIM = N_HEADS * HEAD_DIM
KV_DIM = N_KV_HEADS * HEAD_DIM
N_BUCKETS = 32
MAX_EXACT = N_BUCKETS // 2
MAX_DISTANCE = 128
D_CONV = D_MODEL
CONV_WIDTH = 31
N_GROUPS = 4
EXPERTS_PER_GROUP = 8
N_EXPERTS = N_GROUPS * EXPERTS_PER_GROUP
TOP_K = 2
D_EXPERT = 256
EPS = 1e-6

LANES = 128
SUBLANES = 8
N_PAIRS = N_HEADS // 2
MOE_ROWS = 512
MASK_VALUE = -1e30
VMEM_LIMIT = 56 * 1024 * 1024
ROW_TILE = 512

_F32 = jnp.float32
_BF16 = jnp.bfloat16


def _resident(a):
    return pl.BlockSpec(a.shape, lambda *_: (0,) * a.ndim, pipeline_mode=pl.Buffered(1))


def _dot(a, b):
    return jnp.dot(a, b, preferred_element_type=_F32)


def _split_bf16(x):
    hi = x.astype(_BF16)
    lo = (x - hi.astype(_F32)).astype(_BF16)
    return hi, lo


HALF = D_MODEL // 2


def _pack_bf16_pairs(x):
    lo = pltpu.bitcast(x[:, :HALF].astype(_BF16).astype(_F32), jnp.uint32)
    hi = pltpu.bitcast(x[:, HALF:].astype(_BF16).astype(_F32), jnp.uint32)
    return hi | (lo >> 16)


def _unpack_bf16_pairs(w):
    lo = pltpu.bitcast(w << 16, _F32)
    hi = pltpu.bitcast(w & jnp.uint32(0xFFFF0000), _F32)
    return jnp.concatenate([lo, hi], axis=1)


def _head_rms_scale(z):
    low = lax.broadcasted_iota(jnp.int32, (z.shape[0], LANES), 1) < HEAD_DIM
    slabs = []
    for c in range(z.shape[1] // LANES):
        sq = z[:, c * LANES:(c + 1) * LANES]
        sq = sq * sq
        first = jnp.sum(jnp.where(low, sq, 0.0), axis=-1, keepdims=True)
        second = jnp.sum(jnp.where(low, 0.0, sq), axis=-1, keepdims=True)
        slabs.append(lax.rsqrt(jnp.where(low, first, second) * (1.0 / HEAD_DIM) + EPS))
    return jnp.concatenate(slabs, axis=1)


def _inproj_kernel(x_ref, g_ref, w_ref, qg_ref, kg_ref,
                   glu_ref, q_ref, k_ref, v_ref, ga_ref, gb_ref):
    x = x_ref[...]
    xn = x * lax.rsqrt(jnp.mean(x * x, axis=-1, keepdims=True) + EPS) * g_ref[...]
    xb = xn.astype(_BF16)

    def seg(lo, width):
        return _dot(xb, w_ref[:, lo:lo + width])

    a = seg(0, D_CONV)
    b = seg(D_CONV, D_CONV)
    glu_ref[...] = a * jax.nn.sigmoid(b)
    off = 2 * D_CONV
    q = seg(off, Q_DIM)
    q_ref[...] = (q * _head_rms_scale(q) * qg_ref[...]).astype(q_ref.dtype)
    off += Q_DIM
    k = seg(off, KV_DIM)
    k_ref[...] = k * _head_rms_scale(k) * kg_ref[...]
    off += KV_DIM
    v_ref[...] = seg(off, KV_DIM)
    off += KV_DIM
    ga_ref[...] = jax.nn.sigmoid(seg(off, D_MODEL)).astype(ga_ref.dtype)
    off += D_MODEL
    gb_ref[...] = jax.nn.sigmoid(seg(off, D_MODEL)).astype(gb_ref.dtype)


def _inproj(x, g, w_in_b, qg, kg, q_dtype, tm):
    n = x.shape[0]
    in_dim = w_in_b.shape[1]
    row = lambda w: pl.BlockSpec((tm, w), lambda i: (i, 0))
    full = _resident
    return pl.pallas_call(
        _inproj_kernel,
        grid=(n // tm,),
        in_specs=[row(D_MODEL), full(g), full(w_in_b), full(qg), full(kg)],
        out_specs=[row(D_CONV), row(Q_DIM), row(KV_DIM), row(KV_DIM), row(D_MODEL), row(D_MODEL)],
        out_shape=[jax.ShapeDtypeStruct((n, D_CONV), _F32),
                   jax.ShapeDtypeStruct((n, Q_DIM), q_dtype),
                   jax.ShapeDtypeStruct((n, KV_DIM), _F32),
                   jax.ShapeDtypeStruct((n, KV_DIM), _F32),
                   jax.ShapeDtypeStruct((n, D_MODEL), _BF16),
                   jax.ShapeDtypeStruct((n, D_MODEL), _BF16)],
        compiler_params=pltpu.CompilerParams(dimension_semantics=("arbitrary",),
                                             vmem_limit_bytes=VMEM_LIMIT),
        name="inproj",
    )(x, g, w_in_b, qg, kg)


def _ln_swish_project(y, lng_ref, lnb_ref, wo_ref, bo_ref):
    mu = jnp.mean(y, axis=-1, keepdims=True)
    yc = y - mu
    var = jnp.mean(yc * yc, axis=-1, keepdims=True)
    z = yc * lax.rsqrt(var + EPS) * lng_ref[...] + lnb_ref[...]
    z = z * jax.nn.sigmoid(z)
    return (_dot(z.astype(_BF16), wo_ref[...]) + bo_ref[...]).astype(_BF16)


HALO = 32
CONV_STEPS = 16
CH_TILES = D_CONV // LANES


def _conv_prompt_kernel(glu_ref, w8_ref, b8_ref, lng_ref, lnb_ref, wo_ref, bo_ref,
                        out_ref, hist_ref, y_ref, *, tm):
    i = pl.program_id(1)

    @pl.when(i == 0)
    def _():
        hist_ref[0:HALO * CH_TILES, :] = jnp.zeros((HALO * CH_TILES, LANES), _F32)

    @pl.when(i > 0)
    def _():
        hist_ref[0:HALO * CH_TILES, :] = hist_ref[tm * CH_TILES:(tm + HALO) * CH_TILES, :]

    for c in range(CH_TILES):
        hist_ref[pl.ds(HALO * CH_TILES + c, tm, stride=CH_TILES), :] = glu_ref[:, c * LANES:(c + 1) * LANES]

    first = HALO - (CONV_WIDTH - 1)
    span = CONV_STEPS + CONV_WIDTH - 1

    def chunk(ci, carry):
        t0 = ci * CONV_STEPS
        x = hist_ref[pl.ds(pl.multiple_of((t0 + first) * CH_TILES, CH_TILES), span * CH_TILES), :]
        x = x.reshape(span, CH_TILES, LANES)
        acc = jnp.broadcast_to(b8_ref[...][None], (CONV_STEPS, CH_TILES, LANES))
        for j in range(CONV_WIDTH):
            acc = acc + x[j:j + CONV_STEPS] * w8_ref[j][None]
        y_ref[pl.ds(pl.multiple_of(t0 * CH_TILES, CH_TILES), CONV_STEPS * CH_TILES), :] = (
            acc.reshape(CONV_STEPS * CH_TILES, LANES))
        return carry

    lax.fori_loop(0, tm // CONV_STEPS, chunk, 0)
    y = jnp.concatenate([y_ref[pl.ds(c, tm, stride=CH_TILES), :] for c in range(CH_TILES)], axis=1)
    out_ref[...] = _ln_swish_project(y, lng_ref, lnb_ref, wo_ref, bo_ref)


def _conv_prompt(glu, bsz, t, w_dw, b_dw, lng, lnb, wo_b, bo, tm):
    assert CH_TILES == SUBLANES
    nt = t // tm
    w8 = w_dw.reshape(CONV_WIDTH, CH_TILES, LANES)
    b8 = b_dw.reshape(CH_TILES, LANES)
    full = _resident
    row = pl.BlockSpec((tm, D_CONV), lambda b, i: (b * nt + i, 0))
    return pl.pallas_call(
        functools.partial(_conv_prompt_kernel, tm=tm),
        grid=(bsz, nt),
        in_specs=[row, full(w8), full(b8), full(lng), full(lnb), full(wo_b), full(bo)],
        out_specs=pl.BlockSpec((tm, D_MODEL), lambda b, i: (b * nt + i, 0)),
        out_shape=jax.ShapeDtypeStruct((bsz * t, D_MODEL), _BF16),
        scratch_shapes=[pltpu.VMEM(((tm + HALO) * CH_TILES, LANES), _F32), pltpu.VMEM((tm * CH_TILES, LANES), _F32)],
        compiler_params=pltpu.CompilerParams(dimension_semantics=("arbitrary", "arbitrary"),
                                             vmem_limit_bytes=VMEM_LIMIT),
        name="conv_prompt",
    )(glu, w8, b8, lng, lnb, wo_b, bo)


def _conv_sample_kernel(state_ref, glu_ref, wdw_ref, bdw_ref, lng_ref, lnb_ref, wo_ref, bo_ref,
                        out_ref, state_out_ref):
    keep, steps = state_ref.shape[0], glu_ref.shape[0]

    def hist(u):
        return state_ref[u] if u < keep else glu_ref[u - keep]

    for t in range(steps):
        acc = hist(t) * wdw_ref[0:1, :]
        for j in range(1, CONV_WIDTH):
            acc = acc + hist(t + j) * wdw_ref[j:j + 1, :]
        out_ref[t] = _ln_swish_project(acc + bdw_ref[...], lng_ref, lnb_ref, wo_ref, bo_ref)
    state_out_ref[0:keep - steps] = state_ref[steps:keep]
    state_out_ref[keep - steps:keep] = glu_ref[...]


def _conv_sample(state_t, glu_t, w_dw, b_dw, lng, lnb, wo_b, bo, sb):
    keep, nseq, _ = state_t.shape
    steps = glu_t.shape[0]
    full = _resident
    blk = lambda r: pl.BlockSpec((r, sb, D_CONV), lambda i: (0, i, 0))
    return pl.pallas_call(
        _conv_sample_kernel,
        grid=(nseq // sb,),
        in_specs=[blk(keep), blk(steps), full(w_dw), full(b_dw), full(lng), full(lnb), full(wo_b), full(bo)],
        out_specs=[blk(steps), blk(keep)],
        out_shape=[jax.ShapeDtypeStruct((steps, nseq, D_MODEL), _BF16),
                   jax.ShapeDtypeStruct(state_t.shape, _F32)],
        compiler_params=pltpu.CompilerParams(dimension_semantics=("arbitrary",),
                                             vmem_limit_bytes=VMEM_LIMIT),
        name="conv_sample",
    )(state_t, glu_t, w_dw, b_dw, lng, lnb, wo_b, bo)


def _bucket_map():
    i = np.arange(WINDOW)[:, None]
    j = np.arange(WINDOW)[None, :]
    n = (i - j) % WINDOW
    nf = np.maximum(n, 1).astype(np.float32)
    large = MAX_EXACT + (np.log(nf / np.float32(MAX_EXACT)) / np.float32(math.log(MAX_DISTANCE / MAX_EXACT))
                         * np.float32(N_BUCKETS - MAX_EXACT)).astype(np.int32)
    return np.where(n < MAX_EXACT, n, np.minimum(large, N_BUCKETS - 1)).astype(np.int32)


def _bias_table_kernel(rb_ref, bm_ref, tbl_ref):
    p = pl.program_id(0)
    bm = bm_ref[...]
    for half in range(2):
        h = 2 * p + half
        t = jnp.zeros(bm.shape, _F32)
        for b in range(N_BUCKETS):
            t = jnp.where(bm == b, rb_ref[b, h], t)
        tbl_ref[0, :, half * WINDOW:(half + 1) * WINDOW] = t


def _bias_tables(rel_bias):
    bm = jnp.asarray(_bucket_map())
    return pl.pallas_call(
        _bias_table_kernel,
        grid=(N_PAIRS,),
        in_specs=[pl.BlockSpec(memory_space=pltpu.SMEM), pl.BlockSpec(bm.shape, lambda p: (0, 0))],
        out_specs=pl.BlockSpec((1, WINDOW, 2 * WINDOW), lambda p: (p, 0, 0)),
        out_shape=jax.ShapeDtypeStruct((N_PAIRS, WINDOW, 2 * WINDOW), _F32),
        name="bias_tables",
    )(rel_bias, bm)


def _block_diag_pairs(slab):
    low = lax.broadcasted_iota(jnp.int32, slab.shape, 1) < HEAD_DIM
    swapped = pltpu.roll(slab, HEAD_DIM, axis=1)
    zero = jnp.zeros_like(slab)
    first = jnp.concatenate([jnp.where(low, slab, zero), jnp.where(low, zero, swapped)], axis=0)
    second = jnp.concatenate([jnp.where(low, swapped, zero), jnp.where(low, zero, slab)], axis=0)
    return first.astype(_BF16), second.astype(_BF16)


def _kv_operands(k_blk, v_blk):
    ops = []
    for slab in range(KV_DIM // LANES):
        cols = slice(slab * LANES, (slab + 1) * LANES)
        ops.extend(zip(_block_diag_pairs(k_blk[:, cols]), _block_diag_pairs(v_blk[:, cols])))
    return ops


def _attend(q, prev_ops, own_ops, tbl_ref, sink_ref, prev_shift, store, transposed=False):
    tq = q.shape[0]
    rows = 2 * tq
    row = lax.broadcasted_iota(jnp.int32, (rows, 2 * WINDOW), 0)
    col = lax.broadcasted_iota(jnp.int32, (rows, 2 * WINDOW), 1)
    from_prev = (col & (WINDOW - 1)) > jnp.where(row >= tq, row - tq, row)
    top = lax.broadcasted_iota(jnp.int32, (rows, 1), 0) < tq
    low = lax.broadcasted_iota(jnp.int32, (rows, LANES), 1) < HEAD_DIM
    contract_last = (((1,), (1,)), ((), ()))

    def logits(a, k_op):
        return _dot(a, k_op) if transposed else lax.dot_general(a, k_op, contract_last, preferred_element_type=_F32)

    def weighted_values(pr, v_op):
        return lax.dot_general(pr, v_op, contract_last, preferred_element_type=_F32) if transposed else _dot(pr, v_op)

    for kvh in range(N_KV_HEADS):
        (k_prev, v_prev), (k_own, v_own) = prev_ops[kvh], own_ops[kvh]
        pair_a = 2 * kvh
        pair_b = pair_a + 1
        qq = jnp.concatenate([q[:, pair_a * LANES:(pair_a + 1) * LANES],
                              q[:, pair_b * LANES:(pair_b + 1) * LANES]], axis=0).astype(_BF16)
        sp = logits(qq, k_prev)
        so = logits(qq, k_own)
        bias = jnp.concatenate([tbl_ref[pair_a, 0:tq, :], tbl_ref[pair_b, 0:tq, :]], axis=0)
        s = jnp.where(from_prev, sp + prev_shift, so) + bias
        sink_even = jnp.where(top, sink_ref[2 * pair_a], sink_ref[2 * pair_b])
        sink_odd = jnp.where(top, sink_ref[2 * pair_a + 1], sink_ref[2 * pair_b + 1])
        m_even = jnp.maximum(jnp.max(s[:, :WINDOW], axis=-1, keepdims=True), sink_even)
        m_odd = jnp.maximum(jnp.max(s[:, WINDOW:], axis=-1, keepdims=True), sink_odd)
        p = jnp.exp(s - jnp.where(col < WINDOW, m_even, m_odd)).astype(_BF16)
        zero = jnp.zeros_like(p)
        o = (weighted_values(jnp.where(from_prev, p, zero), v_prev)
             + weighted_values(jnp.where(from_prev, zero, p), v_own))
        pf = p.astype(_F32)
        sums = jnp.where(low, jnp.sum(pf[:, :WINDOW], axis=-1, keepdims=True),
                         jnp.sum(pf[:, WINDOW:], axis=-1, keepdims=True))
        den = sums + jnp.where(low, jnp.exp(sink_even - m_even), jnp.exp(sink_odd - m_odd))
        o = o / den
        store(pair_a, o[:tq])
        store(pair_b, o[tq:])


PROMPT_QBLOCKS = 4


def _attn_prompt_kernel(sink_ref, q_ref, kp_ref, ko_ref, vp_ref, vo_ref, tbl_ref, o_ref):
    prev_shift = jnp.where(pl.program_id(1) == 0, MASK_VALUE, 0.0).astype(_F32)
    ops = [_kv_operands(kp_ref[...], vp_ref[...])]
    for b in range(PROMPT_QBLOCKS):
        rows = slice(b * WINDOW, (b + 1) * WINDOW)
        ops.append(_kv_operands(ko_ref[rows, :], vo_ref[rows, :]))

        def store(pair, o, rows=rows):
            o_ref[rows, pair * LANES:(pair + 1) * LANES] = o.astype(o_ref.dtype)

        _attend(q_ref[rows, :], ops[b], ops[b + 1], tbl_ref, sink_ref,
                prev_shift if b == 0 else jnp.float32(0.0), store)


def _attn_prompt(q, k, v, tbl, sinks, bsz, t):
    tq = PROMPT_QBLOCKS * WINDOW
    nb = t // tq
    own = lambda w: pl.BlockSpec((tq, w), lambda b, i: (b * nb + i, 0))
    prev = lambda w: pl.BlockSpec((WINDOW, w),
                                  lambda b, i: (PROMPT_QBLOCKS * (b * nb + i) - jnp.minimum(i, 1), 0))
    return pl.pallas_call(
        _attn_prompt_kernel,
        grid=(bsz, nb),
        in_specs=[pl.BlockSpec(memory_space=pltpu.SMEM), own(Q_DIM), prev(KV_DIM), own(KV_DIM),
                  prev(KV_DIM), own(KV_DIM), pl.BlockSpec(tbl.shape, lambda b, i: (0, 0, 0))],
        out_specs=own(Q_DIM),
        out_shape=jax.ShapeDtypeStruct((bsz * t, Q_DIM), _BF16),
        compiler_params=pltpu.CompilerParams(dimension_semantics=("arbitrary", "arbitrary"),
                                             vmem_limit_bytes=VMEM_LIMIT),
        name="attn_prompt",
    )(sinks, q, k, k, v, v, tbl)


SAMPLE_UNROLL = 2


def _block_diag_t(x):
    xb = x.astype(_BF16)
    z = jnp.zeros_like(xb)
    return jnp.concatenate([jnp.concatenate([xb, z], axis=1), jnp.concatenate([z, xb], axis=1)], axis=0)


def _attn_sample_kernel(sink_ref, q_ref, kn_ref, vn_ref, ck_ref, cv_ref, tbl_ref, o_ref, cko_ref, cvo_ref,
                        *, sb, steps):
    pad = jnp.zeros((WINDOW - steps, LANES), _F32)
    lane = lax.broadcasted_iota(jnp.int32, (HEAD_DIM, WINDOW), 1)

    def one_sequence(s, carry):
        def store(pair, o):
            o_ref[s, :, pair * LANES:(pair + 1) * LANES] = o

        prev_ops, own_ops = [], []
        for slab in range(KV_DIM // LANES):
            cols = slice(slab * LANES, (slab + 1) * LANES)
            new_k = jnp.concatenate([kn_ref[s][:, cols], pad], axis=0).T
            new_v = jnp.concatenate([vn_ref[s][:, cols], pad], axis=0).T
            for sub in range(2):
                kvh = 2 * slab + sub
                part = slice(sub * HEAD_DIM, (sub + 1) * HEAD_DIM)
                kt, vt = ck_ref[s, kvh], cv_ref[s, kvh]
                cko_ref[s, kvh] = pltpu.roll(jnp.where(lane < steps, new_k[part], kt), WINDOW - steps, axis=1)
                cvo_ref[s, kvh] = pltpu.roll(jnp.where(lane < steps, new_v[part], vt), WINDOW - steps, axis=1)
                prev_ops.append((_block_diag_t(kt), _block_diag_t(vt)))
                own_ops.append((_block_diag_t(new_k[part]), _block_diag_t(new_v[part])))
        _attend(q_ref[s], prev_ops, own_ops, tbl_ref, sink_ref, jnp.float32(0.0), store, transposed=True)
        return carry

    lax.fori_loop(0, sb, one_sequence, 0, unroll=SAMPLE_UNROLL)


def _attn_sample(q, k_new, v_new, cache_kt, cache_vt, tbl, sinks, sb):
    nseq, steps, _ = q.shape
    seq = lambda r, w: pl.BlockSpec((sb, r, w), lambda i: (i, 0, 0))
    cache = pl.BlockSpec((sb, N_KV_HEADS, HEAD_DIM, WINDOW), lambda i: (i, 0, 0, 0))
    return pl.pallas_call(
        functools.partial(_attn_sample_kernel, sb=sb, steps=steps),
        grid=(nseq // sb,),
        in_specs=[pl.BlockSpec(memory_space=pltpu.SMEM), seq(steps, Q_DIM), seq(steps, KV_DIM), seq(steps, KV_DIM),
                  cache, cache, pl.BlockSpec(tbl.shape, lambda i: (0, 0, 0))],
        out_specs=[seq(steps, Q_DIM), cache, cache],
        out_shape=[jax.ShapeDtypeStruct((nseq, steps, Q_DIM), _F32),
                   jax.ShapeDtypeStruct(cache_kt.shape, _F32), jax.ShapeDtypeStruct(cache_vt.shape, _F32)],
        compiler_params=pltpu.CompilerParams(dimension_semantics=("arbitrary",),
                                             vmem_limit_bytes=VMEM_LIMIT),
        name="attn_sample",
    )(sinks, q, k_new, v_new, cache_kt, cache_vt, tbl)


def _lane_min_index(mask, lane):
    return jnp.min(jnp.where(mask, lane, LANES), axis=-1, keepdims=True)


def _finish_kernel(x_ref, conv_ref, o_ref, ga_ref, gb_ref, wa_ref, wo_ref, ng_ref, wr_hi_ref, wr_lo_ref, br_ref,
                   tri_ref, h_ref, hn_ref, route_ref, count_ref, running_ref):
    @pl.when(pl.program_id(0) == 0)
    def _():
        running_ref[...] = jnp.zeros_like(running_ref)

    attn_out = _dot(o_ref[...].astype(_BF16), wa_ref[...])
    merged = ga_ref[...].astype(_F32) * conv_ref[...].astype(_F32) + gb_ref[...].astype(_F32) * attn_out
    h = x_ref[...] + _dot(merged.astype(_BF16), wo_ref[...])
    h_ref[...] = h
    hn = h * lax.rsqrt(jnp.mean(h * h, axis=-1, keepdims=True) + EPS) * ng_ref[...]
    hn_ref[...] = _pack_bf16_pairs(hn)

    hi, lo = _split_bf16(hn)
    logits = _dot(hi, wr_hi_ref[...]) + _dot(lo, wr_hi_ref[...]) + _dot(hi, wr_lo_ref[...]) + br_ref[...]
    lane = lax.broadcasted_iota(jnp.int32, logits.shape, 1)
    gmask = lane < N_GROUPS
    gl = jnp.where(gmask, logits, MASK_VALUE)
    gmax = jnp.max(gl, axis=-1, keepdims=True)
    grp = _lane_min_index(gmask & (gl == gmax), lane)
    p_grp = 1.0 / jnp.sum(jnp.where(gmask, jnp.exp(gl - gmax), 0.0), axis=-1, keepdims=True)
    e_lo = N_GROUPS + grp * EXPERTS_PER_GROUP
    emask = (lane >= e_lo) & (lane < e_lo + EXPERTS_PER_GROUP)
    el = jnp.where(emask, logits, MASK_VALUE)
    ex = jnp.where(emask, jnp.exp(el - jnp.max(el, axis=-1, keepdims=True)), 0.0)
    prob = jnp.where(emask, ex / jnp.sum(ex, axis=-1, keepdims=True), -1.0)
    p1 = jnp.max(prob, axis=-1, keepdims=True)
    i1 = _lane_min_index(prob == p1, lane)
    rest = jnp.where(lane == i1, -1.0, prob)
    p2 = jnp.max(rest, axis=-1, keepdims=True)
    i2 = _lane_min_index(rest == p2, lane)
    w1 = p_grp * p1 / (p1 + p2)
    w2 = p_grp * p2 / (p1 + p2)
    e1 = i1 - N_GROUPS
    e2 = i2 - N_GROUPS

    hot1 = lane == e1
    hot2 = lane == e2
    hot = jnp.where(hot1 | hot2, 1.0, 0.0)
    before = _dot(tri_ref[...], hot.astype(_BF16)) + running_ref[...]
    rank1 = jnp.sum(jnp.where(hot1, before, 0.0), axis=-1, keepdims=True)
    rank2 = jnp.sum(jnp.where(hot2, before, 0.0), axis=-1, keepdims=True)
    running_ref[...] += jnp.sum(hot, axis=0, keepdims=True)
    count_ref[...] = jnp.broadcast_to(running_ref[...], count_ref.shape)

    fields = (e1.astype(_F32), e2.astype(_F32), w1, w2, rank1, rank2)
    route = jnp.zeros(logits.shape, _F32)
    for pos, val in enumerate(fields):
        route = jnp.where(lane == pos, val, route)
    route_ref[...] = route


ROUTE_E, ROUTE_W, ROUTE_RANK = 0, 2, 4
DEST_ROWS = 1024


def _finish(x, conv_out, o, ga, gb, wa_b, wo_b, ng, wr_hi, wr_lo, br, tm):
    n = x.shape[0]
    tri = jnp.asarray(np.tril(np.ones((tm, tm), np.float32), -1), _BF16)
    row = lambda w: pl.BlockSpec((tm, w), lambda i: (i, 0))
    full = _resident
    return pl.pallas_call(
        _finish_kernel,
        grid=(n // tm,),
        in_specs=[row(D_MODEL), row(D_MODEL), row(Q_DIM), row(D_MODEL), row(D_MODEL),
                  full(wa_b), full(wo_b), full(ng), full(wr_hi), full(wr_lo), full(br), full(tri)],
        out_specs=[row(D_MODEL), row(HALF), row(LANES), pl.BlockSpec((SUBLANES, LANES), lambda i: (0, 0))],
        out_shape=[jax.ShapeDtypeStruct((n, D_MODEL), _F32),
                   jax.ShapeDtypeStruct((n, HALF), jnp.uint32),
                   jax.ShapeDtypeStruct((n, LANES), _F32),
                   jax.ShapeDtypeStruct((SUBLANES, LANES), _F32)],
        scratch_shapes=[pltpu.VMEM((1, LANES), _F32)],
        compiler_params=pltpu.CompilerParams(dimension_semantics=("arbitrary",),
                                             vmem_limit_bytes=VMEM_LIMIT),
        name="finish",
    )(x, conv_out, o, ga, gb, wa_b, wo_b, ng, wr_hi, wr_lo, br, tri)


def _dest_kernel(route_ref, starts_ref, dest_ref):
    route = route_ref[...]
    lane = lax.broadcasted_iota(jnp.int32, route.shape, 1)
    out = jnp.zeros(route.shape, jnp.int32)
    for j in range(TOP_K):
        e = route[:, ROUTE_E + j:ROUTE_E + j + 1].astype(jnp.int32)
        start = jnp.sum(jnp.where(lane == e, starts_ref[...], 0.0), axis=-1, keepdims=True)
        d = (start + route[:, ROUTE_RANK + j:ROUTE_RANK + j + 1]).astype(jnp.int32)
        out = jnp.where(lane == j, d, out)
    dest_ref[...] = out


def _dest(route, starts_row, tm):
    n = route.shape[0]
    row = pl.BlockSpec((tm, LANES), lambda i: (i, 0))
    return pl.pallas_call(
        _dest_kernel,
        grid=(n // tm,),
        in_specs=[row, pl.BlockSpec((1, LANES), lambda i: (0, 0))],
        out_specs=row,
        out_shape=jax.ShapeDtypeStruct((n, LANES), jnp.int32),
        name="dest",
    )(route, starts_row)


SC_CORES = 2
SC_SUBCORES = 16
SC_WORKERS = SC_CORES * SC_SUBCORES
SC_IN_FLIGHT = 4
SC_CHUNK_BYTES = 64 * 1024


def _sc_move_rows(src, idx, gather):
    n, d = src.shape
    b = idx.shape[0]
    per_worker = b // SC_WORKERS
    assert per_worker * SC_WORKERS == b and (gather or n % per_worker == 0), (b, n)
    chunk = min(per_worker // SC_IN_FLIGHT, SC_CHUNK_BYTES // (d * 4))
    n_iters = per_worker // (chunk * SC_IN_FLIGHT)
    assert n_iters * chunk * SC_IN_FLIGHT == per_worker and chunk % SUBLANES == 0, (per_worker, chunk)
    mesh = plsc.VectorSubcoreMesh(core_axis_name="c", subcore_axis_name="s",
                                  num_cores=SC_CORES, num_subcores=SC_SUBCORES)
    scratch = ([pltpu.VMEM((chunk,), jnp.int32)] * SC_IN_FLIGHT + [pltpu.VMEM((chunk, d), src.dtype)] * SC_IN_FLIGHT
               + [pltpu.SemaphoreType.DMA] * SC_IN_FLIGHT)

    @functools.partial(pl.kernel, mesh=mesh, out_type=jax.ShapeDtypeStruct((b, d), src.dtype),
                       scratch_types=scratch, name="sc_gather_rows" if gather else "sc_scatter_rows",
                       cost_estimate=pl.CostEstimate(flops=0, transcendentals=0, bytes_accessed=2 * b * d * 4 + b * 4))
    def move(src_hbm, idx_hbm, out_hbm, *bufs):
        idx_v = bufs[:SC_IN_FLIGHT]
        rows_v = bufs[SC_IN_FLIGHT:2 * SC_IN_FLIGHT]
        sems = bufs[2 * SC_IN_FLIGHT:]
        worker = lax.axis_index("s") * SC_CORES + lax.axis_index("c")

        @pl.loop(0, n_iters)
        def _(it):
            bases = [pl.multiple_of(worker * per_worker + (it * SC_IN_FLIGHT + j) * chunk, chunk)
                     for j in range(SC_IN_FLIGHT)]
            loads = [pltpu.async_copy(idx_hbm.at[pl.ds(bases[j], chunk)], idx_v[j], sems[j])
                     for j in range(SC_IN_FLIGHT)]
            reads = []
            for j in range(SC_IN_FLIGHT):
                loads[j].wait()
                if gather:
                    rows = src_hbm.at[idx_v[j]]
                else:
                    rows = src_hbm.at[pl.ds(pl.multiple_of(lax.rem(bases[j], n), chunk), chunk)]
                reads.append(pltpu.async_copy(rows, rows_v[j], sems[j]))
            writes = []
            for j in range(SC_IN_FLIGHT):
                reads[j].wait()
                dst = out_hbm.at[pl.ds(bases[j], chunk)] if gather else out_hbm.at[idx_v[j]]
                writes.append(pltpu.async_copy(rows_v[j], dst, sems[j]))
            for w in writes:
                w.wait()

    return move(src, idx)


def _sc_gather_rows(table, idx):
    return _sc_move_rows(table, idx, gather=True)


def _sc_scatter_rows(src, idx):
    return _sc_move_rows(src, idx, gather=False)


def _expert_kernel(blk_ref, exp_ref, lo_ref, hi_ref, x_ref, wg_ref, wu_ref, wd_ref, yb_ref,
                   wg_b, wu_b, wd_b, held_ref):
    del blk_ref
    k = pl.program_id(0)
    lo, hi, e = lo_ref[k], hi_ref[k], exp_ref[k]

    @pl.when(k == 0)
    def _():
        held_ref[0] = -1

    @pl.when(hi > lo)
    def _():
        @pl.when(held_ref[0] != e)
        def _():
            wg_b[...] = wg_ref[0].astype(_BF16)
            wu_b[...] = wu_ref[0].astype(_BF16)
            wd_b[...] = wd_ref[0].astype(_BF16)
            held_ref[0] = e

        xb = _unpack_bf16_pairs(x_ref[...]).astype(_BF16)
        g = _dot(xb, wg_b[...])
        u = _dot(xb, wu_b[...])
        hid = g * jax.nn.sigmoid(g) * u
        y = _dot(hid.astype(_BF16), wd_b[...])
        r = lax.broadcasted_iota(jnp.int32, yb_ref.shape, 0)
        pltpu.store(yb_ref, _pack_bf16_pairs(y), mask=(r >= lo) & (r < hi))


def _experts(items, xs, w_gate, w_up, w_down):
    n_items = items[0].shape[0]
    wspec = lambda a: pl.BlockSpec((1,) + a.shape[1:], lambda k, blk, exp, lo, hi: (exp[k], 0, 0))
    rows = pl.BlockSpec((MOE_ROWS, HALF), lambda k, blk, exp, lo, hi: (blk[k], 0))
    grid_spec = pltpu.PrefetchScalarGridSpec(
        num_scalar_prefetch=4,
        grid=(n_items,),
        in_specs=[rows, wspec(w_gate), wspec(w_up), wspec(w_down)],
        out_specs=rows,
        scratch_shapes=[pltpu.VMEM(w_gate.shape[1:], _BF16), pltpu.VMEM(w_up.shape[1:], _BF16),
                        pltpu.VMEM(w_down.shape[1:], _BF16), pltpu.SMEM((1,), jnp.int32)],
    )
    return pl.pallas_call(
        _expert_kernel,
        grid_spec=grid_spec,
        out_shape=jax.ShapeDtypeStruct(xs.shape, xs.dtype),
        compiler_params=pltpu.CompilerParams(dimension_semantics=("arbitrary",),
                                             vmem_limit_bytes=VMEM_LIMIT),
        cost_estimate=pl.CostEstimate(
            flops=n_items * MOE_ROWS * 6 * D_MODEL * D_EXPERT, transcendentals=n_items * MOE_ROWS * D_EXPERT,
            bytes_accessed=2 * xs.size * 4 + (w_gate.size + w_up.size + w_down.size) * 4),
        name="experts",
    )(*items, xs, w_gate, w_up, w_down)


def _combine_kernel(h_ref, route_ref, g0_ref, g1_ref, y_ref):
    route = route_ref[...]
    y_ref[...] = (h_ref[...] + route[:, ROUTE_W:ROUTE_W + 1] * _unpack_bf16_pairs(g0_ref[...])
                  + route[:, ROUTE_W + 1:ROUTE_W + 2] * _unpack_bf16_pairs(g1_ref[...]))


def _combine(h, route, g, tm):
    n = h.shape[0]
    nt = n // tm
    row = lambda w: pl.BlockSpec((tm, w), lambda i: (i, 0))
    return pl.pallas_call(
        _combine_kernel,
        grid=(nt,),
        in_specs=[row(D_MODEL), row(LANES), row(HALF), pl.BlockSpec((tm, HALF), lambda i: (nt + i, 0))],
        out_specs=row(D_MODEL),
        out_shape=jax.ShapeDtypeStruct((n, D_MODEL), _F32),
        compiler_params=pltpu.CompilerParams(dimension_semantics=("arbitrary",),
                                             vmem_limit_bytes=VMEM_LIMIT),
        name="combine",
    )(h, route, g, g)


def _work_items(counts, n_pairs):
    n_blocks = n_pairs // MOE_ROWS
    starts = jnp.cumsum(counts) - counts
    cuts = jnp.sort(jnp.concatenate([jnp.arange(n_blocks, dtype=jnp.int32) * MOE_ROWS, starts]))
    ends = jnp.concatenate([cuts[1:], jnp.full((1,), n_pairs, jnp.int32)])
    blk = jnp.minimum(cuts // MOE_ROWS, n_blocks - 1)
    expert = jnp.clip(jnp.sum(starts[None, :] <= cuts[:, None], axis=1) - 1, 0, N_EXPERTS - 1).astype(jnp.int32)
    return starts, (blk, expert, cuts - blk * MOE_ROWS, ends - blk * MOE_ROWS)


def _dispatch(hn, route, counts_rows):
    n = hn.shape[0]
    n_pairs = n * TOP_K
    counts = counts_rows[0, :N_EXPERTS].astype(jnp.int32)
    starts, items = _work_items(counts, n_pairs)
    starts_row = jnp.zeros((1, LANES), _F32).at[0, :N_EXPERTS].set(starts.astype(_F32))
    dest = _dest(route, starts_row, min(n, DEST_ROWS))[:, :TOP_K].T.reshape(n_pairs)
    return _sc_scatter_rows(hn, dest), dest, items


def _after(value, *earlier):
    return lax.optimization_barrier((value,) + earlier)[0]


def kernel(x_prompt, x_sample, state_conv, cache_k, cache_v, norm_attn_g, w_in, q_norm_g, k_norm_g, rel_bias, attn_sinks, w_dw, b_dw, conv_ln_g, conv_ln_b, w_conv_out, b_conv_out, w_attn_out, w_out, norm_ffn_g, w_grp, b_grp, w_router, b_router, w_gate, w_up, w_down):
    bsz, t, _ = x_prompt.shape
    nseq, steps, _ = x_sample.shape
    row = lambda a: a.reshape(1, -1).astype(_F32)

    w_in_b = w_in.astype(_BF16)
    wco_b = w_conv_out.astype(_BF16)
    wa_b = w_attn_out.astype(_BF16)
    wo_b = w_out.astype(_BF16)
    qg = row(jnp.tile(q_norm_g, N_HEADS)) * (HEAD_DIM ** -0.5)
    kg = row(jnp.tile(k_norm_g, N_KV_HEADS))
    w_rt = jnp.zeros((D_MODEL, LANES), _F32).at[:, :N_GROUPS].set(w_grp).at[:, N_GROUPS:N_GROUPS + N_EXPERTS].set(w_router)
    wr_hi = w_rt.astype(_BF16)
    wr_lo = (w_rt - wr_hi.astype(_F32)).astype(_BF16)
    b_rt = jnp.zeros((1, LANES), _F32).at[0, :N_GROUPS].set(b_grp).at[0, N_GROUPS:N_GROUPS + N_EXPERTS].set(b_router)
    tbl = _bias_tables(rel_bias)
    conv_params = (w_dw, row(b_dw), row(conv_ln_g), row(conv_ln_b), wco_b, row(b_conv_out))

    def finish(x2d, conv_out, o, ga, gb):
        return _finish(x2d, conv_out, o, ga, gb, wa_b, wo_b, row(norm_ffn_g), wr_hi, wr_lo, b_rt, ROW_TILE)

    xp = x_prompt.reshape(bsz * t, D_MODEL)
    glu, q, k, v, ga, gb = _inproj(xp, row(norm_attn_g), w_in_b, qg, kg, _BF16, ROW_TILE)
    conv_out = _conv_prompt(glu, bsz, t, *conv_params, ROW_TILE)
    o = _attn_prompt(q, k, v, tbl, attn_sinks, bsz, t)
    h_p, hn_p, route_p, counts_p = finish(xp, conv_out, o, ga, gb)
    rows_p, dest_p, items_p = _dispatch(hn_p, route_p, counts_p)
    glu3 = glu.reshape(bsz, t, D_CONV)
    state_conv_prompt = glu3[:, t - (CONV_WIDTH - 1):]
    tail = lambda a: a.reshape(bsz, t, KV_DIM)[:, t - WINDOW:].reshape(bsz, WINDOW, N_KV_HEADS, HEAD_DIM)
    cache_k_prompt, cache_v_prompt = tail(k), tail(v)

    xs = _after(x_sample, dest_p).reshape(nseq * steps, D_MODEL)
    glu, q, k, v, ga, gb = _inproj(xs, row(norm_attn_g), w_in_b, qg, kg, _F32, ROW_TILE)
    glu_t = glu.reshape(nseq, steps, D_CONV).transpose(1, 0, 2)
    conv_out, state_t = _conv_sample(state_conv.transpose(1, 0, 2), glu_t, *conv_params, 64)
    conv_out = conv_out.transpose(1, 0, 2).reshape(nseq * steps, D_MODEL)
    k3 = k.reshape(nseq, steps, KV_DIM)
    v3 = v.reshape(nseq, steps, KV_DIM)
    o, ck_t, cv_t = _attn_sample(q.reshape(nseq, steps, Q_DIM), k3, v3, cache_k.transpose(0, 2, 3, 1),
                                 cache_v.transpose(0, 2, 3, 1), tbl, attn_sinks, 8)
    h_s, hn_s, route_s, counts_s = finish(xs, conv_out, o.reshape(nseq * steps, Q_DIM), ga, gb)
    rows_s, dest_s, items_s = _dispatch(hn_s, route_s, counts_s)
    state_conv_sample = state_t.transpose(1, 0, 2)
    cache_k_sample = ck_t.transpose(0, 3, 1, 2)
    cache_v_sample = cv_t.transpose(0, 3, 1, 2)

    yb_p = _experts(items_p, rows_p, w_gate, w_up, w_down)
    g_p = _sc_gather_rows(yb_p, dest_p)
    yb_s = _experts(items_s, _after(rows_s, yb_p), w_gate, w_up, w_down)
    g_s = _sc_gather_rows(yb_s, dest_s)
    y_prompt = _combine(h_p, route_p, g_p, ROW_TILE).reshape(bsz, t, D_MODEL)
    y_sample = _combine(h_s, route_s, _after(g_s, y_prompt), ROW_TILE).reshape(nseq, steps, D_MODEL)

    return (y_prompt, y_sample, state_conv_prompt, cache_k_prompt, cache_v_prompt,
            state_conv_sample, cache_k_sample, cache_v_sample)
```

```python
import functools
import math

import numpy as np
import jax
import jax.numpy as jnp
from jax import lax
from jax.experimental import pallas as pl
from jax.experimental.pallas import tpu as pltpu
from jax.experimental.pallas import tpu_sc as plsc

D_MODEL = 1024
N_HEADS = 16
HEAD_DIM = 64
N_KV_HEADS = 4
WINDOW = 128
Q_DIM = N_HEADS * HEAD_DIM
KV_DIM = N_KV_HEADS * HEAD_DIM
N_BUCKETS = 32
MAX_EXACT = N_BUCKETS // 2
MAX_DISTANCE = 128
D_CONV = D_MODEL
CONV_WIDTH = 31
N_GROUPS = 4
EXPERTS_PER_GROUP = 8
N_EXPERTS = N_GROUPS * EXPERTS_PER_GROUP
TOP_K = 2
D_EXPERT = 256
EPS = 1e-6

LANES = 128
SUBLANES = 8
N_PAIRS = N_HEADS // 2
MOE_ROWS = 512
MASK_VALUE = -1e30
VMEM_LIMIT = 56 * 1024 * 1024
ROW_TILE = 512

_F32 = jnp.float32
_BF16 = jnp.bfloat16


def _resident(a):
    return pl.BlockSpec(a.shape, lambda *_: (0,) * a.ndim, pipeline_mode=pl.Buffered(1))


def _dot(a, b):
    return jnp.dot(a, b, preferred_element_type=_F32)


def _split_bf16(x):
    hi = x.astype(_BF16)
    lo = (x - hi.astype(_F32)).astype(_BF16)
    return hi, lo


HALF = D_MODEL // 2


def _pack_bf16_pairs(x):
    lo = pltpu.bitcast(x[:, :HALF].astype(_BF16).astype(_F32), jnp.uint32)
    hi = pltpu.bitcast(x[:, HALF:].astype(_BF16).astype(_F32), jnp.uint32)
    return hi | (lo >> 16)


def _unpack_bf16_pairs(w):
    lo = pltpu.bitcast(w << 16, _F32)
    hi = pltpu.bitcast(w & jnp.uint32(0xFFFF0000), _F32)
    return jnp.concatenate([lo, hi], axis=1)


def _head_rms_scale(z):
    low = lax.broadcasted_iota(jnp.int32, (z.shape[0], LANES), 1) < HEAD_DIM
    slabs = []
    for c in range(z.shape[1] // LANES):
        sq = z[:, c * LANES:(c + 1) * LANES]
        sq = sq * sq
        first = jnp.sum(jnp.where(low, sq, 0.0), axis=-1, keepdims=True)
        second = jnp.sum(jnp.where(low, 0.0, sq), axis=-1, keepdims=True)
        slabs.append(lax.rsqrt(jnp.where(low, first, second) * (1.0 / HEAD_DIM) + EPS))
    return jnp.concatenate(slabs, axis=1)


def _inproj_kernel(x_ref, g_ref, w_ref, qg_ref, kg_ref,
                   glu_ref, q_ref, k_ref, v_ref, ga_ref, gb_ref):
    x = x_ref[...]
    xn = x * lax.rsqrt(jnp.mean(x * x, axis=-1, keepdims=True) + EPS) * g_ref[...]
    xb = xn.astype(_BF16)

    def seg(lo, width):
        return _dot(xb, w_ref[:, lo:lo + width])

    a = seg(0, D_CONV)
    b = seg(D_CONV, D_CONV)
    glu_ref[...] = a * jax.nn.sigmoid(b)
    off = 2 * D_CONV
    q = seg(off, Q_DIM)
    q_ref[...] = (q * _head_rms_scale(q) * qg_ref[...]).astype(q_ref.dtype)
    off += Q_DIM
    k = seg(off, KV_DIM)
    k_ref[...] = k * _head_rms_scale(k) * kg_ref[...]
    off += KV_DIM
    v_ref[...] = seg(off, KV_DIM)
    off += KV_DIM
    ga_ref[...] = jax.nn.sigmoid(seg(off, D_MODEL)).astype(ga_ref.dtype)
    off += D_MODEL
    gb_ref[...] = jax.nn.sigmoid(seg(off, D_MODEL)).astype(gb_ref.dtype)


def _inproj(x, g, w_in_b, qg, kg, q_dtype, tm):
    n = x.shape[0]
    in_dim = w_in_b.shape[1]
    row = lambda w: pl.BlockSpec((tm, w), lambda i: (i, 0))
    full = _resident
    return pl.pallas_call(
        _inproj_kernel,
        grid=(n // tm,),
        in_specs=[row(D_MODEL), full(g), full(w_in_b), full(qg), full(kg)],
        out_specs=[row(D_CONV), row(Q_DIM), row(KV_DIM), row(KV_DIM), row(D_MODEL), row(D_MODEL)],
        out_shape=[jax.ShapeDtypeStruct((n, D_CONV), _F32),
                   jax.ShapeDtypeStruct((n, Q_DIM), q_dtype),
                   jax.ShapeDtypeStruct((n, KV_DIM), _F32),
                   jax.ShapeDtypeStruct((n, KV_DIM), _F32),
                   jax.ShapeDtypeStruct((n, D_MODEL), _BF16),
                   jax.ShapeDtypeStruct((n, D_MODEL), _BF16)],
        compiler_params=pltpu.CompilerParams(dimension_semantics=("arbitrary",),
                                             vmem_limit_bytes=VMEM_LIMIT),
        name="inproj",
    )(x, g, w_in_b, qg, kg)


def _ln_swish_project(y, lng_ref, lnb_ref, wo_ref, bo_ref):
    mu = jnp.mean(y, axis=-1, keepdims=True)
    yc = y - mu
    var = jnp.mean(yc * yc, axis=-1, keepdims=True)
    z = yc * lax.rsqrt(var + EPS) * lng_ref[...] + lnb_ref[...]
    z = z * jax.nn.sigmoid(z)
    return (_dot(z.astype(_BF16), wo_ref[...]) + bo_ref[...]).astype(_BF16)


HALO = 32
CONV_STEPS = 16
CH_TILES = D_CONV // LANES


def _conv_prompt_kernel(glu_ref, w8_ref, b8_ref, lng_ref, lnb_ref, wo_ref, bo_ref,
                        out_ref, hist_ref, y_ref, *, tm):
    i = pl.program_id(1)

    @pl.when(i == 0)
    def _():
        hist_ref[0:HALO * CH_TILES, :] = jnp.zeros((HALO * CH_TILES, LANES), _F32)

    @pl.when(i > 0)
    def _():
        hist_ref[0:HALO * CH_TILES, :] = hist_ref[tm * CH_TILES:(tm + HALO) * CH_TILES, :]

    for c in range(CH_TILES):
        hist_ref[pl.ds(HALO * CH_TILES + c, tm, stride=CH_TILES), :] = glu_ref[:, c * LANES:(c + 1) * LANES]

    first = HALO - (CONV_WIDTH - 1)
    span = CONV_STEPS + CONV_WIDTH - 1

    def chunk(ci, carry):
        t0 = ci * CONV_STEPS
        x = hist_ref[pl.ds(pl.multiple_of((t0 + first) * CH_TILES, CH_TILES), span * CH_TILES), :]
        x = x.reshape(span, CH_TILES, LANES)
        acc = jnp.broadcast_to(b8_ref[...][None], (CONV_STEPS, CH_TILES, LANES))
        for j in range(CONV_WIDTH):
            acc = acc + x[j:j + CONV_STEPS] * w8_ref[j][None]
        y_ref[pl.ds(pl.multiple_of(t0 * CH_TILES, CH_TILES), CONV_STEPS * CH_TILES), :] = (
            acc.reshape(CONV_STEPS * CH_TILES, LANES))
        return carry

    lax.fori_loop(0, tm // CONV_STEPS, chunk, 0)
    y = jnp.concatenate([y_ref[pl.ds(c, tm, stride=CH_TILES), :] for c in range(CH_TILES)], axis=1)
    out_ref[...] = _ln_swish_project(y, lng_ref, lnb_ref, wo_ref, bo_ref)


def _conv_prompt(glu, bsz, t, w_dw, b_dw, lng, lnb, wo_b, bo, tm):
    assert CH_TILES == SUBLANES
    nt = t // tm
    w8 = w_dw.reshape(CONV_WIDTH, CH_TILES, LANES)
    b8 = b_dw.reshape(CH_TILES, LANES)
    full = _resident
    row = pl.BlockSpec((tm, D_CONV), lambda b, i: (b * nt + i, 0))
    return pl.pallas_call(
        functools.partial(_conv_prompt_kernel, tm=tm),
        grid=(bsz, nt),
        in_specs=[row, full(w8), full(b8), full(lng), full(lnb), full(wo_b), full(bo)],
        out_specs=pl.BlockSpec((tm, D_MODEL), lambda b, i: (b * nt + i, 0)),
        out_shape=jax.ShapeDtypeStruct((bsz * t, D_MODEL), _BF16),
        scratch_shapes=[pltpu.VMEM(((tm + HALO) * CH_TILES, LANES), _F32), pltpu.VMEM((tm * CH_TILES, LANES), _F32)],
        compiler_params=pltpu.CompilerParams(dimension_semantics=("arbitrary", "arbitrary"),
                                             vmem_limit_bytes=VMEM_LIMIT),
        name="conv_prompt",
    )(glu, w8, b8, lng, lnb, wo_b, bo)


def _conv_sample_kernel(state_ref, glu_ref, wdw_ref, bdw_ref, lng_ref, lnb_ref, wo_ref, bo_ref,
                        out_ref, state_out_ref):
    keep, steps = state_ref.shape[0], glu_ref.shape[0]

    def hist(u):
        return state_ref[u] if u < keep else glu_ref[u - keep]

    for t in range(steps):
        acc = hist(t) * wdw_ref[0:1, :]
        for j in range(1, CONV_WIDTH):
            acc = acc + hist(t + j) * wdw_ref[j:j + 1, :]
        out_ref[t] = _ln_swish_project(acc + bdw_ref[...], lng_ref, lnb_ref, wo_ref, bo_ref)
    state_out_ref[0:keep - steps] = state_ref[steps:keep]
    state_out_ref[keep - steps:keep] = glu_ref[...]


def _conv_sample(state_t, glu_t, w_dw, b_dw, lng, lnb, wo_b, bo, sb):
    keep, nseq, _ = state_t.shape
    steps = glu_t.shape[0]
    full = _resident
    blk = lambda r: pl.BlockSpec((r, sb, D_CONV), lambda i: (0, i, 0))
    return pl.pallas_call(
        _conv_sample_kernel,
        grid=(nseq // sb,),
        in_specs=[blk(keep), blk(steps), full(w_dw), full(b_dw), full(lng), full(lnb), full(wo_b), full(bo)],
        out_specs=[blk(steps), blk(keep)],
        out_shape=[jax.ShapeDtypeStruct((steps, nseq, D_MODEL), _BF16),
                   jax.ShapeDtypeStruct(state_t.shape, _F32)],
        compiler_params=pltpu.CompilerParams(dimension_semantics=("arbitrary",),
                                             vmem_limit_bytes=VMEM_LIMIT),
        name="conv_sample",
    )(state_t, glu_t, w_dw, b_dw, lng, lnb, wo_b, bo)


def _bucket_map():
    i = np.arange(WINDOW)[:, None]
    j = np.arange(WINDOW)[None, :]
    n = (i - j) % WINDOW
    nf = np.maximum(n, 1).astype(np.float32)
    large = MAX_EXACT + (np.log(nf / np.float32(MAX_EXACT)) / np.float32(math.log(MAX_DISTANCE / MAX_EXACT))
                         * np.float32(N_BUCKETS - MAX_EXACT)).astype(np.int32)
    return np.where(n < MAX_EXACT, n, np.minimum(large, N_BUCKETS - 1)).astype(np.int32)


def _bias_table_kernel(rb_ref, bm_ref, tbl_ref):
    p = pl.program_id(0)
    bm = bm_ref[...]
    for half in range(2):
        h = 2 * p + half
        t = jnp.zeros(bm.shape, _F32)
        for b in range(N_BUCKETS):
            t = jnp.where(bm == b, rb_ref[b, h], t)
        tbl_ref[0, :, half * WINDOW:(half + 1) * WINDOW] = t


def _bias_tables(rel_bias):
    bm = jnp.asarray(_bucket_map())
    return pl.pallas_call(
        _bias_table_kernel,
        grid=(N_PAIRS,),
        in_specs=[pl.BlockSpec(memory_space=pltpu.SMEM), pl.BlockSpec(bm.shape, lambda p: (0, 0))],
        out_specs=pl.BlockSpec((1, WINDOW, 2 * WINDOW), lambda p: (p, 0, 0)),
        out_shape=jax.ShapeDtypeStruct((N_PAIRS, WINDOW, 2 * WINDOW), _F32),
        name="bias_tables",
    )(rel_bias, bm)


def _block_diag_pairs(slab):
    low = lax.broadcasted_iota(jnp.int32, slab.shape, 1) < HEAD_DIM
    swapped = pltpu.roll(slab, HEAD_DIM, axis=1)
    zero = jnp.zeros_like(slab)
    first = jnp.concatenate([jnp.where(low, slab, zero), jnp.where(low, zero, swapped)], axis=0)
    second = jnp.concatenate([jnp.where(low, swapped, zero), jnp.where(low, zero, slab)], axis=0)
    return first.astype(_BF16), second.astype(_BF16)


def _kv_operands(k_blk, v_blk):
    ops = []
    for slab in range(KV_DIM // LANES):
        cols = slice(slab * LANES, (slab + 1) * LANES)
        ops.extend(zip(_block_diag_pairs(k_blk[:, cols]), _block_diag_pairs(v_blk[:, cols])))
    return ops


def _attend(q, prev_ops, own_ops, tbl_ref, sink_ref, prev_shift, store, transposed=False):
    tq = q.shape[0]
    rows = 2 * tq
    row = lax.broadcasted_iota(jnp.int32, (rows, 2 * WINDOW), 0)
    col = lax.broadcasted_iota(jnp.int32, (rows, 2 * WINDOW), 1)
    from_prev = (col & (WINDOW - 1)) > jnp.where(row >= tq, row - tq, row)
    top = lax.broadcasted_iota(jnp.int32, (rows, 1), 0) < tq
    low = lax.broadcasted_iota(jnp.int32, (rows, LANES), 1) < HEAD_DIM
    contract_last = (((1,), (1,)), ((), ()))

    def logits(a, k_op):
        return _dot(a, k_op) if transposed else lax.dot_general(a, k_op, contract_last, preferred_element_type=_F32)

    def weighted_values(pr, v_op):
        return lax.dot_general(pr, v_op, contract_last, preferred_element_type=_F32) if transposed else _dot(pr, v_op)

    for kvh in range(N_KV_HEADS):
        (k_prev, v_prev), (k_own, v_own) = prev_ops[kvh], own_ops[kvh]
        pair_a = 2 * kvh
        pair_b = pair_a + 1
        qq = jnp.concatenate([q[:, pair_a * LANES:(pair_a + 1) * LANES],
                              q[:, pair_b * LANES:(pair_b + 1) * LANES]], axis=0).astype(_BF16)
        sp = logits(qq, k_prev)
        so = logits(qq, k_own)
        bias = jnp.concatenate([tbl_ref[pair_a, 0:tq, :], tbl_ref[pair_b, 0:tq, :]], axis=0)
        s = jnp.where(from_prev, sp + prev_shift, so) + bias
        sink_even = jnp.where(top, sink_ref[2 * pair_a], sink_ref[2 * pair_b])
        sink_odd = jnp.where(top, sink_ref[2 * pair_a + 1], sink_ref[2 * pair_b + 1])
        m_even = jnp.maximum(jnp.max(s[:, :WINDOW], axis=-1, keepdims=True), sink_even)
        m_odd = jnp.maximum(jnp.max(s[:, WINDOW:], axis=-1, keepdims=True), sink_odd)
        p = jnp.exp(s - jnp.where(col < WINDOW, m_even, m_odd)).astype(_BF16)
        zero = jnp.zeros_like(p)
        o = (weighted_values(jnp.where(from_prev, p, zero), v_prev)
             + weighted_values(jnp.where(from_prev, zero, p), v_own))
        pf = p.astype(_F32)
        sums = jnp.where(low, jnp.sum(pf[:, :WINDOW], axis=-1, keepdims=True),
                         jnp.sum(pf[:, WINDOW:], axis=-1, keepdims=True))
        den = sums + jnp.where(low, jnp.exp(sink_even - m_even), jnp.exp(sink_odd - m_odd))
        o = o / den
        store(pair_a, o[:tq])
        store(pair_b, o[tq:])


PROMPT_QBLOCKS = 4


def _attn_prompt_kernel(sink_ref, q_ref, kp_ref, ko_ref, vp_ref, vo_ref, tbl_ref, o_ref):
    prev_shift = jnp.where(pl.program_id(1) == 0, MASK_VALUE, 0.0).astype(_F32)
    ops = [_kv_operands(kp_ref[...], vp_ref[...])]
    for b in range(PROMPT_QBLOCKS):
        rows = slice(b * WINDOW, (b + 1) * WINDOW)
        ops.append(_kv_operands(ko_ref[rows, :], vo_ref[rows, :]))

        def store(pair, o, rows=rows):
            o_ref[rows, pair * LANES:(pair + 1) * LANES] = o.astype(o_ref.dtype)

        _attend(q_ref[rows, :], ops[b], ops[b + 1], tbl_ref, sink_ref,
                prev_shift if b == 0 else jnp.float32(0.0), store)


def _attn_prompt(q, k, v, tbl, sinks, bsz, t):
    tq = PROMPT_QBLOCKS * WINDOW
    nb = t // tq
    own = lambda w: pl.BlockSpec((tq, w), lambda b, i: (b * nb + i, 0))
    prev = lambda w: pl.BlockSpec((WINDOW, w),
                                  lambda b, i: (PROMPT_QBLOCKS * (b * nb + i) - jnp.minimum(i, 1), 0))
    return pl.pallas_call(
        _attn_prompt_kernel,
        grid=(bsz, nb),
        in_specs=[pl.BlockSpec(memory_space=pltpu.SMEM), own(Q_DIM), prev(KV_DIM), own(KV_DIM),
                  prev(KV_DIM), own(KV_DIM), pl.BlockSpec(tbl.shape, lambda b, i: (0, 0, 0))],
        out_specs=own(Q_DIM),
        out_shape=jax.ShapeDtypeStruct((bsz * t, Q_DIM), _BF16),
        compiler_params=pltpu.CompilerParams(dimension_semantics=("arbitrary", "arbitrary"),
                                             vmem_limit_bytes=VMEM_LIMIT),
        name="attn_prompt",
    )(sinks, q, k, k, v, v, tbl)


SAMPLE_UNROLL = 2


def _block_diag_t(x):
    xb = x.astype(_BF16)
    z = jnp.zeros_like(xb)
    return jnp.concatenate([jnp.concatenate([xb, z], axis=1), jnp.concatenate([z, xb], axis=1)], axis=0)


def _attn_sample_kernel(sink_ref, q_ref, kn_ref, vn_ref, ck_ref, cv_ref, tbl_ref, o_ref, cko_ref, cvo_ref,
                        *, sb, steps):
    pad = jnp.zeros((WINDOW - steps, LANES), _F32)
    lane = lax.broadcasted_iota(jnp.int32, (HEAD_DIM, WINDOW), 1)

    def one_sequence(s, carry):
        def store(pair, o):
            o_ref[s, :, pair * LANES:(pair + 1) * LANES] = o

        prev_ops, own_ops = [], []
        for slab in range(KV_DIM // LANES):
            cols = slice(slab * LANES, (slab + 1) * LANES)
            new_k = jnp.concatenate([kn_ref[s][:, cols], pad], axis=0).T
            new_v = jnp.concatenate([vn_ref[s][:, cols], pad], axis=0).T
            for sub in range(2):
                kvh = 2 * slab + sub
                part = slice(sub * HEAD_DIM, (sub + 1) * HEAD_DIM)
                kt, vt = ck_ref[s, kvh], cv_ref[s, kvh]
                cko_ref[s, kvh] = pltpu.roll(jnp.where(lane < steps, new_k[part], kt), WINDOW - steps, axis=1)
                cvo_ref[s, kvh] = pltpu.roll(jnp.where(lane < steps, new_v[part], vt), WINDOW - steps, axis=1)
                prev_ops.append((_block_diag_t(kt), _block_diag_t(vt)))
                own_ops.append((_block_diag_t(new_k[part]), _block_diag_t(new_v[part])))
        _attend(q_ref[s], prev_ops, own_ops, tbl_ref, sink_ref, jnp.float32(0.0), store, transposed=True)
        return carry

    lax.fori_loop(0, sb, one_sequence, 0, unroll=SAMPLE_UNROLL)


def _attn_sample(q, k_new, v_new, cache_kt, cache_vt, tbl, sinks, sb):
    nseq, steps, _ = q.shape
    seq = lambda r, w: pl.BlockSpec((sb, r, w), lambda i: (i, 0, 0))
    cache = pl.BlockSpec((sb, N_KV_HEADS, HEAD_DIM, WINDOW), lambda i: (i, 0, 0, 0))
    return pl.pallas_call(
        functools.partial(_attn_sample_kernel, sb=sb, steps=steps),
        grid=(nseq // sb,),
        in_specs=[pl.BlockSpec(memory_space=pltpu.SMEM), seq(steps, Q_DIM), seq(steps, KV_DIM), seq(steps, KV_DIM),
                  cache, cache, pl.BlockSpec(tbl.shape, lambda i: (0, 0, 0))],
        out_specs=[seq(steps, Q_DIM), cache, cache],
        out_shape=[jax.ShapeDtypeStruct((nseq, steps, Q_DIM), _F32),
                   jax.ShapeDtypeStruct(cache_kt.shape, _F32), jax.ShapeDtypeStruct(cache_vt.shape, _F32)],
        compiler_params=pltpu.CompilerParams(dimension_semantics=("arbitrary",),
                                             vmem_limit_bytes=VMEM_LIMIT),
        name="attn_sample",
    )(sinks, q, k_new, v_new, cache_kt, cache_vt, tbl)


def _lane_min_index(mask, lane):
    return jnp.min(jnp.where(mask, lane, float(LANES)), axis=-1, keepdims=True)


def _finish_kernel(x_ref, conv_ref, o_ref, ga_ref, gb_ref, wa_ref, wo_ref, ng_ref, wr_hi_ref, wr_lo_ref, br_ref,
                   tri_ref, h_ref, hn_ref, route_ref, count_ref, running_ref):
    @pl.when(pl.program_id(0) == 0)
    def _():
        running_ref[...] = jnp.zeros_like(running_ref)

    attn_out = _dot(o_ref[...].astype(_BF16), wa_ref[...])
    merged = ga_ref[...].astype(_F32) * conv_ref[...].astype(_F32) + gb_ref[...].astype(_F32) * attn_out
    h = x_ref[...] + _dot(merged.astype(_BF16), wo_ref[...])
    h_ref[...] = h
    hn = h * lax.rsqrt(jnp.mean(h * h, axis=-1, keepdims=True) + EPS) * ng_ref[...]
    hn_ref[...] = _pack_bf16_pairs(hn)

    hi, lo = _split_bf16(hn)
    logits = _dot(hi, wr_hi_ref[...]) + _dot(lo, wr_hi_ref[...]) + _dot(hi, wr_lo_ref[...]) + br_ref[...]
    lane = lax.broadcasted_iota(jnp.int32, logits.shape, 1).astype(_F32)
    gmask = lane < N_GROUPS
    gl = jnp.where(gmask, logits, MASK_VALUE)
    gmax = jnp.max(gl, axis=-1, keepdims=True)
    grp = _lane_min_index(gmask & (gl == gmax), lane)
    p_grp = 1.0 / jnp.sum(jnp.where(gmask, jnp.exp(gl - gmax), 0.0), axis=-1, keepdims=True)
    e_lo = N_GROUPS + grp * EXPERTS_PER_GROUP
    emask = (lane >= e_lo) & (lane < e_lo + EXPERTS_PER_GROUP)
    el = jnp.where(emask, logits, MASK_VALUE)
    ex = jnp.where(emask, jnp.exp(el - jnp.max(el, axis=-1, keepdims=True)), 0.0)
    prob = jnp.where(emask, ex / jnp.sum(ex, axis=-1, keepdims=True), -1.0)
    p1 = jnp.max(prob, axis=-1, keepdims=True)
    i1 = _lane_min_index(prob == p1, lane)
    rest = jnp.where(lane == i1, -1.0, prob)
    p2 = jnp.max(rest, axis=-1, keepdims=True)
    i2 = _lane_min_index(rest == p2, lane)
    w1 = p_grp * p1 / (p1 + p2)
    w2 = p_grp * p2 / (p1 + p2)
    e1 = i1 - N_GROUPS
    e2 = i2 - N_GROUPS

    hot1 = lane == e1
    hot2 = lane == e2
    hot = jnp.where(hot1 | hot2, 1.0, 0.0)
    before = _dot(tri_ref[...], hot.astype(_BF16)) + running_ref[...]
    rank1 = jnp.sum(jnp.where(hot1, before, 0.0), axis=-1, keepdims=True)
    rank2 = jnp.sum(jnp.where(hot2, before, 0.0), axis=-1, keepdims=True)
    running_ref[...] += jnp.sum(hot, axis=0, keepdims=True)
    count_ref[...] = jnp.broadcast_to(running_ref[...], count_ref.shape)

    fields = (e1, e2, w1, w2, rank1, rank2)
    route = jnp.zeros(logits.shape, _F32)
    for pos, val in enumerate(fields):
        route = jnp.where(lane == pos, val, route)
    route_ref[...] = route


ROUTE_E, ROUTE_W, ROUTE_RANK = 0, 2, 4
DEST_ROWS = 1024


def _finish(x, conv_out, o, ga, gb, wa_b, wo_b, ng, wr_hi, wr_lo, br, tm):
    n = x.shape[0]
    tri = jnp.asarray(np.tril(np.ones((tm, tm), np.float32), -1), _BF16)
    row = lambda w: pl.BlockSpec((tm, w), lambda i: (i, 0))
    full = _resident
    return pl.pallas_call(
        _finish_kernel,
        grid=(n // tm,),
        in_specs=[row(D_MODEL), row(D_MODEL), row(Q_DIM), row(D_MODEL), row(D_MODEL),
                  full(wa_b), full(wo_b), full(ng), full(wr_hi), full(wr_lo), full(br), full(tri)],
        out_specs=[row(D_MODEL), row(HALF), row(LANES), pl.BlockSpec((SUBLANES, LANES), lambda i: (0, 0))],
        out_shape=[jax.ShapeDtypeStruct((n, D_MODEL), _F32),
                   jax.ShapeDtypeStruct((n, HALF), jnp.uint32),
                   jax.ShapeDtypeStruct((n, LANES), _F32),
                   jax.ShapeDtypeStruct((SUBLANES, LANES), _F32)],
        scratch_shapes=[pltpu.VMEM((1, LANES), _F32)],
        compiler_params=pltpu.CompilerParams(dimension_semantics=("arbitrary",),
                                             vmem_limit_bytes=VMEM_LIMIT),
        name="finish",
    )(x, conv_out, o, ga, gb, wa_b, wo_b, ng, wr_hi, wr_lo, br, tri)


def _dest_kernel(route_ref, starts_ref, dest_ref):
    route = route_ref[...]
    lane = lax.broadcasted_iota(jnp.int32, route.shape, 1)
    out = jnp.zeros(route.shape, jnp.int32)
    for j in range(TOP_K):
        e = route[:, ROUTE_E + j:ROUTE_E + j + 1].astype(jnp.int32)
        start = jnp.sum(jnp.where(lane == e, starts_ref[...], 0.0), axis=-1, keepdims=True)
        d = (start + route[:, ROUTE_RANK + j:ROUTE_RANK + j + 1]).astype(jnp.int32)
        out = jnp.where(lane == j, d, out)
    dest_ref[...] = out


def _dest(route, starts_row, tm):
    n = route.shape[0]
    row = pl.BlockSpec((tm, LANES), lambda i: (i, 0))
    return pl.pallas_call(
        _dest_kernel,
        grid=(n // tm,),
        in_specs=[row, pl.BlockSpec((1, LANES), lambda i: (0, 0))],
        out_specs=row,
        out_shape=jax.ShapeDtypeStruct((n, LANES), jnp.int32),
        name="dest",
    )(route, starts_row)


SC_CORES = 2
SC_SUBCORES = 16
SC_WORKERS = SC_CORES * SC_SUBCORES
SC_IN_FLIGHT = 4
SC_CHUNK_BYTES = 64 * 1024


def _sc_move_rows(src, idx, gather):
    n, d = src.shape
    b = idx.shape[0]
    per_worker = b // SC_WORKERS
    assert per_worker * SC_WORKERS == b and (gather or n % per_worker == 0), (b, n)
    chunk = min(per_worker // SC_IN_FLIGHT, SC_CHUNK_BYTES // (d * 4))
    n_iters = per_worker // (chunk * SC_IN_FLIGHT)
    assert n_iters * chunk * SC_IN_FLIGHT == per_worker and chunk % SUBLANES == 0, (per_worker, chunk)
    mesh = plsc.VectorSubcoreMesh(core_axis_name="c", subcore_axis_name="s",
                                  num_cores=SC_CORES, num_subcores=SC_SUBCORES)
    scratch = ([pltpu.VMEM((chunk,), jnp.int32)] * SC_IN_FLIGHT + [pltpu.VMEM((chunk, d), src.dtype)] * SC_IN_FLIGHT
               + [pltpu.SemaphoreType.DMA] * SC_IN_FLIGHT)

    @functools.partial(pl.kernel, mesh=mesh, out_type=jax.ShapeDtypeStruct((b, d), src.dtype),
                       scratch_types=scratch, name="sc_gather_rows" if gather else "sc_scatter_rows",
                       cost_estimate=pl.CostEstimate(flops=0, transcendentals=0, bytes_accessed=2 * b * d * 4 + b * 4))
    def move(src_hbm, idx_hbm, out_hbm, *bufs):
        idx_v = bufs[:SC_IN_FLIGHT]
        rows_v = bufs[SC_IN_FLIGHT:2 * SC_IN_FLIGHT]
        sems = bufs[2 * SC_IN_FLIGHT:]
        worker = lax.axis_index("s") * SC_CORES + lax.axis_index("c")

        @pl.loop(0, n_iters)
        def _(it):
            bases = [pl.multiple_of(worker * per_worker + (it * SC_IN_FLIGHT + j) * chunk, chunk)
                     for j in range(SC_IN_FLIGHT)]
            loads = [pltpu.async_copy(idx_hbm.at[pl.ds(bases[j], chunk)], idx_v[j], sems[j])
                     for j in range(SC_IN_FLIGHT)]
            reads = []
            for j in range(SC_IN_FLIGHT):
                loads[j].wait()
                if gather:
                    rows = src_hbm.at[idx_v[j]]
                else:
                    rows = src_hbm.at[pl.ds(pl.multiple_of(lax.rem(bases[j], n), chunk), chunk)]
                reads.append(pltpu.async_copy(rows, rows_v[j], sems[j]))
            writes = []
            for j in range(SC_IN_FLIGHT):
                reads[j].wait()
                dst = out_hbm.at[pl.ds(bases[j], chunk)] if gather else out_hbm.at[idx_v[j]]
                writes.append(pltpu.async_copy(rows_v[j], dst, sems[j]))
            for w in writes:
                w.wait()

    return move(src, idx)


def _sc_gather_rows(table, idx):
    return _sc_move_rows(table, idx, gather=True)


def _sc_scatter_rows(src, idx):
    return _sc_move_rows(src, idx, gather=False)


def _expert_kernel(blk_ref, exp_ref, lo_ref, hi_ref, x_ref, wg_ref, wu_ref, wd_ref, yb_ref,
                   wg_b, wu_b, wd_b, held_ref):
    del blk_ref
    k = pl.program_id(0)
    lo, hi, e = lo_ref[k], hi_ref[k], exp_ref[k]

    @pl.when(k == 0)
    def _():
        held_ref[0] = -1

    @pl.when(hi > lo)
    def _():
        @pl.when(held_ref[0] != e)
        def _():
            wg_b[...] = wg_ref[0].astype(_BF16)
            wu_b[...] = wu_ref[0].astype(_BF16)
            wd_b[...] = wd_ref[0].astype(_BF16)
            held_ref[0] = e

        xb = _unpack_bf16_pairs(x_ref[...]).astype(_BF16)
        g = _dot(xb, wg_b[...])
        u = _dot(xb, wu_b[...])
        hid = g * jax.nn.sigmoid(g) * u
        y = _dot(hid.astype(_BF16), wd_b[...])
        r = lax.broadcasted_iota(jnp.int32, yb_ref.shape, 0)
        pltpu.store(yb_ref, _pack_bf16_pairs(y), mask=(r >= lo) & (r < hi))


def _experts(items, xs, w_gate, w_up, w_down):
    n_items = items[0].shape[0]
    wspec = lambda a: pl.BlockSpec((1,) + a.shape[1:], lambda k, blk, exp, lo, hi: (exp[k], 0, 0))
    rows = pl.BlockSpec((MOE_ROWS, HALF), lambda k, blk, exp, lo, hi: (blk[k], 0))
    grid_spec = pltpu.PrefetchScalarGridSpec(
        num_scalar_prefetch=4,
        grid=(n_items,),
        in_specs=[rows, wspec(w_gate), wspec(w_up), wspec(w_down)],
        out_specs=rows,
        scratch_shapes=[pltpu.VMEM(w_gate.shape[1:], _BF16), pltpu.VMEM(w_up.shape[1:], _BF16),
                        pltpu.VMEM(w_down.shape[1:], _BF16), pltpu.SMEM((1,), jnp.int32)],
    )
    return pl.pallas_call(
        _expert_kernel,
        grid_spec=grid_spec,
        out_shape=jax.ShapeDtypeStruct(xs.shape, xs.dtype),
        compiler_params=pltpu.CompilerParams(dimension_semantics=("arbitrary",),
                                             vmem_limit_bytes=VMEM_LIMIT),
        cost_estimate=pl.CostEstimate(
            flops=n_items * MOE_ROWS * 6 * D_MODEL * D_EXPERT, transcendentals=n_items * MOE_ROWS * D_EXPERT,
            bytes_accessed=2 * xs.size * 4 + (w_gate.size + w_up.size + w_down.size) * 4),
        name="experts",
    )(*items, xs, w_gate, w_up, w_down)


def _combine_kernel(h_ref, route_ref, g0_ref, g1_ref, y_ref):
    route = route_ref[...]
    y_ref[...] = (h_ref[...] + route[:, ROUTE_W:ROUTE_W + 1] * _unpack_bf16_pairs(g0_ref[...])
                  + route[:, ROUTE_W + 1:ROUTE_W + 2] * _unpack_bf16_pairs(g1_ref[...]))


def _combine(h, route, g, tm):
    n = h.shape[0]
    nt = n // tm
    row = lambda w: pl.BlockSpec((tm, w), lambda i: (i, 0))
    return pl.pallas_call(
        _combine_kernel,
        grid=(nt,),
        in_specs=[row(D_MODEL), row(LANES), row(HALF), pl.BlockSpec((tm, HALF), lambda i: (nt + i, 0))],
        out_specs=row(D_MODEL),
        out_shape=jax.ShapeDtypeStruct((n, D_MODEL), _F32),
        compiler_params=pltpu.CompilerParams(dimension_semantics=("arbitrary",),
                                             vmem_limit_bytes=VMEM_LIMIT),
        name="combine",
    )(h, route, g, g)


def _work_items(counts, n_pairs):
    n_blocks = n_pairs // MOE_ROWS
    starts = jnp.cumsum(counts) - counts
    cuts = jnp.sort(jnp.concatenate([jnp.arange(n_blocks, dtype=jnp.int32) * MOE_ROWS, starts]))
    ends = jnp.concatenate([cuts[1:], jnp.full((1,), n_pairs, jnp.int32)])
    blk = jnp.minimum(cuts // MOE_ROWS, n_blocks - 1)
    expert = jnp.clip(jnp.sum(starts[None, :] <= cuts[:, None], axis=1) - 1, 0, N_EXPERTS - 1).astype(jnp.int32)
    return starts, (blk, expert, cuts - blk * MOE_ROWS, ends - blk * MOE_ROWS)


def _dispatch(hn, route, counts_rows):
    n = hn.shape[0]
    n_pairs = n * TOP_K
    counts = counts_rows[0, :N_EXPERTS].astype(jnp.int32)
    starts, items = _work_items(counts, n_pairs)
    starts_row = jnp.zeros((1, LANES), _F32).at[0, :N_EXPERTS].set(starts.astype(_F32))
    dest = _dest(route, starts_row, min(n, DEST_ROWS))[:, :TOP_K].T.reshape(n_pairs)
    return _sc_scatter_rows(hn, dest), dest, items


def _after(value, *earlier):
    return lax.optimization_barrier((value,) + earlier)[0]


def kernel(x_prompt, x_sample, state_conv, cache_k, cache_v, norm_attn_g, w_in, q_norm_g, k_norm_g, rel_bias, attn_sinks, w_dw, b_dw, conv_ln_g, conv_ln_b, w_conv_out, b_conv_out, w_attn_out, w_out, norm_ffn_g, w_grp, b_grp, w_router, b_router, w_gate, w_up, w_down):
    bsz, t, _ = x_prompt.shape
    nseq, steps, _ = x_sample.shape
    row = lambda a: a.reshape(1, -1).astype(_F32)

    w_in_b = w_in.astype(_BF16)
    wco_b = w_conv_out.astype(_BF16)
    wa_b = w_attn_out.astype(_BF16)
    wo_b = w_out.astype(_BF16)
    qg = row(jnp.tile(q_norm_g, N_HEADS)) * (HEAD_DIM ** -0.5)
    kg = row(jnp.tile(k_norm_g, N_KV_HEADS))
    w_rt = jnp.zeros((D_MODEL, LANES), _F32).at[:, :N_GROUPS].set(w_grp).at[:, N_GROUPS:N_GROUPS + N_EXPERTS].set(w_router)
    wr_hi = w_rt.astype(_BF16)
    wr_lo = (w_rt - wr_hi.astype(_F32)).astype(_BF16)
    b_rt = jnp.zeros((1, LANES), _F32).at[0, :N_GROUPS].set(b_grp).at[0, N_GROUPS:N_GROUPS + N_EXPERTS].set(b_router)
    tbl = _bias_tables(rel_bias)
    conv_params = (w_dw, row(b_dw), row(conv_ln_g), row(conv_ln_b), wco_b, row(b_conv_out))

    def finish(x2d, conv_out, o, ga, gb):
        return _finish(x2d, conv_out, o, ga, gb, wa_b, wo_b, row(norm_ffn_g), wr_hi, wr_lo, b_rt, ROW_TILE)

    xp = x_prompt.reshape(bsz * t, D_MODEL)
    glu, q, k, v, ga, gb = _inproj(xp, row(norm_attn_g), w_in_b, qg, kg, _BF16, ROW_TILE)
    conv_out = _conv_prompt(glu, bsz, t, *conv_params, ROW_TILE)
    o = _attn_prompt(q, k, v, tbl, attn_sinks, bsz, t)
    h_p, hn_p, route_p, counts_p = finish(xp, conv_out, o, ga, gb)
    rows_p, dest_p, items_p = _dispatch(hn_p, route_p, counts_p)
    glu3 = glu.reshape(bsz, t, D_CONV)
    state_conv_prompt = glu3[:, t - (CONV_WIDTH - 1):]
    tail = lambda a: a.reshape(bsz, t, KV_DIM)[:, t - WINDOW:].reshape(bsz, WINDOW, N_KV_HEADS, HEAD_DIM)
    cache_k_prompt, cache_v_prompt = tail(k), tail(v)

    xs = _after(x_sample, dest_p).reshape(nseq * steps, D_MODEL)
    glu, q, k, v, ga, gb = _inproj(xs, row(norm_attn_g), w_in_b, qg, kg, _F32, ROW_TILE)
    glu_t = glu.reshape(nseq, steps, D_CONV).transpose(1, 0, 2)
    conv_out, state_t = _conv_sample(state_conv.transpose(1, 0, 2), glu_t, *conv_params, 64)
    conv_out = conv_out.transpose(1, 0, 2).reshape(nseq * steps, D_MODEL)
    k3 = k.reshape(nseq, steps, KV_DIM)
    v3 = v.reshape(nseq, steps, KV_DIM)
    o, ck_t, cv_t = _attn_sample(q.reshape(nseq, steps, Q_DIM), k3, v3, cache_k.transpose(0, 2, 3, 1),
                                 cache_v.transpose(0, 2, 3, 1), tbl, attn_sinks, 8)
    h_s, hn_s, route_s, counts_s = finish(xs, conv_out, o.reshape(nseq * steps, Q_DIM), ga, gb)
    rows_s, dest_s, items_s = _dispatch(hn_s, route_s, counts_s)
    state_conv_sample = state_t.transpose(1, 0, 2)
    cache_k_sample = ck_t.transpose(0, 3, 1, 2)
    cache_v_sample = cv_t.transpose(0, 3, 1, 2)

    yb_p = _experts(items_p, rows_p, w_gate, w_up, w_down)
    g_p = _sc_gather_rows(yb_p, dest_p)
    yb_s = _experts(items_s, _after(rows_s, yb_p), w_gate, w_up, w_down)
    g_s = _sc_gather_rows(yb_s, dest_s)
    y_prompt = _combine(h_p, route_p, g_p, ROW_TILE).reshape(bsz, t, D_MODEL)
    y_sample = _combine(h_s, route_s, _after(g_s, y_prompt), ROW_TILE).reshape(nseq, steps, D_MODEL)

    return (y_prompt, y_sample, state_conv_prompt, cache_k_prompt, cache_v_prompt,
            state_conv_sample, cache_k_sample, cache_v_sample)
```

```python
import functools
import math

import numpy as np
import jax
import jax.numpy as jnp
from jax import lax
from jax.experimental import pallas as pl
from jax.experimental.pallas import tpu as pltpu
from jax.experimental.pallas import tpu_sc as plsc

D_MODEL = 1024
N_HEADS = 16
HEAD_DIM = 64
N_KV_HEADS = 4
WINDOW = 128
Q_DIM = N_HEADS * HEAD_DIM
KV_DIM = N_KV_HEADS * HEAD_DIM
N_BUCKETS = 32
MAX_EXACT = N_BUCKETS // 2
MAX_DISTANCE = 128
D_CONV = D_MODEL
CONV_WIDTH = 31
N_GROUPS = 4
EXPERTS_PER_GROUP = 8
N_EXPERTS = N_GROUPS * EXPERTS_PER_GROUP
TOP_K = 2
D_EXPERT = 256
EPS = 1e-6

LANES = 128
SUBLANES = 8
N_PAIRS = N_HEADS // 2
MOE_ROWS = 512
MASK_VALUE = -1e30
VMEM_LIMIT = 56 * 1024 * 1024
ROW_TILE = 512

_F32 = jnp.float32
_BF16 = jnp.bfloat16


def _resident(a):
    return pl.BlockSpec(a.shape, lambda *_: (0,) * a.ndim, pipeline_mode=pl.Buffered(1))


def _dot(a, b):
    return jnp.dot(a, b, preferred_element_type=_F32)


def _split_bf16(x):
    hi = x.astype(_BF16)
    lo = (x - hi.astype(_F32)).astype(_BF16)
    return hi, lo


HALF = D_MODEL // 2


def _pack_bf16_pairs(x):
    lo = pltpu.bitcast(x[:, :HALF].astype(_BF16).astype(_F32), jnp.uint32)
    hi = pltpu.bitcast(x[:, HALF:].astype(_BF16).astype(_F32), jnp.uint32)
    return hi | (lo >> 16)


def _unpack_bf16_pairs(w):
    lo = pltpu.bitcast(w << 16, _F32)
    hi = pltpu.bitcast(w & jnp.uint32(0xFFFF0000), _F32)
    return jnp.concatenate([lo, hi], axis=1)


def _head_rms_scale(z):
    low = lax.broadcasted_iota(jnp.int32, (z.shape[0], LANES), 1) < HEAD_DIM
    slabs = []
    for c in range(z.shape[1] // LANES):
        sq = z[:, c * LANES:(c + 1) * LANES]
        sq = sq * sq
        first = jnp.sum(jnp.where(low, sq, 0.0), axis=-1, keepdims=True)
        second = jnp.sum(jnp.where(low, 0.0, sq), axis=-1, keepdims=True)
        slabs.append(lax.rsqrt(jnp.where(low, first, second) * (1.0 / HEAD_DIM) + EPS))
    return jnp.concatenate(slabs, axis=1)


def _inproj_kernel(x_ref, g_ref, w_ref, qg_ref, kg_ref,
                   glu_ref, q_ref, k_ref, v_ref, ga_ref, gb_ref):
    x = x_ref[...]
    xn = x * lax.rsqrt(jnp.mean(x * x, axis=-1, keepdims=True) + EPS) * g_ref[...]
    xb = xn.astype(_BF16)

    def seg(lo, width):
        return _dot(xb, w_ref[:, lo:lo + width])

    a = seg(0, D_CONV)
    b = seg(D_CONV, D_CONV)
    glu_ref[...] = a * jax.nn.sigmoid(b)
    off = 2 * D_CONV
    q = seg(off, Q_DIM)
    q_ref[...] = (q * _head_rms_scale(q) * qg_ref[...]).astype(q_ref.dtype)
    off += Q_DIM
    k = seg(off, KV_DIM)
    k_ref[...] = k * _head_rms_scale(k) * kg_ref[...]
    off += KV_DIM
    v_ref[...] = seg(off, KV_DIM)
    off += KV_DIM
    ga_ref[...] = jax.nn.sigmoid(seg(off, D_MODEL)).astype(ga_ref.dtype)
    off += D_MODEL
    gb_ref[...] = jax.nn.sigmoid(seg(off, D_MODEL)).astype(gb_ref.dtype)


def _inproj(x, g, w_in_b, qg, kg, q_dtype, tm):
    n = x.shape[0]
    in_dim = w_in_b.shape[1]
    row = lambda w: pl.BlockSpec((tm, w), lambda i: (i, 0))
    full = _resident
    return pl.pallas_call(
        _inproj_kernel,
        grid=(n // tm,),
        in_specs=[row(D_MODEL), full(g), full(w_in_b), full(qg), full(kg)],
        out_specs=[row(D_CONV), row(Q_DIM), row(KV_DIM), row(KV_DIM), row(D_MODEL), row(D_MODEL)],
        out_shape=[jax.ShapeDtypeStruct((n, D_CONV), _F32),
                   jax.ShapeDtypeStruct((n, Q_DIM), q_dtype),
                   jax.ShapeDtypeStruct((n, KV_DIM), _F32),
                   jax.ShapeDtypeStruct((n, KV_DIM), _F32),
                   jax.ShapeDtypeStruct((n, D_MODEL), _BF16),
                   jax.ShapeDtypeStruct((n, D_MODEL), _BF16)],
        compiler_params=pltpu.CompilerParams(dimension_semantics=("arbitrary",),
                                             vmem_limit_bytes=VMEM_LIMIT),
        name="inproj",
    )(x, g, w_in_b, qg, kg)


def _ln_swish_project(y, lng_ref, lnb_ref, wo_ref, bo_ref):
    mu = jnp.mean(y, axis=-1, keepdims=True)
    yc = y - mu
    var = jnp.mean(yc * yc, axis=-1, keepdims=True)
    z = yc * lax.rsqrt(var + EPS) * lng_ref[...] + lnb_ref[...]
    z = z * jax.nn.sigmoid(z)
    return (_dot(z.astype(_BF16), wo_ref[...]) + bo_ref[...]).astype(_BF16)


HALO = 32
CONV_STEPS = 16
CH_TILES = D_CONV // LANES


def _conv_prompt_kernel(glu_ref, w8_ref, b8_ref, lng_ref, lnb_ref, wo_ref, bo_ref,
                        out_ref, hist_ref, y_ref, *, tm):
    i = pl.program_id(1)

    @pl.when(i == 0)
    def _():
        hist_ref[0:HALO * CH_TILES, :] = jnp.zeros((HALO * CH_TILES, LANES), _F32)

    @pl.when(i > 0)
    def _():
        hist_ref[0:HALO * CH_TILES, :] = hist_ref[tm * CH_TILES:(tm + HALO) * CH_TILES, :]

    for c in range(CH_TILES):
        hist_ref[pl.ds(HALO * CH_TILES + c, tm, stride=CH_TILES), :] = glu_ref[:, c * LANES:(c + 1) * LANES]

    first = HALO - (CONV_WIDTH - 1)
    span = CONV_STEPS + CONV_WIDTH - 1

    def chunk(ci, carry):
        t0 = ci * CONV_STEPS
        x = hist_ref[pl.ds(pl.multiple_of((t0 + first) * CH_TILES, CH_TILES), span * CH_TILES), :]
        x = x.reshape(span, CH_TILES, LANES)
        acc = jnp.broadcast_to(b8_ref[...][None], (CONV_STEPS, CH_TILES, LANES))
        for j in range(CONV_WIDTH):
            acc = acc + x[j:j + CONV_STEPS] * w8_ref[j][None]
        y_ref[pl.ds(pl.multiple_of(t0 * CH_TILES, CH_TILES), CONV_STEPS * CH_TILES), :] = (
            acc.reshape(CONV_STEPS * CH_TILES, LANES))
        return carry

    lax.fori_loop(0, tm // CONV_STEPS, chunk, 0)
    y = jnp.concatenate([y_ref[pl.ds(c, tm, stride=CH_TILES), :] for c in range(CH_TILES)], axis=1)
    out_ref[...] = _ln_swish_project(y, lng_ref, lnb_ref, wo_ref, bo_ref)


def _conv_prompt(glu, bsz, t, w_dw, b_dw, lng, lnb, wo_b, bo, tm):
    assert CH_TILES == SUBLANES
    nt = t // tm
    w8 = w_dw.reshape(CONV_WIDTH, CH_TILES, LANES)
    b8 = b_dw.reshape(CH_TILES, LANES)
    full = _resident
    row = pl.BlockSpec((tm, D_CONV), lambda b, i: (b * nt + i, 0))
    return pl.pallas_call(
        functools.partial(_conv_prompt_kernel, tm=tm),
        grid=(bsz, nt),
        in_specs=[row, full(w8), full(b8), full(lng), full(lnb), full(wo_b), full(bo)],
        out_specs=pl.BlockSpec((tm, D_MODEL), lambda b, i: (b * nt + i, 0)),
        out_shape=jax.ShapeDtypeStruct((bsz * t, D_MODEL), _BF16),
        scratch_shapes=[pltpu.VMEM(((tm + HALO) * CH_TILES, LANES), _F32), pltpu.VMEM((tm * CH_TILES, LANES), _F32)],
        compiler_params=pltpu.CompilerParams(dimension_semantics=("arbitrary", "arbitrary"),
                                             vmem_limit_bytes=VMEM_LIMIT),
        name="conv_prompt",
    )(glu, w8, b8, lng, lnb, wo_b, bo)


PIECE = 256


def _inproj_conv_kernel(x_ref, g_ref, w_ref, qg_ref, kg_ref, w8_ref, b8_ref, lng_ref, lnb_ref, wo_ref, bo_ref,
                        q_ref, k_ref, v_ref, ga_ref, gb_ref, conv_ref, tail_ref, hist_ref, y_ref, *, tm):
    i = pl.program_id(1)

    @pl.when(i == 0)
    def _():
        hist_ref[0:HALO * CH_TILES, :] = jnp.zeros((HALO * CH_TILES, LANES), _F32)

    @pl.when(i > 0)
    def _():
        hist_ref[0:HALO * CH_TILES, :] = hist_ref[tm * CH_TILES:(tm + HALO) * CH_TILES, :]

    x = x_ref[...]
    xn = x * lax.rsqrt(jnp.mean(x * x, axis=-1, keepdims=True) + EPS) * g_ref[...]
    xb = xn.astype(_BF16)

    def seg(lo, width):
        return _dot(xb, w_ref[:, lo:lo + width])

    for p in range(D_CONV // PIECE):
        lo = p * PIECE
        glu = seg(lo, PIECE) * jax.nn.sigmoid(seg(D_CONV + lo, PIECE))
        tail_ref[0, :, lo:lo + PIECE] = glu[tm - HALO:, :]
        for c in range(PIECE // LANES):
            tile = lo // LANES + c
            hist_ref[pl.ds(HALO * CH_TILES + tile, tm, stride=CH_TILES), :] = glu[:, c * LANES:(c + 1) * LANES]

    def conv_chunk(ci):
        t0 = ci * CONV_STEPS
        xw = hist_ref[(t0 + first) * CH_TILES:(t0 + first + span) * CH_TILES, :].reshape(span, CH_TILES, LANES)
        acc = jnp.broadcast_to(b8_ref[...][None], (CONV_STEPS, CH_TILES, LANES))
        for j in range(CONV_WIDTH):
            acc = acc + xw[j:j + CONV_STEPS] * w8_ref[j][None]
        y_ref[t0 * CH_TILES:(t0 + CONV_STEPS) * CH_TILES, :] = acc.reshape(CONV_STEPS * CH_TILES, LANES)

    def q_piece(lo):
        q = seg(2 * D_CONV + lo, PIECE)
        q_ref[:, lo:lo + PIECE] = (q * _head_rms_scale(q) * qg_ref[:, lo:lo + PIECE]).astype(q_ref.dtype)

    def k_piece(lo):
        k = seg(2 * D_CONV + Q_DIM + lo, PIECE)
        k_ref[:, lo:lo + PIECE] = k * _head_rms_scale(k) * kg_ref[:, lo:lo + PIECE]

    def v_piece(lo):
        v_ref[:, lo:lo + PIECE] = seg(2 * D_CONV + Q_DIM + KV_DIM + lo, PIECE)

    def gate_piece(ref, base, lo):
        ref[:, lo:lo + PIECE] = jax.nn.sigmoid(seg(base + lo, PIECE)).astype(ref.dtype)

    gate_base = 2 * D_CONV + Q_DIM + 2 * KV_DIM
    pieces = ([functools.partial(q_piece, lo) for lo in range(0, Q_DIM, PIECE)]
              + [functools.partial(k_piece, lo) for lo in range(0, KV_DIM, PIECE)]
              + [functools.partial(v_piece, lo) for lo in range(0, KV_DIM, PIECE)]
              + [functools.partial(gate_piece, ga_ref, gate_base, lo) for lo in range(0, D_MODEL, PIECE)]
              + [functools.partial(gate_piece, gb_ref, gate_base + D_MODEL, lo) for lo in range(0, D_MODEL, PIECE)])

    first = HALO - (CONV_WIDTH - 1)
    span = CONV_STEPS + CONV_WIDTH - 1
    n_chunks = tm // CONV_STEPS
    done = 0
    for idx, piece in enumerate(pieces):
        piece()
        upto = (idx + 1) * n_chunks // len(pieces)
        for ci in range(done, upto):
            conv_chunk(ci)
        done = upto

    y = jnp.concatenate([y_ref[pl.ds(c, tm, stride=CH_TILES), :] for c in range(CH_TILES)], axis=1)
    conv_ref[...] = _ln_swish_project(y, lng_ref, lnb_ref, wo_ref, bo_ref)


def _inproj_conv(x, bsz, t, g, w_in_b, qg, kg, w_dw, b_dw, lng, lnb, wo_b, bo, tm):
    assert CH_TILES == SUBLANES
    nt = t // tm
    n = bsz * t
    w8 = w_dw.reshape(CONV_WIDTH, CH_TILES, LANES)
    b8 = b_dw.reshape(CH_TILES, LANES)
    full = _resident
    row = lambda w: pl.BlockSpec((tm, w), lambda b, i: (b * nt + i, 0))
    return pl.pallas_call(
        functools.partial(_inproj_conv_kernel, tm=tm),
        grid=(bsz, nt),
        in_specs=[row(D_MODEL), full(g), full(w_in_b), full(qg), full(kg), full(w8), full(b8), full(lng), full(lnb),
                  full(wo_b), full(bo)],
        out_specs=[row(Q_DIM), row(KV_DIM), row(KV_DIM), row(D_MODEL), row(D_MODEL), row(D_MODEL),
                   pl.BlockSpec((1, HALO, D_CONV), lambda b, i: (b, 0, 0))],
        out_shape=[jax.ShapeDtypeStruct((n, Q_DIM), _BF16),
                   jax.ShapeDtypeStruct((n, KV_DIM), _F32),
                   jax.ShapeDtypeStruct((n, KV_DIM), _F32),
                   jax.ShapeDtypeStruct((n, D_MODEL), _BF16),
                   jax.ShapeDtypeStruct((n, D_MODEL), _BF16),
                   jax.ShapeDtypeStruct((n, D_MODEL), _BF16),
                   jax.ShapeDtypeStruct((bsz, HALO, D_CONV), _F32)],
        scratch_shapes=[pltpu.VMEM(((tm + HALO) * CH_TILES, LANES), _F32), pltpu.VMEM((tm * CH_TILES, LANES), _F32)],
        compiler_params=pltpu.CompilerParams(dimension_semantics=("arbitrary", "arbitrary"),
                                             vmem_limit_bytes=VMEM_LIMIT),
        name="inproj_conv",
    )(x, g, w_in_b, qg, kg, w8, b8, lng, lnb, wo_b, bo)


def _conv_sample_kernel(state_ref, glu_ref, wdw_ref, bdw_ref, lng_ref, lnb_ref, wo_ref, bo_ref,
                        out_ref, state_out_ref):
    keep, steps = state_ref.shape[0], glu_ref.shape[0]

    def hist(u):
        return state_ref[u] if u < keep else glu_ref[u - keep]

    for t in range(steps):
        acc = hist(t) * wdw_ref[0:1, :]
        for j in range(1, CONV_WIDTH):
            acc = acc + hist(t + j) * wdw_ref[j:j + 1, :]
        out_ref[t] = _ln_swish_project(acc + bdw_ref[...], lng_ref, lnb_ref, wo_ref, bo_ref)
    state_out_ref[0:keep - steps] = state_ref[steps:keep]
    state_out_ref[keep - steps:keep] = glu_ref[...]


def _conv_sample(state_t, glu_t, w_dw, b_dw, lng, lnb, wo_b, bo, sb):
    keep, nseq, _ = state_t.shape
    steps = glu_t.shape[0]
    full = _resident
    blk = lambda r: pl.BlockSpec((r, sb, D_CONV), lambda i: (0, i, 0))
    return pl.pallas_call(
        _conv_sample_kernel,
        grid=(nseq // sb,),
        in_specs=[blk(keep), blk(steps), full(w_dw), full(b_dw), full(lng), full(lnb), full(wo_b), full(bo)],
        out_specs=[blk(steps), blk(keep)],
        out_shape=[jax.ShapeDtypeStruct((steps, nseq, D_MODEL), _BF16),
                   jax.ShapeDtypeStruct(state_t.shape, _F32)],
        compiler_params=pltpu.CompilerParams(dimension_semantics=("arbitrary",),
                                             vmem_limit_bytes=VMEM_LIMIT),
        name="conv_sample",
    )(state_t, glu_t, w_dw, b_dw, lng, lnb, wo_b, bo)


def _bucket_map():
    i = np.arange(WINDOW)[:, None]
    j = np.arange(WINDOW)[None, :]
    n = (i - j) % WINDOW
    nf = np.maximum(n, 1).astype(np.float32)
    large = MAX_EXACT + (np.log(nf / np.float32(MAX_EXACT)) / np.float32(math.log(MAX_DISTANCE / MAX_EXACT))
                         * np.float32(N_BUCKETS - MAX_EXACT)).astype(np.int32)
    return np.where(n < MAX_EXACT, n, np.minimum(large, N_BUCKETS - 1)).astype(np.int32)


def _bias_table_kernel(rb_ref, bm_ref, tbl_ref):
    p = pl.program_id(0)
    bm = bm_ref[...]
    for half in range(2):
        h = 2 * p + half
        t = jnp.zeros(bm.shape, _F32)
        for b in range(N_BUCKETS):
            t = jnp.where(bm == b, rb_ref[b, h], t)
        tbl_ref[0, :, half * WINDOW:(half + 1) * WINDOW] = t


def _bias_tables(rel_bias):
    bm = jnp.asarray(_bucket_map())
    return pl.pallas_call(
        _bias_table_kernel,
        grid=(N_PAIRS,),
        in_specs=[pl.BlockSpec(memory_space=pltpu.SMEM), pl.BlockSpec(bm.shape, lambda p: (0, 0))],
        out_specs=pl.BlockSpec((1, WINDOW, 2 * WINDOW), lambda p: (p, 0, 0)),
        out_shape=jax.ShapeDtypeStruct((N_PAIRS, WINDOW, 2 * WINDOW), _F32),
        name="bias_tables",
    )(rel_bias, bm)


def _block_diag_pairs(slab):
    low = lax.broadcasted_iota(jnp.int32, slab.shape, 1) < HEAD_DIM
    swapped = pltpu.roll(slab, HEAD_DIM, axis=1)
    zero = jnp.zeros_like(slab)
    first = jnp.concatenate([jnp.where(low, slab, zero), jnp.where(low, zero, swapped)], axis=0)
    second = jnp.concatenate([jnp.where(low, swapped, zero), jnp.where(low, zero, slab)], axis=0)
    return first.astype(_BF16), second.astype(_BF16)


def _kv_operands(k_blk, v_blk):
    ops = []
    for slab in range(KV_DIM // LANES):
        cols = slice(slab * LANES, (slab + 1) * LANES)
        ops.extend(zip(_block_diag_pairs(k_blk[:, cols]), _block_diag_pairs(v_blk[:, cols])))
    return ops


def _attend(q, prev_ops, own_ops, tbl_ref, sink_ref, prev_shift, store, transposed=False):
    tq = q.shape[0]
    rows = 2 * tq
    row = lax.broadcasted_iota(jnp.int32, (rows, 2 * WINDOW), 0)
    col = lax.broadcasted_iota(jnp.int32, (rows, 2 * WINDOW), 1)
    from_prev = (col & (WINDOW - 1)) > jnp.where(row >= tq, row - tq, row)
    top = lax.broadcasted_iota(jnp.int32, (rows, 1), 0) < tq
    low = lax.broadcasted_iota(jnp.int32, (rows, LANES), 1) < HEAD_DIM
    contract_last = (((1,), (1,)), ((), ()))

    def logits(a, k_op):
        return _dot(a, k_op) if transposed else lax.dot_general(a, k_op, contract_last, preferred_element_type=_F32)

    def weighted_values(pr, v_op):
        return lax.dot_general(pr, v_op, contract_last, preferred_element_type=_F32) if transposed else _dot(pr, v_op)

    for kvh in range(N_KV_HEADS):
        (k_prev, v_prev), (k_own, v_own) = prev_ops[kvh], own_ops[kvh]
        pair_a = 2 * kvh
        pair_b = pair_a + 1
        qq = jnp.concatenate([q[:, pair_a * LANES:(pair_a + 1) * LANES],
                              q[:, pair_b * LANES:(pair_b + 1) * LANES]], axis=0).astype(_BF16)
        sp = logits(qq, k_prev)
        so = logits(qq, k_own)
        bias = jnp.concatenate([tbl_ref[pair_a, 0:tq, :], tbl_ref[pair_b, 0:tq, :]], axis=0)
        s = jnp.where(from_prev, sp + prev_shift, so) + bias
        sink_even = jnp.where(top, sink_ref[2 * pair_a], sink_ref[2 * pair_b])
        sink_odd = jnp.where(top, sink_ref[2 * pair_a + 1], sink_ref[2 * pair_b + 1])
        m_even = jnp.maximum(jnp.max(s[:, :WINDOW], axis=-1, keepdims=True), sink_even)
        m_odd = jnp.maximum(jnp.max(s[:, WINDOW:], axis=-1, keepdims=True), sink_odd)
        p = jnp.exp(s - jnp.where(col < WINDOW, m_even, m_odd)).astype(_BF16)
        zero = jnp.zeros_like(p)
        o = (weighted_values(jnp.where(from_prev, p, zero), v_prev)
             + weighted_values(jnp.where(from_prev, zero, p), v_own))
        pf = p.astype(_F32)
        sums = jnp.where(low, jnp.sum(pf[:, :WINDOW], axis=-1, keepdims=True),
                         jnp.sum(pf[:, WINDOW:], axis=-1, keepdims=True))
        den = sums + jnp.where(low, jnp.exp(sink_even - m_even), jnp.exp(sink_odd - m_odd))
        o = o / den
        store(pair_a, o[:tq])
        store(pair_b, o[tq:])


PROMPT_QBLOCKS = 4


def _attn_prompt_kernel(sink_ref, q_ref, kp_ref, ko_ref, vp_ref, vo_ref, tbl_ref, o_ref):
    prev_shift = jnp.where(pl.program_id(1) == 0, MASK_VALUE, 0.0).astype(_F32)
    ops = [_kv_operands(kp_ref[...], vp_ref[...])]
    for b in range(PROMPT_QBLOCKS):
        rows = slice(b * WINDOW, (b + 1) * WINDOW)
        ops.append(_kv_operands(ko_ref[rows, :], vo_ref[rows, :]))

        def store(pair, o, rows=rows):
            o_ref[rows, pair * LANES:(pair + 1) * LANES] = o.astype(o_ref.dtype)

        _attend(q_ref[rows, :], ops[b], ops[b + 1], tbl_ref, sink_ref,
                prev_shift if b == 0 else jnp.float32(0.0), store)


def _attn_prompt(q, k, v, tbl, sinks, bsz, t):
    tq = PROMPT_QBLOCKS * WINDOW
    nb = t // tq
    own = lambda w: pl.BlockSpec((tq, w), lambda b, i: (b * nb + i, 0))
    prev = lambda w: pl.BlockSpec((WINDOW, w),
                                  lambda b, i: (PROMPT_QBLOCKS * (b * nb + i) - jnp.minimum(i, 1), 0))
    return pl.pallas_call(
        _attn_prompt_kernel,
        grid=(bsz, nb),
        in_specs=[pl.BlockSpec(memory_space=pltpu.SMEM), own(Q_DIM), prev(KV_DIM), own(KV_DIM),
                  prev(KV_DIM), own(KV_DIM), pl.BlockSpec(tbl.shape, lambda b, i: (0, 0, 0))],
        out_specs=own(Q_DIM),
        out_shape=jax.ShapeDtypeStruct((bsz * t, Q_DIM), _BF16),
        compiler_params=pltpu.CompilerParams(dimension_semantics=("arbitrary", "arbitrary"),
                                             vmem_limit_bytes=VMEM_LIMIT),
        name="attn_prompt",
    )(sinks, q, k, k, v, v, tbl)


SAMPLE_UNROLL = 2


def _block_diag_t(x):
    xb = x.astype(_BF16)
    z = jnp.zeros_like(xb)
    return jnp.concatenate([jnp.concatenate([xb, z], axis=1), jnp.concatenate([z, xb], axis=1)], axis=0)


def _attn_sample_kernel(sink_ref, q_ref, kn_ref, vn_ref, ck_ref, cv_ref, tbl_ref, o_ref, cko_ref, cvo_ref,
                        *, sb, steps):
    pad = jnp.zeros((WINDOW - steps, LANES), _F32)
    lane = lax.broadcasted_iota(jnp.int32, (HEAD_DIM, WINDOW), 1)

    def one_sequence(s, carry):
        def store(pair, o):
            o_ref[s, :, pair * LANES:(pair + 1) * LANES] = o

        prev_ops, own_ops = [], []
        for slab in range(KV_DIM // LANES):
            cols = slice(slab * LANES, (slab + 1) * LANES)
            new_k = jnp.concatenate([kn_ref[s][:, cols], pad], axis=0).T
            new_v = jnp.concatenate([vn_ref[s][:, cols], pad], axis=0).T
            for sub in range(2):
                kvh = 2 * slab + sub
                part = slice(sub * HEAD_DIM, (sub + 1) * HEAD_DIM)
                kt, vt = ck_ref[s, kvh], cv_ref[s, kvh]
                cko_ref[s, kvh] = pltpu.roll(jnp.where(lane < steps, new_k[part], kt), WINDOW - steps, axis=1)
                cvo_ref[s, kvh] = pltpu.roll(jnp.where(lane < steps, new_v[part], vt), WINDOW - steps, axis=1)
                prev_ops.append((_block_diag_t(kt), _block_diag_t(vt)))
                own_ops.append((_block_diag_t(new_k[part]), _block_diag_t(new_v[part])))
        _attend(q_ref[s], prev_ops, own_ops, tbl_ref, sink_ref, jnp.float32(0.0), store, transposed=True)
        return carry

    lax.fori_loop(0, sb, one_sequence, 0, unroll=SAMPLE_UNROLL)


def _attn_sample(q, k_new, v_new, cache_kt, cache_vt, tbl, sinks, sb):
    nseq, steps, _ = q.shape
    seq = lambda r, w: pl.BlockSpec((sb, r, w), lambda i: (i, 0, 0))
    cache = pl.BlockSpec((sb, N_KV_HEADS, HEAD_DIM, WINDOW), lambda i: (i, 0, 0, 0))
    return pl.pallas_call(
        functools.partial(_attn_sample_kernel, sb=sb, steps=steps),
        grid=(nseq // sb,),
        in_specs=[pl.BlockSpec(memory_space=pltpu.SMEM), seq(steps, Q_DIM), seq(steps, KV_DIM), seq(steps, KV_DIM),
                  cache, cache, pl.BlockSpec(tbl.shape, lambda i: (0, 0, 0))],
        out_specs=[seq(steps, Q_DIM), cache, cache],
        out_shape=[jax.ShapeDtypeStruct((nseq, steps, Q_DIM), _F32),
                   jax.ShapeDtypeStruct(cache_kt.shape, _F32), jax.ShapeDtypeStruct(cache_vt.shape, _F32)],
        compiler_params=pltpu.CompilerParams(dimension_semantics=("arbitrary",),
                                             vmem_limit_bytes=VMEM_LIMIT),
        name="attn_sample",
    )(sinks, q, k_new, v_new, cache_kt, cache_vt, tbl)


def _lane_min_index(mask, lane):
    return jnp.min(jnp.where(mask, lane, float(LANES)), axis=-1, keepdims=True)


def _finish_kernel(x_ref, conv_ref, o_ref, ga_ref, gb_ref, wa_ref, wo_ref, ng_ref, wr_hi_ref, wr_lo_ref, br_ref,
                   tri_ref, h_ref, hn_ref, route_ref, count_ref, running_ref):
    @pl.when(pl.program_id(0) == 0)
    def _():
        running_ref[...] = jnp.zeros_like(running_ref)

    attn_out = _dot(o_ref[...].astype(_BF16), wa_ref[...])
    merged = ga_ref[...].astype(_F32) * conv_ref[...].astype(_F32) + gb_ref[...].astype(_F32) * attn_out
    h = x_ref[...] + _dot(merged.astype(_BF16), wo_ref[...])
    h_ref[...] = h
    hn = h * lax.rsqrt(jnp.mean(h * h, axis=-1, keepdims=True) + EPS) * ng_ref[...]
    hn_ref[...] = _pack_bf16_pairs(hn)

    hi, lo = _split_bf16(hn)
    logits = _dot(hi, wr_hi_ref[...]) + _dot(lo, wr_hi_ref[...]) + _dot(hi, wr_lo_ref[...]) + br_ref[...]
    lane = lax.broadcasted_iota(jnp.int32, logits.shape, 1).astype(_F32)
    gmask = lane < N_GROUPS
    gl = jnp.where(gmask, logits, MASK_VALUE)
    gmax = jnp.max(gl, axis=-1, keepdims=True)
    grp = _lane_min_index(gmask & (gl == gmax), lane)
    p_grp = 1.0 / jnp.sum(jnp.where(gmask, jnp.exp(gl - gmax), 0.0), axis=-1, keepdims=True)
    e_lo = N_GROUPS + grp * EXPERTS_PER_GROUP
    emask = (lane >= e_lo) & (lane < e_lo + EXPERTS_PER_GROUP)
    el = jnp.where(emask, logits, MASK_VALUE)
    ex = jnp.where(emask, jnp.exp(el - jnp.max(el, axis=-1, keepdims=True)), 0.0)
    prob = jnp.where(emask, ex / jnp.sum(ex, axis=-1, keepdims=True), -1.0)
    p1 = jnp.max(prob, axis=-1, keepdims=True)
    i1 = _lane_min_index(prob == p1, lane)
    rest = jnp.where(lane == i1, -1.0, prob)
    p2 = jnp.max(rest, axis=-1, keepdims=True)
    i2 = _lane_min_index(rest == p2, lane)
    w1 = p_grp * p1 / (p1 + p2)
    w2 = p_grp * p2 / (p1 + p2)
    e1 = i1 - N_GROUPS
    e2 = i2 - N_GROUPS

    hot1 = lane == e1
    hot2 = lane == e2
    hot = jnp.where(hot1 | hot2, 1.0, 0.0)
    before = _dot(tri_ref[...], hot.astype(_BF16)) + running_ref[...]
    rank1 = jnp.sum(jnp.where(hot1, before, 0.0), axis=-1, keepdims=True)
    rank2 = jnp.sum(jnp.where(hot2, before, 0.0), axis=-1, keepdims=True)
    running_ref[...] += jnp.sum(hot, axis=0, keepdims=True)
    count_ref[...] = jnp.broadcast_to(running_ref[...], count_ref.shape)

    fields = (e1, e2, w1, w2, rank1, rank2)
    route = jnp.zeros(logits.shape, _F32)
    for pos, val in enumerate(fields):
        route = jnp.where(lane == pos, val, route)
    route_ref[...] = route


ROUTE_E, ROUTE_W, ROUTE_RANK = 0, 2, 4
DEST_ROWS = 1024


def _finish(x, conv_out, o, ga, gb, wa_b, wo_b, ng, wr_hi, wr_lo, br, tm):
    n = x.shape[0]
    tri = jnp.asarray(np.tril(np.ones((tm, tm), np.float32), -1), _BF16)
    row = lambda w: pl.BlockSpec((tm, w), lambda i: (i, 0))
    full = _resident
    return pl.pallas_call(
        _finish_kernel,
        grid=(n // tm,),
        in_specs=[row(D_MODEL), row(D_MODEL), row(Q_DIM), row(D_MODEL), row(D_MODEL),
                  full(wa_b), full(wo_b), full(ng), full(wr_hi), full(wr_lo), full(br), full(tri)],
        out_specs=[row(D_MODEL), row(HALF), row(LANES), pl.BlockSpec((SUBLANES, LANES), lambda i: (0, 0))],
        out_shape=[jax.ShapeDtypeStruct((n, D_MODEL), _F32),
                   jax.ShapeDtypeStruct((n, HALF), jnp.uint32),
                   jax.ShapeDtypeStruct((n, LANES), _F32),
                   jax.ShapeDtypeStruct((SUBLANES, LANES), _F32)],
        scratch_shapes=[pltpu.VMEM((1, LANES), _F32)],
        compiler_params=pltpu.CompilerParams(dimension_semantics=("arbitrary",),
                                             vmem_limit_bytes=VMEM_LIMIT),
        name="finish",
    )(x, conv_out, o, ga, gb, wa_b, wo_b, ng, wr_hi, wr_lo, br, tri)


def _dest_kernel(route_ref, starts_ref, dest_ref):
    route = route_ref[...]
    lane = lax.broadcasted_iota(jnp.int32, route.shape, 1)
    out = jnp.zeros(route.shape, jnp.int32)
    for j in range(TOP_K):
        e = route[:, ROUTE_E + j:ROUTE_E + j + 1].astype(jnp.int32)
        start = jnp.sum(jnp.where(lane == e, starts_ref[...], 0.0), axis=-1, keepdims=True)
        d = (start + route[:, ROUTE_RANK + j:ROUTE_RANK + j + 1]).astype(jnp.int32)
        out = jnp.where(lane == j, d, out)
    dest_ref[...] = out


def _dest(route, starts_row, tm):
    n = route.shape[0]
    row = pl.BlockSpec((tm, LANES), lambda i: (i, 0))
    return pl.pallas_call(
        _dest_kernel,
        grid=(n // tm,),
        in_specs=[row, pl.BlockSpec((1, LANES), lambda i: (0, 0))],
        out_specs=row,
        out_shape=jax.ShapeDtypeStruct((n, LANES), jnp.int32),
        name="dest",
    )(route, starts_row)


SC_CORES = 2
SC_SUBCORES = 16
SC_WORKERS = SC_CORES * SC_SUBCORES
SC_IN_FLIGHT = 4
SC_CHUNK_BYTES = 64 * 1024


def _sc_move_rows(src, idx, gather):
    n, d = src.shape
    b = idx.shape[0]
    per_worker = b // SC_WORKERS
    assert per_worker * SC_WORKERS == b and (gather or n % per_worker == 0), (b, n)
    chunk = min(per_worker // SC_IN_FLIGHT, SC_CHUNK_BYTES // (d * 4))
    n_iters = per_worker // (chunk * SC_IN_FLIGHT)
    assert n_iters * chunk * SC_IN_FLIGHT == per_worker and chunk % SUBLANES == 0, (per_worker, chunk)
    mesh = plsc.VectorSubcoreMesh(core_axis_name="c", subcore_axis_name="s",
                                  num_cores=SC_CORES, num_subcores=SC_SUBCORES)
    scratch = ([pltpu.VMEM((chunk,), jnp.int32)] * SC_IN_FLIGHT + [pltpu.VMEM((chunk, d), src.dtype)] * SC_IN_FLIGHT
               + [pltpu.SemaphoreType.DMA] * SC_IN_FLIGHT)

    @functools.partial(pl.kernel, mesh=mesh, out_type=jax.ShapeDtypeStruct((b, d), src.dtype),
                       scratch_types=scratch, name="sc_gather_rows" if gather else "sc_scatter_rows",
                       cost_estimate=pl.CostEstimate(flops=0, transcendentals=0, bytes_accessed=2 * b * d * 4 + b * 4))
    def move(src_hbm, idx_hbm, out_hbm, *bufs):
        idx_v = bufs[:SC_IN_FLIGHT]
        rows_v = bufs[SC_IN_FLIGHT:2 * SC_IN_FLIGHT]
        sems = bufs[2 * SC_IN_FLIGHT:]
        worker = lax.axis_index("s") * SC_CORES + lax.axis_index("c")

        @pl.loop(0, n_iters)
        def _(it):
            bases = [pl.multiple_of(worker * per_worker + (it * SC_IN_FLIGHT + j) * chunk, chunk)
                     for j in range(SC_IN_FLIGHT)]
            loads = [pltpu.async_copy(idx_hbm.at[pl.ds(bases[j], chunk)], idx_v[j], sems[j])
                     for j in range(SC_IN_FLIGHT)]
            reads = []
            for j in range(SC_IN_FLIGHT):
                loads[j].wait()
                if gather:
                    rows = src_hbm.at[idx_v[j]]
                else:
                    rows = src_hbm.at[pl.ds(pl.multiple_of(lax.rem(bases[j], n), chunk), chunk)]
                reads.append(pltpu.async_copy(rows, rows_v[j], sems[j]))
            writes = []
            for j in range(SC_IN_FLIGHT):
                reads[j].wait()
                dst = out_hbm.at[pl.ds(bases[j], chunk)] if gather else out_hbm.at[idx_v[j]]
                writes.append(pltpu.async_copy(rows_v[j], dst, sems[j]))
            for w in writes:
                w.wait()

    return move(src, idx)


def _sc_gather_rows(table, idx):
    return _sc_move_rows(table, idx, gather=True)


def _sc_scatter_rows(src, idx):
    return _sc_move_rows(src, idx, gather=False)


def _expert_kernel(blk_ref, exp_ref, lo_ref, hi_ref, x_ref, wg_ref, wu_ref, wd_ref, yb_ref,
                   wg_b, wu_b, wd_b, held_ref):
    del blk_ref
    k = pl.program_id(0)
    lo, hi, e = lo_ref[k], hi_ref[k], exp_ref[k]

    @pl.when(k == 0)
    def _():
        held_ref[0] = -1

    @pl.when(hi > lo)
    def _():
        @pl.when(held_ref[0] != e)
        def _():
            wg_b[...] = wg_ref[0].astype(_BF16)
            wu_b[...] = wu_ref[0].astype(_BF16)
            wd_b[...] = wd_ref[0].astype(_BF16)
            held_ref[0] = e

        xb = _unpack_bf16_pairs(x_ref[...]).astype(_BF16)
        g = _dot(xb, wg_b[...])
        u = _dot(xb, wu_b[...])
        hid = g * jax.nn.sigmoid(g) * u
        y = _dot(hid.astype(_BF16), wd_b[...])
        r = lax.broadcasted_iota(jnp.int32, yb_ref.shape, 0)
        pltpu.store(yb_ref, _pack_bf16_pairs(y), mask=(r >= lo) & (r < hi))


def _experts(items, xs, w_gate, w_up, w_down):
    n_items = items[0].shape[0]
    wspec = lambda a: pl.BlockSpec((1,) + a.shape[1:], lambda k, blk, exp, lo, hi: (exp[k], 0, 0))
    rows = pl.BlockSpec((MOE_ROWS, HALF), lambda k, blk, exp, lo, hi: (blk[k], 0))
    grid_spec = pltpu.PrefetchScalarGridSpec(
        num_scalar_prefetch=4,
        grid=(n_items,),
        in_specs=[rows, wspec(w_gate), wspec(w_up), wspec(w_down)],
        out_specs=rows,
        scratch_shapes=[pltpu.VMEM(w_gate.shape[1:], _BF16), pltpu.VMEM(w_up.shape[1:], _BF16),
                        pltpu.VMEM(w_down.shape[1:], _BF16), pltpu.SMEM((1,), jnp.int32)],
    )
    return pl.pallas_call(
        _expert_kernel,
        grid_spec=grid_spec,
        out_shape=jax.ShapeDtypeStruct(xs.shape, xs.dtype),
        compiler_params=pltpu.CompilerParams(dimension_semantics=("arbitrary",),
                                             vmem_limit_bytes=VMEM_LIMIT),
        cost_estimate=pl.CostEstimate(
            flops=n_items * MOE_ROWS * 6 * D_MODEL * D_EXPERT, transcendentals=n_items * MOE_ROWS * D_EXPERT,
            bytes_accessed=2 * xs.size * 4 + (w_gate.size + w_up.size + w_down.size) * 4),
        name="experts",
    )(*items, xs, w_gate, w_up, w_down)


def _combine_kernel(h_ref, route_ref, g0_ref, g1_ref, y_ref):
    route = route_ref[...]
    y_ref[...] = (h_ref[...] + route[:, ROUTE_W:ROUTE_W + 1] * _unpack_bf16_pairs(g0_ref[...])
                  + route[:, ROUTE_W + 1:ROUTE_W + 2] * _unpack_bf16_pairs(g1_ref[...]))


def _combine(h, route, g, tm):
    n = h.shape[0]
    nt = n // tm
    row = lambda w: pl.BlockSpec((tm, w), lambda i: (i, 0))
    return pl.pallas_call(
        _combine_kernel,
        grid=(nt,),
        in_specs=[row(D_MODEL), row(LANES), row(HALF), pl.BlockSpec((tm, HALF), lambda i: (nt + i, 0))],
        out_specs=row(D_MODEL),
        out_shape=jax.ShapeDtypeStruct((n, D_MODEL), _F32),
        compiler_params=pltpu.CompilerParams(dimension_semantics=("arbitrary",),
                                             vmem_limit_bytes=VMEM_LIMIT),
        name="combine",
    )(h, route, g, g)


def _work_items(counts, n_pairs):
    n_blocks = n_pairs // MOE_ROWS
    starts = jnp.cumsum(counts) - counts
    cuts = jnp.sort(jnp.concatenate([jnp.arange(n_blocks, dtype=jnp.int32) * MOE_ROWS, starts]))
    ends = jnp.concatenate([cuts[1:], jnp.full((1,), n_pairs, jnp.int32)])
    blk = jnp.minimum(cuts // MOE_ROWS, n_blocks - 1)
    expert = jnp.clip(jnp.sum(starts[None, :] <= cuts[:, None], axis=1) - 1, 0, N_EXPERTS - 1).astype(jnp.int32)
    return starts, (blk, expert, cuts - blk * MOE_ROWS, ends - blk * MOE_ROWS)


def _dispatch(hn, route, counts_rows):
    n = hn.shape[0]
    n_pairs = n * TOP_K
    counts = counts_rows[0, :N_EXPERTS].astype(jnp.int32)
    starts, items = _work_items(counts, n_pairs)
    starts_row = jnp.zeros((1, LANES), _F32).at[0, :N_EXPERTS].set(starts.astype(_F32))
    dest = _dest(route, starts_row, min(n, DEST_ROWS))[:, :TOP_K].T.reshape(n_pairs)
    return _sc_scatter_rows(hn, dest), dest, items


def _after(value, *earlier):
    return lax.optimization_barrier((value,) + earlier)[0]


def kernel(x_prompt, x_sample, state_conv, cache_k, cache_v, norm_attn_g, w_in, q_norm_g, k_norm_g, rel_bias, attn_sinks, w_dw, b_dw, conv_ln_g, conv_ln_b, w_conv_out, b_conv_out, w_attn_out, w_out, norm_ffn_g, w_grp, b_grp, w_router, b_router, w_gate, w_up, w_down):
    bsz, t, _ = x_prompt.shape
    nseq, steps, _ = x_sample.shape
    row = lambda a: a.reshape(1, -1).astype(_F32)

    w_in_b = w_in.astype(_BF16)
    wco_b = w_conv_out.astype(_BF16)
    wa_b = w_attn_out.astype(_BF16)
    wo_b = w_out.astype(_BF16)
    qg = row(jnp.tile(q_norm_g, N_HEADS)) * (HEAD_DIM ** -0.5)
    kg = row(jnp.tile(k_norm_g, N_KV_HEADS))
    w_rt = jnp.zeros((D_MODEL, LANES), _F32).at[:, :N_GROUPS].set(w_grp).at[:, N_GROUPS:N_GROUPS + N_EXPERTS].set(w_router)
    wr_hi = w_rt.astype(_BF16)
    wr_lo = (w_rt - wr_hi.astype(_F32)).astype(_BF16)
    b_rt = jnp.zeros((1, LANES), _F32).at[0, :N_GROUPS].set(b_grp).at[0, N_GROUPS:N_GROUPS + N_EXPERTS].set(b_router)
    tbl = _bias_tables(rel_bias)
    conv_params = (w_dw, row(b_dw), row(conv_ln_g), row(conv_ln_b), wco_b, row(b_conv_out))

    def finish(x2d, conv_out, o, ga, gb):
        return _finish(x2d, conv_out, o, ga, gb, wa_b, wo_b, row(norm_ffn_g), wr_hi, wr_lo, b_rt, ROW_TILE)

    xp = x_prompt.reshape(bsz * t, D_MODEL)
    q, k, v, ga, gb, conv_out, glu_tail = _inproj_conv(xp, bsz, t, row(norm_attn_g), w_in_b, qg, kg, *conv_params,
                                                       ROW_TILE)
    o = _attn_prompt(q, k, v, tbl, attn_sinks, bsz, t)
    h_p, hn_p, route_p, counts_p = finish(xp, conv_out, o, ga, gb)
    rows_p, dest_p, items_p = _dispatch(hn_p, route_p, counts_p)
    state_conv_prompt = glu_tail[:, HALO - (CONV_WIDTH - 1):]
    tail = lambda a: a.reshape(bsz, t, KV_DIM)[:, t - WINDOW:].reshape(bsz, WINDOW, N_KV_HEADS, HEAD_DIM)
    cache_k_prompt, cache_v_prompt = tail(k), tail(v)

    xs = _after(x_sample, dest_p).reshape(nseq * steps, D_MODEL)
    glu, q, k, v, ga, gb = _inproj(xs, row(norm_attn_g), w_in_b, qg, kg, _F32, ROW_TILE)
    glu_t = glu.reshape(nseq, steps, D_CONV).transpose(1, 0, 2)
    conv_out, state_t = _conv_sample(state_conv.transpose(1, 0, 2), glu_t, *conv_params, 64)
    conv_out = conv_out.transpose(1, 0, 2).reshape(nseq * steps, D_MODEL)
    k3 = k.reshape(nseq, steps, KV_DIM)
    v3 = v.reshape(nseq, steps, KV_DIM)
    o, ck_t, cv_t = _attn_sample(q.reshape(nseq, steps, Q_DIM), k3, v3, cache_k.transpose(0, 2, 3, 1),
                                 cache_v.transpose(0, 2, 3, 1), tbl, attn_sinks, 8)
    h_s, hn_s, route_s, counts_s = finish(xs, conv_out, o.reshape(nseq * steps, Q_DIM), ga, gb)
    rows_s, dest_s, items_s = _dispatch(hn_s, route_s, counts_s)
    state_conv_sample = state_t.transpose(1, 0, 2)
    cache_k_sample = ck_t.transpose(0, 3, 1, 2)
    cache_v_sample = cv_t.transpose(0, 3, 1, 2)

    yb_p = _experts(items_p, rows_p, w_gate, w_up, w_down)
    g_p = _sc_gather_rows(yb_p, dest_p)
    yb_s = _experts(items_s, _after(rows_s, yb_p), w_gate, w_up, w_down)
    g_s = _sc_gather_rows(yb_s, dest_s)
    y_prompt = _combine(h_p, route_p, g_p, ROW_TILE).reshape(bsz, t, D_MODEL)
    y_sample = _combine(h_s, route_s, _after(g_s, y_prompt), ROW_TILE).reshape(nseq, steps, D_MODEL)

    return (y_prompt, y_sample, state_conv_prompt, cache_k_prompt, cache_v_prompt,
            state_conv_sample, cache_k_sample, cache_v_sample)
```

```python
import functools
import math

import numpy as np
import jax
import jax.numpy as jnp
from jax import lax
from jax.experimental import pallas as pl
from jax.experimental.pallas import tpu as pltpu
from jax.experimental.pallas import tpu_sc as plsc

D_MODEL = 1024
N_HEADS = 16
HEAD_DIM = 64
N_KV_HEADS = 4
WINDOW = 128
Q_DIM = N_HEADS * HEAD_DIM
KV_DIM = N_KV_HEADS * HEAD_DIM
N_BUCKETS = 32
MAX_EXACT = N_BUCKETS // 2
MAX_DISTANCE = 128
D_CONV = D_MODEL
CONV_WIDTH = 31
N_GROUPS = 4
EXPERTS_PER_GROUP = 8
N_EXPERTS = N_GROUPS * EXPERTS_PER_GROUP
TOP_K = 2
D_EXPERT = 256
EPS = 1e-6

LANES = 128
SUBLANES = 8
N_PAIRS = N_HEADS // 2
MOE_ROWS = 512
MASK_VALUE = -1e30
VMEM_LIMIT = 56 * 1024 * 1024
ROW_TILE = 512

_F32 = jnp.float32
_BF16 = jnp.bfloat16


def _resident(a):
    return pl.BlockSpec(a.shape, lambda *_: (0,) * a.ndim, pipeline_mode=pl.Buffered(1))


def _dot(a, b):
    return jnp.dot(a, b, preferred_element_type=_F32)


def _split_bf16(x):
    hi = x.astype(_BF16)
    lo = (x - hi.astype(_F32)).astype(_BF16)
    return hi, lo


HALF = D_MODEL // 2


def _pack_pair(lo, hi):
    lo_bits = pltpu.bitcast(lo.astype(_BF16).astype(_F32), jnp.uint32)
    hi_bits = pltpu.bitcast(hi.astype(_BF16).astype(_F32), jnp.uint32)
    return hi_bits | (lo_bits >> 16)


def _pack_bf16_pairs(x):
    return _pack_pair(x[:, :HALF], x[:, HALF:])


def _unpack_bf16_pairs(w):
    lo = pltpu.bitcast(w << 16, _F32)
    hi = pltpu.bitcast(w & jnp.uint32(0xFFFF0000), _F32)
    return jnp.concatenate([lo, hi], axis=1)


def _head_rms_scale(z):
    low = lax.broadcasted_iota(jnp.int32, (z.shape[0], LANES), 1) < HEAD_DIM
    slabs = []
    for c in range(z.shape[1] // LANES):
        sq = z[:, c * LANES:(c + 1) * LANES]
        sq = sq * sq
        first = jnp.sum(jnp.where(low, sq, 0.0), axis=-1, keepdims=True)
        second = jnp.sum(jnp.where(low, 0.0, sq), axis=-1, keepdims=True)
        slabs.append(lax.rsqrt(jnp.where(low, first, second) * (1.0 / HEAD_DIM) + EPS))
    return jnp.concatenate(slabs, axis=1)


def _inproj_kernel(x_ref, g_ref, w_ref, qg_ref, kg_ref,
                   glu_ref, q_ref, k_ref, v_ref, ga_ref, gb_ref):
    x = x_ref[...]
    xn = x * lax.rsqrt(jnp.mean(x * x, axis=-1, keepdims=True) + EPS) * g_ref[...]
    xb = xn.astype(_BF16)

    def seg(lo, width):
        return _dot(xb, w_ref[:, lo:lo + width])

    a = seg(0, D_CONV)
    b = seg(D_CONV, D_CONV)
    glu_ref[...] = a * jax.nn.sigmoid(b)
    off = 2 * D_CONV
    q = seg(off, Q_DIM)
    q_ref[...] = (q * _head_rms_scale(q) * qg_ref[...]).astype(q_ref.dtype)
    off += Q_DIM
    k = seg(off, KV_DIM)
    k_ref[...] = k * _head_rms_scale(k) * kg_ref[...]
    off += KV_DIM
    v_ref[...] = seg(off, KV_DIM)
    off += KV_DIM
    ga_ref[...] = jax.nn.sigmoid(seg(off, D_MODEL)).astype(ga_ref.dtype)
    off += D_MODEL
    gb_ref[...] = jax.nn.sigmoid(seg(off, D_MODEL)).astype(gb_ref.dtype)


def _inproj(x, g, w_in_b, qg, kg, q_dtype, tm):
    n = x.shape[0]
    in_dim = w_in_b.shape[1]
    row = lambda w: pl.BlockSpec((tm, w), lambda i: (i, 0))
    full = _resident
    return pl.pallas_call(
        _inproj_kernel,
        grid=(n // tm,),
        in_specs=[row(D_MODEL), full(g), full(w_in_b), full(qg), full(kg)],
        out_specs=[row(D_CONV), row(Q_DIM), row(KV_DIM), row(KV_DIM), row(D_MODEL), row(D_MODEL)],
        out_shape=[jax.ShapeDtypeStruct((n, D_CONV), _F32),
                   jax.ShapeDtypeStruct((n, Q_DIM), q_dtype),
                   jax.ShapeDtypeStruct((n, KV_DIM), _F32),
                   jax.ShapeDtypeStruct((n, KV_DIM), _F32),
                   jax.ShapeDtypeStruct((n, D_MODEL), _BF16),
                   jax.ShapeDtypeStruct((n, D_MODEL), _BF16)],
        compiler_params=pltpu.CompilerParams(dimension_semantics=("arbitrary",),
                                             vmem_limit_bytes=VMEM_LIMIT),
        name="inproj",
    )(x, g, w_in_b, qg, kg)


def _ln_swish_project(y, lng_ref, lnb_ref, wo_ref, bo_ref):
    mu = jnp.mean(y, axis=-1, keepdims=True)
    yc = y - mu
    var = jnp.mean(yc * yc, axis=-1, keepdims=True)
    z = yc * lax.rsqrt(var + EPS) * lng_ref[...] + lnb_ref[...]
    z = z * jax.nn.sigmoid(z)
    return (_dot(z.astype(_BF16), wo_ref[...]) + bo_ref[...]).astype(_BF16)


HALO = 32
CONV_STEPS = 16
CH_TILES = D_CONV // LANES


PIECE = 256
PAIR_TILES = 2 * CH_TILES


def _inproj_conv_kernel(x_ref, g_ref, w_ref, qg_ref, kg_ref, w16_ref, b16_ref, lng_ref, lnb_ref, wo_ref, bo_ref,
                        q_ref, k_ref, v_ref, ga_ref, gb_ref, conv_ref, tail_ref, hist_ref, y_ref, *, tm):
    i = pl.program_id(1)
    rows = 2 * tm

    @pl.when(i == 0)
    def _():
        hist_ref[0:HALO * CH_TILES, :] = jnp.zeros((HALO * CH_TILES, LANES), jnp.uint32)

    @pl.when(i > 0)
    def _():
        hist_ref[0:HALO * CH_TILES, :] = hist_ref[tm * CH_TILES:(tm + HALO) * CH_TILES, :]

    x = jnp.concatenate([x_ref[0], x_ref[1]], axis=0)
    xn = x * lax.rsqrt(jnp.mean(x * x, axis=-1, keepdims=True) + EPS) * g_ref[...]
    xb = xn.astype(_BF16)

    def seg(lo, width):
        return _dot(xb, w_ref[:, lo:lo + width])

    def put(ref, lo, val):
        ref[0, :, lo:lo + PIECE] = val[:tm].astype(ref.dtype)
        ref[1, :, lo:lo + PIECE] = val[tm:].astype(ref.dtype)

    for p in range(D_CONV // PIECE):
        lo = p * PIECE
        glu = seg(lo, PIECE) * jax.nn.sigmoid(seg(D_CONV + lo, PIECE))
        tail_ref[0, :, lo:lo + PIECE] = glu[tm - HALO:tm, :]
        tail_ref[1, :, lo:lo + PIECE] = glu[rows - HALO:, :]
        words = _pack_pair(glu[:tm], glu[tm:])
        for c in range(PIECE // LANES):
            tile = lo // LANES + c
            hist_ref[pl.ds(HALO * CH_TILES + tile, tm, stride=CH_TILES), :] = words[:, c * LANES:(c + 1) * LANES]

    first = HALO - (CONV_WIDTH - 1)

    def conv_chunk(ci, carry):
        t0 = ci * CONV_STEPS
        acc = jnp.zeros((CONV_STEPS, PAIR_TILES, LANES), _F32)
        for j in range(CONV_WIDTH):
            lo = pl.multiple_of((t0 + first + j) * CH_TILES, CH_TILES)
            xw = pltpu.bitcast(hist_ref[pl.ds(lo, CONV_STEPS * CH_TILES), :], _BF16)
            acc = acc + (xw.reshape(CONV_STEPS, PAIR_TILES, LANES).astype(_F32)
                         * w16_ref[j][None].astype(_F32))
        y_ref[pl.ds(pl.multiple_of(t0 * PAIR_TILES, PAIR_TILES), CONV_STEPS * PAIR_TILES), :] = (
            (acc + b16_ref[...][None]).reshape(CONV_STEPS * PAIR_TILES, LANES))
        return carry

    def q_piece(lo):
        q = seg(2 * D_CONV + lo, PIECE)
        put(q_ref, lo, q * _head_rms_scale(q) * qg_ref[:, lo:lo + PIECE])

    def k_piece(lo):
        k = seg(2 * D_CONV + Q_DIM + lo, PIECE)
        put(k_ref, lo, k * _head_rms_scale(k) * kg_ref[:, lo:lo + PIECE])

    def v_piece(lo):
        put(v_ref, lo, seg(2 * D_CONV + Q_DIM + KV_DIM + lo, PIECE))

    def gate_piece(ref, base, lo):
        put(ref, lo, jax.nn.sigmoid(seg(base + lo, PIECE)))

    gate_base = 2 * D_CONV + Q_DIM + 2 * KV_DIM
    pieces = ([functools.partial(q_piece, lo) for lo in range(0, Q_DIM, PIECE)]
              + [functools.partial(k_piece, lo) for lo in range(0, KV_DIM, PIECE)]
              + [functools.partial(v_piece, lo) for lo in range(0, KV_DIM, PIECE)]
              + [functools.partial(gate_piece, ga_ref, gate_base, lo) for lo in range(0, D_MODEL, PIECE)]
              + [functools.partial(gate_piece, gb_ref, gate_base + D_MODEL, lo) for lo in range(0, D_MODEL, PIECE)])

    for piece in pieces:
        piece()
    lax.fori_loop(0, tm // CONV_STEPS, conv_chunk, 0)

    y = jnp.concatenate(
        [jnp.concatenate([y_ref[pl.ds(2 * c + s, tm, stride=PAIR_TILES), :] for c in range(CH_TILES)], axis=1)
         for s in range(2)], axis=0)
    out = _ln_swish_project(y, lng_ref, lnb_ref, wo_ref, bo_ref)
    conv_ref[0] = out[:tm]
    conv_ref[1] = out[tm:]


def _inproj_conv(x, g, w_in_b, qg, kg, w_dw, b_dw, lng, lnb, wo_b, bo, tm):
    bsz, t, _ = x.shape
    assert CH_TILES == SUBLANES and bsz % 2 == 0 and t % tm == 0
    nt = t // tm
    w16 = jnp.repeat(w_dw.reshape(CONV_WIDTH, CH_TILES, LANES), 2, axis=1).astype(_BF16)
    b16 = jnp.repeat(b_dw.reshape(CH_TILES, LANES), 2, axis=0)
    full = _resident
    blk = lambda w: pl.BlockSpec((2, tm, w), lambda p, i: (p, i, 0))
    shape = lambda w, dt: jax.ShapeDtypeStruct((bsz, t, w), dt)
    outs = pl.pallas_call(
        functools.partial(_inproj_conv_kernel, tm=tm),
        grid=(bsz // 2, nt),
        in_specs=[blk(D_MODEL), full(g), full(w_in_b), full(qg), full(kg), full(w16), full(b16), full(lng), full(lnb),
                  full(wo_b), full(bo)],
        out_specs=[blk(Q_DIM), blk(KV_DIM), blk(KV_DIM), blk(D_MODEL), blk(D_MODEL), blk(D_MODEL),
                   pl.BlockSpec((2, HALO, D_CONV), lambda p, i: (p, 0, 0))],
        out_shape=[shape(Q_DIM, _BF16), shape(KV_DIM, _F32), shape(KV_DIM, _F32), shape(D_MODEL, _BF16),
                   shape(D_MODEL, _BF16), shape(D_MODEL, _BF16), jax.ShapeDtypeStruct((bsz, HALO, D_CONV), _F32)],
        scratch_shapes=[pltpu.VMEM(((tm + HALO) * CH_TILES, LANES), jnp.uint32),
                        pltpu.VMEM((tm * PAIR_TILES, LANES), _F32)],
        compiler_params=pltpu.CompilerParams(dimension_semantics=("arbitrary", "arbitrary"),
                                             vmem_limit_bytes=VMEM_LIMIT),
        name="inproj_conv",
    )(x, g, w_in_b, qg, kg, w16, b16, lng, lnb, wo_b, bo)
    return [o.reshape(bsz * t, o.shape[-1]) for o in outs[:-1]] + [outs[-1]]


def _conv_sample_kernel(state_ref, glu_ref, wdw_ref, bdw_ref, lng_ref, lnb_ref, wo_ref, bo_ref,
                        out_ref, state_out_ref):
    keep, steps = state_ref.shape[0], glu_ref.shape[0]

    def hist(u):
        return state_ref[u] if u < keep else glu_ref[u - keep]

    for t in range(steps):
        acc = hist(t) * wdw_ref[0:1, :]
        for j in range(1, CONV_WIDTH):
            acc = acc + hist(t + j) * wdw_ref[j:j + 1, :]
        out_ref[t] = _ln_swish_project(acc + bdw_ref[...], lng_ref, lnb_ref, wo_ref, bo_ref)
    state_out_ref[0:keep - steps] = state_ref[steps:keep]
    state_out_ref[keep - steps:keep] = glu_ref[...]


def _conv_sample(state_t, glu_t, w_dw, b_dw, lng, lnb, wo_b, bo, sb):
    keep, nseq, _ = state_t.shape
    steps = glu_t.shape[0]
    full = _resident
    blk = lambda r: pl.BlockSpec((r, sb, D_CONV), lambda i: (0, i, 0))
    return pl.pallas_call(
        _conv_sample_kernel,
        grid=(nseq // sb,),
        in_specs=[blk(keep), blk(steps), full(w_dw), full(b_dw), full(lng), full(lnb), full(wo_b), full(bo)],
        out_specs=[blk(steps), blk(keep)],
        out_shape=[jax.ShapeDtypeStruct((steps, nseq, D_MODEL), _BF16),
                   jax.ShapeDtypeStruct(state_t.shape, _F32)],
        compiler_params=pltpu.CompilerParams(dimension_semantics=("arbitrary",),
                                             vmem_limit_bytes=VMEM_LIMIT),
        name="conv_sample",
    )(state_t, glu_t, w_dw, b_dw, lng, lnb, wo_b, bo)


def _bucket_map():
    i = np.arange(WINDOW)[:, None]
    j = np.arange(WINDOW)[None, :]
    n = (i - j) % WINDOW
    nf = np.maximum(n, 1).astype(np.float32)
    large = MAX_EXACT + (np.log(nf / np.float32(MAX_EXACT)) / np.float32(math.log(MAX_DISTANCE / MAX_EXACT))
                         * np.float32(N_BUCKETS - MAX_EXACT)).astype(np.int32)
    return np.where(n < MAX_EXACT, n, np.minimum(large, N_BUCKETS - 1)).astype(np.int32)


def _bias_table_kernel(rb_ref, bm_ref, tbl_ref):
    p = pl.program_id(0)
    bm = bm_ref[...]
    for half in range(2):
        h = 2 * p + half
        t = jnp.zeros(bm.shape, _F32)
        for b in range(N_BUCKETS):
            t = jnp.where(bm == b, rb_ref[b, h], t)
        tbl_ref[0, :, half * WINDOW:(half + 1) * WINDOW] = t


def _bias_tables(rel_bias):
    bm = jnp.asarray(_bucket_map())
    return pl.pallas_call(
        _bias_table_kernel,
        grid=(N_PAIRS,),
        in_specs=[pl.BlockSpec(memory_space=pltpu.SMEM), pl.BlockSpec(bm.shape, lambda p: (0, 0))],
        out_specs=pl.BlockSpec((1, WINDOW, 2 * WINDOW), lambda p: (p, 0, 0)),
        out_shape=jax.ShapeDtypeStruct((N_PAIRS, WINDOW, 2 * WINDOW), _F32),
        name="bias_tables",
    )(rel_bias, bm)


def _block_diag_pairs(slab):
    low = lax.broadcasted_iota(jnp.int32, slab.shape, 1) < HEAD_DIM
    swapped = pltpu.roll(slab, HEAD_DIM, axis=1)
    zero = jnp.zeros_like(slab)
    first = jnp.concatenate([jnp.where(low, slab, zero), jnp.where(low, zero, swapped)], axis=0)
    second = jnp.concatenate([jnp.where(low, swapped, zero), jnp.where(low, zero, slab)], axis=0)
    return first.astype(_BF16), second.astype(_BF16)


def _kv_operands(k_blk, v_blk):
    ops = []
    for slab in range(KV_DIM // LANES):
        cols = slice(slab * LANES, (slab + 1) * LANES)
        ops.extend(zip(_block_diag_pairs(k_blk[:, cols]), _block_diag_pairs(v_blk[:, cols])))
    return ops


def _attend(q, prev_ops, own_ops, tbl_ref, sink_ref, prev_shift, store, transposed=False):
    tq = q.shape[0]
    rows = 2 * tq
    row = lax.broadcasted_iota(jnp.int32, (rows, 2 * WINDOW), 0)
    col = lax.broadcasted_iota(jnp.int32, (rows, 2 * WINDOW), 1)
    from_prev = (col & (WINDOW - 1)) > jnp.where(row >= tq, row - tq, row)
    top = lax.broadcasted_iota(jnp.int32, (rows, 1), 0) < tq
    low = lax.broadcasted_iota(jnp.int32, (rows, LANES), 1) < HEAD_DIM
    contract_last = (((1,), (1,)), ((), ()))

    def logits(a, k_op):
        return _dot(a, k_op) if transposed else lax.dot_general(a, k_op, contract_last, preferred_element_type=_F32)

    def weighted_values(pr, v_op):
        return lax.dot_general(pr, v_op, contract_last, preferred_element_type=_F32) if transposed else _dot(pr, v_op)

    for kvh in range(N_KV_HEADS):
        (k_prev, v_prev), (k_own, v_own) = prev_ops[kvh], own_ops[kvh]
        pair_a = 2 * kvh
        pair_b = pair_a + 1
        qq = jnp.concatenate([q[:, pair_a * LANES:(pair_a + 1) * LANES],
                              q[:, pair_b * LANES:(pair_b + 1) * LANES]], axis=0).astype(_BF16)
        sp = logits(qq, k_prev)
        so = logits(qq, k_own)
        bias = jnp.concatenate([tbl_ref[pair_a, 0:tq, :], tbl_ref[pair_b, 0:tq, :]], axis=0)
        s = jnp.where(from_prev, sp + prev_shift, so) + bias
        sink_even = jnp.where(top, sink_ref[2 * pair_a], sink_ref[2 * pair_b])
        sink_odd = jnp.where(top, sink_ref[2 * pair_a + 1], sink_ref[2 * pair_b + 1])
        m_even = jnp.maximum(jnp.max(s[:, :WINDOW], axis=-1, keepdims=True), sink_even)
        m_odd = jnp.maximum(jnp.max(s[:, WINDOW:], axis=-1, keepdims=True), sink_odd)
        p = jnp.exp(s - jnp.where(col < WINDOW, m_even, m_odd)).astype(_BF16)
        zero = jnp.zeros_like(p)
        o = (weighted_values(jnp.where(from_prev, p, zero), v_prev)
             + weighted_values(jnp.where(from_prev, zero, p), v_own))
        pf = p.astype(_F32)
        sums = jnp.where(low, jnp.sum(pf[:, :WINDOW], axis=-1, keepdims=True),
                         jnp.sum(pf[:, WINDOW:], axis=-1, keepdims=True))
        den = sums + jnp.where(low, jnp.exp(sink_even - m_even), jnp.exp(sink_odd - m_odd))
        o = o / den
        store(pair_a, o[:tq])
        store(pair_b, o[tq:])


PROMPT_QBLOCKS = 4


def _attn_prompt_kernel(sink_ref, q_ref, kp_ref, ko_ref, vp_ref, vo_ref, tbl_ref, o_ref):
    prev_shift = jnp.where(pl.program_id(1) == 0, MASK_VALUE, 0.0).astype(_F32)
    ops = [_kv_operands(kp_ref[...], vp_ref[...])]
    for b in range(PROMPT_QBLOCKS):
        rows = slice(b * WINDOW, (b + 1) * WINDOW)
        ops.append(_kv_operands(ko_ref[rows, :], vo_ref[rows, :]))

        def store(pair, o, rows=rows):
            o_ref[rows, pair * LANES:(pair + 1) * LANES] = o.astype(o_ref.dtype)

        _attend(q_ref[rows, :], ops[b], ops[b + 1], tbl_ref, sink_ref,
                prev_shift if b == 0 else jnp.float32(0.0), store)


def _attn_prompt(q, k, v, tbl, sinks, bsz, t):
    tq = PROMPT_QBLOCKS * WINDOW
    nb = t // tq
    own = lambda w: pl.BlockSpec((tq, w), lambda b, i: (b * nb + i, 0))
    prev = lambda w: pl.BlockSpec((WINDOW, w),
                                  lambda b, i: (PROMPT_QBLOCKS * (b * nb + i) - jnp.minimum(i, 1), 0))
    return pl.pallas_call(
        _attn_prompt_kernel,
        grid=(bsz, nb),
        in_specs=[pl.BlockSpec(memory_space=pltpu.SMEM), own(Q_DIM), prev(KV_DIM), own(KV_DIM),
                  prev(KV_DIM), own(KV_DIM), pl.BlockSpec(tbl.shape, lambda b, i: (0, 0, 0))],
        out_specs=own(Q_DIM),
        out_shape=jax.ShapeDtypeStruct((bsz * t, Q_DIM), _BF16),
        compiler_params=pltpu.CompilerParams(dimension_semantics=("arbitrary", "arbitrary"),
                                             vmem_limit_bytes=VMEM_LIMIT),
        name="attn_prompt",
    )(sinks, q, k, k, v, v, tbl)


SAMPLE_UNROLL = 2


def _block_diag_t(x):
    xb = x.astype(_BF16)
    z = jnp.zeros_like(xb)
    return jnp.concatenate([jnp.concatenate([xb, z], axis=1), jnp.concatenate([z, xb], axis=1)], axis=0)


def _attn_sample_kernel(sink_ref, q_ref, kn_ref, vn_ref, ck_ref, cv_ref, tbl_ref, o_ref, cko_ref, cvo_ref,
                        *, sb, steps):
    pad = jnp.zeros((WINDOW - steps, LANES), _F32)
    lane = lax.broadcasted_iota(jnp.int32, (HEAD_DIM, WINDOW), 1)

    def one_sequence(s, carry):
        def store(pair, o):
            o_ref[s, :, pair * LANES:(pair + 1) * LANES] = o

        prev_ops, own_ops = [], []
        for slab in range(KV_DIM // LANES):
            cols = slice(slab * LANES, (slab + 1) * LANES)
            new_k = jnp.concatenate([kn_ref[s][:, cols], pad], axis=0).T
            new_v = jnp.concatenate([vn_ref[s][:, cols], pad], axis=0).T
            for sub in range(2):
                kvh = 2 * slab + sub
                part = slice(sub * HEAD_DIM, (sub + 1) * HEAD_DIM)
                kt, vt = ck_ref[s, kvh], cv_ref[s, kvh]
                cko_ref[s, kvh] = pltpu.roll(jnp.where(lane < steps, new_k[part], kt), WINDOW - steps, axis=1)
                cvo_ref[s, kvh] = pltpu.roll(jnp.where(lane < steps, new_v[part], vt), WINDOW - steps, axis=1)
                prev_ops.append((_block_diag_t(kt), _block_diag_t(vt)))
                own_ops.append((_block_diag_t(new_k[part]), _block_diag_t(new_v[part])))
        _attend(q_ref[s], prev_ops, own_ops, tbl_ref, sink_ref, jnp.float32(0.0), store, transposed=True)
        return carry

    lax.fori_loop(0, sb, one_sequence, 0, unroll=SAMPLE_UNROLL)


def _attn_sample(q, k_new, v_new, cache_kt, cache_vt, tbl, sinks, sb):
    nseq, steps, _ = q.shape
    seq = lambda r, w: pl.BlockSpec((sb, r, w), lambda i: (i, 0, 0))
    cache = pl.BlockSpec((sb, N_KV_HEADS, HEAD_DIM, WINDOW), lambda i: (i, 0, 0, 0))
    return pl.pallas_call(
        functools.partial(_attn_sample_kernel, sb=sb, steps=steps),
        grid=(nseq // sb,),
        in_specs=[pl.BlockSpec(memory_space=pltpu.SMEM), seq(steps, Q_DIM), seq(steps, KV_DIM), seq(steps, KV_DIM),
                  cache, cache, pl.BlockSpec(tbl.shape, lambda i: (0, 0, 0))],
        out_specs=[seq(steps, Q_DIM), cache, cache],
        out_shape=[jax.ShapeDtypeStruct((nseq, steps, Q_DIM), _F32),
                   jax.ShapeDtypeStruct(cache_kt.shape, _F32), jax.ShapeDtypeStruct(cache_vt.shape, _F32)],
        compiler_params=pltpu.CompilerParams(dimension_semantics=("arbitrary",),
                                             vmem_limit_bytes=VMEM_LIMIT),
        name="attn_sample",
    )(sinks, q, k_new, v_new, cache_kt, cache_vt, tbl)


def _lane_min_index(mask, lane):
    return jnp.min(jnp.where(mask, lane, float(LANES)), axis=-1, keepdims=True)


def _finish_kernel(x_ref, conv_ref, o_ref, ga_ref, gb_ref, wa_ref, wo_ref, ng_ref, wr_hi_ref, wr_lo_ref, br_ref,
                   tri_ref, h_ref, hn_ref, route_ref, count_ref, running_ref):
    @pl.when(pl.program_id(0) == 0)
    def _():
        running_ref[...] = jnp.zeros_like(running_ref)

    attn_out = _dot(o_ref[...].astype(_BF16), wa_ref[...])
    merged = ga_ref[...].astype(_F32) * conv_ref[...].astype(_F32) + gb_ref[...].astype(_F32) * attn_out
    h = x_ref[...] + _dot(merged.astype(_BF16), wo_ref[...])
    h_ref[...] = h
    hn = h * lax.rsqrt(jnp.mean(h * h, axis=-1, keepdims=True) + EPS) * ng_ref[...]
    hn_ref[...] = _pack_bf16_pairs(hn)

    hi, lo = _split_bf16(hn)
    logits = _dot(hi, wr_hi_ref[...]) + _dot(lo, wr_hi_ref[...]) + _dot(hi, wr_lo_ref[...]) + br_ref[...]
    lane = lax.broadcasted_iota(jnp.int32, logits.shape, 1).astype(_F32)
    gmask = lane < N_GROUPS
    gl = jnp.where(gmask, logits, MASK_VALUE)
    gmax = jnp.max(gl, axis=-1, keepdims=True)
    grp = _lane_min_index(gmask & (gl == gmax), lane)
    p_grp = 1.0 / jnp.sum(jnp.where(gmask, jnp.exp(gl - gmax), 0.0), axis=-1, keepdims=True)
    e_lo = N_GROUPS + grp * EXPERTS_PER_GROUP
    emask = (lane >= e_lo) & (lane < e_lo + EXPERTS_PER_GROUP)
    el = jnp.where(emask, logits, MASK_VALUE)
    ex = jnp.where(emask, jnp.exp(el - jnp.max(el, axis=-1, keepdims=True)), 0.0)
    prob = jnp.where(emask, ex / jnp.sum(ex, axis=-1, keepdims=True), -1.0)
    p1 = jnp.max(prob, axis=-1, keepdims=True)
    i1 = _lane_min_index(prob == p1, lane)
    rest = jnp.where(lane == i1, -1.0, prob)
    p2 = jnp.max(rest, axis=-1, keepdims=True)
    i2 = _lane_min_index(rest == p2, lane)
    w1 = p_grp * p1 / (p1 + p2)
    w2 = p_grp * p2 / (p1 + p2)
    e1 = i1 - N_GROUPS
    e2 = i2 - N_GROUPS

    hot1 = lane == e1
    hot2 = lane == e2
    hot = jnp.where(hot1 | hot2, 1.0, 0.0)
    before = _dot(tri_ref[...], hot.astype(_BF16)) + running_ref[...]
    rank1 = jnp.sum(jnp.where(hot1, before, 0.0), axis=-1, keepdims=True)
    rank2 = jnp.sum(jnp.where(hot2, before, 0.0), axis=-1, keepdims=True)
    running_ref[...] += jnp.sum(hot, axis=0, keepdims=True)
    count_ref[...] = jnp.broadcast_to(running_ref[...], count_ref.shape)

    fields = (e1, e2, w1, w2, rank1, rank2)
    route = jnp.zeros(logits.shape, _F32)
    for pos, val in enumerate(fields):
        route = jnp.where(lane == pos, val, route)
    route_ref[...] = route


ROUTE_E, ROUTE_W, ROUTE_RANK = 0, 2, 4
DEST_ROWS = 1024


def _finish(x, conv_out, o, ga, gb, wa_b, wo_b, ng, wr_hi, wr_lo, br, tm):
    n = x.shape[0]
    tri = jnp.asarray(np.tril(np.ones((tm, tm), np.float32), -1), _BF16)
    row = lambda w: pl.BlockSpec((tm, w), lambda i: (i, 0))
    full = _resident
    return pl.pallas_call(
        _finish_kernel,
        grid=(n // tm,),
        in_specs=[row(D_MODEL), row(D_MODEL), row(Q_DIM), row(D_MODEL), row(D_MODEL),
                  full(wa_b), full(wo_b), full(ng), full(wr_hi), full(wr_lo), full(br), full(tri)],
        out_specs=[row(D_MODEL), row(HALF), row(LANES), pl.BlockSpec((SUBLANES, LANES), lambda i: (0, 0))],
        out_shape=[jax.ShapeDtypeStruct((n, D_MODEL), _F32),
                   jax.ShapeDtypeStruct((n, HALF), jnp.uint32),
                   jax.ShapeDtypeStruct((n, LANES), _F32),
                   jax.ShapeDtypeStruct((SUBLANES, LANES), _F32)],
        scratch_shapes=[pltpu.VMEM((1, LANES), _F32)],
        compiler_params=pltpu.CompilerParams(dimension_semantics=("arbitrary",),
                                             vmem_limit_bytes=VMEM_LIMIT),
        name="finish",
    )(x, conv_out, o, ga, gb, wa_b, wo_b, ng, wr_hi, wr_lo, br, tri)


def _dest_kernel(route_ref, starts_ref, dest_ref):
    route = route_ref[...]
    lane = lax.broadcasted_iota(jnp.int32, route.shape, 1)
    out = jnp.zeros(route.shape, jnp.int32)
    for j in range(TOP_K):
        e = route[:, ROUTE_E + j:ROUTE_E + j + 1].astype(jnp.int32)
        start = jnp.sum(jnp.where(lane == e, starts_ref[...], 0.0), axis=-1, keepdims=True)
        d = (start + route[:, ROUTE_RANK + j:ROUTE_RANK + j + 1]).astype(jnp.int32)
        out = jnp.where(lane == j, d, out)
    dest_ref[...] = out


def _dest(route, starts_row, tm):
    n = route.shape[0]
    row = pl.BlockSpec((tm, LANES), lambda i: (i, 0))
    return pl.pallas_call(
        _dest_kernel,
        grid=(n // tm,),
        in_specs=[row, pl.BlockSpec((1, LANES), lambda i: (0, 0))],
        out_specs=row,
        out_shape=jax.ShapeDtypeStruct((n, LANES), jnp.int32),
        name="dest",
    )(route, starts_row)


SC_CORES = 2
SC_SUBCORES = 16
SC_WORKERS = SC_CORES * SC_SUBCORES
SC_IN_FLIGHT = 4
SC_CHUNK_BYTES = 64 * 1024


def _sc_move_rows(src, idx, gather):
    n, d = src.shape
    b = idx.shape[0]
    per_worker = b // SC_WORKERS
    assert per_worker * SC_WORKERS == b and (gather or n % per_worker == 0), (b, n)
    chunk = min(per_worker // SC_IN_FLIGHT, SC_CHUNK_BYTES // (d * 4))
    n_iters = per_worker // (chunk * SC_IN_FLIGHT)
    assert n_iters * chunk * SC_IN_FLIGHT == per_worker and chunk % SUBLANES == 0, (per_worker, chunk)
    mesh = plsc.VectorSubcoreMesh(core_axis_name="c", subcore_axis_name="s",
                                  num_cores=SC_CORES, num_subcores=SC_SUBCORES)
    scratch = ([pltpu.VMEM((chunk,), jnp.int32)] * SC_IN_FLIGHT + [pltpu.VMEM((chunk, d), src.dtype)] * SC_IN_FLIGHT
               + [pltpu.SemaphoreType.DMA] * SC_IN_FLIGHT)

    @functools.partial(pl.kernel, mesh=mesh, out_type=jax.ShapeDtypeStruct((b, d), src.dtype),
                       scratch_types=scratch, name="sc_gather_rows" if gather else "sc_scatter_rows",
                       cost_estimate=pl.CostEstimate(flops=0, transcendentals=0, bytes_accessed=2 * b * d * 4 + b * 4))
    def move(src_hbm, idx_hbm, out_hbm, *bufs):
        idx_v = bufs[:SC_IN_FLIGHT]
        rows_v = bufs[SC_IN_FLIGHT:2 * SC_IN_FLIGHT]
        sems = bufs[2 * SC_IN_FLIGHT:]
        worker = lax.axis_index("s") * SC_CORES + lax.axis_index("c")

        @pl.loop(0, n_iters)
        def _(it):
            bases = [pl.multiple_of(worker * per_worker + (it * SC_IN_FLIGHT + j) * chunk, chunk)
                     for j in range(SC_IN_FLIGHT)]
            loads = [pltpu.async_copy(idx_hbm.at[pl.ds(bases[j], chunk)], idx_v[j], sems[j])
                     for j in range(SC_IN_FLIGHT)]
            reads = []
            for j in range(SC_IN_FLIGHT):
                loads[j].wait()
                if gather:
                    rows = src_hbm.at[idx_v[j]]
                else:
                    rows = src_hbm.at[pl.ds(pl.multiple_of(lax.rem(bases[j], n), chunk), chunk)]
                reads.append(pltpu.async_copy(rows, rows_v[j], sems[j]))
            writes = []
            for j in range(SC_IN_FLIGHT):
                reads[j].wait()
                dst = out_hbm.at[pl.ds(bases[j], chunk)] if gather else out_hbm.at[idx_v[j]]
                writes.append(pltpu.async_copy(rows_v[j], dst, sems[j]))
            for w in writes:
                w.wait()

    return move(src, idx)


def _sc_gather_rows(table, idx):
    return _sc_move_rows(table, idx, gather=True)


def _sc_scatter_rows(src, idx):
    return _sc_move_rows(src, idx, gather=False)


def _expert_kernel(blk_ref, exp_ref, lo_ref, hi_ref, x_ref, wg_ref, wu_ref, wd_ref, yb_ref,
                   wg_b, wu_b, wd_b, held_ref):
    del blk_ref
    k = pl.program_id(0)
    lo, hi, e = lo_ref[k], hi_ref[k], exp_ref[k]

    @pl.when(k == 0)
    def _():
        held_ref[0] = -1

    @pl.when(hi > lo)
    def _():
        @pl.when(held_ref[0] != e)
        def _():
            wg_b[...] = wg_ref[0].astype(_BF16)
            wu_b[...] = wu_ref[0].astype(_BF16)
            wd_b[...] = wd_ref[0].astype(_BF16)
            held_ref[0] = e

        xb = _unpack_bf16_pairs(x_ref[...]).astype(_BF16)
        g = _dot(xb, wg_b[...])
        u = _dot(xb, wu_b[...])
        hid = g * jax.nn.sigmoid(g) * u
        y = _dot(hid.astype(_BF16), wd_b[...])
        r = lax.broadcasted_iota(jnp.int32, yb_ref.shape, 0)
        pltpu.store(yb_ref, _pack_bf16_pairs(y), mask=(r >= lo) & (r < hi))


def _experts(items, xs, w_gate, w_up, w_down):
    n_items = items[0].shape[0]
    wspec = lambda a: pl.BlockSpec((1,) + a.shape[1:], lambda k, blk, exp, lo, hi: (exp[k], 0, 0))
    rows = pl.BlockSpec((MOE_ROWS, HALF), lambda k, blk, exp, lo, hi: (blk[k], 0))
    grid_spec = pltpu.PrefetchScalarGridSpec(
        num_scalar_prefetch=4,
        grid=(n_items,),
        in_specs=[rows, wspec(w_gate), wspec(w_up), wspec(w_down)],
        out_specs=rows,
        scratch_shapes=[pltpu.VMEM(w_gate.shape[1:], _BF16), pltpu.VMEM(w_up.shape[1:], _BF16),
                        pltpu.VMEM(w_down.shape[1:], _BF16), pltpu.SMEM((1,), jnp.int32)],
    )
    return pl.pallas_call(
        _expert_kernel,
        grid_spec=grid_spec,
        out_shape=jax.ShapeDtypeStruct(xs.shape, xs.dtype),
        compiler_params=pltpu.CompilerParams(dimension_semantics=("arbitrary",),
                                             vmem_limit_bytes=VMEM_LIMIT),
        cost_estimate=pl.CostEstimate(
            flops=n_items * MOE_ROWS * 6 * D_MODEL * D_EXPERT, transcendentals=n_items * MOE_ROWS * D_EXPERT,
            bytes_accessed=2 * xs.size * 4 + (w_gate.size + w_up.size + w_down.size) * 4),
        name="experts",
    )(*items, xs, w_gate, w_up, w_down)


def _combine_kernel(h_ref, route_ref, g0_ref, g1_ref, y_ref):
    route = route_ref[...]
    y_ref[...] = (h_ref[...] + route[:, ROUTE_W:ROUTE_W + 1] * _unpack_bf16_pairs(g0_ref[...])
                  + route[:, ROUTE_W + 1:ROUTE_W + 2] * _unpack_bf16_pairs(g1_ref[...]))


def _combine(h, route, g, tm):
    n = h.shape[0]
    nt = n // tm
    row = lambda w: pl.BlockSpec((tm, w), lambda i: (i, 0))
    return pl.pallas_call(
        _combine_kernel,
        grid=(nt,),
        in_specs=[row(D_MODEL), row(LANES), row(HALF), pl.BlockSpec((tm, HALF), lambda i: (nt + i, 0))],
        out_specs=row(D_MODEL),
        out_shape=jax.ShapeDtypeStruct((n, D_MODEL), _F32),
        compiler_params=pltpu.CompilerParams(dimension_semantics=("arbitrary",),
                                             vmem_limit_bytes=VMEM_LIMIT),
        name="combine",
    )(h, route, g, g)


def _work_items(counts, n_pairs):
    n_blocks = n_pairs // MOE_ROWS
    starts = jnp.cumsum(counts) - counts
    cuts = jnp.sort(jnp.concatenate([jnp.arange(n_blocks, dtype=jnp.int32) * MOE_ROWS, starts]))
    ends = jnp.concatenate([cuts[1:], jnp.full((1,), n_pairs, jnp.int32)])
    blk = jnp.minimum(cuts // MOE_ROWS, n_blocks - 1)
    expert = jnp.clip(jnp.sum(starts[None, :] <= cuts[:, None], axis=1) - 1, 0, N_EXPERTS - 1).astype(jnp.int32)
    return starts, (blk, expert, cuts - blk * MOE_ROWS, ends - blk * MOE_ROWS)


def _dispatch(hn, route, counts_rows):
    n = hn.shape[0]
    n_pairs = n * TOP_K
    counts = counts_rows[0, :N_EXPERTS].astype(jnp.int32)
    starts, items = _work_items(counts, n_pairs)
    starts_row = jnp.zeros((1, LANES), _F32).at[0, :N_EXPERTS].set(starts.astype(_F32))
    dest = _dest(route, starts_row, min(n, DEST_ROWS))[:, :TOP_K].T.reshape(n_pairs)
    return _sc_scatter_rows(hn, dest), dest, items


def _after(value, *earlier):
    return lax.optimization_barrier((value,) + earlier)[0]


def kernel(x_prompt, x_sample, state_conv, cache_k, cache_v, norm_attn_g, w_in, q_norm_g, k_norm_g, rel_bias, attn_sinks, w_dw, b_dw, conv_ln_g, conv_ln_b, w_conv_out, b_conv_out, w_attn_out, w_out, norm_ffn_g, w_grp, b_grp, w_router, b_router, w_gate, w_up, w_down):
    bsz, t, _ = x_prompt.shape
    nseq, steps, _ = x_sample.shape
    row = lambda a: a.reshape(1, -1).astype(_F32)

    w_in_b = w_in.astype(_BF16)
    wco_b = w_conv_out.astype(_BF16)
    wa_b = w_attn_out.astype(_BF16)
    wo_b = w_out.astype(_BF16)
    qg = row(jnp.tile(q_norm_g, N_HEADS)) * (HEAD_DIM ** -0.5)
    kg = row(jnp.tile(k_norm_g, N_KV_HEADS))
    w_rt = jnp.zeros((D_MODEL, LANES), _F32).at[:, :N_GROUPS].set(w_grp).at[:, N_GROUPS:N_GROUPS + N_EXPERTS].set(w_router)
    wr_hi = w_rt.astype(_BF16)
    wr_lo = (w_rt - wr_hi.astype(_F32)).astype(_BF16)
    b_rt = jnp.zeros((1, LANES), _F32).at[0, :N_GROUPS].set(b_grp).at[0, N_GROUPS:N_GROUPS + N_EXPERTS].set(b_router)
    tbl = _bias_tables(rel_bias)
    conv_params = (w_dw, row(b_dw), row(conv_ln_g), row(conv_ln_b), wco_b, row(b_conv_out))

    def finish(x2d, conv_out, o, ga, gb):
        return _finish(x2d, conv_out, o, ga, gb, wa_b, wo_b, row(norm_ffn_g), wr_hi, wr_lo, b_rt, ROW_TILE)

    xp = x_prompt.reshape(bsz * t, D_MODEL)
    q, k, v, ga, gb, conv_out, glu_tail = _inproj_conv(x_prompt, row(norm_attn_g), w_in_b, qg, kg, *conv_params,
                                                       ROW_TILE // 2)
    o = _attn_prompt(q, k, v, tbl, attn_sinks, bsz, t)
    h_p, hn_p, route_p, counts_p = finish(xp, conv_out, o, ga, gb)
    rows_p, dest_p, items_p = _dispatch(hn_p, route_p, counts_p)
    state_conv_prompt = glu_tail[:, HALO - (CONV_WIDTH - 1):]
    tail = lambda a: a.reshape(bsz, t, KV_DIM)[:, t - WINDOW:].reshape(bsz, WINDOW, N_KV_HEADS, HEAD_DIM)
    cache_k_prompt, cache_v_prompt = tail(k), tail(v)

    xs = _after(x_sample, dest_p).reshape(nseq * steps, D_MODEL)
    glu, q, k, v, ga, gb = _inproj(xs, row(norm_attn_g), w_in_b, qg, kg, _F32, ROW_TILE)
    glu_t = glu.reshape(nseq, steps, D_CONV).transpose(1, 0, 2)
    conv_out, state_t = _conv_sample(state_conv.transpose(1, 0, 2), glu_t, *conv_params, 64)
    conv_out = conv_out.transpose(1, 0, 2).reshape(nseq * steps, D_MODEL)
    k3 = k.reshape(nseq, steps, KV_DIM)
    v3 = v.reshape(nseq, steps, KV_DIM)
    o, ck_t, cv_t = _attn_sample(q.reshape(nseq, steps, Q_DIM), k3, v3, cache_k.transpose(0, 2, 3, 1),
                                 cache_v.transpose(0, 2, 3, 1), tbl, attn_sinks, 8)
    h_s, hn_s, route_s, counts_s = finish(xs, conv_out, o.reshape(nseq * steps, Q_DIM), ga, gb)
    rows_s, dest_s, items_s = _dispatch(hn_s, route_s, counts_s)
    state_conv_sample = state_t.transpose(1, 0, 2)
    cache_k_sample = ck_t.transpose(0, 3, 1, 2)
    cache_v_sample = cv_t.transpose(0, 3, 1, 2)

    yb_p = _experts(items_p, rows_p, w_gate, w_up, w_down)
    g_p = _sc_gather_rows(yb_p, dest_p)
    yb_s = _experts(items_s, _after(rows_s, yb_p), w_gate, w_up, w_down)
    g_s = _sc_gather_rows(yb_s, dest_s)
    y_prompt = _combine(h_p, route_p, g_p, ROW_TILE).reshape(bsz, t, D_MODEL)
    y_sample = _combine(h_s, route_s, _after(g_s, y_prompt), ROW_TILE).reshape(nseq, steps, D_MODEL)

    return (y_prompt, y_sample, state_conv_prompt, cache_k_prompt, cache_v_prompt,
            state_conv_sample, cache_k_sample, cache_v_sample)
```

```python
import functools
import math

import numpy as np
import jax
import jax.numpy as jnp
from jax import lax
from jax.experimental import pallas as pl
from jax.experimental.pallas import tpu as pltpu
from jax.experimental.pallas import tpu_sc as plsc

D_MODEL = 1024
N_HEADS = 16
HEAD_DIM = 64
N_KV_HEADS = 4
WINDOW = 128
Q_DIM = N_HEADS * HEAD_DIM
KV_DIM = N_KV_HEADS * HEAD_DIM
N_BUCKETS = 32
MAX_EXACT = N_BUCKETS // 2
MAX_DISTANCE = 128
D_CONV = D_MODEL
CONV_WIDTH = 31
N_GROUPS = 4
EXPERTS_PER_GROUP = 8
N_EXPERTS = N_GROUPS * EXPERTS_PER_GROUP
TOP_K = 2
D_EXPERT = 256
EPS = 1e-6

LANES = 128
SUBLANES = 8
N_PAIRS = N_HEADS // 2
MOE_ROWS = 512
MASK_VALUE = -1e30
VMEM_LIMIT = 56 * 1024 * 1024
ROW_TILE = 512

_F32 = jnp.float32
_BF16 = jnp.bfloat16


def _resident(a):
    return pl.BlockSpec(a.shape, lambda *_: (0,) * a.ndim, pipeline_mode=pl.Buffered(1))


def _dot(a, b):
    return jnp.dot(a, b, preferred_element_type=_F32)


HALF = D_MODEL // 2


def _pack_pair(lo, hi):
    lo_bits = pltpu.bitcast(lo.astype(_BF16).astype(_F32), jnp.uint32)
    hi_bits = pltpu.bitcast(hi.astype(_BF16).astype(_F32), jnp.uint32)
    return hi_bits | (lo_bits >> 16)


def _pack_bf16_pairs(x):
    return _pack_pair(x[:, :HALF], x[:, HALF:])


def _unpack_bf16_pairs(w):
    lo = pltpu.bitcast(w << 16, _F32)
    hi = pltpu.bitcast(w & jnp.uint32(0xFFFF0000), _F32)
    return jnp.concatenate([lo, hi], axis=1)


def _head_rms_scale(z):
    low = lax.broadcasted_iota(jnp.int32, (z.shape[0], LANES), 1) < HEAD_DIM
    slabs = []
    for c in range(z.shape[1] // LANES):
        sq = z[:, c * LANES:(c + 1) * LANES]
        sq = sq * sq
        first = jnp.sum(jnp.where(low, sq, 0.0), axis=-1, keepdims=True)
        second = jnp.sum(jnp.where(low, 0.0, sq), axis=-1, keepdims=True)
        slabs.append(lax.rsqrt(jnp.where(low, first, second) * (1.0 / HEAD_DIM) + EPS))
    return jnp.concatenate(slabs, axis=1)


def _inproj_kernel(x_ref, g_ref, w_ref, qg_ref, kg_ref,
                   glu_ref, q_ref, k_ref, v_ref, ga_ref, gb_ref):
    x = x_ref[...]
    xn = x * lax.rsqrt(jnp.mean(x * x, axis=-1, keepdims=True) + EPS) * g_ref[...]
    xb = xn.astype(_BF16)

    def seg(lo, width):
        return _dot(xb, w_ref[:, lo:lo + width])

    a = seg(0, D_CONV)
    b = seg(D_CONV, D_CONV)
    glu_ref[...] = a * jax.nn.sigmoid(b)
    off = 2 * D_CONV
    q = seg(off, Q_DIM)
    q_ref[...] = (q * _head_rms_scale(q) * qg_ref[...]).astype(q_ref.dtype)
    off += Q_DIM
    k = seg(off, KV_DIM)
    k_ref[...] = k * _head_rms_scale(k) * kg_ref[...]
    off += KV_DIM
    v_ref[...] = seg(off, KV_DIM)
    off += KV_DIM
    ga_ref[...] = jax.nn.sigmoid(seg(off, D_MODEL)).astype(ga_ref.dtype)
    off += D_MODEL
    gb_ref[...] = jax.nn.sigmoid(seg(off, D_MODEL)).astype(gb_ref.dtype)


def _inproj(x, g, w_in_b, qg, kg, q_dtype, tm):
    n = x.shape[0]
    in_dim = w_in_b.shape[1]
    row = lambda w: pl.BlockSpec((tm, w), lambda i: (i, 0))
    full = _resident
    return pl.pallas_call(
        _inproj_kernel,
        grid=(n // tm,),
        in_specs=[row(D_MODEL), full(g), full(w_in_b), full(qg), full(kg)],
        out_specs=[row(D_CONV), row(Q_DIM), row(KV_DIM), row(KV_DIM), row(D_MODEL), row(D_MODEL)],
        out_shape=[jax.ShapeDtypeStruct((n, D_CONV), _F32),
                   jax.ShapeDtypeStruct((n, Q_DIM), q_dtype),
                   jax.ShapeDtypeStruct((n, KV_DIM), _F32),
                   jax.ShapeDtypeStruct((n, KV_DIM), _F32),
                   jax.ShapeDtypeStruct((n, D_MODEL), _BF16),
                   jax.ShapeDtypeStruct((n, D_MODEL), _BF16)],
        compiler_params=pltpu.CompilerParams(dimension_semantics=("arbitrary",),
                                             vmem_limit_bytes=VMEM_LIMIT),
        name="inproj",
    )(x, g, w_in_b, qg, kg)


def _ln_swish_project(y, lng_ref, lnb_ref, wo_ref, bo_ref):
    mu = jnp.mean(y, axis=-1, keepdims=True)
    yc = y - mu
    var = jnp.mean(yc * yc, axis=-1, keepdims=True)
    z = yc * lax.rsqrt(var + EPS) * lng_ref[...] + lnb_ref[...]
    z = z * jax.nn.sigmoid(z)
    return (_dot(z.astype(_BF16), wo_ref[...]) + bo_ref[...]).astype(_BF16)


HALO = 32
CONV_STEPS = 16
CH_TILES = D_CONV // LANES


PIECE = 256
PAIR_TILES = 2 * CH_TILES


def _inproj_conv_kernel(x_ref, g_ref, w_ref, qg_ref, kg_ref, w16_ref, b16_ref, lng_ref, lnb_ref, wo_ref, bo_ref,
                        q_ref, k_ref, v_ref, ga_ref, gb_ref, conv_ref, tail_ref, hist_ref, y_ref, *, tm):
    i = pl.program_id(1)
    rows = 2 * tm

    @pl.when(i == 0)
    def _():
        hist_ref[0:HALO * CH_TILES, :] = jnp.zeros((HALO * CH_TILES, LANES), jnp.uint32)

    @pl.when(i > 0)
    def _():
        hist_ref[0:HALO * CH_TILES, :] = hist_ref[tm * CH_TILES:(tm + HALO) * CH_TILES, :]

    x = jnp.concatenate([x_ref[0], x_ref[1]], axis=0)
    xn = x * lax.rsqrt(jnp.mean(x * x, axis=-1, keepdims=True) + EPS) * g_ref[...]
    xb = xn.astype(_BF16)

    def seg(lo, width):
        return _dot(xb, w_ref[:, lo:lo + width])

    def put(ref, lo, val):
        ref[0, :, lo:lo + PIECE] = val[:tm].astype(ref.dtype)
        ref[1, :, lo:lo + PIECE] = val[tm:].astype(ref.dtype)

    for p in range(D_CONV // PIECE):
        lo = p * PIECE
        glu = seg(lo, PIECE) * jax.nn.sigmoid(seg(D_CONV + lo, PIECE))
        tail_ref[0, :, lo:lo + PIECE] = glu[tm - HALO:tm, :]
        tail_ref[1, :, lo:lo + PIECE] = glu[rows - HALO:, :]
        words = _pack_pair(glu[:tm], glu[tm:])
        for c in range(PIECE // LANES):
            tile = lo // LANES + c
            hist_ref[pl.ds(HALO * CH_TILES + tile, tm, stride=CH_TILES), :] = words[:, c * LANES:(c + 1) * LANES]

    first = HALO - (CONV_WIDTH - 1)

    def conv_chunk(ci, carry):
        t0 = ci * CONV_STEPS
        acc = jnp.zeros((CONV_STEPS, PAIR_TILES, LANES), _F32)
        for j in range(CONV_WIDTH):
            lo = pl.multiple_of((t0 + first + j) * CH_TILES, CH_TILES)
            xw = pltpu.bitcast(hist_ref[pl.ds(lo, CONV_STEPS * CH_TILES), :], _BF16)
            acc = acc + (xw.reshape(CONV_STEPS, PAIR_TILES, LANES).astype(_F32)
                         * w16_ref[j][None].astype(_F32))
        y_ref[pl.ds(pl.multiple_of(t0 * PAIR_TILES, PAIR_TILES), CONV_STEPS * PAIR_TILES), :] = (
            (acc + b16_ref[...][None]).reshape(CONV_STEPS * PAIR_TILES, LANES))
        return carry

    def q_piece(lo):
        q = seg(2 * D_CONV + lo, PIECE)
        put(q_ref, lo, q * _head_rms_scale(q) * qg_ref[:, lo:lo + PIECE])

    def k_piece(lo):
        k = seg(2 * D_CONV + Q_DIM + lo, PIECE)
        put(k_ref, lo, k * _head_rms_scale(k) * kg_ref[:, lo:lo + PIECE])

    def v_piece(lo):
        put(v_ref, lo, seg(2 * D_CONV + Q_DIM + KV_DIM + lo, PIECE))

    def gate_piece(ref, base, lo):
        put(ref, lo, jax.nn.sigmoid(seg(base + lo, PIECE)))

    gate_base = 2 * D_CONV + Q_DIM + 2 * KV_DIM
    pieces = ([functools.partial(q_piece, lo) for lo in range(0, Q_DIM, PIECE)]
              + [functools.partial(k_piece, lo) for lo in range(0, KV_DIM, PIECE)]
              + [functools.partial(v_piece, lo) for lo in range(0, KV_DIM, PIECE)]
              + [functools.partial(gate_piece, ga_ref, gate_base, lo) for lo in range(0, D_MODEL, PIECE)]
              + [functools.partial(gate_piece, gb_ref, gate_base + D_MODEL, lo) for lo in range(0, D_MODEL, PIECE)])

    for piece in pieces:
        piece()
    lax.fori_loop(0, tm // CONV_STEPS, conv_chunk, 0)

    y = jnp.concatenate(
        [jnp.concatenate([y_ref[pl.ds(2 * c + s, tm, stride=PAIR_TILES), :] for c in range(CH_TILES)], axis=1)
         for s in range(2)], axis=0)
    out = _ln_swish_project(y, lng_ref, lnb_ref, wo_ref, bo_ref)
    conv_ref[0] = out[:tm]
    conv_ref[1] = out[tm:]


def _inproj_conv(x, g, w_in_b, qg, kg, w_dw, b_dw, lng, lnb, wo_b, bo, tm):
    bsz, t, _ = x.shape
    assert CH_TILES == SUBLANES and bsz % 2 == 0 and t % tm == 0
    nt = t // tm
    w16 = jnp.repeat(w_dw.reshape(CONV_WIDTH, CH_TILES, LANES), 2, axis=1).astype(_BF16)
    b16 = jnp.repeat(b_dw.reshape(CH_TILES, LANES), 2, axis=0)
    full = _resident
    blk = lambda w: pl.BlockSpec((2, tm, w), lambda p, i: (p, i, 0))
    shape = lambda w, dt: jax.ShapeDtypeStruct((bsz, t, w), dt)
    outs = pl.pallas_call(
        functools.partial(_inproj_conv_kernel, tm=tm),
        grid=(bsz // 2, nt),
        in_specs=[blk(D_MODEL), full(g), full(w_in_b), full(qg), full(kg), full(w16), full(b16), full(lng), full(lnb),
                  full(wo_b), full(bo)],
        out_specs=[blk(Q_DIM), blk(KV_DIM), blk(KV_DIM), blk(D_MODEL), blk(D_MODEL), blk(D_MODEL),
                   pl.BlockSpec((2, HALO, D_CONV), lambda p, i: (p, 0, 0))],
        out_shape=[shape(Q_DIM, _BF16), shape(KV_DIM, _F32), shape(KV_DIM, _F32), shape(D_MODEL, _BF16),
                   shape(D_MODEL, _BF16), shape(D_MODEL, _BF16), jax.ShapeDtypeStruct((bsz, HALO, D_CONV), _F32)],
        scratch_shapes=[pltpu.VMEM(((tm + HALO) * CH_TILES, LANES), jnp.uint32),
                        pltpu.VMEM((tm * PAIR_TILES, LANES), _F32)],
        compiler_params=pltpu.CompilerParams(dimension_semantics=("arbitrary", "arbitrary"),
                                             vmem_limit_bytes=VMEM_LIMIT),
        name="inproj_conv",
    )(x, g, w_in_b, qg, kg, w16, b16, lng, lnb, wo_b, bo)
    return [o.reshape(bsz * t, o.shape[-1]) for o in outs[:-1]] + [outs[-1]]


def _conv_sample_kernel(state_ref, glu_ref, wdw_ref, bdw_ref, lng_ref, lnb_ref, wo_ref, bo_ref,
                        out_ref, state_out_ref):
    keep, steps = state_ref.shape[0], glu_ref.shape[0]

    def hist(u):
        return state_ref[u] if u < keep else glu_ref[u - keep]

    for t in range(steps):
        acc = hist(t) * wdw_ref[0:1, :]
        for j in range(1, CONV_WIDTH):
            acc = acc + hist(t + j) * wdw_ref[j:j + 1, :]
        out_ref[t] = _ln_swish_project(acc + bdw_ref[...], lng_ref, lnb_ref, wo_ref, bo_ref)
    state_out_ref[0:keep - steps] = state_ref[steps:keep]
    state_out_ref[keep - steps:keep] = glu_ref[...]


def _conv_sample(state_t, glu_t, w_dw, b_dw, lng, lnb, wo_b, bo, sb):
    keep, nseq, _ = state_t.shape
    steps = glu_t.shape[0]
    full = _resident
    blk = lambda r: pl.BlockSpec((r, sb, D_CONV), lambda i: (0, i, 0))
    return pl.pallas_call(
        _conv_sample_kernel,
        grid=(nseq // sb,),
        in_specs=[blk(keep), blk(steps), full(w_dw), full(b_dw), full(lng), full(lnb), full(wo_b), full(bo)],
        out_specs=[blk(steps), blk(keep)],
        out_shape=[jax.ShapeDtypeStruct((steps, nseq, D_MODEL), _BF16),
                   jax.ShapeDtypeStruct(state_t.shape, _F32)],
        compiler_params=pltpu.CompilerParams(dimension_semantics=("arbitrary",),
                                             vmem_limit_bytes=VMEM_LIMIT),
        name="conv_sample",
    )(state_t, glu_t, w_dw, b_dw, lng, lnb, wo_b, bo)


def _bucket_map():
    i = np.arange(WINDOW)[:, None]
    j = np.arange(WINDOW)[None, :]
    n = (i - j) % WINDOW
    nf = np.maximum(n, 1).astype(np.float32)
    large = MAX_EXACT + (np.log(nf / np.float32(MAX_EXACT)) / np.float32(math.log(MAX_DISTANCE / MAX_EXACT))
                         * np.float32(N_BUCKETS - MAX_EXACT)).astype(np.int32)
    return np.where(n < MAX_EXACT, n, np.minimum(large, N_BUCKETS - 1)).astype(np.int32)


def _bias_table_kernel(rb_ref, bm_ref, tbl_ref):
    p = pl.program_id(0)
    bm = bm_ref[...]
    for half in range(2):
        h = 2 * p + half
        t = jnp.zeros(bm.shape, _F32)
        for b in range(N_BUCKETS):
            t = jnp.where(bm == b, rb_ref[b, h], t)
        tbl_ref[0, :, half * WINDOW:(half + 1) * WINDOW] = t


def _bias_tables(rel_bias):
    bm = jnp.asarray(_bucket_map())
    return pl.pallas_call(
        _bias_table_kernel,
        grid=(N_PAIRS,),
        in_specs=[pl.BlockSpec(memory_space=pltpu.SMEM), pl.BlockSpec(bm.shape, lambda p: (0, 0))],
        out_specs=pl.BlockSpec((1, WINDOW, 2 * WINDOW), lambda p: (p, 0, 0)),
        out_shape=jax.ShapeDtypeStruct((N_PAIRS, WINDOW, 2 * WINDOW), _F32),
        name="bias_tables",
    )(rel_bias, bm)


def _block_diag_pairs(slab):
    low = lax.broadcasted_iota(jnp.int32, slab.shape, 1) < HEAD_DIM
    swapped = pltpu.roll(slab, HEAD_DIM, axis=1)
    zero = jnp.zeros_like(slab)
    first = jnp.concatenate([jnp.where(low, slab, zero), jnp.where(low, zero, swapped)], axis=0)
    second = jnp.concatenate([jnp.where(low, swapped, zero), jnp.where(low, zero, slab)], axis=0)
    return first.astype(_BF16), second.astype(_BF16)


def _kv_operands(k_blk, v_blk):
    ops = []
    for slab in range(KV_DIM // LANES):
        cols = slice(slab * LANES, (slab + 1) * LANES)
        ops.extend(zip(_block_diag_pairs(k_blk[:, cols]), _block_diag_pairs(v_blk[:, cols])))
    return ops


def _attend(q, prev_ops, own_ops, tbl_ref, sink_ref, prev_shift, store, transposed=False):
    tq = q.shape[0]
    rows = 2 * tq
    row = lax.broadcasted_iota(jnp.int32, (rows, 2 * WINDOW), 0)
    col = lax.broadcasted_iota(jnp.int32, (rows, 2 * WINDOW), 1)
    from_prev = (col & (WINDOW - 1)) > jnp.where(row >= tq, row - tq, row)
    top = lax.broadcasted_iota(jnp.int32, (rows, 1), 0) < tq
    low = lax.broadcasted_iota(jnp.int32, (rows, LANES), 1) < HEAD_DIM
    contract_last = (((1,), (1,)), ((), ()))

    def logits(a, k_op):
        return _dot(a, k_op) if transposed else lax.dot_general(a, k_op, contract_last, preferred_element_type=_F32)

    def weighted_values(pr, v_op):
        return lax.dot_general(pr, v_op, contract_last, preferred_element_type=_F32) if transposed else _dot(pr, v_op)

    for kvh in range(N_KV_HEADS):
        (k_prev, v_prev), (k_own, v_own) = prev_ops[kvh], own_ops[kvh]
        pair_a = 2 * kvh
        pair_b = pair_a + 1
        qq = jnp.concatenate([q[:, pair_a * LANES:(pair_a + 1) * LANES],
                              q[:, pair_b * LANES:(pair_b + 1) * LANES]], axis=0).astype(_BF16)
        sp = logits(qq, k_prev)
        so = logits(qq, k_own)
        bias = jnp.concatenate([tbl_ref[pair_a, 0:tq, :], tbl_ref[pair_b, 0:tq, :]], axis=0)
        s = jnp.where(from_prev, sp + prev_shift, so) + bias
        sink_even = jnp.where(top, sink_ref[2 * pair_a], sink_ref[2 * pair_b])
        sink_odd = jnp.where(top, sink_ref[2 * pair_a + 1], sink_ref[2 * pair_b + 1])
        m_even = jnp.maximum(jnp.max(s[:, :WINDOW], axis=-1, keepdims=True), sink_even)
        m_odd = jnp.maximum(jnp.max(s[:, WINDOW:], axis=-1, keepdims=True), sink_odd)
        p = jnp.exp(s - jnp.where(col < WINDOW, m_even, m_odd)).astype(_BF16)
        zero = jnp.zeros_like(p)
        o = (weighted_values(jnp.where(from_prev, p, zero), v_prev)
             + weighted_values(jnp.where(from_prev, zero, p), v_own))
        pf = p.astype(_F32)
        sums = jnp.where(low, jnp.sum(pf[:, :WINDOW], axis=-1, keepdims=True),
                         jnp.sum(pf[:, WINDOW:], axis=-1, keepdims=True))
        den = sums + jnp.where(low, jnp.exp(sink_even - m_even), jnp.exp(sink_odd - m_odd))
        o = o / den
        store(pair_a, o[:tq])
        store(pair_b, o[tq:])


PROMPT_QBLOCKS = 4


def _attn_prompt_kernel(sink_ref, q_ref, kp_ref, ko_ref, vp_ref, vo_ref, tbl_ref, o_ref):
    prev_shift = jnp.where(pl.program_id(1) == 0, MASK_VALUE, 0.0).astype(_F32)
    ops = [_kv_operands(kp_ref[...], vp_ref[...])]
    for b in range(PROMPT_QBLOCKS):
        rows = slice(b * WINDOW, (b + 1) * WINDOW)
        ops.append(_kv_operands(ko_ref[rows, :], vo_ref[rows, :]))

        def store(pair, o, rows=rows):
            o_ref[rows, pair * LANES:(pair + 1) * LANES] = o.astype(o_ref.dtype)

        _attend(q_ref[rows, :], ops[b], ops[b + 1], tbl_ref, sink_ref,
                prev_shift if b == 0 else jnp.float32(0.0), store)


def _attn_prompt(q, k, v, tbl, sinks, bsz, t):
    tq = PROMPT_QBLOCKS * WINDOW
    nb = t // tq
    own = lambda w: pl.BlockSpec((tq, w), lambda b, i: (b * nb + i, 0))
    prev = lambda w: pl.BlockSpec((WINDOW, w),
                                  lambda b, i: (PROMPT_QBLOCKS * (b * nb + i) - jnp.minimum(i, 1), 0))
    return pl.pallas_call(
        _attn_prompt_kernel,
        grid=(bsz, nb),
        in_specs=[pl.BlockSpec(memory_space=pltpu.SMEM), own(Q_DIM), prev(KV_DIM), own(KV_DIM),
                  prev(KV_DIM), own(KV_DIM), pl.BlockSpec(tbl.shape, lambda b, i: (0, 0, 0))],
        out_specs=own(Q_DIM),
        out_shape=jax.ShapeDtypeStruct((bsz * t, Q_DIM), _BF16),
        compiler_params=pltpu.CompilerParams(dimension_semantics=("arbitrary", "arbitrary"),
                                             vmem_limit_bytes=VMEM_LIMIT),
        name="attn_prompt",
    )(sinks, q, k, k, v, v, tbl)


SAMPLE_UNROLL = 2


def _block_diag_t(x):
    xb = x.astype(_BF16)
    z = jnp.zeros_like(xb)
    return jnp.concatenate([jnp.concatenate([xb, z], axis=1), jnp.concatenate([z, xb], axis=1)], axis=0)


def _attn_sample_kernel(sink_ref, q_ref, kn_ref, vn_ref, ck_ref, cv_ref, tbl_ref, o_ref, cko_ref, cvo_ref,
                        *, sb, steps):
    pad = jnp.zeros((WINDOW - steps, LANES), _F32)
    lane = lax.broadcasted_iota(jnp.int32, (HEAD_DIM, WINDOW), 1)

    def one_sequence(s, carry):
        def store(pair, o):
            o_ref[s, :, pair * LANES:(pair + 1) * LANES] = o

        prev_ops, own_ops = [], []
        for slab in range(KV_DIM // LANES):
            cols = slice(slab * LANES, (slab + 1) * LANES)
            new_k = jnp.concatenate([kn_ref[s][:, cols], pad], axis=0).T
            new_v = jnp.concatenate([vn_ref[s][:, cols], pad], axis=0).T
            for sub in range(2):
                kvh = 2 * slab + sub
                part = slice(sub * HEAD_DIM, (sub + 1) * HEAD_DIM)
                kt, vt = ck_ref[s, kvh], cv_ref[s, kvh]
                cko_ref[s, kvh] = pltpu.roll(jnp.where(lane < steps, new_k[part], kt), WINDOW - steps, axis=1)
                cvo_ref[s, kvh] = pltpu.roll(jnp.where(lane < steps, new_v[part], vt), WINDOW - steps, axis=1)
                prev_ops.append((_block_diag_t(kt), _block_diag_t(vt)))
                own_ops.append((_block_diag_t(new_k[part]), _block_diag_t(new_v[part])))
        _attend(q_ref[s], prev_ops, own_ops, tbl_ref, sink_ref, jnp.float32(0.0), store, transposed=True)
        return carry

    lax.fori_loop(0, sb, one_sequence, 0, unroll=SAMPLE_UNROLL)


def _attn_sample(q, k_new, v_new, cache_kt, cache_vt, tbl, sinks, sb):
    nseq, steps, _ = q.shape
    seq = lambda r, w: pl.BlockSpec((sb, r, w), lambda i: (i, 0, 0))
    cache = pl.BlockSpec((sb, N_KV_HEADS, HEAD_DIM, WINDOW), lambda i: (i, 0, 0, 0))
    return pl.pallas_call(
        functools.partial(_attn_sample_kernel, sb=sb, steps=steps),
        grid=(nseq // sb,),
        in_specs=[pl.BlockSpec(memory_space=pltpu.SMEM), seq(steps, Q_DIM), seq(steps, KV_DIM), seq(steps, KV_DIM),
                  cache, cache, pl.BlockSpec(tbl.shape, lambda i: (0, 0, 0))],
        out_specs=[seq(steps, Q_DIM), cache, cache],
        out_shape=[jax.ShapeDtypeStruct((nseq, steps, Q_DIM), _F32),
                   jax.ShapeDtypeStruct(cache_kt.shape, _F32), jax.ShapeDtypeStruct(cache_vt.shape, _F32)],
        compiler_params=pltpu.CompilerParams(dimension_semantics=("arbitrary",),
                                             vmem_limit_bytes=VMEM_LIMIT),
        name="attn_sample",
    )(sinks, q, k_new, v_new, cache_kt, cache_vt, tbl)


def _lane_min_index(mask, lane):
    return jnp.min(jnp.where(mask, lane, float(LANES)), axis=-1, keepdims=True)


def _finish_kernel(x_ref, conv_ref, o_ref, ga_ref, gb_ref, wa_ref, wo_ref, ng_ref, wr_ref, br_ref,
                   tri_ref, h_ref, hn_ref, route_ref, count_ref, running_ref):
    @pl.when(pl.program_id(0) == 0)
    def _():
        running_ref[...] = jnp.zeros_like(running_ref)

    attn_out = _dot(o_ref[...].astype(_BF16), wa_ref[...])
    merged = ga_ref[...].astype(_F32) * conv_ref[...].astype(_F32) + gb_ref[...].astype(_F32) * attn_out
    h = x_ref[...] + _dot(merged.astype(_BF16), wo_ref[...])
    h_ref[...] = h
    hn = h * lax.rsqrt(jnp.mean(h * h, axis=-1, keepdims=True) + EPS) * ng_ref[...]
    hn_ref[...] = _pack_bf16_pairs(hn)

    logits = _dot(hn.astype(_BF16), wr_ref[...]) + br_ref[...]
    lane = lax.broadcasted_iota(jnp.int32, logits.shape, 1).astype(_F32)
    gmask = lane < N_GROUPS
    gl = jnp.where(gmask, logits, MASK_VALUE)
    gmax = jnp.max(gl, axis=-1, keepdims=True)
    grp = _lane_min_index(gmask & (gl == gmax), lane)
    p_grp = 1.0 / jnp.sum(jnp.where(gmask, jnp.exp(gl - gmax), 0.0), axis=-1, keepdims=True)
    e_lo = N_GROUPS + grp * EXPERTS_PER_GROUP
    emask = (lane >= e_lo) & (lane < e_lo + EXPERTS_PER_GROUP)
    el = jnp.where(emask, logits, MASK_VALUE)
    ex = jnp.where(emask, jnp.exp(el - jnp.max(el, axis=-1, keepdims=True)), 0.0)
    prob = jnp.where(emask, ex / jnp.sum(ex, axis=-1, keepdims=True), -1.0)
    p1 = jnp.max(prob, axis=-1, keepdims=True)
    i1 = _lane_min_index(prob == p1, lane)
    rest = jnp.where(lane == i1, -1.0, prob)
    p2 = jnp.max(rest, axis=-1, keepdims=True)
    i2 = _lane_min_index(rest == p2, lane)
    w1 = p_grp * p1 / (p1 + p2)
    w2 = p_grp * p2 / (p1 + p2)
    e1 = i1 - N_GROUPS
    e2 = i2 - N_GROUPS

    hot1 = lane == e1
    hot2 = lane == e2
    hot = jnp.where(hot1 | hot2, 1.0, 0.0)
    before = _dot(tri_ref[...], hot.astype(_BF16)) + running_ref[...]
    rank1 = jnp.sum(jnp.where(hot1, before, 0.0), axis=-1, keepdims=True)
    rank2 = jnp.sum(jnp.where(hot2, before, 0.0), axis=-1, keepdims=True)
    running_ref[...] += jnp.sum(hot, axis=0, keepdims=True)
    count_ref[...] = jnp.broadcast_to(running_ref[...], count_ref.shape)

    fields = (e1, e2, w1, w2, rank1, rank2)
    route = jnp.zeros(logits.shape, _F32)
    for pos, val in enumerate(fields):
        route = jnp.where(lane == pos, val, route)
    route_ref[...] = route


ROUTE_E, ROUTE_W, ROUTE_RANK = 0, 2, 4
DEST_ROWS = 1024


def _finish(x, conv_out, o, ga, gb, wa_b, wo_b, ng, wr_b, br, tm):
    n = x.shape[0]
    tri = jnp.asarray(np.tril(np.ones((tm, tm), np.float32), -1), _BF16)
    row = lambda w: pl.BlockSpec((tm, w), lambda i: (i, 0))
    full = _resident
    return pl.pallas_call(
        _finish_kernel,
        grid=(n // tm,),
        in_specs=[row(D_MODEL), row(D_MODEL), row(Q_DIM), row(D_MODEL), row(D_MODEL),
                  full(wa_b), full(wo_b), full(ng), full(wr_b), full(br), full(tri)],
        out_specs=[row(D_MODEL), row(HALF), row(LANES), pl.BlockSpec((SUBLANES, LANES), lambda i: (0, 0))],
        out_shape=[jax.ShapeDtypeStruct((n, D_MODEL), _F32),
                   jax.ShapeDtypeStruct((n, HALF), jnp.uint32),
                   jax.ShapeDtypeStruct((n, LANES), _F32),
                   jax.ShapeDtypeStruct((SUBLANES, LANES), _F32)],
        scratch_shapes=[pltpu.VMEM((1, LANES), _F32)],
        compiler_params=pltpu.CompilerParams(dimension_semantics=("arbitrary",),
                                             vmem_limit_bytes=VMEM_LIMIT),
        name="finish",
    )(x, conv_out, o, ga, gb, wa_b, wo_b, ng, wr_b, br, tri)


def _dest_kernel(route_ref, starts_ref, dest_ref):
    route = route_ref[...]
    lane = lax.broadcasted_iota(jnp.int32, route.shape, 1)
    out = jnp.zeros(route.shape, jnp.int32)
    for j in range(TOP_K):
        e = route[:, ROUTE_E + j:ROUTE_E + j + 1].astype(jnp.int32)
        start = jnp.sum(jnp.where(lane == e, starts_ref[...], 0.0), axis=-1, keepdims=True)
        d = (start + route[:, ROUTE_RANK + j:ROUTE_RANK + j + 1]).astype(jnp.int32)
        out = jnp.where(lane == j, d, out)
    dest_ref[...] = out


def _dest(route, starts_row, tm):
    n = route.shape[0]
    row = pl.BlockSpec((tm, LANES), lambda i: (i, 0))
    return pl.pallas_call(
        _dest_kernel,
        grid=(n // tm,),
        in_specs=[row, pl.BlockSpec((1, LANES), lambda i: (0, 0))],
        out_specs=row,
        out_shape=jax.ShapeDtypeStruct((n, LANES), jnp.int32),
        name="dest",
    )(route, starts_row)


SC_CORES = 2
SC_SUBCORES = 16
SC_WORKERS = SC_CORES * SC_SUBCORES
SC_IN_FLIGHT = 4
SC_CHUNK_BYTES = 64 * 1024


def _sc_move_rows(src, idx, gather):
    n, d = src.shape
    b = idx.shape[0]
    per_worker = b // SC_WORKERS
    assert per_worker * SC_WORKERS == b and (gather or n % per_worker == 0), (b, n)
    chunk = min(per_worker // SC_IN_FLIGHT, SC_CHUNK_BYTES // (d * 4))
    n_iters = per_worker // (chunk * SC_IN_FLIGHT)
    assert n_iters * chunk * SC_IN_FLIGHT == per_worker and chunk % SUBLANES == 0, (per_worker, chunk)
    mesh = plsc.VectorSubcoreMesh(core_axis_name="c", subcore_axis_name="s",
                                  num_cores=SC_CORES, num_subcores=SC_SUBCORES)
    scratch = ([pltpu.VMEM((chunk,), jnp.int32)] * SC_IN_FLIGHT + [pltpu.VMEM((chunk, d), src.dtype)] * SC_IN_FLIGHT
               + [pltpu.SemaphoreType.DMA] * SC_IN_FLIGHT)

    @functools.partial(pl.kernel, mesh=mesh, out_type=jax.ShapeDtypeStruct((b, d), src.dtype),
                       scratch_types=scratch, name="sc_gather_rows" if gather else "sc_scatter_rows",
                       cost_estimate=pl.CostEstimate(flops=0, transcendentals=0, bytes_accessed=2 * b * d * 4 + b * 4))
    def move(src_hbm, idx_hbm, out_hbm, *bufs):
        idx_v = bufs[:SC_IN_FLIGHT]
        rows_v = bufs[SC_IN_FLIGHT:2 * SC_IN_FLIGHT]
        sems = bufs[2 * SC_IN_FLIGHT:]
        worker = lax.axis_index("s") * SC_CORES + lax.axis_index("c")

        @pl.loop(0, n_iters)
        def _(it):
            bases = [pl.multiple_of(worker * per_worker + (it * SC_IN_FLIGHT + j) * chunk, chunk)
                     for j in range(SC_IN_FLIGHT)]
            loads = [pltpu.async_copy(idx_hbm.at[pl.ds(bases[j], chunk)], idx_v[j], sems[j])
                     for j in range(SC_IN_FLIGHT)]
            reads = []
            for j in range(SC_IN_FLIGHT):
                loads[j].wait()
                if gather:
                    rows = src_hbm.at[idx_v[j]]
                else:
                    rows = src_hbm.at[pl.ds(pl.multiple_of(lax.rem(bases[j], n), chunk), chunk)]
                reads.append(pltpu.async_copy(rows, rows_v[j], sems[j]))
            writes = []
            for j in range(SC_IN_FLIGHT):
                reads[j].wait()
                dst = out_hbm.at[pl.ds(bases[j], chunk)] if gather else out_hbm.at[idx_v[j]]
                writes.append(pltpu.async_copy(rows_v[j], dst, sems[j]))
            for w in writes:
                w.wait()

    return move(src, idx)


def _sc_gather_rows(table, idx):
    return _sc_move_rows(table, idx, gather=True)


def _sc_scatter_rows(src, idx):
    return _sc_move_rows(src, idx, gather=False)


def _expert_kernel(blk_ref, exp_ref, lo_ref, hi_ref, x_ref, wg_ref, wu_ref, wd_ref, yb_ref,
                   wg_b, wu_b, wd_b, held_ref):
    del blk_ref
    k = pl.program_id(0)
    lo, hi, e = lo_ref[k], hi_ref[k], exp_ref[k]

    @pl.when(k == 0)
    def _():
        held_ref[0] = -1

    @pl.when(hi > lo)
    def _():
        @pl.when(held_ref[0] != e)
        def _():
            wg_b[...] = wg_ref[0].astype(_BF16)
            wu_b[...] = wu_ref[0].astype(_BF16)
            wd_b[...] = wd_ref[0].astype(_BF16)
            held_ref[0] = e

        xb = _unpack_bf16_pairs(x_ref[...]).astype(_BF16)
        g = _dot(xb, wg_b[...])
        u = _dot(xb, wu_b[...])
        hid = g * jax.nn.sigmoid(g) * u
        y = _dot(hid.astype(_BF16), wd_b[...])
        r = lax.broadcasted_iota(jnp.int32, yb_ref.shape, 0)
        pltpu.store(yb_ref, _pack_bf16_pairs(y), mask=(r >= lo) & (r < hi))


def _experts(items, xs, w_gate, w_up, w_down):
    n_items = items[0].shape[0]
    wspec = lambda a: pl.BlockSpec((1,) + a.shape[1:], lambda k, blk, exp, lo, hi: (exp[k], 0, 0))
    rows = pl.BlockSpec((MOE_ROWS, HALF), lambda k, blk, exp, lo, hi: (blk[k], 0))
    grid_spec = pltpu.PrefetchScalarGridSpec(
        num_scalar_prefetch=4,
        grid=(n_items,),
        in_specs=[rows, wspec(w_gate), wspec(w_up), wspec(w_down)],
        out_specs=rows,
        scratch_shapes=[pltpu.VMEM(w_gate.shape[1:], _BF16), pltpu.VMEM(w_up.shape[1:], _BF16),
                        pltpu.VMEM(w_down.shape[1:], _BF16), pltpu.SMEM((1,), jnp.int32)],
    )
    return pl.pallas_call(
        _expert_kernel,
        grid_spec=grid_spec,
        out_shape=jax.ShapeDtypeStruct(xs.shape, xs.dtype),
        compiler_params=pltpu.CompilerParams(dimension_semantics=("arbitrary",),
                                             vmem_limit_bytes=VMEM_LIMIT),
        cost_estimate=pl.CostEstimate(
            flops=n_items * MOE_ROWS * 6 * D_MODEL * D_EXPERT, transcendentals=n_items * MOE_ROWS * D_EXPERT,
            bytes_accessed=2 * xs.size * 4 + (w_gate.size + w_up.size + w_down.size) * 4),
        name="experts",
    )(*items, xs, w_gate, w_up, w_down)


def _combine_kernel(h_ref, route_ref, g0_ref, g1_ref, y_ref):
    route = route_ref[...]
    y_ref[...] = (h_ref[...] + route[:, ROUTE_W:ROUTE_W + 1] * _unpack_bf16_pairs(g0_ref[...])
                  + route[:, ROUTE_W + 1:ROUTE_W + 2] * _unpack_bf16_pairs(g1_ref[...]))


def _combine(h, route, g, tm):
    n = h.shape[0]
    nt = n // tm
    row = lambda w: pl.BlockSpec((tm, w), lambda i: (i, 0))
    return pl.pallas_call(
        _combine_kernel,
        grid=(nt,),
        in_specs=[row(D_MODEL), row(LANES), row(HALF), pl.BlockSpec((tm, HALF), lambda i: (nt + i, 0))],
        out_specs=row(D_MODEL),
        out_shape=jax.ShapeDtypeStruct((n, D_MODEL), _F32),
        compiler_params=pltpu.CompilerParams(dimension_semantics=("arbitrary",),
                                             vmem_limit_bytes=VMEM_LIMIT),
        name="combine",
    )(h, route, g, g)


def _work_items(counts, n_pairs):
    n_blocks = n_pairs // MOE_ROWS
    starts = jnp.cumsum(counts) - counts
    cuts = jnp.sort(jnp.concatenate([jnp.arange(n_blocks, dtype=jnp.int32) * MOE_ROWS, starts]))
    ends = jnp.concatenate([cuts[1:], jnp.full((1,), n_pairs, jnp.int32)])
    blk = jnp.minimum(cuts // MOE_ROWS, n_blocks - 1)
    expert = jnp.clip(jnp.sum(starts[None, :] <= cuts[:, None], axis=1) - 1, 0, N_EXPERTS - 1).astype(jnp.int32)
    return starts, (blk, expert, cuts - blk * MOE_ROWS, ends - blk * MOE_ROWS)


def _dispatch(hn, route, counts_rows):
    n = hn.shape[0]
    n_pairs = n * TOP_K
    counts = counts_rows[0, :N_EXPERTS].astype(jnp.int32)
    starts, items = _work_items(counts, n_pairs)
    starts_row = jnp.zeros((1, LANES), _F32).at[0, :N_EXPERTS].set(starts.astype(_F32))
    dest = _dest(route, starts_row, min(n, DEST_ROWS))[:, :TOP_K].T.reshape(n_pairs)
    return _sc_scatter_rows(hn, dest), dest, items


def _after(value, *earlier):
    return lax.optimization_barrier((value,) + earlier)[0]


def kernel(x_prompt, x_sample, state_conv, cache_k, cache_v, norm_attn_g, w_in, q_norm_g, k_norm_g, rel_bias, attn_sinks, w_dw, b_dw, conv_ln_g, conv_ln_b, w_conv_out, b_conv_out, w_attn_out, w_out, norm_ffn_g, w_grp, b_grp, w_router, b_router, w_gate, w_up, w_down):
    bsz, t, _ = x_prompt.shape
    nseq, steps, _ = x_sample.shape
    row = lambda a: a.reshape(1, -1).astype(_F32)

    w_in_b = w_in.astype(_BF16)
    wco_b = w_conv_out.astype(_BF16)
    wa_b = w_attn_out.astype(_BF16)
    wo_b = w_out.astype(_BF16)
    qg = row(jnp.tile(q_norm_g, N_HEADS)) * (HEAD_DIM ** -0.5)
    kg = row(jnp.tile(k_norm_g, N_KV_HEADS))
    w_rt = jnp.zeros((D_MODEL, LANES), _F32).at[:, :N_GROUPS].set(w_grp).at[:, N_GROUPS:N_GROUPS + N_EXPERTS].set(w_router)
    wr_b = w_rt.astype(_BF16)
    b_rt = jnp.zeros((1, LANES), _F32).at[0, :N_GROUPS].set(b_grp).at[0, N_GROUPS:N_GROUPS + N_EXPERTS].set(b_router)
    tbl = _bias_tables(rel_bias)
    conv_params = (w_dw, row(b_dw), row(conv_ln_g), row(conv_ln_b), wco_b, row(b_conv_out))

    def finish(x2d, conv_out, o, ga, gb):
        return _finish(x2d, conv_out, o, ga, gb, wa_b, wo_b, row(norm_ffn_g), wr_b, b_rt, ROW_TILE)

    xp = x_prompt.reshape(bsz * t, D_MODEL)
    q, k, v, ga, gb, conv_out, glu_tail = _inproj_conv(x_prompt, row(norm_attn_g), w_in_b, qg, kg, *conv_params,
                                                       ROW_TILE // 2)
    o = _attn_prompt(q, k, v, tbl, attn_sinks, bsz, t)
    h_p, hn_p, route_p, counts_p = finish(xp, conv_out, o, ga, gb)
    rows_p, dest_p, items_p = _dispatch(hn_p, route_p, counts_p)
    state_conv_prompt = glu_tail[:, HALO - (CONV_WIDTH - 1):]
    tail = lambda a: a.reshape(bsz, t, KV_DIM)[:, t - WINDOW:].reshape(bsz, WINDOW, N_KV_HEADS, HEAD_DIM)
    cache_k_prompt, cache_v_prompt = tail(k), tail(v)

    xs = _after(x_sample, dest_p).reshape(nseq * steps, D_MODEL)
    glu, q, k, v, ga, gb = _inproj(xs, row(norm_attn_g), w_in_b, qg, kg, _F32, ROW_TILE)
    glu_t = glu.reshape(nseq, steps, D_CONV).transpose(1, 0, 2)
    conv_out, state_t = _conv_sample(state_conv.transpose(1, 0, 2), glu_t, *conv_params, 64)
    conv_out = conv_out.transpose(1, 0, 2).reshape(nseq * steps, D_MODEL)
    k3 = k.reshape(nseq, steps, KV_DIM)
    v3 = v.reshape(nseq, steps, KV_DIM)
    o, ck_t, cv_t = _attn_sample(q.reshape(nseq, steps, Q_DIM), k3, v3, cache_k.transpose(0, 2, 3, 1),
                                 cache_v.transpose(0, 2, 3, 1), tbl, attn_sinks, 8)
    h_s, hn_s, route_s, counts_s = finish(xs, conv_out, o.reshape(nseq * steps, Q_DIM), ga, gb)
    rows_s, dest_s, items_s = _dispatch(hn_s, route_s, counts_s)
    state_conv_sample = state_t.transpose(1, 0, 2)
    cache_k_sample = ck_t.transpose(0, 3, 1, 2)
    cache_v_sample = cv_t.transpose(0, 3, 1, 2)

    yb_p = _experts(items_p, rows_p, w_gate, w_up, w_down)
    g_p = _sc_gather_rows(yb_p, dest_p)
    yb_s = _experts(items_s, _after(rows_s, yb_p), w_gate, w_up, w_down)
    g_s = _sc_gather_rows(yb_s, dest_s)
    y_prompt = _combine(h_p, route_p, g_p, ROW_TILE).reshape(bsz, t, D_MODEL)
    y_sample = _combine(h_s, route_s, _after(g_s, y_prompt), ROW_TILE).reshape(nseq, steps, D_MODEL)

    return (y_prompt, y_sample, state_conv_prompt, cache_k_prompt, cache_v_prompt,
            state_conv_sample, cache_k_sample, cache_v_sample)
```

```python
import functools
import math

import numpy as np
import jax
import jax.numpy as jnp
from jax import lax
from jax.experimental import pallas as pl
from jax.experimental.pallas import tpu as pltpu
from jax.experimental.pallas import tpu_sc as plsc

D_MODEL = 1024
N_HEADS = 16
HEAD_DIM = 64
N_KV_HEADS = 4
WINDOW = 128
Q_DIM = N_HEADS * HEAD_DIM
KV_DIM = N_KV_HEADS * HEAD_DIM
N_BUCKETS = 32
MAX_EXACT = N_BUCKETS // 2
MAX_DISTANCE = 128
D_CONV = D_MODEL
CONV_WIDTH = 31
N_GROUPS = 4
EXPERTS_PER_GROUP = 8
N_EXPERTS = N_GROUPS * EXPERTS_PER_GROUP
TOP_K = 2
D_EXPERT = 256
EPS = 1e-6

LANES = 128
SUBLANES = 8
N_PAIRS = N_HEADS // 2
MOE_ROWS = 512
MOE_MIN_BLOCKS = 32
MASK_VALUE = -1e30
VMEM_LIMIT = 56 * 1024 * 1024
ROW_TILE = 512

_F32 = jnp.float32
_BF16 = jnp.bfloat16


def _resident(a):
    return pl.BlockSpec(a.shape, lambda *_: (0,) * a.ndim, pipeline_mode=pl.Buffered(1))


def _dot(a, b):
    return jnp.dot(a, b, preferred_element_type=_F32)


HALF = D_MODEL // 2


def _pack_pair(lo, hi):
    lo_bits = pltpu.bitcast(lo.astype(_BF16).astype(_F32), jnp.uint32)
    hi_bits = pltpu.bitcast(hi.astype(_BF16).astype(_F32), jnp.uint32)
    return hi_bits | (lo_bits >> 16)


def _pack_bf16_pairs(x):
    return _pack_pair(x[:, :HALF], x[:, HALF:])


def _unpack_bf16_pairs(w):
    lo = pltpu.bitcast(w << 16, _F32)
    hi = pltpu.bitcast(w & jnp.uint32(0xFFFF0000), _F32)
    return jnp.concatenate([lo, hi], axis=1)


def _head_rms_scale(z):
    low = lax.broadcasted_iota(jnp.int32, (z.shape[0], LANES), 1) < HEAD_DIM
    slabs = []
    for c in range(z.shape[1] // LANES):
        sq = z[:, c * LANES:(c + 1) * LANES]
        sq = sq * sq
        first = jnp.sum(jnp.where(low, sq, 0.0), axis=-1, keepdims=True)
        second = jnp.sum(jnp.where(low, 0.0, sq), axis=-1, keepdims=True)
        slabs.append(lax.rsqrt(jnp.where(low, first, second) * (1.0 / HEAD_DIM) + EPS))
    return jnp.concatenate(slabs, axis=1)


def _inproj_kernel(x_ref, g_ref, w_ref, qg_ref, kg_ref,
                   glu_ref, q_ref, k_ref, v_ref, ga_ref, gb_ref):
    x = x_ref[...]
    xn = x * lax.rsqrt(jnp.mean(x * x, axis=-1, keepdims=True) + EPS) * g_ref[...]
    xb = xn.astype(_BF16)

    def seg(lo, width):
        return _dot(xb, w_ref[:, lo:lo + width])

    a = seg(0, D_CONV)
    b = seg(D_CONV, D_CONV)
    glu_ref[...] = a * jax.nn.sigmoid(b)
    off = 2 * D_CONV
    q = seg(off, Q_DIM)
    q_ref[...] = (q * _head_rms_scale(q) * qg_ref[...]).astype(q_ref.dtype)
    off += Q_DIM
    k = seg(off, KV_DIM)
    k_ref[...] = k * _head_rms_scale(k) * kg_ref[...]
    off += KV_DIM
    v_ref[...] = seg(off, KV_DIM)
    off += KV_DIM
    ga_ref[...] = jax.nn.sigmoid(seg(off, D_MODEL)).astype(ga_ref.dtype)
    off += D_MODEL
    gb_ref[...] = jax.nn.sigmoid(seg(off, D_MODEL)).astype(gb_ref.dtype)


def _inproj(x, g, w_in_b, qg, kg, q_dtype, tm):
    n = x.shape[0]
    in_dim = w_in_b.shape[1]
    row = lambda w: pl.BlockSpec((tm, w), lambda i: (i, 0))
    full = _resident
    return pl.pallas_call(
        _inproj_kernel,
        grid=(n // tm,),
        in_specs=[row(D_MODEL), full(g), full(w_in_b), full(qg), full(kg)],
        out_specs=[row(D_CONV), row(Q_DIM), row(KV_DIM), row(KV_DIM), row(D_MODEL), row(D_MODEL)],
        out_shape=[jax.ShapeDtypeStruct((n, D_CONV), _F32),
                   jax.ShapeDtypeStruct((n, Q_DIM), q_dtype),
                   jax.ShapeDtypeStruct((n, KV_DIM), _F32),
                   jax.ShapeDtypeStruct((n, KV_DIM), _F32),
                   jax.ShapeDtypeStruct((n, D_MODEL), _BF16),
                   jax.ShapeDtypeStruct((n, D_MODEL), _BF16)],
        compiler_params=pltpu.CompilerParams(dimension_semantics=("arbitrary",),
                                             vmem_limit_bytes=VMEM_LIMIT),
        name="inproj",
    )(x, g, w_in_b, qg, kg)


def _ln_swish_project(y, lng_ref, lnb_ref, wo_ref, bo_ref):
    mu = jnp.mean(y, axis=-1, keepdims=True)
    yc = y - mu
    var = jnp.mean(yc * yc, axis=-1, keepdims=True)
    z = yc * lax.rsqrt(var + EPS) * lng_ref[...] + lnb_ref[...]
    z = z * jax.nn.sigmoid(z)
    return (_dot(z.astype(_BF16), wo_ref[...]) + bo_ref[...]).astype(_BF16)


HALO = 32
CONV_STEPS = 16
CH_TILES = D_CONV // LANES


PIECE = 256
PAIR_TILES = 2 * CH_TILES


def _inproj_conv_kernel(x_ref, g_ref, w_ref, qg_ref, kg_ref, w16_ref, b16_ref, lng_ref, lnb_ref, wo_ref, bo_ref,
                        q_ref, k_ref, v_ref, ga_ref, gb_ref, conv_ref, tail_ref, hist_ref, y_ref, *, tm):
    i = pl.program_id(1)
    rows = 2 * tm

    @pl.when(i == 0)
    def _():
        hist_ref[0:HALO * CH_TILES, :] = jnp.zeros((HALO * CH_TILES, LANES), jnp.uint32)

    @pl.when(i > 0)
    def _():
        hist_ref[0:HALO * CH_TILES, :] = hist_ref[tm * CH_TILES:(tm + HALO) * CH_TILES, :]

    x = jnp.concatenate([x_ref[0], x_ref[1]], axis=0)
    xn = x * lax.rsqrt(jnp.mean(x * x, axis=-1, keepdims=True) + EPS) * g_ref[...]
    xb = xn.astype(_BF16)

    def seg(lo, width):
        return _dot(xb, w_ref[:, lo:lo + width])

    def put(ref, lo, val):
        ref[0, :, lo:lo + PIECE] = val[:tm].astype(ref.dtype)
        ref[1, :, lo:lo + PIECE] = val[tm:].astype(ref.dtype)

    for p in range(D_CONV // PIECE):
        lo = p * PIECE
        glu = seg(lo, PIECE) * jax.nn.sigmoid(seg(D_CONV + lo, PIECE))
        tail_ref[0, :, lo:lo + PIECE] = glu[tm - HALO:tm, :]
        tail_ref[1, :, lo:lo + PIECE] = glu[rows - HALO:, :]
        words = _pack_pair(glu[:tm], glu[tm:])
        for c in range(PIECE // LANES):
            tile = lo // LANES + c
            hist_ref[pl.ds(HALO * CH_TILES + tile, tm, stride=CH_TILES), :] = words[:, c * LANES:(c + 1) * LANES]

    first = HALO - (CONV_WIDTH - 1)

    def conv_chunk(ci, carry):
        t0 = ci * CONV_STEPS
        acc = jnp.zeros((CONV_STEPS, PAIR_TILES, LANES), _F32)
        for j in range(CONV_WIDTH):
            lo = pl.multiple_of((t0 + first + j) * CH_TILES, CH_TILES)
            xw = pltpu.bitcast(hist_ref[pl.ds(lo, CONV_STEPS * CH_TILES), :], _BF16)
            acc = acc + (xw.reshape(CONV_STEPS, PAIR_TILES, LANES).astype(_F32)
                         * w16_ref[j][None].astype(_F32))
        y_ref[pl.ds(pl.multiple_of(t0 * PAIR_TILES, PAIR_TILES), CONV_STEPS * PAIR_TILES), :] = (
            (acc + b16_ref[...][None]).reshape(CONV_STEPS * PAIR_TILES, LANES))
        return carry

    def q_piece(lo):
        q = seg(2 * D_CONV + lo, PIECE)
        put(q_ref, lo, q * _head_rms_scale(q) * qg_ref[:, lo:lo + PIECE])

    def k_piece(lo):
        k = seg(2 * D_CONV + Q_DIM + lo, PIECE)
        put(k_ref, lo, k * _head_rms_scale(k) * kg_ref[:, lo:lo + PIECE])

    def v_piece(lo):
        put(v_ref, lo, seg(2 * D_CONV + Q_DIM + KV_DIM + lo, PIECE))

    def gate_piece(ref, base, lo):
        put(ref, lo, jax.nn.sigmoid(seg(base + lo, PIECE)))

    gate_base = 2 * D_CONV + Q_DIM + 2 * KV_DIM
    pieces = ([functools.partial(q_piece, lo) for lo in range(0, Q_DIM, PIECE)]
              + [functools.partial(k_piece, lo) for lo in range(0, KV_DIM, PIECE)]
              + [functools.partial(v_piece, lo) for lo in range(0, KV_DIM, PIECE)]
              + [functools.partial(gate_piece, ga_ref, gate_base, lo) for lo in range(0, D_MODEL, PIECE)]
              + [functools.partial(gate_piece, gb_ref, gate_base + D_MODEL, lo) for lo in range(0, D_MODEL, PIECE)])

    for piece in pieces:
        piece()
    lax.fori_loop(0, tm // CONV_STEPS, conv_chunk, 0)

    y = jnp.concatenate(
        [jnp.concatenate([y_ref[pl.ds(2 * c + s, tm, stride=PAIR_TILES), :] for c in range(CH_TILES)], axis=1)
         for s in range(2)], axis=0)
    out = _ln_swish_project(y, lng_ref, lnb_ref, wo_ref, bo_ref)
    conv_ref[0] = out[:tm]
    conv_ref[1] = out[tm:]


def _inproj_conv(x, g, w_in_b, qg, kg, w_dw, b_dw, lng, lnb, wo_b, bo, tm):
    bsz, t, _ = x.shape
    assert CH_TILES == SUBLANES and bsz % 2 == 0 and t % tm == 0
    nt = t // tm
    w16 = jnp.repeat(w_dw.reshape(CONV_WIDTH, CH_TILES, LANES), 2, axis=1).astype(_BF16)
    b16 = jnp.repeat(b_dw.reshape(CH_TILES, LANES), 2, axis=0)
    full = _resident
    blk = lambda w: pl.BlockSpec((2, tm, w), lambda p, i: (p, i, 0))
    shape = lambda w, dt: jax.ShapeDtypeStruct((bsz, t, w), dt)
    outs = pl.pallas_call(
        functools.partial(_inproj_conv_kernel, tm=tm),
        grid=(bsz // 2, nt),
        in_specs=[blk(D_MODEL), full(g), full(w_in_b), full(qg), full(kg), full(w16), full(b16), full(lng), full(lnb),
                  full(wo_b), full(bo)],
        out_specs=[blk(Q_DIM), blk(KV_DIM), blk(KV_DIM), blk(D_MODEL), blk(D_MODEL), blk(D_MODEL),
                   pl.BlockSpec((2, HALO, D_CONV), lambda p, i: (p, 0, 0))],
        out_shape=[shape(Q_DIM, _BF16), shape(KV_DIM, _F32), shape(KV_DIM, _F32), shape(D_MODEL, _BF16),
                   shape(D_MODEL, _BF16), shape(D_MODEL, _BF16), jax.ShapeDtypeStruct((bsz, HALO, D_CONV), _F32)],
        scratch_shapes=[pltpu.VMEM(((tm + HALO) * CH_TILES, LANES), jnp.uint32),
                        pltpu.VMEM((tm * PAIR_TILES, LANES), _F32)],
        compiler_params=pltpu.CompilerParams(dimension_semantics=("arbitrary", "arbitrary"),
                                             vmem_limit_bytes=VMEM_LIMIT),
        name="inproj_conv",
    )(x, g, w_in_b, qg, kg, w16, b16, lng, lnb, wo_b, bo)
    return [o.reshape(bsz * t, o.shape[-1]) for o in outs[:-1]] + [outs[-1]]


def _conv_sample_kernel(state_ref, glu_ref, wdw_ref, bdw_ref, lng_ref, lnb_ref, wo_ref, bo_ref,
                        out_ref, state_out_ref):
    keep, steps = state_ref.shape[0], glu_ref.shape[0]

    def hist(u):
        return state_ref[u] if u < keep else glu_ref[u - keep]

    for t in range(steps):
        acc = hist(t) * wdw_ref[0:1, :]
        for j in range(1, CONV_WIDTH):
            acc = acc + hist(t + j) * wdw_ref[j:j + 1, :]
        out_ref[t] = _ln_swish_project(acc + bdw_ref[...], lng_ref, lnb_ref, wo_ref, bo_ref)
    state_out_ref[0:keep - steps] = state_ref[steps:keep]
    state_out_ref[keep - steps:keep] = glu_ref[...]


def _conv_sample(state_t, glu_t, w_dw, b_dw, lng, lnb, wo_b, bo, sb):
    keep, nseq, _ = state_t.shape
    steps = glu_t.shape[0]
    full = _resident
    blk = lambda r: pl.BlockSpec((r, sb, D_CONV), lambda i: (0, i, 0))
    return pl.pallas_call(
        _conv_sample_kernel,
        grid=(nseq // sb,),
        in_specs=[blk(keep), blk(steps), full(w_dw), full(b_dw), full(lng), full(lnb), full(wo_b), full(bo)],
        out_specs=[blk(steps), blk(keep)],
        out_shape=[jax.ShapeDtypeStruct((steps, nseq, D_MODEL), _BF16),
                   jax.ShapeDtypeStruct(state_t.shape, _F32)],
        compiler_params=pltpu.CompilerParams(dimension_semantics=("arbitrary",),
                                             vmem_limit_bytes=VMEM_LIMIT),
        name="conv_sample",
    )(state_t, glu_t, w_dw, b_dw, lng, lnb, wo_b, bo)


def _bucket_map():
    i = np.arange(WINDOW)[:, None]
    j = np.arange(WINDOW)[None, :]
    n = (i - j) % WINDOW
    nf = np.maximum(n, 1).astype(np.float32)
    large = MAX_EXACT + (np.log(nf / np.float32(MAX_EXACT)) / np.float32(math.log(MAX_DISTANCE / MAX_EXACT))
                         * np.float32(N_BUCKETS - MAX_EXACT)).astype(np.int32)
    return np.where(n < MAX_EXACT, n, np.minimum(large, N_BUCKETS - 1)).astype(np.int32)


def _bias_table_kernel(rb_ref, bm_ref, tbl_ref):
    p = pl.program_id(0)
    bm = bm_ref[...]
    for half in range(2):
        h = 2 * p + half
        t = jnp.zeros(bm.shape, _F32)
        for b in range(N_BUCKETS):
            t = jnp.where(bm == b, rb_ref[b, h], t)
        tbl_ref[0, :, half * WINDOW:(half + 1) * WINDOW] = t


def _bias_tables(rel_bias):
    bm = jnp.asarray(_bucket_map())
    return pl.pallas_call(
        _bias_table_kernel,
        grid=(N_PAIRS,),
        in_specs=[pl.BlockSpec(memory_space=pltpu.SMEM), pl.BlockSpec(bm.shape, lambda p: (0, 0))],
        out_specs=pl.BlockSpec((1, WINDOW, 2 * WINDOW), lambda p: (p, 0, 0)),
        out_shape=jax.ShapeDtypeStruct((N_PAIRS, WINDOW, 2 * WINDOW), _F32),
        name="bias_tables",
    )(rel_bias, bm)


def _block_diag_pairs(slab):
    low = lax.broadcasted_iota(jnp.int32, slab.shape, 1) < HEAD_DIM
    swapped = pltpu.roll(slab, HEAD_DIM, axis=1)
    zero = jnp.zeros_like(slab)
    first = jnp.concatenate([jnp.where(low, slab, zero), jnp.where(low, zero, swapped)], axis=0)
    second = jnp.concatenate([jnp.where(low, swapped, zero), jnp.where(low, zero, slab)], axis=0)
    return first.astype(_BF16), second.astype(_BF16)


def _kv_operands(k_blk, v_blk):
    ops = []
    for slab in range(KV_DIM // LANES):
        cols = slice(slab * LANES, (slab + 1) * LANES)
        ops.extend(zip(_block_diag_pairs(k_blk[:, cols]), _block_diag_pairs(v_blk[:, cols])))
    return ops


def _attend(q, prev_ops, own_ops, tbl_ref, sink_ref, prev_shift, store, transposed=False):
    tq = q.shape[0]
    rows = 2 * tq
    row = lax.broadcasted_iota(jnp.int32, (rows, 2 * WINDOW), 0)
    col = lax.broadcasted_iota(jnp.int32, (rows, 2 * WINDOW), 1)
    from_prev = (col & (WINDOW - 1)) > jnp.where(row >= tq, row - tq, row)
    top = lax.broadcasted_iota(jnp.int32, (rows, 1), 0) < tq
    low = lax.broadcasted_iota(jnp.int32, (rows, LANES), 1) < HEAD_DIM
    contract_last = (((1,), (1,)), ((), ()))

    def logits(a, k_op):
        return _dot(a, k_op) if transposed else lax.dot_general(a, k_op, contract_last, preferred_element_type=_F32)

    def weighted_values(pr, v_op):
        return lax.dot_general(pr, v_op, contract_last, preferred_element_type=_F32) if transposed else _dot(pr, v_op)

    for kvh in range(N_KV_HEADS):
        (k_prev, v_prev), (k_own, v_own) = prev_ops[kvh], own_ops[kvh]
        pair_a = 2 * kvh
        pair_b = pair_a + 1
        qq = jnp.concatenate([q[:, pair_a * LANES:(pair_a + 1) * LANES],
                              q[:, pair_b * LANES:(pair_b + 1) * LANES]], axis=0).astype(_BF16)
        sp = logits(qq, k_prev)
        so = logits(qq, k_own)
        bias = jnp.concatenate([tbl_ref[pair_a, 0:tq, :], tbl_ref[pair_b, 0:tq, :]], axis=0)
        s = jnp.where(from_prev, sp + prev_shift, so) + bias
        sink_even = jnp.where(top, sink_ref[2 * pair_a], sink_ref[2 * pair_b])
        sink_odd = jnp.where(top, sink_ref[2 * pair_a + 1], sink_ref[2 * pair_b + 1])
        m_even = jnp.maximum(jnp.max(s[:, :WINDOW], axis=-1, keepdims=True), sink_even)
        m_odd = jnp.maximum(jnp.max(s[:, WINDOW:], axis=-1, keepdims=True), sink_odd)
        p = jnp.exp(s - jnp.where(col < WINDOW, m_even, m_odd)).astype(_BF16)
        zero = jnp.zeros_like(p)
        o = (weighted_values(jnp.where(from_prev, p, zero), v_prev)
             + weighted_values(jnp.where(from_prev, zero, p), v_own))
        pf = p.astype(_F32)
        sums = jnp.where(low, jnp.sum(pf[:, :WINDOW], axis=-1, keepdims=True),
                         jnp.sum(pf[:, WINDOW:], axis=-1, keepdims=True))
        den = sums + jnp.where(low, jnp.exp(sink_even - m_even), jnp.exp(sink_odd - m_odd))
        o = o / den
        store(pair_a, o[:tq])
        store(pair_b, o[tq:])


PROMPT_QBLOCKS = 4


def _attn_prompt_kernel(sink_ref, q_ref, kp_ref, ko_ref, vp_ref, vo_ref, tbl_ref, o_ref):
    prev_shift = jnp.where(pl.program_id(1) == 0, MASK_VALUE, 0.0).astype(_F32)
    ops = [_kv_operands(kp_ref[...], vp_ref[...])]
    for b in range(PROMPT_QBLOCKS):
        rows = slice(b * WINDOW, (b + 1) * WINDOW)
        ops.append(_kv_operands(ko_ref[rows, :], vo_ref[rows, :]))

        def store(pair, o, rows=rows):
            o_ref[rows, pair * LANES:(pair + 1) * LANES] = o.astype(o_ref.dtype)

        _attend(q_ref[rows, :], ops[b], ops[b + 1], tbl_ref, sink_ref,
                prev_shift if b == 0 else jnp.float32(0.0), store)


def _attn_prompt(q, k, v, tbl, sinks, bsz, t):
    tq = PROMPT_QBLOCKS * WINDOW
    nb = t // tq
    own = lambda w: pl.BlockSpec((tq, w), lambda b, i: (b * nb + i, 0))
    prev = lambda w: pl.BlockSpec((WINDOW, w),
                                  lambda b, i: (PROMPT_QBLOCKS * (b * nb + i) - jnp.minimum(i, 1), 0))
    return pl.pallas_call(
        _attn_prompt_kernel,
        grid=(bsz, nb),
        in_specs=[pl.BlockSpec(memory_space=pltpu.SMEM), own(Q_DIM), prev(KV_DIM), own(KV_DIM),
                  prev(KV_DIM), own(KV_DIM), pl.BlockSpec(tbl.shape, lambda b, i: (0, 0, 0))],
        out_specs=own(Q_DIM),
        out_shape=jax.ShapeDtypeStruct((bsz * t, Q_DIM), _BF16),
        compiler_params=pltpu.CompilerParams(dimension_semantics=("arbitrary", "arbitrary"),
                                             vmem_limit_bytes=VMEM_LIMIT),
        name="attn_prompt",
    )(sinks, q, k, k, v, v, tbl)


SAMPLE_UNROLL = 2


def _block_diag_t(x):
    xb = x.astype(_BF16)
    z = jnp.zeros_like(xb)
    return jnp.concatenate([jnp.concatenate([xb, z], axis=1), jnp.concatenate([z, xb], axis=1)], axis=0)


def _attn_sample_kernel(sink_ref, q_ref, kn_ref, vn_ref, ck_ref, cv_ref, tbl_ref, o_ref, cko_ref, cvo_ref,
                        *, sb, steps):
    pad = jnp.zeros((WINDOW - steps, LANES), _F32)
    lane = lax.broadcasted_iota(jnp.int32, (HEAD_DIM, WINDOW), 1)

    def one_sequence(s, carry):
        def store(pair, o):
            o_ref[s, :, pair * LANES:(pair + 1) * LANES] = o

        prev_ops, own_ops = [], []
        for slab in range(KV_DIM // LANES):
            cols = slice(slab * LANES, (slab + 1) * LANES)
            new_k = jnp.concatenate([kn_ref[s][:, cols], pad], axis=0).T
            new_v = jnp.concatenate([vn_ref[s][:, cols], pad], axis=0).T
            for sub in range(2):
                kvh = 2 * slab + sub
                part = slice(sub * HEAD_DIM, (sub + 1) * HEAD_DIM)
                kt, vt = ck_ref[s, kvh], cv_ref[s, kvh]
                cko_ref[s, kvh] = pltpu.roll(jnp.where(lane < steps, new_k[part], kt), WINDOW - steps, axis=1)
                cvo_ref[s, kvh] = pltpu.roll(jnp.where(lane < steps, new_v[part], vt), WINDOW - steps, axis=1)
                prev_ops.append((_block_diag_t(kt), _block_diag_t(vt)))
                own_ops.append((_block_diag_t(new_k[part]), _block_diag_t(new_v[part])))
        _attend(q_ref[s], prev_ops, own_ops, tbl_ref, sink_ref, jnp.float32(0.0), store, transposed=True)
        return carry

    lax.fori_loop(0, sb, one_sequence, 0, unroll=SAMPLE_UNROLL)


def _attn_sample(q, k_new, v_new, cache_kt, cache_vt, tbl, sinks, sb):
    nseq, steps, _ = q.shape
    seq = lambda r, w: pl.BlockSpec((sb, r, w), lambda i: (i, 0, 0))
    cache = pl.BlockSpec((sb, N_KV_HEADS, HEAD_DIM, WINDOW), lambda i: (i, 0, 0, 0))
    return pl.pallas_call(
        functools.partial(_attn_sample_kernel, sb=sb, steps=steps),
        grid=(nseq // sb,),
        in_specs=[pl.BlockSpec(memory_space=pltpu.SMEM), seq(steps, Q_DIM), seq(steps, KV_DIM), seq(steps, KV_DIM),
                  cache, cache, pl.BlockSpec(tbl.shape, lambda i: (0, 0, 0))],
        out_specs=[seq(steps, Q_DIM), cache, cache],
        out_shape=[jax.ShapeDtypeStruct((nseq, steps, Q_DIM), _F32),
                   jax.ShapeDtypeStruct(cache_kt.shape, _F32), jax.ShapeDtypeStruct(cache_vt.shape, _F32)],
        compiler_params=pltpu.CompilerParams(dimension_semantics=("arbitrary",),
                                             vmem_limit_bytes=VMEM_LIMIT),
        name="attn_sample",
    )(sinks, q, k_new, v_new, cache_kt, cache_vt, tbl)


def _lane_min_index(mask, lane):
    return jnp.min(jnp.where(mask, lane, float(LANES)), axis=-1, keepdims=True)


def _finish_kernel(x_ref, conv_ref, o_ref, ga_ref, gb_ref, wa_ref, wo_ref, ng_ref, wr_ref, br_ref,
                   tri_ref, h_ref, hn_ref, route_ref, count_ref, running_ref):
    @pl.when(pl.program_id(0) == 0)
    def _():
        running_ref[...] = jnp.zeros_like(running_ref)

    attn_out = _dot(o_ref[...].astype(_BF16), wa_ref[...])
    merged = ga_ref[...].astype(_F32) * conv_ref[...].astype(_F32) + gb_ref[...].astype(_F32) * attn_out
    h = x_ref[...] + _dot(merged.astype(_BF16), wo_ref[...])
    h_ref[...] = h
    hn = h * lax.rsqrt(jnp.mean(h * h, axis=-1, keepdims=True) + EPS) * ng_ref[...]
    hn_ref[...] = _pack_bf16_pairs(hn)

    logits = _dot(hn.astype(_BF16), wr_ref[...]) + br_ref[...]
    lane = lax.broadcasted_iota(jnp.int32, logits.shape, 1).astype(_F32)
    gmask = lane < N_GROUPS
    gl = jnp.where(gmask, logits, MASK_VALUE)
    gmax = jnp.max(gl, axis=-1, keepdims=True)
    grp = _lane_min_index(gmask & (gl == gmax), lane)
    p_grp = 1.0 / jnp.sum(jnp.where(gmask, jnp.exp(gl - gmax), 0.0), axis=-1, keepdims=True)
    e_lo = N_GROUPS + grp * EXPERTS_PER_GROUP
    emask = (lane >= e_lo) & (lane < e_lo + EXPERTS_PER_GROUP)
    el = jnp.where(emask, logits, MASK_VALUE)
    ex = jnp.where(emask, jnp.exp(el - jnp.max(el, axis=-1, keepdims=True)), 0.0)
    prob = jnp.where(emask, ex / jnp.sum(ex, axis=-1, keepdims=True), -1.0)
    p1 = jnp.max(prob, axis=-1, keepdims=True)
    i1 = _lane_min_index(prob == p1, lane)
    rest = jnp.where(lane == i1, -1.0, prob)
    p2 = jnp.max(rest, axis=-1, keepdims=True)
    i2 = _lane_min_index(rest == p2, lane)
    w1 = p_grp * p1 / (p1 + p2)
    w2 = p_grp * p2 / (p1 + p2)
    e1 = i1 - N_GROUPS
    e2 = i2 - N_GROUPS

    hot1 = lane == e1
    hot2 = lane == e2
    hot = jnp.where(hot1 | hot2, 1.0, 0.0)
    before = _dot(tri_ref[...], hot.astype(_BF16)) + running_ref[...]
    rank1 = jnp.sum(jnp.where(hot1, before, 0.0), axis=-1, keepdims=True)
    rank2 = jnp.sum(jnp.where(hot2, before, 0.0), axis=-1, keepdims=True)
    running_ref[...] += jnp.sum(hot, axis=0, keepdims=True)
    count_ref[...] = jnp.broadcast_to(running_ref[...], count_ref.shape)

    fields = (e1, e2, w1, w2, rank1, rank2)
    route = jnp.zeros(logits.shape, _F32)
    for pos, val in enumerate(fields):
        route = jnp.where(lane == pos, val, route)
    route_ref[...] = route


ROUTE_E, ROUTE_W, ROUTE_RANK = 0, 2, 4
DEST_ROWS = 1024


def _finish(x, conv_out, o, ga, gb, wa_b, wo_b, ng, wr_b, br, tm):
    n = x.shape[0]
    tri = jnp.asarray(np.tril(np.ones((tm, tm), np.float32), -1), _BF16)
    row = lambda w: pl.BlockSpec((tm, w), lambda i: (i, 0))
    full = _resident
    return pl.pallas_call(
        _finish_kernel,
        grid=(n // tm,),
        in_specs=[row(D_MODEL), row(D_MODEL), row(Q_DIM), row(D_MODEL), row(D_MODEL),
                  full(wa_b), full(wo_b), full(ng), full(wr_b), full(br), full(tri)],
        out_specs=[row(D_MODEL), row(HALF), row(LANES), pl.BlockSpec((SUBLANES, LANES), lambda i: (0, 0))],
        out_shape=[jax.ShapeDtypeStruct((n, D_MODEL), _F32),
                   jax.ShapeDtypeStruct((n, HALF), jnp.uint32),
                   jax.ShapeDtypeStruct((n, LANES), _F32),
                   jax.ShapeDtypeStruct((SUBLANES, LANES), _F32)],
        scratch_shapes=[pltpu.VMEM((1, LANES), _F32)],
        compiler_params=pltpu.CompilerParams(dimension_semantics=("arbitrary",),
                                             vmem_limit_bytes=VMEM_LIMIT),
        name="finish",
    )(x, conv_out, o, ga, gb, wa_b, wo_b, ng, wr_b, br, tri)


def _dest_kernel(route_ref, starts_ref, dest_ref):
    route = route_ref[...]
    lane = lax.broadcasted_iota(jnp.int32, route.shape, 1)
    out = jnp.zeros(route.shape, jnp.int32)
    for j in range(TOP_K):
        e = route[:, ROUTE_E + j:ROUTE_E + j + 1].astype(jnp.int32)
        start = jnp.sum(jnp.where(lane == e, starts_ref[...], 0.0), axis=-1, keepdims=True)
        d = (start + route[:, ROUTE_RANK + j:ROUTE_RANK + j + 1]).astype(jnp.int32)
        out = jnp.where(lane == j, d, out)
    dest_ref[...] = out


def _dest(route, starts_row, tm):
    n = route.shape[0]
    row = pl.BlockSpec((tm, LANES), lambda i: (i, 0))
    return pl.pallas_call(
        _dest_kernel,
        grid=(n // tm,),
        in_specs=[row, pl.BlockSpec((1, LANES), lambda i: (0, 0))],
        out_specs=row,
        out_shape=jax.ShapeDtypeStruct((n, LANES), jnp.int32),
        name="dest",
    )(route, starts_row)


SC_CORES = 2
SC_SUBCORES = 16
SC_WORKERS = SC_CORES * SC_SUBCORES
SC_IN_FLIGHT = 4
SC_CHUNK_BYTES = 64 * 1024


def _sc_move_rows(src, idx, gather):
    n, d = src.shape
    b = idx.shape[0]
    per_worker = b // SC_WORKERS
    assert per_worker * SC_WORKERS == b and (gather or n % per_worker == 0), (b, n)
    chunk = min(per_worker // SC_IN_FLIGHT, SC_CHUNK_BYTES // (d * 4))
    n_iters = per_worker // (chunk * SC_IN_FLIGHT)
    assert n_iters * chunk * SC_IN_FLIGHT == per_worker and chunk % SUBLANES == 0, (per_worker, chunk)
    mesh = plsc.VectorSubcoreMesh(core_axis_name="c", subcore_axis_name="s",
                                  num_cores=SC_CORES, num_subcores=SC_SUBCORES)
    scratch = ([pltpu.VMEM((chunk,), jnp.int32)] * SC_IN_FLIGHT + [pltpu.VMEM((chunk, d), src.dtype)] * SC_IN_FLIGHT
               + [pltpu.SemaphoreType.DMA] * SC_IN_FLIGHT)

    @functools.partial(pl.kernel, mesh=mesh, out_type=jax.ShapeDtypeStruct((b, d), src.dtype),
                       scratch_types=scratch, name="sc_gather_rows" if gather else "sc_scatter_rows",
                       cost_estimate=pl.CostEstimate(flops=0, transcendentals=0, bytes_accessed=2 * b * d * 4 + b * 4))
    def move(src_hbm, idx_hbm, out_hbm, *bufs):
        idx_v = bufs[:SC_IN_FLIGHT]
        rows_v = bufs[SC_IN_FLIGHT:2 * SC_IN_FLIGHT]
        sems = bufs[2 * SC_IN_FLIGHT:]
        worker = lax.axis_index("s") * SC_CORES + lax.axis_index("c")

        @pl.loop(0, n_iters)
        def _(it):
            bases = [pl.multiple_of(worker * per_worker + (it * SC_IN_FLIGHT + j) * chunk, chunk)
                     for j in range(SC_IN_FLIGHT)]
            loads = [pltpu.async_copy(idx_hbm.at[pl.ds(bases[j], chunk)], idx_v[j], sems[j])
                     for j in range(SC_IN_FLIGHT)]
            reads = []
            for j in range(SC_IN_FLIGHT):
                loads[j].wait()
                if gather:
                    rows = src_hbm.at[idx_v[j]]
                else:
                    rows = src_hbm.at[pl.ds(pl.multiple_of(lax.rem(bases[j], n), chunk), chunk)]
                reads.append(pltpu.async_copy(rows, rows_v[j], sems[j]))
            writes = []
            for j in range(SC_IN_FLIGHT):
                reads[j].wait()
                dst = out_hbm.at[pl.ds(bases[j], chunk)] if gather else out_hbm.at[idx_v[j]]
                writes.append(pltpu.async_copy(rows_v[j], dst, sems[j]))
            for w in writes:
                w.wait()

    return move(src, idx)


def _sc_gather_rows(table, idx):
    return _sc_move_rows(table, idx, gather=True)


def _sc_scatter_rows(src, idx):
    return _sc_move_rows(src, idx, gather=False)


def _expert_kernel(blk_ref, exp_ref, lo_ref, hi_ref, x_ref, wg_ref, wu_ref, wd_ref, yb_ref,
                   wg_b, wu_b, wd_b, held_ref):
    del blk_ref
    k = pl.program_id(0)
    lo, hi, e = lo_ref[k], hi_ref[k], exp_ref[k]

    @pl.when(k == 0)
    def _():
        held_ref[0] = -1

    @pl.when(hi > lo)
    def _():
        @pl.when(held_ref[0] != e)
        def _():
            wg_b[...] = wg_ref[0].astype(_BF16)
            wu_b[...] = wu_ref[0].astype(_BF16)
            wd_b[...] = wd_ref[0].astype(_BF16)
            held_ref[0] = e

        xb = _unpack_bf16_pairs(x_ref[...]).astype(_BF16)
        g = _dot(xb, wg_b[...])
        u = _dot(xb, wu_b[...])
        hid = g * jax.nn.sigmoid(g) * u
        y = _dot(hid.astype(_BF16), wd_b[...])
        r = lax.broadcasted_iota(jnp.int32, yb_ref.shape, 0)
        pltpu.store(yb_ref, _pack_bf16_pairs(y), mask=(r >= lo) & (r < hi))


def _experts(items, xs, w_gate, w_up, w_down):
    n_items = items[0].shape[0]
    block_rows = _block_rows(xs.shape[0])
    wspec = lambda a: pl.BlockSpec((1,) + a.shape[1:], lambda k, blk, exp, lo, hi: (exp[k], 0, 0))
    rows = pl.BlockSpec((block_rows, HALF), lambda k, blk, exp, lo, hi: (blk[k], 0))
    grid_spec = pltpu.PrefetchScalarGridSpec(
        num_scalar_prefetch=4,
        grid=(n_items,),
        in_specs=[rows, wspec(w_gate), wspec(w_up), wspec(w_down)],
        out_specs=rows,
        scratch_shapes=[pltpu.VMEM(w_gate.shape[1:], _BF16), pltpu.VMEM(w_up.shape[1:], _BF16),
                        pltpu.VMEM(w_down.shape[1:], _BF16), pltpu.SMEM((1,), jnp.int32)],
    )
    return pl.pallas_call(
        _expert_kernel,
        grid_spec=grid_spec,
        out_shape=jax.ShapeDtypeStruct(xs.shape, xs.dtype),
        compiler_params=pltpu.CompilerParams(dimension_semantics=("arbitrary",),
                                             vmem_limit_bytes=VMEM_LIMIT),
        cost_estimate=pl.CostEstimate(
            flops=n_items * block_rows * 6 * D_MODEL * D_EXPERT, transcendentals=n_items * block_rows * D_EXPERT,
            bytes_accessed=2 * xs.size * 4 + (w_gate.size + w_up.size + w_down.size) * 4),
        name="experts",
    )(*items, xs, w_gate, w_up, w_down)


def _combine_kernel(h_ref, route_ref, g0_ref, g1_ref, y_ref):
    route = route_ref[...]
    y_ref[...] = (h_ref[...] + route[:, ROUTE_W:ROUTE_W + 1] * _unpack_bf16_pairs(g0_ref[...])
                  + route[:, ROUTE_W + 1:ROUTE_W + 2] * _unpack_bf16_pairs(g1_ref[...]))


def _combine(h, route, g, tm):
    n = h.shape[0]
    nt = n // tm
    row = lambda w: pl.BlockSpec((tm, w), lambda i: (i, 0))
    return pl.pallas_call(
        _combine_kernel,
        grid=(nt,),
        in_specs=[row(D_MODEL), row(LANES), row(HALF), pl.BlockSpec((tm, HALF), lambda i: (nt + i, 0))],
        out_specs=row(D_MODEL),
        out_shape=jax.ShapeDtypeStruct((n, D_MODEL), _F32),
        compiler_params=pltpu.CompilerParams(dimension_semantics=("arbitrary",),
                                             vmem_limit_bytes=VMEM_LIMIT),
        name="combine",
    )(h, route, g, g)


def _block_rows(n_rows):
    return MOE_ROWS if n_rows >= MOE_MIN_BLOCKS * MOE_ROWS else MOE_ROWS // 4


def _work_items(counts, n_pairs):
    block_rows = _block_rows(n_pairs)
    n_blocks = n_pairs // block_rows
    starts = jnp.cumsum(counts) - counts
    cuts = jnp.sort(jnp.concatenate([jnp.arange(n_blocks, dtype=jnp.int32) * block_rows, starts]))
    ends = jnp.concatenate([cuts[1:], jnp.full((1,), n_pairs, jnp.int32)])
    blk = jnp.minimum(cuts // block_rows, n_blocks - 1)
    expert = jnp.clip(jnp.sum(starts[None, :] <= cuts[:, None], axis=1) - 1, 0, N_EXPERTS - 1).astype(jnp.int32)
    return starts, (blk, expert, cuts - blk * block_rows, ends - blk * block_rows)


def _dispatch(hn, route, counts_rows):
    n = hn.shape[0]
    n_pairs = n * TOP_K
    counts = counts_rows[0, :N_EXPERTS].astype(jnp.int32)
    starts, items = _work_items(counts, n_pairs)
    starts_row = jnp.zeros((1, LANES), _F32).at[0, :N_EXPERTS].set(starts.astype(_F32))
    dest = _dest(route, starts_row, min(n, DEST_ROWS))[:, :TOP_K].T.reshape(n_pairs)
    return _sc_scatter_rows(hn, dest), dest, items


def _after(value, *earlier):
    return lax.optimization_barrier((value,) + earlier)[0]


def kernel(x_prompt, x_sample, state_conv, cache_k, cache_v, norm_attn_g, w_in, q_norm_g, k_norm_g, rel_bias, attn_sinks, w_dw, b_dw, conv_ln_g, conv_ln_b, w_conv_out, b_conv_out, w_attn_out, w_out, norm_ffn_g, w_grp, b_grp, w_router, b_router, w_gate, w_up, w_down):
    bsz, t, _ = x_prompt.shape
    nseq, steps, _ = x_sample.shape
    row = lambda a: a.reshape(1, -1).astype(_F32)

    w_in_b = w_in.astype(_BF16)
    wco_b = w_conv_out.astype(_BF16)
    wa_b = w_attn_out.astype(_BF16)
    wo_b = w_out.astype(_BF16)
    qg = row(jnp.tile(q_norm_g, N_HEADS)) * (HEAD_DIM ** -0.5)
    kg = row(jnp.tile(k_norm_g, N_KV_HEADS))
    w_rt = jnp.zeros((D_MODEL, LANES), _F32).at[:, :N_GROUPS].set(w_grp).at[:, N_GROUPS:N_GROUPS + N_EXPERTS].set(w_router)
    wr_b = w_rt.astype(_BF16)
    b_rt = jnp.zeros((1, LANES), _F32).at[0, :N_GROUPS].set(b_grp).at[0, N_GROUPS:N_GROUPS + N_EXPERTS].set(b_router)
    tbl = _bias_tables(rel_bias)
    conv_params = (w_dw, row(b_dw), row(conv_ln_g), row(conv_ln_b), wco_b, row(b_conv_out))

    def finish(x2d, conv_out, o, ga, gb):
        return _finish(x2d, conv_out, o, ga, gb, wa_b, wo_b, row(norm_ffn_g), wr_b, b_rt, ROW_TILE)

    xp = x_prompt.reshape(bsz * t, D_MODEL)
    q, k, v, ga, gb, conv_out, glu_tail = _inproj_conv(x_prompt, row(norm_attn_g), w_in_b, qg, kg, *conv_params,
                                                       ROW_TILE // 2)
    o = _attn_prompt(q, k, v, tbl, attn_sinks, bsz, t)
    h_p, hn_p, route_p, counts_p = finish(xp, conv_out, o, ga, gb)
    rows_p, dest_p, items_p = _dispatch(hn_p, route_p, counts_p)
    state_conv_prompt = glu_tail[:, HALO - (CONV_WIDTH - 1):]
    tail = lambda a: a.reshape(bsz, t, KV_DIM)[:, t - WINDOW:].reshape(bsz, WINDOW, N_KV_HEADS, HEAD_DIM)
    cache_k_prompt, cache_v_prompt = tail(k), tail(v)

    xs = _after(x_sample, dest_p).reshape(nseq * steps, D_MODEL)
    glu, q, k, v, ga, gb = _inproj(xs, row(norm_attn_g), w_in_b, qg, kg, _F32, ROW_TILE)
    glu_t = glu.reshape(nseq, steps, D_CONV).transpose(1, 0, 2)
    conv_out, state_t = _conv_sample(state_conv.transpose(1, 0, 2), glu_t, *conv_params, 64)
    conv_out = conv_out.transpose(1, 0, 2).reshape(nseq * steps, D_MODEL)
    k3 = k.reshape(nseq, steps, KV_DIM)
    v3 = v.reshape(nseq, steps, KV_DIM)
    o, ck_t, cv_t = _attn_sample(q.reshape(nseq, steps, Q_DIM), k3, v3, cache_k.transpose(0, 2, 3, 1),
                                 cache_v.transpose(0, 2, 3, 1), tbl, attn_sinks, 8)
    h_s, hn_s, route_s, counts_s = finish(xs, conv_out, o.reshape(nseq * steps, Q_DIM), ga, gb)
    rows_s, dest_s, items_s = _dispatch(hn_s, route_s, counts_s)
    state_conv_sample = state_t.transpose(1, 0, 2)
    cache_k_sample = ck_t.transpose(0, 3, 1, 2)
    cache_v_sample = cv_t.transpose(0, 3, 1, 2)

    yb_p = _experts(items_p, rows_p, w_gate, w_up, w_down)
    g_p = _sc_gather_rows(yb_p, dest_p)
    yb_s = _experts(items_s, _after(rows_s, yb_p), w_gate, w_up, w_down)
    g_s = _sc_gather_rows(yb_s, dest_s)
    y_prompt = _combine(h_p, route_p, g_p, ROW_TILE).reshape(bsz, t, D_MODEL)
    y_sample = _combine(h_s, route_s, _after(g_s, y_prompt), ROW_TILE).reshape(nseq, steps, D_MODEL)

    return (y_prompt, y_sample, state_conv_prompt, cache_k_prompt, cache_v_prompt,
            state_conv_sample, cache_k_sample, cache_v_sample)
```

```python
import functools
import math

import numpy as np
import jax
import jax.numpy as jnp
from jax import lax
from jax.experimental import pallas as pl
from jax.experimental.pallas import tpu as pltpu
from jax.experimental.pallas import tpu_sc as plsc

D_MODEL = 1024
N_HEADS = 16
HEAD_DIM = 64
N_KV_HEADS = 4
WINDOW = 128
Q_DIM = N_HEADS * HEAD_DIM
KV_DIM = N_KV_HEADS * HEAD_DIM
N_BUCKETS = 32
MAX_EXACT = N_BUCKETS // 2
MAX_DISTANCE = 128
D_CONV = D_MODEL
CONV_WIDTH = 31
N_GROUPS = 4
EXPERTS_PER_GROUP = 8
N_EXPERTS = N_GROUPS * EXPERTS_PER_GROUP
TOP_K = 2
D_EXPERT = 256
EPS = 1e-6

LANES = 128
SUBLANES = 8
N_PAIRS = N_HEADS // 2
MOE_ROWS = 512
MASK_VALUE = -1e30
VMEM_LIMIT = 56 * 1024 * 1024
ROW_TILE = 512

_F32 = jnp.float32
_BF16 = jnp.bfloat16


def _resident(a):
    return pl.BlockSpec(a.shape, lambda *_: (0,) * a.ndim, pipeline_mode=pl.Buffered(1))


def _dot(a, b):
    return jnp.dot(a, b, preferred_element_type=_F32)


HALF = D_MODEL // 2


def _pack_pair(lo, hi):
    lo_bits = pltpu.bitcast(lo.astype(_BF16).astype(_F32), jnp.uint32)
    hi_bits = pltpu.bitcast(hi.astype(_BF16).astype(_F32), jnp.uint32)
    return hi_bits | (lo_bits >> 16)


def _pack_bf16_pairs(x):
    return _pack_pair(x[:, :HALF], x[:, HALF:])


def _unpack_bf16_pairs(w):
    lo = pltpu.bitcast(w << 16, _F32)
    hi = pltpu.bitcast(w & jnp.uint32(0xFFFF0000), _F32)
    return jnp.concatenate([lo, hi], axis=1)


def _head_rms_scale(z):
    low = lax.broadcasted_iota(jnp.int32, (z.shape[0], LANES), 1) < HEAD_DIM
    slabs = []
    for c in range(z.shape[1] // LANES):
        sq = z[:, c * LANES:(c + 1) * LANES]
        sq = sq * sq
        first = jnp.sum(jnp.where(low, sq, 0.0), axis=-1, keepdims=True)
        second = jnp.sum(jnp.where(low, 0.0, sq), axis=-1, keepdims=True)
        slabs.append(lax.rsqrt(jnp.where(low, first, second) * (1.0 / HEAD_DIM) + EPS))
    return jnp.concatenate(slabs, axis=1)


def _inproj_kernel(x_ref, g_ref, w_ref, qg_ref, kg_ref,
                   glu_ref, q_ref, k_ref, v_ref, ga_ref, gb_ref):
    x = x_ref[...]
    xn = x * lax.rsqrt(jnp.mean(x * x, axis=-1, keepdims=True) + EPS) * g_ref[...]
    xb = xn.astype(_BF16)

    def seg(lo, width):
        return _dot(xb, w_ref[:, lo:lo + width])

    a = seg(0, D_CONV)
    b = seg(D_CONV, D_CONV)
    glu_ref[...] = a * jax.nn.sigmoid(b)
    off = 2 * D_CONV
    q = seg(off, Q_DIM)
    q_ref[...] = (q * _head_rms_scale(q) * qg_ref[...]).astype(q_ref.dtype)
    off += Q_DIM
    k = seg(off, KV_DIM)
    k_ref[...] = k * _head_rms_scale(k) * kg_ref[...]
    off += KV_DIM
    v_ref[...] = seg(off, KV_DIM)
    off += KV_DIM
    ga_ref[...] = jax.nn.sigmoid(seg(off, D_MODEL)).astype(ga_ref.dtype)
    off += D_MODEL
    gb_ref[...] = jax.nn.sigmoid(seg(off, D_MODEL)).astype(gb_ref.dtype)


def _inproj(x, g, w_in_b, qg, kg, q_dtype, tm):
    n = x.shape[0]
    in_dim = w_in_b.shape[1]
    row = lambda w: pl.BlockSpec((tm, w), lambda i: (i, 0))
    full = _resident
    return pl.pallas_call(
        _inproj_kernel,
        grid=(n // tm,),
        in_specs=[row(D_MODEL), full(g), full(w_in_b), full(qg), full(kg)],
        out_specs=[row(D_CONV), row(Q_DIM), row(KV_DIM), row(KV_DIM), row(D_MODEL), row(D_MODEL)],
        out_shape=[jax.ShapeDtypeStruct((n, D_CONV), _F32),
                   jax.ShapeDtypeStruct((n, Q_DIM), q_dtype),
                   jax.ShapeDtypeStruct((n, KV_DIM), _F32),
                   jax.ShapeDtypeStruct((n, KV_DIM), _F32),
                   jax.ShapeDtypeStruct((n, D_MODEL), _BF16),
                   jax.ShapeDtypeStruct((n, D_MODEL), _BF16)],
        compiler_params=pltpu.CompilerParams(dimension_semantics=("arbitrary",),
                                             vmem_limit_bytes=VMEM_LIMIT),
        name="inproj",
    )(x, g, w_in_b, qg, kg)


def _ln_swish_project(y, lng_ref, lnb_ref, wo_ref, bo_ref):
    mu = jnp.mean(y, axis=-1, keepdims=True)
    yc = y - mu
    var = jnp.mean(yc * yc, axis=-1, keepdims=True)
    z = yc * lax.rsqrt(var + EPS) * lng_ref[...] + lnb_ref[...]
    z = z * jax.nn.sigmoid(z)
    return (_dot(z.astype(_BF16), wo_ref[...]) + bo_ref[...]).astype(_BF16)


HALO = 32
CONV_STEPS = 16
CH_TILES = D_CONV // LANES


PIECE = 256
PAIR_TILES = 2 * CH_TILES


def _inproj_conv_kernel(x_ref, g_ref, w_ref, qg_ref, kg_ref, w16_ref, b16_ref, lng_ref, lnb_ref, wo_ref, bo_ref,
                        q_ref, k_ref, v_ref, ga_ref, gb_ref, conv_ref, tail_ref, hist_ref, y_ref, *, tm):
    i = pl.program_id(1)
    rows = 2 * tm

    @pl.when(i == 0)
    def _():
        hist_ref[0:HALO * CH_TILES, :] = jnp.zeros((HALO * CH_TILES, LANES), jnp.uint32)

    @pl.when(i > 0)
    def _():
        hist_ref[0:HALO * CH_TILES, :] = hist_ref[tm * CH_TILES:(tm + HALO) * CH_TILES, :]

    x = jnp.concatenate([x_ref[0], x_ref[1]], axis=0)
    xn = x * lax.rsqrt(jnp.mean(x * x, axis=-1, keepdims=True) + EPS) * g_ref[...]
    xb = xn.astype(_BF16)

    def seg(lo, width):
        return _dot(xb, w_ref[:, lo:lo + width])

    def put(ref, lo, val):
        ref[0, :, lo:lo + PIECE] = val[:tm].astype(ref.dtype)
        ref[1, :, lo:lo + PIECE] = val[tm:].astype(ref.dtype)

    for p in range(D_CONV // PIECE):
        lo = p * PIECE
        glu = seg(lo, PIECE) * jax.nn.sigmoid(seg(D_CONV + lo, PIECE))
        tail_ref[0, :, lo:lo + PIECE] = glu[tm - HALO:tm, :]
        tail_ref[1, :, lo:lo + PIECE] = glu[rows - HALO:, :]
        words = _pack_pair(glu[:tm], glu[tm:])
        for c in range(PIECE // LANES):
            tile = lo // LANES + c
            hist_ref[pl.ds(HALO * CH_TILES + tile, tm, stride=CH_TILES), :] = words[:, c * LANES:(c + 1) * LANES]

    first = HALO - (CONV_WIDTH - 1)

    def conv_chunk(ci, carry):
        t0 = ci * CONV_STEPS
        acc = jnp.zeros((CONV_STEPS, PAIR_TILES, LANES), _F32)
        for j in range(CONV_WIDTH):
            lo = pl.multiple_of((t0 + first + j) * CH_TILES, CH_TILES)
            xw = pltpu.bitcast(hist_ref[pl.ds(lo, CONV_STEPS * CH_TILES), :], _BF16)
            acc = acc + (xw.reshape(CONV_STEPS, PAIR_TILES, LANES).astype(_F32)
                         * w16_ref[j][None].astype(_F32))
        y_ref[pl.ds(pl.multiple_of(t0 * PAIR_TILES, PAIR_TILES), CONV_STEPS * PAIR_TILES), :] = (
            (acc + b16_ref[...][None]).reshape(CONV_STEPS * PAIR_TILES, LANES))
        return carry

    def q_piece(lo):
        q = seg(2 * D_CONV + lo, PIECE)
        put(q_ref, lo, q * _head_rms_scale(q) * qg_ref[:, lo:lo + PIECE])

    def k_piece(lo):
        k = seg(2 * D_CONV + Q_DIM + lo, PIECE)
        put(k_ref, lo, k * _head_rms_scale(k) * kg_ref[:, lo:lo + PIECE])

    def v_piece(lo):
        put(v_ref, lo, seg(2 * D_CONV + Q_DIM + KV_DIM + lo, PIECE))

    def gate_piece(ref, base, lo):
        put(ref, lo, jax.nn.sigmoid(seg(base + lo, PIECE)))

    gate_base = 2 * D_CONV + Q_DIM + 2 * KV_DIM
    pieces = ([functools.partial(q_piece, lo) for lo in range(0, Q_DIM, PIECE)]
              + [functools.partial(k_piece, lo) for lo in range(0, KV_DIM, PIECE)]
              + [functools.partial(v_piece, lo) for lo in range(0, KV_DIM, PIECE)]
              + [functools.partial(gate_piece, ga_ref, gate_base, lo) for lo in range(0, D_MODEL, PIECE)]
              + [functools.partial(gate_piece, gb_ref, gate_base + D_MODEL, lo) for lo in range(0, D_MODEL, PIECE)])

    for piece in pieces:
        piece()
    lax.fori_loop(0, tm // CONV_STEPS, conv_chunk, 0)

    y = jnp.concatenate(
        [jnp.concatenate([y_ref[pl.ds(2 * c + s, tm, stride=PAIR_TILES), :] for c in range(CH_TILES)], axis=1)
         for s in range(2)], axis=0)
    out = _ln_swish_project(y, lng_ref, lnb_ref, wo_ref, bo_ref)
    conv_ref[0] = out[:tm]
    conv_ref[1] = out[tm:]


def _inproj_conv(x, g, w_in_b, qg, kg, w_dw, b_dw, lng, lnb, wo_b, bo, tm):
    bsz, t, _ = x.shape
    assert CH_TILES == SUBLANES and bsz % 2 == 0 and t % tm == 0
    nt = t // tm
    w16 = jnp.repeat(w_dw.reshape(CONV_WIDTH, CH_TILES, LANES), 2, axis=1).astype(_BF16)
    b16 = jnp.repeat(b_dw.reshape(CH_TILES, LANES), 2, axis=0)
    full = _resident
    blk = lambda w: pl.BlockSpec((2, tm, w), lambda p, i: (p, i, 0))
    shape = lambda w, dt: jax.ShapeDtypeStruct((bsz, t, w), dt)
    outs = pl.pallas_call(
        functools.partial(_inproj_conv_kernel, tm=tm),
        grid=(bsz // 2, nt),
        in_specs=[blk(D_MODEL), full(g), full(w_in_b), full(qg), full(kg), full(w16), full(b16), full(lng), full(lnb),
                  full(wo_b), full(bo)],
        out_specs=[blk(Q_DIM), blk(KV_DIM), blk(KV_DIM), blk(D_MODEL), blk(D_MODEL), blk(D_MODEL),
                   pl.BlockSpec((2, HALO, D_CONV), lambda p, i: (p, 0, 0))],
        out_shape=[shape(Q_DIM, _BF16), shape(KV_DIM, _F32), shape(KV_DIM, _F32), shape(D_MODEL, _BF16),
                   shape(D_MODEL, _BF16), shape(D_MODEL, _BF16), jax.ShapeDtypeStruct((bsz, HALO, D_CONV), _F32)],
        scratch_shapes=[pltpu.VMEM(((tm + HALO) * CH_TILES, LANES), jnp.uint32),
                        pltpu.VMEM((tm * PAIR_TILES, LANES), _F32)],
        compiler_params=pltpu.CompilerParams(dimension_semantics=("arbitrary", "arbitrary"),
                                             vmem_limit_bytes=VMEM_LIMIT),
        name="inproj_conv",
    )(x, g, w_in_b, qg, kg, w16, b16, lng, lnb, wo_b, bo)
    return [o.reshape(bsz * t, o.shape[-1]) for o in outs[:-1]] + [outs[-1]]


def _conv_sample_kernel(state_ref, glu_ref, wdw_ref, bdw_ref, lng_ref, lnb_ref, wo_ref, bo_ref,
                        out_ref, state_out_ref):
    keep, steps = state_ref.shape[0], glu_ref.shape[0]

    def hist(u):
        return state_ref[u] if u < keep else glu_ref[u - keep]

    for t in range(steps):
        acc = hist(t) * wdw_ref[0:1, :]
        for j in range(1, CONV_WIDTH):
            acc = acc + hist(t + j) * wdw_ref[j:j + 1, :]
        out_ref[t] = _ln_swish_project(acc + bdw_ref[...], lng_ref, lnb_ref, wo_ref, bo_ref)
    state_out_ref[0:keep - steps] = state_ref[steps:keep]
    state_out_ref[keep - steps:keep] = glu_ref[...]


def _conv_sample(state_t, glu_t, w_dw, b_dw, lng, lnb, wo_b, bo, sb):
    keep, nseq, _ = state_t.shape
    steps = glu_t.shape[0]
    full = _resident
    blk = lambda r: pl.BlockSpec((r, sb, D_CONV), lambda i: (0, i, 0))
    return pl.pallas_call(
        _conv_sample_kernel,
        grid=(nseq // sb,),
        in_specs=[blk(keep), blk(steps), full(w_dw), full(b_dw), full(lng), full(lnb), full(wo_b), full(bo)],
        out_specs=[blk(steps), blk(keep)],
        out_shape=[jax.ShapeDtypeStruct((steps, nseq, D_MODEL), _BF16),
                   jax.ShapeDtypeStruct(state_t.shape, _F32)],
        compiler_params=pltpu.CompilerParams(dimension_semantics=("arbitrary",),
                                             vmem_limit_bytes=VMEM_LIMIT),
        name="conv_sample",
    )(state_t, glu_t, w_dw, b_dw, lng, lnb, wo_b, bo)


def _bucket_map():
    i = np.arange(WINDOW)[:, None]
    j = np.arange(WINDOW)[None, :]
    n = (i - j) % WINDOW
    nf = np.maximum(n, 1).astype(np.float32)
    large = MAX_EXACT + (np.log(nf / np.float32(MAX_EXACT)) / np.float32(math.log(MAX_DISTANCE / MAX_EXACT))
                         * np.float32(N_BUCKETS - MAX_EXACT)).astype(np.int32)
    return np.where(n < MAX_EXACT, n, np.minimum(large, N_BUCKETS - 1)).astype(np.int32)


def _bias_table_kernel(rb_ref, bm_ref, tbl_ref):
    p = pl.program_id(0)
    bm = bm_ref[...]
    for half in range(2):
        h = 2 * p + half
        t = jnp.zeros(bm.shape, _F32)
        for b in range(N_BUCKETS):
            t = jnp.where(bm == b, rb_ref[b, h], t)
        tbl_ref[0, :, half * WINDOW:(half + 1) * WINDOW] = t


def _bias_tables(rel_bias):
    bm = jnp.asarray(_bucket_map())
    return pl.pallas_call(
        _bias_table_kernel,
        grid=(N_PAIRS,),
        in_specs=[pl.BlockSpec(memory_space=pltpu.SMEM), pl.BlockSpec(bm.shape, lambda p: (0, 0))],
        out_specs=pl.BlockSpec((1, WINDOW, 2 * WINDOW), lambda p: (p, 0, 0)),
        out_shape=jax.ShapeDtypeStruct((N_PAIRS, WINDOW, 2 * WINDOW), _F32),
        name="bias_tables",
    )(rel_bias, bm)


def _block_diag_pairs(slab):
    low = lax.broadcasted_iota(jnp.int32, slab.shape, 1) < HEAD_DIM
    swapped = pltpu.roll(slab, HEAD_DIM, axis=1)
    zero = jnp.zeros_like(slab)
    first = jnp.concatenate([jnp.where(low, slab, zero), jnp.where(low, zero, swapped)], axis=0)
    second = jnp.concatenate([jnp.where(low, swapped, zero), jnp.where(low, zero, slab)], axis=0)
    return first.astype(_BF16), second.astype(_BF16)


def _kv_operands(k_blk, v_blk):
    ops = []
    for slab in range(KV_DIM // LANES):
        cols = slice(slab * LANES, (slab + 1) * LANES)
        ops.extend(zip(_block_diag_pairs(k_blk[:, cols]), _block_diag_pairs(v_blk[:, cols])))
    return ops


def _attend(q, prev_ops, own_ops, tbl_ref, sink_ref, prev_shift, store, transposed=False):
    tq = q.shape[0]
    rows = 2 * tq
    row = lax.broadcasted_iota(jnp.int32, (rows, 2 * WINDOW), 0)
    col = lax.broadcasted_iota(jnp.int32, (rows, 2 * WINDOW), 1)
    from_prev = (col & (WINDOW - 1)) > jnp.where(row >= tq, row - tq, row)
    top = lax.broadcasted_iota(jnp.int32, (rows, 1), 0) < tq
    low = lax.broadcasted_iota(jnp.int32, (rows, LANES), 1) < HEAD_DIM
    contract_last = (((1,), (1,)), ((), ()))

    def logits(a, k_op):
        return _dot(a, k_op) if transposed else lax.dot_general(a, k_op, contract_last, preferred_element_type=_F32)

    def weighted_values(pr, v_op):
        return lax.dot_general(pr, v_op, contract_last, preferred_element_type=_F32) if transposed else _dot(pr, v_op)

    for kvh in range(N_KV_HEADS):
        (k_prev, v_prev), (k_own, v_own) = prev_ops[kvh], own_ops[kvh]
        pair_a = 2 * kvh
        pair_b = pair_a + 1
        qq = jnp.concatenate([q[:, pair_a * LANES:(pair_a + 1) * LANES],
                              q[:, pair_b * LANES:(pair_b + 1) * LANES]], axis=0).astype(_BF16)
        sp = logits(qq, k_prev)
        so = logits(qq, k_own)
        bias = jnp.concatenate([tbl_ref[pair_a, 0:tq, :], tbl_ref[pair_b, 0:tq, :]], axis=0)
        s = jnp.where(from_prev, sp + prev_shift, so) + bias
        sink_even = jnp.where(top, sink_ref[2 * pair_a], sink_ref[2 * pair_b])
        sink_odd = jnp.where(top, sink_ref[2 * pair_a + 1], sink_ref[2 * pair_b + 1])
        m_even = jnp.maximum(jnp.max(s[:, :WINDOW], axis=-1, keepdims=True), sink_even)
        m_odd = jnp.maximum(jnp.max(s[:, WINDOW:], axis=-1, keepdims=True), sink_odd)
        e_even = jnp.exp(s[:, :WINDOW] - m_even)
        e_odd = jnp.exp(s[:, WINDOW:] - m_odd)
        p = jnp.concatenate([e_even, e_odd], axis=1).astype(_BF16)
        zero = jnp.zeros_like(p)
        o = (weighted_values(jnp.where(from_prev, p, zero), v_prev)
             + weighted_values(jnp.where(from_prev, zero, p), v_own))
        den_even = jnp.sum(e_even, axis=-1, keepdims=True) + jnp.exp(sink_even - m_even)
        den_odd = jnp.sum(e_odd, axis=-1, keepdims=True) + jnp.exp(sink_odd - m_odd)
        o = o * jnp.where(low, 1.0 / den_even, 1.0 / den_odd)
        store(pair_a, o[:tq])
        store(pair_b, o[tq:])


PROMPT_QBLOCKS = 4


def _attn_prompt_kernel(sink_ref, q_ref, kp_ref, ko_ref, vp_ref, vo_ref, tbl_ref, o_ref):
    prev_shift = jnp.where(pl.program_id(1) == 0, MASK_VALUE, 0.0).astype(_F32)
    ops = [_kv_operands(kp_ref[...], vp_ref[...])]
    for b in range(PROMPT_QBLOCKS):
        rows = slice(b * WINDOW, (b + 1) * WINDOW)
        ops.append(_kv_operands(ko_ref[rows, :], vo_ref[rows, :]))

        def store(pair, o, rows=rows):
            o_ref[rows, pair * LANES:(pair + 1) * LANES] = o.astype(o_ref.dtype)

        _attend(q_ref[rows, :], ops[b], ops[b + 1], tbl_ref, sink_ref,
                prev_shift if b == 0 else jnp.float32(0.0), store)


def _attn_prompt(q, k, v, tbl, sinks, bsz, t):
    tq = PROMPT_QBLOCKS * WINDOW
    nb = t // tq
    own = lambda w: pl.BlockSpec((tq, w), lambda b, i: (b * nb + i, 0))
    prev = lambda w: pl.BlockSpec((WINDOW, w),
                                  lambda b, i: (PROMPT_QBLOCKS * (b * nb + i) - jnp.minimum(i, 1), 0))
    return pl.pallas_call(
        _attn_prompt_kernel,
        grid=(bsz, nb),
        in_specs=[pl.BlockSpec(memory_space=pltpu.SMEM), own(Q_DIM), prev(KV_DIM), own(KV_DIM),
                  prev(KV_DIM), own(KV_DIM), pl.BlockSpec(tbl.shape, lambda b, i: (0, 0, 0))],
        out_specs=own(Q_DIM),
        out_shape=jax.ShapeDtypeStruct((bsz * t, Q_DIM), _BF16),
        compiler_params=pltpu.CompilerParams(dimension_semantics=("arbitrary", "arbitrary"),
                                             vmem_limit_bytes=VMEM_LIMIT),
        name="attn_prompt",
    )(sinks, q, k, k, v, v, tbl)


SAMPLE_UNROLL = 2


def _block_diag_t(x):
    xb = x.astype(_BF16)
    z = jnp.zeros_like(xb)
    return jnp.concatenate([jnp.concatenate([xb, z], axis=1), jnp.concatenate([z, xb], axis=1)], axis=0)


def _attn_sample_kernel(sink_ref, q_ref, kn_ref, vn_ref, ck_ref, cv_ref, tbl_ref, o_ref, cko_ref, cvo_ref,
                        *, sb, steps):
    pad = jnp.zeros((WINDOW - steps, LANES), _F32)
    lane = lax.broadcasted_iota(jnp.int32, (HEAD_DIM, WINDOW), 1)

    def one_sequence(s, carry):
        def store(pair, o):
            o_ref[s, :, pair * LANES:(pair + 1) * LANES] = o

        prev_ops, own_ops = [], []
        for slab in range(KV_DIM // LANES):
            cols = slice(slab * LANES, (slab + 1) * LANES)
            new_k = jnp.concatenate([kn_ref[s][:, cols], pad], axis=0).T
            new_v = jnp.concatenate([vn_ref[s][:, cols], pad], axis=0).T
            for sub in range(2):
                kvh = 2 * slab + sub
                part = slice(sub * HEAD_DIM, (sub + 1) * HEAD_DIM)
                kt, vt = ck_ref[s, kvh], cv_ref[s, kvh]
                cko_ref[s, kvh] = pltpu.roll(jnp.where(lane < steps, new_k[part], kt), WINDOW - steps, axis=1)
                cvo_ref[s, kvh] = pltpu.roll(jnp.where(lane < steps, new_v[part], vt), WINDOW - steps, axis=1)
                prev_ops.append((_block_diag_t(kt), _block_diag_t(vt)))
                own_ops.append((_block_diag_t(new_k[part]), _block_diag_t(new_v[part])))
        _attend(q_ref[s], prev_ops, own_ops, tbl_ref, sink_ref, jnp.float32(0.0), store, transposed=True)
        return carry

    lax.fori_loop(0, sb, one_sequence, 0, unroll=SAMPLE_UNROLL)


def _attn_sample(q, k_new, v_new, cache_kt, cache_vt, tbl, sinks, sb):
    nseq, steps, _ = q.shape
    seq = lambda r, w: pl.BlockSpec((sb, r, w), lambda i: (i, 0, 0))
    cache = pl.BlockSpec((sb, N_KV_HEADS, HEAD_DIM, WINDOW), lambda i: (i, 0, 0, 0))
    return pl.pallas_call(
        functools.partial(_attn_sample_kernel, sb=sb, steps=steps),
        grid=(nseq // sb,),
        in_specs=[pl.BlockSpec(memory_space=pltpu.SMEM), seq(steps, Q_DIM), seq(steps, KV_DIM), seq(steps, KV_DIM),
                  cache, cache, pl.BlockSpec(tbl.shape, lambda i: (0, 0, 0))],
        out_specs=[seq(steps, Q_DIM), cache, cache],
        out_shape=[jax.ShapeDtypeStruct((nseq, steps, Q_DIM), _F32),
                   jax.ShapeDtypeStruct(cache_kt.shape, _F32), jax.ShapeDtypeStruct(cache_vt.shape, _F32)],
        compiler_params=pltpu.CompilerParams(dimension_semantics=("arbitrary",),
                                             vmem_limit_bytes=VMEM_LIMIT),
        name="attn_sample",
    )(sinks, q, k_new, v_new, cache_kt, cache_vt, tbl)


def _lane_min_index(mask, lane):
    return jnp.min(jnp.where(mask, lane, float(LANES)), axis=-1, keepdims=True)


def _finish_kernel(x_ref, conv_ref, o_ref, ga_ref, gb_ref, wa_ref, wo_ref, ng_ref, wr_ref, br_ref,
                   tri_ref, h_ref, hn_ref, route_ref, count_ref, running_ref):
    @pl.when(pl.program_id(0) == 0)
    def _():
        running_ref[...] = jnp.zeros_like(running_ref)

    attn_out = _dot(o_ref[...].astype(_BF16), wa_ref[...])
    merged = ga_ref[...].astype(_F32) * conv_ref[...].astype(_F32) + gb_ref[...].astype(_F32) * attn_out
    h = x_ref[...] + _dot(merged.astype(_BF16), wo_ref[...])
    h_ref[...] = h
    hn = h * lax.rsqrt(jnp.mean(h * h, axis=-1, keepdims=True) + EPS) * ng_ref[...]
    hn_ref[...] = _pack_bf16_pairs(hn)

    logits = _dot(hn.astype(_BF16), wr_ref[...]) + br_ref[...]
    lane = lax.broadcasted_iota(jnp.int32, logits.shape, 1).astype(_F32)
    gmask = lane < N_GROUPS
    gl = jnp.where(gmask, logits, MASK_VALUE)
    gmax = jnp.max(gl, axis=-1, keepdims=True)
    grp = _lane_min_index(gmask & (gl == gmax), lane)
    p_grp = 1.0 / jnp.sum(jnp.where(gmask, jnp.exp(gl - gmax), 0.0), axis=-1, keepdims=True)
    e_lo = N_GROUPS + grp * EXPERTS_PER_GROUP
    emask = (lane >= e_lo) & (lane < e_lo + EXPERTS_PER_GROUP)
    el = jnp.where(emask, logits, MASK_VALUE)
    ex = jnp.where(emask, jnp.exp(el - jnp.max(el, axis=-1, keepdims=True)), 0.0)
    prob = jnp.where(emask, ex / jnp.sum(ex, axis=-1, keepdims=True), -1.0)
    p1 = jnp.max(prob, axis=-1, keepdims=True)
    i1 = _lane_min_index(prob == p1, lane)
    rest = jnp.where(lane == i1, -1.0, prob)
    p2 = jnp.max(rest, axis=-1, keepdims=True)
    i2 = _lane_min_index(rest == p2, lane)
    w1 = p_grp * p1 / (p1 + p2)
    w2 = p_grp * p2 / (p1 + p2)
    e1 = i1 - N_GROUPS
    e2 = i2 - N_GROUPS

    hot1 = lane == e1
    hot2 = lane == e2
    hot = jnp.where(hot1 | hot2, 1.0, 0.0)
    before = _dot(tri_ref[...], hot.astype(_BF16)) + running_ref[...]
    rank1 = jnp.sum(jnp.where(hot1, before, 0.0), axis=-1, keepdims=True)
    rank2 = jnp.sum(jnp.where(hot2, before, 0.0), axis=-1, keepdims=True)
    running_ref[...] += jnp.sum(hot, axis=0, keepdims=True)
    count_ref[...] = jnp.broadcast_to(running_ref[...], count_ref.shape)

    fields = (e1, e2, w1, w2, rank1, rank2)
    route = jnp.zeros(logits.shape, _F32)
    for pos, val in enumerate(fields):
        route = jnp.where(lane == pos, val, route)
    route_ref[...] = route


ROUTE_E, ROUTE_W, ROUTE_RANK = 0, 2, 4
DEST_ROWS = 1024


def _finish(x, conv_out, o, ga, gb, wa_b, wo_b, ng, wr_b, br, tm):
    n = x.shape[0]
    tri = jnp.asarray(np.tril(np.ones((tm, tm), np.float32), -1), _BF16)
    row = lambda w: pl.BlockSpec((tm, w), lambda i: (i, 0))
    full = _resident
    return pl.pallas_call(
        _finish_kernel,
        grid=(n // tm,),
        in_specs=[row(D_MODEL), row(D_MODEL), row(Q_DIM), row(D_MODEL), row(D_MODEL),
                  full(wa_b), full(wo_b), full(ng), full(wr_b), full(br), full(tri)],
        out_specs=[row(D_MODEL), row(HALF), row(LANES), pl.BlockSpec((SUBLANES, LANES), lambda i: (0, 0))],
        out_shape=[jax.ShapeDtypeStruct((n, D_MODEL), _F32),
                   jax.ShapeDtypeStruct((n, HALF), jnp.uint32),
                   jax.ShapeDtypeStruct((n, LANES), _F32),
                   jax.ShapeDtypeStruct((SUBLANES, LANES), _F32)],
        scratch_shapes=[pltpu.VMEM((1, LANES), _F32)],
        compiler_params=pltpu.CompilerParams(dimension_semantics=("arbitrary",),
                                             vmem_limit_bytes=VMEM_LIMIT),
        name="finish",
    )(x, conv_out, o, ga, gb, wa_b, wo_b, ng, wr_b, br, tri)


def _dest_kernel(route_ref, starts_ref, dest_ref):
    route = route_ref[...]
    lane = lax.broadcasted_iota(jnp.int32, route.shape, 1)
    out = jnp.zeros(route.shape, jnp.int32)
    for j in range(TOP_K):
        e = route[:, ROUTE_E + j:ROUTE_E + j + 1].astype(jnp.int32)
        start = jnp.sum(jnp.where(lane == e, starts_ref[...], 0.0), axis=-1, keepdims=True)
        d = (start + route[:, ROUTE_RANK + j:ROUTE_RANK + j + 1]).astype(jnp.int32)
        out = jnp.where(lane == j, d, out)
    dest_ref[...] = out


def _dest(route, starts_row, tm):
    n = route.shape[0]
    row = pl.BlockSpec((tm, LANES), lambda i: (i, 0))
    return pl.pallas_call(
        _dest_kernel,
        grid=(n // tm,),
        in_specs=[row, pl.BlockSpec((1, LANES), lambda i: (0, 0))],
        out_specs=row,
        out_shape=jax.ShapeDtypeStruct((n, LANES), jnp.int32),
        name="dest",
    )(route, starts_row)


SC_CORES = 2
SC_SUBCORES = 16
SC_WORKERS = SC_CORES * SC_SUBCORES
SC_IN_FLIGHT = 4
SC_CHUNK_BYTES = 64 * 1024


def _sc_move_rows(src, idx, gather):
    n, d = src.shape
    b = idx.shape[0]
    per_worker = b // SC_WORKERS
    assert per_worker * SC_WORKERS == b and (gather or n % per_worker == 0), (b, n)
    chunk = min(per_worker // SC_IN_FLIGHT, SC_CHUNK_BYTES // (d * 4))
    n_iters = per_worker // (chunk * SC_IN_FLIGHT)
    assert n_iters * chunk * SC_IN_FLIGHT == per_worker and chunk % SUBLANES == 0, (per_worker, chunk)
    mesh = plsc.VectorSubcoreMesh(core_axis_name="c", subcore_axis_name="s",
                                  num_cores=SC_CORES, num_subcores=SC_SUBCORES)
    scratch = ([pltpu.VMEM((chunk,), jnp.int32)] * SC_IN_FLIGHT + [pltpu.VMEM((chunk, d), src.dtype)] * SC_IN_FLIGHT
               + [pltpu.SemaphoreType.DMA] * SC_IN_FLIGHT)

    @functools.partial(pl.kernel, mesh=mesh, out_type=jax.ShapeDtypeStruct((b, d), src.dtype),
                       scratch_types=scratch, name="sc_gather_rows" if gather else "sc_scatter_rows",
                       cost_estimate=pl.CostEstimate(flops=0, transcendentals=0, bytes_accessed=2 * b * d * 4 + b * 4))
    def move(src_hbm, idx_hbm, out_hbm, *bufs):
        idx_v = bufs[:SC_IN_FLIGHT]
        rows_v = bufs[SC_IN_FLIGHT:2 * SC_IN_FLIGHT]
        sems = bufs[2 * SC_IN_FLIGHT:]
        worker = lax.axis_index("s") * SC_CORES + lax.axis_index("c")

        @pl.loop(0, n_iters)
        def _(it):
            bases = [pl.multiple_of(worker * per_worker + (it * SC_IN_FLIGHT + j) * chunk, chunk)
                     for j in range(SC_IN_FLIGHT)]
            loads = [pltpu.async_copy(idx_hbm.at[pl.ds(bases[j], chunk)], idx_v[j], sems[j])
                     for j in range(SC_IN_FLIGHT)]
            reads = []
            for j in range(SC_IN_FLIGHT):
                loads[j].wait()
                if gather:
                    rows = src_hbm.at[idx_v[j]]
                else:
                    rows = src_hbm.at[pl.ds(pl.multiple_of(lax.rem(bases[j], n), chunk), chunk)]
                reads.append(pltpu.async_copy(rows, rows_v[j], sems[j]))
            writes = []
            for j in range(SC_IN_FLIGHT):
                reads[j].wait()
                dst = out_hbm.at[pl.ds(bases[j], chunk)] if gather else out_hbm.at[idx_v[j]]
                writes.append(pltpu.async_copy(rows_v[j], dst, sems[j]))
            for w in writes:
                w.wait()

    return move(src, idx)


def _sc_gather_rows(table, idx):
    return _sc_move_rows(table, idx, gather=True)


def _sc_scatter_rows(src, idx):
    return _sc_move_rows(src, idx, gather=False)


def _expert_kernel(blk_ref, exp_ref, lo_ref, hi_ref, x_ref, wg_ref, wu_ref, wd_ref, yb_ref,
                   wg_b, wu_b, wd_b, held_ref):
    del blk_ref
    k = pl.program_id(0)
    lo, hi, e = lo_ref[k], hi_ref[k], exp_ref[k]

    @pl.when(k == 0)
    def _():
        held_ref[0] = -1

    @pl.when(hi > lo)
    def _():
        @pl.when(held_ref[0] != e)
        def _():
            wg_b[...] = wg_ref[0].astype(_BF16)
            wu_b[...] = wu_ref[0].astype(_BF16)
            wd_b[...] = wd_ref[0].astype(_BF16)
            held_ref[0] = e

        xb = _unpack_bf16_pairs(x_ref[...]).astype(_BF16)
        g = _dot(xb, wg_b[...])
        u = _dot(xb, wu_b[...])
        hid = g * jax.nn.sigmoid(g) * u
        y = _dot(hid.astype(_BF16), wd_b[...])
        r = lax.broadcasted_iota(jnp.int32, yb_ref.shape, 0)
        pltpu.store(yb_ref, _pack_bf16_pairs(y), mask=(r >= lo) & (r < hi))


def _experts(items, xs, w_gate, w_up, w_down):
    n_items = items[0].shape[0]
    wspec = lambda a: pl.BlockSpec((1,) + a.shape[1:], lambda k, blk, exp, lo, hi: (exp[k], 0, 0))
    rows = pl.BlockSpec((MOE_ROWS, HALF), lambda k, blk, exp, lo, hi: (blk[k], 0))
    grid_spec = pltpu.PrefetchScalarGridSpec(
        num_scalar_prefetch=4,
        grid=(n_items,),
        in_specs=[rows, wspec(w_gate), wspec(w_up), wspec(w_down)],
        out_specs=rows,
        scratch_shapes=[pltpu.VMEM(w_gate.shape[1:], _BF16), pltpu.VMEM(w_up.shape[1:], _BF16),
                        pltpu.VMEM(w_down.shape[1:], _BF16), pltpu.SMEM((1,), jnp.int32)],
    )
    return pl.pallas_call(
        _expert_kernel,
        grid_spec=grid_spec,
        out_shape=jax.ShapeDtypeStruct(xs.shape, xs.dtype),
        compiler_params=pltpu.CompilerParams(dimension_semantics=("arbitrary",),
                                             vmem_limit_bytes=VMEM_LIMIT),
        cost_estimate=pl.CostEstimate(
            flops=n_items * MOE_ROWS * 6 * D_MODEL * D_EXPERT, transcendentals=n_items * MOE_ROWS * D_EXPERT,
            bytes_accessed=2 * xs.size * 4 + (w_gate.size + w_up.size + w_down.size) * 4),
        name="experts",
    )(*items, xs, w_gate, w_up, w_down)


def _combine_kernel(h_ref, route_ref, g0_ref, g1_ref, y_ref):
    route = route_ref[...]
    y_ref[...] = (h_ref[...] + route[:, ROUTE_W:ROUTE_W + 1] * _unpack_bf16_pairs(g0_ref[...])
                  + route[:, ROUTE_W + 1:ROUTE_W + 2] * _unpack_bf16_pairs(g1_ref[...]))


def _combine(h, route, g, tm):
    n = h.shape[0]
    nt = n // tm
    row = lambda w: pl.BlockSpec((tm, w), lambda i: (i, 0))
    return pl.pallas_call(
        _combine_kernel,
        grid=(nt,),
        in_specs=[row(D_MODEL), row(LANES), row(HALF), pl.BlockSpec((tm, HALF), lambda i: (nt + i, 0))],
        out_specs=row(D_MODEL),
        out_shape=jax.ShapeDtypeStruct((n, D_MODEL), _F32),
        compiler_params=pltpu.CompilerParams(dimension_semantics=("arbitrary",),
                                             vmem_limit_bytes=VMEM_LIMIT),
        name="combine",
    )(h, route, g, g)


def _work_items(counts, n_pairs):
    n_blocks = n_pairs // MOE_ROWS
    starts = jnp.cumsum(counts) - counts
    cuts = jnp.sort(jnp.concatenate([jnp.arange(n_blocks, dtype=jnp.int32) * MOE_ROWS, starts]))
    ends = jnp.concatenate([cuts[1:], jnp.full((1,), n_pairs, jnp.int32)])
    blk = jnp.minimum(cuts // MOE_ROWS, n_blocks - 1)
    expert = jnp.clip(jnp.sum(starts[None, :] <= cuts[:, None], axis=1) - 1, 0, N_EXPERTS - 1).astype(jnp.int32)
    return starts, (blk, expert, cuts - blk * MOE_ROWS, ends - blk * MOE_ROWS)


def _dispatch(hn, route, counts_rows):
    n = hn.shape[0]
    n_pairs = n * TOP_K
    counts = counts_rows[0, :N_EXPERTS].astype(jnp.int32)
    starts, items = _work_items(counts, n_pairs)
    starts_row = jnp.zeros((1, LANES), _F32).at[0, :N_EXPERTS].set(starts.astype(_F32))
    dest = _dest(route, starts_row, min(n, DEST_ROWS))[:, :TOP_K].T.reshape(n_pairs)
    return _sc_scatter_rows(hn, dest), dest, items


def _after(value, *earlier):
    return lax.optimization_barrier((value,) + earlier)[0]


def kernel(x_prompt, x_sample, state_conv, cache_k, cache_v, norm_attn_g, w_in, q_norm_g, k_norm_g, rel_bias, attn_sinks, w_dw, b_dw, conv_ln_g, conv_ln_b, w_conv_out, b_conv_out, w_attn_out, w_out, norm_ffn_g, w_grp, b_grp, w_router, b_router, w_gate, w_up, w_down):
    bsz, t, _ = x_prompt.shape
    nseq, steps, _ = x_sample.shape
    row = lambda a: a.reshape(1, -1).astype(_F32)

    w_in_b = w_in.astype(_BF16)
    wco_b = w_conv_out.astype(_BF16)
    wa_b = w_attn_out.astype(_BF16)
    wo_b = w_out.astype(_BF16)
    qg = row(jnp.tile(q_norm_g, N_HEADS)) * (HEAD_DIM ** -0.5)
    kg = row(jnp.tile(k_norm_g, N_KV_HEADS))
    w_rt = jnp.zeros((D_MODEL, LANES), _F32).at[:, :N_GROUPS].set(w_grp).at[:, N_GROUPS:N_GROUPS + N_EXPERTS].set(w_router)
    wr_b = w_rt.astype(_BF16)
    b_rt = jnp.zeros((1, LANES), _F32).at[0, :N_GROUPS].set(b_grp).at[0, N_GROUPS:N_GROUPS + N_EXPERTS].set(b_router)
    tbl = _bias_tables(rel_bias)
    conv_params = (w_dw, row(b_dw), row(conv_ln_g), row(conv_ln_b), wco_b, row(b_conv_out))

    def finish(x2d, conv_out, o, ga, gb):
        return _finish(x2d, conv_out, o, ga, gb, wa_b, wo_b, row(norm_ffn_g), wr_b, b_rt, ROW_TILE)

    xp = x_prompt.reshape(bsz * t, D_MODEL)
    q, k, v, ga, gb, conv_out, glu_tail = _inproj_conv(x_prompt, row(norm_attn_g), w_in_b, qg, kg, *conv_params,
                                                       ROW_TILE // 2)
    o = _attn_prompt(q, k, v, tbl, attn_sinks, bsz, t)
    h_p, hn_p, route_p, counts_p = finish(xp, conv_out, o, ga, gb)
    rows_p, dest_p, items_p = _dispatch(hn_p, route_p, counts_p)
    state_conv_prompt = glu_tail[:, HALO - (CONV_WIDTH - 1):]
    tail = lambda a: a.reshape(bsz, t, KV_DIM)[:, t - WINDOW:].reshape(bsz, WINDOW, N_KV_HEADS, HEAD_DIM)
    cache_k_prompt, cache_v_prompt = tail(k), tail(v)

    xs = _after(x_sample, dest_p).reshape(nseq * steps, D_MODEL)
    glu, q, k, v, ga, gb = _inproj(xs, row(norm_attn_g), w_in_b, qg, kg, _F32, ROW_TILE)
    glu_t = glu.reshape(nseq, steps, D_CONV).transpose(1, 0, 2)
    conv_out, state_t = _conv_sample(state_conv.transpose(1, 0, 2), glu_t, *conv_params, 64)
    conv_out = conv_out.transpose(1, 0, 2).reshape(nseq * steps, D_MODEL)
    k3 = k.reshape(nseq, steps, KV_DIM)
    v3 = v.reshape(nseq, steps, KV_DIM)
    o, ck_t, cv_t = _attn_sample(q.reshape(nseq, steps, Q_DIM), k3, v3, cache_k.transpose(0, 2, 3, 1),
                                 cache_v.transpose(0, 2, 3, 1), tbl, attn_sinks, 8)
    h_s, hn_s, route_s, counts_s = finish(xs, conv_out, o.reshape(nseq * steps, Q_DIM), ga, gb)
    rows_s, dest_s, items_s = _dispatch(hn_s, route_s, counts_s)
    state_conv_sample = state_t.transpose(1, 0, 2)
    cache_k_sample = ck_t.transpose(0, 3, 1, 2)
    cache_v_sample = cv_t.transpose(0, 3, 1, 2)

    yb_p = _experts(items_p, rows_p, w_gate, w_up, w_down)
    g_p = _sc_gather_rows(yb_p, dest_p)
    yb_s = _experts(items_s, _after(rows_s, yb_p), w_gate, w_up, w_down)
    g_s = _sc_gather_rows(yb_s, dest_s)
    y_prompt = _combine(h_p, route_p, g_p, ROW_TILE).reshape(bsz, t, D_MODEL)
    y_sample = _combine(h_s, route_s, _after(g_s, y_prompt), ROW_TILE).reshape(nseq, steps, D_MODEL)

    return (y_prompt, y_sample, state_conv_prompt, cache_k_prompt, cache_v_prompt,
            state_conv_sample, cache_k_sample, cache_v_sample)
```

```python
import functools
import math

import numpy as np
import jax
import jax.numpy as jnp
from jax import lax
from jax.experimental import pallas as pl
from jax.experimental.pallas import tpu as pltpu
from jax.experimental.pallas import tpu_sc as plsc

D_MODEL = 1024
N_HEADS = 16
HEAD_DIM = 64
N_KV_HEADS = 4
WINDOW = 128
Q_DIM = N_HEADS * HEAD_DIM
KV_DIM = N_KV_HEADS * HEAD_DIM
N_BUCKETS = 32
MAX_EXACT = N_BUCKETS // 2
MAX_DISTANCE = 128
D_CONV = D_MODEL
CONV_WIDTH = 31
N_GROUPS = 4
EXPERTS_PER_GROUP = 8
N_EXPERTS = N_GROUPS * EXPERTS_PER_GROUP
TOP_K = 2
D_EXPERT = 256
EPS = 1e-6

LANES = 128
SUBLANES = 8
N_PAIRS = N_HEADS // 2
MOE_ROWS = 512
MASK_VALUE = -1e30
VMEM_LIMIT = 56 * 1024 * 1024
ROW_TILE = 512
SAMPLE_CONV_SEQS = 64
SAMPLE_ATTN_SEQS = 8

_F32 = jnp.float32
_BF16 = jnp.bfloat16


def _resident(a):
    return pl.BlockSpec(a.shape, lambda *_: (0,) * a.ndim, pipeline_mode=pl.Buffered(1))


def _dot(a, b):
    return jnp.dot(a, b, preferred_element_type=_F32)


HALF = D_MODEL // 2


def _pack_pair(lo, hi):
    lo_bits = pltpu.bitcast(lo.astype(_BF16).astype(_F32), jnp.uint32)
    hi_bits = pltpu.bitcast(hi.astype(_BF16).astype(_F32), jnp.uint32)
    return hi_bits | (lo_bits >> 16)


def _pack_bf16_pairs(x):
    return _pack_pair(x[:, :HALF], x[:, HALF:])


def _unpack_bf16_pairs(w):
    lo = pltpu.bitcast(w << 16, _F32)
    hi = pltpu.bitcast(w & jnp.uint32(0xFFFF0000), _F32)
    return jnp.concatenate([lo, hi], axis=1)


def _head_rms_scale(z):
    low = lax.broadcasted_iota(jnp.int32, (z.shape[0], LANES), 1) < HEAD_DIM
    slabs = []
    for c in range(z.shape[1] // LANES):
        sq = z[:, c * LANES:(c + 1) * LANES]
        sq = sq * sq
        first = jnp.sum(jnp.where(low, sq, 0.0), axis=-1, keepdims=True)
        second = jnp.sum(jnp.where(low, 0.0, sq), axis=-1, keepdims=True)
        slabs.append(lax.rsqrt(jnp.where(low, first, second) * (1.0 / HEAD_DIM) + EPS))
    return jnp.concatenate(slabs, axis=1)


def _inproj_kernel(x_ref, g_ref, w_ref, qg_ref, kg_ref,
                   glu_ref, q_ref, k_ref, v_ref, ga_ref, gb_ref):
    x = x_ref[...]
    xn = x * lax.rsqrt(jnp.mean(x * x, axis=-1, keepdims=True) + EPS) * g_ref[...]
    xb = xn.astype(_BF16)

    def seg(lo, width):
        return _dot(xb, w_ref[:, lo:lo + width])

    a = seg(0, D_CONV)
    b = seg(D_CONV, D_CONV)
    glu_ref[...] = a * jax.nn.sigmoid(b)
    off = 2 * D_CONV
    q = seg(off, Q_DIM)
    q_ref[...] = (q * _head_rms_scale(q) * qg_ref[...]).astype(q_ref.dtype)
    off += Q_DIM
    k = seg(off, KV_DIM)
    k_ref[...] = k * _head_rms_scale(k) * kg_ref[...]
    off += KV_DIM
    v_ref[...] = seg(off, KV_DIM)
    off += KV_DIM
    ga_ref[...] = jax.nn.sigmoid(seg(off, D_MODEL)).astype(ga_ref.dtype)
    off += D_MODEL
    gb_ref[...] = jax.nn.sigmoid(seg(off, D_MODEL)).astype(gb_ref.dtype)


def _inproj(x, g, w_in_b, qg, kg, q_dtype, tm):
    n = x.shape[0]
    row = lambda w: pl.BlockSpec((tm, w), lambda i: (i, 0))
    full = _resident
    return pl.pallas_call(
        _inproj_kernel,
        grid=(n // tm,),
        in_specs=[row(D_MODEL), full(g), full(w_in_b), full(qg), full(kg)],
        out_specs=[row(D_CONV), row(Q_DIM), row(KV_DIM), row(KV_DIM), row(D_MODEL), row(D_MODEL)],
        out_shape=[jax.ShapeDtypeStruct((n, D_CONV), _F32),
                   jax.ShapeDtypeStruct((n, Q_DIM), q_dtype),
                   jax.ShapeDtypeStruct((n, KV_DIM), _F32),
                   jax.ShapeDtypeStruct((n, KV_DIM), _F32),
                   jax.ShapeDtypeStruct((n, D_MODEL), _BF16),
                   jax.ShapeDtypeStruct((n, D_MODEL), _BF16)],
        compiler_params=pltpu.CompilerParams(dimension_semantics=("arbitrary",),
                                             vmem_limit_bytes=VMEM_LIMIT),
        name="inproj",
    )(x, g, w_in_b, qg, kg)


def _ln_swish_project(y, lng_ref, lnb_ref, wo_ref, bo_ref):
    mu = jnp.mean(y, axis=-1, keepdims=True)
    yc = y - mu
    var = jnp.mean(yc * yc, axis=-1, keepdims=True)
    z = yc * lax.rsqrt(var + EPS) * lng_ref[...] + lnb_ref[...]
    z = z * jax.nn.sigmoid(z)
    return (_dot(z.astype(_BF16), wo_ref[...]) + bo_ref[...]).astype(_BF16)


HALO = 32
CONV_STEPS = 16
CH_TILES = D_CONV // LANES


PIECE = 256
PAIR_TILES = 2 * CH_TILES


def _inproj_conv_kernel(x_ref, g_ref, w_ref, qg_ref, kg_ref, w16_ref, b16_ref, lng_ref, lnb_ref, wo_ref, bo_ref,
                        q_ref, k_ref, v_ref, ga_ref, gb_ref, conv_ref, tail_ref, hist_ref, y_ref, *, tm):
    i = pl.program_id(1)
    rows = 2 * tm

    @pl.when(i == 0)
    def _():
        hist_ref[0:HALO * CH_TILES, :] = jnp.zeros((HALO * CH_TILES, LANES), jnp.uint32)

    @pl.when(i > 0)
    def _():
        hist_ref[0:HALO * CH_TILES, :] = hist_ref[tm * CH_TILES:(tm + HALO) * CH_TILES, :]

    x = jnp.concatenate([x_ref[0], x_ref[1]], axis=0)
    xn = x * lax.rsqrt(jnp.mean(x * x, axis=-1, keepdims=True) + EPS) * g_ref[...]
    xb = xn.astype(_BF16)

    def seg(lo, width):
        return _dot(xb, w_ref[:, lo:lo + width])

    def put(ref, lo, val):
        ref[0, :, lo:lo + PIECE] = val[:tm].astype(ref.dtype)
        ref[1, :, lo:lo + PIECE] = val[tm:].astype(ref.dtype)

    for p in range(D_CONV // PIECE):
        lo = p * PIECE
        glu = seg(lo, PIECE) * jax.nn.sigmoid(seg(D_CONV + lo, PIECE))
        tail_ref[0, :, lo:lo + PIECE] = glu[tm - HALO:tm, :]
        tail_ref[1, :, lo:lo + PIECE] = glu[rows - HALO:, :]
        words = _pack_pair(glu[:tm], glu[tm:])
        for c in range(PIECE // LANES):
            tile = lo // LANES + c
            hist_ref[pl.ds(HALO * CH_TILES + tile, tm, stride=CH_TILES), :] = words[:, c * LANES:(c + 1) * LANES]

    first = HALO - (CONV_WIDTH - 1)

    def conv_chunk(ci, carry):
        t0 = ci * CONV_STEPS
        acc = jnp.zeros((CONV_STEPS, PAIR_TILES, LANES), _F32)
        for j in range(CONV_WIDTH):
            lo = pl.multiple_of((t0 + first + j) * CH_TILES, CH_TILES)
            xw = pltpu.bitcast(hist_ref[pl.ds(lo, CONV_STEPS * CH_TILES), :], _BF16)
            acc = acc + (xw.reshape(CONV_STEPS, PAIR_TILES, LANES).astype(_F32)
                         * w16_ref[j][None].astype(_F32))
        y_ref[pl.ds(pl.multiple_of(t0 * PAIR_TILES, PAIR_TILES), CONV_STEPS * PAIR_TILES), :] = (
            (acc + b16_ref[...][None]).reshape(CONV_STEPS * PAIR_TILES, LANES))
        return carry

    def q_piece(lo):
        q = seg(2 * D_CONV + lo, PIECE)
        put(q_ref, lo, q * _head_rms_scale(q) * qg_ref[:, lo:lo + PIECE])

    def k_piece(lo):
        k = seg(2 * D_CONV + Q_DIM + lo, PIECE)
        put(k_ref, lo, k * _head_rms_scale(k) * kg_ref[:, lo:lo + PIECE])

    def v_piece(lo):
        put(v_ref, lo, seg(2 * D_CONV + Q_DIM + KV_DIM + lo, PIECE))

    def gate_piece(ref, base, lo):
        put(ref, lo, jax.nn.sigmoid(seg(base + lo, PIECE)))

    gate_base = 2 * D_CONV + Q_DIM + 2 * KV_DIM
    pieces = ([functools.partial(q_piece, lo) for lo in range(0, Q_DIM, PIECE)]
              + [functools.partial(k_piece, lo) for lo in range(0, KV_DIM, PIECE)]
              + [functools.partial(v_piece, lo) for lo in range(0, KV_DIM, PIECE)]
              + [functools.partial(gate_piece, ga_ref, gate_base, lo) for lo in range(0, D_MODEL, PIECE)]
              + [functools.partial(gate_piece, gb_ref, gate_base + D_MODEL, lo) for lo in range(0, D_MODEL, PIECE)])

    for piece in pieces:
        piece()
    lax.fori_loop(0, tm // CONV_STEPS, conv_chunk, 0)

    y = jnp.concatenate(
        [jnp.concatenate([y_ref[pl.ds(2 * c + s, tm, stride=PAIR_TILES), :] for c in range(CH_TILES)], axis=1)
         for s in range(2)], axis=0)
    out = _ln_swish_project(y, lng_ref, lnb_ref, wo_ref, bo_ref)
    conv_ref[0] = out[:tm]
    conv_ref[1] = out[tm:]


def _inproj_conv(x, g, w_in_b, qg, kg, w_dw, b_dw, lng, lnb, wo_b, bo, tm):
    bsz, t, _ = x.shape
    assert CH_TILES == SUBLANES and bsz % 2 == 0 and t % tm == 0
    nt = t // tm
    w16 = jnp.repeat(w_dw.reshape(CONV_WIDTH, CH_TILES, LANES), 2, axis=1).astype(_BF16)
    b16 = jnp.repeat(b_dw.reshape(CH_TILES, LANES), 2, axis=0)
    full = _resident
    blk = lambda w: pl.BlockSpec((2, tm, w), lambda p, i: (p, i, 0))
    shape = lambda w, dt: jax.ShapeDtypeStruct((bsz, t, w), dt)
    outs = pl.pallas_call(
        functools.partial(_inproj_conv_kernel, tm=tm),
        grid=(bsz // 2, nt),
        in_specs=[blk(D_MODEL), full(g), full(w_in_b), full(qg), full(kg), full(w16), full(b16), full(lng), full(lnb),
                  full(wo_b), full(bo)],
        out_specs=[blk(Q_DIM), blk(KV_DIM), blk(KV_DIM), blk(D_MODEL), blk(D_MODEL), blk(D_MODEL),
                   pl.BlockSpec((2, HALO, D_CONV), lambda p, i: (p, 0, 0))],
        out_shape=[shape(Q_DIM, _BF16), shape(KV_DIM, _F32), shape(KV_DIM, _F32), shape(D_MODEL, _BF16),
                   shape(D_MODEL, _BF16), shape(D_MODEL, _BF16), jax.ShapeDtypeStruct((bsz, HALO, D_CONV), _F32)],
        scratch_shapes=[pltpu.VMEM(((tm + HALO) * CH_TILES, LANES), jnp.uint32),
                        pltpu.VMEM((tm * PAIR_TILES, LANES), _F32)],
        compiler_params=pltpu.CompilerParams(dimension_semantics=("arbitrary", "arbitrary"),
                                             vmem_limit_bytes=VMEM_LIMIT),
        name="inproj_conv",
    )(x, g, w_in_b, qg, kg, w16, b16, lng, lnb, wo_b, bo)
    return [o.reshape(bsz * t, o.shape[-1]) for o in outs[:-1]] + [outs[-1]]


def _conv_sample_kernel(state_ref, glu_ref, wdw_ref, bdw_ref, lng_ref, lnb_ref, wo_ref, bo_ref,
                        out_ref, state_out_ref):
    keep, steps = state_ref.shape[0], glu_ref.shape[0]

    def hist(u):
        return state_ref[u] if u < keep else glu_ref[u - keep]

    for t in range(steps):
        acc = hist(t) * wdw_ref[0:1, :]
        for j in range(1, CONV_WIDTH):
            acc = acc + hist(t + j) * wdw_ref[j:j + 1, :]
        out_ref[t] = _ln_swish_project(acc + bdw_ref[...], lng_ref, lnb_ref, wo_ref, bo_ref)
    state_out_ref[0:keep - steps] = state_ref[steps:keep]
    state_out_ref[keep - steps:keep] = glu_ref[...]


def _conv_sample(state_t, glu_t, w_dw, b_dw, lng, lnb, wo_b, bo, sb):
    keep, nseq, _ = state_t.shape
    steps = glu_t.shape[0]
    full = _resident
    blk = lambda r: pl.BlockSpec((r, sb, D_CONV), lambda i: (0, i, 0))
    return pl.pallas_call(
        _conv_sample_kernel,
        grid=(nseq // sb,),
        in_specs=[blk(keep), blk(steps), full(w_dw), full(b_dw), full(lng), full(lnb), full(wo_b), full(bo)],
        out_specs=[blk(steps), blk(keep)],
        out_shape=[jax.ShapeDtypeStruct((steps, nseq, D_MODEL), _BF16),
                   jax.ShapeDtypeStruct(state_t.shape, _F32)],
        compiler_params=pltpu.CompilerParams(dimension_semantics=("arbitrary",),
                                             vmem_limit_bytes=VMEM_LIMIT),
        name="conv_sample",
    )(state_t, glu_t, w_dw, b_dw, lng, lnb, wo_b, bo)


def _bucket_map():
    i = np.arange(WINDOW)[:, None]
    j = np.arange(WINDOW)[None, :]
    n = (i - j) % WINDOW
    nf = np.maximum(n, 1).astype(np.float32)
    large = MAX_EXACT + (np.log(nf / np.float32(MAX_EXACT)) / np.float32(math.log(MAX_DISTANCE / MAX_EXACT))
                         * np.float32(N_BUCKETS - MAX_EXACT)).astype(np.int32)
    return np.where(n < MAX_EXACT, n, np.minimum(large, N_BUCKETS - 1)).astype(np.int32)


def _bias_table_kernel(rb_ref, bm_ref, tbl_ref):
    p = pl.program_id(0)
    bm = bm_ref[...]
    for half in range(2):
        h = 2 * p + half
        t = jnp.zeros(bm.shape, _F32)
        for b in range(N_BUCKETS):
            t = jnp.where(bm == b, rb_ref[b, h], t)
        tbl_ref[0, :, half * WINDOW:(half + 1) * WINDOW] = t


def _bias_tables(rel_bias):
    bm = jnp.asarray(_bucket_map())
    return pl.pallas_call(
        _bias_table_kernel,
        grid=(N_PAIRS,),
        in_specs=[pl.BlockSpec(memory_space=pltpu.SMEM), pl.BlockSpec(bm.shape, lambda p: (0, 0))],
        out_specs=pl.BlockSpec((1, WINDOW, 2 * WINDOW), lambda p: (p, 0, 0)),
        out_shape=jax.ShapeDtypeStruct((N_PAIRS, WINDOW, 2 * WINDOW), _F32),
        name="bias_tables",
    )(rel_bias, bm)


def _block_diag_pairs(slab):
    low = lax.broadcasted_iota(jnp.int32, slab.shape, 1) < HEAD_DIM
    swapped = pltpu.roll(slab, HEAD_DIM, axis=1)
    zero = jnp.zeros_like(slab)
    first = jnp.concatenate([jnp.where(low, slab, zero), jnp.where(low, zero, swapped)], axis=0)
    second = jnp.concatenate([jnp.where(low, swapped, zero), jnp.where(low, zero, slab)], axis=0)
    return first.astype(_BF16), second.astype(_BF16)


def _kv_operands(k_blk, v_blk):
    ops = []
    for slab in range(KV_DIM // LANES):
        cols = slice(slab * LANES, (slab + 1) * LANES)
        ops.extend(zip(_block_diag_pairs(k_blk[:, cols]), _block_diag_pairs(v_blk[:, cols])))
    return ops


def _attend(q, prev_ops, own_ops, tbl_ref, sink_ref, prev_shift, store, transposed=False):
    tq = q.shape[0]
    rows = 2 * tq
    row = lax.broadcasted_iota(jnp.int32, (rows, 2 * WINDOW), 0)
    col = lax.broadcasted_iota(jnp.int32, (rows, 2 * WINDOW), 1)
    from_prev = (col & (WINDOW - 1)) > jnp.where(row >= tq, row - tq, row)
    top = lax.broadcasted_iota(jnp.int32, (rows, 1), 0) < tq
    low = lax.broadcasted_iota(jnp.int32, (rows, LANES), 1) < HEAD_DIM
    contract_last = (((1,), (1,)), ((), ()))

    def logits(a, k_op):
        return _dot(a, k_op) if transposed else lax.dot_general(a, k_op, contract_last, preferred_element_type=_F32)

    def weighted_values(pr, v_op):
        return lax.dot_general(pr, v_op, contract_last, preferred_element_type=_F32) if transposed else _dot(pr, v_op)

    for kvh in range(N_KV_HEADS):
        (k_prev, v_prev), (k_own, v_own) = prev_ops[kvh], own_ops[kvh]
        pair_a = 2 * kvh
        pair_b = pair_a + 1
        qq = jnp.concatenate([q[:, pair_a * LANES:(pair_a + 1) * LANES],
                              q[:, pair_b * LANES:(pair_b + 1) * LANES]], axis=0).astype(_BF16)
        sp = logits(qq, k_prev)
        so = logits(qq, k_own)
        bias = jnp.concatenate([tbl_ref[pair_a, 0:tq, :], tbl_ref[pair_b, 0:tq, :]], axis=0)
        s = jnp.where(from_prev, sp + prev_shift, so) + bias
        sink_even = jnp.where(top, sink_ref[2 * pair_a], sink_ref[2 * pair_b])
        sink_odd = jnp.where(top, sink_ref[2 * pair_a + 1], sink_ref[2 * pair_b + 1])
        m_even = jnp.maximum(jnp.max(s[:, :WINDOW], axis=-1, keepdims=True), sink_even)
        m_odd = jnp.maximum(jnp.max(s[:, WINDOW:], axis=-1, keepdims=True), sink_odd)
        e_even = jnp.exp(s[:, :WINDOW] - m_even)
        e_odd = jnp.exp(s[:, WINDOW:] - m_odd)
        p = jnp.concatenate([e_even, e_odd], axis=1).astype(_BF16)
        zero = jnp.zeros_like(p)
        o = (weighted_values(jnp.where(from_prev, p, zero), v_prev)
             + weighted_values(jnp.where(from_prev, zero, p), v_own))
        den_even = jnp.sum(e_even, axis=-1, keepdims=True) + jnp.exp(sink_even - m_even)
        den_odd = jnp.sum(e_odd, axis=-1, keepdims=True) + jnp.exp(sink_odd - m_odd)
        o = o * jnp.where(low, 1.0 / den_even, 1.0 / den_odd)
        store(pair_a, o[:tq])
        store(pair_b, o[tq:])


PROMPT_QBLOCKS = 4


def _attn_prompt_kernel(sink_ref, q_ref, kp_ref, ko_ref, vp_ref, vo_ref, tbl_ref, o_ref):
    prev_shift = jnp.where(pl.program_id(1) == 0, MASK_VALUE, 0.0).astype(_F32)
    ops = [_kv_operands(kp_ref[...], vp_ref[...])]
    for b in range(PROMPT_QBLOCKS):
        rows = slice(b * WINDOW, (b + 1) * WINDOW)
        ops.append(_kv_operands(ko_ref[rows, :], vo_ref[rows, :]))

        def store(pair, o, rows=rows):
            o_ref[rows, pair * LANES:(pair + 1) * LANES] = o.astype(o_ref.dtype)

        _attend(q_ref[rows, :], ops[b], ops[b + 1], tbl_ref, sink_ref,
                prev_shift if b == 0 else jnp.float32(0.0), store)


def _attn_prompt(q, k, v, tbl, sinks, bsz, t):
    tq = PROMPT_QBLOCKS * WINDOW
    nb = t // tq
    own = lambda w: pl.BlockSpec((tq, w), lambda b, i: (b * nb + i, 0))
    prev = lambda w: pl.BlockSpec((WINDOW, w),
                                  lambda b, i: (PROMPT_QBLOCKS * (b * nb + i) - jnp.minimum(i, 1), 0))
    return pl.pallas_call(
        _attn_prompt_kernel,
        grid=(bsz, nb),
        in_specs=[pl.BlockSpec(memory_space=pltpu.SMEM), own(Q_DIM), prev(KV_DIM), own(KV_DIM),
                  prev(KV_DIM), own(KV_DIM), pl.BlockSpec(tbl.shape, lambda b, i: (0, 0, 0))],
        out_specs=own(Q_DIM),
        out_shape=jax.ShapeDtypeStruct((bsz * t, Q_DIM), _BF16),
        compiler_params=pltpu.CompilerParams(dimension_semantics=("arbitrary", "arbitrary"),
                                             vmem_limit_bytes=VMEM_LIMIT),
        name="attn_prompt",
    )(sinks, q, k, k, v, v, tbl)


SAMPLE_UNROLL = 2


def _block_diag_t(x):
    xb = x.astype(_BF16)
    z = jnp.zeros_like(xb)
    return jnp.concatenate([jnp.concatenate([xb, z], axis=1), jnp.concatenate([z, xb], axis=1)], axis=0)


def _attn_sample_kernel(sink_ref, q_ref, kn_ref, vn_ref, ck_ref, cv_ref, tbl_ref, o_ref, cko_ref, cvo_ref,
                        *, sb, steps):
    pad = jnp.zeros((WINDOW - steps, LANES), _F32)
    lane = lax.broadcasted_iota(jnp.int32, (HEAD_DIM, WINDOW), 1)

    def one_sequence(s, carry):
        def store(pair, o):
            o_ref[s, :, pair * LANES:(pair + 1) * LANES] = o

        prev_ops, own_ops = [], []
        for slab in range(KV_DIM // LANES):
            cols = slice(slab * LANES, (slab + 1) * LANES)
            new_k = jnp.concatenate([kn_ref[s][:, cols], pad], axis=0).T
            new_v = jnp.concatenate([vn_ref[s][:, cols], pad], axis=0).T
            for sub in range(2):
                kvh = 2 * slab + sub
                part = slice(sub * HEAD_DIM, (sub + 1) * HEAD_DIM)
                kt, vt = ck_ref[s, kvh], cv_ref[s, kvh]
                cko_ref[s, kvh] = pltpu.roll(jnp.where(lane < steps, new_k[part], kt), WINDOW - steps, axis=1)
                cvo_ref[s, kvh] = pltpu.roll(jnp.where(lane < steps, new_v[part], vt), WINDOW - steps, axis=1)
                prev_ops.append((_block_diag_t(kt), _block_diag_t(vt)))
                own_ops.append((_block_diag_t(new_k[part]), _block_diag_t(new_v[part])))
        _attend(q_ref[s], prev_ops, own_ops, tbl_ref, sink_ref, jnp.float32(0.0), store, transposed=True)
        return carry

    lax.fori_loop(0, sb, one_sequence, 0, unroll=SAMPLE_UNROLL)


def _attn_sample(q, k_new, v_new, cache_kt, cache_vt, tbl, sinks, sb):
    nseq, steps, _ = q.shape
    seq = lambda r, w: pl.BlockSpec((sb, r, w), lambda i: (i, 0, 0))
    cache = pl.BlockSpec((sb, N_KV_HEADS, HEAD_DIM, WINDOW), lambda i: (i, 0, 0, 0))
    return pl.pallas_call(
        functools.partial(_attn_sample_kernel, sb=sb, steps=steps),
        grid=(nseq // sb,),
        in_specs=[pl.BlockSpec(memory_space=pltpu.SMEM), seq(steps, Q_DIM), seq(steps, KV_DIM), seq(steps, KV_DIM),
                  cache, cache, pl.BlockSpec(tbl.shape, lambda i: (0, 0, 0))],
        out_specs=[seq(steps, Q_DIM), cache, cache],
        out_shape=[jax.ShapeDtypeStruct((nseq, steps, Q_DIM), _F32),
                   jax.ShapeDtypeStruct(cache_kt.shape, _F32), jax.ShapeDtypeStruct(cache_vt.shape, _F32)],
        compiler_params=pltpu.CompilerParams(dimension_semantics=("arbitrary",),
                                             vmem_limit_bytes=VMEM_LIMIT),
        name="attn_sample",
    )(sinks, q, k_new, v_new, cache_kt, cache_vt, tbl)


def _lane_min_index(mask, lane):
    return jnp.min(jnp.where(mask, lane, float(LANES)), axis=-1, keepdims=True)


def _finish_kernel(x_ref, conv_ref, o_ref, ga_ref, gb_ref, wa_ref, wo_ref, ng_ref, wr_ref, br_ref,
                   tri_ref, h_ref, hn_ref, route_ref, count_ref, running_ref):
    @pl.when(pl.program_id(0) == 0)
    def _():
        running_ref[...] = jnp.zeros_like(running_ref)

    attn_out = _dot(o_ref[...].astype(_BF16), wa_ref[...])
    merged = ga_ref[...].astype(_F32) * conv_ref[...].astype(_F32) + gb_ref[...].astype(_F32) * attn_out
    h = x_ref[...] + _dot(merged.astype(_BF16), wo_ref[...])
    h_ref[...] = h
    hn = h * lax.rsqrt(jnp.mean(h * h, axis=-1, keepdims=True) + EPS) * ng_ref[...]
    hn_ref[...] = _pack_bf16_pairs(hn)

    logits = _dot(hn.astype(_BF16), wr_ref[...]) + br_ref[...]
    lane = lax.broadcasted_iota(jnp.int32, logits.shape, 1).astype(_F32)
    gmask = lane < N_GROUPS
    gl = jnp.where(gmask, logits, MASK_VALUE)
    gmax = jnp.max(gl, axis=-1, keepdims=True)
    grp = _lane_min_index(gmask & (gl == gmax), lane)
    p_grp = 1.0 / jnp.sum(jnp.where(gmask, jnp.exp(gl - gmax), 0.0), axis=-1, keepdims=True)
    e_lo = N_GROUPS + grp * EXPERTS_PER_GROUP
    emask = (lane >= e_lo) & (lane < e_lo + EXPERTS_PER_GROUP)
    el = jnp.where(emask, logits, MASK_VALUE)
    ex = jnp.where(emask, jnp.exp(el - jnp.max(el, axis=-1, keepdims=True)), 0.0)
    prob = jnp.where(emask, ex / jnp.sum(ex, axis=-1, keepdims=True), -1.0)
    p1 = jnp.max(prob, axis=-1, keepdims=True)
    i1 = _lane_min_index(prob == p1, lane)
    rest = jnp.where(lane == i1, -1.0, prob)
    p2 = jnp.max(rest, axis=-1, keepdims=True)
    i2 = _lane_min_index(rest == p2, lane)
    w1 = p_grp * p1 / (p1 + p2)
    w2 = p_grp * p2 / (p1 + p2)
    e1 = i1 - N_GROUPS
    e2 = i2 - N_GROUPS

    hot1 = lane == e1
    hot2 = lane == e2
    hot = jnp.where(hot1 | hot2, 1.0, 0.0)
    before = _dot(tri_ref[...], hot.astype(_BF16)) + running_ref[...]
    rank1 = jnp.sum(jnp.where(hot1, before, 0.0), axis=-1, keepdims=True)
    rank2 = jnp.sum(jnp.where(hot2, before, 0.0), axis=-1, keepdims=True)
    running_ref[...] += jnp.sum(hot, axis=0, keepdims=True)
    count_ref[...] = jnp.broadcast_to(running_ref[...], count_ref.shape)

    fields = (e1, e2, w1, w2, rank1, rank2)
    route = jnp.zeros(logits.shape, _F32)
    for pos, val in enumerate(fields):
        route = jnp.where(lane == pos, val, route)
    route_ref[...] = route


ROUTE_E, ROUTE_W, ROUTE_RANK = 0, 2, 4
DEST_ROWS = 1024


def _finish(x, conv_out, o, ga, gb, wa_b, wo_b, ng, wr_b, br, tm):
    n = x.shape[0]
    tri = jnp.asarray(np.tril(np.ones((tm, tm), np.float32), -1), _BF16)
    row = lambda w: pl.BlockSpec((tm, w), lambda i: (i, 0))
    full = _resident
    return pl.pallas_call(
        _finish_kernel,
        grid=(n // tm,),
        in_specs=[row(D_MODEL), row(D_MODEL), row(Q_DIM), row(D_MODEL), row(D_MODEL),
                  full(wa_b), full(wo_b), full(ng), full(wr_b), full(br), full(tri)],
        out_specs=[row(D_MODEL), row(HALF), row(LANES), pl.BlockSpec((SUBLANES, LANES), lambda i: (0, 0))],
        out_shape=[jax.ShapeDtypeStruct((n, D_MODEL), _F32),
                   jax.ShapeDtypeStruct((n, HALF), jnp.uint32),
                   jax.ShapeDtypeStruct((n, LANES), _F32),
                   jax.ShapeDtypeStruct((SUBLANES, LANES), _F32)],
        scratch_shapes=[pltpu.VMEM((1, LANES), _F32)],
        compiler_params=pltpu.CompilerParams(dimension_semantics=("arbitrary",),
                                             vmem_limit_bytes=VMEM_LIMIT),
        name="finish",
    )(x, conv_out, o, ga, gb, wa_b, wo_b, ng, wr_b, br, tri)


def _dest_kernel(route_ref, starts_ref, dest_ref):
    route = route_ref[...]
    lane = lax.broadcasted_iota(jnp.int32, route.shape, 1)
    out = jnp.zeros(route.shape, jnp.int32)
    for j in range(TOP_K):
        e = route[:, ROUTE_E + j:ROUTE_E + j + 1].astype(jnp.int32)
        start = jnp.sum(jnp.where(lane == e, starts_ref[...], 0.0), axis=-1, keepdims=True)
        d = (start + route[:, ROUTE_RANK + j:ROUTE_RANK + j + 1]).astype(jnp.int32)
        out = jnp.where(lane == j, d, out)
    dest_ref[...] = out


def _dest(route, starts_row, tm):
    n = route.shape[0]
    row = pl.BlockSpec((tm, LANES), lambda i: (i, 0))
    return pl.pallas_call(
        _dest_kernel,
        grid=(n // tm,),
        in_specs=[row, pl.BlockSpec((1, LANES), lambda i: (0, 0))],
        out_specs=row,
        out_shape=jax.ShapeDtypeStruct((n, LANES), jnp.int32),
        name="dest",
    )(route, starts_row)


SC_CORES = 2
SC_SUBCORES = 16
SC_WORKERS = SC_CORES * SC_SUBCORES
SC_IN_FLIGHT = 4
SC_CHUNK_BYTES = 64 * 1024


def _sc_move_rows(src, idx, gather):
    n, d = src.shape
    b = idx.shape[0]
    per_worker = b // SC_WORKERS
    assert per_worker * SC_WORKERS == b and (gather or n % per_worker == 0), (b, n)
    chunk = min(per_worker // SC_IN_FLIGHT, SC_CHUNK_BYTES // (d * 4))
    n_iters = per_worker // (chunk * SC_IN_FLIGHT)
    assert n_iters * chunk * SC_IN_FLIGHT == per_worker and chunk % SUBLANES == 0, (per_worker, chunk)
    mesh = plsc.VectorSubcoreMesh(core_axis_name="c", subcore_axis_name="s",
                                  num_cores=SC_CORES, num_subcores=SC_SUBCORES)
    scratch = ([pltpu.VMEM((chunk,), jnp.int32)] * SC_IN_FLIGHT + [pltpu.VMEM((chunk, d), src.dtype)] * SC_IN_FLIGHT
               + [pltpu.SemaphoreType.DMA] * SC_IN_FLIGHT)

    @functools.partial(pl.kernel, mesh=mesh, out_type=jax.ShapeDtypeStruct((b, d), src.dtype),
                       scratch_types=scratch, name="sc_gather_rows" if gather else "sc_scatter_rows",
                       cost_estimate=pl.CostEstimate(flops=0, transcendentals=0, bytes_accessed=2 * b * d * 4 + b * 4))
    def move(src_hbm, idx_hbm, out_hbm, *bufs):
        idx_v = bufs[:SC_IN_FLIGHT]
        rows_v = bufs[SC_IN_FLIGHT:2 * SC_IN_FLIGHT]
        sems = bufs[2 * SC_IN_FLIGHT:]
        worker = lax.axis_index("s") * SC_CORES + lax.axis_index("c")

        @pl.loop(0, n_iters)
        def _(it):
            bases = [pl.multiple_of(worker * per_worker + (it * SC_IN_FLIGHT + j) * chunk, chunk)
                     for j in range(SC_IN_FLIGHT)]
            loads = [pltpu.async_copy(idx_hbm.at[pl.ds(bases[j], chunk)], idx_v[j], sems[j])
                     for j in range(SC_IN_FLIGHT)]
            reads = []
            for j in range(SC_IN_FLIGHT):
                loads[j].wait()
                if gather:
                    rows = src_hbm.at[idx_v[j]]
                else:
                    rows = src_hbm.at[pl.ds(pl.multiple_of(lax.rem(bases[j], n), chunk), chunk)]
                reads.append(pltpu.async_copy(rows, rows_v[j], sems[j]))
            writes = []
            for j in range(SC_IN_FLIGHT):
                reads[j].wait()
                dst = out_hbm.at[pl.ds(bases[j], chunk)] if gather else out_hbm.at[idx_v[j]]
                writes.append(pltpu.async_copy(rows_v[j], dst, sems[j]))
            for w in writes:
                w.wait()

    return move(src, idx)


def _sc_gather_rows(table, idx):
    return _sc_move_rows(table, idx, gather=True)


def _sc_scatter_rows(src, idx):
    return _sc_move_rows(src, idx, gather=False)


def _expert_kernel(blk_ref, exp_ref, lo_ref, hi_ref, x_ref, wg_ref, wu_ref, wd_ref, yb_ref,
                   wg_b, wu_b, wd_b, held_ref):
    del blk_ref
    k = pl.program_id(0)
    lo, hi, e = lo_ref[k], hi_ref[k], exp_ref[k]

    @pl.when(k == 0)
    def _():
        held_ref[0] = -1

    @pl.when(hi > lo)
    def _():
        @pl.when(held_ref[0] != e)
        def _():
            wg_b[...] = wg_ref[0].astype(_BF16)
            wu_b[...] = wu_ref[0].astype(_BF16)
            wd_b[...] = wd_ref[0].astype(_BF16)
            held_ref[0] = e

        xb = _unpack_bf16_pairs(x_ref[...]).astype(_BF16)
        g = _dot(xb, wg_b[...])
        u = _dot(xb, wu_b[...])
        hid = g * jax.nn.sigmoid(g) * u
        y = _dot(hid.astype(_BF16), wd_b[...])
        r = lax.broadcasted_iota(jnp.int32, yb_ref.shape, 0)
        pltpu.store(yb_ref, _pack_bf16_pairs(y), mask=(r >= lo) & (r < hi))


def _experts(items, xs, w_gate, w_up, w_down):
    n_items = items[0].shape[0]
    wspec = lambda a: pl.BlockSpec((1,) + a.shape[1:], lambda k, blk, exp, lo, hi: (exp[k], 0, 0))
    rows = pl.BlockSpec((MOE_ROWS, HALF), lambda k, blk, exp, lo, hi: (blk[k], 0))
    grid_spec = pltpu.PrefetchScalarGridSpec(
        num_scalar_prefetch=4,
        grid=(n_items,),
        in_specs=[rows, wspec(w_gate), wspec(w_up), wspec(w_down)],
        out_specs=rows,
        scratch_shapes=[pltpu.VMEM(w_gate.shape[1:], _BF16), pltpu.VMEM(w_up.shape[1:], _BF16),
                        pltpu.VMEM(w_down.shape[1:], _BF16), pltpu.SMEM((1,), jnp.int32)],
    )
    return pl.pallas_call(
        _expert_kernel,
        grid_spec=grid_spec,
        out_shape=jax.ShapeDtypeStruct(xs.shape, xs.dtype),
        compiler_params=pltpu.CompilerParams(dimension_semantics=("arbitrary",),
                                             vmem_limit_bytes=VMEM_LIMIT),
        cost_estimate=pl.CostEstimate(
            flops=n_items * MOE_ROWS * 6 * D_MODEL * D_EXPERT, transcendentals=n_items * MOE_ROWS * D_EXPERT,
            bytes_accessed=2 * xs.size * 4 + (w_gate.size + w_up.size + w_down.size) * 4),
        name="experts",
    )(*items, xs, w_gate, w_up, w_down)


def _combine_kernel(h_ref, route_ref, g0_ref, g1_ref, y_ref):
    route = route_ref[...]
    y_ref[...] = (h_ref[...] + route[:, ROUTE_W:ROUTE_W + 1] * _unpack_bf16_pairs(g0_ref[...])
                  + route[:, ROUTE_W + 1:ROUTE_W + 2] * _unpack_bf16_pairs(g1_ref[...]))


def _combine(h, route, g, tm):
    n = h.shape[0]
    nt = n // tm
    row = lambda w: pl.BlockSpec((tm, w), lambda i: (i, 0))
    return pl.pallas_call(
        _combine_kernel,
        grid=(nt,),
        in_specs=[row(D_MODEL), row(LANES), row(HALF), pl.BlockSpec((tm, HALF), lambda i: (nt + i, 0))],
        out_specs=row(D_MODEL),
        out_shape=jax.ShapeDtypeStruct((n, D_MODEL), _F32),
        compiler_params=pltpu.CompilerParams(dimension_semantics=("arbitrary",),
                                             vmem_limit_bytes=VMEM_LIMIT),
        name="combine",
    )(h, route, g, g)


def _work_items(counts, n_pairs):
    n_blocks = n_pairs // MOE_ROWS
    starts = jnp.cumsum(counts) - counts
    cuts = jnp.sort(jnp.concatenate([jnp.arange(n_blocks, dtype=jnp.int32) * MOE_ROWS, starts]))
    ends = jnp.concatenate([cuts[1:], jnp.full((1,), n_pairs, jnp.int32)])
    blk = jnp.minimum(cuts // MOE_ROWS, n_blocks - 1)
    expert = jnp.clip(jnp.sum(starts[None, :] <= cuts[:, None], axis=1) - 1, 0, N_EXPERTS - 1).astype(jnp.int32)
    return starts, (blk, expert, cuts - blk * MOE_ROWS, ends - blk * MOE_ROWS)


def _dispatch(hn, route, counts_rows):
    n = hn.shape[0]
    n_pairs = n * TOP_K
    counts = counts_rows[0, :N_EXPERTS].astype(jnp.int32)
    starts, items = _work_items(counts, n_pairs)
    starts_row = jnp.zeros((1, LANES), _F32).at[0, :N_EXPERTS].set(starts.astype(_F32))
    dest = _dest(route, starts_row, min(n, DEST_ROWS))[:, :TOP_K].T.reshape(n_pairs)
    return _sc_scatter_rows(hn, dest), dest, items


def _after(value, *earlier):
    return lax.optimization_barrier((value,) + earlier)[0]


def kernel(x_prompt, x_sample, state_conv, cache_k, cache_v, norm_attn_g, w_in, q_norm_g, k_norm_g, rel_bias, attn_sinks, w_dw, b_dw, conv_ln_g, conv_ln_b, w_conv_out, b_conv_out, w_attn_out, w_out, norm_ffn_g, w_grp, b_grp, w_router, b_router, w_gate, w_up, w_down):
    bsz, t, _ = x_prompt.shape
    nseq, steps, _ = x_sample.shape
    row = lambda a: a.reshape(1, -1).astype(_F32)

    w_in_b = w_in.astype(_BF16)
    wco_b = w_conv_out.astype(_BF16)
    wa_b = w_attn_out.astype(_BF16)
    wo_b = w_out.astype(_BF16)
    qg = row(jnp.tile(q_norm_g, N_HEADS)) * (HEAD_DIM ** -0.5)
    kg = row(jnp.tile(k_norm_g, N_KV_HEADS))
    w_rt = jnp.zeros((D_MODEL, LANES), _F32).at[:, :N_GROUPS].set(w_grp).at[:, N_GROUPS:N_GROUPS + N_EXPERTS].set(w_router)
    wr_b = w_rt.astype(_BF16)
    b_rt = jnp.zeros((1, LANES), _F32).at[0, :N_GROUPS].set(b_grp).at[0, N_GROUPS:N_GROUPS + N_EXPERTS].set(b_router)
    tbl = _bias_tables(rel_bias)
    conv_params = (w_dw, row(b_dw), row(conv_ln_g), row(conv_ln_b), wco_b, row(b_conv_out))

    def finish(x2d, conv_out, o, ga, gb):
        return _finish(x2d, conv_out, o, ga, gb, wa_b, wo_b, row(norm_ffn_g), wr_b, b_rt, ROW_TILE)

    xp = x_prompt.reshape(bsz * t, D_MODEL)
    q, k, v, ga, gb, conv_out, glu_tail = _inproj_conv(x_prompt, row(norm_attn_g), w_in_b, qg, kg, *conv_params,
                                                       ROW_TILE // 2)
    o = _attn_prompt(q, k, v, tbl, attn_sinks, bsz, t)
    h_p, hn_p, route_p, counts_p = finish(xp, conv_out, o, ga, gb)
    rows_p, dest_p, items_p = _dispatch(hn_p, route_p, counts_p)
    state_conv_prompt = glu_tail[:, HALO - (CONV_WIDTH - 1):]
    tail = lambda a: a.reshape(bsz, t, KV_DIM)[:, t - WINDOW:].reshape(bsz, WINDOW, N_KV_HEADS, HEAD_DIM)
    cache_k_prompt, cache_v_prompt = tail(k), tail(v)

    xs = _after(x_sample, dest_p).reshape(nseq * steps, D_MODEL)
    glu, q, k, v, ga, gb = _inproj(xs, row(norm_attn_g), w_in_b, qg, kg, _F32, ROW_TILE)
    glu_t = glu.reshape(nseq, steps, D_CONV).transpose(1, 0, 2)
    conv_out, state_t = _conv_sample(state_conv.transpose(1, 0, 2), glu_t, *conv_params, SAMPLE_CONV_SEQS)
    conv_out = conv_out.transpose(1, 0, 2).reshape(nseq * steps, D_MODEL)
    k3 = k.reshape(nseq, steps, KV_DIM)
    v3 = v.reshape(nseq, steps, KV_DIM)
    o, ck_t, cv_t = _attn_sample(q.reshape(nseq, steps, Q_DIM), k3, v3, cache_k.transpose(0, 2, 3, 1),
                                 cache_v.transpose(0, 2, 3, 1), tbl, attn_sinks, SAMPLE_ATTN_SEQS)
    h_s, hn_s, route_s, counts_s = finish(xs, conv_out, o.reshape(nseq * steps, Q_DIM), ga, gb)
    rows_s, dest_s, items_s = _dispatch(hn_s, route_s, counts_s)
    state_conv_sample = state_t.transpose(1, 0, 2)
    cache_k_sample = ck_t.transpose(0, 3, 1, 2)
    cache_v_sample = cv_t.transpose(0, 3, 1, 2)

    yb_p = _experts(items_p, rows_p, w_gate, w_up, w_down)
    g_p = _sc_gather_rows(yb_p, dest_p)
    yb_s = _experts(items_s, _after(rows_s, yb_p), w_gate, w_up, w_down)
    g_s = _sc_gather_rows(yb_s, dest_s)
    y_prompt = _combine(h_p, route_p, g_p, ROW_TILE).reshape(bsz, t, D_MODEL)
    y_sample = _combine(h_s, route_s, _after(g_s, y_prompt), ROW_TILE).reshape(nseq, steps, D_MODEL)

    return (y_prompt, y_sample, state_conv_prompt, cache_k_prompt, cache_v_prompt,
            state_conv_sample, cache_k_sample, cache_v_sample)
```

```python
import functools
import math

import numpy as np
import jax
import jax.numpy as jnp
from jax import lax
from jax.experimental import pallas as pl
from jax.experimental.pallas import tpu as pltpu
from jax.experimental.pallas import tpu_sc as plsc

D_MODEL = 1024
N_HEADS = 16
HEAD_DIM = 64
N_KV_HEADS = 4
WINDOW = 128
Q_DIM = N_HEADS * HEAD_DIM
KV_DIM = N_KV_HEADS * HEAD_DIM
N_BUCKETS = 32
MAX_EXACT = N_BUCKETS // 2
MAX_DISTANCE = 128
D_CONV = D_MODEL
CONV_WIDTH = 31
N_GROUPS = 4
EXPERTS_PER_GROUP = 8
N_EXPERTS = N_GROUPS * EXPERTS_PER_GROUP
TOP_K = 2
D_EXPERT = 256
EPS = 1e-6

LANES = 128
SUBLANES = 8
N_PAIRS = N_HEADS // 2
MOE_ROWS = 512
MASK_VALUE = -1e30
VMEM_LIMIT = 56 * 1024 * 1024
ROW_TILE = 512
SAMPLE_CONV_SEQS = 64
SAMPLE_ATTN_SEQS = 8

_F32 = jnp.float32
_BF16 = jnp.bfloat16


def _resident(a):
    return pl.BlockSpec(a.shape, lambda *_: (0,) * a.ndim, pipeline_mode=pl.Buffered(1))


def _dot(a, b):
    return jnp.dot(a, b, preferred_element_type=_F32)


HALF = D_MODEL // 2


def _pack_pair(lo, hi):
    lo_bits = pltpu.bitcast(lo.astype(_BF16).astype(_F32), jnp.uint32)
    hi_bits = pltpu.bitcast(hi.astype(_BF16).astype(_F32), jnp.uint32)
    return hi_bits | (lo_bits >> 16)


def _pack_bf16_pairs(x):
    return _pack_pair(x[:, :HALF], x[:, HALF:])


def _unpack_bf16_pairs(w):
    lo = pltpu.bitcast(w << 16, _F32)
    hi = pltpu.bitcast(w & jnp.uint32(0xFFFF0000), _F32)
    return jnp.concatenate([lo, hi], axis=1)


def _head_rms_scale(z):
    low = lax.broadcasted_iota(jnp.int32, (z.shape[0], LANES), 1) < HEAD_DIM
    slabs = []
    for c in range(z.shape[1] // LANES):
        sq = z[:, c * LANES:(c + 1) * LANES]
        sq = sq * sq
        first = jnp.sum(jnp.where(low, sq, 0.0), axis=-1, keepdims=True)
        second = jnp.sum(jnp.where(low, 0.0, sq), axis=-1, keepdims=True)
        slabs.append(lax.rsqrt(jnp.where(low, first, second) * (1.0 / HEAD_DIM) + EPS))
    return jnp.concatenate(slabs, axis=1)


def _inproj_kernel(x_ref, g_ref, w_ref, qg_ref, kg_ref,
                   glu_ref, q_ref, k_ref, v_ref, ga_ref, gb_ref):
    x = x_ref[...]
    xn = x * lax.rsqrt(jnp.mean(x * x, axis=-1, keepdims=True) + EPS) * g_ref[...]
    xb = xn.astype(_BF16)

    def seg(lo, width):
        return _dot(xb, w_ref[:, lo:lo + width])

    a = seg(0, D_CONV)
    b = seg(D_CONV, D_CONV)
    glu_ref[...] = a * jax.nn.sigmoid(b)
    off = 2 * D_CONV
    q = seg(off, Q_DIM)
    q_ref[...] = (q * _head_rms_scale(q) * qg_ref[...]).astype(q_ref.dtype)
    off += Q_DIM
    k = seg(off, KV_DIM)
    k_ref[...] = k * _head_rms_scale(k) * kg_ref[...]
    off += KV_DIM
    v_ref[...] = seg(off, KV_DIM)
    off += KV_DIM
    ga_ref[...] = jax.nn.sigmoid(seg(off, D_MODEL)).astype(ga_ref.dtype)
    off += D_MODEL
    gb_ref[...] = jax.nn.sigmoid(seg(off, D_MODEL)).astype(gb_ref.dtype)


def _inproj(x, g, w_in_b, qg, kg, q_dtype, tm):
    n = x.shape[0]
    row = lambda w: pl.BlockSpec((tm, w), lambda i: (i, 0))
    full = _resident
    return pl.pallas_call(
        _inproj_kernel,
        grid=(n // tm,),
        in_specs=[row(D_MODEL), full(g), full(w_in_b), full(qg), full(kg)],
        out_specs=[row(D_CONV), row(Q_DIM), row(KV_DIM), row(KV_DIM), row(D_MODEL), row(D_MODEL)],
        out_shape=[jax.ShapeDtypeStruct((n, D_CONV), _F32),
                   jax.ShapeDtypeStruct((n, Q_DIM), q_dtype),
                   jax.ShapeDtypeStruct((n, KV_DIM), _F32),
                   jax.ShapeDtypeStruct((n, KV_DIM), _F32),
                   jax.ShapeDtypeStruct((n, D_MODEL), _BF16),
                   jax.ShapeDtypeStruct((n, D_MODEL), _BF16)],
        compiler_params=pltpu.CompilerParams(dimension_semantics=("arbitrary",),
                                             vmem_limit_bytes=VMEM_LIMIT),
        name="inproj",
    )(x, g, w_in_b, qg, kg)


def _ln_swish_project(y, lng_ref, lnb_ref, wo_ref, bo_ref):
    mu = jnp.mean(y, axis=-1, keepdims=True)
    yc = y - mu
    var = jnp.mean(yc * yc, axis=-1, keepdims=True)
    z = yc * lax.rsqrt(var + EPS) * lng_ref[...] + lnb_ref[...]
    z = z * jax.nn.sigmoid(z)
    return (_dot(z.astype(_BF16), wo_ref[...]) + bo_ref[...]).astype(_BF16)


HALO = 32
CONV_STEPS = 16
CH_TILES = D_CONV // LANES


PIECE = 256
PAIR_TILES = 2 * CH_TILES


def _inproj_conv_kernel(x_ref, g_ref, w_ref, qg_ref, kg_ref, w16_ref, b16_ref, lng_ref, lnb_ref, wo_ref, bo_ref,
                        q_ref, k_ref, v_ref, ga_ref, gb_ref, conv_ref, tail_ref, hist_ref, y_ref, *, tm):
    i = pl.program_id(1)
    rows = 2 * tm

    @pl.when(i == 0)
    def _():
        hist_ref[0:HALO * CH_TILES, :] = jnp.zeros((HALO * CH_TILES, LANES), jnp.uint32)

    @pl.when(i > 0)
    def _():
        hist_ref[0:HALO * CH_TILES, :] = hist_ref[tm * CH_TILES:(tm + HALO) * CH_TILES, :]

    x = jnp.concatenate([x_ref[0], x_ref[1]], axis=0)
    xn = x * lax.rsqrt(jnp.mean(x * x, axis=-1, keepdims=True) + EPS) * g_ref[...]
    xb = xn.astype(_BF16)

    def seg(lo, width):
        return _dot(xb, w_ref[:, lo:lo + width])

    def put(ref, lo, val):
        ref[0, :, lo:lo + PIECE] = val[:tm].astype(ref.dtype)
        ref[1, :, lo:lo + PIECE] = val[tm:].astype(ref.dtype)

    for p in range(D_CONV // PIECE):
        lo = p * PIECE
        glu = seg(lo, PIECE) * jax.nn.sigmoid(seg(D_CONV + lo, PIECE))
        tail_ref[0, :, lo:lo + PIECE] = glu[tm - HALO:tm, :]
        tail_ref[1, :, lo:lo + PIECE] = glu[rows - HALO:, :]
        words = _pack_pair(glu[:tm], glu[tm:])
        for c in range(PIECE // LANES):
            tile = lo // LANES + c
            hist_ref[pl.ds(HALO * CH_TILES + tile, tm, stride=CH_TILES), :] = words[:, c * LANES:(c + 1) * LANES]

    first = HALO - (CONV_WIDTH - 1)

    def conv_chunk(ci, carry):
        t0 = ci * CONV_STEPS
        acc = jnp.zeros((CONV_STEPS, PAIR_TILES, LANES), _F32)
        for j in range(CONV_WIDTH):
            lo = pl.multiple_of((t0 + first + j) * CH_TILES, CH_TILES)
            xw = pltpu.bitcast(hist_ref[pl.ds(lo, CONV_STEPS * CH_TILES), :], _BF16)
            acc = acc + (xw.reshape(CONV_STEPS, PAIR_TILES, LANES).astype(_F32)
                         * w16_ref[j][None].astype(_F32))
        y_ref[pl.ds(pl.multiple_of(t0 * PAIR_TILES, PAIR_TILES), CONV_STEPS * PAIR_TILES), :] = (
            (acc + b16_ref[...][None]).reshape(CONV_STEPS * PAIR_TILES, LANES))
        return carry

    def q_piece(lo):
        q = seg(2 * D_CONV + lo, PIECE)
        put(q_ref, lo, q * _head_rms_scale(q) * qg_ref[:, lo:lo + PIECE])

    def k_piece(lo):
        k = seg(2 * D_CONV + Q_DIM + lo, PIECE)
        put(k_ref, lo, k * _head_rms_scale(k) * kg_ref[:, lo:lo + PIECE])

    def v_piece(lo):
        put(v_ref, lo, seg(2 * D_CONV + Q_DIM + KV_DIM + lo, PIECE))

    def gate_piece(ref, base, lo):
        put(ref, lo, jax.nn.sigmoid(seg(base + lo, PIECE)))

    gate_base = 2 * D_CONV + Q_DIM + 2 * KV_DIM
    pieces = ([functools.partial(q_piece, lo) for lo in range(0, Q_DIM, PIECE)]
              + [functools.partial(k_piece, lo) for lo in range(0, KV_DIM, PIECE)]
              + [functools.partial(v_piece, lo) for lo in range(0, KV_DIM, PIECE)]
              + [functools.partial(gate_piece, ga_ref, gate_base, lo) for lo in range(0, D_MODEL, PIECE)]
              + [functools.partial(gate_piece, gb_ref, gate_base + D_MODEL, lo) for lo in range(0, D_MODEL, PIECE)])

    for piece in pieces:
        piece()
    lax.fori_loop(0, tm // CONV_STEPS, conv_chunk, 0)

    y = jnp.concatenate(
        [jnp.concatenate([y_ref[pl.ds(2 * c + s, tm, stride=PAIR_TILES), :] for c in range(CH_TILES)], axis=1)
         for s in range(2)], axis=0)
    out = _ln_swish_project(y, lng_ref, lnb_ref, wo_ref, bo_ref)
    conv_ref[0] = out[:tm]
    conv_ref[1] = out[tm:]


def _inproj_conv(x, g, w_in_b, qg, kg, w_dw, b_dw, lng, lnb, wo_b, bo, tm):
    bsz, t, _ = x.shape
    assert CH_TILES == SUBLANES and bsz % 2 == 0 and t % tm == 0
    nt = t // tm
    w16 = jnp.repeat(w_dw.reshape(CONV_WIDTH, CH_TILES, LANES), 2, axis=1).astype(_BF16)
    b16 = jnp.repeat(b_dw.reshape(CH_TILES, LANES), 2, axis=0)
    full = _resident
    blk = lambda w: pl.BlockSpec((2, tm, w), lambda p, i: (p, i, 0))
    shape = lambda w, dt: jax.ShapeDtypeStruct((bsz, t, w), dt)
    outs = pl.pallas_call(
        functools.partial(_inproj_conv_kernel, tm=tm),
        grid=(bsz // 2, nt),
        in_specs=[blk(D_MODEL), full(g), full(w_in_b), full(qg), full(kg), full(w16), full(b16), full(lng), full(lnb),
                  full(wo_b), full(bo)],
        out_specs=[blk(Q_DIM), blk(KV_DIM), blk(KV_DIM), blk(D_MODEL), blk(D_MODEL), blk(D_MODEL),
                   pl.BlockSpec((2, HALO, D_CONV), lambda p, i: (p, 0, 0))],
        out_shape=[shape(Q_DIM, _BF16), shape(KV_DIM, _F32), shape(KV_DIM, _F32), shape(D_MODEL, _BF16),
                   shape(D_MODEL, _BF16), shape(D_MODEL, _BF16), jax.ShapeDtypeStruct((bsz, HALO, D_CONV), _F32)],
        scratch_shapes=[pltpu.VMEM(((tm + HALO) * CH_TILES, LANES), jnp.uint32),
                        pltpu.VMEM((tm * PAIR_TILES, LANES), _F32)],
        compiler_params=pltpu.CompilerParams(dimension_semantics=("arbitrary", "arbitrary"),
                                             vmem_limit_bytes=VMEM_LIMIT),
        name="inproj_conv",
    )(x, g, w_in_b, qg, kg, w16, b16, lng, lnb, wo_b, bo)
    return [o.reshape(bsz * t, o.shape[-1]) for o in outs[:-1]] + [outs[-1]]


def _conv_sample_kernel(state_ref, glu_ref, wdw_ref, bdw_ref, lng_ref, lnb_ref, wo_ref, bo_ref,
                        out_ref, state_out_ref):
    keep, steps = state_ref.shape[0], glu_ref.shape[0]

    def hist(u):
        return state_ref[u] if u < keep else glu_ref[u - keep]

    for t in range(steps):
        acc = hist(t) * wdw_ref[0:1, :]
        for j in range(1, CONV_WIDTH):
            acc = acc + hist(t + j) * wdw_ref[j:j + 1, :]
        out_ref[t] = _ln_swish_project(acc + bdw_ref[...], lng_ref, lnb_ref, wo_ref, bo_ref)
    state_out_ref[0:keep - steps] = state_ref[steps:keep]
    state_out_ref[keep - steps:keep] = glu_ref[...]


def _conv_sample(state_t, glu_t, w_dw, b_dw, lng, lnb, wo_b, bo, sb):
    keep, nseq, _ = state_t.shape
    steps = glu_t.shape[0]
    full = _resident
    blk = lambda r: pl.BlockSpec((r, sb, D_CONV), lambda i: (0, i, 0))
    return pl.pallas_call(
        _conv_sample_kernel,
        grid=(nseq // sb,),
        in_specs=[blk(keep), blk(steps), full(w_dw), full(b_dw), full(lng), full(lnb), full(wo_b), full(bo)],
        out_specs=[blk(steps), blk(keep)],
        out_shape=[jax.ShapeDtypeStruct((steps, nseq, D_MODEL), _BF16),
                   jax.ShapeDtypeStruct(state_t.shape, _F32)],
        compiler_params=pltpu.CompilerParams(dimension_semantics=("arbitrary",),
                                             vmem_limit_bytes=VMEM_LIMIT),
        name="conv_sample",
    )(state_t, glu_t, w_dw, b_dw, lng, lnb, wo_b, bo)


def _bucket_map():
    i = np.arange(WINDOW)[:, None]
    j = np.arange(WINDOW)[None, :]
    n = (i - j) % WINDOW
    nf = np.maximum(n, 1).astype(np.float32)
    large = MAX_EXACT + (np.log(nf / np.float32(MAX_EXACT)) / np.float32(math.log(MAX_DISTANCE / MAX_EXACT))
                         * np.float32(N_BUCKETS - MAX_EXACT)).astype(np.int32)
    return np.where(n < MAX_EXACT, n, np.minimum(large, N_BUCKETS - 1)).astype(np.int32)


def _bias_table_kernel(rb_ref, bm_ref, tbl_ref):
    p = pl.program_id(0)
    bm = bm_ref[...]
    for half in range(2):
        h = 2 * p + half
        t = jnp.zeros(bm.shape, _F32)
        for b in range(N_BUCKETS):
            t = jnp.where(bm == b, rb_ref[b, h], t)
        tbl_ref[0, :, half * WINDOW:(half + 1) * WINDOW] = t


def _bias_tables(rel_bias):
    bm = jnp.asarray(_bucket_map())
    return pl.pallas_call(
        _bias_table_kernel,
        grid=(N_PAIRS,),
        in_specs=[pl.BlockSpec(memory_space=pltpu.SMEM), pl.BlockSpec(bm.shape, lambda p: (0, 0))],
        out_specs=pl.BlockSpec((1, WINDOW, 2 * WINDOW), lambda p: (p, 0, 0)),
        out_shape=jax.ShapeDtypeStruct((N_PAIRS, WINDOW, 2 * WINDOW), _F32),
        name="bias_tables",
    )(rel_bias, bm)


def _block_diag_pairs(slab):
    low = lax.broadcasted_iota(jnp.int32, slab.shape, 1) < HEAD_DIM
    swapped = pltpu.roll(slab, HEAD_DIM, axis=1)
    zero = jnp.zeros_like(slab)
    first = jnp.concatenate([jnp.where(low, slab, zero), jnp.where(low, zero, swapped)], axis=0)
    second = jnp.concatenate([jnp.where(low, swapped, zero), jnp.where(low, zero, slab)], axis=0)
    return first.astype(_BF16), second.astype(_BF16)


def _kv_operands(k_blk, v_blk):
    ops = []
    for slab in range(KV_DIM // LANES):
        cols = slice(slab * LANES, (slab + 1) * LANES)
        ops.extend(zip(_block_diag_pairs(k_blk[:, cols]), _block_diag_pairs(v_blk[:, cols])))
    return ops


def _attend(q, prev_ops, own_ops, tbl_ref, sink_ref, prev_shift, store, transposed=False):
    tq = q.shape[0]
    rows = 2 * tq
    row = lax.broadcasted_iota(jnp.int32, (rows, 2 * WINDOW), 0)
    col = lax.broadcasted_iota(jnp.int32, (rows, 2 * WINDOW), 1)
    from_prev = (col & (WINDOW - 1)) > jnp.where(row >= tq, row - tq, row)
    top = lax.broadcasted_iota(jnp.int32, (rows, 1), 0) < tq
    low = lax.broadcasted_iota(jnp.int32, (rows, LANES), 1) < HEAD_DIM
    contract_last = (((1,), (1,)), ((), ()))

    def logits(a, k_op):
        return _dot(a, k_op) if transposed else lax.dot_general(a, k_op, contract_last, preferred_element_type=_F32)

    def weighted_values(pr, v_op):
        return lax.dot_general(pr, v_op, contract_last, preferred_element_type=_F32) if transposed else _dot(pr, v_op)

    for kvh in range(N_KV_HEADS):
        (k_prev, v_prev), (k_own, v_own) = prev_ops[kvh], own_ops[kvh]
        pair_a = 2 * kvh
        pair_b = pair_a + 1
        qq = jnp.concatenate([q[:, pair_a * LANES:(pair_a + 1) * LANES],
                              q[:, pair_b * LANES:(pair_b + 1) * LANES]], axis=0).astype(_BF16)
        sp = logits(qq, k_prev)
        so = logits(qq, k_own)
        bias = jnp.concatenate([tbl_ref[pair_a, 0:tq, :], tbl_ref[pair_b, 0:tq, :]], axis=0)
        s = jnp.where(from_prev, sp + prev_shift, so) + bias
        sink_even = jnp.where(top, sink_ref[2 * pair_a], sink_ref[2 * pair_b])
        sink_odd = jnp.where(top, sink_ref[2 * pair_a + 1], sink_ref[2 * pair_b + 1])
        m_even = jnp.maximum(jnp.max(s[:, :WINDOW], axis=-1, keepdims=True), sink_even)
        m_odd = jnp.maximum(jnp.max(s[:, WINDOW:], axis=-1, keepdims=True), sink_odd)
        e_even = jnp.exp(s[:, :WINDOW] - m_even)
        e_odd = jnp.exp(s[:, WINDOW:] - m_odd)
        p = jnp.concatenate([e_even, e_odd], axis=1).astype(_BF16)
        zero = jnp.zeros_like(p)
        o = (weighted_values(jnp.where(from_prev, p, zero), v_prev)
             + weighted_values(jnp.where(from_prev, zero, p), v_own))
        den_even = jnp.sum(e_even, axis=-1, keepdims=True) + jnp.exp(sink_even - m_even)
        den_odd = jnp.sum(e_odd, axis=-1, keepdims=True) + jnp.exp(sink_odd - m_odd)
        o = o * jnp.where(low, 1.0 / den_even, 1.0 / den_odd)
        store(pair_a, o[:tq])
        store(pair_b, o[tq:])


PROMPT_QBLOCKS = 4


def _attn_prompt_kernel(sink_ref, q_ref, kp_ref, ko_ref, vp_ref, vo_ref, tbl_ref, o_ref):
    prev_shift = jnp.where(pl.program_id(1) == 0, MASK_VALUE, 0.0).astype(_F32)
    ops = [_kv_operands(kp_ref[...], vp_ref[...])]
    for b in range(PROMPT_QBLOCKS):
        rows = slice(b * WINDOW, (b + 1) * WINDOW)
        ops.append(_kv_operands(ko_ref[rows, :], vo_ref[rows, :]))

        def store(pair, o, rows=rows):
            o_ref[rows, pair * LANES:(pair + 1) * LANES] = o.astype(o_ref.dtype)

        _attend(q_ref[rows, :], ops[b], ops[b + 1], tbl_ref, sink_ref,
                prev_shift if b == 0 else jnp.float32(0.0), store)


def _attn_prompt(q, k, v, tbl, sinks, bsz, t):
    tq = PROMPT_QBLOCKS * WINDOW
    nb = t // tq
    own = lambda w: pl.BlockSpec((tq, w), lambda b, i: (b * nb + i, 0))
    prev = lambda w: pl.BlockSpec((WINDOW, w),
                                  lambda b, i: (PROMPT_QBLOCKS * (b * nb + i) - jnp.minimum(i, 1), 0))
    return pl.pallas_call(
        _attn_prompt_kernel,
        grid=(bsz, nb),
        in_specs=[pl.BlockSpec(memory_space=pltpu.SMEM), own(Q_DIM), prev(KV_DIM), own(KV_DIM),
                  prev(KV_DIM), own(KV_DIM), pl.BlockSpec(tbl.shape, lambda b, i: (0, 0, 0))],
        out_specs=own(Q_DIM),
        out_shape=jax.ShapeDtypeStruct((bsz * t, Q_DIM), _BF16),
        compiler_params=pltpu.CompilerParams(dimension_semantics=("arbitrary", "arbitrary"),
                                             vmem_limit_bytes=VMEM_LIMIT),
        name="attn_prompt",
    )(sinks, q, k, k, v, v, tbl)


def _block_diag_t(x):
    xb = x.astype(_BF16)
    z = jnp.zeros_like(xb)
    return jnp.concatenate([jnp.concatenate([xb, z], axis=1), jnp.concatenate([z, xb], axis=1)], axis=0)


def _attn_sample_kernel(sink_ref, q_ref, kn_ref, vn_ref, ck_ref, cv_ref, tbl_ref, o_ref, cko_ref, cvo_ref,
                        *, sb, steps):
    pad = jnp.zeros((WINDOW - steps, LANES), _F32)
    lane = lax.broadcasted_iota(jnp.int32, (HEAD_DIM, WINDOW), 1)

    def one_sequence(s):
        def store(pair, o):
            o_ref[s, :, pair * LANES:(pair + 1) * LANES] = o

        prev_ops, own_ops = [], []
        for slab in range(KV_DIM // LANES):
            cols = slice(slab * LANES, (slab + 1) * LANES)
            new_k = jnp.concatenate([kn_ref[s][:, cols], pad], axis=0).T
            new_v = jnp.concatenate([vn_ref[s][:, cols], pad], axis=0).T
            for sub in range(2):
                kvh = 2 * slab + sub
                part = slice(sub * HEAD_DIM, (sub + 1) * HEAD_DIM)
                kt, vt = ck_ref[s, kvh], cv_ref[s, kvh]
                cko_ref[s, kvh] = pltpu.roll(jnp.where(lane < steps, new_k[part], kt), WINDOW - steps, axis=1)
                cvo_ref[s, kvh] = pltpu.roll(jnp.where(lane < steps, new_v[part], vt), WINDOW - steps, axis=1)
                prev_ops.append((_block_diag_t(kt), _block_diag_t(vt)))
                own_ops.append((_block_diag_t(new_k[part]), _block_diag_t(new_v[part])))
        _attend(q_ref[s], prev_ops, own_ops, tbl_ref, sink_ref, jnp.float32(0.0), store, transposed=True)

    for s in range(sb):
        one_sequence(s)


def _attn_sample(q, k_new, v_new, cache_kt, cache_vt, tbl, sinks, sb):
    nseq, steps, _ = q.shape
    seq = lambda r, w: pl.BlockSpec((sb, r, w), lambda i: (i, 0, 0))
    cache = pl.BlockSpec((sb, N_KV_HEADS, HEAD_DIM, WINDOW), lambda i: (i, 0, 0, 0))
    return pl.pallas_call(
        functools.partial(_attn_sample_kernel, sb=sb, steps=steps),
        grid=(nseq // sb,),
        in_specs=[pl.BlockSpec(memory_space=pltpu.SMEM), seq(steps, Q_DIM), seq(steps, KV_DIM), seq(steps, KV_DIM),
                  cache, cache, pl.BlockSpec(tbl.shape, lambda i: (0, 0, 0))],
        out_specs=[seq(steps, Q_DIM), cache, cache],
        out_shape=[jax.ShapeDtypeStruct((nseq, steps, Q_DIM), _F32),
                   jax.ShapeDtypeStruct(cache_kt.shape, _F32), jax.ShapeDtypeStruct(cache_vt.shape, _F32)],
        compiler_params=pltpu.CompilerParams(dimension_semantics=("arbitrary",),
                                             vmem_limit_bytes=VMEM_LIMIT),
        name="attn_sample",
    )(sinks, q, k_new, v_new, cache_kt, cache_vt, tbl)


def _lane_min_index(mask, lane):
    return jnp.min(jnp.where(mask, lane, float(LANES)), axis=-1, keepdims=True)


def _finish_kernel(x_ref, conv_ref, o_ref, ga_ref, gb_ref, wa_ref, wo_ref, ng_ref, wr_ref, br_ref,
                   tri_ref, h_ref, hn_ref, route_ref, count_ref, running_ref):
    @pl.when(pl.program_id(0) == 0)
    def _():
        running_ref[...] = jnp.zeros_like(running_ref)

    attn_out = _dot(o_ref[...].astype(_BF16), wa_ref[...])
    merged = ga_ref[...].astype(_F32) * conv_ref[...].astype(_F32) + gb_ref[...].astype(_F32) * attn_out
    h = x_ref[...] + _dot(merged.astype(_BF16), wo_ref[...])
    h_ref[...] = h
    hn = h * lax.rsqrt(jnp.mean(h * h, axis=-1, keepdims=True) + EPS) * ng_ref[...]
    hn_ref[...] = _pack_bf16_pairs(hn)

    logits = _dot(hn.astype(_BF16), wr_ref[...]) + br_ref[...]
    lane = lax.broadcasted_iota(jnp.int32, logits.shape, 1).astype(_F32)
    gmask = lane < N_GROUPS
    gl = jnp.where(gmask, logits, MASK_VALUE)
    gmax = jnp.max(gl, axis=-1, keepdims=True)
    grp = _lane_min_index(gmask & (gl == gmax), lane)
    p_grp = 1.0 / jnp.sum(jnp.where(gmask, jnp.exp(gl - gmax), 0.0), axis=-1, keepdims=True)
    e_lo = N_GROUPS + grp * EXPERTS_PER_GROUP
    emask = (lane >= e_lo) & (lane < e_lo + EXPERTS_PER_GROUP)
    el = jnp.where(emask, logits, MASK_VALUE)
    ex = jnp.where(emask, jnp.exp(el - jnp.max(el, axis=-1, keepdims=True)), 0.0)
    prob = jnp.where(emask, ex / jnp.sum(ex, axis=-1, keepdims=True), -1.0)
    p1 = jnp.max(prob, axis=-1, keepdims=True)
    i1 = _lane_min_index(prob == p1, lane)
    rest = jnp.where(lane == i1, -1.0, prob)
    p2 = jnp.max(rest, axis=-1, keepdims=True)
    i2 = _lane_min_index(rest == p2, lane)
    w1 = p_grp * p1 / (p1 + p2)
    w2 = p_grp * p2 / (p1 + p2)
    e1 = i1 - N_GROUPS
    e2 = i2 - N_GROUPS

    hot1 = lane == e1
    hot2 = lane == e2
    hot = jnp.where(hot1 | hot2, 1.0, 0.0)
    before = _dot(tri_ref[...], hot.astype(_BF16)) + running_ref[...]
    rank1 = jnp.sum(jnp.where(hot1, before, 0.0), axis=-1, keepdims=True)
    rank2 = jnp.sum(jnp.where(hot2, before, 0.0), axis=-1, keepdims=True)
    running_ref[...] += jnp.sum(hot, axis=0, keepdims=True)
    count_ref[...] = jnp.broadcast_to(running_ref[...], count_ref.shape)

    fields = (e1, e2, w1, w2, rank1, rank2)
    route = jnp.zeros(logits.shape, _F32)
    for pos, val in enumerate(fields):
        route = jnp.where(lane == pos, val, route)
    route_ref[...] = route


ROUTE_E, ROUTE_W, ROUTE_RANK = 0, 2, 4
DEST_ROWS = 1024


def _finish(x, conv_out, o, ga, gb, wa_b, wo_b, ng, wr_b, br, tm):
    n = x.shape[0]
    tri = jnp.asarray(np.tril(np.ones((tm, tm), np.float32), -1), _BF16)
    row = lambda w: pl.BlockSpec((tm, w), lambda i: (i, 0))
    full = _resident
    return pl.pallas_call(
        _finish_kernel,
        grid=(n // tm,),
        in_specs=[row(D_MODEL), row(D_MODEL), row(Q_DIM), row(D_MODEL), row(D_MODEL),
                  full(wa_b), full(wo_b), full(ng), full(wr_b), full(br), full(tri)],
        out_specs=[row(D_MODEL), row(HALF), row(LANES), pl.BlockSpec((SUBLANES, LANES), lambda i: (0, 0))],
        out_shape=[jax.ShapeDtypeStruct((n, D_MODEL), _F32),
                   jax.ShapeDtypeStruct((n, HALF), jnp.uint32),
                   jax.ShapeDtypeStruct((n, LANES), _F32),
                   jax.ShapeDtypeStruct((SUBLANES, LANES), _F32)],
        scratch_shapes=[pltpu.VMEM((1, LANES), _F32)],
        compiler_params=pltpu.CompilerParams(dimension_semantics=("arbitrary",),
                                             vmem_limit_bytes=VMEM_LIMIT),
        name="finish",
    )(x, conv_out, o, ga, gb, wa_b, wo_b, ng, wr_b, br, tri)


def _dest_kernel(route_ref, starts_ref, dest_ref):
    route = route_ref[...]
    lane = lax.broadcasted_iota(jnp.int32, route.shape, 1)
    out = jnp.zeros(route.shape, jnp.int32)
    for j in range(TOP_K):
        e = route[:, ROUTE_E + j:ROUTE_E + j + 1].astype(jnp.int32)
        start = jnp.sum(jnp.where(lane == e, starts_ref[...], 0.0), axis=-1, keepdims=True)
        d = (start + route[:, ROUTE_RANK + j:ROUTE_RANK + j + 1]).astype(jnp.int32)
        out = jnp.where(lane == j, d, out)
    dest_ref[...] = out


def _dest(route, starts_row, tm):
    n = route.shape[0]
    row = pl.BlockSpec((tm, LANES), lambda i: (i, 0))
    return pl.pallas_call(
        _dest_kernel,
        grid=(n // tm,),
        in_specs=[row, pl.BlockSpec((1, LANES), lambda i: (0, 0))],
        out_specs=row,
        out_shape=jax.ShapeDtypeStruct((n, LANES), jnp.int32),
        name="dest",
    )(route, starts_row)


SC_CORES = 2
SC_SUBCORES = 16
SC_WORKERS = SC_CORES * SC_SUBCORES
SC_IN_FLIGHT = 4
SC_CHUNK_BYTES = 64 * 1024


def _sc_move_rows(src, idx, gather):
    n, d = src.shape
    b = idx.shape[0]
    per_worker = b // SC_WORKERS
    assert per_worker * SC_WORKERS == b and (gather or n % per_worker == 0), (b, n)
    chunk = min(per_worker // SC_IN_FLIGHT, SC_CHUNK_BYTES // (d * 4))
    n_iters = per_worker // (chunk * SC_IN_FLIGHT)
    assert n_iters * chunk * SC_IN_FLIGHT == per_worker and chunk % SUBLANES == 0, (per_worker, chunk)
    mesh = plsc.VectorSubcoreMesh(core_axis_name="c", subcore_axis_name="s",
                                  num_cores=SC_CORES, num_subcores=SC_SUBCORES)
    scratch = ([pltpu.VMEM((chunk,), jnp.int32)] * SC_IN_FLIGHT + [pltpu.VMEM((chunk, d), src.dtype)] * SC_IN_FLIGHT
               + [pltpu.SemaphoreType.DMA] * SC_IN_FLIGHT)

    @functools.partial(pl.kernel, mesh=mesh, out_type=jax.ShapeDtypeStruct((b, d), src.dtype),
                       scratch_types=scratch, name="sc_gather_rows" if gather else "sc_scatter_rows",
                       cost_estimate=pl.CostEstimate(flops=0, transcendentals=0, bytes_accessed=2 * b * d * 4 + b * 4))
    def move(src_hbm, idx_hbm, out_hbm, *bufs):
        idx_v = bufs[:SC_IN_FLIGHT]
        rows_v = bufs[SC_IN_FLIGHT:2 * SC_IN_FLIGHT]
        sems = bufs[2 * SC_IN_FLIGHT:]
        worker = lax.axis_index("s") * SC_CORES + lax.axis_index("c")

        @pl.loop(0, n_iters)
        def _(it):
            bases = [pl.multiple_of(worker * per_worker + (it * SC_IN_FLIGHT + j) * chunk, chunk)
                     for j in range(SC_IN_FLIGHT)]
            loads = [pltpu.async_copy(idx_hbm.at[pl.ds(bases[j], chunk)], idx_v[j], sems[j])
                     for j in range(SC_IN_FLIGHT)]
            reads = []
            for j in range(SC_IN_FLIGHT):
                loads[j].wait()
                if gather:
                    rows = src_hbm.at[idx_v[j]]
                else:
                    rows = src_hbm.at[pl.ds(pl.multiple_of(lax.rem(bases[j], n), chunk), chunk)]
                reads.append(pltpu.async_copy(rows, rows_v[j], sems[j]))
            writes = []
            for j in range(SC_IN_FLIGHT):
                reads[j].wait()
                dst = out_hbm.at[pl.ds(bases[j], chunk)] if gather else out_hbm.at[idx_v[j]]
                writes.append(pltpu.async_copy(rows_v[j], dst, sems[j]))
            for w in writes:
                w.wait()

    return move(src, idx)


def _sc_gather_rows(table, idx):
    return _sc_move_rows(table, idx, gather=True)


def _sc_scatter_rows(src, idx):
    return _sc_move_rows(src, idx, gather=False)


def _expert_kernel(blk_ref, exp_ref, lo_ref, hi_ref, x_ref, wg_ref, wu_ref, wd_ref, yb_ref,
                   wg_b, wu_b, wd_b, held_ref):
    del blk_ref
    k = pl.program_id(0)
    lo, hi, e = lo_ref[k], hi_ref[k], exp_ref[k]

    @pl.when(k == 0)
    def _():
        held_ref[0] = -1

    @pl.when(hi > lo)
    def _():
        @pl.when(held_ref[0] != e)
        def _():
            wg_b[...] = wg_ref[0].astype(_BF16)
            wu_b[...] = wu_ref[0].astype(_BF16)
            wd_b[...] = wd_ref[0].astype(_BF16)
            held_ref[0] = e

        xb = _unpack_bf16_pairs(x_ref[...]).astype(_BF16)
        g = _dot(xb, wg_b[...])
        u = _dot(xb, wu_b[...])
        hid = g * jax.nn.sigmoid(g) * u
        y = _dot(hid.astype(_BF16), wd_b[...])
        r = lax.broadcasted_iota(jnp.int32, yb_ref.shape, 0)
        pltpu.store(yb_ref, _pack_bf16_pairs(y), mask=(r >= lo) & (r < hi))


def _experts(items, xs, w_gate, w_up, w_down):
    n_items = items[0].shape[0]
    wspec = lambda a: pl.BlockSpec((1,) + a.shape[1:], lambda k, blk, exp, lo, hi: (exp[k], 0, 0))
    rows = pl.BlockSpec((MOE_ROWS, HALF), lambda k, blk, exp, lo, hi: (blk[k], 0))
    grid_spec = pltpu.PrefetchScalarGridSpec(
        num_scalar_prefetch=4,
        grid=(n_items,),
        in_specs=[rows, wspec(w_gate), wspec(w_up), wspec(w_down)],
        out_specs=rows,
        scratch_shapes=[pltpu.VMEM(w_gate.shape[1:], _BF16), pltpu.VMEM(w_up.shape[1:], _BF16),
                        pltpu.VMEM(w_down.shape[1:], _BF16), pltpu.SMEM((1,), jnp.int32)],
    )
    return pl.pallas_call(
        _expert_kernel,
        grid_spec=grid_spec,
        out_shape=jax.ShapeDtypeStruct(xs.shape, xs.dtype),
        compiler_params=pltpu.CompilerParams(dimension_semantics=("arbitrary",),
                                             vmem_limit_bytes=VMEM_LIMIT),
        cost_estimate=pl.CostEstimate(
            flops=n_items * MOE_ROWS * 6 * D_MODEL * D_EXPERT, transcendentals=n_items * MOE_ROWS * D_EXPERT,
            bytes_accessed=2 * xs.size * 4 + (w_gate.size + w_up.size + w_down.size) * 4),
        name="experts",
    )(*items, xs, w_gate, w_up, w_down)


def _combine_kernel(h_ref, route_ref, g0_ref, g1_ref, y_ref):
    route = route_ref[...]
    y_ref[...] = (h_ref[...] + route[:, ROUTE_W:ROUTE_W + 1] * _unpack_bf16_pairs(g0_ref[...])
                  + route[:, ROUTE_W + 1:ROUTE_W + 2] * _unpack_bf16_pairs(g1_ref[...]))


def _combine(h, route, g, tm):
    n = h.shape[0]
    nt = n // tm
    row = lambda w: pl.BlockSpec((tm, w), lambda i: (i, 0))
    return pl.pallas_call(
        _combine_kernel,
        grid=(nt,),
        in_specs=[row(D_MODEL), row(LANES), row(HALF), pl.BlockSpec((tm, HALF), lambda i: (nt + i, 0))],
        out_specs=row(D_MODEL),
        out_shape=jax.ShapeDtypeStruct((n, D_MODEL), _F32),
        compiler_params=pltpu.CompilerParams(dimension_semantics=("arbitrary",),
                                             vmem_limit_bytes=VMEM_LIMIT),
        name="combine",
    )(h, route, g, g)


def _work_items(counts, n_pairs):
    n_blocks = n_pairs // MOE_ROWS
    starts = jnp.cumsum(counts) - counts
    cuts = jnp.sort(jnp.concatenate([jnp.arange(n_blocks, dtype=jnp.int32) * MOE_ROWS, starts]))
    ends = jnp.concatenate([cuts[1:], jnp.full((1,), n_pairs, jnp.int32)])
    blk = jnp.minimum(cuts // MOE_ROWS, n_blocks - 1)
    expert = jnp.clip(jnp.sum(starts[None, :] <= cuts[:, None], axis=1) - 1, 0, N_EXPERTS - 1).astype(jnp.int32)
    return starts, (blk, expert, cuts - blk * MOE_ROWS, ends - blk * MOE_ROWS)


def _dispatch(hn, route, counts_rows):
    n = hn.shape[0]
    n_pairs = n * TOP_K
    counts = counts_rows[0, :N_EXPERTS].astype(jnp.int32)
    starts, items = _work_items(counts, n_pairs)
    starts_row = jnp.zeros((1, LANES), _F32).at[0, :N_EXPERTS].set(starts.astype(_F32))
    dest = _dest(route, starts_row, min(n, DEST_ROWS))[:, :TOP_K].T.reshape(n_pairs)
    return _sc_scatter_rows(hn, dest), dest, items


def _after(value, *earlier):
    return lax.optimization_barrier((value,) + earlier)[0]


def kernel(x_prompt, x_sample, state_conv, cache_k, cache_v, norm_attn_g, w_in, q_norm_g, k_norm_g, rel_bias, attn_sinks, w_dw, b_dw, conv_ln_g, conv_ln_b, w_conv_out, b_conv_out, w_attn_out, w_out, norm_ffn_g, w_grp, b_grp, w_router, b_router, w_gate, w_up, w_down):
    bsz, t, _ = x_prompt.shape
    nseq, steps, _ = x_sample.shape
    row = lambda a: a.reshape(1, -1).astype(_F32)

    w_in_b = w_in.astype(_BF16)
    wco_b = w_conv_out.astype(_BF16)
    wa_b = w_attn_out.astype(_BF16)
    wo_b = w_out.astype(_BF16)
    qg = row(jnp.tile(q_norm_g, N_HEADS)) * (HEAD_DIM ** -0.5)
    kg = row(jnp.tile(k_norm_g, N_KV_HEADS))
    w_rt = jnp.zeros((D_MODEL, LANES), _F32).at[:, :N_GROUPS].set(w_grp).at[:, N_GROUPS:N_GROUPS + N_EXPERTS].set(w_router)
    wr_b = w_rt.astype(_BF16)
    b_rt = jnp.zeros((1, LANES), _F32).at[0, :N_GROUPS].set(b_grp).at[0, N_GROUPS:N_GROUPS + N_EXPERTS].set(b_router)
    tbl = _bias_tables(rel_bias)
    conv_params = (w_dw, row(b_dw), row(conv_ln_g), row(conv_ln_b), wco_b, row(b_conv_out))

    def finish(x2d, conv_out, o, ga, gb):
        return _finish(x2d, conv_out, o, ga, gb, wa_b, wo_b, row(norm_ffn_g), wr_b, b_rt, ROW_TILE)

    xp = x_prompt.reshape(bsz * t, D_MODEL)
    q, k, v, ga, gb, conv_out, glu_tail = _inproj_conv(x_prompt, row(norm_attn_g), w_in_b, qg, kg, *conv_params,
                                                       ROW_TILE // 2)
    o = _attn_prompt(q, k, v, tbl, attn_sinks, bsz, t)
    h_p, hn_p, route_p, counts_p = finish(xp, conv_out, o, ga, gb)
    rows_p, dest_p, items_p = _dispatch(hn_p, route_p, counts_p)
    state_conv_prompt = glu_tail[:, HALO - (CONV_WIDTH - 1):]
    tail = lambda a: a.reshape(bsz, t, KV_DIM)[:, t - WINDOW:].reshape(bsz, WINDOW, N_KV_HEADS, HEAD_DIM)
    cache_k_prompt, cache_v_prompt = tail(k), tail(v)

    xs = _after(x_sample, dest_p).reshape(nseq * steps, D_MODEL)
    glu, q, k, v, ga, gb = _inproj(xs, row(norm_attn_g), w_in_b, qg, kg, _F32, ROW_TILE)
    glu_t = glu.reshape(nseq, steps, D_CONV).transpose(1, 0, 2)
    conv_out, state_t = _conv_sample(state_conv.transpose(1, 0, 2), glu_t, *conv_params, SAMPLE_CONV_SEQS)
    conv_out = conv_out.transpose(1, 0, 2).reshape(nseq * steps, D_MODEL)
    k3 = k.reshape(nseq, steps, KV_DIM)
    v3 = v.reshape(nseq, steps, KV_DIM)
    o, ck_t, cv_t = _attn_sample(q.reshape(nseq, steps, Q_DIM), k3, v3, cache_k.transpose(0, 2, 3, 1),
                                 cache_v.transpose(0, 2, 3, 1), tbl, attn_sinks, SAMPLE_ATTN_SEQS)
    h_s, hn_s, route_s, counts_s = finish(xs, conv_out, o.reshape(nseq * steps, Q_DIM), ga, gb)
    rows_s, dest_s, items_s = _dispatch(hn_s, route_s, counts_s)
    state_conv_sample = state_t.transpose(1, 0, 2)
    cache_k_sample = ck_t.transpose(0, 3, 1, 2)
    cache_v_sample = cv_t.transpose(0, 3, 1, 2)

    yb_p = _experts(items_p, rows_p, w_gate, w_up, w_down)
    g_p = _sc_gather_rows(yb_p, dest_p)
    yb_s = _experts(items_s, _after(rows_s, yb_p), w_gate, w_up, w_down)
    g_s = _sc_gather_rows(yb_s, dest_s)
    y_prompt = _combine(h_p, route_p, g_p, ROW_TILE).reshape(bsz, t, D_MODEL)
    y_sample = _combine(h_s, route_s, _after(g_s, y_prompt), ROW_TILE).reshape(nseq, steps, D_MODEL)

    return (y_prompt, y_sample, state_conv_prompt, cache_k_prompt, cache_v_prompt,
            state_conv_sample, cache_k_sample, cache_v_sample)
```

```python
import functools
import math

import numpy as np
import jax
import jax.numpy as jnp
from jax import lax
from jax.experimental import pallas as pl
from jax.experimental.pallas import tpu as pltpu
from jax.experimental.pallas import tpu_sc as plsc

D_MODEL = 1024
N_HEADS = 16
HEAD_DIM = 64
N_KV_HEADS = 4
WINDOW = 128
Q_DIM = N_HEADS * HEAD_DIM
KV_DIM = N_KV_HEADS * HEAD_DIM
N_BUCKETS = 32
MAX_EXACT = N_BUCKETS // 2
MAX_DISTANCE = 128
D_CONV = D_MODEL
CONV_WIDTH = 31
N_GROUPS = 4
EXPERTS_PER_GROUP = 8
N_EXPERTS = N_GROUPS * EXPERTS_PER_GROUP
TOP_K = 2
D_EXPERT = 256
EPS = 1e-6

LANES = 128
SUBLANES = 8
N_PAIRS = N_HEADS // 2
MOE_ROWS = 512
MASK_VALUE = -1e30
VMEM_LIMIT = 56 * 1024 * 1024
ROW_TILE = 512
SAMPLE_CONV_SEQS = 64
SAMPLE_ATTN_SEQS = 16

_F32 = jnp.float32
_BF16 = jnp.bfloat16


def _resident(a):
    return pl.BlockSpec(a.shape, lambda *_: (0,) * a.ndim, pipeline_mode=pl.Buffered(1))


def _dot(a, b):
    return jnp.dot(a, b, preferred_element_type=_F32)


HALF = D_MODEL // 2


def _pack_pair(lo, hi):
    lo_bits = pltpu.bitcast(lo.astype(_BF16).astype(_F32), jnp.uint32)
    hi_bits = pltpu.bitcast(hi.astype(_BF16).astype(_F32), jnp.uint32)
    return hi_bits | (lo_bits >> 16)


def _pack_bf16_pairs(x):
    return _pack_pair(x[:, :HALF], x[:, HALF:])


def _unpack_bf16_pairs(w):
    lo = pltpu.bitcast(w << 16, _F32)
    hi = pltpu.bitcast(w & jnp.uint32(0xFFFF0000), _F32)
    return jnp.concatenate([lo, hi], axis=1)


def _head_rms_scale(z):
    low = lax.broadcasted_iota(jnp.int32, (z.shape[0], LANES), 1) < HEAD_DIM
    slabs = []
    for c in range(z.shape[1] // LANES):
        sq = z[:, c * LANES:(c + 1) * LANES]
        sq = sq * sq
        first = jnp.sum(jnp.where(low, sq, 0.0), axis=-1, keepdims=True)
        second = jnp.sum(jnp.where(low, 0.0, sq), axis=-1, keepdims=True)
        slabs.append(lax.rsqrt(jnp.where(low, first, second) * (1.0 / HEAD_DIM) + EPS))
    return jnp.concatenate(slabs, axis=1)


def _inproj_kernel(x_ref, g_ref, w_ref, qg_ref, kg_ref,
                   glu_ref, q_ref, k_ref, v_ref, ga_ref, gb_ref):
    x = x_ref[...]
    xn = x * lax.rsqrt(jnp.mean(x * x, axis=-1, keepdims=True) + EPS) * g_ref[...]
    xb = xn.astype(_BF16)

    def seg(lo, width):
        return _dot(xb, w_ref[:, lo:lo + width])

    a = seg(0, D_CONV)
    b = seg(D_CONV, D_CONV)
    glu_ref[...] = a * jax.nn.sigmoid(b)
    off = 2 * D_CONV
    q = seg(off, Q_DIM)
    q_ref[...] = (q * _head_rms_scale(q) * qg_ref[...]).astype(q_ref.dtype)
    off += Q_DIM
    k = seg(off, KV_DIM)
    k_ref[...] = k * _head_rms_scale(k) * kg_ref[...]
    off += KV_DIM
    v_ref[...] = seg(off, KV_DIM)
    off += KV_DIM
    ga_ref[...] = jax.nn.sigmoid(seg(off, D_MODEL)).astype(ga_ref.dtype)
    off += D_MODEL
    gb_ref[...] = jax.nn.sigmoid(seg(off, D_MODEL)).astype(gb_ref.dtype)


def _inproj(x, g, w_in_b, qg, kg, q_dtype, tm):
    n = x.shape[0]
    row = lambda w: pl.BlockSpec((tm, w), lambda i: (i, 0))
    full = _resident
    return pl.pallas_call(
        _inproj_kernel,
        grid=(n // tm,),
        in_specs=[row(D_MODEL), full(g), full(w_in_b), full(qg), full(kg)],
        out_specs=[row(D_CONV), row(Q_DIM), row(KV_DIM), row(KV_DIM), row(D_MODEL), row(D_MODEL)],
        out_shape=[jax.ShapeDtypeStruct((n, D_CONV), _F32),
                   jax.ShapeDtypeStruct((n, Q_DIM), q_dtype),
                   jax.ShapeDtypeStruct((n, KV_DIM), _F32),
                   jax.ShapeDtypeStruct((n, KV_DIM), _F32),
                   jax.ShapeDtypeStruct((n, D_MODEL), _BF16),
                   jax.ShapeDtypeStruct((n, D_MODEL), _BF16)],
        compiler_params=pltpu.CompilerParams(dimension_semantics=("arbitrary",),
                                             vmem_limit_bytes=VMEM_LIMIT),
        name="inproj",
    )(x, g, w_in_b, qg, kg)


def _ln_swish_project(y, lng_ref, lnb_ref, wo_ref, bo_ref):
    mu = jnp.mean(y, axis=-1, keepdims=True)
    yc = y - mu
    var = jnp.mean(yc * yc, axis=-1, keepdims=True)
    z = yc * lax.rsqrt(var + EPS) * lng_ref[...] + lnb_ref[...]
    z = z * jax.nn.sigmoid(z)
    return (_dot(z.astype(_BF16), wo_ref[...]) + bo_ref[...]).astype(_BF16)


HALO = 32
CONV_STEPS = 16
CH_TILES = D_CONV // LANES


PIECE = 256
PAIR_TILES = 2 * CH_TILES


def _inproj_conv_kernel(x_ref, g_ref, w_ref, qg_ref, kg_ref, w16_ref, b16_ref, lng_ref, lnb_ref, wo_ref, bo_ref,
                        q_ref, k_ref, v_ref, ga_ref, gb_ref, conv_ref, tail_ref, hist_ref, y_ref, *, tm):
    i = pl.program_id(1)
    rows = 2 * tm

    @pl.when(i == 0)
    def _():
        hist_ref[0:HALO * CH_TILES, :] = jnp.zeros((HALO * CH_TILES, LANES), jnp.uint32)

    @pl.when(i > 0)
    def _():
        hist_ref[0:HALO * CH_TILES, :] = hist_ref[tm * CH_TILES:(tm + HALO) * CH_TILES, :]

    x = jnp.concatenate([x_ref[0], x_ref[1]], axis=0)
    xn = x * lax.rsqrt(jnp.mean(x * x, axis=-1, keepdims=True) + EPS) * g_ref[...]
    xb = xn.astype(_BF16)

    def seg(lo, width):
        return _dot(xb, w_ref[:, lo:lo + width])

    def put(ref, lo, val):
        ref[0, :, lo:lo + PIECE] = val[:tm].astype(ref.dtype)
        ref[1, :, lo:lo + PIECE] = val[tm:].astype(ref.dtype)

    for p in range(D_CONV // PIECE):
        lo = p * PIECE
        glu = seg(lo, PIECE) * jax.nn.sigmoid(seg(D_CONV + lo, PIECE))
        tail_ref[0, :, lo:lo + PIECE] = glu[tm - HALO:tm, :]
        tail_ref[1, :, lo:lo + PIECE] = glu[rows - HALO:, :]
        words = _pack_pair(glu[:tm], glu[tm:])
        for c in range(PIECE // LANES):
            tile = lo // LANES + c
            hist_ref[pl.ds(HALO * CH_TILES + tile, tm, stride=CH_TILES), :] = words[:, c * LANES:(c + 1) * LANES]

    first = HALO - (CONV_WIDTH - 1)

    def conv_chunk(ci, carry):
        t0 = ci * CONV_STEPS
        acc = jnp.zeros((CONV_STEPS, PAIR_TILES, LANES), _F32)
        for j in range(CONV_WIDTH):
            lo = pl.multiple_of((t0 + first + j) * CH_TILES, CH_TILES)
            xw = pltpu.bitcast(hist_ref[pl.ds(lo, CONV_STEPS * CH_TILES), :], _BF16)
            acc = acc + (xw.reshape(CONV_STEPS, PAIR_TILES, LANES).astype(_F32)
                         * w16_ref[j][None].astype(_F32))
        y_ref[pl.ds(pl.multiple_of(t0 * PAIR_TILES, PAIR_TILES), CONV_STEPS * PAIR_TILES), :] = (
            (acc + b16_ref[...][None]).reshape(CONV_STEPS * PAIR_TILES, LANES))
        return carry

    def q_piece(lo):
        q = seg(2 * D_CONV + lo, PIECE)
        put(q_ref, lo, q * _head_rms_scale(q) * qg_ref[:, lo:lo + PIECE])

    def k_piece(lo):
        k = seg(2 * D_CONV + Q_DIM + lo, PIECE)
        put(k_ref, lo, k * _head_rms_scale(k) * kg_ref[:, lo:lo + PIECE])

    def v_piece(lo):
        put(v_ref, lo, seg(2 * D_CONV + Q_DIM + KV_DIM + lo, PIECE))

    def gate_piece(ref, base, lo):
        put(ref, lo, jax.nn.sigmoid(seg(base + lo, PIECE)))

    gate_base = 2 * D_CONV + Q_DIM + 2 * KV_DIM
    pieces = ([functools.partial(q_piece, lo) for lo in range(0, Q_DIM, PIECE)]
              + [functools.partial(k_piece, lo) for lo in range(0, KV_DIM, PIECE)]
              + [functools.partial(v_piece, lo) for lo in range(0, KV_DIM, PIECE)]
              + [functools.partial(gate_piece, ga_ref, gate_base, lo) for lo in range(0, D_MODEL, PIECE)]
              + [functools.partial(gate_piece, gb_ref, gate_base + D_MODEL, lo) for lo in range(0, D_MODEL, PIECE)])

    for piece in pieces:
        piece()
    lax.fori_loop(0, tm // CONV_STEPS, conv_chunk, 0)

    y = jnp.concatenate(
        [jnp.concatenate([y_ref[pl.ds(2 * c + s, tm, stride=PAIR_TILES), :] for c in range(CH_TILES)], axis=1)
         for s in range(2)], axis=0)
    out = _ln_swish_project(y, lng_ref, lnb_ref, wo_ref, bo_ref)
    conv_ref[0] = out[:tm]
    conv_ref[1] = out[tm:]


def _inproj_conv(x, g, w_in_b, qg, kg, w_dw, b_dw, lng, lnb, wo_b, bo, tm):
    bsz, t, _ = x.shape
    assert CH_TILES == SUBLANES and bsz % 2 == 0 and t % tm == 0
    nt = t // tm
    w16 = jnp.repeat(w_dw.reshape(CONV_WIDTH, CH_TILES, LANES), 2, axis=1).astype(_BF16)
    b16 = jnp.repeat(b_dw.reshape(CH_TILES, LANES), 2, axis=0)
    full = _resident
    blk = lambda w: pl.BlockSpec((2, tm, w), lambda p, i: (p, i, 0))
    shape = lambda w, dt: jax.ShapeDtypeStruct((bsz, t, w), dt)
    outs = pl.pallas_call(
        functools.partial(_inproj_conv_kernel, tm=tm),
        grid=(bsz // 2, nt),
        in_specs=[blk(D_MODEL), full(g), full(w_in_b), full(qg), full(kg), full(w16), full(b16), full(lng), full(lnb),
                  full(wo_b), full(bo)],
        out_specs=[blk(Q_DIM), blk(KV_DIM), blk(KV_DIM), blk(D_MODEL), blk(D_MODEL), blk(D_MODEL),
                   pl.BlockSpec((2, HALO, D_CONV), lambda p, i: (p, 0, 0))],
        out_shape=[shape(Q_DIM, _BF16), shape(KV_DIM, _F32), shape(KV_DIM, _F32), shape(D_MODEL, _BF16),
                   shape(D_MODEL, _BF16), shape(D_MODEL, _BF16), jax.ShapeDtypeStruct((bsz, HALO, D_CONV), _F32)],
        scratch_shapes=[pltpu.VMEM(((tm + HALO) * CH_TILES, LANES), jnp.uint32),
                        pltpu.VMEM((tm * PAIR_TILES, LANES), _F32)],
        compiler_params=pltpu.CompilerParams(dimension_semantics=("arbitrary", "arbitrary"),
                                             vmem_limit_bytes=VMEM_LIMIT),
        name="inproj_conv",
    )(x, g, w_in_b, qg, kg, w16, b16, lng, lnb, wo_b, bo)
    return [o.reshape(bsz * t, o.shape[-1]) for o in outs[:-1]] + [outs[-1]]


def _conv_sample_kernel(state_ref, glu_ref, wdw_ref, bdw_ref, lng_ref, lnb_ref, wo_ref, bo_ref,
                        out_ref, state_out_ref):
    keep, steps = state_ref.shape[0], glu_ref.shape[0]

    def hist(u):
        return state_ref[u] if u < keep else glu_ref[u - keep]

    for t in range(steps):
        acc = hist(t) * wdw_ref[0:1, :]
        for j in range(1, CONV_WIDTH):
            acc = acc + hist(t + j) * wdw_ref[j:j + 1, :]
        out_ref[t] = _ln_swish_project(acc + bdw_ref[...], lng_ref, lnb_ref, wo_ref, bo_ref)
    state_out_ref[0:keep - steps] = state_ref[steps:keep]
    state_out_ref[keep - steps:keep] = glu_ref[...]


def _conv_sample(state_t, glu_t, w_dw, b_dw, lng, lnb, wo_b, bo, sb):
    keep, nseq, _ = state_t.shape
    steps = glu_t.shape[0]
    full = _resident
    blk = lambda r: pl.BlockSpec((r, sb, D_CONV), lambda i: (0, i, 0))
    return pl.pallas_call(
        _conv_sample_kernel,
        grid=(nseq // sb,),
        in_specs=[blk(keep), blk(steps), full(w_dw), full(b_dw), full(lng), full(lnb), full(wo_b), full(bo)],
        out_specs=[blk(steps), blk(keep)],
        out_shape=[jax.ShapeDtypeStruct((steps, nseq, D_MODEL), _BF16),
                   jax.ShapeDtypeStruct(state_t.shape, _F32)],
        compiler_params=pltpu.CompilerParams(dimension_semantics=("arbitrary",),
                                             vmem_limit_bytes=VMEM_LIMIT),
        name="conv_sample",
    )(state_t, glu_t, w_dw, b_dw, lng, lnb, wo_b, bo)


def _bucket_map():
    i = np.arange(WINDOW)[:, None]
    j = np.arange(WINDOW)[None, :]
    n = (i - j) % WINDOW
    nf = np.maximum(n, 1).astype(np.float32)
    large = MAX_EXACT + (np.log(nf / np.float32(MAX_EXACT)) / np.float32(math.log(MAX_DISTANCE / MAX_EXACT))
                         * np.float32(N_BUCKETS - MAX_EXACT)).astype(np.int32)
    return np.where(n < MAX_EXACT, n, np.minimum(large, N_BUCKETS - 1)).astype(np.int32)


def _bias_table_kernel(rb_ref, bm_ref, tbl_ref):
    p = pl.program_id(0)
    bm = bm_ref[...]
    for half in range(2):
        h = 2 * p + half
        t = jnp.zeros(bm.shape, _F32)
        for b in range(N_BUCKETS):
            t = jnp.where(bm == b, rb_ref[b, h], t)
        tbl_ref[0, :, half * WINDOW:(half + 1) * WINDOW] = t


def _bias_tables(rel_bias):
    bm = jnp.asarray(_bucket_map())
    return pl.pallas_call(
        _bias_table_kernel,
        grid=(N_PAIRS,),
        in_specs=[pl.BlockSpec(memory_space=pltpu.SMEM), pl.BlockSpec(bm.shape, lambda p: (0, 0))],
        out_specs=pl.BlockSpec((1, WINDOW, 2 * WINDOW), lambda p: (p, 0, 0)),
        out_shape=jax.ShapeDtypeStruct((N_PAIRS, WINDOW, 2 * WINDOW), _F32),
        name="bias_tables",
    )(rel_bias, bm)


def _block_diag_pairs(slab):
    low = lax.broadcasted_iota(jnp.int32, slab.shape, 1) < HEAD_DIM
    swapped = pltpu.roll(slab, HEAD_DIM, axis=1)
    zero = jnp.zeros_like(slab)
    first = jnp.concatenate([jnp.where(low, slab, zero), jnp.where(low, zero, swapped)], axis=0)
    second = jnp.concatenate([jnp.where(low, swapped, zero), jnp.where(low, zero, slab)], axis=0)
    return first.astype(_BF16), second.astype(_BF16)


def _kv_operands(k_blk, v_blk):
    ops = []
    for slab in range(KV_DIM // LANES):
        cols = slice(slab * LANES, (slab + 1) * LANES)
        ops.extend(zip(_block_diag_pairs(k_blk[:, cols]), _block_diag_pairs(v_blk[:, cols])))
    return ops


def _attend(q, prev_ops, own_ops, tbl_ref, sink_ref, prev_shift, store, transposed=False):
    tq = q.shape[0]
    rows = 2 * tq
    row = lax.broadcasted_iota(jnp.int32, (rows, 2 * WINDOW), 0)
    col = lax.broadcasted_iota(jnp.int32, (rows, 2 * WINDOW), 1)
    from_prev = (col & (WINDOW - 1)) > jnp.where(row >= tq, row - tq, row)
    top = lax.broadcasted_iota(jnp.int32, (rows, 1), 0) < tq
    low = lax.broadcasted_iota(jnp.int32, (rows, LANES), 1) < HEAD_DIM
    contract_last = (((1,), (1,)), ((), ()))

    def logits(a, k_op):
        return _dot(a, k_op) if transposed else lax.dot_general(a, k_op, contract_last, preferred_element_type=_F32)

    def weighted_values(pr, v_op):
        return lax.dot_general(pr, v_op, contract_last, preferred_element_type=_F32) if transposed else _dot(pr, v_op)

    for kvh in range(N_KV_HEADS):
        (k_prev, v_prev), (k_own, v_own) = prev_ops[kvh], own_ops[kvh]
        pair_a = 2 * kvh
        pair_b = pair_a + 1
        qq = jnp.concatenate([q[:, pair_a * LANES:(pair_a + 1) * LANES],
                              q[:, pair_b * LANES:(pair_b + 1) * LANES]], axis=0).astype(_BF16)
        sp = logits(qq, k_prev)
        so = logits(qq, k_own)
        bias = jnp.concatenate([tbl_ref[pair_a, 0:tq, :], tbl_ref[pair_b, 0:tq, :]], axis=0)
        s = jnp.where(from_prev, sp + prev_shift, so) + bias
        sink_even = jnp.where(top, sink_ref[2 * pair_a], sink_ref[2 * pair_b])
        sink_odd = jnp.where(top, sink_ref[2 * pair_a + 1], sink_ref[2 * pair_b + 1])
        m_even = jnp.maximum(jnp.max(s[:, :WINDOW], axis=-1, keepdims=True), sink_even)
        m_odd = jnp.maximum(jnp.max(s[:, WINDOW:], axis=-1, keepdims=True), sink_odd)
        e_even = jnp.exp(s[:, :WINDOW] - m_even)
        e_odd = jnp.exp(s[:, WINDOW:] - m_odd)
        p = jnp.concatenate([e_even, e_odd], axis=1).astype(_BF16)
        zero = jnp.zeros_like(p)
        o = (weighted_values(jnp.where(from_prev, p, zero), v_prev)
             + weighted_values(jnp.where(from_prev, zero, p), v_own))
        den_even = jnp.sum(e_even, axis=-1, keepdims=True) + jnp.exp(sink_even - m_even)
        den_odd = jnp.sum(e_odd, axis=-1, keepdims=True) + jnp.exp(sink_odd - m_odd)
        o = o * jnp.where(low, 1.0 / den_even, 1.0 / den_odd)
        store(pair_a, o[:tq])
        store(pair_b, o[tq:])


PROMPT_QBLOCKS = 8


def _attn_prompt_kernel(sink_ref, q_ref, kp_ref, ko_ref, vp_ref, vo_ref, tbl_ref, o_ref):
    prev_shift = jnp.where(pl.program_id(1) == 0, MASK_VALUE, 0.0).astype(_F32)
    ops = [_kv_operands(kp_ref[...], vp_ref[...])]
    for b in range(PROMPT_QBLOCKS):
        rows = slice(b * WINDOW, (b + 1) * WINDOW)
        ops.append(_kv_operands(ko_ref[rows, :], vo_ref[rows, :]))

        def store(pair, o, rows=rows):
            o_ref[rows, pair * LANES:(pair + 1) * LANES] = o.astype(o_ref.dtype)

        _attend(q_ref[rows, :], ops[b], ops[b + 1], tbl_ref, sink_ref,
                prev_shift if b == 0 else jnp.float32(0.0), store)


def _attn_prompt(q, k, v, tbl, sinks, bsz, t):
    tq = PROMPT_QBLOCKS * WINDOW
    nb = t // tq
    own = lambda w: pl.BlockSpec((tq, w), lambda b, i: (b * nb + i, 0))
    prev = lambda w: pl.BlockSpec((WINDOW, w),
                                  lambda b, i: (PROMPT_QBLOCKS * (b * nb + i) - jnp.minimum(i, 1), 0))
    return pl.pallas_call(
        _attn_prompt_kernel,
        grid=(bsz, nb),
        in_specs=[pl.BlockSpec(memory_space=pltpu.SMEM), own(Q_DIM), prev(KV_DIM), own(KV_DIM),
                  prev(KV_DIM), own(KV_DIM), pl.BlockSpec(tbl.shape, lambda b, i: (0, 0, 0))],
        out_specs=own(Q_DIM),
        out_shape=jax.ShapeDtypeStruct((bsz * t, Q_DIM), _BF16),
        compiler_params=pltpu.CompilerParams(dimension_semantics=("arbitrary", "arbitrary"),
                                             vmem_limit_bytes=VMEM_LIMIT),
        name="attn_prompt",
    )(sinks, q, k, k, v, v, tbl)


def _block_diag_t(x):
    xb = x.astype(_BF16)
    z = jnp.zeros_like(xb)
    return jnp.concatenate([jnp.concatenate([xb, z], axis=1), jnp.concatenate([z, xb], axis=1)], axis=0)


def _attn_sample_kernel(sink_ref, q_ref, kn_ref, vn_ref, ck_ref, cv_ref, tbl_ref, o_ref, cko_ref, cvo_ref,
                        *, sb, steps):
    pad = jnp.zeros((WINDOW - steps, LANES), _F32)
    lane = lax.broadcasted_iota(jnp.int32, (HEAD_DIM, WINDOW), 1)

    def one_sequence(s):
        def store(pair, o):
            o_ref[s, :, pair * LANES:(pair + 1) * LANES] = o

        prev_ops, own_ops = [], []
        for slab in range(KV_DIM // LANES):
            cols = slice(slab * LANES, (slab + 1) * LANES)
            new_k = jnp.concatenate([kn_ref[s][:, cols], pad], axis=0).T
            new_v = jnp.concatenate([vn_ref[s][:, cols], pad], axis=0).T
            for sub in range(2):
                kvh = 2 * slab + sub
                part = slice(sub * HEAD_DIM, (sub + 1) * HEAD_DIM)
                kt, vt = ck_ref[s, kvh], cv_ref[s, kvh]
                cko_ref[s, kvh] = pltpu.roll(jnp.where(lane < steps, new_k[part], kt), WINDOW - steps, axis=1)
                cvo_ref[s, kvh] = pltpu.roll(jnp.where(lane < steps, new_v[part], vt), WINDOW - steps, axis=1)
                prev_ops.append((_block_diag_t(kt), _block_diag_t(vt)))
                own_ops.append((_block_diag_t(new_k[part]), _block_diag_t(new_v[part])))
        _attend(q_ref[s], prev_ops, own_ops, tbl_ref, sink_ref, jnp.float32(0.0), store, transposed=True)

    for s in range(sb):
        one_sequence(s)


def _attn_sample(q, k_new, v_new, cache_kt, cache_vt, tbl, sinks, sb):
    nseq, steps, _ = q.shape
    seq = lambda r, w: pl.BlockSpec((sb, r, w), lambda i: (i, 0, 0))
    cache = pl.BlockSpec((sb, N_KV_HEADS, HEAD_DIM, WINDOW), lambda i: (i, 0, 0, 0))
    return pl.pallas_call(
        functools.partial(_attn_sample_kernel, sb=sb, steps=steps),
        grid=(nseq // sb,),
        in_specs=[pl.BlockSpec(memory_space=pltpu.SMEM), seq(steps, Q_DIM), seq(steps, KV_DIM), seq(steps, KV_DIM),
                  cache, cache, pl.BlockSpec(tbl.shape, lambda i: (0, 0, 0))],
        out_specs=[seq(steps, Q_DIM), cache, cache],
        out_shape=[jax.ShapeDtypeStruct((nseq, steps, Q_DIM), _F32),
                   jax.ShapeDtypeStruct(cache_kt.shape, _F32), jax.ShapeDtypeStruct(cache_vt.shape, _F32)],
        compiler_params=pltpu.CompilerParams(dimension_semantics=("arbitrary",),
                                             vmem_limit_bytes=VMEM_LIMIT),
        name="attn_sample",
    )(sinks, q, k_new, v_new, cache_kt, cache_vt, tbl)


def _lane_min_index(mask, lane):
    return jnp.min(jnp.where(mask, lane, float(LANES)), axis=-1, keepdims=True)


def _finish_kernel(x_ref, conv_ref, o_ref, ga_ref, gb_ref, wa_ref, wo_ref, ng_ref, wr_ref, br_ref,
                   tri_ref, h_ref, hn_ref, route_ref, count_ref, running_ref):
    @pl.when(pl.program_id(0) == 0)
    def _():
        running_ref[...] = jnp.zeros_like(running_ref)

    attn_out = _dot(o_ref[...].astype(_BF16), wa_ref[...])
    merged = ga_ref[...].astype(_F32) * conv_ref[...].astype(_F32) + gb_ref[...].astype(_F32) * attn_out
    h = x_ref[...] + _dot(merged.astype(_BF16), wo_ref[...])
    h_ref[...] = h
    hn = h * lax.rsqrt(jnp.mean(h * h, axis=-1, keepdims=True) + EPS) * ng_ref[...]
    hn_ref[...] = _pack_bf16_pairs(hn)

    logits = _dot(hn.astype(_BF16), wr_ref[...]) + br_ref[...]
    lane = lax.broadcasted_iota(jnp.int32, logits.shape, 1).astype(_F32)
    gmask = lane < N_GROUPS
    gl = jnp.where(gmask, logits, MASK_VALUE)
    gmax = jnp.max(gl, axis=-1, keepdims=True)
    grp = _lane_min_index(gmask & (gl == gmax), lane)
    p_grp = 1.0 / jnp.sum(jnp.where(gmask, jnp.exp(gl - gmax), 0.0), axis=-1, keepdims=True)
    e_lo = N_GROUPS + grp * EXPERTS_PER_GROUP
    emask = (lane >= e_lo) & (lane < e_lo + EXPERTS_PER_GROUP)
    el = jnp.where(emask, logits, MASK_VALUE)
    ex = jnp.where(emask, jnp.exp(el - jnp.max(el, axis=-1, keepdims=True)), 0.0)
    prob = jnp.where(emask, ex / jnp.sum(ex, axis=-1, keepdims=True), -1.0)
    p1 = jnp.max(prob, axis=-1, keepdims=True)
    i1 = _lane_min_index(prob == p1, lane)
    rest = jnp.where(lane == i1, -1.0, prob)
    p2 = jnp.max(rest, axis=-1, keepdims=True)
    i2 = _lane_min_index(rest == p2, lane)
    w1 = p_grp * p1 / (p1 + p2)
    w2 = p_grp * p2 / (p1 + p2)
    e1 = i1 - N_GROUPS
    e2 = i2 - N_GROUPS

    hot1 = lane == e1
    hot2 = lane == e2
    hot = jnp.where(hot1 | hot2, 1.0, 0.0)
    before = _dot(tri_ref[...], hot.astype(_BF16)) + running_ref[...]
    rank1 = jnp.sum(jnp.where(hot1, before, 0.0), axis=-1, keepdims=True)
    rank2 = jnp.sum(jnp.where(hot2, before, 0.0), axis=-1, keepdims=True)
    running_ref[...] += jnp.sum(hot, axis=0, keepdims=True)
    count_ref[...] = jnp.broadcast_to(running_ref[...], count_ref.shape)

    fields = (e1, e2, w1, w2, rank1, rank2)
    route = jnp.zeros(logits.shape, _F32)
    for pos, val in enumerate(fields):
        route = jnp.where(lane == pos, val, route)
    route_ref[...] = route


ROUTE_E, ROUTE_W, ROUTE_RANK = 0, 2, 4
DEST_ROWS = 1024


def _finish(x, conv_out, o, ga, gb, wa_b, wo_b, ng, wr_b, br, tm):
    n = x.shape[0]
    tri = jnp.asarray(np.tril(np.ones((tm, tm), np.float32), -1), _BF16)
    row = lambda w: pl.BlockSpec((tm, w), lambda i: (i, 0))
    full = _resident
    return pl.pallas_call(
        _finish_kernel,
        grid=(n // tm,),
        in_specs=[row(D_MODEL), row(D_MODEL), row(Q_DIM), row(D_MODEL), row(D_MODEL),
                  full(wa_b), full(wo_b), full(ng), full(wr_b), full(br), full(tri)],
        out_specs=[row(D_MODEL), row(HALF), row(LANES), pl.BlockSpec((SUBLANES, LANES), lambda i: (0, 0))],
        out_shape=[jax.ShapeDtypeStruct((n, D_MODEL), _F32),
                   jax.ShapeDtypeStruct((n, HALF), jnp.uint32),
                   jax.ShapeDtypeStruct((n, LANES), _F32),
                   jax.ShapeDtypeStruct((SUBLANES, LANES), _F32)],
        scratch_shapes=[pltpu.VMEM((1, LANES), _F32)],
        compiler_params=pltpu.CompilerParams(dimension_semantics=("arbitrary",),
                                             vmem_limit_bytes=VMEM_LIMIT),
        name="finish",
    )(x, conv_out, o, ga, gb, wa_b, wo_b, ng, wr_b, br, tri)


def _dest_kernel(route_ref, starts_ref, dest_ref):
    route = route_ref[...]
    lane = lax.broadcasted_iota(jnp.int32, route.shape, 1)
    out = jnp.zeros(route.shape, jnp.int32)
    for j in range(TOP_K):
        e = route[:, ROUTE_E + j:ROUTE_E + j + 1].astype(jnp.int32)
        start = jnp.sum(jnp.where(lane == e, starts_ref[...], 0.0), axis=-1, keepdims=True)
        d = (start + route[:, ROUTE_RANK + j:ROUTE_RANK + j + 1]).astype(jnp.int32)
        out = jnp.where(lane == j, d, out)
    dest_ref[...] = out


def _dest(route, starts_row, tm):
    n = route.shape[0]
    row = pl.BlockSpec((tm, LANES), lambda i: (i, 0))
    return pl.pallas_call(
        _dest_kernel,
        grid=(n // tm,),
        in_specs=[row, pl.BlockSpec((1, LANES), lambda i: (0, 0))],
        out_specs=row,
        out_shape=jax.ShapeDtypeStruct((n, LANES), jnp.int32),
        name="dest",
    )(route, starts_row)


SC_CORES = 2
SC_SUBCORES = 16
SC_WORKERS = SC_CORES * SC_SUBCORES
SC_IN_FLIGHT = 4
SC_CHUNK_BYTES = 64 * 1024


def _sc_move_rows(src, idx, gather):
    n, d = src.shape
    b = idx.shape[0]
    per_worker = b // SC_WORKERS
    assert per_worker * SC_WORKERS == b and (gather or n % per_worker == 0), (b, n)
    chunk = min(per_worker // SC_IN_FLIGHT, SC_CHUNK_BYTES // (d * 4))
    n_iters = per_worker // (chunk * SC_IN_FLIGHT)
    assert n_iters * chunk * SC_IN_FLIGHT == per_worker and chunk % SUBLANES == 0, (per_worker, chunk)
    mesh = plsc.VectorSubcoreMesh(core_axis_name="c", subcore_axis_name="s",
                                  num_cores=SC_CORES, num_subcores=SC_SUBCORES)
    scratch = ([pltpu.VMEM((chunk,), jnp.int32)] * SC_IN_FLIGHT + [pltpu.VMEM((chunk, d), src.dtype)] * SC_IN_FLIGHT
               + [pltpu.SemaphoreType.DMA] * SC_IN_FLIGHT)

    @functools.partial(pl.kernel, mesh=mesh, out_type=jax.ShapeDtypeStruct((b, d), src.dtype),
                       scratch_types=scratch, name="sc_gather_rows" if gather else "sc_scatter_rows",
                       cost_estimate=pl.CostEstimate(flops=0, transcendentals=0, bytes_accessed=2 * b * d * 4 + b * 4))
    def move(src_hbm, idx_hbm, out_hbm, *bufs):
        idx_v = bufs[:SC_IN_FLIGHT]
        rows_v = bufs[SC_IN_FLIGHT:2 * SC_IN_FLIGHT]
        sems = bufs[2 * SC_IN_FLIGHT:]
        worker = lax.axis_index("s") * SC_CORES + lax.axis_index("c")

        @pl.loop(0, n_iters)
        def _(it):
            bases = [pl.multiple_of(worker * per_worker + (it * SC_IN_FLIGHT + j) * chunk, chunk)
                     for j in range(SC_IN_FLIGHT)]
            loads = [pltpu.async_copy(idx_hbm.at[pl.ds(bases[j], chunk)], idx_v[j], sems[j])
                     for j in range(SC_IN_FLIGHT)]
            reads = []
            for j in range(SC_IN_FLIGHT):
                loads[j].wait()
                if gather:
                    rows = src_hbm.at[idx_v[j]]
                else:
                    rows = src_hbm.at[pl.ds(pl.multiple_of(lax.rem(bases[j], n), chunk), chunk)]
                reads.append(pltpu.async_copy(rows, rows_v[j], sems[j]))
            writes = []
            for j in range(SC_IN_FLIGHT):
                reads[j].wait()
                dst = out_hbm.at[pl.ds(bases[j], chunk)] if gather else out_hbm.at[idx_v[j]]
                writes.append(pltpu.async_copy(rows_v[j], dst, sems[j]))
            for w in writes:
                w.wait()

    return move(src, idx)


def _sc_gather_rows(table, idx):
    return _sc_move_rows(table, idx, gather=True)


def _sc_scatter_rows(src, idx):
    return _sc_move_rows(src, idx, gather=False)


def _expert_kernel(blk_ref, exp_ref, lo_ref, hi_ref, x_ref, wg_ref, wu_ref, wd_ref, yb_ref,
                   wg_b, wu_b, wd_b, held_ref):
    del blk_ref
    k = pl.program_id(0)
    lo, hi, e = lo_ref[k], hi_ref[k], exp_ref[k]

    @pl.when(k == 0)
    def _():
        held_ref[0] = -1

    @pl.when(hi > lo)
    def _():
        @pl.when(held_ref[0] != e)
        def _():
            wg_b[...] = wg_ref[0].astype(_BF16)
            wu_b[...] = wu_ref[0].astype(_BF16)
            wd_b[...] = wd_ref[0].astype(_BF16)
            held_ref[0] = e

        xb = _unpack_bf16_pairs(x_ref[...]).astype(_BF16)
        g = _dot(xb, wg_b[...])
        u = _dot(xb, wu_b[...])
        hid = g * jax.nn.sigmoid(g) * u
        y = _dot(hid.astype(_BF16), wd_b[...])
        r = lax.broadcasted_iota(jnp.int32, yb_ref.shape, 0)
        pltpu.store(yb_ref, _pack_bf16_pairs(y), mask=(r >= lo) & (r < hi))


def _experts(items, xs, w_gate, w_up, w_down):
    n_items = items[0].shape[0]
    wspec = lambda a: pl.BlockSpec((1,) + a.shape[1:], lambda k, blk, exp, lo, hi: (exp[k], 0, 0))
    rows = pl.BlockSpec((MOE_ROWS, HALF), lambda k, blk, exp, lo, hi: (blk[k], 0))
    grid_spec = pltpu.PrefetchScalarGridSpec(
        num_scalar_prefetch=4,
        grid=(n_items,),
        in_specs=[rows, wspec(w_gate), wspec(w_up), wspec(w_down)],
        out_specs=rows,
        scratch_shapes=[pltpu.VMEM(w_gate.shape[1:], _BF16), pltpu.VMEM(w_up.shape[1:], _BF16),
                        pltpu.VMEM(w_down.shape[1:], _BF16), pltpu.SMEM((1,), jnp.int32)],
    )
    return pl.pallas_call(
        _expert_kernel,
        grid_spec=grid_spec,
        out_shape=jax.ShapeDtypeStruct(xs.shape, xs.dtype),
        compiler_params=pltpu.CompilerParams(dimension_semantics=("arbitrary",),
                                             vmem_limit_bytes=VMEM_LIMIT),
        cost_estimate=pl.CostEstimate(
            flops=n_items * MOE_ROWS * 6 * D_MODEL * D_EXPERT, transcendentals=n_items * MOE_ROWS * D_EXPERT,
            bytes_accessed=2 * xs.size * 4 + (w_gate.size + w_up.size + w_down.size) * 4),
        name="experts",
    )(*items, xs, w_gate, w_up, w_down)


def _combine_kernel(h_ref, route_ref, g0_ref, g1_ref, y_ref):
    route = route_ref[...]
    y_ref[...] = (h_ref[...] + route[:, ROUTE_W:ROUTE_W + 1] * _unpack_bf16_pairs(g0_ref[...])
                  + route[:, ROUTE_W + 1:ROUTE_W + 2] * _unpack_bf16_pairs(g1_ref[...]))


def _combine(h, route, g, tm):
    n = h.shape[0]
    nt = n // tm
    row = lambda w: pl.BlockSpec((tm, w), lambda i: (i, 0))
    return pl.pallas_call(
        _combine_kernel,
        grid=(nt,),
        in_specs=[row(D_MODEL), row(LANES), row(HALF), pl.BlockSpec((tm, HALF), lambda i: (nt + i, 0))],
        out_specs=row(D_MODEL),
        out_shape=jax.ShapeDtypeStruct((n, D_MODEL), _F32),
        compiler_params=pltpu.CompilerParams(dimension_semantics=("arbitrary",),
                                             vmem_limit_bytes=VMEM_LIMIT),
        name="combine",
    )(h, route, g, g)


def _work_items(counts, n_pairs):
    n_blocks = n_pairs // MOE_ROWS
    starts = jnp.cumsum(counts) - counts
    cuts = jnp.sort(jnp.concatenate([jnp.arange(n_blocks, dtype=jnp.int32) * MOE_ROWS, starts]))
    ends = jnp.concatenate([cuts[1:], jnp.full((1,), n_pairs, jnp.int32)])
    blk = jnp.minimum(cuts // MOE_ROWS, n_blocks - 1)
    expert = jnp.clip(jnp.sum(starts[None, :] <= cuts[:, None], axis=1) - 1, 0, N_EXPERTS - 1).astype(jnp.int32)
    return starts, (blk, expert, cuts - blk * MOE_ROWS, ends - blk * MOE_ROWS)


def _dispatch(hn, route, counts_rows):
    n = hn.shape[0]
    n_pairs = n * TOP_K
    counts = counts_rows[0, :N_EXPERTS].astype(jnp.int32)
    starts, items = _work_items(counts, n_pairs)
    starts_row = jnp.zeros((1, LANES), _F32).at[0, :N_EXPERTS].set(starts.astype(_F32))
    dest = _dest(route, starts_row, min(n, DEST_ROWS))[:, :TOP_K].T.reshape(n_pairs)
    return _sc_scatter_rows(hn, dest), dest, items


def _after(value, *earlier):
    return lax.optimization_barrier((value,) + earlier)[0]


def kernel(x_prompt, x_sample, state_conv, cache_k, cache_v, norm_attn_g, w_in, q_norm_g, k_norm_g, rel_bias, attn_sinks, w_dw, b_dw, conv_ln_g, conv_ln_b, w_conv_out, b_conv_out, w_attn_out, w_out, norm_ffn_g, w_grp, b_grp, w_router, b_router, w_gate, w_up, w_down):
    bsz, t, _ = x_prompt.shape
    nseq, steps, _ = x_sample.shape
    row = lambda a: a.reshape(1, -1).astype(_F32)

    w_in_b = w_in.astype(_BF16)
    wco_b = w_conv_out.astype(_BF16)
    wa_b = w_attn_out.astype(_BF16)
    wo_b = w_out.astype(_BF16)
    qg = row(jnp.tile(q_norm_g, N_HEADS)) * (HEAD_DIM ** -0.5)
    kg = row(jnp.tile(k_norm_g, N_KV_HEADS))
    w_rt = jnp.zeros((D_MODEL, LANES), _F32).at[:, :N_GROUPS].set(w_grp).at[:, N_GROUPS:N_GROUPS + N_EXPERTS].set(w_router)
    wr_b = w_rt.astype(_BF16)
    b_rt = jnp.zeros((1, LANES), _F32).at[0, :N_GROUPS].set(b_grp).at[0, N_GROUPS:N_GROUPS + N_EXPERTS].set(b_router)
    tbl = _bias_tables(rel_bias)
    conv_params = (w_dw, row(b_dw), row(conv_ln_g), row(conv_ln_b), wco_b, row(b_conv_out))

    def finish(x2d, conv_out, o, ga, gb):
        return _finish(x2d, conv_out, o, ga, gb, wa_b, wo_b, row(norm_ffn_g), wr_b, b_rt, ROW_TILE)

    xp = x_prompt.reshape(bsz * t, D_MODEL)
    q, k, v, ga, gb, conv_out, glu_tail = _inproj_conv(x_prompt, row(norm_attn_g), w_in_b, qg, kg, *conv_params,
                                                       ROW_TILE // 2)
    o = _attn_prompt(q, k, v, tbl, attn_sinks, bsz, t)
    h_p, hn_p, route_p, counts_p = finish(xp, conv_out, o, ga, gb)
    rows_p, dest_p, items_p = _dispatch(hn_p, route_p, counts_p)
    state_conv_prompt = glu_tail[:, HALO - (CONV_WIDTH - 1):]
    tail = lambda a: a.reshape(bsz, t, KV_DIM)[:, t - WINDOW:].reshape(bsz, WINDOW, N_KV_HEADS, HEAD_DIM)
    cache_k_prompt, cache_v_prompt = tail(k), tail(v)

    xs = _after(x_sample, dest_p).reshape(nseq * steps, D_MODEL)
    glu, q, k, v, ga, gb = _inproj(xs, row(norm_attn_g), w_in_b, qg, kg, _F32, ROW_TILE)
    glu_t = glu.reshape(nseq, steps, D_CONV).transpose(1, 0, 2)
    conv_out, state_t = _conv_sample(state_conv.transpose(1, 0, 2), glu_t, *conv_params, SAMPLE_CONV_SEQS)
    conv_out = conv_out.transpose(1, 0, 2).reshape(nseq * steps, D_MODEL)
    k3 = k.reshape(nseq, steps, KV_DIM)
    v3 = v.reshape(nseq, steps, KV_DIM)
    o, ck_t, cv_t = _attn_sample(q.reshape(nseq, steps, Q_DIM), k3, v3, cache_k.transpose(0, 2, 3, 1),
                                 cache_v.transpose(0, 2, 3, 1), tbl, attn_sinks, SAMPLE_ATTN_SEQS)
    h_s, hn_s, route_s, counts_s = finish(xs, conv_out, o.reshape(nseq * steps, Q_DIM), ga, gb)
    rows_s, dest_s, items_s = _dispatch(hn_s, route_s, counts_s)
    state_conv_sample = state_t.transpose(1, 0, 2)
    cache_k_sample = ck_t.transpose(0, 3, 1, 2)
    cache_v_sample = cv_t.transpose(0, 3, 1, 2)

    yb_p = _experts(items_p, rows_p, w_gate, w_up, w_down)
    g_p = _sc_gather_rows(yb_p, dest_p)
    yb_s = _experts(items_s, _after(rows_s, yb_p), w_gate, w_up, w_down)
    g_s = _sc_gather_rows(yb_s, dest_s)
    y_prompt = _combine(h_p, route_p, g_p, ROW_TILE).reshape(bsz, t, D_MODEL)
    y_sample = _combine(h_s, route_s, _after(g_s, y_prompt), ROW_TILE).reshape(nseq, steps, D_MODEL)

    return (y_prompt, y_sample, state_conv_prompt, cache_k_prompt, cache_v_prompt,
            state_conv_sample, cache_k_sample, cache_v_sample)
```

```python
import functools
import math

import numpy as np
import jax
import jax.numpy as jnp
from jax import lax
from jax.experimental import pallas as pl
from jax.experimental.pallas import tpu as pltpu
from jax.experimental.pallas import tpu_sc as plsc

D_MODEL = 1024
N_HEADS = 16
HEAD_DIM = 64
N_KV_HEADS = 4
WINDOW = 128
Q_DIM = N_HEADS * HEAD_DIM
KV_DIM = N_KV_HEADS * HEAD_DIM
N_BUCKETS = 32
MAX_EXACT = N_BUCKETS // 2
MAX_DISTANCE = 128
D_CONV = D_MODEL
CONV_WIDTH = 31
N_GROUPS = 4
EXPERTS_PER_GROUP = 8
N_EXPERTS = N_GROUPS * EXPERTS_PER_GROUP
TOP_K = 2
D_EXPERT = 256
EPS = 1e-6

LANES = 128
SUBLANES = 8
N_PAIRS = N_HEADS // 2
MOE_ROWS = 512
MASK_VALUE = -1e30
VMEM_LIMIT = 56 * 1024 * 1024
ROW_TILE = 512
SAMPLE_CONV_SEQS = 64
SAMPLE_ATTN_SEQS = 8

_F32 = jnp.float32
_BF16 = jnp.bfloat16


def _resident(a):
    return pl.BlockSpec(a.shape, lambda *_: (0,) * a.ndim, pipeline_mode=pl.Buffered(1))


def _dot(a, b):
    return jnp.dot(a, b, preferred_element_type=_F32)


HALF = D_MODEL // 2


def _pack_pair(lo, hi):
    lo_bits = pltpu.bitcast(lo.astype(_BF16).astype(_F32), jnp.uint32)
    hi_bits = pltpu.bitcast(hi.astype(_BF16).astype(_F32), jnp.uint32)
    return hi_bits | (lo_bits >> 16)


def _pack_bf16_pairs(x):
    return _pack_pair(x[:, :HALF], x[:, HALF:])


def _unpack_bf16_pairs(w):
    lo = pltpu.bitcast(w << 16, _F32)
    hi = pltpu.bitcast(w & jnp.uint32(0xFFFF0000), _F32)
    return jnp.concatenate([lo, hi], axis=1)


def _head_rms_scale(z):
    low = lax.broadcasted_iota(jnp.int32, (z.shape[0], LANES), 1) < HEAD_DIM
    slabs = []
    for c in range(z.shape[1] // LANES):
        sq = z[:, c * LANES:(c + 1) * LANES]
        sq = sq * sq
        first = jnp.sum(jnp.where(low, sq, 0.0), axis=-1, keepdims=True)
        second = jnp.sum(jnp.where(low, 0.0, sq), axis=-1, keepdims=True)
        slabs.append(lax.rsqrt(jnp.where(low, first, second) * (1.0 / HEAD_DIM) + EPS))
    return jnp.concatenate(slabs, axis=1)


def _inproj_kernel(x_ref, g_ref, w_ref, qg_ref, kg_ref,
                   glu_ref, q_ref, k_ref, v_ref, ga_ref, gb_ref):
    x = x_ref[...]
    xn = x * lax.rsqrt(jnp.mean(x * x, axis=-1, keepdims=True) + EPS) * g_ref[...]
    xb = xn.astype(_BF16)

    def seg(lo, width):
        return _dot(xb, w_ref[:, lo:lo + width])

    a = seg(0, D_CONV)
    b = seg(D_CONV, D_CONV)
    glu_ref[...] = a * jax.nn.sigmoid(b)
    off = 2 * D_CONV
    q = seg(off, Q_DIM)
    q_ref[...] = (q * _head_rms_scale(q) * qg_ref[...]).astype(q_ref.dtype)
    off += Q_DIM
    k = seg(off, KV_DIM)
    k_ref[...] = k * _head_rms_scale(k) * kg_ref[...]
    off += KV_DIM
    v_ref[...] = seg(off, KV_DIM)
    off += KV_DIM
    ga_ref[...] = jax.nn.sigmoid(seg(off, D_MODEL)).astype(ga_ref.dtype)
    off += D_MODEL
    gb_ref[...] = jax.nn.sigmoid(seg(off, D_MODEL)).astype(gb_ref.dtype)


def _inproj(x, g, w_in_b, qg, kg, q_dtype, tm):
    n = x.shape[0]
    row = lambda w: pl.BlockSpec((tm, w), lambda i: (i, 0))
    full = _resident
    return pl.pallas_call(
        _inproj_kernel,
        grid=(n // tm,),
        in_specs=[row(D_MODEL), full(g), full(w_in_b), full(qg), full(kg)],
        out_specs=[row(D_CONV), row(Q_DIM), row(KV_DIM), row(KV_DIM), row(D_MODEL), row(D_MODEL)],
        out_shape=[jax.ShapeDtypeStruct((n, D_CONV), _F32),
                   jax.ShapeDtypeStruct((n, Q_DIM), q_dtype),
                   jax.ShapeDtypeStruct((n, KV_DIM), _F32),
                   jax.ShapeDtypeStruct((n, KV_DIM), _F32),
                   jax.ShapeDtypeStruct((n, D_MODEL), _BF16),
                   jax.ShapeDtypeStruct((n, D_MODEL), _BF16)],
        compiler_params=pltpu.CompilerParams(dimension_semantics=("arbitrary",),
                                             vmem_limit_bytes=VMEM_LIMIT),
        name="inproj",
    )(x, g, w_in_b, qg, kg)


def _ln_swish_project(y, lng_ref, lnb_ref, wo_ref, bo_ref):
    mu = jnp.mean(y, axis=-1, keepdims=True)
    yc = y - mu
    var = jnp.mean(yc * yc, axis=-1, keepdims=True)
    z = yc * lax.rsqrt(var + EPS) * lng_ref[...] + lnb_ref[...]
    z = z * jax.nn.sigmoid(z)
    return (_dot(z.astype(_BF16), wo_ref[...]) + bo_ref[...]).astype(_BF16)


HALO = 32
CONV_STEPS = 16
CH_TILES = D_CONV // LANES


PIECE = 256
PAIR_TILES = 2 * CH_TILES


def _inproj_conv_kernel(x_ref, g_ref, w_ref, qg_ref, kg_ref, w16_ref, b16_ref, lng_ref, lnb_ref, wo_ref, bo_ref,
                        q_ref, k_ref, v_ref, ga_ref, gb_ref, conv_ref, tail_ref, hist_ref, y_ref, *, tm):
    i = pl.program_id(1)
    rows = 2 * tm

    @pl.when(i == 0)
    def _():
        hist_ref[0:HALO * CH_TILES, :] = jnp.zeros((HALO * CH_TILES, LANES), jnp.uint32)

    @pl.when(i > 0)
    def _():
        hist_ref[0:HALO * CH_TILES, :] = hist_ref[tm * CH_TILES:(tm + HALO) * CH_TILES, :]

    x = jnp.concatenate([x_ref[0], x_ref[1]], axis=0)
    xn = x * lax.rsqrt(jnp.mean(x * x, axis=-1, keepdims=True) + EPS) * g_ref[...]
    xb = xn.astype(_BF16)

    def seg(lo, width):
        return _dot(xb, w_ref[:, lo:lo + width])

    def put(ref, lo, val):
        ref[0, :, lo:lo + PIECE] = val[:tm].astype(ref.dtype)
        ref[1, :, lo:lo + PIECE] = val[tm:].astype(ref.dtype)

    for p in range(D_CONV // PIECE):
        lo = p * PIECE
        glu = seg(lo, PIECE) * jax.nn.sigmoid(seg(D_CONV + lo, PIECE))
        tail_ref[0, :, lo:lo + PIECE] = glu[tm - HALO:tm, :]
        tail_ref[1, :, lo:lo + PIECE] = glu[rows - HALO:, :]
        words = _pack_pair(glu[:tm], glu[tm:])
        for c in range(PIECE // LANES):
            tile = lo // LANES + c
            hist_ref[pl.ds(HALO * CH_TILES + tile, tm, stride=CH_TILES), :] = words[:, c * LANES:(c + 1) * LANES]

    first = HALO - (CONV_WIDTH - 1)

    def conv_chunk(ci, carry):
        t0 = ci * CONV_STEPS
        acc = jnp.zeros((CONV_STEPS, PAIR_TILES, LANES), _F32)
        for j in range(CONV_WIDTH):
            lo = pl.multiple_of((t0 + first + j) * CH_TILES, CH_TILES)
            xw = pltpu.bitcast(hist_ref[pl.ds(lo, CONV_STEPS * CH_TILES), :], _BF16)
            acc = acc + (xw.reshape(CONV_STEPS, PAIR_TILES, LANES).astype(_F32)
                         * w16_ref[j][None].astype(_F32))
        y_ref[pl.ds(pl.multiple_of(t0 * PAIR_TILES, PAIR_TILES), CONV_STEPS * PAIR_TILES), :] = (
            (acc + b16_ref[...][None]).reshape(CONV_STEPS * PAIR_TILES, LANES))
        return carry

    def q_piece(lo):
        q = seg(2 * D_CONV + lo, PIECE)
        put(q_ref, lo, q * _head_rms_scale(q) * qg_ref[:, lo:lo + PIECE])

    def k_piece(lo):
        k = seg(2 * D_CONV + Q_DIM + lo, PIECE)
        put(k_ref, lo, k * _head_rms_scale(k) * kg_ref[:, lo:lo + PIECE])

    def v_piece(lo):
        put(v_ref, lo, seg(2 * D_CONV + Q_DIM + KV_DIM + lo, PIECE))

    def gate_piece(ref, base, lo):
        put(ref, lo, jax.nn.sigmoid(seg(base + lo, PIECE)))

    gate_base = 2 * D_CONV + Q_DIM + 2 * KV_DIM
    pieces = ([functools.partial(q_piece, lo) for lo in range(0, Q_DIM, PIECE)]
              + [functools.partial(k_piece, lo) for lo in range(0, KV_DIM, PIECE)]
              + [functools.partial(v_piece, lo) for lo in range(0, KV_DIM, PIECE)]
              + [functools.partial(gate_piece, ga_ref, gate_base, lo) for lo in range(0, D_MODEL, PIECE)]
              + [functools.partial(gate_piece, gb_ref, gate_base + D_MODEL, lo) for lo in range(0, D_MODEL, PIECE)])

    for piece in pieces:
        piece()
    lax.fori_loop(0, tm // CONV_STEPS, conv_chunk, 0)

    y = jnp.concatenate(
        [jnp.concatenate([y_ref[pl.ds(2 * c + s, tm, stride=PAIR_TILES), :] for c in range(CH_TILES)], axis=1)
         for s in range(2)], axis=0)
    out = _ln_swish_project(y, lng_ref, lnb_ref, wo_ref, bo_ref)
    conv_ref[0] = out[:tm]
    conv_ref[1] = out[tm:]


def _inproj_conv(x, g, w_in_b, qg, kg, w_dw, b_dw, lng, lnb, wo_b, bo, tm):
    bsz, t, _ = x.shape
    assert CH_TILES == SUBLANES and bsz % 2 == 0 and t % tm == 0
    nt = t // tm
    w16 = jnp.repeat(w_dw.reshape(CONV_WIDTH, CH_TILES, LANES), 2, axis=1).astype(_BF16)
    b16 = jnp.repeat(b_dw.reshape(CH_TILES, LANES), 2, axis=0)
    full = _resident
    blk = lambda w: pl.BlockSpec((2, tm, w), lambda p, i: (p, i, 0))
    shape = lambda w, dt: jax.ShapeDtypeStruct((bsz, t, w), dt)
    outs = pl.pallas_call(
        functools.partial(_inproj_conv_kernel, tm=tm),
        grid=(bsz // 2, nt),
        in_specs=[blk(D_MODEL), full(g), full(w_in_b), full(qg), full(kg), full(w16), full(b16), full(lng), full(lnb),
                  full(wo_b), full(bo)],
        out_specs=[blk(Q_DIM), blk(KV_DIM), blk(KV_DIM), blk(D_MODEL), blk(D_MODEL), blk(D_MODEL),
                   pl.BlockSpec((2, HALO, D_CONV), lambda p, i: (p, 0, 0))],
        out_shape=[shape(Q_DIM, _BF16), shape(KV_DIM, _F32), shape(KV_DIM, _F32), shape(D_MODEL, _BF16),
                   shape(D_MODEL, _BF16), shape(D_MODEL, _BF16), jax.ShapeDtypeStruct((bsz, HALO, D_CONV), _F32)],
        scratch_shapes=[pltpu.VMEM(((tm + HALO) * CH_TILES, LANES), jnp.uint32),
                        pltpu.VMEM((tm * PAIR_TILES, LANES), _F32)],
        compiler_params=pltpu.CompilerParams(dimension_semantics=("arbitrary", "arbitrary"),
                                             vmem_limit_bytes=VMEM_LIMIT),
        name="inproj_conv",
    )(x, g, w_in_b, qg, kg, w16, b16, lng, lnb, wo_b, bo)
    return [o.reshape(bsz * t, o.shape[-1]) for o in outs[:-1]] + [outs[-1]]


def _conv_sample_kernel(state_ref, glu_ref, wdw_ref, bdw_ref, lng_ref, lnb_ref, wo_ref, bo_ref,
                        out_ref, state_out_ref):
    keep, steps = state_ref.shape[0], glu_ref.shape[0]

    def hist(u):
        return state_ref[u] if u < keep else glu_ref[u - keep]

    for t in range(steps):
        acc = hist(t) * wdw_ref[0:1, :]
        for j in range(1, CONV_WIDTH):
            acc = acc + hist(t + j) * wdw_ref[j:j + 1, :]
        out_ref[t] = _ln_swish_project(acc + bdw_ref[...], lng_ref, lnb_ref, wo_ref, bo_ref)
    state_out_ref[0:keep - steps] = state_ref[steps:keep]
    state_out_ref[keep - steps:keep] = glu_ref[...]


def _conv_sample(state_t, glu_t, w_dw, b_dw, lng, lnb, wo_b, bo, sb):
    keep, nseq, _ = state_t.shape
    steps = glu_t.shape[0]
    full = _resident
    blk = lambda r: pl.BlockSpec((r, sb, D_CONV), lambda i: (0, i, 0))
    return pl.pallas_call(
        _conv_sample_kernel,
        grid=(nseq // sb,),
        in_specs=[blk(keep), blk(steps), full(w_dw), full(b_dw), full(lng), full(lnb), full(wo_b), full(bo)],
        out_specs=[blk(steps), blk(keep)],
        out_shape=[jax.ShapeDtypeStruct((steps, nseq, D_MODEL), _BF16),
                   jax.ShapeDtypeStruct(state_t.shape, _F32)],
        compiler_params=pltpu.CompilerParams(dimension_semantics=("arbitrary",),
                                             vmem_limit_bytes=VMEM_LIMIT),
        name="conv_sample",
    )(state_t, glu_t, w_dw, b_dw, lng, lnb, wo_b, bo)


def _bucket_map():
    i = np.arange(WINDOW)[:, None]
    j = np.arange(WINDOW)[None, :]
    n = (i - j) % WINDOW
    nf = np.maximum(n, 1).astype(np.float32)
    large = MAX_EXACT + (np.log(nf / np.float32(MAX_EXACT)) / np.float32(math.log(MAX_DISTANCE / MAX_EXACT))
                         * np.float32(N_BUCKETS - MAX_EXACT)).astype(np.int32)
    return np.where(n < MAX_EXACT, n, np.minimum(large, N_BUCKETS - 1)).astype(np.int32)


def _bias_table_kernel(rb_ref, bm_ref, tbl_ref):
    p = pl.program_id(0)
    bm = bm_ref[...]
    for half in range(2):
        h = 2 * p + half
        t = jnp.zeros(bm.shape, _F32)
        for b in range(N_BUCKETS):
            t = jnp.where(bm == b, rb_ref[b, h], t)
        tbl_ref[0, :, half * WINDOW:(half + 1) * WINDOW] = t


def _bias_tables(rel_bias):
    bm = jnp.asarray(_bucket_map())
    return pl.pallas_call(
        _bias_table_kernel,
        grid=(N_PAIRS,),
        in_specs=[pl.BlockSpec(memory_space=pltpu.SMEM), pl.BlockSpec(bm.shape, lambda p: (0, 0))],
        out_specs=pl.BlockSpec((1, WINDOW, 2 * WINDOW), lambda p: (p, 0, 0)),
        out_shape=jax.ShapeDtypeStruct((N_PAIRS, WINDOW, 2 * WINDOW), _F32),
        name="bias_tables",
    )(rel_bias, bm)


def _block_diag_pairs(slab):
    low = lax.broadcasted_iota(jnp.int32, slab.shape, 1) < HEAD_DIM
    swapped = pltpu.roll(slab, HEAD_DIM, axis=1)
    zero = jnp.zeros_like(slab)
    first = jnp.concatenate([jnp.where(low, slab, zero), jnp.where(low, zero, swapped)], axis=0)
    second = jnp.concatenate([jnp.where(low, swapped, zero), jnp.where(low, zero, slab)], axis=0)
    return first.astype(_BF16), second.astype(_BF16)


def _kv_operands(k_blk, v_blk):
    ops = []
    for slab in range(KV_DIM // LANES):
        cols = slice(slab * LANES, (slab + 1) * LANES)
        ops.extend(zip(_block_diag_pairs(k_blk[:, cols]), _block_diag_pairs(v_blk[:, cols])))
    return ops


def _attend(q, prev_ops, own_ops, tbl_ref, sink_ref, prev_shift, store, transposed=False):
    tq = q.shape[0]
    rows = 2 * tq
    row = lax.broadcasted_iota(jnp.int32, (rows, 2 * WINDOW), 0)
    col = lax.broadcasted_iota(jnp.int32, (rows, 2 * WINDOW), 1)
    from_prev = (col & (WINDOW - 1)) > jnp.where(row >= tq, row - tq, row)
    top = lax.broadcasted_iota(jnp.int32, (rows, 1), 0) < tq
    low = lax.broadcasted_iota(jnp.int32, (rows, LANES), 1) < HEAD_DIM
    contract_last = (((1,), (1,)), ((), ()))

    def logits(a, k_op):
        return _dot(a, k_op) if transposed else lax.dot_general(a, k_op, contract_last, preferred_element_type=_F32)

    def weighted_values(pr, v_op):
        return lax.dot_general(pr, v_op, contract_last, preferred_element_type=_F32) if transposed else _dot(pr, v_op)

    for kvh in range(N_KV_HEADS):
        (k_prev, v_prev), (k_own, v_own) = prev_ops[kvh], own_ops[kvh]
        pair_a = 2 * kvh
        pair_b = pair_a + 1
        qq = jnp.concatenate([q[:, pair_a * LANES:(pair_a + 1) * LANES],
                              q[:, pair_b * LANES:(pair_b + 1) * LANES]], axis=0).astype(_BF16)
        sp = logits(qq, k_prev)
        so = logits(qq, k_own)
        bias = jnp.concatenate([tbl_ref[pair_a, 0:tq, :], tbl_ref[pair_b, 0:tq, :]], axis=0)
        s = jnp.where(from_prev, sp + prev_shift, so) + bias
        sink_even = jnp.where(top, sink_ref[2 * pair_a], sink_ref[2 * pair_b])
        sink_odd = jnp.where(top, sink_ref[2 * pair_a + 1], sink_ref[2 * pair_b + 1])
        m_even = jnp.maximum(jnp.max(s[:, :WINDOW], axis=-1, keepdims=True), sink_even)
        m_odd = jnp.maximum(jnp.max(s[:, WINDOW:], axis=-1, keepdims=True), sink_odd)
        e_even = jnp.exp(s[:, :WINDOW] - m_even)
        e_odd = jnp.exp(s[:, WINDOW:] - m_odd)
        p = jnp.concatenate([e_even, e_odd], axis=1).astype(_BF16)
        zero = jnp.zeros_like(p)
        o = (weighted_values(jnp.where(from_prev, p, zero), v_prev)
             + weighted_values(jnp.where(from_prev, zero, p), v_own))
        den_even = jnp.sum(e_even, axis=-1, keepdims=True) + jnp.exp(sink_even - m_even)
        den_odd = jnp.sum(e_odd, axis=-1, keepdims=True) + jnp.exp(sink_odd - m_odd)
        o = o * jnp.where(low, 1.0 / den_even, 1.0 / den_odd)
        store(pair_a, o[:tq])
        store(pair_b, o[tq:])


PROMPT_QBLOCKS = 4


def _attn_prompt_kernel(sink_ref, q_ref, kp_ref, ko_ref, vp_ref, vo_ref, tbl_ref, o_ref):
    prev_shift = jnp.where(pl.program_id(1) == 0, MASK_VALUE, 0.0).astype(_F32)
    ops = [_kv_operands(kp_ref[...], vp_ref[...])]
    for b in range(PROMPT_QBLOCKS):
        rows = slice(b * WINDOW, (b + 1) * WINDOW)
        ops.append(_kv_operands(ko_ref[rows, :], vo_ref[rows, :]))

        def store(pair, o, rows=rows):
            o_ref[rows, pair * LANES:(pair + 1) * LANES] = o.astype(o_ref.dtype)

        _attend(q_ref[rows, :], ops[b], ops[b + 1], tbl_ref, sink_ref,
                prev_shift if b == 0 else jnp.float32(0.0), store)


def _attn_prompt(q, k, v, tbl, sinks, bsz, t):
    tq = PROMPT_QBLOCKS * WINDOW
    nb = t // tq
    own = lambda w: pl.BlockSpec((tq, w), lambda b, i: (b * nb + i, 0))
    prev = lambda w: pl.BlockSpec((WINDOW, w),
                                  lambda b, i: (PROMPT_QBLOCKS * (b * nb + i) - jnp.minimum(i, 1), 0))
    return pl.pallas_call(
        _attn_prompt_kernel,
        grid=(bsz, nb),
        in_specs=[pl.BlockSpec(memory_space=pltpu.SMEM), own(Q_DIM), prev(KV_DIM), own(KV_DIM),
                  prev(KV_DIM), own(KV_DIM), pl.BlockSpec(tbl.shape, lambda b, i: (0, 0, 0))],
        out_specs=own(Q_DIM),
        out_shape=jax.ShapeDtypeStruct((bsz * t, Q_DIM), _BF16),
        compiler_params=pltpu.CompilerParams(dimension_semantics=("arbitrary", "arbitrary"),
                                             vmem_limit_bytes=VMEM_LIMIT),
        name="attn_prompt",
    )(sinks, q, k, k, v, v, tbl)


def _block_diag_t(x):
    xb = x.astype(_BF16)
    z = jnp.zeros_like(xb)
    return jnp.concatenate([jnp.concatenate([xb, z], axis=1), jnp.concatenate([z, xb], axis=1)], axis=0)


def _attn_sample_kernel(sink_ref, q_ref, kn_ref, vn_ref, ck_ref, cv_ref, tbl_ref, o_ref, cko_ref, cvo_ref,
                        *, sb, steps):
    pad = jnp.zeros((WINDOW - steps, LANES), _F32)
    lane = lax.broadcasted_iota(jnp.int32, (HEAD_DIM, WINDOW), 1)

    def one_sequence(s):
        def store(pair, o):
            o_ref[s, :, pair * LANES:(pair + 1) * LANES] = o

        prev_ops, own_ops = [], []
        for slab in range(KV_DIM // LANES):
            cols = slice(slab * LANES, (slab + 1) * LANES)
            new_k = jnp.concatenate([kn_ref[s][:, cols], pad], axis=0).T
            new_v = jnp.concatenate([vn_ref[s][:, cols], pad], axis=0).T
            for sub in range(2):
                kvh = 2 * slab + sub
                part = slice(sub * HEAD_DIM, (sub + 1) * HEAD_DIM)
                kt, vt = ck_ref[s, kvh], cv_ref[s, kvh]
                cko_ref[s, kvh] = pltpu.roll(jnp.where(lane < steps, new_k[part], kt), WINDOW - steps, axis=1)
                cvo_ref[s, kvh] = pltpu.roll(jnp.where(lane < steps, new_v[part], vt), WINDOW - steps, axis=1)
                prev_ops.append((_block_diag_t(kt), _block_diag_t(vt)))
                own_ops.append((_block_diag_t(new_k[part]), _block_diag_t(new_v[part])))
        _attend(q_ref[s], prev_ops, own_ops, tbl_ref, sink_ref, jnp.float32(0.0), store, transposed=True)

    for s in range(sb):
        one_sequence(s)


def _attn_sample(q, k_new, v_new, cache_kt, cache_vt, tbl, sinks, sb):
    nseq, steps, _ = q.shape
    seq = lambda r, w: pl.BlockSpec((sb, r, w), lambda i: (i, 0, 0))
    cache = pl.BlockSpec((sb, N_KV_HEADS, HEAD_DIM, WINDOW), lambda i: (i, 0, 0, 0))
    return pl.pallas_call(
        functools.partial(_attn_sample_kernel, sb=sb, steps=steps),
        grid=(nseq // sb,),
        in_specs=[pl.BlockSpec(memory_space=pltpu.SMEM), seq(steps, Q_DIM), seq(steps, KV_DIM), seq(steps, KV_DIM),
                  cache, cache, pl.BlockSpec(tbl.shape, lambda i: (0, 0, 0))],
        out_specs=[seq(steps, Q_DIM), cache, cache],
        out_shape=[jax.ShapeDtypeStruct((nseq, steps, Q_DIM), _F32),
                   jax.ShapeDtypeStruct(cache_kt.shape, _F32), jax.ShapeDtypeStruct(cache_vt.shape, _F32)],
        compiler_params=pltpu.CompilerParams(dimension_semantics=("arbitrary",),
                                             vmem_limit_bytes=VMEM_LIMIT),
        name="attn_sample",
    )(sinks, q, k_new, v_new, cache_kt, cache_vt, tbl)


def _lane_min_index(mask, lane):
    return jnp.min(jnp.where(mask, lane, float(LANES)), axis=-1, keepdims=True)


def _finish_kernel(x_ref, conv_ref, o_ref, ga_ref, gb_ref, wa_ref, wo_ref, ng_ref, wr_ref, br_ref,
                   tri_ref, h_ref, hn_ref, route_ref, count_ref, running_ref):
    @pl.when(pl.program_id(0) == 0)
    def _():
        running_ref[...] = jnp.zeros_like(running_ref)

    attn_out = _dot(o_ref[...].astype(_BF16), wa_ref[...])
    merged = ga_ref[...].astype(_F32) * conv_ref[...].astype(_F32) + gb_ref[...].astype(_F32) * attn_out
    h = x_ref[...] + _dot(merged.astype(_BF16), wo_ref[...])
    h_ref[...] = h
    hn = h * lax.rsqrt(jnp.mean(h * h, axis=-1, keepdims=True) + EPS) * ng_ref[...]
    hn_ref[...] = _pack_bf16_pairs(hn)

    logits = _dot(hn.astype(_BF16), wr_ref[...]) + br_ref[...]
    lane = lax.broadcasted_iota(jnp.int32, logits.shape, 1).astype(_F32)
    gmask = lane < N_GROUPS
    gl = jnp.where(gmask, logits, MASK_VALUE)
    gmax = jnp.max(gl, axis=-1, keepdims=True)
    grp = _lane_min_index(gmask & (gl == gmax), lane)
    p_grp = 1.0 / jnp.sum(jnp.where(gmask, jnp.exp(gl - gmax), 0.0), axis=-1, keepdims=True)
    e_lo = N_GROUPS + grp * EXPERTS_PER_GROUP
    emask = (lane >= e_lo) & (lane < e_lo + EXPERTS_PER_GROUP)
    el = jnp.where(emask, logits, MASK_VALUE)
    ex = jnp.where(emask, jnp.exp(el - jnp.max(el, axis=-1, keepdims=True)), 0.0)
    prob = jnp.where(emask, ex / jnp.sum(ex, axis=-1, keepdims=True), -1.0)
    p1 = jnp.max(prob, axis=-1, keepdims=True)
    i1 = _lane_min_index(prob == p1, lane)
    rest = jnp.where(lane == i1, -1.0, prob)
    p2 = jnp.max(rest, axis=-1, keepdims=True)
    i2 = _lane_min_index(rest == p2, lane)
    w1 = p_grp * p1 / (p1 + p2)
    w2 = p_grp * p2 / (p1 + p2)
    e1 = i1 - N_GROUPS
    e2 = i2 - N_GROUPS

    hot1 = lane == e1
    hot2 = lane == e2
    hot = jnp.where(hot1 | hot2, 1.0, 0.0)
    before = _dot(tri_ref[...], hot.astype(_BF16)) + running_ref[...]
    rank1 = jnp.sum(jnp.where(hot1, before, 0.0), axis=-1, keepdims=True)
    rank2 = jnp.sum(jnp.where(hot2, before, 0.0), axis=-1, keepdims=True)
    running_ref[...] += jnp.sum(hot, axis=0, keepdims=True)
    count_ref[...] = jnp.broadcast_to(running_ref[...], count_ref.shape)

    fields = (e1, e2, w1, w2, rank1, rank2)
    route = jnp.zeros(logits.shape, _F32)
    for pos, val in enumerate(fields):
        route = jnp.where(lane == pos, val, route)
    route_ref[...] = route


ROUTE_E, ROUTE_W, ROUTE_RANK = 0, 2, 4
DEST_ROWS = 1024


def _finish(x, conv_out, o, ga, gb, wa_b, wo_b, ng, wr_b, br, tm):
    n = x.shape[0]
    tri = jnp.asarray(np.tril(np.ones((tm, tm), np.float32), -1), _BF16)
    row = lambda w: pl.BlockSpec((tm, w), lambda i: (i, 0))
    full = _resident
    return pl.pallas_call(
        _finish_kernel,
        grid=(n // tm,),
        in_specs=[row(D_MODEL), row(D_MODEL), row(Q_DIM), row(D_MODEL), row(D_MODEL),
                  full(wa_b), full(wo_b), full(ng), full(wr_b), full(br), full(tri)],
        out_specs=[row(D_MODEL), row(HALF), row(LANES), pl.BlockSpec((SUBLANES, LANES), lambda i: (0, 0))],
        out_shape=[jax.ShapeDtypeStruct((n, D_MODEL), _F32),
                   jax.ShapeDtypeStruct((n, HALF), jnp.uint32),
                   jax.ShapeDtypeStruct((n, LANES), _F32),
                   jax.ShapeDtypeStruct((SUBLANES, LANES), _F32)],
        scratch_shapes=[pltpu.VMEM((1, LANES), _F32)],
        compiler_params=pltpu.CompilerParams(dimension_semantics=("arbitrary",),
                                             vmem_limit_bytes=VMEM_LIMIT),
        name="finish",
    )(x, conv_out, o, ga, gb, wa_b, wo_b, ng, wr_b, br, tri)


def _dest_kernel(route_ref, starts_ref, dest_ref):
    route = route_ref[...]
    lane = lax.broadcasted_iota(jnp.int32, route.shape, 1)
    out = jnp.zeros(route.shape, jnp.int32)
    for j in range(TOP_K):
        e = route[:, ROUTE_E + j:ROUTE_E + j + 1].astype(jnp.int32)
        start = jnp.sum(jnp.where(lane == e, starts_ref[...], 0.0), axis=-1, keepdims=True)
        d = (start + route[:, ROUTE_RANK + j:ROUTE_RANK + j + 1]).astype(jnp.int32)
        out = jnp.where(lane == j, d, out)
    dest_ref[...] = out


def _dest(route, starts_row, tm):
    n = route.shape[0]
    row = pl.BlockSpec((tm, LANES), lambda i: (i, 0))
    return pl.pallas_call(
        _dest_kernel,
        grid=(n // tm,),
        in_specs=[row, pl.BlockSpec((1, LANES), lambda i: (0, 0))],
        out_specs=row,
        out_shape=jax.ShapeDtypeStruct((n, LANES), jnp.int32),
        name="dest",
    )(route, starts_row)


SC_CORES = 2
SC_SUBCORES = 16
SC_WORKERS = SC_CORES * SC_SUBCORES
SC_IN_FLIGHT = 4
SC_CHUNK_BYTES = 64 * 1024


def _sc_move_rows(src, idx, gather):
    n, d = src.shape
    b = idx.shape[0]
    per_worker = b // SC_WORKERS
    assert per_worker * SC_WORKERS == b and (gather or n % per_worker == 0), (b, n)
    chunk = min(per_worker // SC_IN_FLIGHT, SC_CHUNK_BYTES // (d * 4))
    n_iters = per_worker // (chunk * SC_IN_FLIGHT)
    assert n_iters * chunk * SC_IN_FLIGHT == per_worker and chunk % SUBLANES == 0, (per_worker, chunk)
    mesh = plsc.VectorSubcoreMesh(core_axis_name="c", subcore_axis_name="s",
                                  num_cores=SC_CORES, num_subcores=SC_SUBCORES)
    scratch = ([pltpu.VMEM((chunk,), jnp.int32)] * SC_IN_FLIGHT + [pltpu.VMEM((chunk, d), src.dtype)] * SC_IN_FLIGHT
               + [pltpu.SemaphoreType.DMA] * SC_IN_FLIGHT)

    @functools.partial(pl.kernel, mesh=mesh, out_type=jax.ShapeDtypeStruct((b, d), src.dtype),
                       scratch_types=scratch, name="sc_gather_rows" if gather else "sc_scatter_rows",
                       cost_estimate=pl.CostEstimate(flops=0, transcendentals=0, bytes_accessed=2 * b * d * 4 + b * 4))
    def move(src_hbm, idx_hbm, out_hbm, *bufs):
        idx_v = bufs[:SC_IN_FLIGHT]
        rows_v = bufs[SC_IN_FLIGHT:2 * SC_IN_FLIGHT]
        sems = bufs[2 * SC_IN_FLIGHT:]
        worker = lax.axis_index("s") * SC_CORES + lax.axis_index("c")

        @pl.loop(0, n_iters)
        def _(it):
            bases = [pl.multiple_of(worker * per_worker + (it * SC_IN_FLIGHT + j) * chunk, chunk)
                     for j in range(SC_IN_FLIGHT)]
            loads = [pltpu.async_copy(idx_hbm.at[pl.ds(bases[j], chunk)], idx_v[j], sems[j])
                     for j in range(SC_IN_FLIGHT)]
            reads = []
            for j in range(SC_IN_FLIGHT):
                loads[j].wait()
                if gather:
                    rows = src_hbm.at[idx_v[j]]
                else:
                    rows = src_hbm.at[pl.ds(pl.multiple_of(lax.rem(bases[j], n), chunk), chunk)]
                reads.append(pltpu.async_copy(rows, rows_v[j], sems[j]))
            writes = []
            for j in range(SC_IN_FLIGHT):
                reads[j].wait()
                dst = out_hbm.at[pl.ds(bases[j], chunk)] if gather else out_hbm.at[idx_v[j]]
                writes.append(pltpu.async_copy(rows_v[j], dst, sems[j]))
            for w in writes:
                w.wait()

    return move(src, idx)


def _sc_gather_rows(table, idx):
    return _sc_move_rows(table, idx, gather=True)


def _sc_scatter_rows(src, idx):
    return _sc_move_rows(src, idx, gather=False)


def _expert_kernel(blk_ref, exp_ref, lo_ref, hi_ref, x_ref, wg_ref, wu_ref, wd_ref, yb_ref):
    del blk_ref, exp_ref
    k = pl.program_id(0)
    lo, hi = lo_ref[k], hi_ref[k]

    @pl.when(hi > lo)
    def _():
        x = _unpack_bf16_pairs(x_ref[...])
        g = _dot(x, wg_ref[0])
        u = _dot(x, wu_ref[0])
        hid = g * jax.nn.sigmoid(g) * u
        y = _dot(hid, wd_ref[0])
        r = lax.broadcasted_iota(jnp.int32, yb_ref.shape, 0)
        pltpu.store(yb_ref, _pack_bf16_pairs(y), mask=(r >= lo) & (r < hi))


def _experts(items, xs, w_gate, w_up, w_down):
    n_items = items[0].shape[0]
    wspec = lambda a: pl.BlockSpec((1,) + a.shape[1:], lambda k, blk, exp, lo, hi: (exp[k], 0, 0))
    rows = pl.BlockSpec((MOE_ROWS, HALF), lambda k, blk, exp, lo, hi: (blk[k], 0))
    grid_spec = pltpu.PrefetchScalarGridSpec(
        num_scalar_prefetch=4,
        grid=(n_items,),
        in_specs=[rows, wspec(w_gate), wspec(w_up), wspec(w_down)],
        out_specs=rows,
    )
    return pl.pallas_call(
        _expert_kernel,
        grid_spec=grid_spec,
        out_shape=jax.ShapeDtypeStruct(xs.shape, xs.dtype),
        compiler_params=pltpu.CompilerParams(dimension_semantics=("arbitrary",),
                                             vmem_limit_bytes=VMEM_LIMIT),
        cost_estimate=pl.CostEstimate(
            flops=n_items * MOE_ROWS * 6 * D_MODEL * D_EXPERT, transcendentals=n_items * MOE_ROWS * D_EXPERT,
            bytes_accessed=2 * xs.size * 4 + (w_gate.size + w_up.size + w_down.size) * 4),
        name="experts",
    )(*items, xs, w_gate, w_up, w_down)


def _combine_kernel(h_ref, route_ref, g0_ref, g1_ref, y_ref):
    route = route_ref[...]
    y_ref[...] = (h_ref[...] + route[:, ROUTE_W:ROUTE_W + 1] * _unpack_bf16_pairs(g0_ref[...])
                  + route[:, ROUTE_W + 1:ROUTE_W + 2] * _unpack_bf16_pairs(g1_ref[...]))


def _combine(h, route, g, tm):
    n = h.shape[0]
    nt = n // tm
    row = lambda w: pl.BlockSpec((tm, w), lambda i: (i, 0))
    return pl.pallas_call(
        _combine_kernel,
        grid=(nt,),
        in_specs=[row(D_MODEL), row(LANES), row(HALF), pl.BlockSpec((tm, HALF), lambda i: (nt + i, 0))],
        out_specs=row(D_MODEL),
        out_shape=jax.ShapeDtypeStruct((n, D_MODEL), _F32),
        compiler_params=pltpu.CompilerParams(dimension_semantics=("arbitrary",),
                                             vmem_limit_bytes=VMEM_LIMIT),
        name="combine",
    )(h, route, g, g)


def _work_items(counts, n_pairs):
    n_blocks = n_pairs // MOE_ROWS
    starts = jnp.cumsum(counts) - counts
    cuts = jnp.sort(jnp.concatenate([jnp.arange(n_blocks, dtype=jnp.int32) * MOE_ROWS, starts]))
    ends = jnp.concatenate([cuts[1:], jnp.full((1,), n_pairs, jnp.int32)])
    blk = jnp.minimum(cuts // MOE_ROWS, n_blocks - 1)
    expert = jnp.clip(jnp.sum(starts[None, :] <= cuts[:, None], axis=1) - 1, 0, N_EXPERTS - 1).astype(jnp.int32)
    return starts, (blk, expert, cuts - blk * MOE_ROWS, ends - blk * MOE_ROWS)


def _dispatch(hn, route, counts_rows):
    n = hn.shape[0]
    n_pairs = n * TOP_K
    counts = counts_rows[0, :N_EXPERTS].astype(jnp.int32)
    starts, items = _work_items(counts, n_pairs)
    starts_row = jnp.zeros((1, LANES), _F32).at[0, :N_EXPERTS].set(starts.astype(_F32))
    dest = _dest(route, starts_row, min(n, DEST_ROWS))[:, :TOP_K].T.reshape(n_pairs)
    return _sc_scatter_rows(hn, dest), dest, items


def _after(value, *earlier):
    return lax.optimization_barrier((value,) + earlier)[0]


def kernel(x_prompt, x_sample, state_conv, cache_k, cache_v, norm_attn_g, w_in, q_norm_g, k_norm_g, rel_bias, attn_sinks, w_dw, b_dw, conv_ln_g, conv_ln_b, w_conv_out, b_conv_out, w_attn_out, w_out, norm_ffn_g, w_grp, b_grp, w_router, b_router, w_gate, w_up, w_down):
    bsz, t, _ = x_prompt.shape
    nseq, steps, _ = x_sample.shape
    row = lambda a: a.reshape(1, -1).astype(_F32)

    w_in_b = w_in.astype(_BF16)
    wco_b = w_conv_out.astype(_BF16)
    wa_b = w_attn_out.astype(_BF16)
    wo_b = w_out.astype(_BF16)
    qg = row(jnp.tile(q_norm_g, N_HEADS)) * (HEAD_DIM ** -0.5)
    kg = row(jnp.tile(k_norm_g, N_KV_HEADS))
    w_rt = jnp.zeros((D_MODEL, LANES), _F32).at[:, :N_GROUPS].set(w_grp).at[:, N_GROUPS:N_GROUPS + N_EXPERTS].set(w_router)
    wr_b = w_rt.astype(_BF16)
    b_rt = jnp.zeros((1, LANES), _F32).at[0, :N_GROUPS].set(b_grp).at[0, N_GROUPS:N_GROUPS + N_EXPERTS].set(b_router)
    tbl = _bias_tables(rel_bias)
    conv_params = (w_dw, row(b_dw), row(conv_ln_g), row(conv_ln_b), wco_b, row(b_conv_out))

    def finish(x2d, conv_out, o, ga, gb):
        return _finish(x2d, conv_out, o, ga, gb, wa_b, wo_b, row(norm_ffn_g), wr_b, b_rt, ROW_TILE)

    xp = x_prompt.reshape(bsz * t, D_MODEL)
    q, k, v, ga, gb, conv_out, glu_tail = _inproj_conv(x_prompt, row(norm_attn_g), w_in_b, qg, kg, *conv_params,
                                                       ROW_TILE // 2)
    o = _attn_prompt(q, k, v, tbl, attn_sinks, bsz, t)
    h_p, hn_p, route_p, counts_p = finish(xp, conv_out, o, ga, gb)
    rows_p, dest_p, items_p = _dispatch(hn_p, route_p, counts_p)
    state_conv_prompt = glu_tail[:, HALO - (CONV_WIDTH - 1):]
    tail = lambda a: a.reshape(bsz, t, KV_DIM)[:, t - WINDOW:].reshape(bsz, WINDOW, N_KV_HEADS, HEAD_DIM)
    cache_k_prompt, cache_v_prompt = tail(k), tail(v)

    xs = _after(x_sample, dest_p).reshape(nseq * steps, D_MODEL)
    glu, q, k, v, ga, gb = _inproj(xs, row(norm_attn_g), w_in_b, qg, kg, _F32, ROW_TILE)
    glu_t = glu.reshape(nseq, steps, D_CONV).transpose(1, 0, 2)
    conv_out, state_t = _conv_sample(state_conv.transpose(1, 0, 2), glu_t, *conv_params, SAMPLE_CONV_SEQS)
    conv_out = conv_out.transpose(1, 0, 2).reshape(nseq * steps, D_MODEL)
    k3 = k.reshape(nseq, steps, KV_DIM)
    v3 = v.reshape(nseq, steps, KV_DIM)
    o, ck_t, cv_t = _attn_sample(q.reshape(nseq, steps, Q_DIM), k3, v3, cache_k.transpose(0, 2, 3, 1),
                                 cache_v.transpose(0, 2, 3, 1), tbl, attn_sinks, SAMPLE_ATTN_SEQS)
    h_s, hn_s, route_s, counts_s = finish(xs, conv_out, o.reshape(nseq * steps, Q_DIM), ga, gb)
    rows_s, dest_s, items_s = _dispatch(hn_s, route_s, counts_s)
    state_conv_sample = state_t.transpose(1, 0, 2)
    cache_k_sample = ck_t.transpose(0, 3, 1, 2)
    cache_v_sample = cv_t.transpose(0, 3, 1, 2)

    yb_p = _experts(items_p, rows_p, w_gate, w_up, w_down)
    g_p = _sc_gather_rows(yb_p, dest_p)
    yb_s = _experts(items_s, _after(rows_s, yb_p), w_gate, w_up, w_down)
    g_s = _sc_gather_rows(yb_s, dest_s)
    y_prompt = _combine(h_p, route_p, g_p, ROW_TILE).reshape(bsz, t, D_MODEL)
    y_sample = _combine(h_s, route_s, _after(g_s, y_prompt), ROW_TILE).reshape(nseq, steps, D_MODEL)

    return (y_prompt, y_sample, state_conv_prompt, cache_k_prompt, cache_v_prompt,
            state_conv_sample, cache_k_sample, cache_v_sample)
```

```python
import functools
import math

import numpy as np
import jax
import jax.numpy as jnp
from jax import lax
from jax.experimental import pallas as pl
from jax.experimental.pallas import tpu as pltpu
from jax.experimental.pallas import tpu_sc as plsc

D_MODEL = 1024
N_HEADS = 16
HEAD_DIM = 64
N_KV_HEADS = 4
WINDOW = 128
Q_DIM = N_HEADS * HEAD_DIM
KV_DIM = N_KV_HEADS * HEAD_DIM
N_BUCKETS = 32
MAX_EXACT = N_BUCKETS // 2
MAX_DISTANCE = 128
D_CONV = D_MODEL
CONV_WIDTH = 31
N_GROUPS = 4
EXPERTS_PER_GROUP = 8
N_EXPERTS = N_GROUPS * EXPERTS_PER_GROUP
TOP_K = 2
D_EXPERT = 256
EPS = 1e-6

LANES = 128
SUBLANES = 8
N_PAIRS = N_HEADS // 2
MOE_ROWS = 512
MASK_VALUE = -1e30
VMEM_LIMIT = 56 * 1024 * 1024
ROW_TILE = 512
SAMPLE_CONV_SEQS = 64
SAMPLE_ATTN_SEQS = 8

_F32 = jnp.float32
_BF16 = jnp.bfloat16


def _resident(a):
    return pl.BlockSpec(a.shape, lambda *_: (0,) * a.ndim, pipeline_mode=pl.Buffered(1))


def _dot(a, b):
    return jnp.dot(a, b, preferred_element_type=_F32)


HALF = D_MODEL // 2


def _pack_pair(lo, hi):
    lo_bits = pltpu.bitcast(lo.astype(_BF16).astype(_F32), jnp.uint32)
    hi_bits = pltpu.bitcast(hi.astype(_BF16).astype(_F32), jnp.uint32)
    return hi_bits | (lo_bits >> 16)


def _pack_bf16_pairs(x):
    return _pack_pair(x[:, :HALF], x[:, HALF:])


def _unpack_bf16_pairs(w):
    lo = pltpu.bitcast(w << 16, _F32)
    hi = pltpu.bitcast(w & jnp.uint32(0xFFFF0000), _F32)
    return jnp.concatenate([lo, hi], axis=1)


def _head_rms_scale(z):
    low = lax.broadcasted_iota(jnp.int32, (z.shape[0], LANES), 1) < HEAD_DIM
    slabs = []
    for c in range(z.shape[1] // LANES):
        sq = z[:, c * LANES:(c + 1) * LANES]
        sq = sq * sq
        first = jnp.sum(jnp.where(low, sq, 0.0), axis=-1, keepdims=True)
        second = jnp.sum(jnp.where(low, 0.0, sq), axis=-1, keepdims=True)
        slabs.append(lax.rsqrt(jnp.where(low, first, second) * (1.0 / HEAD_DIM) + EPS))
    return jnp.concatenate(slabs, axis=1)


def _inproj_kernel(x_ref, g_ref, w_ref, qg_ref, kg_ref,
                   glu_ref, q_ref, k_ref, v_ref, ga_ref, gb_ref):
    x = x_ref[...]
    xn = x * lax.rsqrt(jnp.mean(x * x, axis=-1, keepdims=True) + EPS) * g_ref[...]
    xb = xn.astype(_BF16)

    def seg(lo, width):
        return _dot(xb, w_ref[:, lo:lo + width])

    a = seg(0, D_CONV)
    b = seg(D_CONV, D_CONV)
    glu_ref[...] = a * jax.nn.sigmoid(b)
    off = 2 * D_CONV
    q = seg(off, Q_DIM)
    q_ref[...] = (q * _head_rms_scale(q) * qg_ref[...]).astype(q_ref.dtype)
    off += Q_DIM
    k = seg(off, KV_DIM)
    k_ref[...] = k * _head_rms_scale(k) * kg_ref[...]
    off += KV_DIM
    v_ref[...] = seg(off, KV_DIM)
    off += KV_DIM
    ga_ref[...] = jax.nn.sigmoid(seg(off, D_MODEL)).astype(ga_ref.dtype)
    off += D_MODEL
    gb_ref[...] = jax.nn.sigmoid(seg(off, D_MODEL)).astype(gb_ref.dtype)


def _inproj(x, g, w_in_b, qg, kg, q_dtype, tm):
    n = x.shape[0]
    row = lambda w: pl.BlockSpec((tm, w), lambda i: (i, 0))
    full = _resident
    return pl.pallas_call(
        _inproj_kernel,
        grid=(n // tm,),
        in_specs=[row(D_MODEL), full(g), full(w_in_b), full(qg), full(kg)],
        out_specs=[row(D_CONV), row(Q_DIM), row(KV_DIM), row(KV_DIM), row(D_MODEL), row(D_MODEL)],
        out_shape=[jax.ShapeDtypeStruct((n, D_CONV), _F32),
                   jax.ShapeDtypeStruct((n, Q_DIM), q_dtype),
                   jax.ShapeDtypeStruct((n, KV_DIM), _F32),
                   jax.ShapeDtypeStruct((n, KV_DIM), _F32),
                   jax.ShapeDtypeStruct((n, D_MODEL), _BF16),
                   jax.ShapeDtypeStruct((n, D_MODEL), _BF16)],
        compiler_params=pltpu.CompilerParams(dimension_semantics=("arbitrary",),
                                             vmem_limit_bytes=VMEM_LIMIT),
        name="inproj",
    )(x, g, w_in_b, qg, kg)


def _ln_swish_project(y, lng_ref, lnb_ref, wo_ref, bo_ref):
    mu = jnp.mean(y, axis=-1, keepdims=True)
    yc = y - mu
    var = jnp.mean(yc * yc, axis=-1, keepdims=True)
    z = yc * lax.rsqrt(var + EPS) * lng_ref[...] + lnb_ref[...]
    z = z * jax.nn.sigmoid(z)
    return (_dot(z.astype(_BF16), wo_ref[...]) + bo_ref[...]).astype(_BF16)


HALO = 32
CONV_STEPS = 16
CH_TILES = D_CONV // LANES


PIECE = 256
PAIR_TILES = 2 * CH_TILES


def _inproj_conv_kernel(x_ref, g_ref, w_ref, qg_ref, kg_ref, w16_ref, b16_ref, lng_ref, lnb_ref, wo_ref, bo_ref,
                        q_ref, k_ref, v_ref, ga_ref, gb_ref, conv_ref, tail_ref, hist_ref, y_ref, *, tm):
    i = pl.program_id(1)
    rows = 2 * tm

    @pl.when(i == 0)
    def _():
        hist_ref[0:HALO * CH_TILES, :] = jnp.zeros((HALO * CH_TILES, LANES), jnp.uint32)

    @pl.when(i > 0)
    def _():
        hist_ref[0:HALO * CH_TILES, :] = hist_ref[tm * CH_TILES:(tm + HALO) * CH_TILES, :]

    x = jnp.concatenate([x_ref[0], x_ref[1]], axis=0)
    xn = x * lax.rsqrt(jnp.mean(x * x, axis=-1, keepdims=True) + EPS) * g_ref[...]
    xb = xn.astype(_BF16)

    def seg(lo, width):
        return _dot(xb, w_ref[:, lo:lo + width])

    def put(ref, lo, val):
        ref[0, :, lo:lo + PIECE] = val[:tm].astype(ref.dtype)
        ref[1, :, lo:lo + PIECE] = val[tm:].astype(ref.dtype)

    for p in range(D_CONV // PIECE):
        lo = p * PIECE
        glu = seg(lo, PIECE) * jax.nn.sigmoid(seg(D_CONV + lo, PIECE))
        tail_ref[0, :, lo:lo + PIECE] = glu[tm - HALO:tm, :]
        tail_ref[1, :, lo:lo + PIECE] = glu[rows - HALO:, :]
        words = _pack_pair(glu[:tm], glu[tm:])
        for c in range(PIECE // LANES):
            tile = lo // LANES + c
            hist_ref[pl.ds(HALO * CH_TILES + tile, tm, stride=CH_TILES), :] = words[:, c * LANES:(c + 1) * LANES]

    first = HALO - (CONV_WIDTH - 1)

    def conv_chunk(ci, carry):
        t0 = ci * CONV_STEPS
        acc = jnp.zeros((CONV_STEPS, PAIR_TILES, LANES), _F32)
        for j in range(CONV_WIDTH):
            lo = pl.multiple_of((t0 + first + j) * CH_TILES, CH_TILES)
            xw = pltpu.bitcast(hist_ref[pl.ds(lo, CONV_STEPS * CH_TILES), :], _BF16)
            acc = acc + (xw.reshape(CONV_STEPS, PAIR_TILES, LANES).astype(_F32)
                         * w16_ref[j][None].astype(_F32))
        y_ref[pl.ds(pl.multiple_of(t0 * PAIR_TILES, PAIR_TILES), CONV_STEPS * PAIR_TILES), :] = (
            (acc + b16_ref[...][None]).reshape(CONV_STEPS * PAIR_TILES, LANES))
        return carry

    def q_piece(lo):
        q = seg(2 * D_CONV + lo, PIECE)
        put(q_ref, lo, q * _head_rms_scale(q) * qg_ref[:, lo:lo + PIECE])

    def k_piece(lo):
        k = seg(2 * D_CONV + Q_DIM + lo, PIECE)
        put(k_ref, lo, k * _head_rms_scale(k) * kg_ref[:, lo:lo + PIECE])

    def v_piece(lo):
        put(v_ref, lo, seg(2 * D_CONV + Q_DIM + KV_DIM + lo, PIECE))

    def gate_piece(ref, base, lo):
        put(ref, lo, jax.nn.sigmoid(seg(base + lo, PIECE)))

    gate_base = 2 * D_CONV + Q_DIM + 2 * KV_DIM
    pieces = ([functools.partial(q_piece, lo) for lo in range(0, Q_DIM, PIECE)]
              + [functools.partial(k_piece, lo) for lo in range(0, KV_DIM, PIECE)]
              + [functools.partial(v_piece, lo) for lo in range(0, KV_DIM, PIECE)]
              + [functools.partial(gate_piece, ga_ref, gate_base, lo) for lo in range(0, D_MODEL, PIECE)]
              + [functools.partial(gate_piece, gb_ref, gate_base + D_MODEL, lo) for lo in range(0, D_MODEL, PIECE)])

    for piece in pieces:
        piece()
    lax.fori_loop(0, tm // CONV_STEPS, conv_chunk, 0)

    y = jnp.concatenate(
        [jnp.concatenate([y_ref[pl.ds(2 * c + s, tm, stride=PAIR_TILES), :] for c in range(CH_TILES)], axis=1)
         for s in range(2)], axis=0)
    out = _ln_swish_project(y, lng_ref, lnb_ref, wo_ref, bo_ref)
    conv_ref[0] = out[:tm]
    conv_ref[1] = out[tm:]


def _inproj_conv(x, g, w_in_b, qg, kg, w_dw, b_dw, lng, lnb, wo_b, bo, tm):
    bsz, t, _ = x.shape
    assert CH_TILES == SUBLANES and bsz % 2 == 0 and t % tm == 0
    nt = t // tm
    w16 = jnp.repeat(w_dw.reshape(CONV_WIDTH, CH_TILES, LANES), 2, axis=1).astype(_BF16)
    b16 = jnp.repeat(b_dw.reshape(CH_TILES, LANES), 2, axis=0)
    full = _resident
    blk = lambda w: pl.BlockSpec((2, tm, w), lambda p, i: (p, i, 0))
    shape = lambda w, dt: jax.ShapeDtypeStruct((bsz, t, w), dt)
    outs = pl.pallas_call(
        functools.partial(_inproj_conv_kernel, tm=tm),
        grid=(bsz // 2, nt),
        in_specs=[blk(D_MODEL), full(g), full(w_in_b), full(qg), full(kg), full(w16), full(b16), full(lng), full(lnb),
                  full(wo_b), full(bo)],
        out_specs=[blk(Q_DIM), blk(KV_DIM), blk(KV_DIM), blk(D_MODEL), blk(D_MODEL), blk(D_MODEL),
                   pl.BlockSpec((2, HALO, D_CONV), lambda p, i: (p, 0, 0))],
        out_shape=[shape(Q_DIM, _BF16), shape(KV_DIM, _F32), shape(KV_DIM, _F32), shape(D_MODEL, _BF16),
                   shape(D_MODEL, _BF16), shape(D_MODEL, _BF16), jax.ShapeDtypeStruct((bsz, HALO, D_CONV), _F32)],
        scratch_shapes=[pltpu.VMEM(((tm + HALO) * CH_TILES, LANES), jnp.uint32),
                        pltpu.VMEM((tm * PAIR_TILES, LANES), _F32)],
        compiler_params=pltpu.CompilerParams(dimension_semantics=("arbitrary", "arbitrary"),
                                             vmem_limit_bytes=VMEM_LIMIT),
        name="inproj_conv",
    )(x, g, w_in_b, qg, kg, w16, b16, lng, lnb, wo_b, bo)
    return [o.reshape(bsz * t, o.shape[-1]) for o in outs[:-1]] + [outs[-1]]


def _conv_sample_kernel(state_ref, glu_ref, wdw_ref, bdw_ref, lng_ref, lnb_ref, wo_ref, bo_ref,
                        out_ref, state_out_ref):
    keep, steps = state_ref.shape[0], glu_ref.shape[0]

    def hist(u):
        return state_ref[u] if u < keep else glu_ref[u - keep]

    for t in range(steps):
        acc = hist(t) * wdw_ref[0:1, :]
        for j in range(1, CONV_WIDTH):
            acc = acc + hist(t + j) * wdw_ref[j:j + 1, :]
        out_ref[t] = _ln_swish_project(acc + bdw_ref[...], lng_ref, lnb_ref, wo_ref, bo_ref)
    state_out_ref[0:keep - steps] = state_ref[steps:keep]
    state_out_ref[keep - steps:keep] = glu_ref[...]


def _conv_sample(state_t, glu_t, w_dw, b_dw, lng, lnb, wo_b, bo, sb):
    keep, nseq, _ = state_t.shape
    steps = glu_t.shape[0]
    full = _resident
    blk = lambda r: pl.BlockSpec((r, sb, D_CONV), lambda i: (0, i, 0))
    return pl.pallas_call(
        _conv_sample_kernel,
        grid=(nseq // sb,),
        in_specs=[blk(keep), blk(steps), full(w_dw), full(b_dw), full(lng), full(lnb), full(wo_b), full(bo)],
        out_specs=[blk(steps), blk(keep)],
        out_shape=[jax.ShapeDtypeStruct((steps, nseq, D_MODEL), _BF16),
                   jax.ShapeDtypeStruct(state_t.shape, _F32)],
        compiler_params=pltpu.CompilerParams(dimension_semantics=("arbitrary",),
                                             vmem_limit_bytes=VMEM_LIMIT),
        name="conv_sample",
    )(state_t, glu_t, w_dw, b_dw, lng, lnb, wo_b, bo)


def _bucket_map():
    i = np.arange(WINDOW)[:, None]
    j = np.arange(WINDOW)[None, :]
    n = (i - j) % WINDOW
    nf = np.maximum(n, 1).astype(np.float32)
    large = MAX_EXACT + (np.log(nf / np.float32(MAX_EXACT)) / np.float32(math.log(MAX_DISTANCE / MAX_EXACT))
                         * np.float32(N_BUCKETS - MAX_EXACT)).astype(np.int32)
    return np.where(n < MAX_EXACT, n, np.minimum(large, N_BUCKETS - 1)).astype(np.int32)


def _bias_table_kernel(rb_ref, bm_ref, tbl_ref):
    p = pl.program_id(0)
    bm = bm_ref[...]
    for half in range(2):
        h = 2 * p + half
        t = jnp.zeros(bm.shape, _F32)
        for b in range(N_BUCKETS):
            t = jnp.where(bm == b, rb_ref[b, h], t)
        tbl_ref[0, :, half * WINDOW:(half + 1) * WINDOW] = t


def _bias_tables(rel_bias):
    bm = jnp.asarray(_bucket_map())
    return pl.pallas_call(
        _bias_table_kernel,
        grid=(N_PAIRS,),
        in_specs=[pl.BlockSpec(memory_space=pltpu.SMEM), pl.BlockSpec(bm.shape, lambda p: (0, 0))],
        out_specs=pl.BlockSpec((1, WINDOW, 2 * WINDOW), lambda p: (p, 0, 0)),
        out_shape=jax.ShapeDtypeStruct((N_PAIRS, WINDOW, 2 * WINDOW), _F32),
        name="bias_tables",
    )(rel_bias, bm)


def _block_diag_pairs(slab):
    low = lax.broadcasted_iota(jnp.int32, slab.shape, 1) < HEAD_DIM
    swapped = pltpu.roll(slab, HEAD_DIM, axis=1)
    zero = jnp.zeros_like(slab)
    first = jnp.concatenate([jnp.where(low, slab, zero), jnp.where(low, zero, swapped)], axis=0)
    second = jnp.concatenate([jnp.where(low, swapped, zero), jnp.where(low, zero, slab)], axis=0)
    return first.astype(_BF16), second.astype(_BF16)


def _kv_operands(k_blk, v_blk):
    ops = []
    for slab in range(KV_DIM // LANES):
        cols = slice(slab * LANES, (slab + 1) * LANES)
        ops.extend(zip(_block_diag_pairs(k_blk[:, cols]), _block_diag_pairs(v_blk[:, cols])))
    return ops


def _attend(q, prev_ops, own_ops, tbl_ref, sink_ref, prev_shift, store, transposed=False):
    tq = q.shape[0]
    rows = 2 * tq
    row = lax.broadcasted_iota(jnp.int32, (rows, 2 * WINDOW), 0)
    col = lax.broadcasted_iota(jnp.int32, (rows, 2 * WINDOW), 1)
    from_prev = (col & (WINDOW - 1)) > jnp.where(row >= tq, row - tq, row)
    top = lax.broadcasted_iota(jnp.int32, (rows, 1), 0) < tq
    low = lax.broadcasted_iota(jnp.int32, (rows, LANES), 1) < HEAD_DIM
    contract_last = (((1,), (1,)), ((), ()))

    def logits(a, k_op):
        return _dot(a, k_op) if transposed else lax.dot_general(a, k_op, contract_last, preferred_element_type=_F32)

    def weighted_values(pr, v_op):
        return lax.dot_general(pr, v_op, contract_last, preferred_element_type=_F32) if transposed else _dot(pr, v_op)

    for kvh in range(N_KV_HEADS):
        (k_prev, v_prev), (k_own, v_own) = prev_ops[kvh], own_ops[kvh]
        pair_a = 2 * kvh
        pair_b = pair_a + 1
        qq = jnp.concatenate([q[:, pair_a * LANES:(pair_a + 1) * LANES],
                              q[:, pair_b * LANES:(pair_b + 1) * LANES]], axis=0).astype(_BF16)
        sp = logits(qq, k_prev)
        so = logits(qq, k_own)
        bias = jnp.concatenate([tbl_ref[pair_a, 0:tq, :], tbl_ref[pair_b, 0:tq, :]], axis=0)
        s = jnp.where(from_prev, sp + prev_shift, so) + bias
        sink_even = jnp.where(top, sink_ref[2 * pair_a], sink_ref[2 * pair_b])
        sink_odd = jnp.where(top, sink_ref[2 * pair_a + 1], sink_ref[2 * pair_b + 1])
        m_even = jnp.maximum(jnp.max(s[:, :WINDOW], axis=-1, keepdims=True), sink_even)
        m_odd = jnp.maximum(jnp.max(s[:, WINDOW:], axis=-1, keepdims=True), sink_odd)
        e_even = jnp.exp(s[:, :WINDOW] - m_even)
        e_odd = jnp.exp(s[:, WINDOW:] - m_odd)
        p = jnp.concatenate([e_even, e_odd], axis=1).astype(_BF16)
        zero = jnp.zeros_like(p)
        o = (weighted_values(jnp.where(from_prev, p, zero), v_prev)
             + weighted_values(jnp.where(from_prev, zero, p), v_own))
        den_even = jnp.sum(e_even, axis=-1, keepdims=True) + jnp.exp(sink_even - m_even)
        den_odd = jnp.sum(e_odd, axis=-1, keepdims=True) + jnp.exp(sink_odd - m_odd)
        o = o * jnp.where(low, 1.0 / den_even, 1.0 / den_odd)
        store(pair_a, o[:tq])
        store(pair_b, o[tq:])


PROMPT_QBLOCKS = 4


def _attn_prompt_kernel(sink_ref, q_ref, kp_ref, ko_ref, vp_ref, vo_ref, tbl_ref, o_ref):
    prev_shift = jnp.where(pl.program_id(1) == 0, MASK_VALUE, 0.0).astype(_F32)
    ops = [_kv_operands(kp_ref[...], vp_ref[...])]
    for b in range(PROMPT_QBLOCKS):
        rows = slice(b * WINDOW, (b + 1) * WINDOW)
        ops.append(_kv_operands(ko_ref[rows, :], vo_ref[rows, :]))

        def store(pair, o, rows=rows):
            o_ref[rows, pair * LANES:(pair + 1) * LANES] = o.astype(o_ref.dtype)

        _attend(q_ref[rows, :], ops[b], ops[b + 1], tbl_ref, sink_ref,
                prev_shift if b == 0 else jnp.float32(0.0), store)


def _attn_prompt(q, k, v, tbl, sinks, bsz, t):
    tq = PROMPT_QBLOCKS * WINDOW
    nb = t // tq
    own = lambda w: pl.BlockSpec((tq, w), lambda b, i: (b * nb + i, 0))
    prev = lambda w: pl.BlockSpec((WINDOW, w),
                                  lambda b, i: (PROMPT_QBLOCKS * (b * nb + i) - jnp.minimum(i, 1), 0))
    return pl.pallas_call(
        _attn_prompt_kernel,
        grid=(bsz, nb),
        in_specs=[pl.BlockSpec(memory_space=pltpu.SMEM), own(Q_DIM), prev(KV_DIM), own(KV_DIM),
                  prev(KV_DIM), own(KV_DIM), pl.BlockSpec(tbl.shape, lambda b, i: (0, 0, 0))],
        out_specs=own(Q_DIM),
        out_shape=jax.ShapeDtypeStruct((bsz * t, Q_DIM), _BF16),
        compiler_params=pltpu.CompilerParams(dimension_semantics=("arbitrary", "arbitrary"),
                                             vmem_limit_bytes=VMEM_LIMIT),
        name="attn_prompt",
    )(sinks, q, k, k, v, v, tbl)


def _block_diag_t(x):
    xb = x.astype(_BF16)
    z = jnp.zeros_like(xb)
    return jnp.concatenate([jnp.concatenate([xb, z], axis=1), jnp.concatenate([z, xb], axis=1)], axis=0)


def _attn_sample_kernel(sink_ref, q_ref, kn_ref, vn_ref, ck_ref, cv_ref, tbl_ref, o_ref, cko_ref, cvo_ref,
                        *, sb, steps):
    pad = jnp.zeros((WINDOW - steps, LANES), _F32)
    lane = lax.broadcasted_iota(jnp.int32, (HEAD_DIM, WINDOW), 1)

    def one_sequence(s):
        def store(pair, o):
            o_ref[s, :, pair * LANES:(pair + 1) * LANES] = o

        prev_ops, own_ops = [], []
        for slab in range(KV_DIM // LANES):
            cols = slice(slab * LANES, (slab + 1) * LANES)
            new_k = jnp.concatenate([kn_ref[s][:, cols], pad], axis=0).T
            new_v = jnp.concatenate([vn_ref[s][:, cols], pad], axis=0).T
            for sub in range(2):
                kvh = 2 * slab + sub
                part = slice(sub * HEAD_DIM, (sub + 1) * HEAD_DIM)
                kt, vt = ck_ref[s, kvh], cv_ref[s, kvh]
                cko_ref[s, kvh] = pltpu.roll(jnp.where(lane < steps, new_k[part], kt), WINDOW - steps, axis=1)
                cvo_ref[s, kvh] = pltpu.roll(jnp.where(lane < steps, new_v[part], vt), WINDOW - steps, axis=1)
                prev_ops.append((_block_diag_t(kt), _block_diag_t(vt)))
                own_ops.append((_block_diag_t(new_k[part]), _block_diag_t(new_v[part])))
        _attend(q_ref[s], prev_ops, own_ops, tbl_ref, sink_ref, jnp.float32(0.0), store, transposed=True)

    for s in range(sb):
        one_sequence(s)


def _attn_sample(q, k_new, v_new, cache_kt, cache_vt, tbl, sinks, sb):
    nseq, steps, _ = q.shape
    seq = lambda r, w: pl.BlockSpec((sb, r, w), lambda i: (i, 0, 0))
    cache = pl.BlockSpec((sb, N_KV_HEADS, HEAD_DIM, WINDOW), lambda i: (i, 0, 0, 0))
    return pl.pallas_call(
        functools.partial(_attn_sample_kernel, sb=sb, steps=steps),
        grid=(nseq // sb,),
        in_specs=[pl.BlockSpec(memory_space=pltpu.SMEM), seq(steps, Q_DIM), seq(steps, KV_DIM), seq(steps, KV_DIM),
                  cache, cache, pl.BlockSpec(tbl.shape, lambda i: (0, 0, 0))],
        out_specs=[seq(steps, Q_DIM), cache, cache],
        out_shape=[jax.ShapeDtypeStruct((nseq, steps, Q_DIM), _F32),
                   jax.ShapeDtypeStruct(cache_kt.shape, _F32), jax.ShapeDtypeStruct(cache_vt.shape, _F32)],
        compiler_params=pltpu.CompilerParams(dimension_semantics=("arbitrary",),
                                             vmem_limit_bytes=VMEM_LIMIT),
        name="attn_sample",
    )(sinks, q, k_new, v_new, cache_kt, cache_vt, tbl)


def _lane_min_index(mask, lane):
    return jnp.min(jnp.where(mask, lane, float(LANES)), axis=-1, keepdims=True)


def _finish_kernel(x_ref, conv_ref, o_ref, ga_ref, gb_ref, wa_ref, wo_ref, ng_ref, wr_ref, br_ref,
                   tri_ref, h_ref, hn_ref, route_ref, count_ref, running_ref):
    @pl.when(pl.program_id(0) == 0)
    def _():
        running_ref[...] = jnp.zeros_like(running_ref)

    attn_out = _dot(o_ref[...].astype(_BF16), wa_ref[...])
    merged = ga_ref[...].astype(_F32) * conv_ref[...].astype(_F32) + gb_ref[...].astype(_F32) * attn_out
    h = x_ref[...] + _dot(merged.astype(_BF16), wo_ref[...])
    h_ref[...] = h
    hn = h * lax.rsqrt(jnp.mean(h * h, axis=-1, keepdims=True) + EPS) * ng_ref[...]
    hn_ref[...] = _pack_bf16_pairs(hn)

    logits = _dot(hn.astype(_BF16), wr_ref[...]) + br_ref[...]
    lane = lax.broadcasted_iota(jnp.int32, logits.shape, 1).astype(_F32)
    gmask = lane < N_GROUPS
    gl = jnp.where(gmask, logits, MASK_VALUE)
    gmax = jnp.max(gl, axis=-1, keepdims=True)
    grp = _lane_min_index(gmask & (gl == gmax), lane)
    p_grp = 1.0 / jnp.sum(jnp.where(gmask, jnp.exp(gl - gmax), 0.0), axis=-1, keepdims=True)
    e_lo = N_GROUPS + grp * EXPERTS_PER_GROUP
    emask = (lane >= e_lo) & (lane < e_lo + EXPERTS_PER_GROUP)
    el = jnp.where(emask, logits, MASK_VALUE)
    ex = jnp.where(emask, jnp.exp(el - jnp.max(el, axis=-1, keepdims=True)), 0.0)
    prob = jnp.where(emask, ex / jnp.sum(ex, axis=-1, keepdims=True), -1.0)
    p1 = jnp.max(prob, axis=-1, keepdims=True)
    i1 = _lane_min_index(prob == p1, lane)
    rest = jnp.where(lane == i1, -1.0, prob)
    p2 = jnp.max(rest, axis=-1, keepdims=True)
    i2 = _lane_min_index(rest == p2, lane)
    w1 = p_grp * p1 / (p1 + p2)
    w2 = p_grp * p2 / (p1 + p2)
    e1 = i1 - N_GROUPS
    e2 = i2 - N_GROUPS

    hot1 = lane == e1
    hot2 = lane == e2
    hot = jnp.where(hot1 | hot2, 1.0, 0.0)
    before = _dot(tri_ref[...], hot.astype(_BF16)) + running_ref[...]
    rank1 = jnp.sum(jnp.where(hot1, before, 0.0), axis=-1, keepdims=True)
    rank2 = jnp.sum(jnp.where(hot2, before, 0.0), axis=-1, keepdims=True)
    running_ref[...] += jnp.sum(hot, axis=0, keepdims=True)
    count_ref[...] = jnp.broadcast_to(running_ref[...], count_ref.shape)

    fields = (e1, e2, w1, w2, rank1, rank2)
    route = jnp.zeros(logits.shape, _F32)
    for pos, val in enumerate(fields):
        route = jnp.where(lane == pos, val, route)
    route_ref[...] = route


ROUTE_E, ROUTE_W, ROUTE_RANK = 0, 2, 4
DEST_ROWS = 4096
COMBINE_ROWS = 1024


def _finish(x, conv_out, o, ga, gb, wa_b, wo_b, ng, wr_b, br, tm):
    n = x.shape[0]
    tri = jnp.asarray(np.tril(np.ones((tm, tm), np.float32), -1), _BF16)
    row = lambda w: pl.BlockSpec((tm, w), lambda i: (i, 0))
    full = _resident
    return pl.pallas_call(
        _finish_kernel,
        grid=(n // tm,),
        in_specs=[row(D_MODEL), row(D_MODEL), row(Q_DIM), row(D_MODEL), row(D_MODEL),
                  full(wa_b), full(wo_b), full(ng), full(wr_b), full(br), full(tri)],
        out_specs=[row(D_MODEL), row(HALF), row(LANES), pl.BlockSpec((SUBLANES, LANES), lambda i: (0, 0))],
        out_shape=[jax.ShapeDtypeStruct((n, D_MODEL), _F32),
                   jax.ShapeDtypeStruct((n, HALF), jnp.uint32),
                   jax.ShapeDtypeStruct((n, LANES), _F32),
                   jax.ShapeDtypeStruct((SUBLANES, LANES), _F32)],
        scratch_shapes=[pltpu.VMEM((1, LANES), _F32)],
        compiler_params=pltpu.CompilerParams(dimension_semantics=("arbitrary",),
                                             vmem_limit_bytes=VMEM_LIMIT),
        name="finish",
    )(x, conv_out, o, ga, gb, wa_b, wo_b, ng, wr_b, br, tri)


def _dest_kernel(route_ref, starts_ref, dest_ref):
    route = route_ref[...]
    lane = lax.broadcasted_iota(jnp.int32, route.shape, 1)
    out = jnp.zeros(route.shape, jnp.int32)
    for j in range(TOP_K):
        e = route[:, ROUTE_E + j:ROUTE_E + j + 1].astype(jnp.int32)
        start = jnp.sum(jnp.where(lane == e, starts_ref[...], 0.0), axis=-1, keepdims=True)
        d = (start + route[:, ROUTE_RANK + j:ROUTE_RANK + j + 1]).astype(jnp.int32)
        out = jnp.where(lane == j, d, out)
    dest_ref[...] = out


def _dest(route, starts_row, tm):
    n = route.shape[0]
    row = pl.BlockSpec((tm, LANES), lambda i: (i, 0))
    return pl.pallas_call(
        _dest_kernel,
        grid=(n // tm,),
        in_specs=[row, pl.BlockSpec((1, LANES), lambda i: (0, 0))],
        out_specs=row,
        out_shape=jax.ShapeDtypeStruct((n, LANES), jnp.int32),
        name="dest",
    )(route, starts_row)


SC_CORES = 2
SC_SUBCORES = 16
SC_WORKERS = SC_CORES * SC_SUBCORES
SC_IN_FLIGHT = 4
SC_CHUNK_BYTES = 64 * 1024


def _sc_move_rows(src, idx, gather):
    n, d = src.shape
    b = idx.shape[0]
    per_worker = b // SC_WORKERS
    assert per_worker * SC_WORKERS == b and (gather or n % per_worker == 0), (b, n)
    chunk = min(per_worker // SC_IN_FLIGHT, SC_CHUNK_BYTES // (d * 4))
    n_iters = per_worker // (chunk * SC_IN_FLIGHT)
    assert n_iters * chunk * SC_IN_FLIGHT == per_worker and chunk % SUBLANES == 0, (per_worker, chunk)
    mesh = plsc.VectorSubcoreMesh(core_axis_name="c", subcore_axis_name="s",
                                  num_cores=SC_CORES, num_subcores=SC_SUBCORES)
    scratch = ([pltpu.VMEM((chunk,), jnp.int32)] * SC_IN_FLIGHT + [pltpu.VMEM((chunk, d), src.dtype)] * SC_IN_FLIGHT
               + [pltpu.SemaphoreType.DMA] * SC_IN_FLIGHT)

    @functools.partial(pl.kernel, mesh=mesh, out_type=jax.ShapeDtypeStruct((b, d), src.dtype),
                       scratch_types=scratch, name="sc_gather_rows" if gather else "sc_scatter_rows",
                       cost_estimate=pl.CostEstimate(flops=0, transcendentals=0, bytes_accessed=2 * b * d * 4 + b * 4))
    def move(src_hbm, idx_hbm, out_hbm, *bufs):
        idx_v = bufs[:SC_IN_FLIGHT]
        rows_v = bufs[SC_IN_FLIGHT:2 * SC_IN_FLIGHT]
        sems = bufs[2 * SC_IN_FLIGHT:]
        worker = lax.axis_index("s") * SC_CORES + lax.axis_index("c")

        @pl.loop(0, n_iters)
        def _(it):
            bases = [pl.multiple_of(worker * per_worker + (it * SC_IN_FLIGHT + j) * chunk, chunk)
                     for j in range(SC_IN_FLIGHT)]
            loads = [pltpu.async_copy(idx_hbm.at[pl.ds(bases[j], chunk)], idx_v[j], sems[j])
                     for j in range(SC_IN_FLIGHT)]
            reads = []
            for j in range(SC_IN_FLIGHT):
                loads[j].wait()
                if gather:
                    rows = src_hbm.at[idx_v[j]]
                else:
                    rows = src_hbm.at[pl.ds(pl.multiple_of(lax.rem(bases[j], n), chunk), chunk)]
                reads.append(pltpu.async_copy(rows, rows_v[j], sems[j]))
            writes = []
            for j in range(SC_IN_FLIGHT):
                reads[j].wait()
                dst = out_hbm.at[pl.ds(bases[j], chunk)] if gather else out_hbm.at[idx_v[j]]
                writes.append(pltpu.async_copy(rows_v[j], dst, sems[j]))
            for w in writes:
                w.wait()

    return move(src, idx)


def _sc_gather_rows(table, idx):
    return _sc_move_rows(table, idx, gather=True)


def _sc_scatter_rows(src, idx):
    return _sc_move_rows(src, idx, gather=False)


def _expert_kernel(blk_ref, exp_ref, lo_ref, hi_ref, x_ref, wg_ref, wu_ref, wd_ref, yb_ref):
    del blk_ref, exp_ref
    k = pl.program_id(0)
    lo, hi = lo_ref[k], hi_ref[k]

    @pl.when(hi > lo)
    def _():
        x = _unpack_bf16_pairs(x_ref[...])
        g = _dot(x, wg_ref[0])
        u = _dot(x, wu_ref[0])
        hid = g * jax.nn.sigmoid(g) * u
        y = _dot(hid, wd_ref[0])
        r = lax.broadcasted_iota(jnp.int32, yb_ref.shape, 0)
        pltpu.store(yb_ref, _pack_bf16_pairs(y), mask=(r >= lo) & (r < hi))


def _experts(items, xs, w_gate, w_up, w_down):
    n_items = items[0].shape[0]
    wspec = lambda a: pl.BlockSpec((1,) + a.shape[1:], lambda k, blk, exp, lo, hi: (exp[k], 0, 0))
    rows = pl.BlockSpec((MOE_ROWS, HALF), lambda k, blk, exp, lo, hi: (blk[k], 0))
    grid_spec = pltpu.PrefetchScalarGridSpec(
        num_scalar_prefetch=4,
        grid=(n_items,),
        in_specs=[rows, wspec(w_gate), wspec(w_up), wspec(w_down)],
        out_specs=rows,
    )
    return pl.pallas_call(
        _expert_kernel,
        grid_spec=grid_spec,
        out_shape=jax.ShapeDtypeStruct(xs.shape, xs.dtype),
        compiler_params=pltpu.CompilerParams(dimension_semantics=("arbitrary",),
                                             vmem_limit_bytes=VMEM_LIMIT),
        cost_estimate=pl.CostEstimate(
            flops=n_items * MOE_ROWS * 6 * D_MODEL * D_EXPERT, transcendentals=n_items * MOE_ROWS * D_EXPERT,
            bytes_accessed=2 * xs.size * 4 + (w_gate.size + w_up.size + w_down.size) * 4),
        name="experts",
    )(*items, xs, w_gate, w_up, w_down)


def _combine_kernel(h_ref, route_ref, g0_ref, g1_ref, y_ref):
    route = route_ref[...]
    y_ref[...] = (h_ref[...] + route[:, ROUTE_W:ROUTE_W + 1] * _unpack_bf16_pairs(g0_ref[...])
                  + route[:, ROUTE_W + 1:ROUTE_W + 2] * _unpack_bf16_pairs(g1_ref[...]))


def _combine(h, route, g, tm):
    n = h.shape[0]
    nt = n // tm
    row = lambda w: pl.BlockSpec((tm, w), lambda i: (i, 0))
    return pl.pallas_call(
        _combine_kernel,
        grid=(nt,),
        in_specs=[row(D_MODEL), row(LANES), row(HALF), pl.BlockSpec((tm, HALF), lambda i: (nt + i, 0))],
        out_specs=row(D_MODEL),
        out_shape=jax.ShapeDtypeStruct((n, D_MODEL), _F32),
        compiler_params=pltpu.CompilerParams(dimension_semantics=("arbitrary",),
                                             vmem_limit_bytes=VMEM_LIMIT),
        name="combine",
    )(h, route, g, g)


def _work_items(counts, n_pairs):
    n_blocks = n_pairs // MOE_ROWS
    starts = jnp.cumsum(counts) - counts
    cuts = jnp.sort(jnp.concatenate([jnp.arange(n_blocks, dtype=jnp.int32) * MOE_ROWS, starts]))
    ends = jnp.concatenate([cuts[1:], jnp.full((1,), n_pairs, jnp.int32)])
    blk = jnp.minimum(cuts // MOE_ROWS, n_blocks - 1)
    expert = jnp.clip(jnp.sum(starts[None, :] <= cuts[:, None], axis=1) - 1, 0, N_EXPERTS - 1).astype(jnp.int32)
    return starts, (blk, expert, cuts - blk * MOE_ROWS, ends - blk * MOE_ROWS)


def _dispatch(hn, route, counts_rows):
    n = hn.shape[0]
    n_pairs = n * TOP_K
    counts = counts_rows[0, :N_EXPERTS].astype(jnp.int32)
    starts, items = _work_items(counts, n_pairs)
    starts_row = jnp.zeros((1, LANES), _F32).at[0, :N_EXPERTS].set(starts.astype(_F32))
    dest = _dest(route, starts_row, min(n, DEST_ROWS))[:, :TOP_K].T.reshape(n_pairs)
    return _sc_scatter_rows(hn, dest), dest, items


def _after(value, *earlier):
    return lax.optimization_barrier((value,) + earlier)[0]


def kernel(x_prompt, x_sample, state_conv, cache_k, cache_v, norm_attn_g, w_in, q_norm_g, k_norm_g, rel_bias, attn_sinks, w_dw, b_dw, conv_ln_g, conv_ln_b, w_conv_out, b_conv_out, w_attn_out, w_out, norm_ffn_g, w_grp, b_grp, w_router, b_router, w_gate, w_up, w_down):
    bsz, t, _ = x_prompt.shape
    nseq, steps, _ = x_sample.shape
    row = lambda a: a.reshape(1, -1).astype(_F32)

    w_in_b = w_in.astype(_BF16)
    wco_b = w_conv_out.astype(_BF16)
    wa_b = w_attn_out.astype(_BF16)
    wo_b = w_out.astype(_BF16)
    qg = row(jnp.tile(q_norm_g, N_HEADS)) * (HEAD_DIM ** -0.5)
    kg = row(jnp.tile(k_norm_g, N_KV_HEADS))
    w_rt = jnp.zeros((D_MODEL, LANES), _F32).at[:, :N_GROUPS].set(w_grp).at[:, N_GROUPS:N_GROUPS + N_EXPERTS].set(w_router)
    wr_b = w_rt.astype(_BF16)
    b_rt = jnp.zeros((1, LANES), _F32).at[0, :N_GROUPS].set(b_grp).at[0, N_GROUPS:N_GROUPS + N_EXPERTS].set(b_router)
    tbl = _bias_tables(rel_bias)
    conv_params = (w_dw, row(b_dw), row(conv_ln_g), row(conv_ln_b), wco_b, row(b_conv_out))

    def finish(x2d, conv_out, o, ga, gb):
        return _finish(x2d, conv_out, o, ga, gb, wa_b, wo_b, row(norm_ffn_g), wr_b, b_rt, ROW_TILE)

    xp = x_prompt.reshape(bsz * t, D_MODEL)
    q, k, v, ga, gb, conv_out, glu_tail = _inproj_conv(x_prompt, row(norm_attn_g), w_in_b, qg, kg, *conv_params,
                                                       ROW_TILE // 2)
    o = _attn_prompt(q, k, v, tbl, attn_sinks, bsz, t)
    h_p, hn_p, route_p, counts_p = finish(xp, conv_out, o, ga, gb)
    rows_p, dest_p, items_p = _dispatch(hn_p, route_p, counts_p)
    state_conv_prompt = glu_tail[:, HALO - (CONV_WIDTH - 1):]
    tail = lambda a: a.reshape(bsz, t, KV_DIM)[:, t - WINDOW:].reshape(bsz, WINDOW, N_KV_HEADS, HEAD_DIM)
    cache_k_prompt, cache_v_prompt = tail(k), tail(v)

    xs = _after(x_sample, dest_p).reshape(nseq * steps, D_MODEL)
    glu, q, k, v, ga, gb = _inproj(xs, row(norm_attn_g), w_in_b, qg, kg, _F32, ROW_TILE)
    glu_t = glu.reshape(nseq, steps, D_CONV).transpose(1, 0, 2)
    conv_out, state_t = _conv_sample(state_conv.transpose(1, 0, 2), glu_t, *conv_params, SAMPLE_CONV_SEQS)
    conv_out = conv_out.transpose(1, 0, 2).reshape(nseq * steps, D_MODEL)
    k3 = k.reshape(nseq, steps, KV_DIM)
    v3 = v.reshape(nseq, steps, KV_DIM)
    o, ck_t, cv_t = _attn_sample(q.reshape(nseq, steps, Q_DIM), k3, v3, cache_k.transpose(0, 2, 3, 1),
                                 cache_v.transpose(0, 2, 3, 1), tbl, attn_sinks, SAMPLE_ATTN_SEQS)
    h_s, hn_s, route_s, counts_s = finish(xs, conv_out, o.reshape(nseq * steps, Q_DIM), ga, gb)
    rows_s, dest_s, items_s = _dispatch(hn_s, route_s, counts_s)
    state_conv_sample = state_t.transpose(1, 0, 2)
    cache_k_sample = ck_t.transpose(0, 3, 1, 2)
    cache_v_sample = cv_t.transpose(0, 3, 1, 2)

    yb_p = _experts(items_p, rows_p, w_gate, w_up, w_down)
    g_p = _sc_gather_rows(yb_p, dest_p)
    yb_s = _experts(items_s, _after(rows_s, yb_p), w_gate, w_up, w_down)
    g_s = _sc_gather_rows(yb_s, dest_s)
    y_prompt = _combine(h_p, route_p, g_p, COMBINE_ROWS).reshape(bsz, t, D_MODEL)
    y_sample = _combine(h_s, route_s, _after(g_s, y_prompt), COMBINE_ROWS).reshape(nseq, steps, D_MODEL)

    return (y_prompt, y_sample, state_conv_prompt, cache_k_prompt, cache_v_prompt,
            state_conv_sample, cache_k_sample, cache_v_sample)
```

```python
import functools
import math

import numpy as np
import jax
import jax.numpy as jnp
from jax import lax
from jax.experimental import pallas as pl
from jax.experimental.pallas import tpu as pltpu
from jax.experimental.pallas import tpu_sc as plsc

D_MODEL = 1024
N_HEADS = 16
HEAD_DIM = 64
N_KV_HEADS = 4
WINDOW = 128
Q_DIM = N_HEADS * HEAD_DIM
KV_DIM = N_KV_HEADS * HEAD_DIM
N_BUCKETS = 32
MAX_EXACT = N_BUCKETS // 2
MAX_DISTANCE = 128
D_CONV = D_MODEL
CONV_WIDTH = 31
N_GROUPS = 4
EXPERTS_PER_GROUP = 8
N_EXPERTS = N_GROUPS * EXPERTS_PER_GROUP
TOP_K = 2
D_EXPERT = 256
EPS = 1e-6

LANES = 128
SUBLANES = 8
N_PAIRS = N_HEADS // 2
MOE_ROWS = 512
MASK_VALUE = -1e30
VMEM_LIMIT = 56 * 1024 * 1024
ROW_TILE = 512
SAMPLE_CONV_SEQS = 64
SAMPLE_ATTN_SEQS = 8

_F32 = jnp.float32
_BF16 = jnp.bfloat16


def _resident(a):
    return pl.BlockSpec(a.shape, lambda *_: (0,) * a.ndim, pipeline_mode=pl.Buffered(1))


def _dot(a, b):
    return jnp.dot(a, b, preferred_element_type=_F32)


HALF = D_MODEL // 2


def _pack_pair(lo, hi):
    lo_bits = pltpu.bitcast(lo.astype(_BF16).astype(_F32), jnp.uint32)
    hi_bits = pltpu.bitcast(hi.astype(_BF16).astype(_F32), jnp.uint32)
    return hi_bits | (lo_bits >> 16)


def _pack_bf16_pairs(x):
    return _pack_pair(x[:, :HALF], x[:, HALF:])


def _unpack_bf16_pairs(w):
    lo = pltpu.bitcast(w << 16, _F32)
    hi = pltpu.bitcast(w & jnp.uint32(0xFFFF0000), _F32)
    return jnp.concatenate([lo, hi], axis=1)


def _head_rms_scale(z):
    low = lax.broadcasted_iota(jnp.int32, (z.shape[0], LANES), 1) < HEAD_DIM
    slabs = []
    for c in range(z.shape[1] // LANES):
        sq = z[:, c * LANES:(c + 1) * LANES]
        sq = sq * sq
        first = jnp.sum(jnp.where(low, sq, 0.0), axis=-1, keepdims=True)
        second = jnp.sum(jnp.where(low, 0.0, sq), axis=-1, keepdims=True)
        slabs.append(lax.rsqrt(jnp.where(low, first, second) * (1.0 / HEAD_DIM) + EPS))
    return jnp.concatenate(slabs, axis=1)


def _inproj_kernel(x_ref, g_ref, w_ref, qg_ref, kg_ref,
                   glu_ref, q_ref, k_ref, v_ref, ga_ref, gb_ref):
    x = x_ref[...]
    xn = x * lax.rsqrt(jnp.mean(x * x, axis=-1, keepdims=True) + EPS) * g_ref[...]
    xb = xn.astype(_BF16)

    def seg(lo, width):
        return _dot(xb, w_ref[:, lo:lo + width])

    a = seg(0, D_CONV)
    b = seg(D_CONV, D_CONV)
    glu_ref[...] = a * jax.nn.sigmoid(b)
    off = 2 * D_CONV
    q = seg(off, Q_DIM)
    q_ref[...] = (q * _head_rms_scale(q) * qg_ref[...]).astype(q_ref.dtype)
    off += Q_DIM
    k = seg(off, KV_DIM)
    k_ref[...] = k * _head_rms_scale(k) * kg_ref[...]
    off += KV_DIM
    v_ref[...] = seg(off, KV_DIM)
    off += KV_DIM
    ga_ref[...] = jax.nn.sigmoid(seg(off, D_MODEL)).astype(ga_ref.dtype)
    off += D_MODEL
    gb_ref[...] = jax.nn.sigmoid(seg(off, D_MODEL)).astype(gb_ref.dtype)


def _inproj(x, g, w_in_b, qg, kg, q_dtype, tm):
    n = x.shape[0]
    row = lambda w: pl.BlockSpec((tm, w), lambda i: (i, 0))
    full = _resident
    return pl.pallas_call(
        _inproj_kernel,
        grid=(n // tm,),
        in_specs=[row(D_MODEL), full(g), full(w_in_b), full(qg), full(kg)],
        out_specs=[row(D_CONV), row(Q_DIM), row(KV_DIM), row(KV_DIM), row(D_MODEL), row(D_MODEL)],
        out_shape=[jax.ShapeDtypeStruct((n, D_CONV), _F32),
                   jax.ShapeDtypeStruct((n, Q_DIM), q_dtype),
                   jax.ShapeDtypeStruct((n, KV_DIM), _F32),
                   jax.ShapeDtypeStruct((n, KV_DIM), _F32),
                   jax.ShapeDtypeStruct((n, D_MODEL), _BF16),
                   jax.ShapeDtypeStruct((n, D_MODEL), _BF16)],
        compiler_params=pltpu.CompilerParams(dimension_semantics=("arbitrary",),
                                             vmem_limit_bytes=VMEM_LIMIT),
        name="inproj",
    )(x, g, w_in_b, qg, kg)


def _ln_swish_project(y, lng_ref, lnb_ref, wo_ref, bo_ref):
    mu = jnp.mean(y, axis=-1, keepdims=True)
    yc = y - mu
    var = jnp.mean(yc * yc, axis=-1, keepdims=True)
    z = yc * lax.rsqrt(var + EPS) * lng_ref[...] + lnb_ref[...]
    z = z * jax.nn.sigmoid(z)
    return (_dot(z.astype(_BF16), wo_ref[...]) + bo_ref[...]).astype(_BF16)


HALO = 32
CONV_STEPS = 16
CH_TILES = D_CONV // LANES


PIECE = 256
PAIR_TILES = 2 * CH_TILES


def _inproj_conv_kernel(x_ref, g_ref, w_ref, qg_ref, kg_ref, w16_ref, b16_ref, lng_ref, lnb_ref, wo_ref, bo_ref,
                        q_ref, k_ref, v_ref, ga_ref, gb_ref, conv_ref, tail_ref, hist_ref, y_ref, *, tm):
    i = pl.program_id(1)
    rows = 2 * tm

    @pl.when(i == 0)
    def _():
        hist_ref[0:HALO * CH_TILES, :] = jnp.zeros((HALO * CH_TILES, LANES), jnp.uint32)

    @pl.when(i > 0)
    def _():
        hist_ref[0:HALO * CH_TILES, :] = hist_ref[tm * CH_TILES:(tm + HALO) * CH_TILES, :]

    x = jnp.concatenate([x_ref[0], x_ref[1]], axis=0)
    xn = x * lax.rsqrt(jnp.mean(x * x, axis=-1, keepdims=True) + EPS) * g_ref[...]
    xb = xn.astype(_BF16)

    def seg(lo, width):
        return _dot(xb, w_ref[:, lo:lo + width])

    def put(ref, lo, val):
        ref[0, :, lo:lo + PIECE] = val[:tm].astype(ref.dtype)
        ref[1, :, lo:lo + PIECE] = val[tm:].astype(ref.dtype)

    for p in range(D_CONV // PIECE):
        lo = p * PIECE
        glu = seg(lo, PIECE) * jax.nn.sigmoid(seg(D_CONV + lo, PIECE))
        tail_ref[0, :, lo:lo + PIECE] = glu[tm - HALO:tm, :]
        tail_ref[1, :, lo:lo + PIECE] = glu[rows - HALO:, :]
        words = _pack_pair(glu[:tm], glu[tm:])
        for c in range(PIECE // LANES):
            tile = lo // LANES + c
            hist_ref[pl.ds(HALO * CH_TILES + tile, tm, stride=CH_TILES), :] = words[:, c * LANES:(c + 1) * LANES]

    first = HALO - (CONV_WIDTH - 1)

    def conv_chunk(ci, carry):
        t0 = ci * CONV_STEPS
        acc = jnp.zeros((CONV_STEPS, PAIR_TILES, LANES), _F32)
        for j in range(CONV_WIDTH):
            lo = pl.multiple_of((t0 + first + j) * CH_TILES, CH_TILES)
            xw = pltpu.bitcast(hist_ref[pl.ds(lo, CONV_STEPS * CH_TILES), :], _BF16)
            acc = acc + (xw.reshape(CONV_STEPS, PAIR_TILES, LANES).astype(_F32)
                         * w16_ref[j][None].astype(_F32))
        y_ref[pl.ds(pl.multiple_of(t0 * PAIR_TILES, PAIR_TILES), CONV_STEPS * PAIR_TILES), :] = (
            (acc + b16_ref[...][None]).reshape(CONV_STEPS * PAIR_TILES, LANES))
        return carry

    def q_piece(lo):
        q = seg(2 * D_CONV + lo, PIECE)
        put(q_ref, lo, q * _head_rms_scale(q) * qg_ref[:, lo:lo + PIECE])

    def k_piece(lo):
        k = seg(2 * D_CONV + Q_DIM + lo, PIECE)
        put(k_ref, lo, k * _head_rms_scale(k) * kg_ref[:, lo:lo + PIECE])

    def v_piece(lo):
        put(v_ref, lo, seg(2 * D_CONV + Q_DIM + KV_DIM + lo, PIECE))

    def gate_piece(ref, base, lo):
        put(ref, lo, jax.nn.sigmoid(seg(base + lo, PIECE)))

    gate_base = 2 * D_CONV + Q_DIM + 2 * KV_DIM
    pieces = ([functools.partial(q_piece, lo) for lo in range(0, Q_DIM, PIECE)]
              + [functools.partial(k_piece, lo) for lo in range(0, KV_DIM, PIECE)]
              + [functools.partial(v_piece, lo) for lo in range(0, KV_DIM, PIECE)]
              + [functools.partial(gate_piece, ga_ref, gate_base, lo) for lo in range(0, D_MODEL, PIECE)]
              + [functools.partial(gate_piece, gb_ref, gate_base + D_MODEL, lo) for lo in range(0, D_MODEL, PIECE)])

    for piece in pieces:
        piece()
    lax.fori_loop(0, tm // CONV_STEPS, conv_chunk, 0)

    y = jnp.concatenate(
        [jnp.concatenate([y_ref[pl.ds(2 * c + s, tm, stride=PAIR_TILES), :] for c in range(CH_TILES)], axis=1)
         for s in range(2)], axis=0)
    out = _ln_swish_project(y, lng_ref, lnb_ref, wo_ref, bo_ref)
    conv_ref[0] = out[:tm]
    conv_ref[1] = out[tm:]


def _inproj_conv(x, g, w_in_b, qg, kg, w_dw, b_dw, lng, lnb, wo_b, bo, tm):
    bsz, t, _ = x.shape
    assert CH_TILES == SUBLANES and bsz % 2 == 0 and t % tm == 0
    nt = t // tm
    w16 = jnp.repeat(w_dw.reshape(CONV_WIDTH, CH_TILES, LANES), 2, axis=1).astype(_BF16)
    b16 = jnp.repeat(b_dw.reshape(CH_TILES, LANES), 2, axis=0)
    full = _resident
    blk = lambda w: pl.BlockSpec((2, tm, w), lambda p, i: (p, i, 0))
    shape = lambda w, dt: jax.ShapeDtypeStruct((bsz, t, w), dt)
    outs = pl.pallas_call(
        functools.partial(_inproj_conv_kernel, tm=tm),
        grid=(bsz // 2, nt),
        in_specs=[blk(D_MODEL), full(g), full(w_in_b), full(qg), full(kg), full(w16), full(b16), full(lng), full(lnb),
                  full(wo_b), full(bo)],
        out_specs=[blk(Q_DIM), blk(KV_DIM), blk(KV_DIM), blk(D_MODEL), blk(D_MODEL), blk(D_MODEL),
                   pl.BlockSpec((2, HALO, D_CONV), lambda p, i: (p, 0, 0))],
        out_shape=[shape(Q_DIM, _BF16), shape(KV_DIM, _F32), shape(KV_DIM, _F32), shape(D_MODEL, _BF16),
                   shape(D_MODEL, _BF16), shape(D_MODEL, _BF16), jax.ShapeDtypeStruct((bsz, HALO, D_CONV), _F32)],
        scratch_shapes=[pltpu.VMEM(((tm + HALO) * CH_TILES, LANES), jnp.uint32),
                        pltpu.VMEM((tm * PAIR_TILES, LANES), _F32)],
        compiler_params=pltpu.CompilerParams(dimension_semantics=("arbitrary", "arbitrary"),
                                             vmem_limit_bytes=VMEM_LIMIT),
        name="inproj_conv",
    )(x, g, w_in_b, qg, kg, w16, b16, lng, lnb, wo_b, bo)
    return [o.reshape(bsz * t, o.shape[-1]) for o in outs[:-1]] + [outs[-1]]


SAMPLE_CONV_COLS = 512


def _conv_sample_kernel(state_ref, glu_ref, wb_ref, bdw_ref, lng_ref, lnb_ref, wo_ref, bo_ref,
                        out_ref, state_out_ref, hist_ref, y_ref):
    keep, steps = state_ref.shape[0], glu_ref.shape[0]
    sb = state_ref.shape[1]
    hist_ref[0:keep] = state_ref[...].astype(_BF16)
    hist_ref[keep:keep + steps] = glu_ref[...].astype(_BF16)

    def step(t, carry):
        for c0 in range(0, D_CONV, SAMPLE_CONV_COLS):
            cols = slice(c0, c0 + SAMPLE_CONV_COLS)
            acc = jnp.zeros((sb, SAMPLE_CONV_COLS), _F32)
            for j in range(CONV_WIDTH):
                acc = acc + hist_ref[t + j, :, cols].astype(_F32) * wb_ref[j, :, cols].astype(_F32)
            y_ref[t, :, cols] = acc + bdw_ref[:, cols]
        return carry

    lax.fori_loop(0, steps, step, 0)
    for t in range(steps):
        out_ref[t] = _ln_swish_project(y_ref[t], lng_ref, lnb_ref, wo_ref, bo_ref)
    state_out_ref[0:keep - steps] = state_ref[steps:keep]
    state_out_ref[keep - steps:keep] = glu_ref[...]


def _conv_sample(state_t, glu_t, w_dw, b_dw, lng, lnb, wo_b, bo, sb):
    keep, nseq, _ = state_t.shape
    steps = glu_t.shape[0]
    wb = jnp.broadcast_to(w_dw.astype(_BF16)[:, None, :], (CONV_WIDTH, sb, D_CONV))
    full = _resident
    blk = lambda r: pl.BlockSpec((r, sb, D_CONV), lambda i: (0, i, 0))
    return pl.pallas_call(
        _conv_sample_kernel,
        grid=(nseq // sb,),
        in_specs=[blk(keep), blk(steps), full(wb), full(b_dw), full(lng), full(lnb), full(wo_b), full(bo)],
        out_specs=[blk(steps), blk(keep)],
        out_shape=[jax.ShapeDtypeStruct((steps, nseq, D_MODEL), _BF16),
                   jax.ShapeDtypeStruct(state_t.shape, _F32)],
        scratch_shapes=[pltpu.VMEM((keep + steps, sb, D_CONV), _BF16), pltpu.VMEM((steps, sb, D_CONV), _F32)],
        compiler_params=pltpu.CompilerParams(dimension_semantics=("arbitrary",),
                                             vmem_limit_bytes=VMEM_LIMIT),
        name="conv_sample",
    )(state_t, glu_t, wb, b_dw, lng, lnb, wo_b, bo)


def _bucket_map():
    i = np.arange(WINDOW)[:, None]
    j = np.arange(WINDOW)[None, :]
    n = (i - j) % WINDOW
    nf = np.maximum(n, 1).astype(np.float32)
    large = MAX_EXACT + (np.log(nf / np.float32(MAX_EXACT)) / np.float32(math.log(MAX_DISTANCE / MAX_EXACT))
                         * np.float32(N_BUCKETS - MAX_EXACT)).astype(np.int32)
    return np.where(n < MAX_EXACT, n, np.minimum(large, N_BUCKETS - 1)).astype(np.int32)


def _bias_table_kernel(rb_ref, bm_ref, tbl_ref):
    p = pl.program_id(0)
    bm = bm_ref[...]
    for half in range(2):
        h = 2 * p + half
        t = jnp.zeros(bm.shape, _F32)
        for b in range(N_BUCKETS):
            t = jnp.where(bm == b, rb_ref[b, h], t)
        tbl_ref[0, :, half * WINDOW:(half + 1) * WINDOW] = t


def _bias_tables(rel_bias):
    bm = jnp.asarray(_bucket_map())
    return pl.pallas_call(
        _bias_table_kernel,
        grid=(N_PAIRS,),
        in_specs=[pl.BlockSpec(memory_space=pltpu.SMEM), pl.BlockSpec(bm.shape, lambda p: (0, 0))],
        out_specs=pl.BlockSpec((1, WINDOW, 2 * WINDOW), lambda p: (p, 0, 0)),
        out_shape=jax.ShapeDtypeStruct((N_PAIRS, WINDOW, 2 * WINDOW), _F32),
        name="bias_tables",
    )(rel_bias, bm)


def _block_diag_pairs(slab):
    low = lax.broadcasted_iota(jnp.int32, slab.shape, 1) < HEAD_DIM
    swapped = pltpu.roll(slab, HEAD_DIM, axis=1)
    zero = jnp.zeros_like(slab)
    first = jnp.concatenate([jnp.where(low, slab, zero), jnp.where(low, zero, swapped)], axis=0)
    second = jnp.concatenate([jnp.where(low, swapped, zero), jnp.where(low, zero, slab)], axis=0)
    return first.astype(_BF16), second.astype(_BF16)


def _kv_operands(k_blk, v_blk):
    ops = []
    for slab in range(KV_DIM // LANES):
        cols = slice(slab * LANES, (slab + 1) * LANES)
        ops.extend(zip(_block_diag_pairs(k_blk[:, cols]), _block_diag_pairs(v_blk[:, cols])))
    return ops


def _attend(q, prev_ops, own_ops, tbl_ref, sink_ref, prev_shift, store, transposed=False):
    tq = q.shape[0]
    rows = 2 * tq
    row = lax.broadcasted_iota(jnp.int32, (rows, 2 * WINDOW), 0)
    col = lax.broadcasted_iota(jnp.int32, (rows, 2 * WINDOW), 1)
    from_prev = (col & (WINDOW - 1)) > jnp.where(row >= tq, row - tq, row)
    top = lax.broadcasted_iota(jnp.int32, (rows, 1), 0) < tq
    low = lax.broadcasted_iota(jnp.int32, (rows, LANES), 1) < HEAD_DIM
    contract_last = (((1,), (1,)), ((), ()))

    def logits(a, k_op):
        return _dot(a, k_op) if transposed else lax.dot_general(a, k_op, contract_last, preferred_element_type=_F32)

    def weighted_values(pr, v_op):
        return lax.dot_general(pr, v_op, contract_last, preferred_element_type=_F32) if transposed else _dot(pr, v_op)

    for kvh in range(N_KV_HEADS):
        (k_prev, v_prev), (k_own, v_own) = prev_ops[kvh], own_ops[kvh]
        pair_a = 2 * kvh
        pair_b = pair_a + 1
        qq = jnp.concatenate([q[:, pair_a * LANES:(pair_a + 1) * LANES],
                              q[:, pair_b * LANES:(pair_b + 1) * LANES]], axis=0).astype(_BF16)
        sp = logits(qq, k_prev)
        so = logits(qq, k_own)
        bias = jnp.concatenate([tbl_ref[pair_a, 0:tq, :], tbl_ref[pair_b, 0:tq, :]], axis=0)
        s = jnp.where(from_prev, sp + prev_shift, so) + bias
        sink_even = jnp.where(top, sink_ref[2 * pair_a], sink_ref[2 * pair_b])
        sink_odd = jnp.where(top, sink_ref[2 * pair_a + 1], sink_ref[2 * pair_b + 1])
        m_even = jnp.maximum(jnp.max(s[:, :WINDOW], axis=-1, keepdims=True), sink_even)
        m_odd = jnp.maximum(jnp.max(s[:, WINDOW:], axis=-1, keepdims=True), sink_odd)
        e_even = jnp.exp(s[:, :WINDOW] - m_even)
        e_odd = jnp.exp(s[:, WINDOW:] - m_odd)
        p = jnp.concatenate([e_even, e_odd], axis=1).astype(_BF16)
        zero = jnp.zeros_like(p)
        o = (weighted_values(jnp.where(from_prev, p, zero), v_prev)
             + weighted_values(jnp.where(from_prev, zero, p), v_own))
        den_even = jnp.sum(e_even, axis=-1, keepdims=True) + jnp.exp(sink_even - m_even)
        den_odd = jnp.sum(e_odd, axis=-1, keepdims=True) + jnp.exp(sink_odd - m_odd)
        o = o * jnp.where(low, 1.0 / den_even, 1.0 / den_odd)
        store(pair_a, o[:tq])
        store(pair_b, o[tq:])


PROMPT_QBLOCKS = 4


def _attn_prompt_kernel(sink_ref, q_ref, kp_ref, ko_ref, vp_ref, vo_ref, tbl_ref, o_ref):
    prev_shift = jnp.where(pl.program_id(1) == 0, MASK_VALUE, 0.0).astype(_F32)
    ops = [_kv_operands(kp_ref[...], vp_ref[...])]
    for b in range(PROMPT_QBLOCKS):
        rows = slice(b * WINDOW, (b + 1) * WINDOW)
        ops.append(_kv_operands(ko_ref[rows, :], vo_ref[rows, :]))

        def store(pair, o, rows=rows):
            o_ref[rows, pair * LANES:(pair + 1) * LANES] = o.astype(o_ref.dtype)

        _attend(q_ref[rows, :], ops[b], ops[b + 1], tbl_ref, sink_ref,
                prev_shift if b == 0 else jnp.float32(0.0), store)


def _attn_prompt(q, k, v, tbl, sinks, bsz, t):
    tq = PROMPT_QBLOCKS * WINDOW
    nb = t // tq
    own = lambda w: pl.BlockSpec((tq, w), lambda b, i: (b * nb + i, 0))
    prev = lambda w: pl.BlockSpec((WINDOW, w),
                                  lambda b, i: (PROMPT_QBLOCKS * (b * nb + i) - jnp.minimum(i, 1), 0))
    return pl.pallas_call(
        _attn_prompt_kernel,
        grid=(bsz, nb),
        in_specs=[pl.BlockSpec(memory_space=pltpu.SMEM), own(Q_DIM), prev(KV_DIM), own(KV_DIM),
                  prev(KV_DIM), own(KV_DIM), pl.BlockSpec(tbl.shape, lambda b, i: (0, 0, 0))],
        out_specs=own(Q_DIM),
        out_shape=jax.ShapeDtypeStruct((bsz * t, Q_DIM), _BF16),
        compiler_params=pltpu.CompilerParams(dimension_semantics=("arbitrary", "arbitrary"),
                                             vmem_limit_bytes=VMEM_LIMIT),
        name="attn_prompt",
    )(sinks, q, k, k, v, v, tbl)


def _block_diag_t(x):
    xb = x.astype(_BF16)
    z = jnp.zeros_like(xb)
    return jnp.concatenate([jnp.concatenate([xb, z], axis=1), jnp.concatenate([z, xb], axis=1)], axis=0)


def _attn_sample_kernel(sink_ref, q_ref, kn_ref, vn_ref, ck_ref, cv_ref, tbl_ref, o_ref, cko_ref, cvo_ref,
                        *, sb, steps):
    pad = jnp.zeros((WINDOW - steps, LANES), _F32)
    lane = lax.broadcasted_iota(jnp.int32, (HEAD_DIM, WINDOW), 1)

    def one_sequence(s):
        def store(pair, o):
            o_ref[s, :, pair * LANES:(pair + 1) * LANES] = o

        prev_ops, own_ops = [], []
        for slab in range(KV_DIM // LANES):
            cols = slice(slab * LANES, (slab + 1) * LANES)
            new_k = jnp.concatenate([kn_ref[s][:, cols], pad], axis=0).T
            new_v = jnp.concatenate([vn_ref[s][:, cols], pad], axis=0).T
            for sub in range(2):
                kvh = 2 * slab + sub
                part = slice(sub * HEAD_DIM, (sub + 1) * HEAD_DIM)
                kt, vt = ck_ref[s, kvh], cv_ref[s, kvh]
                cko_ref[s, kvh] = pltpu.roll(jnp.where(lane < steps, new_k[part], kt), WINDOW - steps, axis=1)
                cvo_ref[s, kvh] = pltpu.roll(jnp.where(lane < steps, new_v[part], vt), WINDOW - steps, axis=1)
                prev_ops.append((_block_diag_t(kt), _block_diag_t(vt)))
                own_ops.append((_block_diag_t(new_k[part]), _block_diag_t(new_v[part])))
        _attend(q_ref[s], prev_ops, own_ops, tbl_ref, sink_ref, jnp.float32(0.0), store, transposed=True)

    for s in range(sb):
        one_sequence(s)


def _attn_sample(q, k_new, v_new, cache_kt, cache_vt, tbl, sinks, sb):
    nseq, steps, _ = q.shape
    seq = lambda r, w: pl.BlockSpec((sb, r, w), lambda i: (i, 0, 0))
    cache = pl.BlockSpec((sb, N_KV_HEADS, HEAD_DIM, WINDOW), lambda i: (i, 0, 0, 0))
    return pl.pallas_call(
        functools.partial(_attn_sample_kernel, sb=sb, steps=steps),
        grid=(nseq // sb,),
        in_specs=[pl.BlockSpec(memory_space=pltpu.SMEM), seq(steps, Q_DIM), seq(steps, KV_DIM), seq(steps, KV_DIM),
                  cache, cache, pl.BlockSpec(tbl.shape, lambda i: (0, 0, 0))],
        out_specs=[seq(steps, Q_DIM), cache, cache],
        out_shape=[jax.ShapeDtypeStruct((nseq, steps, Q_DIM), _F32),
                   jax.ShapeDtypeStruct(cache_kt.shape, _F32), jax.ShapeDtypeStruct(cache_vt.shape, _F32)],
        compiler_params=pltpu.CompilerParams(dimension_semantics=("arbitrary",),
                                             vmem_limit_bytes=VMEM_LIMIT),
        name="attn_sample",
    )(sinks, q, k_new, v_new, cache_kt, cache_vt, tbl)


def _lane_min_index(mask, lane):
    return jnp.min(jnp.where(mask, lane, float(LANES)), axis=-1, keepdims=True)


def _finish_kernel(x_ref, conv_ref, o_ref, ga_ref, gb_ref, wa_ref, wo_ref, ng_ref, wr_ref, br_ref,
                   tri_ref, h_ref, hn_ref, route_ref, count_ref, running_ref):
    @pl.when(pl.program_id(0) == 0)
    def _():
        running_ref[...] = jnp.zeros_like(running_ref)

    attn_out = _dot(o_ref[...].astype(_BF16), wa_ref[...])
    merged = ga_ref[...].astype(_F32) * conv_ref[...].astype(_F32) + gb_ref[...].astype(_F32) * attn_out
    h = x_ref[...] + _dot(merged.astype(_BF16), wo_ref[...])
    h_ref[...] = h
    hn = h * lax.rsqrt(jnp.mean(h * h, axis=-1, keepdims=True) + EPS) * ng_ref[...]
    hn_ref[...] = _pack_bf16_pairs(hn)

    logits = _dot(hn.astype(_BF16), wr_ref[...]) + br_ref[...]
    lane = lax.broadcasted_iota(jnp.int32, logits.shape, 1).astype(_F32)
    gmask = lane < N_GROUPS
    gl = jnp.where(gmask, logits, MASK_VALUE)
    gmax = jnp.max(gl, axis=-1, keepdims=True)
    grp = _lane_min_index(gmask & (gl == gmax), lane)
    p_grp = 1.0 / jnp.sum(jnp.where(gmask, jnp.exp(gl - gmax), 0.0), axis=-1, keepdims=True)
    e_lo = N_GROUPS + grp * EXPERTS_PER_GROUP
    emask = (lane >= e_lo) & (lane < e_lo + EXPERTS_PER_GROUP)
    el = jnp.where(emask, logits, MASK_VALUE)
    ex = jnp.where(emask, jnp.exp(el - jnp.max(el, axis=-1, keepdims=True)), 0.0)
    prob = jnp.where(emask, ex / jnp.sum(ex, axis=-1, keepdims=True), -1.0)
    p1 = jnp.max(prob, axis=-1, keepdims=True)
    i1 = _lane_min_index(prob == p1, lane)
    rest = jnp.where(lane == i1, -1.0, prob)
    p2 = jnp.max(rest, axis=-1, keepdims=True)
    i2 = _lane_min_index(rest == p2, lane)
    w1 = p_grp * p1 / (p1 + p2)
    w2 = p_grp * p2 / (p1 + p2)
    e1 = i1 - N_GROUPS
    e2 = i2 - N_GROUPS

    hot1 = lane == e1
    hot2 = lane == e2
    hot = jnp.where(hot1 | hot2, 1.0, 0.0)
    before = _dot(tri_ref[...], hot.astype(_BF16)) + running_ref[...]
    rank1 = jnp.sum(jnp.where(hot1, before, 0.0), axis=-1, keepdims=True)
    rank2 = jnp.sum(jnp.where(hot2, before, 0.0), axis=-1, keepdims=True)
    running_ref[...] += jnp.sum(hot, axis=0, keepdims=True)
    count_ref[...] = jnp.broadcast_to(running_ref[...], count_ref.shape)

    fields = (e1, e2, w1, w2, rank1, rank2)
    route = jnp.zeros(logits.shape, _F32)
    for pos, val in enumerate(fields):
        route = jnp.where(lane == pos, val, route)
    route_ref[...] = route


ROUTE_E, ROUTE_W, ROUTE_RANK = 0, 2, 4
DEST_ROWS = 4096
COMBINE_ROWS = 1024


def _finish(x, conv_out, o, ga, gb, wa_b, wo_b, ng, wr_b, br, tm):
    n = x.shape[0]
    tri = jnp.asarray(np.tril(np.ones((tm, tm), np.float32), -1), _BF16)
    row = lambda w: pl.BlockSpec((tm, w), lambda i: (i, 0))
    full = _resident
    return pl.pallas_call(
        _finish_kernel,
        grid=(n // tm,),
        in_specs=[row(D_MODEL), row(D_MODEL), row(Q_DIM), row(D_MODEL), row(D_MODEL),
                  full(wa_b), full(wo_b), full(ng), full(wr_b), full(br), full(tri)],
        out_specs=[row(D_MODEL), row(HALF), row(LANES), pl.BlockSpec((SUBLANES, LANES), lambda i: (0, 0))],
        out_shape=[jax.ShapeDtypeStruct((n, D_MODEL), _F32),
                   jax.ShapeDtypeStruct((n, HALF), jnp.uint32),
                   jax.ShapeDtypeStruct((n, LANES), _F32),
                   jax.ShapeDtypeStruct((SUBLANES, LANES), _F32)],
        scratch_shapes=[pltpu.VMEM((1, LANES), _F32)],
        compiler_params=pltpu.CompilerParams(dimension_semantics=("arbitrary",),
                                             vmem_limit_bytes=VMEM_LIMIT),
        name="finish",
    )(x, conv_out, o, ga, gb, wa_b, wo_b, ng, wr_b, br, tri)


def _dest_kernel(route_ref, starts_ref, dest_ref):
    route = route_ref[...]
    lane = lax.broadcasted_iota(jnp.int32, route.shape, 1)
    out = jnp.zeros(route.shape, jnp.int32)
    for j in range(TOP_K):
        e = route[:, ROUTE_E + j:ROUTE_E + j + 1].astype(jnp.int32)
        start = jnp.sum(jnp.where(lane == e, starts_ref[...], 0.0), axis=-1, keepdims=True)
        d = (start + route[:, ROUTE_RANK + j:ROUTE_RANK + j + 1]).astype(jnp.int32)
        out = jnp.where(lane == j, d, out)
    dest_ref[...] = out


def _dest(route, starts_row, tm):
    n = route.shape[0]
    row = pl.BlockSpec((tm, LANES), lambda i: (i, 0))
    return pl.pallas_call(
        _dest_kernel,
        grid=(n // tm,),
        in_specs=[row, pl.BlockSpec((1, LANES), lambda i: (0, 0))],
        out_specs=row,
        out_shape=jax.ShapeDtypeStruct((n, LANES), jnp.int32),
        name="dest",
    )(route, starts_row)


SC_CORES = 2
SC_SUBCORES = 16
SC_WORKERS = SC_CORES * SC_SUBCORES
SC_IN_FLIGHT = 4
SC_CHUNK_BYTES = 64 * 1024


def _sc_move_rows(src, idx, gather):
    n, d = src.shape
    b = idx.shape[0]
    per_worker = b // SC_WORKERS
    assert per_worker * SC_WORKERS == b and (gather or n % per_worker == 0), (b, n)
    chunk = min(per_worker // SC_IN_FLIGHT, SC_CHUNK_BYTES // (d * 4))
    n_iters = per_worker // (chunk * SC_IN_FLIGHT)
    assert n_iters * chunk * SC_IN_FLIGHT == per_worker and chunk % SUBLANES == 0, (per_worker, chunk)
    mesh = plsc.VectorSubcoreMesh(core_axis_name="c", subcore_axis_name="s",
                                  num_cores=SC_CORES, num_subcores=SC_SUBCORES)
    scratch = ([pltpu.VMEM((chunk,), jnp.int32)] * SC_IN_FLIGHT + [pltpu.VMEM((chunk, d), src.dtype)] * SC_IN_FLIGHT
               + [pltpu.SemaphoreType.DMA] * SC_IN_FLIGHT)

    @functools.partial(pl.kernel, mesh=mesh, out_type=jax.ShapeDtypeStruct((b, d), src.dtype),
                       scratch_types=scratch, name="sc_gather_rows" if gather else "sc_scatter_rows",
                       cost_estimate=pl.CostEstimate(flops=0, transcendentals=0, bytes_accessed=2 * b * d * 4 + b * 4))
    def move(src_hbm, idx_hbm, out_hbm, *bufs):
        idx_v = bufs[:SC_IN_FLIGHT]
        rows_v = bufs[SC_IN_FLIGHT:2 * SC_IN_FLIGHT]
        sems = bufs[2 * SC_IN_FLIGHT:]
        worker = lax.axis_index("s") * SC_CORES + lax.axis_index("c")

        @pl.loop(0, n_iters)
        def _(it):
            bases = [pl.multiple_of(worker * per_worker + (it * SC_IN_FLIGHT + j) * chunk, chunk)
                     for j in range(SC_IN_FLIGHT)]
            loads = [pltpu.async_copy(idx_hbm.at[pl.ds(bases[j], chunk)], idx_v[j], sems[j])
                     for j in range(SC_IN_FLIGHT)]
            reads = []
            for j in range(SC_IN_FLIGHT):
                loads[j].wait()
                if gather:
                    rows = src_hbm.at[idx_v[j]]
                else:
                    rows = src_hbm.at[pl.ds(pl.multiple_of(lax.rem(bases[j], n), chunk), chunk)]
                reads.append(pltpu.async_copy(rows, rows_v[j], sems[j]))
            writes = []
            for j in range(SC_IN_FLIGHT):
                reads[j].wait()
                dst = out_hbm.at[pl.ds(bases[j], chunk)] if gather else out_hbm.at[idx_v[j]]
                writes.append(pltpu.async_copy(rows_v[j], dst, sems[j]))
            for w in writes:
                w.wait()

    return move(src, idx)


def _sc_gather_rows(table, idx):
    return _sc_move_rows(table, idx, gather=True)


def _sc_scatter_rows(src, idx):
    return _sc_move_rows(src, idx, gather=False)


def _expert_kernel(blk_ref, exp_ref, lo_ref, hi_ref, x_ref, wg_ref, wu_ref, wd_ref, yb_ref):
    del blk_ref, exp_ref
    k = pl.program_id(0)
    lo, hi = lo_ref[k], hi_ref[k]

    @pl.when(hi > lo)
    def _():
        x = _unpack_bf16_pairs(x_ref[...])
        g = _dot(x, wg_ref[0])
        u = _dot(x, wu_ref[0])
        hid = g * jax.nn.sigmoid(g) * u
        y = _dot(hid, wd_ref[0])
        r = lax.broadcasted_iota(jnp.int32, yb_ref.shape, 0)
        pltpu.store(yb_ref, _pack_bf16_pairs(y), mask=(r >= lo) & (r < hi))


def _experts(items, xs, w_gate, w_up, w_down):
    n_items = items[0].shape[0]
    wspec = lambda a: pl.BlockSpec((1,) + a.shape[1:], lambda k, blk, exp, lo, hi: (exp[k], 0, 0))
    rows = pl.BlockSpec((MOE_ROWS, HALF), lambda k, blk, exp, lo, hi: (blk[k], 0))
    grid_spec = pltpu.PrefetchScalarGridSpec(
        num_scalar_prefetch=4,
        grid=(n_items,),
        in_specs=[rows, wspec(w_gate), wspec(w_up), wspec(w_down)],
        out_specs=rows,
    )
    return pl.pallas_call(
        _expert_kernel,
        grid_spec=grid_spec,
        out_shape=jax.ShapeDtypeStruct(xs.shape, xs.dtype),
        compiler_params=pltpu.CompilerParams(dimension_semantics=("arbitrary",),
                                             vmem_limit_bytes=VMEM_LIMIT),
        cost_estimate=pl.CostEstimate(
            flops=n_items * MOE_ROWS * 6 * D_MODEL * D_EXPERT, transcendentals=n_items * MOE_ROWS * D_EXPERT,
            bytes_accessed=2 * xs.size * 4 + (w_gate.size + w_up.size + w_down.size) * 4),
        name="experts",
    )(*items, xs, w_gate, w_up, w_down)


def _combine_kernel(h_ref, route_ref, g0_ref, g1_ref, y_ref):
    route = route_ref[...]
    y_ref[...] = (h_ref[...] + route[:, ROUTE_W:ROUTE_W + 1] * _unpack_bf16_pairs(g0_ref[...])
                  + route[:, ROUTE_W + 1:ROUTE_W + 2] * _unpack_bf16_pairs(g1_ref[...]))


def _combine(h, route, g, tm):
    n = h.shape[0]
    nt = n // tm
    row = lambda w: pl.BlockSpec((tm, w), lambda i: (i, 0))
    return pl.pallas_call(
        _combine_kernel,
        grid=(nt,),
        in_specs=[row(D_MODEL), row(LANES), row(HALF), pl.BlockSpec((tm, HALF), lambda i: (nt + i, 0))],
        out_specs=row(D_MODEL),
        out_shape=jax.ShapeDtypeStruct((n, D_MODEL), _F32),
        compiler_params=pltpu.CompilerParams(dimension_semantics=("arbitrary",),
                                             vmem_limit_bytes=VMEM_LIMIT),
        name="combine",
    )(h, route, g, g)


def _work_items(counts, n_pairs):
    n_blocks = n_pairs // MOE_ROWS
    starts = jnp.cumsum(counts) - counts
    cuts = jnp.sort(jnp.concatenate([jnp.arange(n_blocks, dtype=jnp.int32) * MOE_ROWS, starts]))
    ends = jnp.concatenate([cuts[1:], jnp.full((1,), n_pairs, jnp.int32)])
    blk = jnp.minimum(cuts // MOE_ROWS, n_blocks - 1)
    expert = jnp.clip(jnp.sum(starts[None, :] <= cuts[:, None], axis=1) - 1, 0, N_EXPERTS - 1).astype(jnp.int32)
    return starts, (blk, expert, cuts - blk * MOE_ROWS, ends - blk * MOE_ROWS)


def _dispatch(hn, route, counts_rows):
    n = hn.shape[0]
    n_pairs = n * TOP_K
    counts = counts_rows[0, :N_EXPERTS].astype(jnp.int32)
    starts, items = _work_items(counts, n_pairs)
    starts_row = jnp.zeros((1, LANES), _F32).at[0, :N_EXPERTS].set(starts.astype(_F32))
    dest = _dest(route, starts_row, min(n, DEST_ROWS))[:, :TOP_K].T.reshape(n_pairs)
    return _sc_scatter_rows(hn, dest), dest, items


def _after(value, *earlier):
    return lax.optimization_barrier((value,) + earlier)[0]


def kernel(x_prompt, x_sample, state_conv, cache_k, cache_v, norm_attn_g, w_in, q_norm_g, k_norm_g, rel_bias, attn_sinks, w_dw, b_dw, conv_ln_g, conv_ln_b, w_conv_out, b_conv_out, w_attn_out, w_out, norm_ffn_g, w_grp, b_grp, w_router, b_router, w_gate, w_up, w_down):
    bsz, t, _ = x_prompt.shape
    nseq, steps, _ = x_sample.shape
    row = lambda a: a.reshape(1, -1).astype(_F32)

    w_in_b = w_in.astype(_BF16)
    wco_b = w_conv_out.astype(_BF16)
    wa_b = w_attn_out.astype(_BF16)
    wo_b = w_out.astype(_BF16)
    qg = row(jnp.tile(q_norm_g, N_HEADS)) * (HEAD_DIM ** -0.5)
    kg = row(jnp.tile(k_norm_g, N_KV_HEADS))
    w_rt = jnp.zeros((D_MODEL, LANES), _F32).at[:, :N_GROUPS].set(w_grp).at[:, N_GROUPS:N_GROUPS + N_EXPERTS].set(w_router)
    wr_b = w_rt.astype(_BF16)
    b_rt = jnp.zeros((1, LANES), _F32).at[0, :N_GROUPS].set(b_grp).at[0, N_GROUPS:N_GROUPS + N_EXPERTS].set(b_router)
    tbl = _bias_tables(rel_bias)
    conv_params = (w_dw, row(b_dw), row(conv_ln_g), row(conv_ln_b), wco_b, row(b_conv_out))

    def finish(x2d, conv_out, o, ga, gb):
        return _finish(x2d, conv_out, o, ga, gb, wa_b, wo_b, row(norm_ffn_g), wr_b, b_rt, ROW_TILE)

    xp = x_prompt.reshape(bsz * t, D_MODEL)
    q, k, v, ga, gb, conv_out, glu_tail = _inproj_conv(x_prompt, row(norm_attn_g), w_in_b, qg, kg, *conv_params,
                                                       ROW_TILE // 2)
    o = _attn_prompt(q, k, v, tbl, attn_sinks, bsz, t)
    h_p, hn_p, route_p, counts_p = finish(xp, conv_out, o, ga, gb)
    rows_p, dest_p, items_p = _dispatch(hn_p, route_p, counts_p)
    state_conv_prompt = glu_tail[:, HALO - (CONV_WIDTH - 1):]
    tail = lambda a: a.reshape(bsz, t, KV_DIM)[:, t - WINDOW:].reshape(bsz, WINDOW, N_KV_HEADS, HEAD_DIM)
    cache_k_prompt, cache_v_prompt = tail(k), tail(v)

    xs = _after(x_sample, dest_p).reshape(nseq * steps, D_MODEL)
    glu, q, k, v, ga, gb = _inproj(xs, row(norm_attn_g), w_in_b, qg, kg, _F32, ROW_TILE)
    glu_t = glu.reshape(nseq, steps, D_CONV).transpose(1, 0, 2)
    conv_out, state_t = _conv_sample(state_conv.transpose(1, 0, 2), glu_t, *conv_params, SAMPLE_CONV_SEQS)
    conv_out = conv_out.transpose(1, 0, 2).reshape(nseq * steps, D_MODEL)
    k3 = k.reshape(nseq, steps, KV_DIM)
    v3 = v.reshape(nseq, steps, KV_DIM)
    o, ck_t, cv_t = _attn_sample(q.reshape(nseq, steps, Q_DIM), k3, v3, cache_k.transpose(0, 2, 3, 1),
                                 cache_v.transpose(0, 2, 3, 1), tbl, attn_sinks, SAMPLE_ATTN_SEQS)
    h_s, hn_s, route_s, counts_s = finish(xs, conv_out, o.reshape(nseq * steps, Q_DIM), ga, gb)
    rows_s, dest_s, items_s = _dispatch(hn_s, route_s, counts_s)
    state_conv_sample = state_t.transpose(1, 0, 2)
    cache_k_sample = ck_t.transpose(0, 3, 1, 2)
    cache_v_sample = cv_t.transpose(0, 3, 1, 2)

    yb_p = _experts(items_p, rows_p, w_gate, w_up, w_down)
    g_p = _sc_gather_rows(yb_p, dest_p)
    yb_s = _experts(items_s, _after(rows_s, yb_p), w_gate, w_up, w_down)
    g_s = _sc_gather_rows(yb_s, dest_s)
    y_prompt = _combine(h_p, route_p, g_p, COMBINE_ROWS).reshape(bsz, t, D_MODEL)
    y_sample = _combine(h_s, route_s, _after(g_s, y_prompt), COMBINE_ROWS).reshape(nseq, steps, D_MODEL)

    return (y_prompt, y_sample, state_conv_prompt, cache_k_prompt, cache_v_prompt,
            state_conv_sample, cache_k_sample, cache_v_sample)
```
